```python
import math
import jax, jax.numpy as jnp
from jax import lax
import numpy as np

D_MODEL = 2048
BATCH = 8
SEQ = 4096
DEPTH = 1

D_MIX = D_MODEL
D_SSM = D_MIX // 2
SSM_GROUP = 16
SSM_GROUPS = D_SSM // SSM_GROUP
SSM_STATE = 64
D_SGU = D_MIX - D_SSM
SGU_CHUNK = 128
SGU_HEADS = 8
SGU_HEAD_DIM = D_SGU // SGU_HEADS
D_FFN = -(-8 * D_MODEL // (3 * 256)) * 256
PLE_DIM = 256
EPS = 1e-6
DT_MIN = 1e-3
DT_MAX = 1e-1
LAMBDA_RE_MAX = -1e-4

kernel_name = "hybrid_s5_sgu_parallel_heads"


def rmsnorm(x, g):
    xf = x.astype(jnp.float32)
    r = lax.rsqrt(jnp.mean(xf * xf, axis=-1, keepdims=True) + EPS)
    return (xf * r).astype(x.dtype) * g


def layernorm(x, g, b):
    xf = x.astype(jnp.float32)
    mu = jnp.mean(xf, axis=-1, keepdims=True)
    xc = xf - mu
    r = lax.rsqrt(jnp.mean(xc * xc, axis=-1, keepdims=True) + EPS)
    return (xc * r).astype(x.dtype) * g + b


def _complex_linear_combine(e1, e2):
    a1r, a1i, b1r, b1i = e1
    a2r, a2i, b2r, b2i = e2
    ar = a2r * a1r - a2i * a1i
    ai = a2r * a1i + a2i * a1r
    br = a2r * b1r - a2i * b1i + b2r
    bi = a2r * b1i + a2i * b1r + b2i
    return (ar, ai, br, bi)


def s5_mixer(u, lam_re, lam_im, log_step, b_re, b_im, c_re, c_im, d, glu_w, glu_b):
    bsz, L, _ = u.shape
    f32 = jnp.float32
    ug = u.reshape(bsz, L, SSM_GROUPS, SSM_GROUP).astype(f32)
    lr = jnp.minimum(lam_re.astype(f32), LAMBDA_RE_MAX)
    li = lam_im.astype(f32)
    dt = jnp.exp(log_step.astype(f32))[:, None]
    mag = jnp.exp(lr * dt)
    ang = li * dt
    abar_re = mag * jnp.cos(ang)
    abar_im = mag * jnp.sin(ang)
    nr = abar_re - 1.0
    ni = abar_im
    den = lr * lr + li * li
    q_re = (nr * lr + ni * li) / den
    q_im = (ni * lr - nr * li) / den
    b_re32 = b_re.astype(f32)
    b_im32 = b_im.astype(f32)
    bbar_re = q_re[..., None] * b_re32 - q_im[..., None] * b_im32
    bbar_im = q_re[..., None] * b_im32 + q_im[..., None] * b_re32
    bu_re = jnp.einsum('blgh,gph->blgp', ug, bbar_re)
    bu_im = jnp.einsum('blgh,gph->blgp', ug, bbar_im)
    a_re = jnp.broadcast_to(abar_re, bu_re.shape)
    a_im = jnp.broadcast_to(abar_im, bu_im.shape)
    _, _, s_re, s_im = lax.associative_scan(
        _complex_linear_combine, (a_re, a_im, bu_re, bu_im), axis=1)
    y = (jnp.einsum('blgp,ghp->blgh', s_re, c_re.astype(f32))
         - jnp.einsum('blgp,ghp->blgh', s_im, c_im.astype(f32))
         + d.astype(f32) * ug)
    y = y.reshape(bsz, L, D_SSM).astype(u.dtype)
    y = jax.nn.gelu(y)
    return y * jax.nn.sigmoid(y @ glu_w + glu_b)


def sgu_mixer(u, v, ln_g, ln_b, w_s, b_s):
    bsz, L, _ = u.shape
    u = jax.nn.gelu(u)
    v = layernorm(jax.nn.gelu(v), ln_g, ln_b)
    vc = v.reshape(bsz, L // SGU_CHUNK, SGU_CHUNK, SGU_HEADS, SGU_HEAD_DIM)
    mask = jnp.tril(jnp.ones((SGU_CHUNK, SGU_CHUNK), dtype=bool))
    w = jnp.where(mask[None], w_s, jnp.zeros_like(w_s))
    s = jnp.einsum('hts,bnshc->bnthc', w, vc) + b_s.T[None, None, :, :, None]
    return u * s.reshape(bsz, L, D_SGU)


def _fwd_setup_inputs(seed: int = 0) -> dict:
    key = jax.random.key(seed)
    ks = jax.random.split(key, 32)
    f32 = jnp.float32
    nrm = lambda k, shape, scale: jax.random.normal(k, shape, f32) * scale
    gain = lambda k, shape: 1.0 + 0.01 * jax.random.normal(k, shape, f32)
    x = jax.random.normal(ks[0], (BATCH, SEQ, D_MODEL), f32)
    p = jax.random.normal(ks[1], (DEPTH, BATCH, SEQ, PLE_DIM), f32)
    norm_mix_g = gain(ks[2], (DEPTH, D_MODEL))
    w_in = nrm(ks[3], (DEPTH, D_MODEL, D_SSM + 2 * D_SGU), D_MODEL ** -0.5)
    ssm_lambda_re = -0.5 + 0.01 * jax.random.normal(ks[4], (DEPTH, SSM_GROUPS, SSM_STATE), f32)
    ssm_lambda_im = (jnp.pi * jnp.arange(SSM_STATE, dtype=f32))[None, None, :] \
        + 0.01 * jax.random.normal(ks[5], (DEPTH, SSM_GROUPS, SSM_STATE), f32)
    ssm_log_step = math.log(DT_MIN) + jax.random.uniform(ks[6], (DEPTH, SSM_GROUPS), f32) \
        * (math.log(DT_MAX) - math.log(DT_MIN))
    bs = (2.0 * SSM_GROUP) ** -0.5
    cs = (2.0 * SSM_STATE) ** -0.5
    ssm_b_re = nrm(ks[7], (DEPTH, SSM_GROUPS, SSM_STATE, SSM_GROUP), bs)
    ssm_b_im = nrm(ks[8], (DEPTH, SSM_GROUPS, SSM_STATE, SSM_GROUP), bs)
    ssm_c_re = nrm(ks[9], (DEPTH, SSM_GROUPS, SSM_GROUP, SSM_STATE), cs)
    ssm_c_im = nrm(ks[10], (DEPTH, SSM_GROUPS, SSM_GROUP, SSM_STATE), cs)
    ssm_d = nrm(ks[11], (DEPTH, SSM_GROUPS, SSM_GROUP), 0.5)
    ssm_glu_w = nrm(ks[12], (DEPTH, D_SSM, D_SSM), D_SSM ** -0.5)
    ssm_glu_b = nrm(ks[13], (DEPTH, D_SSM), 0.01)
    sgu_ln_g = gain(ks[14], (DEPTH, D_SGU))
    sgu_ln_b = nrm(ks[15], (DEPTH, D_SGU), 0.01)
    sgu_w = nrm(ks[16], (DEPTH, SGU_HEADS, SGU_CHUNK, SGU_CHUNK), SGU_CHUNK ** -0.5)
    sgu_b = gain(ks[17], (DEPTH, SGU_HEADS, SGU_CHUNK))
    out_norm_ssm_g = gain(ks[18], (DEPTH, D_SSM))
    out_norm_sgu_g = gain(ks[19], (DEPTH, D_SGU))
    w_out = nrm(ks[20], (DEPTH, D_MIX, D_MODEL), D_MIX ** -0.5)
    norm_ffn_g = gain(ks[21], (DEPTH, D_MODEL))
    w_ffn_in = nrm(ks[22], (DEPTH, D_MODEL, 2 * D_FFN), D_MODEL ** -0.5)
    w_ffn_out = nrm(ks[23], (DEPTH, D_FFN, D_MODEL), D_FFN ** -0.5)
    norm_ple_g = gain(ks[24], (DEPTH, D_MODEL))
    w_ple_gate = nrm(ks[25], (DEPTH, D_MODEL, D_MODEL), D_MODEL ** -0.5)
    b_ple_gate = nrm(ks[26], (DEPTH, D_MODEL), 0.01)
    w_ple_proj = nrm(ks[27], (DEPTH, PLE_DIM, D_MODEL), PLE_DIM ** -0.5)
    final_norm_g = gain(ks[28], (D_MODEL,))
    return {
        "x": x, "p": p, "norm_mix_g": norm_mix_g, "w_in": w_in,
        "ssm_lambda_re": ssm_lambda_re, "ssm_lambda_im": ssm_lambda_im,
        "ssm_log_step": ssm_log_step, "ssm_b_re": ssm_b_re, "ssm_b_im": ssm_b_im,
        "ssm_c_re": ssm_c_re, "ssm_c_im": ssm_c_im, "ssm_d": ssm_d,
        "ssm_glu_w": ssm_glu_w, "ssm_glu_b": ssm_glu_b,
        "sgu_ln_g": sgu_ln_g, "sgu_ln_b": sgu_ln_b, "sgu_w": sgu_w, "sgu_b": sgu_b,
        "out_norm_ssm_g": out_norm_ssm_g, "out_norm_sgu_g": out_norm_sgu_g,
        "w_out": w_out, "norm_ffn_g": norm_ffn_g, "w_ffn_in": w_ffn_in,
        "w_ffn_out": w_ffn_out, "norm_ple_g": norm_ple_g, "w_ple_gate": w_ple_gate,
        "b_ple_gate": b_ple_gate, "w_ple_proj": w_ple_proj, "final_norm_g": final_norm_g,
    }


def _fwd_reference(x, p, norm_mix_g, w_in, ssm_lambda_re, ssm_lambda_im, ssm_log_step,
              ssm_b_re, ssm_b_im, ssm_c_re, ssm_c_im, ssm_d, ssm_glu_w, ssm_glu_b,
              sgu_ln_g, sgu_ln_b, sgu_w, sgu_b, out_norm_ssm_g, out_norm_sgu_g,
              w_out, norm_ffn_g, w_ffn_in, w_ffn_out, norm_ple_g, w_ple_gate,
              b_ple_gate, w_ple_proj, final_norm_g):
    for i in range(DEPTH):
        h = rmsnorm(x, norm_mix_g[i])
        z = h @ w_in[i]
        z_ssm = z[..., :D_SSM]
        z_u = z[..., D_SSM:D_SSM + D_SGU]
        z_v = z[..., D_SSM + D_SGU:]
        y_a = s5_mixer(z_ssm, ssm_lambda_re[i], ssm_lambda_im[i], ssm_log_step[i],
                       ssm_b_re[i], ssm_b_im[i], ssm_c_re[i], ssm_c_im[i], ssm_d[i],
                       ssm_glu_w[i], ssm_glu_b[i])
        y_b = sgu_mixer(z_u, z_v, sgu_ln_g[i], sgu_ln_b[i], sgu_w[i], sgu_b[i])
        y = jnp.concatenate([rmsnorm(y_a, out_norm_ssm_g[i]),
                             rmsnorm(y_b, out_norm_sgu_g[i])], axis=-1)
        x = x + y @ w_out[i]
        h = rmsnorm(x, norm_ffn_g[i])
        gu = h @ w_ffn_in[i]
        x = x + (jax.nn.silu(gu[..., :D_FFN]) * gu[..., D_FFN:]) @ w_ffn_out[i]
        h = rmsnorm(x, norm_ple_g[i])
        gate = jax.nn.sigmoid(h @ w_ple_gate[i] + b_ple_gate[i])
        x = x + gate * (p[i] @ w_ple_proj[i])
    return rmsnorm(x, final_norm_g)


import jax as _jax
import jax.numpy as _jnp

TWIN_FORMAT = 'train_step'
FWD_PARAMS = ['x', 'p', 'norm_mix_g', 'w_in', 'ssm_lambda_re', 'ssm_lambda_im', 'ssm_log_step', 'ssm_b_re', 'ssm_b_im', 'ssm_c_re', 'ssm_c_im', 'ssm_d', 'ssm_glu_w', 'ssm_glu_b', 'sgu_ln_g', 'sgu_ln_b', 'sgu_w', 'sgu_b', 'out_norm_ssm_g', 'out_norm_sgu_g', 'w_out', 'norm_ffn_g', 'w_ffn_in', 'w_ffn_out', 'norm_ple_g', 'w_ple_gate', 'b_ple_gate', 'w_ple_proj', 'final_norm_g']
TWIN_WEIGHTS = ['norm_mix_g', 'w_in', 'ssm_lambda_re', 'ssm_lambda_im', 'ssm_log_step', 'ssm_b_re', 'ssm_b_im', 'ssm_c_re', 'ssm_c_im', 'ssm_d', 'ssm_glu_w', 'ssm_glu_b', 'sgu_ln_g', 'sgu_ln_b', 'sgu_w', 'sgu_b', 'out_norm_ssm_g', 'out_norm_sgu_g', 'w_out', 'norm_ffn_g', 'w_ffn_in', 'w_ffn_out', 'norm_ple_g', 'w_ple_gate', 'b_ple_gate', 'w_ple_proj', 'final_norm_g']
TWIN_DIFF_INPUT = 'x'
TWIN_INPUTS = ['x', 'p', 'norm_mix_g', 'w_in', 'ssm_lambda_re', 'ssm_lambda_im', 'ssm_log_step', 'ssm_b_re', 'ssm_b_im', 'ssm_c_re', 'ssm_c_im', 'ssm_d', 'ssm_glu_w', 'ssm_glu_b', 'sgu_ln_g', 'sgu_ln_b', 'sgu_w', 'sgu_b', 'out_norm_ssm_g', 'out_norm_sgu_g', 'w_out', 'norm_ffn_g', 'w_ffn_in', 'w_ffn_out', 'norm_ple_g', 'w_ple_gate', 'b_ple_gate', 'w_ple_proj', 'final_norm_g', 'loss_target', 'm_norm_mix_g', 'm_w_in', 'm_ssm_lambda_re', 'm_ssm_lambda_im', 'm_ssm_log_step', 'm_ssm_b_re', 'm_ssm_b_im', 'm_ssm_c_re', 'm_ssm_c_im', 'm_ssm_d', 'm_ssm_glu_w', 'm_ssm_glu_b', 'm_sgu_ln_g', 'm_sgu_ln_b', 'm_sgu_w', 'm_sgu_b', 'm_out_norm_ssm_g', 'm_out_norm_sgu_g', 'm_w_out', 'm_norm_ffn_g', 'm_w_ffn_in', 'm_w_ffn_out', 'm_norm_ple_g', 'm_w_ple_gate', 'm_b_ple_gate', 'm_w_ple_proj', 'm_final_norm_g', 'v_norm_mix_g', 'v_w_in', 'v_ssm_lambda_re', 'v_ssm_lambda_im', 'v_ssm_log_step', 'v_ssm_b_re', 'v_ssm_b_im', 'v_ssm_c_re', 'v_ssm_c_im', 'v_ssm_d', 'v_ssm_glu_w', 'v_ssm_glu_b', 'v_sgu_ln_g', 'v_sgu_ln_b', 'v_sgu_w', 'v_sgu_b', 'v_out_norm_ssm_g', 'v_out_norm_sgu_g', 'v_w_out', 'v_norm_ffn_g', 'v_w_ffn_in', 'v_w_ffn_out', 'v_norm_ple_g', 'v_w_ple_gate', 'v_b_ple_gate', 'v_w_ple_proj', 'v_final_norm_g']
TWIN_OUTPUTS = ['loss', 'grad_x', 'grad_norm_mix_g', 'grad_w_in', 'grad_ssm_lambda_re', 'grad_ssm_lambda_im', 'grad_ssm_log_step', 'grad_ssm_b_re', 'grad_ssm_b_im', 'grad_ssm_c_re', 'grad_ssm_c_im', 'grad_ssm_d', 'grad_ssm_glu_w', 'grad_ssm_glu_b', 'grad_sgu_ln_g', 'grad_sgu_ln_b', 'grad_sgu_w', 'grad_sgu_b', 'grad_out_norm_ssm_g', 'grad_out_norm_sgu_g', 'grad_w_out', 'grad_norm_ffn_g', 'grad_w_ffn_in', 'grad_w_ffn_out', 'grad_norm_ple_g', 'grad_w_ple_gate', 'grad_b_ple_gate', 'grad_w_ple_proj', 'grad_final_norm_g', 'delta_norm_mix_g', 'delta_w_in', 'delta_ssm_lambda_re', 'delta_ssm_lambda_im', 'delta_ssm_log_step', 'delta_ssm_b_re', 'delta_ssm_b_im', 'delta_ssm_c_re', 'delta_ssm_c_im', 'delta_ssm_d', 'delta_ssm_glu_w', 'delta_ssm_glu_b', 'delta_sgu_ln_g', 'delta_sgu_ln_b', 'delta_sgu_w', 'delta_sgu_b', 'delta_out_norm_ssm_g', 'delta_out_norm_sgu_g', 'delta_w_out', 'delta_norm_ffn_g', 'delta_w_ffn_in', 'delta_w_ffn_out', 'delta_norm_ple_g', 'delta_w_ple_gate', 'delta_b_ple_gate', 'delta_w_ple_proj', 'delta_final_norm_g', 'new_m_norm_mix_g', 'new_m_w_in', 'new_m_ssm_lambda_re', 'new_m_ssm_lambda_im', 'new_m_ssm_log_step', 'new_m_ssm_b_re', 'new_m_ssm_b_im', 'new_m_ssm_c_re', 'new_m_ssm_c_im', 'new_m_ssm_d', 'new_m_ssm_glu_w', 'new_m_ssm_glu_b', 'new_m_sgu_ln_g', 'new_m_sgu_ln_b', 'new_m_sgu_w', 'new_m_sgu_b', 'new_m_out_norm_ssm_g', 'new_m_out_norm_sgu_g', 'new_m_w_out', 'new_m_norm_ffn_g', 'new_m_w_ffn_in', 'new_m_w_ffn_out', 'new_m_norm_ple_g', 'new_m_w_ple_gate', 'new_m_b_ple_gate', 'new_m_w_ple_proj', 'new_m_final_norm_g', 'new_v_norm_mix_g', 'new_v_w_in', 'new_v_ssm_lambda_re', 'new_v_ssm_lambda_im', 'new_v_ssm_log_step', 'new_v_ssm_b_re', 'new_v_ssm_b_im', 'new_v_ssm_c_re', 'new_v_ssm_c_im', 'new_v_ssm_d', 'new_v_ssm_glu_w', 'new_v_ssm_glu_b', 'new_v_sgu_ln_g', 'new_v_sgu_ln_b', 'new_v_sgu_w', 'new_v_sgu_b', 'new_v_out_norm_ssm_g', 'new_v_out_norm_sgu_g', 'new_v_w_out', 'new_v_norm_ffn_g', 'new_v_w_ffn_in', 'new_v_w_ffn_out', 'new_v_norm_ple_g', 'new_v_w_ple_gate', 'new_v_b_ple_gate', 'new_v_w_ple_proj', 'new_v_final_norm_g']
TWIN_LEAF_KINDS = {'loss': 'loss', 'grad_x': 'grad_x', 'grad_norm_mix_g': 'grad_w', 'grad_w_in': 'grad_w', 'grad_ssm_lambda_re': 'grad_w', 'grad_ssm_lambda_im': 'grad_w', 'grad_ssm_log_step': 'grad_w', 'grad_ssm_b_re': 'grad_w', 'grad_ssm_b_im': 'grad_w', 'grad_ssm_c_re': 'grad_w', 'grad_ssm_c_im': 'grad_w', 'grad_ssm_d': 'grad_w', 'grad_ssm_glu_w': 'grad_w', 'grad_ssm_glu_b': 'grad_w', 'grad_sgu_ln_g': 'grad_w', 'grad_sgu_ln_b': 'grad_w', 'grad_sgu_w': 'grad_w', 'grad_sgu_b': 'grad_w', 'grad_out_norm_ssm_g': 'grad_w', 'grad_out_norm_sgu_g': 'grad_w', 'grad_w_out': 'grad_w', 'grad_norm_ffn_g': 'grad_w', 'grad_w_ffn_in': 'grad_w', 'grad_w_ffn_out': 'grad_w', 'grad_norm_ple_g': 'grad_w', 'grad_w_ple_gate': 'grad_w', 'grad_b_ple_gate': 'grad_w', 'grad_w_ple_proj': 'grad_w', 'grad_final_norm_g': 'grad_w', 'delta_norm_mix_g': 'delta_w', 'delta_w_in': 'delta_w', 'delta_ssm_lambda_re': 'delta_w', 'delta_ssm_lambda_im': 'delta_w', 'delta_ssm_log_step': 'delta_w', 'delta_ssm_b_re': 'delta_w', 'delta_ssm_b_im': 'delta_w', 'delta_ssm_c_re': 'delta_w', 'delta_ssm_c_im': 'delta_w', 'delta_ssm_d': 'delta_w', 'delta_ssm_glu_w': 'delta_w', 'delta_ssm_glu_b': 'delta_w', 'delta_sgu_ln_g': 'delta_w', 'delta_sgu_ln_b': 'delta_w', 'delta_sgu_w': 'delta_w', 'delta_sgu_b': 'delta_w', 'delta_out_norm_ssm_g': 'delta_w', 'delta_out_norm_sgu_g': 'delta_w', 'delta_w_out': 'delta_w', 'delta_norm_ffn_g': 'delta_w', 'delta_w_ffn_in': 'delta_w', 'delta_w_ffn_out': 'delta_w', 'delta_norm_ple_g': 'delta_w', 'delta_w_ple_gate': 'delta_w', 'delta_b_ple_gate': 'delta_w', 'delta_w_ple_proj': 'delta_w', 'delta_final_norm_g': 'delta_w', 'new_m_norm_mix_g': 'new_m', 'new_m_w_in': 'new_m', 'new_m_ssm_lambda_re': 'new_m', 'new_m_ssm_lambda_im': 'new_m', 'new_m_ssm_log_step': 'new_m', 'new_m_ssm_b_re': 'new_m', 'new_m_ssm_b_im': 'new_m', 'new_m_ssm_c_re': 'new_m', 'new_m_ssm_c_im': 'new_m', 'new_m_ssm_d': 'new_m', 'new_m_ssm_glu_w': 'new_m', 'new_m_ssm_glu_b': 'new_m', 'new_m_sgu_ln_g': 'new_m', 'new_m_sgu_ln_b': 'new_m', 'new_m_sgu_w': 'new_m', 'new_m_sgu_b': 'new_m', 'new_m_out_norm_ssm_g': 'new_m', 'new_m_out_norm_sgu_g': 'new_m', 'new_m_w_out': 'new_m', 'new_m_norm_ffn_g': 'new_m', 'new_m_w_ffn_in': 'new_m', 'new_m_w_ffn_out': 'new_m', 'new_m_norm_ple_g': 'new_m', 'new_m_w_ple_gate': 'new_m', 'new_m_b_ple_gate': 'new_m', 'new_m_w_ple_proj': 'new_m', 'new_m_final_norm_g': 'new_m', 'new_v_norm_mix_g': 'new_v', 'new_v_w_in': 'new_v', 'new_v_ssm_lambda_re': 'new_v', 'new_v_ssm_lambda_im': 'new_v', 'new_v_ssm_log_step': 'new_v', 'new_v_ssm_b_re': 'new_v', 'new_v_ssm_b_im': 'new_v', 'new_v_ssm_c_re': 'new_v', 'new_v_ssm_c_im': 'new_v', 'new_v_ssm_d': 'new_v', 'new_v_ssm_glu_w': 'new_v', 'new_v_ssm_glu_b': 'new_v', 'new_v_sgu_ln_g': 'new_v', 'new_v_sgu_ln_b': 'new_v', 'new_v_sgu_w': 'new_v', 'new_v_sgu_b': 'new_v', 'new_v_out_norm_ssm_g': 'new_v', 'new_v_out_norm_sgu_g': 'new_v', 'new_v_w_out': 'new_v', 'new_v_norm_ffn_g': 'new_v', 'new_v_w_ffn_in': 'new_v', 'new_v_w_ffn_out': 'new_v', 'new_v_norm_ple_g': 'new_v', 'new_v_w_ple_gate': 'new_v', 'new_v_b_ple_gate': 'new_v', 'new_v_w_ple_proj': 'new_v', 'new_v_final_norm_g': 'new_v'}


def _forward(args):
    return _fwd_reference(*[args[k] for k in FWD_PARAMS])


def _output_shape():
    def fwd():
        inp = _fwd_setup_inputs(0)
        return _fwd_reference(*[inp[k] for k in FWD_PARAMS])
    out = _jax.eval_shape(fwd)
    return out.shape, out.dtype

N_MICROBATCH = 1
ADAM_LR = 0.001
ADAM_B1 = 0.9
ADAM_B2 = 0.999
ADAM_EPS = 1e-08
ADAM_WD = 0.01
ADAM_STEP = 10
PER_EXAMPLE_BATCH_AXIS = {'x': 0, 'p': 1, 'loss_target': 0}
SHARED_INPUTS = []
_WEIGHT_DTYPES = {'norm_mix_g': _jnp.float32, 'w_in': _jnp.float32, 'ssm_lambda_re': _jnp.float32, 'ssm_lambda_im': _jnp.float32, 'ssm_log_step': _jnp.float32, 'ssm_b_re': _jnp.float32, 'ssm_b_im': _jnp.float32, 'ssm_c_re': _jnp.float32, 'ssm_c_im': _jnp.float32, 'ssm_d': _jnp.float32, 'ssm_glu_w': _jnp.float32, 'ssm_glu_b': _jnp.float32, 'sgu_ln_g': _jnp.float32, 'sgu_ln_b': _jnp.float32, 'sgu_w': _jnp.float32, 'sgu_b': _jnp.float32, 'out_norm_ssm_g': _jnp.float32, 'out_norm_sgu_g': _jnp.float32, 'w_out': _jnp.float32, 'norm_ffn_g': _jnp.float32, 'w_ffn_in': _jnp.float32, 'w_ffn_out': _jnp.float32, 'norm_ple_g': _jnp.float32, 'w_ple_gate': _jnp.float32, 'b_ple_gate': _jnp.float32, 'w_ple_proj': _jnp.float32, 'final_norm_g': _jnp.float32}
MOMENT_SCALE = {'norm_mix_g': 7.384930e-02, 'w_in': 5.959131e-02, 'ssm_lambda_re': 6.371796e-03, 'ssm_lambda_im': 6.991930e-03, 'ssm_log_step': 6.106010e+00, 'ssm_b_re': 4.392555e-03, 'ssm_b_im': 4.341252e-03, 'ssm_c_re': 8.397039e-03, 'ssm_c_im': 8.513606e-03, 'ssm_d': 1.395776e-01, 'ssm_glu_w': 9.505570e-03, 'ssm_glu_b': 3.331427e-02, 'sgu_ln_g': 3.640433e-02, 'sgu_ln_b': 3.631868e-02, 'sgu_w': 3.579627e-02, 'sgu_b': 5.292835e-02, 'out_norm_ssm_g': 6.961110e-02, 'out_norm_sgu_g': 6.750332e-02, 'w_out': 6.546098e-02, 'norm_ffn_g': 4.737343e-02, 'w_ffn_in': 2.013078e-02, 'w_ffn_out': 3.283944e-02, 'norm_ple_g': 1.150337e-02, 'w_ple_gate': 1.167555e-02, 'b_ple_gate': 1.437036e-02, 'w_ple_proj': 2.966719e-02, 'final_norm_g': 1.603665e+01}


def _to_microbatches(a, axis):
    t = _jnp.moveaxis(a, axis, 0)
    t = t.reshape((N_MICROBATCH, t.shape[0] // N_MICROBATCH) + t.shape[1:])
    return _jnp.moveaxis(t, 1, axis + 1)


def setup_inputs(seed: int = 0) -> dict:
    inp = _fwd_setup_inputs(seed)
    key = _jax.random.fold_in(_jax.random.key(seed), 7919)
    shape, _ = _output_shape()
    out = dict(inp)
    out["loss_target"] = _jax.random.normal(_jax.random.fold_in(key, 0), shape, _jnp.float32)
    for i, name in enumerate(TWIN_WEIGHTS):
        w = inp[name].astype(_jnp.float32)
        if MOMENT_SCALE is None:
            s = _jnp.sqrt(_jnp.mean(_jnp.square(w)) + 1e-30)
        else:
            s = MOMENT_SCALE[name]
        km, kv = _jax.random.split(_jax.random.fold_in(key, i + 1))
        out[name] = w
        out["m_" + name] = s * _jax.random.normal(km, w.shape, _jnp.float32)
        out["v_" + name] = (s * s) * _jax.random.uniform(kv, w.shape, _jnp.float32, 0.5, 1.5)
    if N_MICROBATCH > 1:
        for name, axis in PER_EXAMPLE_BATCH_AXIS.items():
            out[name] = _to_microbatches(out[name], axis)
    return {'x': out['x'], 'p': out['p'], 'norm_mix_g': out['norm_mix_g'], 'w_in': out['w_in'], 'ssm_lambda_re': out['ssm_lambda_re'], 'ssm_lambda_im': out['ssm_lambda_im'], 'ssm_log_step': out['ssm_log_step'], 'ssm_b_re': out['ssm_b_re'], 'ssm_b_im': out['ssm_b_im'], 'ssm_c_re': out['ssm_c_re'], 'ssm_c_im': out['ssm_c_im'], 'ssm_d': out['ssm_d'], 'ssm_glu_w': out['ssm_glu_w'], 'ssm_glu_b': out['ssm_glu_b'], 'sgu_ln_g': out['sgu_ln_g'], 'sgu_ln_b': out['sgu_ln_b'], 'sgu_w': out['sgu_w'], 'sgu_b': out['sgu_b'], 'out_norm_ssm_g': out['out_norm_ssm_g'], 'out_norm_sgu_g': out['out_norm_sgu_g'], 'w_out': out['w_out'], 'norm_ffn_g': out['norm_ffn_g'], 'w_ffn_in': out['w_ffn_in'], 'w_ffn_out': out['w_ffn_out'], 'norm_ple_g': out['norm_ple_g'], 'w_ple_gate': out['w_ple_gate'], 'b_ple_gate': out['b_ple_gate'], 'w_ple_proj': out['w_ple_proj'], 'final_norm_g': out['final_norm_g'], 'loss_target': out['loss_target'], 'm_norm_mix_g': out['m_norm_mix_g'], 'm_w_in': out['m_w_in'], 'm_ssm_lambda_re': out['m_ssm_lambda_re'], 'm_ssm_lambda_im': out['m_ssm_lambda_im'], 'm_ssm_log_step': out['m_ssm_log_step'], 'm_ssm_b_re': out['m_ssm_b_re'], 'm_ssm_b_im': out['m_ssm_b_im'], 'm_ssm_c_re': out['m_ssm_c_re'], 'm_ssm_c_im': out['m_ssm_c_im'], 'm_ssm_d': out['m_ssm_d'], 'm_ssm_glu_w': out['m_ssm_glu_w'], 'm_ssm_glu_b': out['m_ssm_glu_b'], 'm_sgu_ln_g': out['m_sgu_ln_g'], 'm_sgu_ln_b': out['m_sgu_ln_b'], 'm_sgu_w': out['m_sgu_w'], 'm_sgu_b': out['m_sgu_b'], 'm_out_norm_ssm_g': out['m_out_norm_ssm_g'], 'm_out_norm_sgu_g': out['m_out_norm_sgu_g'], 'm_w_out': out['m_w_out'], 'm_norm_ffn_g': out['m_norm_ffn_g'], 'm_w_ffn_in': out['m_w_ffn_in'], 'm_w_ffn_out': out['m_w_ffn_out'], 'm_norm_ple_g': out['m_norm_ple_g'], 'm_w_ple_gate': out['m_w_ple_gate'], 'm_b_ple_gate': out['m_b_ple_gate'], 'm_w_ple_proj': out['m_w_ple_proj'], 'm_final_norm_g': out['m_final_norm_g'], 'v_norm_mix_g': out['v_norm_mix_g'], 'v_w_in': out['v_w_in'], 'v_ssm_lambda_re': out['v_ssm_lambda_re'], 'v_ssm_lambda_im': out['v_ssm_lambda_im'], 'v_ssm_log_step': out['v_ssm_log_step'], 'v_ssm_b_re': out['v_ssm_b_re'], 'v_ssm_b_im': out['v_ssm_b_im'], 'v_ssm_c_re': out['v_ssm_c_re'], 'v_ssm_c_im': out['v_ssm_c_im'], 'v_ssm_d': out['v_ssm_d'], 'v_ssm_glu_w': out['v_ssm_glu_w'], 'v_ssm_glu_b': out['v_ssm_glu_b'], 'v_sgu_ln_g': out['v_sgu_ln_g'], 'v_sgu_ln_b': out['v_sgu_ln_b'], 'v_sgu_w': out['v_sgu_w'], 'v_sgu_b': out['v_sgu_b'], 'v_out_norm_ssm_g': out['v_out_norm_ssm_g'], 'v_out_norm_sgu_g': out['v_out_norm_sgu_g'], 'v_w_out': out['v_w_out'], 'v_norm_ffn_g': out['v_norm_ffn_g'], 'v_w_ffn_in': out['v_w_ffn_in'], 'v_w_ffn_out': out['v_w_ffn_out'], 'v_norm_ple_g': out['v_norm_ple_g'], 'v_w_ple_gate': out['v_w_ple_gate'], 'v_b_ple_gate': out['v_b_ple_gate'], 'v_w_ple_proj': out['v_w_ple_proj'], 'v_final_norm_g': out['v_final_norm_g']}


def _loss(weights, diff, rest, loss_target):
    with _jax.named_scope("forward"):
        args = {**rest, TWIN_DIFF_INPUT: diff, **{k: w.astype(_WEIGHT_DTYPES[k]) for k, w in weights.items()}}
        y = _forward(args)
    with _jax.named_scope("loss_head"):
        err = _jnp.square(y.astype(_jnp.float32) - loss_target)
        return 0.5 * _jnp.sum(_jnp.mean(err, axis=-1)) if err.ndim else 0.5 * err


def _adamw(w, g, m, v):
    m = ADAM_B1 * m + (1.0 - ADAM_B1) * g
    v = ADAM_B2 * v + (1.0 - ADAM_B2) * _jnp.square(g)
    m_hat = m / (1.0 - ADAM_B1 ** ADAM_STEP)
    v_hat = v / (1.0 - ADAM_B2 ** ADAM_STEP)
    delta = -ADAM_LR * (m_hat / (_jnp.sqrt(v_hat) + ADAM_EPS) + ADAM_WD * w)
    return delta, m, v


def reference(x, p, norm_mix_g, w_in, ssm_lambda_re, ssm_lambda_im, ssm_log_step, ssm_b_re, ssm_b_im, ssm_c_re, ssm_c_im, ssm_d, ssm_glu_w, ssm_glu_b, sgu_ln_g, sgu_ln_b, sgu_w, sgu_b, out_norm_ssm_g, out_norm_sgu_g, w_out, norm_ffn_g, w_ffn_in, w_ffn_out, norm_ple_g, w_ple_gate, b_ple_gate, w_ple_proj, final_norm_g, loss_target, m_norm_mix_g, m_w_in, m_ssm_lambda_re, m_ssm_lambda_im, m_ssm_log_step, m_ssm_b_re, m_ssm_b_im, m_ssm_c_re, m_ssm_c_im, m_ssm_d, m_ssm_glu_w, m_ssm_glu_b, m_sgu_ln_g, m_sgu_ln_b, m_sgu_w, m_sgu_b, m_out_norm_ssm_g, m_out_norm_sgu_g, m_w_out, m_norm_ffn_g, m_w_ffn_in, m_w_ffn_out, m_norm_ple_g, m_w_ple_gate, m_b_ple_gate, m_w_ple_proj, m_final_norm_g, v_norm_mix_g, v_w_in, v_ssm_lambda_re, v_ssm_lambda_im, v_ssm_log_step, v_ssm_b_re, v_ssm_b_im, v_ssm_c_re, v_ssm_c_im, v_ssm_d, v_ssm_glu_w, v_ssm_glu_b, v_sgu_ln_g, v_sgu_ln_b, v_sgu_w, v_sgu_b, v_out_norm_ssm_g, v_out_norm_sgu_g, v_w_out, v_norm_ffn_g, v_w_ffn_in, v_w_ffn_out, v_norm_ple_g, v_w_ple_gate, v_b_ple_gate, v_w_ple_proj, v_final_norm_g):
    given = dict(x=x, p=p, norm_mix_g=norm_mix_g, w_in=w_in, ssm_lambda_re=ssm_lambda_re, ssm_lambda_im=ssm_lambda_im, ssm_log_step=ssm_log_step, ssm_b_re=ssm_b_re, ssm_b_im=ssm_b_im, ssm_c_re=ssm_c_re, ssm_c_im=ssm_c_im, ssm_d=ssm_d, ssm_glu_w=ssm_glu_w, ssm_glu_b=ssm_glu_b, sgu_ln_g=sgu_ln_g, sgu_ln_b=sgu_ln_b, sgu_w=sgu_w, sgu_b=sgu_b, out_norm_ssm_g=out_norm_ssm_g, out_norm_sgu_g=out_norm_sgu_g, w_out=w_out, norm_ffn_g=norm_ffn_g, w_ffn_in=w_ffn_in, w_ffn_out=w_ffn_out, norm_ple_g=norm_ple_g, w_ple_gate=w_ple_gate, b_ple_gate=b_ple_gate, w_ple_proj=w_ple_proj, final_norm_g=final_norm_g, loss_target=loss_target, m_norm_mix_g=m_norm_mix_g, m_w_in=m_w_in, m_ssm_lambda_re=m_ssm_lambda_re, m_ssm_lambda_im=m_ssm_lambda_im, m_ssm_log_step=m_ssm_log_step, m_ssm_b_re=m_ssm_b_re, m_ssm_b_im=m_ssm_b_im, m_ssm_c_re=m_ssm_c_re, m_ssm_c_im=m_ssm_c_im, m_ssm_d=m_ssm_d, m_ssm_glu_w=m_ssm_glu_w, m_ssm_glu_b=m_ssm_glu_b, m_sgu_ln_g=m_sgu_ln_g, m_sgu_ln_b=m_sgu_ln_b, m_sgu_w=m_sgu_w, m_sgu_b=m_sgu_b, m_out_norm_ssm_g=m_out_norm_ssm_g, m_out_norm_sgu_g=m_out_norm_sgu_g, m_w_out=m_w_out, m_norm_ffn_g=m_norm_ffn_g, m_w_ffn_in=m_w_ffn_in, m_w_ffn_out=m_w_ffn_out, m_norm_ple_g=m_norm_ple_g, m_w_ple_gate=m_w_ple_gate, m_b_ple_gate=m_b_ple_gate, m_w_ple_proj=m_w_ple_proj, m_final_norm_g=m_final_norm_g, v_norm_mix_g=v_norm_mix_g, v_w_in=v_w_in, v_ssm_lambda_re=v_ssm_lambda_re, v_ssm_lambda_im=v_ssm_lambda_im, v_ssm_log_step=v_ssm_log_step, v_ssm_b_re=v_ssm_b_re, v_ssm_b_im=v_ssm_b_im, v_ssm_c_re=v_ssm_c_re, v_ssm_c_im=v_ssm_c_im, v_ssm_d=v_ssm_d, v_ssm_glu_w=v_ssm_glu_w, v_ssm_glu_b=v_ssm_glu_b, v_sgu_ln_g=v_sgu_ln_g, v_sgu_ln_b=v_sgu_ln_b, v_sgu_w=v_sgu_w, v_sgu_b=v_sgu_b, v_out_norm_ssm_g=v_out_norm_ssm_g, v_out_norm_sgu_g=v_out_norm_sgu_g, v_w_out=v_w_out, v_norm_ffn_g=v_norm_ffn_g, v_w_ffn_in=v_w_ffn_in, v_w_ffn_out=v_w_ffn_out, v_norm_ple_g=v_norm_ple_g, v_w_ple_gate=v_w_ple_gate, v_b_ple_gate=v_b_ple_gate, v_w_ple_proj=v_w_ple_proj, v_final_norm_g=v_final_norm_g)
    weights = {n: given[n] for n in TWIN_WEIGHTS}
    shared = {n: given[n] for n in SHARED_INPUTS}
    per_example = {n: given[n] for n in ['x', 'p']}
    grad_fn = _jax.value_and_grad(_loss, argnums=(0, 1))

    def one_microbatch(ex, loss_target):
        ex = dict(ex)
        diff = ex.pop(TWIN_DIFF_INPUT)
        return grad_fn(weights, diff, {**shared, **ex}, loss_target)

    if N_MICROBATCH == 1:
        loss, (grad_w, grad_x) = one_microbatch(per_example, given["loss_target"])
    else:
        def body(carry, xs):
            loss_sum, grad_sum = carry
            l_k, (gw_k, gx_k) = one_microbatch(xs[0], xs[1])
            with _jax.named_scope("update"):
                return (loss_sum + l_k, _jax.tree.map(_jnp.add, grad_sum, gw_k)), gx_k

        init = (_jnp.zeros((), _jnp.float32), _jax.tree.map(_jnp.zeros_like, weights))
        (loss, grad_w), grad_x = _jax.lax.scan(body, init, (per_example, given["loss_target"]))
    with _jax.named_scope("update"):
        delta_w, new_m, new_v = {}, {}, {}
        for n in TWIN_WEIGHTS:
            delta_w[n], new_m[n], new_v[n] = _adamw(weights[n], grad_w[n], given["m_" + n], given["v_" + n])
    return (loss, grad_x, *[grad_w[n] for n in TWIN_WEIGHTS], *[delta_w[n] for n in TWIN_WEIGHTS],
            *[new_m[n] for n in TWIN_WEIGHTS], *[new_v[n] for n in TWIN_WEIGHTS])
```

```python
import functools

import jax
import jax.numpy as jnp
from jax import lax
from jax.experimental import pallas as pl
from jax.experimental.pallas import tpu as pltpu

F32 = jnp.float32
BF16 = jnp.bfloat16

EPS = 1e-6
LAMBDA_RE_MAX = -1e-4
ADAM_LR = 0.001
ADAM_B1 = 0.9
ADAM_B2 = 0.999
ADAM_EPS = 1e-08
ADAM_WD = 0.01
ADAM_STEP = 10

N_CHIPS = 4
N_DEV = 8
SUBLANES = 8
LANES = 128
SSM_CH_BLOCK = 256
SCAN_LANES = 256
VMEM_LIMIT = 56 * 1024 * 1024

MESH = pl.DeviceIdType.MESH


def _pick(n, pref, mult):
    if n <= pref:
        return n
    t = (pref // mult) * mult
    while t >= mult:
        if n % t == 0:
            return t
        t -= mult
    return n


def _params(semantics):
    return pltpu.CompilerParams(dimension_semantics=semantics, vmem_limit_bytes=VMEM_LIMIT)


class _Cols:
    def __init__(self, arr, width, blk):
        self.arr, self.width, self.blk = arr, width, blk


def _sds(shape, dtype):
    return jax.ShapeDtypeStruct(tuple(shape), dtype)


def _rowwise(name, fn, rows, params, row_outs, acc_outs=(), tr=256):
    rows = [r if isinstance(r, _Cols) else _Cols(r, r.shape[1], 0) for r in rows]
    m = rows[0].arr.shape[0]
    tr = _pick(m, tr, 16)
    n_in = len(rows) + len(params)
    n_ro = len(row_outs)

    def body(*refs):
        vals = fn(*[r[...] for r in refs[:n_in]])
        if not isinstance(vals, (tuple, list)):
            vals = (vals,)
        outs = refs[n_in:]
        for r, v in zip(outs[:n_ro], vals[:n_ro]):
            r[...] = v.astype(r.dtype)
        first = pl.program_id(0) == 0
        for r, v in zip(outs[n_ro:], vals[n_ro:]):
            @pl.when(first)
            def _():
                r[...] = jnp.zeros(r.shape, r.dtype)
            r[...] += v.astype(r.dtype).reshape(r.shape)

    in_specs = [pl.BlockSpec((tr, r.width), lambda i, b=r.blk: (i, b)) for r in rows]
    in_specs += [pl.BlockSpec(p.shape, lambda i, nd=p.ndim: (0,) * nd) for p in params]
    out_specs = [pl.BlockSpec((tr, o.shape[1]), lambda i: (i, 0)) for o in row_outs]
    out_specs += [pl.BlockSpec(o.shape, lambda i, nd=len(o.shape): (0,) * nd) for o in acc_outs]
    outs = pl.pallas_call(
        body, name=name, grid=(m // tr,), in_specs=in_specs, out_specs=out_specs,
        out_shape=[*row_outs, *acc_outs], compiler_params=_params(("arbitrary",)),
    )(*[r.arr for r in rows], *params)
    return outs


def _whole(name, fn, ins, outs):
    n_in = len(ins)

    def body(*refs):
        vals = fn(*[r[...] for r in refs[:n_in]])
        if not isinstance(vals, (tuple, list)):
            vals = (vals,)
        for r, v in zip(refs[n_in:], vals):
            r[...] = v.astype(r.dtype).reshape(r.shape)

    vm = pl.BlockSpec(memory_space=pltpu.VMEM)
    return pl.pallas_call(body, name=name, in_specs=[vm] * n_in, out_specs=[vm] * len(outs), out_shape=list(outs),
                          compiler_params=pltpu.CompilerParams(vmem_limit_bytes=VMEM_LIMIT))(*ins)


def _mm_nn(name, a, w, *, sharded=False, res=None, out_dtype=F32, tm=512, tn=256):
    m, k = a.shape
    tm = _pick(m, tm, 16)
    if sharded:
        s, _, ns = w.shape
        n = s * ns
        tn = _pick(ns, tn, LANES)
        per = ns // tn
        w_spec = pl.BlockSpec((None, k, tn), lambda i, j: (j // per, 0, j % per))
    else:
        n = w.shape[1]
        tn = _pick(n, tn, LANES)
        w_spec = pl.BlockSpec((k, tn), lambda i, j: (0, j))

    def body(a_ref, w_ref, *rest):
        acc = jnp.dot(a_ref[...], w_ref[...], preferred_element_type=F32)
        if res is not None:
            acc = acc + rest[0][...]
        rest[-1][...] = acc.astype(out_dtype)

    in_specs = [pl.BlockSpec((tm, k), lambda i, j: (i, 0)), w_spec]
    ops = [a, w]
    if res is not None:
        in_specs.append(pl.BlockSpec((tm, tn), lambda i, j: (i, j)))
        ops.append(res)
    return pl.pallas_call(
        body, name=name, grid=(m // tm, n // tn), in_specs=in_specs,
        out_specs=pl.BlockSpec((tm, tn), lambda i, j: (i, j)), out_shape=_sds((m, n), out_dtype),
        compiler_params=_params(("arbitrary", "arbitrary")),
    )(*ops)


def _mm_nt(name, g, w, *, sharded=False, tm=512, tk=256):
    m, n = g.shape
    tm = _pick(m, tm, 16)
    dims = (((1,), (1,)), ((), ()))
    if sharded:
        s, k, ns = w.shape
        tk = _pick(k, tk, LANES)
        w_spec = pl.BlockSpec((s, tk, ns), lambda i, j: (0, j, 0))

        def body(g_ref, w_ref, o_ref):
            acc = lax.dot_general(g_ref[:, 0:ns], w_ref[0], dims, preferred_element_type=F32)
            for q in range(1, s):
                acc = acc + lax.dot_general(g_ref[:, q * ns:(q + 1) * ns], w_ref[q], dims, preferred_element_type=F32)
            o_ref[...] = acc
    else:
        k = w.shape[0]
        tk = _pick(k, tk, LANES)
        w_spec = pl.BlockSpec((tk, n), lambda i, j: (j, 0))

        def body(g_ref, w_ref, o_ref):
            o_ref[...] = lax.dot_general(g_ref[...], w_ref[...], dims, preferred_element_type=F32)

    return pl.pallas_call(
        body, name=name, grid=(m // tm, k // tk), in_specs=[pl.BlockSpec((tm, n), lambda i, j: (i, 0)), w_spec],
        out_specs=pl.BlockSpec((tm, tk), lambda i, j: (i, j)), out_shape=_sds((m, k), F32),
        compiler_params=_params(("arbitrary", "arbitrary")),
    )(g, w)


def _mm_tn(name, a, g, *, shards=0, tk=512, tn=256):
    m, k = a.shape
    n = g.shape[1]
    tk = _pick(k, tk, LANES)
    dims = (((0,), (0,)), ((), ()))
    if shards:
        ns = n // shards
        tn = _pick(ns, tn, LANES)
        per = ns // tn
        out_spec = pl.BlockSpec((None, tk, tn), lambda i, j: (j // per, i, j % per))
        out_shape = _sds((shards, k, ns), F32)
    else:
        tn = _pick(n, tn, LANES)
        out_spec = pl.BlockSpec((tk, tn), lambda i, j: (i, j))
        out_shape = _sds((k, n), F32)

    def body(a_ref, g_ref, o_ref):
        o_ref[...] = lax.dot_general(a_ref[...], g_ref[...], dims, preferred_element_type=F32)

    return pl.pallas_call(
        body, name=name, grid=(k // tk, n // tn),
        in_specs=[pl.BlockSpec((m, tk), lambda i, j: (0, i)), pl.BlockSpec((m, tn), lambda i, j: (0, j))],
        out_specs=out_spec, out_shape=out_shape, compiler_params=_params(("arbitrary", "arbitrary")),
    )(a, g)


def _rms(x, g):
    r = lax.rsqrt(jnp.mean(x * x, axis=-1, keepdims=True) + EPS)
    return (x * r) * g


def _glu_out(y_pre, q, glu_b, g_norm):
    ya0 = jax.nn.gelu(y_pre)
    return _rms(ya0 * jax.nn.sigmoid(q + glu_b), g_norm)


def _sgu_rows(zu, zv, ln_g, ln_b, w_s, b_st, g_norm):
    heads, t, _ = w_s.shape
    hd = zu.shape[1] // heads
    uu = jax.nn.gelu(zu)
    vv = jax.nn.gelu(zv)
    mu = jnp.mean(vv, axis=-1, keepdims=True)
    xc = vv - mu
    r = lax.rsqrt(jnp.mean(xc * xc, axis=-1, keepdims=True) + EPS)
    vn = (xc * r) * ln_g + ln_b
    row = lax.broadcasted_iota(jnp.int32, (t, t), 0)
    col = lax.broadcasted_iota(jnp.int32, (t, t), 1)
    causal = row >= col
    chunks = []
    for n in range(zu.shape[0] // t):
        blocks = []
        for h in range(heads):
            wm = jnp.where(causal, w_s[h], jnp.zeros_like(w_s[h])).astype(BF16)
            vb = vn[n * t:(n + 1) * t, h * hd:(h + 1) * hd].astype(BF16)
            blocks.append(jnp.dot(wm, vb, preferred_element_type=F32) + b_st[:, h:h + 1])
        chunks.append(jnp.concatenate(blocks, axis=1))
    s = jnp.concatenate(chunks, axis=0) if len(chunks) > 1 else chunks[0]
    return _rms(uu * s, g_norm)


def _swiglu(gate, up):
    return jax.nn.silu(gate) * up


def _head_loss(x2, gpre, pp, b_g, g_final, target):
    gate = jax.nn.sigmoid(gpre + b_g)
    out = _rms(x2 + gate * pp, g_final)
    err = jnp.square(out - target)
    return 0.5 * jnp.sum(jnp.mean(err, axis=-1))


def _ssm_disc(lam_re, lam_im, log_step_col):
    lr = jnp.minimum(lam_re, LAMBDA_RE_MAX)
    li = lam_im
    dt = jnp.exp(log_step_col)
    mag = jnp.exp(lr * dt)
    ang = li * dt
    abar_re = mag * jnp.cos(ang)
    abar_im = mag * jnp.sin(ang)
    nr = abar_re - 1.0
    ni = abar_im
    den = lr * lr + li * li
    q_re = (nr * lr + ni * li) / den
    q_im = (ni * lr - nr * li) / den
    return abar_re, abar_im, q_re, q_im


def _ssm_bbar(q_re_col, q_im_col, b_re, b_im):
    return q_re_col * b_re - q_im_col * b_im, q_re_col * b_im + q_im_col * b_re


def _adamw(w, g, m, v):
    m = ADAM_B1 * m + (1.0 - ADAM_B1) * g
    v = ADAM_B2 * v + (1.0 - ADAM_B2) * jnp.square(g)
    m_hat = m / (1.0 - ADAM_B1 ** ADAM_STEP)
    v_hat = v / (1.0 - ADAM_B2 ** ADAM_STEP)
    delta = -ADAM_LR * (m_hat / (jnp.sqrt(v_hat) + ADAM_EPS) + ADAM_WD * w)
    return delta, m, v


class _SsmDims:
    def __init__(self, groups, state, gch):
        self.g, self.p, self.h = groups, state, gch
        self.d = groups * gch
        self.cb = min(SSM_CH_BLOCK, self.d)
        self.gb = self.cb // gch
        self.ns = self.gb * state
        self.nb = self.d // self.cb


def _ssm_forward_params(sd, lam_re, lam_im, log_step, b_re, b_im):
    gp = sd.g * sd.p

    def disc(lr, li, ls):
        ar, ai, qr, qi = _ssm_disc(lr, li, ls)
        pr, pi_ = [ar], [ai]
        for _ in range(SUBLANES - 1):
            pr, pi_ = pr + [pr[-1] * ar - pi_[-1] * ai], pi_ + [pr[-1] * ai + pi_[-1] * ar]
        return ar, ai, qr, qi, jnp.concatenate(pr, axis=0), jnp.concatenate(pi_, axis=0)

    gp_s = _sds((sd.g, sd.p), F32)
    pw_s = _sds((SUBLANES * sd.g, sd.p), F32)
    ar, ai, qr, qi, pw_re, pw_im = _whole("ssm_disc", disc, [lam_re, lam_im, log_step.reshape(sd.g, 1)],
                                          [gp_s, gp_s, gp_s, gp_s, pw_s, pw_s])
    qr_col, qi_col = qr.reshape(gp, 1), qi.reshape(gp, 1)
    bb_s = _sds((gp, sd.h), F32)
    bbar_re, bbar_im = _whole("ssm_bbar", _ssm_bbar, [qr_col, qi_col, b_re.reshape(gp, sd.h), b_im.reshape(gp, sd.h)],
                              [bb_s, bb_s])
    return qr_col, qi_col, bbar_re, bbar_im, pw_re.reshape(SUBLANES, sd.g, sd.p), pw_im.reshape(SUBLANES, sd.g, sd.p)


def _blockdiag_in(sd, bbar):
    b = bbar.reshape(sd.nb, sd.gb, sd.p, sd.h).transpose(0, 1, 3, 2)
    eye = jnp.eye(sd.gb, dtype=bbar.dtype)
    return (b[:, :, :, None, :] * eye[None, :, None, :, None]).reshape(sd.nb, sd.cb, sd.ns)


def _blockdiag_out(sd, c):
    cc = c.reshape(sd.nb, sd.gb, sd.h, sd.p).transpose(0, 1, 3, 2)
    eye = jnp.eye(sd.gb, dtype=c.dtype)
    return (cc[:, :, :, None, :] * eye[None, :, None, :, None]).reshape(sd.nb, sd.ns, sd.cb)


def _diag_in(sd, dense):
    x = dense.reshape(sd.nb, sd.gb, sd.h, sd.gb, sd.p)
    return jnp.einsum("jghgp->jgph", x).reshape(sd.g * sd.p, sd.h)


def _diag_out(sd, dense):
    x = dense.reshape(sd.nb, sd.gb, sd.p, sd.gb, sd.h)
    return jnp.einsum("jgpgh->jghp", x).reshape(sd.g, sd.h, sd.p)


def _scan_consts(sd, pw_re, pw_im, reverse):
    pr = pw_re.reshape(SUBLANES, sd.nb, sd.ns)
    pi_ = pw_im.reshape(SUBLANES, sd.nb, sd.ns)
    if reverse:
        pi_ = -pi_
    rows = jnp.arange(SUBLANES)[None, :, None]
    parts = []
    for d in (1, 2, 4):
        keep = (rows < SUBLANES - d) if reverse else (rows >= d)
        parts += [jnp.where(keep, pr[d - 1][:, None, :], 0.0), jnp.where(keep, pi_[d - 1][:, None, :], 0.0)]
    cr, ci = pr.transpose(1, 0, 2), pi_.transpose(1, 0, 2)
    if reverse:
        cr, ci = cr[:, ::-1, :], ci[:, ::-1, :]
    return jnp.concatenate(parts + [cr, ci], axis=1).astype(F32)


def _block_scan(s_ref, cst_ref, carry_ref, sd, rows, reverse):
    ns = sd.ns
    nblk = rows // SUBLANES
    w = min(SCAN_LANES, ns)
    for c0 in range(0, ns, w):
        re_l, im_l = slice(c0, c0 + w), slice(ns + c0, ns + c0 + w)
        cst = [cst_ref[k * SUBLANES:(k + 1) * SUBLANES, c0:c0 + w] for k in range(8)]

        def step(k, carry, re_l=re_l, im_l=im_l, cst=cst):
            cr, ci = carry
            blk = (nblk - 1 - k) if reverse else k
            r0 = pl.multiple_of(blk * SUBLANES, SUBLANES)
            xr = s_ref[pl.ds(r0, SUBLANES), re_l]
            xi = s_ref[pl.ds(r0, SUBLANES), im_l]
            for n, d in enumerate((1, 2, 4)):
                ar, ai = cst[2 * n], cst[2 * n + 1]
                shift = (SUBLANES - d) if reverse else d
                sr = pltpu.roll(xr, shift, 0)
                si = pltpu.roll(xi, shift, 0)
                xr, xi = xr + ar * sr - ai * si, xi + ar * si + ai * sr
            br = jnp.broadcast_to(cr, xr.shape)
            bi = jnp.broadcast_to(ci, xi.shape)
            xr, xi = xr + cst[6] * br - cst[7] * bi, xi + cst[6] * bi + cst[7] * br
            s_ref[pl.ds(r0, SUBLANES), re_l] = xr
            s_ref[pl.ds(r0, SUBLANES), im_l] = xi
            edge = slice(0, 1) if reverse else slice(SUBLANES - 1, SUBLANES)
            return xr[edge, :], xi[edge, :]

        cr, ci = lax.fori_loop(0, nblk, step, (carry_ref[0:1, re_l], carry_ref[0:1, im_l]), unroll=2)
        carry_ref[0:1, re_l] = cr
        carry_ref[0:1, im_l] = ci


def _ssm_fwd(sd, z, wb, wc, cst, d_row, tt=256):
    n_tok = z.shape[0]
    tt = _pick(n_tok, tt, SUBLANES)
    cb, ns2 = sd.cb, 2 * sd.ns

    def body(z_ref, wb_ref, wc_ref, cst_ref, d_ref, y_ref, s_ref, carry_ref):
        @pl.when(pl.program_id(1) == 0)
        def _():
            carry_ref[...] = jnp.zeros(carry_ref.shape, F32)
        u = z_ref[...]
        s_ref[...] = jnp.dot(u.astype(BF16), wb_ref[...], preferred_element_type=F32)
        _block_scan(s_ref, cst_ref, carry_ref, sd, tt, reverse=False)
        y = jnp.dot(s_ref[...].astype(BF16), wc_ref[...], preferred_element_type=F32)
        y_ref[...] = y + d_ref[...] * u

    return pl.pallas_call(
        body, name="ssm_fwd", grid=(sd.nb, n_tok // tt),
        in_specs=[pl.BlockSpec((tt, cb), lambda j, i: (i, j)),
                  pl.BlockSpec((None, cb, ns2), lambda j, i: (j, 0, 0)),
                  pl.BlockSpec((None, ns2, cb), lambda j, i: (j, 0, 0)),
                  pl.BlockSpec((None, 8 * SUBLANES, sd.ns), lambda j, i: (j, 0, 0)),
                  pl.BlockSpec((1, cb), lambda j, i: (0, j))],
        out_specs=[pl.BlockSpec((tt, cb), lambda j, i: (i, j)), pl.BlockSpec((tt, ns2), lambda j, i: (i, j))],
        out_shape=[_sds((n_tok, sd.d), F32), _sds((n_tok, sd.nb * ns2), F32)],
        scratch_shapes=[pltpu.VMEM((SUBLANES, ns2), F32)],
        compiler_params=_params(("arbitrary", "arbitrary")),
    )(z, wb, wc, cst, d_row)


def _ssm_bwd(sd, dy, z, states, wct, wbt, cst_rev, d_row, tt=256):
    n_tok = z.shape[0]
    tt = _pick(n_tok, tt, SUBLANES)
    nt = n_tok // tt
    cb, ns, ns2 = sd.cb, sd.ns, 2 * sd.ns
    blocks_per_tile = tt // SUBLANES
    tn_dims = (((0,), (0,)), ((), ()))

    def body(dy_ref, z_ref, s_ref, sp_ref, wct_ref, wbt_ref, cst_ref, d_ref,
             du_ref, dwb_ref, dwc_ref, da_ref, dd_ref, lam_ref, carry_ref):
        i = pl.program_id(1)

        @pl.when(i == 0)
        def _():
            carry_ref[...] = jnp.zeros(carry_ref.shape, F32)
            dwb_ref[...] = jnp.zeros(dwb_ref.shape, F32)
            dwc_ref[...] = jnp.zeros(dwc_ref.shape, F32)
            da_ref[...] = jnp.zeros(da_ref.shape, F32)
            dd_ref[...] = jnp.zeros(dd_ref.shape, F32)

        dy_t = dy_ref[...]
        u = z_ref[...]
        dy16 = dy_t.astype(BF16)
        lam_ref[...] = jnp.dot(dy16, wct_ref[...], preferred_element_type=F32)
        _block_scan(lam_ref, cst_ref, carry_ref, sd, tt, reverse=True)
        lam = lam_ref[...]
        lam16 = lam.astype(BF16)
        du_ref[...] = jnp.dot(lam16, wbt_ref[...], preferred_element_type=F32) + d_ref[...] * dy_t
        dd_ref[0:1, :] += jnp.sum(dy_t * u, axis=0, keepdims=True)
        dwb_ref[...] += lax.dot_general(u.astype(BF16), lam16, tn_dims, preferred_element_type=F32)
        s = s_ref[...]
        dwc_ref[...] += lax.dot_general(s.astype(BF16), dy16, tn_dims, preferred_element_type=F32)
        before = jnp.where(i == nt - 1, 0.0, 1.0) * sp_ref[SUBLANES - 1:SUBLANES, :]
        first_row = lax.broadcasted_iota(jnp.int32, s.shape, 0) == 0
        prev = jnp.where(first_row, jnp.broadcast_to(before, s.shape), pltpu.roll(s, 1, 0))
        lr, li = lam[:, :ns], lam[:, ns:]
        pr, pi_ = prev[:, :ns], prev[:, ns:]
        da_ref[0:1, 0:ns] += jnp.sum(lr * pr + li * pi_, axis=0, keepdims=True)
        da_ref[0:1, ns:ns2] += jnp.sum(li * pr - lr * pi_, axis=0, keepdims=True)

    rev = lambda i: nt - 1 - i
    return pl.pallas_call(
        body, name="ssm_bwd", grid=(sd.nb, nt),
        in_specs=[pl.BlockSpec((tt, cb), lambda j, i: (rev(i), j)),
                  pl.BlockSpec((tt, cb), lambda j, i: (rev(i), j)),
                  pl.BlockSpec((tt, ns2), lambda j, i: (rev(i), j)),
                  pl.BlockSpec((SUBLANES, ns2), lambda j, i: (jnp.maximum(rev(i) * blocks_per_tile - 1, 0), j)),
                  pl.BlockSpec((None, cb, ns2), lambda j, i: (j, 0, 0)),
                  pl.BlockSpec((None, ns2, cb), lambda j, i: (j, 0, 0)),
                  pl.BlockSpec((None, 8 * SUBLANES, ns), lambda j, i: (j, 0, 0)),
                  pl.BlockSpec((1, cb), lambda j, i: (0, j))],
        out_specs=[pl.BlockSpec((tt, cb), lambda j, i: (rev(i), j)),
                   pl.BlockSpec((None, cb, ns2), lambda j, i: (j, 0, 0)),
                   pl.BlockSpec((None, ns2, cb), lambda j, i: (j, 0, 0)),
                   pl.BlockSpec((None, SUBLANES, ns2), lambda j, i: (j, 0, 0)),
                   pl.BlockSpec((None, SUBLANES, cb), lambda j, i: (j, 0, 0))],
        out_shape=[_sds((n_tok, sd.d), F32), _sds((sd.nb, cb, ns2), F32), _sds((sd.nb, ns2, cb), F32),
                   _sds((sd.nb, SUBLANES, ns2), F32), _sds((sd.nb, SUBLANES, cb), F32)],
        scratch_shapes=[pltpu.VMEM((tt, ns2), F32), pltpu.VMEM((SUBLANES, ns2), F32)],
        compiler_params=_params(("arbitrary", "arbitrary")),
    )(dy, z, states, states, wct, wbt, cst_rev, d_row)


def _local_grads(x, p, target, wg, sp):
    n_tok, d_model = x.shape
    d_ssm = sp["ssm_d"].shape[0] * sp["ssm_d"].shape[1]
    d_sgu = sp["sgu_ln_g"].shape[-1]
    d_ffn = wg["w_ffn_out"].shape[0]
    sd = _SsmDims(sp["ssm_b_re"].shape[0], sp["ssm_b_re"].shape[1], sp["ssm_b_re"].shape[2])
    heads, chunk, _ = sp["sgu_w"].shape
    row = lambda v: v.reshape(1, -1)
    tok = lambda w, dt=F32: _sds((n_tok, w), dt)
    acc = lambda w: _sds((1, w), F32)

    g_mix = row(sp["norm_mix_g"])
    (h1,) = _rowwise("norm_mix", lambda a, g: _rms(a, g), [x], [g_mix], [tok(d_model, BF16)])
    z = _mm_nn("proj_in", h1, wg["w_in"], sharded=True)

    qr_col, qi_col, bbar_re, bbar_im, pw_re, pw_im = _ssm_forward_params(
        sd, sp["ssm_lambda_re"], sp["ssm_lambda_im"], sp["ssm_log_step"], sp["ssm_b_re"], sp["ssm_b_im"])
    wb = jnp.concatenate([_blockdiag_in(sd, bbar_re), _blockdiag_in(sd, bbar_im)], axis=2).astype(BF16)
    wc = jnp.concatenate([_blockdiag_out(sd, sp["ssm_c_re"]), -_blockdiag_out(sd, sp["ssm_c_im"])], axis=1).astype(BF16)
    d_row = row(sp["ssm_d"])
    y_pre, states = _ssm_fwd(sd, z, wb, wc, _scan_consts(sd, pw_re, pw_im, False), d_row)

    (ya0_16,) = _rowwise("ssm_gelu", lambda a: jax.nn.gelu(a), [y_pre], [], [tok(d_ssm, BF16)])
    q = _mm_nn("ssm_glu", ya0_16, wg["ssm_glu_w"])
    glu_b, g_ossm = row(sp["ssm_glu_b"]), row(sp["out_norm_ssm_g"])
    (ya_n,) = _rowwise("ssm_glu_out", _glu_out, [y_pre, q], [glu_b, g_ossm], [tok(d_ssm, BF16)])

    assert d_ssm == d_sgu
    zu, zv = _Cols(z, d_sgu, 1), _Cols(z, d_sgu, 2)
    ln_g, ln_b, g_osgu = row(sp["sgu_ln_g"]), row(sp["sgu_ln_b"]), row(sp["out_norm_sgu_g"])
    b_st = sp["sgu_b"].T
    sgu_tr = 2 * chunk
    (yb_n,) = _rowwise("sgu", _sgu_rows, [zu, zv], [ln_g, ln_b, sp["sgu_w"], b_st, g_osgu], [tok(d_sgu, BF16)], tr=sgu_tr)

    ycat = jnp.concatenate([ya_n, yb_n], axis=1)
    x1 = _mm_nn("proj_out", ycat, wg["w_out"], res=x)

    g_ffn = row(sp["norm_ffn_g"])
    (h2,) = _rowwise("norm_ffn", lambda a, g: _rms(a, g), [x1], [g_ffn], [tok(d_model, BF16)])
    gu = _mm_nn("ffn_in", h2, wg["w_ffn_in"], sharded=True)
    gate_c, up_c = _Cols(gu, d_ffn, 0), _Cols(gu, d_ffn, 1)
    (act,) = _rowwise("swiglu", _swiglu, [gate_c, up_c], [], [tok(d_ffn, BF16)], tr=128)
    x2 = _mm_nn("ffn_out", act, wg["w_ffn_out"], res=x1)

    g_ple = row(sp["norm_ple_g"])
    (h3,) = _rowwise("norm_ple", lambda a, g: _rms(a, g), [x2], [g_ple], [tok(d_model, BF16)])
    gpre = _mm_nn("ple_gate", h3, wg["w_ple_gate"])
    (p16,) = _rowwise("ple_cast", lambda a: a, [p], [], [tok(p.shape[1], BF16)])
    pp = _mm_nn("ple_proj", p16, wg["w_ple_proj"], sharded=True)

    b_g, g_fin = row(sp["b_ple_gate"]), row(sp["final_norm_g"])

    def head(x2_t, gpre_t, pp_t, tgt_t, b_g_v, g_fin_v):
        loss, grads = jax.value_and_grad(_head_loss, argnums=(0, 1, 2, 3, 4))(x2_t, gpre_t, pp_t, b_g_v, g_fin_v, tgt_t)
        dx2, dgpre, dpp, db, dg = grads
        return dx2, dgpre.astype(BF16), dpp.astype(BF16), jnp.full((1, LANES), loss, F32), db, dg

    dx2_head, dgpre16, dpp16, loss_row, d_b_g, d_g_fin = _rowwise(
        "head", head, [x2, gpre, pp, target], [b_g, g_fin],
        [tok(d_model), tok(d_model, BF16), tok(d_model, BF16)], [acc(LANES), acc(d_model), acc(d_model)])
    loss = loss_row[0, 0]

    grads = {}
    grads["w_ple_proj"] = _mm_tn("d_ple_proj", p16, dpp16, shards=N_CHIPS, tk=256)
    grads["w_ple_gate"] = _mm_tn("d_ple_gate", h3, dgpre16)
    dh3 = _mm_nt("d_h3", dgpre16, wg["w_ple_gate"])

    def norm_bwd(x_t, dres_t, dh_t, g_v):
        _, vjp = jax.vjp(_rms, x_t, g_v)
        dx, dg = vjp(dh_t)
        dx = dres_t + dx
        return dx, dx.astype(BF16), dg

    dx2, dx2_16, d_g_ple = _rowwise("d_norm_ple", norm_bwd, [x2, dx2_head, dh3], [g_ple],
                                    [tok(d_model), tok(d_model, BF16)], [acc(d_model)])
    grads["w_ffn_out"] = _mm_tn("d_ffn_out", act, dx2_16)
    dact = _mm_nt("d_act", dx2_16, wg["w_ffn_out"])

    def swiglu_bwd(gate_t, up_t, dact_t):
        _, vjp = jax.vjp(_swiglu, gate_t, up_t)
        dg, du = vjp(dact_t)
        return jnp.concatenate([dg, du], axis=1)

    (dgu16,) = _rowwise("d_swiglu", swiglu_bwd, [gate_c, up_c, dact], [], [tok(2 * d_ffn, BF16)], tr=128)
    grads["w_ffn_in"] = _mm_tn("d_ffn_in", h2, dgu16, shards=N_CHIPS)
    dh2 = _mm_nt("d_h2", dgu16, wg["w_ffn_in"], sharded=True, tm=256)
    dx1, dx1_16, d_g_ffn = _rowwise("d_norm_ffn", norm_bwd, [x1, dx2, dh2], [g_ffn],
                                    [tok(d_model), tok(d_model, BF16)], [acc(d_model)])
    grads["w_out"] = _mm_tn("d_proj_out", ycat, dx1_16)
    dycat = _mm_nt("d_ycat", dx1_16, wg["w_out"])

    def glu_out_bwd(y_pre_t, q_t, dy_t, glu_b_v, g_v):
        _, vjp = jax.vjp(_glu_out, y_pre_t, q_t, glu_b_v, g_v)
        dy_pre, dq, db, dg = vjp(dy_t)
        return dy_pre, dq.astype(BF16), db, dg

    dy_pre_a, dq16, d_glu_b, d_g_ossm = _rowwise(
        "d_ssm_glu_out", glu_out_bwd, [y_pre, q, _Cols(dycat, d_ssm, 0)], [glu_b, g_ossm],
        [tok(d_ssm), tok(d_ssm, BF16)], [acc(d_ssm), acc(d_ssm)])
    grads["ssm_glu_w"] = _mm_tn("d_ssm_glu", ya0_16, dq16)
    dya0 = _mm_nt("d_ya0", dq16, wg["ssm_glu_w"])

    def gelu_bwd(y_pre_t, dy_a_t, dya0_t):
        _, vjp = jax.vjp(jax.nn.gelu, y_pre_t)
        return dy_a_t + vjp(dya0_t)[0]

    (dy_pre,) = _rowwise("d_ssm_gelu", gelu_bwd, [y_pre, dy_pre_a, dya0], [], [tok(d_ssm)])

    wct, wbt = jnp.swapaxes(wc, 1, 2), jnp.swapaxes(wb, 1, 2)
    dz_ssm, dwb, dwc, da, dd = _ssm_bwd(sd, dy_pre, z, states, wct, wbt, _scan_consts(sd, pw_re, pw_im, True), d_row)

    def sgu_bwd(zu_t, zv_t, dy_t, ln_g_v, ln_b_v, w_v, b_v, g_v):
        _, vjp = jax.vjp(_sgu_rows, zu_t, zv_t, ln_g_v, ln_b_v, w_v, b_v, g_v)
        dzu, dzv, dlg, dlb, dw, db, dg = vjp(dy_t)
        return dzu, dzv, dlg, dlb, dw, db, dg

    dzu, dzv, d_ln_g, d_ln_b, d_sgu_w, d_b_st, d_g_osgu = _rowwise(
        "d_sgu", sgu_bwd, [zu, zv, _Cols(dycat, d_sgu, 1)], [ln_g, ln_b, sp["sgu_w"], b_st, g_osgu],
        [tok(d_sgu, BF16), tok(d_sgu, BF16)],
        [acc(d_sgu), acc(d_sgu), _sds(sp["sgu_w"].shape, F32), _sds(b_st.shape, F32), acc(d_sgu)], tr=sgu_tr)

    (dz_ssm16,) = _rowwise("d_ssm_cast", lambda a: a, [dz_ssm], [], [tok(d_ssm, BF16)])
    dz16 = jnp.concatenate([dz_ssm16, dzu, dzv], axis=1)
    grads["w_in"] = _mm_tn("d_proj_in", h1, dz16, shards=N_CHIPS)
    dh1 = _mm_nt("d_h1", dz16, wg["w_in"], sharded=True)

    def norm_in_bwd(x_t, dres_t, dh_t, g_v):
        _, vjp = jax.vjp(_rms, x_t, g_v)
        dx, dg = vjp(dh_t)
        return dres_t + dx, dg

    grad_x, d_g_mix = _rowwise("d_norm_mix", norm_in_bwd, [x, dx1, dh1], [g_mix], [tok(d_model)], [acc(d_model)])

    gp = sd.g * sd.p
    dbbar_re = _diag_in(sd, dwb[:, :, :sd.ns])
    dbbar_im = _diag_in(sd, dwb[:, :, sd.ns:])
    d_c_re = _diag_out(sd, dwc[:, :sd.ns, :])
    d_c_im = -_diag_out(sd, dwc[:, sd.ns:, :])
    dabar_re = da[:, 0, :sd.ns].reshape(sd.g, sd.p)
    dabar_im = da[:, 0, sd.ns:].reshape(sd.g, sd.p)
    d_ssm_d = dd[:, 0, :].reshape(sd.g, sd.h)

    def bbar_bwd(qr, qi, b_re_v, b_im_v, dre, dim):
        _, vjp = jax.vjp(_ssm_bbar, qr, qi, b_re_v, b_im_v)
        return vjp((dre, dim))

    b_re2, b_im2 = sp["ssm_b_re"].reshape(gp, sd.h), sp["ssm_b_im"].reshape(gp, sd.h)
    dq_re, dq_im, d_b_re, d_b_im = _whole(
        "d_ssm_bbar", bbar_bwd, [qr_col, qi_col, b_re2, b_im2, dbbar_re, dbbar_im],
        [_sds((gp, 1), F32), _sds((gp, 1), F32), _sds((gp, sd.h), F32), _sds((gp, sd.h), F32)])

    def disc_bwd(lr, li, ls, dar, dai, dqr, dqi):
        _, vjp = jax.vjp(_ssm_disc, lr, li, ls)
        return vjp((dar, dai, dqr, dqi))

    gp_s = _sds((sd.g, sd.p), F32)
    d_lam_re, d_lam_im, d_log_step = _whole(
        "d_ssm_disc", disc_bwd,
        [sp["ssm_lambda_re"], sp["ssm_lambda_im"], sp["ssm_log_step"].reshape(sd.g, 1),
         dabar_re, dabar_im, dq_re.reshape(sd.g, sd.p), dq_im.reshape(sd.g, sd.p)],
        [gp_s, gp_s, _sds((sd.g, 1), F32)])

    small = {
        "norm_mix_g": d_g_mix, "ssm_lambda_re": d_lam_re, "ssm_lambda_im": d_lam_im, "ssm_log_step": d_log_step,
        "ssm_b_re": d_b_re, "ssm_b_im": d_b_im, "ssm_c_re": d_c_re, "ssm_c_im": d_c_im, "ssm_d": d_ssm_d,
        "ssm_glu_b": d_glu_b, "sgu_ln_g": d_ln_g, "sgu_ln_b": d_ln_b, "sgu_w": d_sgu_w, "sgu_b": d_b_st.T,
        "out_norm_ssm_g": d_g_ossm, "out_norm_sgu_g": d_g_osgu, "norm_ffn_g": d_g_ffn, "norm_ple_g": d_g_ple,
        "b_ple_gate": d_b_g, "final_norm_g": d_g_fin,
    }
    return loss, grad_x, grads, small


def _place():
    x, y, c = lax.axis_index("x"), lax.axis_index("y"), lax.axis_index("c")
    chips = [(1 - x, y), (x, 1 - y), (1 - x, 1 - y)]
    return x, y, c, chips


ANY = pl.BlockSpec(memory_space=pl.ANY)


def _allgather_weights(shards):
    n = len(shards)

    def body(*refs):
        ins, outs = refs[:n], refs[n:2 * n]
        send_sems, recv_sems, local_sems = refs[2 * n:]
        x, y, c, chips = _place()
        sibling = (x, y, 1 - c)
        mine = 2 * x + y

        def remote(k, src, dst, to):
            return pltpu.make_async_remote_copy(src_ref=src, dst_ref=dst, send_sem=send_sems.at[k], recv_sem=recv_sems.at[k],
                                                device_id=to, device_id_type=MESH)

        local = [pltpu.make_async_copy(ins[w], outs[w].at[mine], local_sems.at[w]) for w in range(n)]
        for cp in local:
            cp.start()
        sends = []
        for w in range(n):
            for j, chip in enumerate(chips):
                sends.append(remote(3 * w + j, ins[w].at[c], outs[w].at[mine, c], (*chip, c)))
        for cp in sends:
            cp.start()
        passed = []
        for w in range(n):
            for j, (cx, cy) in enumerate(chips):
                theirs = outs[w].at[2 * cx + cy, c]
                remote(3 * w + j, theirs, theirs, (x, y, c)).wait_recv()
                fwd = remote(3 * n + 3 * w + j, theirs, theirs, sibling)
                fwd.start()
                passed.append(fwd)
        for w in range(n):
            for j, (cx, cy) in enumerate(chips):
                theirs = outs[w].at[2 * cx + cy, 1 - c]
                remote(3 * n + 3 * w + j, theirs, theirs, (x, y, c)).wait_recv()
        for cp in sends + passed:
            cp.wait_send()
        for cp in local:
            cp.wait()

    return pl.pallas_call(
        body, name="allgather_weights", in_specs=[ANY] * n, out_specs=[ANY] * n,
        out_shape=[_sds((N_CHIPS, *s.shape), s.dtype) for s in shards],
        scratch_shapes=[pltpu.SemaphoreType.DMA((6 * n,)), pltpu.SemaphoreType.DMA((6 * n,)), pltpu.SemaphoreType.DMA((n,))],
    )(*shards)


def _swap_halves(grads):
    n = len(grads)

    def body(*refs):
        ins, outs = refs[:n], refs[n:2 * n]
        send_sems, recv_sems = refs[2 * n:]
        x, y, c, _ = _place()
        copies = [pltpu.make_async_remote_copy(src_ref=ins[w].at[:, 1 - c], dst_ref=outs[w], send_sem=send_sems.at[w],
                                               recv_sem=recv_sems.at[w], device_id=(x, y, 1 - c), device_id_type=MESH)
                  for w in range(n)]
        for cp in copies:
            cp.start()
        for cp in copies:
            cp.wait()

    return pl.pallas_call(
        body, name="grad_swap_halves", in_specs=[ANY] * n, out_specs=[ANY] * n,
        out_shape=[_sds((g.shape[0], *g.shape[2:]), g.dtype) for g in grads],
        scratch_shapes=[pltpu.SemaphoreType.DMA((n,)), pltpu.SemaphoreType.DMA((n,))],
    )(*grads)


def _scatter_quarters(halves):
    n = len(halves)

    def body(*refs):
        ins, outs = refs[:n], refs[n:2 * n]
        send_sems, recv_sems = refs[2 * n:]
        x, y, c, chips = _place()
        copies = []
        for w in range(n):
            for j, (cx, cy) in enumerate(chips):
                copies.append(pltpu.make_async_remote_copy(
                    src_ref=ins[w].at[2 * cx + cy], dst_ref=outs[w].at[j], send_sem=send_sems.at[3 * w + j],
                    recv_sem=recv_sems.at[3 * w + j], device_id=(cx, cy, c), device_id_type=MESH))
        for cp in copies:
            cp.start()
        for cp in copies:
            cp.wait()

    return pl.pallas_call(
        body, name="grad_scatter", in_specs=[ANY] * n, out_specs=[ANY] * n,
        out_shape=[_sds((3, *h.shape[1:]), h.dtype) for h in halves],
        scratch_shapes=[pltpu.SemaphoreType.DMA((3 * n,)), pltpu.SemaphoreType.DMA((3 * n,))],
    )(*halves)


def _join_halves(parts):
    n = len(parts)

    def body(*refs):
        ins, outs = refs[:n], refs[n:2 * n]
        send_sems, recv_sems, local_sems = refs[2 * n:]
        x, y, c, _ = _place()
        local = [pltpu.make_async_copy(ins[w], outs[w].at[c], local_sems.at[w]) for w in range(n)]
        copies = [pltpu.make_async_remote_copy(src_ref=ins[w], dst_ref=outs[w].at[c], send_sem=send_sems.at[w],
                                               recv_sem=recv_sems.at[w], device_id=(x, y, 1 - c), device_id_type=MESH)
                  for w in range(n)]
        for cp in local + copies:
            cp.start()
        for w in range(n):
            theirs = outs[w].at[1 - c]
            pltpu.make_async_remote_copy(src_ref=theirs, dst_ref=theirs, send_sem=send_sems.at[w], recv_sem=recv_sems.at[w],
                                         device_id=(x, y, c), device_id_type=MESH).wait_recv()
        for cp in copies:
            cp.wait_send()
        for cp in local:
            cp.wait()

    return pl.pallas_call(
        body, name="grad_join_halves", in_specs=[ANY] * n, out_specs=[ANY] * n,
        out_shape=[_sds((2, *p.shape), p.dtype) for p in parts],
        scratch_shapes=[pltpu.SemaphoreType.DMA((n,)), pltpu.SemaphoreType.DMA((n,)), pltpu.SemaphoreType.DMA((n,))],
    )(*parts)


def _allgather_small(block):
    rows, lanes = block.shape

    def body(x_ref, out_ref, send_sems, recv_sems, local_sem):
        x, y, c, chips = _place()
        me, sibling = (x, y, c), (x, y, 1 - c)

        def slot(px, py, pc):
            return out_ref.at[4 * px + 2 * py + pc]

        def copy(k, block_of, to, src=None):
            return pltpu.make_async_remote_copy(src_ref=slot(*block_of) if src is None else src, dst_ref=slot(*block_of),
                                                send_sem=send_sems.at[k], recv_sem=recv_sems.at[k], device_id=to, device_id_type=MESH)

        mine = pltpu.make_async_copy(x_ref, slot(*me), local_sem)
        mine.start()
        first = [copy(0, me, sibling, src=x_ref)]
        first += [copy(1 + j, me, (*chip, c), src=x_ref) for j, chip in enumerate(chips)]
        for cp in first:
            cp.start()
        passed = [copy(4 + j, (*chip, c), sibling) for j, chip in enumerate(chips)]
        for j, chip in enumerate(chips):
            copy(1 + j, (*chip, c), me).wait_recv()
            passed[j].start()
        copy(0, sibling, me).wait_recv()
        for j, chip in enumerate(chips):
            copy(4 + j, (*chip, 1 - c), me).wait_recv()
        for cp in first + passed:
            cp.wait_send()
        mine.wait()

    return pl.pallas_call(
        body, name="allgather_small", in_specs=[ANY], out_specs=ANY, out_shape=_sds((N_DEV, rows, lanes), block.dtype),
        scratch_shapes=[pltpu.SemaphoreType.DMA((7,)), pltpu.SemaphoreType.DMA((7,)), pltpu.SemaphoreType.DMA],
    )(block)


def _sum_leading(name, stacked, first=None, tr=512):
    n, rows, cols = stacked.shape
    tr = _pick(rows, tr, SUBLANES)

    def body(*refs):
        s_ref, o_ref = refs[-2], refs[-1]
        acc = refs[0][...] if first is not None else s_ref[0]
        for k in range(0 if first is not None else 1, n):
            acc = acc + s_ref[k]
        o_ref[...] = acc

    in_specs = [pl.BlockSpec((n, tr, cols), lambda i: (0, i, 0))]
    ops = [stacked]
    if first is not None:
        in_specs.insert(0, pl.BlockSpec((tr, cols), lambda i: (i, 0)))
        ops.insert(0, first)
    return pl.pallas_call(body, name=name, grid=(rows // tr,), in_specs=in_specs,
                          out_specs=pl.BlockSpec((tr, cols), lambda i: (i, 0)), out_shape=_sds((rows, cols), stacked.dtype),
                          compiler_params=_params(("arbitrary",)))(*ops)


def _add_halves(name, full, c, received, tr=256):
    s, _, rh, cols = full.shape
    tr = _pick(rh, tr, SUBLANES)

    def body(c_ref, a_ref, b_ref, o_ref):
        o_ref[...] = a_ref[...] + b_ref[...]

    grid_spec = pltpu.PrefetchScalarGridSpec(
        num_scalar_prefetch=1, grid=(s, rh // tr),
        in_specs=[pl.BlockSpec((None, None, tr, cols), lambda q, i, c_ref: (q, c_ref[0], i, 0)),
                  pl.BlockSpec((None, tr, cols), lambda q, i, c_ref: (q, i, 0))],
        out_specs=pl.BlockSpec((None, tr, cols), lambda q, i, c_ref: (q, i, 0)))
    return pl.pallas_call(body, name=name, grid_spec=grid_spec, out_shape=_sds((s, rh, cols), full.dtype),
                          compiler_params=_params(("arbitrary", "arbitrary")))(c.reshape(1).astype(jnp.int32), full, received)


def _pick_shard(name, halves, shard, tr=256):
    _, rh, cols = halves.shape
    tr = _pick(rh, tr, SUBLANES)

    def body(s_ref, a_ref, o_ref):
        o_ref[...] = a_ref[...]

    grid_spec = pltpu.PrefetchScalarGridSpec(
        num_scalar_prefetch=1, grid=(rh // tr,),
        in_specs=[pl.BlockSpec((None, tr, cols), lambda i, s_ref: (s_ref[0], i, 0))],
        out_specs=pl.BlockSpec((tr, cols), lambda i, s_ref: (i, 0)))
    return pl.pallas_call(body, name=name, grid_spec=grid_spec, out_shape=_sds((rh, cols), halves.dtype),
                          compiler_params=_params(("arbitrary",)))(shard.reshape(1).astype(jnp.int32), halves)


LARGE = ("w_in", "ssm_glu_w", "w_out", "w_ffn_in", "w_ffn_out", "w_ple_gate", "w_ple_proj")
COLUMN_SHARDED = ("w_in", "w_ffn_in", "w_ple_proj")
SMALL = ("norm_mix_g", "ssm_lambda_re", "ssm_lambda_im", "ssm_log_step", "ssm_b_re", "ssm_b_im", "ssm_c_re", "ssm_c_im",
         "ssm_d", "ssm_glu_b", "sgu_ln_g", "sgu_ln_b", "sgu_w", "sgu_b", "out_norm_ssm_g", "out_norm_sgu_g", "norm_ffn_g",
         "norm_ple_g", "b_ple_gate", "final_norm_g")
WEIGHTS = ("norm_mix_g", "w_in", "ssm_lambda_re", "ssm_lambda_im", "ssm_log_step", "ssm_b_re", "ssm_b_im", "ssm_c_re",
           "ssm_c_im", "ssm_d", "ssm_glu_w", "ssm_glu_b", "sgu_ln_g", "sgu_ln_b", "sgu_w", "sgu_b", "out_norm_ssm_g",
           "out_norm_sgu_g", "w_out", "norm_ffn_g", "w_ffn_in", "w_ffn_out", "norm_ple_g", "w_ple_gate", "b_ple_gate",
           "w_ple_proj", "final_norm_g")
PACK_ROWS = SUBLANES * LANES


def _pack(arrays):
    parts = []
    for a in arrays:
        flat = a.reshape(-1).astype(F32)
        pad = -flat.shape[0] % PACK_ROWS
        parts.append(jnp.pad(flat, (0, pad)) if pad else flat)
    return jnp.concatenate(parts).reshape(-1, LANES)


def _unpack(packed, like):
    flat = packed.reshape(-1)
    out, at = [], 0
    for a in like:
        size = a.size
        out.append(flat[at:at + size].reshape(a.shape))
        at += size + (-size % PACK_ROWS)
    return out


def _reduce_large(grads, c, shard):
    names = list(grads)
    full = [grads[k].reshape(N_CHIPS, 2, grads[k].shape[1] // 2, grads[k].shape[2]) for k in names]
    received = _swap_halves(full)
    halves = [_add_halves("grad_add_halves_" + k, f, c, r) for k, f, r in zip(names, full, received)]
    quarters = _scatter_quarters(halves)
    own = [_pick_shard("grad_own_" + k, h, shard) for k, h in zip(names, halves)]
    parts = [_sum_leading("grad_sum_" + k, q, first=o) for k, q, o in zip(names, quarters, own)]
    joined = _join_halves(parts)
    return {k: j.reshape(2 * j.shape[1], j.shape[2]) for k, j in zip(names, joined)}


def kernel(x, p, norm_mix_g, w_in, ssm_lambda_re, ssm_lambda_im, ssm_log_step, ssm_b_re, ssm_b_im, ssm_c_re, ssm_c_im, ssm_d, ssm_glu_w, ssm_glu_b, sgu_ln_g, sgu_ln_b, sgu_w, sgu_b, out_norm_ssm_g, out_norm_sgu_g, w_out, norm_ffn_g, w_ffn_in, w_ffn_out, norm_ple_g, w_ple_gate, b_ple_gate, w_ple_proj, final_norm_g, loss_target, m_norm_mix_g, m_w_in, m_ssm_lambda_re, m_ssm_lambda_im, m_ssm_log_step, m_ssm_b_re, m_ssm_b_im, m_ssm_c_re, m_ssm_c_im, m_ssm_d, m_ssm_glu_w, m_ssm_glu_b, m_sgu_ln_g, m_sgu_ln_b, m_sgu_w, m_sgu_b, m_out_norm_ssm_g, m_out_norm_sgu_g, m_w_out, m_norm_ffn_g, m_w_ffn_in, m_w_ffn_out, m_norm_ple_g, m_w_ple_gate, m_b_ple_gate, m_w_ple_proj, m_final_norm_g, v_norm_mix_g, v_w_in, v_ssm_lambda_re, v_ssm_lambda_im, v_ssm_log_step, v_ssm_b_re, v_ssm_b_im, v_ssm_c_re, v_ssm_c_im, v_ssm_d, v_ssm_glu_w, v_ssm_glu_b, v_sgu_ln_g, v_sgu_ln_b, v_sgu_w, v_sgu_b, v_out_norm_ssm_g, v_out_norm_sgu_g, v_w_out, v_norm_ffn_g, v_w_ffn_in, v_w_ffn_out, v_norm_ple_g, v_w_ple_gate, v_b_ple_gate, v_w_ple_proj, v_final_norm_g):
    given = dict(locals())
    w = {k: given[k] for k in WEIGHTS}
    m = {k: given["m_" + k] for k in WEIGHTS}
    v = {k: given["v_" + k] for k in WEIGHTS}
    c = lax.axis_index("c")
    shard = 2 * lax.axis_index("x") + lax.axis_index("y")

    halves16 = []
    for k in LARGE:
        rows, cols = w[k].shape[1:]
        (w16,) = _rowwise("cast_" + k, lambda a: a, [w[k].reshape(rows, cols)], [], [_sds((rows, cols), BF16)])
        halves16.append(w16.reshape(2, rows // 2, cols))
    gathered = _allgather_weights(halves16)
    wg = {}
    for k, g in zip(LARGE, gathered):
        _, _, rh, cols = g.shape
        wg[k] = g.reshape(N_CHIPS, 2 * rh, cols) if k in COLUMN_SHARDED else g.reshape(N_CHIPS * 2 * rh, cols)

    unlayer = lambda a: a if a.ndim == 1 else a[0]
    sp = {k: unlayer(w[k]) for k in SMALL}
    n_tok, d_model = x.shape[1:]
    loss, grad_x, grads, small = _local_grads(x.reshape(n_tok, d_model), p.reshape(n_tok, p.shape[-1]),
                                              loss_target.reshape(n_tok, d_model), wg, sp)
    loss = lax.psum(loss, ("x", "y", "c"))

    shard_major = {}
    for k in LARGE:
        g = grads[k]
        shard_major[k] = g if k in COLUMN_SHARDED else g.reshape(N_CHIPS, g.shape[0] // N_CHIPS, g.shape[1])
    reduced = _reduce_large(shard_major, c, shard)

    grad_w, delta_w, new_m, new_v = {}, {}, {}, {}
    for k in LARGE:
        shape = w[k].shape
        two_d = lambda a: a.reshape(shape[1:])
        like = _sds(shape[1:], F32)
        d_k, m_k, v_k = _rowwise("adamw_" + k, _adamw, [two_d(w[k]), reduced[k], two_d(m[k]), two_d(v[k])], [], [like, like, like])
        grad_w[k], delta_w[k], new_m[k], new_v[k] = (a.reshape(shape) for a in (reduced[k], d_k, m_k, v_k))

    packed_g = _sum_leading("small_sum", _allgather_small(_pack([small[k].reshape(w[k].shape) for k in SMALL])))
    like = _sds(packed_g.shape, F32)
    d_s, m_s, v_s = _rowwise("adamw_small", _adamw, [_pack([w[k] for k in SMALL]), packed_g, _pack([m[k] for k in SMALL]),
                                                     _pack([v[k] for k in SMALL])], [], [like, like, like])
    shapes = [w[k] for k in SMALL]
    for k, g_k, d_k, m_k, v_k in zip(SMALL, _unpack(packed_g, shapes), _unpack(d_s, shapes), _unpack(m_s, shapes), _unpack(v_s, shapes)):
        grad_w[k], delta_w[k], new_m[k], new_v[k] = g_k, d_k, m_k, v_k

    return (loss, grad_x.reshape(x.shape), *[grad_w[k] for k in WEIGHTS], *[delta_w[k] for k in WEIGHTS],
            *[new_m[k] for k in WEIGHTS], *[new_v[k] for k in WEIGHTS])
```

```python
import functools

import jax
import jax.numpy as jnp
from jax import lax
from jax.experimental import pallas as pl
from jax.experimental.pallas import tpu as pltpu

F32 = jnp.float32
BF16 = jnp.bfloat16

EPS = 1e-6
LAMBDA_RE_MAX = -1e-4
ADAM_LR = 0.001
ADAM_B1 = 0.9
ADAM_B2 = 0.999
ADAM_EPS = 1e-08
ADAM_WD = 0.01
ADAM_STEP = 10

N_CHIPS = 4
N_DEV = 8
SUBLANES = 8
LANES = 128
SSM_CH_BLOCK = 256
SCAN_LANES = 256
VMEM_LIMIT = 56 * 1024 * 1024

MESH = pl.DeviceIdType.MESH


def _pick(n, pref, mult):
    if n <= pref:
        return n
    t = (pref // mult) * mult
    while t >= mult:
        if n % t == 0:
            return t
        t -= mult
    return n


def _params(semantics):
    return pltpu.CompilerParams(dimension_semantics=semantics, vmem_limit_bytes=VMEM_LIMIT)


class _Cols:
    def __init__(self, arr, width, blk):
        self.arr, self.width, self.blk = arr, width, blk


def _sds(shape, dtype):
    return jax.ShapeDtypeStruct(tuple(shape), dtype)


def _rowwise(name, fn, rows, params, row_outs, acc_outs=(), tr=256):
    rows = [r if isinstance(r, _Cols) else _Cols(r, r.shape[1], 0) for r in rows]
    m = rows[0].arr.shape[0]
    tr = _pick(m, tr, 16)
    n_in = len(rows) + len(params)
    n_ro = len(row_outs)

    def body(*refs):
        vals = fn(*[r[...] for r in refs[:n_in]])
        if not isinstance(vals, (tuple, list)):
            vals = (vals,)
        outs = refs[n_in:]
        for r, v in zip(outs[:n_ro], vals[:n_ro]):
            r[...] = v.astype(r.dtype)
        first = pl.program_id(0) == 0
        for r, v in zip(outs[n_ro:], vals[n_ro:]):
            @pl.when(first)
            def _():
                r[...] = jnp.zeros(r.shape, r.dtype)
            r[...] += v.astype(r.dtype).reshape(r.shape)

    in_specs = [pl.BlockSpec((tr, r.width), lambda i, b=r.blk: (i, b)) for r in rows]
    in_specs += [pl.BlockSpec(p.shape, lambda i, nd=p.ndim: (0,) * nd) for p in params]
    out_specs = [pl.BlockSpec((tr, o.shape[1]), lambda i: (i, 0)) for o in row_outs]
    out_specs += [pl.BlockSpec(o.shape, lambda i, nd=len(o.shape): (0,) * nd) for o in acc_outs]
    outs = pl.pallas_call(
        body, name=name, grid=(m // tr,), in_specs=in_specs, out_specs=out_specs,
        out_shape=[*row_outs, *acc_outs], compiler_params=_params(("arbitrary",)),
    )(*[r.arr for r in rows], *params)
    return outs


def _whole(name, fn, ins, outs):
    n_in = len(ins)

    def body(*refs):
        vals = fn(*[r[...] for r in refs[:n_in]])
        if not isinstance(vals, (tuple, list)):
            vals = (vals,)
        for r, v in zip(refs[n_in:], vals):
            r[...] = v.astype(r.dtype).reshape(r.shape)

    vm = pl.BlockSpec(memory_space=pltpu.VMEM)
    return pl.pallas_call(body, name=name, in_specs=[vm] * n_in, out_specs=[vm] * len(outs), out_shape=list(outs),
                          compiler_params=pltpu.CompilerParams(vmem_limit_bytes=VMEM_LIMIT))(*ins)


def _mm_nn(name, a, w, *, sharded=False, res=None, out_dtype=F32, tm=512, tn=256):
    m, k = a.shape
    tm = _pick(m, tm, 16)
    if sharded:
        s, _, ns = w.shape
        n = s * ns
        tn = _pick(ns, tn, LANES)
        per = ns // tn
        w_spec = pl.BlockSpec((None, k, tn), lambda i, j: (j // per, 0, j % per))
    else:
        n = w.shape[1]
        tn = _pick(n, tn, LANES)
        w_spec = pl.BlockSpec((k, tn), lambda i, j: (0, j))

    def body(a_ref, w_ref, *rest):
        acc = jnp.dot(a_ref[...], w_ref[...], preferred_element_type=F32)
        if res is not None:
            acc = acc + rest[0][...]
        rest[-1][...] = acc.astype(out_dtype)

    in_specs = [pl.BlockSpec((tm, k), lambda i, j: (i, 0)), w_spec]
    ops = [a, w]
    if res is not None:
        in_specs.append(pl.BlockSpec((tm, tn), lambda i, j: (i, j)))
        ops.append(res)
    return pl.pallas_call(
        body, name=name, grid=(m // tm, n // tn), in_specs=in_specs,
        out_specs=pl.BlockSpec((tm, tn), lambda i, j: (i, j)), out_shape=_sds((m, n), out_dtype),
        compiler_params=_params(("arbitrary", "arbitrary")),
    )(*ops)


def _mm_nt(name, g, w, *, sharded=False, tm=512, tk=256):
    m, n = g.shape
    tm = _pick(m, tm, 16)
    dims = (((1,), (1,)), ((), ()))
    if sharded:
        s, k, ns = w.shape
        tk = _pick(k, tk, LANES)
        w_spec = pl.BlockSpec((s, tk, ns), lambda i, j: (0, j, 0))

        def body(g_ref, w_ref, o_ref):
            acc = lax.dot_general(g_ref[:, 0:ns], w_ref[0], dims, preferred_element_type=F32)
            for q in range(1, s):
                acc = acc + lax.dot_general(g_ref[:, q * ns:(q + 1) * ns], w_ref[q], dims, preferred_element_type=F32)
            o_ref[...] = acc
    else:
        k = w.shape[0]
        tk = _pick(k, tk, LANES)
        w_spec = pl.BlockSpec((tk, n), lambda i, j: (j, 0))

        def body(g_ref, w_ref, o_ref):
            o_ref[...] = lax.dot_general(g_ref[...], w_ref[...], dims, preferred_element_type=F32)

    return pl.pallas_call(
        body, name=name, grid=(m // tm, k // tk), in_specs=[pl.BlockSpec((tm, n), lambda i, j: (i, 0)), w_spec],
        out_specs=pl.BlockSpec((tm, tk), lambda i, j: (i, j)), out_shape=_sds((m, k), F32),
        compiler_params=_params(("arbitrary", "arbitrary")),
    )(g, w)


def _mm_tn(name, a, g, *, shards=0, tk=512, tn=256):
    m, k = a.shape
    n = g.shape[1]
    tk = _pick(k, tk, LANES)
    dims = (((0,), (0,)), ((), ()))
    if shards:
        ns = n // shards
        tn = _pick(ns, tn, LANES)
        per = ns // tn
        out_spec = pl.BlockSpec((None, tk, tn), lambda i, j: (j // per, i, j % per))
        out_shape = _sds((shards, k, ns), F32)
    else:
        tn = _pick(n, tn, LANES)
        out_spec = pl.BlockSpec((tk, tn), lambda i, j: (i, j))
        out_shape = _sds((k, n), F32)

    def body(a_ref, g_ref, o_ref):
        o_ref[...] = lax.dot_general(a_ref[...], g_ref[...], dims, preferred_element_type=F32)

    return pl.pallas_call(
        body, name=name, grid=(k // tk, n // tn),
        in_specs=[pl.BlockSpec((m, tk), lambda i, j: (0, i)), pl.BlockSpec((m, tn), lambda i, j: (0, j))],
        out_specs=out_spec, out_shape=out_shape, compiler_params=_params(("arbitrary", "arbitrary")),
    )(a, g)


def _rms(x, g):
    r = lax.rsqrt(jnp.mean(x * x, axis=-1, keepdims=True) + EPS)
    return (x * r) * g


def _glu_out(y_pre, q, glu_b, g_norm):
    ya0 = jax.nn.gelu(y_pre)
    return _rms(ya0 * jax.nn.sigmoid(q + glu_b), g_norm)


def _sgu_rows(zu, zv, ln_g, ln_b, w_s, b_st, g_norm):
    heads, t, _ = w_s.shape
    hd = zu.shape[1] // heads
    uu = jax.nn.gelu(zu)
    vv = jax.nn.gelu(zv)
    mu = jnp.mean(vv, axis=-1, keepdims=True)
    xc = vv - mu
    r = lax.rsqrt(jnp.mean(xc * xc, axis=-1, keepdims=True) + EPS)
    vn = (xc * r) * ln_g + ln_b
    row = lax.broadcasted_iota(jnp.int32, (t, t), 0)
    col = lax.broadcasted_iota(jnp.int32, (t, t), 1)
    causal = row >= col
    chunks = []
    for n in range(zu.shape[0] // t):
        blocks = []
        for h in range(heads):
            wm = jnp.where(causal, w_s[h], jnp.zeros_like(w_s[h])).astype(BF16)
            vb = vn[n * t:(n + 1) * t, h * hd:(h + 1) * hd].astype(BF16)
            blocks.append(jnp.dot(wm, vb, preferred_element_type=F32) + b_st[:, h:h + 1])
        chunks.append(jnp.concatenate(blocks, axis=1))
    s = jnp.concatenate(chunks, axis=0) if len(chunks) > 1 else chunks[0]
    return _rms(uu * s, g_norm)


def _swiglu(gate, up):
    return jax.nn.silu(gate) * up


def _head_loss(x2, gpre, pp, b_g, g_final, target):
    gate = jax.nn.sigmoid(gpre + b_g)
    out = _rms(x2 + gate * pp, g_final)
    err = jnp.square(out - target)
    return 0.5 * jnp.sum(jnp.mean(err, axis=-1))


def _ssm_disc(lam_re, lam_im, log_step_col):
    lr = jnp.minimum(lam_re, LAMBDA_RE_MAX)
    li = lam_im
    dt = jnp.exp(log_step_col)
    mag = jnp.exp(lr * dt)
    ang = li * dt
    abar_re = mag * jnp.cos(ang)
    abar_im = mag * jnp.sin(ang)
    nr = abar_re - 1.0
    ni = abar_im
    den = lr * lr + li * li
    q_re = (nr * lr + ni * li) / den
    q_im = (ni * lr - nr * li) / den
    return abar_re, abar_im, q_re, q_im


def _ssm_bbar(q_re_col, q_im_col, b_re, b_im):
    return q_re_col * b_re - q_im_col * b_im, q_re_col * b_im + q_im_col * b_re


def _adamw(w, g, m, v):
    m = ADAM_B1 * m + (1.0 - ADAM_B1) * g
    v = ADAM_B2 * v + (1.0 - ADAM_B2) * jnp.square(g)
    m_hat = m / (1.0 - ADAM_B1 ** ADAM_STEP)
    v_hat = v / (1.0 - ADAM_B2 ** ADAM_STEP)
    delta = -ADAM_LR * (m_hat / (jnp.sqrt(v_hat) + ADAM_EPS) + ADAM_WD * w)
    return delta, m, v


class _SsmDims:
    def __init__(self, groups, state, gch):
        self.g, self.p, self.h = groups, state, gch
        self.d = groups * gch
        self.cb = min(SSM_CH_BLOCK, self.d)
        self.gb = self.cb // gch
        self.ns = self.gb * state
        self.nb = self.d // self.cb


def _ssm_forward_params(sd, lam_re, lam_im, log_step, b_re, b_im):
    gp = sd.g * sd.p

    def disc(lr, li, ls):
        ar, ai, qr, qi = _ssm_disc(lr, li, ls)
        pr, pi_ = [ar], [ai]
        for _ in range(SUBLANES - 1):
            pr, pi_ = pr + [pr[-1] * ar - pi_[-1] * ai], pi_ + [pr[-1] * ai + pi_[-1] * ar]
        return ar, ai, qr, qi, jnp.concatenate(pr, axis=0), jnp.concatenate(pi_, axis=0)

    gp_s = _sds((sd.g, sd.p), F32)
    pw_s = _sds((SUBLANES * sd.g, sd.p), F32)
    ar, ai, qr, qi, pw_re, pw_im = _whole("ssm_disc", disc, [lam_re, lam_im, log_step.reshape(sd.g, 1)],
                                          [gp_s, gp_s, gp_s, gp_s, pw_s, pw_s])
    qr_col, qi_col = qr.reshape(gp, 1), qi.reshape(gp, 1)
    bb_s = _sds((gp, sd.h), F32)
    bbar_re, bbar_im = _whole("ssm_bbar", _ssm_bbar, [qr_col, qi_col, b_re.reshape(gp, sd.h), b_im.reshape(gp, sd.h)],
                              [bb_s, bb_s])
    return qr_col, qi_col, bbar_re, bbar_im, pw_re.reshape(SUBLANES, sd.g, sd.p), pw_im.reshape(SUBLANES, sd.g, sd.p)


def _blockdiag_in(sd, bbar):
    b = bbar.reshape(sd.nb, sd.gb, sd.p, sd.h).transpose(0, 1, 3, 2)
    eye = jnp.eye(sd.gb, dtype=bbar.dtype)
    return (b[:, :, :, None, :] * eye[None, :, None, :, None]).reshape(sd.nb, sd.cb, sd.ns)


def _blockdiag_out(sd, c):
    cc = c.reshape(sd.nb, sd.gb, sd.h, sd.p).transpose(0, 1, 3, 2)
    eye = jnp.eye(sd.gb, dtype=c.dtype)
    return (cc[:, :, :, None, :] * eye[None, :, None, :, None]).reshape(sd.nb, sd.ns, sd.cb)


def _diag_in(sd, dense):
    x = dense.reshape(sd.nb, sd.gb, sd.h, sd.gb, sd.p)
    return jnp.einsum("jghgp->jgph", x).reshape(sd.g * sd.p, sd.h)


def _diag_out(sd, dense):
    x = dense.reshape(sd.nb, sd.gb, sd.p, sd.gb, sd.h)
    return jnp.einsum("jgpgh->jghp", x).reshape(sd.g, sd.h, sd.p)


def _scan_consts(sd, pw_re, pw_im, reverse):
    pr = pw_re.reshape(SUBLANES, sd.nb, sd.ns)
    pi_ = pw_im.reshape(SUBLANES, sd.nb, sd.ns)
    if reverse:
        pi_ = -pi_
    rows = jnp.arange(SUBLANES)[None, :, None]
    parts = []
    for d in (1, 2, 4):
        keep = (rows < SUBLANES - d) if reverse else (rows >= d)
        parts += [jnp.where(keep, pr[d - 1][:, None, :], 0.0), jnp.where(keep, pi_[d - 1][:, None, :], 0.0)]
    cr, ci = pr.transpose(1, 0, 2), pi_.transpose(1, 0, 2)
    if reverse:
        cr, ci = cr[:, ::-1, :], ci[:, ::-1, :]
    return jnp.concatenate(parts + [cr, ci], axis=1).astype(F32)


def _block_scan(s_ref, cst_ref, carry_ref, sd, rows, reverse):
    ns = sd.ns
    nblk = rows // SUBLANES
    w = min(SCAN_LANES, ns)
    for c0 in range(0, ns, w):
        re_l, im_l = slice(c0, c0 + w), slice(ns + c0, ns + c0 + w)
        cst = [cst_ref[k * SUBLANES:(k + 1) * SUBLANES, c0:c0 + w] for k in range(8)]

        def step(k, carry, re_l=re_l, im_l=im_l, cst=cst):
            cr, ci = carry
            blk = (nblk - 1 - k) if reverse else k
            r0 = pl.multiple_of(blk * SUBLANES, SUBLANES)
            xr = s_ref[pl.ds(r0, SUBLANES), re_l]
            xi = s_ref[pl.ds(r0, SUBLANES), im_l]
            for n, d in enumerate((1, 2, 4)):
                ar, ai = cst[2 * n], cst[2 * n + 1]
                shift = (SUBLANES - d) if reverse else d
                sr = pltpu.roll(xr, shift, 0)
                si = pltpu.roll(xi, shift, 0)
                xr, xi = xr + ar * sr - ai * si, xi + ar * si + ai * sr
            br = jnp.broadcast_to(cr, xr.shape)
            bi = jnp.broadcast_to(ci, xi.shape)
            xr, xi = xr + cst[6] * br - cst[7] * bi, xi + cst[6] * bi + cst[7] * br
            s_ref[pl.ds(r0, SUBLANES), re_l] = xr
            s_ref[pl.ds(r0, SUBLANES), im_l] = xi
            edge = slice(0, 1) if reverse else slice(SUBLANES - 1, SUBLANES)
            return xr[edge, :], xi[edge, :]

        cr, ci = lax.fori_loop(0, nblk, step, (carry_ref[0:1, re_l], carry_ref[0:1, im_l]), unroll=2)
        carry_ref[0:1, re_l] = cr
        carry_ref[0:1, im_l] = ci


def _ssm_fwd(sd, z, wb, wc, cst, d_row, tt=256):
    n_tok = z.shape[0]
    tt = _pick(n_tok, tt, SUBLANES)
    cb, ns2 = sd.cb, 2 * sd.ns

    def body(z_ref, wb_ref, wc_ref, cst_ref, d_ref, y_ref, s_ref, carry_ref):
        @pl.when(pl.program_id(1) == 0)
        def _():
            carry_ref[...] = jnp.zeros(carry_ref.shape, F32)
        u = z_ref[...]
        s_ref[...] = jnp.dot(u.astype(BF16), wb_ref[...], preferred_element_type=F32)
        _block_scan(s_ref, cst_ref, carry_ref, sd, tt, reverse=False)
        y = jnp.dot(s_ref[...].astype(BF16), wc_ref[...], preferred_element_type=F32)
        y_ref[...] = y + d_ref[...] * u

    return pl.pallas_call(
        body, name="ssm_fwd", grid=(sd.nb, n_tok // tt),
        in_specs=[pl.BlockSpec((tt, cb), lambda j, i: (i, j)),
                  pl.BlockSpec((None, cb, ns2), lambda j, i: (j, 0, 0)),
                  pl.BlockSpec((None, ns2, cb), lambda j, i: (j, 0, 0)),
                  pl.BlockSpec((None, 8 * SUBLANES, sd.ns), lambda j, i: (j, 0, 0)),
                  pl.BlockSpec((1, cb), lambda j, i: (0, j))],
        out_specs=[pl.BlockSpec((tt, cb), lambda j, i: (i, j)), pl.BlockSpec((tt, ns2), lambda j, i: (i, j))],
        out_shape=[_sds((n_tok, sd.d), F32), _sds((n_tok, sd.nb * ns2), F32)],
        scratch_shapes=[pltpu.VMEM((SUBLANES, ns2), F32)],
        compiler_params=_params(("arbitrary", "arbitrary")),
    )(z, wb, wc, cst, d_row)


def _ssm_bwd(sd, dy, z, states, wct, wbt, cst_rev, d_row, tt=256):
    n_tok = z.shape[0]
    tt = _pick(n_tok, tt, SUBLANES)
    nt = n_tok // tt
    cb, ns, ns2 = sd.cb, sd.ns, 2 * sd.ns
    blocks_per_tile = tt // SUBLANES
    tn_dims = (((0,), (0,)), ((), ()))

    def body(dy_ref, z_ref, s_ref, sp_ref, wct_ref, wbt_ref, cst_ref, d_ref,
             du_ref, dwb_ref, dwc_ref, da_ref, dd_ref, lam_ref, carry_ref):
        i = pl.program_id(1)

        @pl.when(i == 0)
        def _():
            carry_ref[...] = jnp.zeros(carry_ref.shape, F32)
            dwb_ref[...] = jnp.zeros(dwb_ref.shape, F32)
            dwc_ref[...] = jnp.zeros(dwc_ref.shape, F32)
            da_ref[...] = jnp.zeros(da_ref.shape, F32)
            dd_ref[...] = jnp.zeros(dd_ref.shape, F32)

        dy_t = dy_ref[...]
        u = z_ref[...]
        dy16 = dy_t.astype(BF16)
        lam_ref[...] = jnp.dot(dy16, wct_ref[...], preferred_element_type=F32)
        _block_scan(lam_ref, cst_ref, carry_ref, sd, tt, reverse=True)
        lam = lam_ref[...]
        lam16 = lam.astype(BF16)
        du_ref[...] = jnp.dot(lam16, wbt_ref[...], preferred_element_type=F32) + d_ref[...] * dy_t
        dd_ref[0:1, :] += jnp.sum(dy_t * u, axis=0, keepdims=True)
        dwb_ref[...] += lax.dot_general(u.astype(BF16), lam16, tn_dims, preferred_element_type=F32)
        s = s_ref[...]
        dwc_ref[...] += lax.dot_general(s.astype(BF16), dy16, tn_dims, preferred_element_type=F32)
        before = jnp.where(i == nt - 1, 0.0, 1.0) * sp_ref[SUBLANES - 1:SUBLANES, :]
        first_row = lax.broadcasted_iota(jnp.int32, s.shape, 0) == 0
        prev = jnp.where(first_row, jnp.broadcast_to(before, s.shape), pltpu.roll(s, 1, 0))
        lr, li = lam[:, :ns], lam[:, ns:]
        pr, pi_ = prev[:, :ns], prev[:, ns:]
        da_ref[0:1, 0:ns] += jnp.sum(lr * pr + li * pi_, axis=0, keepdims=True)
        da_ref[0:1, ns:ns2] += jnp.sum(li * pr - lr * pi_, axis=0, keepdims=True)

    rev = lambda i: nt - 1 - i
    return pl.pallas_call(
        body, name="ssm_bwd", grid=(sd.nb, nt),
        in_specs=[pl.BlockSpec((tt, cb), lambda j, i: (rev(i), j)),
                  pl.BlockSpec((tt, cb), lambda j, i: (rev(i), j)),
                  pl.BlockSpec((tt, ns2), lambda j, i: (rev(i), j)),
                  pl.BlockSpec((SUBLANES, ns2), lambda j, i: (jnp.maximum(rev(i) * blocks_per_tile - 1, 0), j)),
                  pl.BlockSpec((None, cb, ns2), lambda j, i: (j, 0, 0)),
                  pl.BlockSpec((None, ns2, cb), lambda j, i: (j, 0, 0)),
                  pl.BlockSpec((None, 8 * SUBLANES, ns), lambda j, i: (j, 0, 0)),
                  pl.BlockSpec((1, cb), lambda j, i: (0, j))],
        out_specs=[pl.BlockSpec((tt, cb), lambda j, i: (rev(i), j)),
                   pl.BlockSpec((None, cb, ns2), lambda j, i: (j, 0, 0)),
                   pl.BlockSpec((None, ns2, cb), lambda j, i: (j, 0, 0)),
                   pl.BlockSpec((None, SUBLANES, ns2), lambda j, i: (j, 0, 0)),
                   pl.BlockSpec((None, SUBLANES, cb), lambda j, i: (j, 0, 0))],
        out_shape=[_sds((n_tok, sd.d), F32), _sds((sd.nb, cb, ns2), F32), _sds((sd.nb, ns2, cb), F32),
                   _sds((sd.nb, SUBLANES, ns2), F32), _sds((sd.nb, SUBLANES, cb), F32)],
        scratch_shapes=[pltpu.VMEM((tt, ns2), F32), pltpu.VMEM((SUBLANES, ns2), F32)],
        compiler_params=_params(("arbitrary", "arbitrary")),
    )(dy, z, states, states, wct, wbt, cst_rev, d_row)


def _local_grads(x, p, target, wg, sp):
    n_tok, d_model = x.shape
    d_ssm = sp["ssm_d"].shape[0] * sp["ssm_d"].shape[1]
    d_sgu = sp["sgu_ln_g"].shape[-1]
    d_ffn = wg["w_ffn_out"].shape[0]
    sd = _SsmDims(sp["ssm_b_re"].shape[0], sp["ssm_b_re"].shape[1], sp["ssm_b_re"].shape[2])
    heads, chunk, _ = sp["sgu_w"].shape
    row = lambda v: v.reshape(1, -1)
    tok = lambda w, dt=F32: _sds((n_tok, w), dt)
    acc = lambda w: _sds((1, w), F32)

    g_mix = row(sp["norm_mix_g"])
    (h1,) = _rowwise("norm_mix", lambda a, g: _rms(a, g), [x], [g_mix], [tok(d_model, BF16)])
    z = _mm_nn("proj_in", h1, wg["w_in"], sharded=True)

    qr_col, qi_col, bbar_re, bbar_im, pw_re, pw_im = _ssm_forward_params(
        sd, sp["ssm_lambda_re"], sp["ssm_lambda_im"], sp["ssm_log_step"], sp["ssm_b_re"], sp["ssm_b_im"])
    wb = jnp.concatenate([_blockdiag_in(sd, bbar_re), _blockdiag_in(sd, bbar_im)], axis=2).astype(BF16)
    wc = jnp.concatenate([_blockdiag_out(sd, sp["ssm_c_re"]), -_blockdiag_out(sd, sp["ssm_c_im"])], axis=1).astype(BF16)
    d_row = row(sp["ssm_d"])
    y_pre, states = _ssm_fwd(sd, z, wb, wc, _scan_consts(sd, pw_re, pw_im, False), d_row)

    (ya0_16,) = _rowwise("ssm_gelu", lambda a: jax.nn.gelu(a), [y_pre], [], [tok(d_ssm, BF16)])
    q = _mm_nn("ssm_glu", ya0_16, wg["ssm_glu_w"])
    glu_b, g_ossm = row(sp["ssm_glu_b"]), row(sp["out_norm_ssm_g"])
    (ya_n,) = _rowwise("ssm_glu_out", _glu_out, [y_pre, q], [glu_b, g_ossm], [tok(d_ssm, BF16)])

    assert d_ssm == d_sgu
    zu, zv = _Cols(z, d_sgu, 1), _Cols(z, d_sgu, 2)
    ln_g, ln_b, g_osgu = row(sp["sgu_ln_g"]), row(sp["sgu_ln_b"]), row(sp["out_norm_sgu_g"])
    b_st = sp["sgu_b"].T
    sgu_tr = 2 * chunk
    (yb_n,) = _rowwise("sgu", _sgu_rows, [zu, zv], [ln_g, ln_b, sp["sgu_w"], b_st, g_osgu], [tok(d_sgu, BF16)], tr=sgu_tr)

    ycat = jnp.concatenate([ya_n, yb_n], axis=1)
    x1 = _mm_nn("proj_out", ycat, wg["w_out"], res=x)

    g_ffn = row(sp["norm_ffn_g"])
    (h2,) = _rowwise("norm_ffn", lambda a, g: _rms(a, g), [x1], [g_ffn], [tok(d_model, BF16)])
    gu = _mm_nn("ffn_in", h2, wg["w_ffn_in"], sharded=True)
    gate_c, up_c = _Cols(gu, d_ffn, 0), _Cols(gu, d_ffn, 1)
    (act,) = _rowwise("swiglu", _swiglu, [gate_c, up_c], [], [tok(d_ffn, BF16)], tr=128)
    x2 = _mm_nn("ffn_out", act, wg["w_ffn_out"], res=x1)

    g_ple = row(sp["norm_ple_g"])
    (h3,) = _rowwise("norm_ple", lambda a, g: _rms(a, g), [x2], [g_ple], [tok(d_model, BF16)])
    gpre = _mm_nn("ple_gate", h3, wg["w_ple_gate"])
    (p16,) = _rowwise("ple_cast", lambda a: a, [p], [], [tok(p.shape[1], BF16)])
    pp = _mm_nn("ple_proj", p16, wg["w_ple_proj"], sharded=True)

    b_g, g_fin = row(sp["b_ple_gate"]), row(sp["final_norm_g"])

    def head(x2_t, gpre_t, pp_t, tgt_t, b_g_v, g_fin_v):
        loss, grads = jax.value_and_grad(_head_loss, argnums=(0, 1, 2, 3, 4))(x2_t, gpre_t, pp_t, b_g_v, g_fin_v, tgt_t)
        dx2, dgpre, dpp, db, dg = grads
        return dx2, dgpre.astype(BF16), dpp.astype(BF16), jnp.full((1, LANES), loss, F32), db, dg

    dx2_head, dgpre16, dpp16, loss_row, d_b_g, d_g_fin = _rowwise(
        "head", head, [x2, gpre, pp, target], [b_g, g_fin],
        [tok(d_model), tok(d_model, BF16), tok(d_model, BF16)], [acc(LANES), acc(d_model), acc(d_model)])
    loss = loss_row[0, 0]

    grads = {}
    grads["w_ple_proj"] = _mm_tn("d_ple_proj", p16, dpp16, shards=N_CHIPS, tk=256)
    grads["w_ple_gate"] = _mm_tn("d_ple_gate", h3, dgpre16)
    dh3 = _mm_nt("d_h3", dgpre16, wg["w_ple_gate"])

    def norm_bwd(x_t, dres_t, dh_t, g_v):
        _, vjp = jax.vjp(_rms, x_t, g_v)
        dx, dg = vjp(dh_t)
        dx = dres_t + dx
        return dx, dx.astype(BF16), dg

    dx2, dx2_16, d_g_ple = _rowwise("d_norm_ple", norm_bwd, [x2, dx2_head, dh3], [g_ple],
                                    [tok(d_model), tok(d_model, BF16)], [acc(d_model)])
    grads["w_ffn_out"] = _mm_tn("d_ffn_out", act, dx2_16)
    dact = _mm_nt("d_act", dx2_16, wg["w_ffn_out"])

    def swiglu_bwd(gate_t, up_t, dact_t):
        _, vjp = jax.vjp(_swiglu, gate_t, up_t)
        dg, du = vjp(dact_t)
        return jnp.concatenate([dg, du], axis=1)

    (dgu16,) = _rowwise("d_swiglu", swiglu_bwd, [gate_c, up_c, dact], [], [tok(2 * d_ffn, BF16)], tr=128)
    grads["w_ffn_in"] = _mm_tn("d_ffn_in", h2, dgu16, shards=N_CHIPS)
    dh2 = _mm_nt("d_h2", dgu16, wg["w_ffn_in"], sharded=True, tm=256)
    dx1, dx1_16, d_g_ffn = _rowwise("d_norm_ffn", norm_bwd, [x1, dx2, dh2], [g_ffn],
                                    [tok(d_model), tok(d_model, BF16)], [acc(d_model)])
    grads["w_out"] = _mm_tn("d_proj_out", ycat, dx1_16)
    dycat = _mm_nt("d_ycat", dx1_16, wg["w_out"])

    def glu_out_bwd(y_pre_t, q_t, dy_t, glu_b_v, g_v):
        _, vjp = jax.vjp(_glu_out, y_pre_t, q_t, glu_b_v, g_v)
        dy_pre, dq, db, dg = vjp(dy_t)
        return dy_pre, dq.astype(BF16), db, dg

    dy_pre_a, dq16, d_glu_b, d_g_ossm = _rowwise(
        "d_ssm_glu_out", glu_out_bwd, [y_pre, q, _Cols(dycat, d_ssm, 0)], [glu_b, g_ossm],
        [tok(d_ssm), tok(d_ssm, BF16)], [acc(d_ssm), acc(d_ssm)])
    grads["ssm_glu_w"] = _mm_tn("d_ssm_glu", ya0_16, dq16)
    dya0 = _mm_nt("d_ya0", dq16, wg["ssm_glu_w"])

    def gelu_bwd(y_pre_t, dy_a_t, dya0_t):
        _, vjp = jax.vjp(jax.nn.gelu, y_pre_t)
        return dy_a_t + vjp(dya0_t)[0]

    (dy_pre,) = _rowwise("d_ssm_gelu", gelu_bwd, [y_pre, dy_pre_a, dya0], [], [tok(d_ssm)])

    wct, wbt = jnp.swapaxes(wc, 1, 2), jnp.swapaxes(wb, 1, 2)
    dz_ssm, dwb, dwc, da, dd = _ssm_bwd(sd, dy_pre, z, states, wct, wbt, _scan_consts(sd, pw_re, pw_im, True), d_row)

    def sgu_bwd(zu_t, zv_t, dy_t, ln_g_v, ln_b_v, w_v, b_v, g_v):
        _, vjp = jax.vjp(_sgu_rows, zu_t, zv_t, ln_g_v, ln_b_v, w_v, b_v, g_v)
        dzu, dzv, dlg, dlb, dw, db, dg = vjp(dy_t)
        return dzu, dzv, dlg, dlb, dw, db, dg

    dzu, dzv, d_ln_g, d_ln_b, d_sgu_w, d_b_st, d_g_osgu = _rowwise(
        "d_sgu", sgu_bwd, [zu, zv, _Cols(dycat, d_sgu, 1)], [ln_g, ln_b, sp["sgu_w"], b_st, g_osgu],
        [tok(d_sgu, BF16), tok(d_sgu, BF16)],
        [acc(d_sgu), acc(d_sgu), _sds(sp["sgu_w"].shape, F32), _sds(b_st.shape, F32), acc(d_sgu)], tr=sgu_tr)

    (dz_ssm16,) = _rowwise("d_ssm_cast", lambda a: a, [dz_ssm], [], [tok(d_ssm, BF16)])
    dz16 = jnp.concatenate([dz_ssm16, dzu, dzv], axis=1)
    grads["w_in"] = _mm_tn("d_proj_in", h1, dz16, shards=N_CHIPS)
    dh1 = _mm_nt("d_h1", dz16, wg["w_in"], sharded=True)

    def norm_in_bwd(x_t, dres_t, dh_t, g_v):
        _, vjp = jax.vjp(_rms, x_t, g_v)
        dx, dg = vjp(dh_t)
        return dres_t + dx, dg

    grad_x, d_g_mix = _rowwise("d_norm_mix", norm_in_bwd, [x, dx1, dh1], [g_mix], [tok(d_model)], [acc(d_model)])

    gp = sd.g * sd.p
    dbbar_re = _diag_in(sd, dwb[:, :, :sd.ns])
    dbbar_im = _diag_in(sd, dwb[:, :, sd.ns:])
    d_c_re = _diag_out(sd, dwc[:, :sd.ns, :])
    d_c_im = -_diag_out(sd, dwc[:, sd.ns:, :])
    dabar_re = da[:, 0, :sd.ns].reshape(sd.g, sd.p)
    dabar_im = da[:, 0, sd.ns:].reshape(sd.g, sd.p)
    d_ssm_d = dd[:, 0, :].reshape(sd.g, sd.h)

    def bbar_bwd(qr, qi, b_re_v, b_im_v, dre, dim):
        _, vjp = jax.vjp(_ssm_bbar, qr, qi, b_re_v, b_im_v)
        return vjp((dre, dim))

    b_re2, b_im2 = sp["ssm_b_re"].reshape(gp, sd.h), sp["ssm_b_im"].reshape(gp, sd.h)
    dq_re, dq_im, d_b_re, d_b_im = _whole(
        "d_ssm_bbar", bbar_bwd, [qr_col, qi_col, b_re2, b_im2, dbbar_re, dbbar_im],
        [_sds((gp, 1), F32), _sds((gp, 1), F32), _sds((gp, sd.h), F32), _sds((gp, sd.h), F32)])

    def disc_bwd(lr, li, ls, dar, dai, dqr, dqi):
        _, vjp = jax.vjp(_ssm_disc, lr, li, ls)
        return vjp((dar, dai, dqr, dqi))

    gp_s = _sds((sd.g, sd.p), F32)
    d_lam_re, d_lam_im, d_log_step = _whole(
        "d_ssm_disc", disc_bwd,
        [sp["ssm_lambda_re"], sp["ssm_lambda_im"], sp["ssm_log_step"].reshape(sd.g, 1),
         dabar_re, dabar_im, dq_re.reshape(sd.g, sd.p), dq_im.reshape(sd.g, sd.p)],
        [gp_s, gp_s, _sds((sd.g, 1), F32)])

    small = {
        "norm_mix_g": d_g_mix, "ssm_lambda_re": d_lam_re, "ssm_lambda_im": d_lam_im, "ssm_log_step": d_log_step,
        "ssm_b_re": d_b_re, "ssm_b_im": d_b_im, "ssm_c_re": d_c_re, "ssm_c_im": d_c_im, "ssm_d": d_ssm_d,
        "ssm_glu_b": d_glu_b, "sgu_ln_g": d_ln_g, "sgu_ln_b": d_ln_b, "sgu_w": d_sgu_w, "sgu_b": d_b_st.T,
        "out_norm_ssm_g": d_g_ossm, "out_norm_sgu_g": d_g_osgu, "norm_ffn_g": d_g_ffn, "norm_ple_g": d_g_ple,
        "b_ple_gate": d_b_g, "final_norm_g": d_g_fin,
    }
    return loss, grad_x, grads, small


def _place():
    x, y, c = lax.axis_index("x"), lax.axis_index("y"), lax.axis_index("c")
    chips = [(1 - x, y), (x, 1 - y), (1 - x, 1 - y)]
    return x, y, c, chips


ANY = pl.BlockSpec(memory_space=pl.ANY)


def _cast_into_slot(name, w2d, shard, tr=256):
    rows, cols = w2d.shape
    rh = rows // 2
    tr = _pick(rh, tr, 16)
    per = rh // tr

    def body(s_ref, a_ref, o_ref):
        o_ref[...] = a_ref[...].astype(BF16)

    grid_spec = pltpu.PrefetchScalarGridSpec(
        num_scalar_prefetch=1, grid=(2, per),
        in_specs=[pl.BlockSpec((tr, cols), lambda h, i, s_ref: (h * per + i, 0))],
        out_specs=pl.BlockSpec((None, None, tr, cols), lambda h, i, s_ref: (s_ref[0], h, i, 0)))
    return pl.pallas_call(body, name=name, grid_spec=grid_spec, out_shape=_sds((N_CHIPS, 2, rh, cols), BF16),
                          compiler_params=_params(("arbitrary", "arbitrary")))(shard.reshape(1).astype(jnp.int32), w2d)


def _allgather_weights(slots):
    n = len(slots)

    def body(*refs):
        outs = refs[n:2 * n]
        send_sems, recv_sems = refs[2 * n:]
        x, y, c, chips = _place()
        sibling = (x, y, 1 - c)
        mine = 2 * x + y

        def remote(k, src, dst, to):
            return pltpu.make_async_remote_copy(src_ref=src, dst_ref=dst, send_sem=send_sems.at[k], recv_sem=recv_sems.at[k],
                                                device_id=to, device_id_type=MESH)

        sends = []
        for w in range(n):
            own = outs[w].at[mine, c]
            for j, chip in enumerate(chips):
                sends.append(remote(3 * w + j, own, own, (*chip, c)))
        for cp in sends:
            cp.start()
        passed = []
        for w in range(n):
            for j, (cx, cy) in enumerate(chips):
                theirs = outs[w].at[2 * cx + cy, c]
                remote(3 * w + j, theirs, theirs, (x, y, c)).wait_recv()
                fwd = remote(3 * n + 3 * w + j, theirs, theirs, sibling)
                fwd.start()
                passed.append(fwd)
        for w in range(n):
            for j, (cx, cy) in enumerate(chips):
                theirs = outs[w].at[2 * cx + cy, 1 - c]
                remote(3 * n + 3 * w + j, theirs, theirs, (x, y, c)).wait_recv()
        for cp in sends + passed:
            cp.wait_send()

    return pl.pallas_call(
        body, name="allgather_weights", in_specs=[ANY] * n, out_specs=[ANY] * n,
        out_shape=[_sds(s.shape, s.dtype) for s in slots], input_output_aliases={w: w for w in range(n)},
        scratch_shapes=[pltpu.SemaphoreType.DMA((6 * n,)), pltpu.SemaphoreType.DMA((6 * n,))],
    )(*slots)


def _swap_halves(grads):
    n = len(grads)

    def body(*refs):
        ins, outs = refs[:n], refs[n:2 * n]
        send_sems, recv_sems = refs[2 * n:]
        x, y, c, _ = _place()
        copies = [pltpu.make_async_remote_copy(src_ref=ins[w].at[:, 1 - c], dst_ref=outs[w], send_sem=send_sems.at[w],
                                               recv_sem=recv_sems.at[w], device_id=(x, y, 1 - c), device_id_type=MESH)
                  for w in range(n)]
        for cp in copies:
            cp.start()
        for cp in copies:
            cp.wait()

    return pl.pallas_call(
        body, name="grad_swap_halves", in_specs=[ANY] * n, out_specs=[ANY] * n,
        out_shape=[_sds((g.shape[0], *g.shape[2:]), g.dtype) for g in grads],
        scratch_shapes=[pltpu.SemaphoreType.DMA((n,)), pltpu.SemaphoreType.DMA((n,))],
    )(*grads)


def _scatter_quarters(halves):
    n = len(halves)

    def body(*refs):
        ins, outs = refs[:n], refs[n:2 * n]
        send_sems, recv_sems = refs[2 * n:]
        x, y, c, chips = _place()
        copies = []
        for w in range(n):
            for j, (cx, cy) in enumerate(chips):
                copies.append(pltpu.make_async_remote_copy(
                    src_ref=ins[w].at[2 * cx + cy], dst_ref=outs[w].at[j], send_sem=send_sems.at[3 * w + j],
                    recv_sem=recv_sems.at[3 * w + j], device_id=(cx, cy, c), device_id_type=MESH))
        for cp in copies:
            cp.start()
        for cp in copies:
            cp.wait()

    return pl.pallas_call(
        body, name="grad_scatter", in_specs=[ANY] * n, out_specs=[ANY] * n,
        out_shape=[_sds((3, *h.shape[1:]), h.dtype) for h in halves],
        scratch_shapes=[pltpu.SemaphoreType.DMA((3 * n,)), pltpu.SemaphoreType.DMA((3 * n,))],
    )(*halves)


def _join_halves(slots):
    n = len(slots)

    def body(*refs):
        outs = refs[n:2 * n]
        send_sems, recv_sems = refs[2 * n:]
        x, y, c, _ = _place()

        def copy(w, half, to):
            return pltpu.make_async_remote_copy(src_ref=outs[w].at[half], dst_ref=outs[w].at[half], send_sem=send_sems.at[w],
                                                recv_sem=recv_sems.at[w], device_id=to, device_id_type=MESH)

        copies = [copy(w, c, (x, y, 1 - c)) for w in range(n)]
        for cp in copies:
            cp.start()
        for w in range(n):
            copy(w, 1 - c, (x, y, c)).wait_recv()
        for cp in copies:
            cp.wait_send()

    return pl.pallas_call(
        body, name="grad_join_halves", in_specs=[ANY] * n, out_specs=[ANY] * n,
        out_shape=[_sds(s.shape, s.dtype) for s in slots], input_output_aliases={w: w for w in range(n)},
        scratch_shapes=[pltpu.SemaphoreType.DMA((n,)), pltpu.SemaphoreType.DMA((n,))],
    )(*slots)


def _allreduce_small(block, tr=256):
    rows, lanes = block.shape
    tr = _pick(rows, tr, SUBLANES)

    def body(x_ref, o_ref, buf, send_sems, recv_sems):
        x, y, c, chips = _place()
        me, sibling = (x, y, c), (x, y, 1 - c)

        def slot(px, py, pc):
            return buf.at[4 * px + 2 * py + pc]

        def copy(k, block_of, to):
            return pltpu.make_async_remote_copy(src_ref=slot(*block_of), dst_ref=slot(*block_of), send_sem=send_sems.at[k],
                                                recv_sem=recv_sems.at[k], device_id=to, device_id_type=MESH)

        slot(*me)[...] = x_ref[...]
        first = [copy(0, me, sibling)] + [copy(1 + j, me, (*chip, c)) for j, chip in enumerate(chips)]
        for cp in first:
            cp.start()
        passed = [copy(4 + j, (*chip, c), sibling) for j, chip in enumerate(chips)]
        for j, chip in enumerate(chips):
            copy(1 + j, (*chip, c), me).wait_recv()
            passed[j].start()
        copy(0, sibling, me).wait_recv()
        for j, chip in enumerate(chips):
            copy(4 + j, (*chip, 1 - c), me).wait_recv()
        for cp in first + passed:
            cp.wait_send()
        for r0 in range(0, rows, tr):
            acc = buf[0, r0:r0 + tr, :]
            for k in range(1, N_DEV):
                acc = acc + buf[k, r0:r0 + tr, :]
            o_ref[r0:r0 + tr, :] = acc

    vm = pl.BlockSpec(memory_space=pltpu.VMEM)
    return pl.pallas_call(
        body, name="allreduce_small", in_specs=[vm], out_specs=vm, out_shape=_sds((rows, lanes), block.dtype),
        scratch_shapes=[pltpu.VMEM((N_DEV, rows, lanes), block.dtype), pltpu.SemaphoreType.DMA((7,)), pltpu.SemaphoreType.DMA((7,))],
        compiler_params=pltpu.CompilerParams(vmem_limit_bytes=VMEM_LIMIT),
    )(block)


def _sum_received(name, own, received, c, tr=256):
    n, rows, cols = received.shape
    tr = _pick(rows, tr, 16)

    def body(c_ref, a_ref, s_ref, o_ref):
        acc = a_ref[...]
        for k in range(n):
            acc = acc + s_ref[k].astype(F32)
        o_ref[...] = acc

    grid_spec = pltpu.PrefetchScalarGridSpec(
        num_scalar_prefetch=1, grid=(rows // tr,),
        in_specs=[pl.BlockSpec((tr, cols), lambda i, c_ref: (i, 0)), pl.BlockSpec((n, tr, cols), lambda i, c_ref: (0, i, 0))],
        out_specs=pl.BlockSpec((None, tr, cols), lambda i, c_ref: (c_ref[0], i, 0)))
    return pl.pallas_call(body, name=name, grid_spec=grid_spec, out_shape=_sds((2, rows, cols), F32),
                          compiler_params=_params(("arbitrary",)))(c.reshape(1).astype(jnp.int32), own, received)


def _add_halves(name, full, c, received, tr=256):
    s, _, rh, cols = full.shape
    tr = _pick(rh, tr, 16)

    def body(c_ref, a_ref, b_ref, o_ref):
        o_ref[...] = (a_ref[...] + b_ref[...]).astype(BF16)

    grid_spec = pltpu.PrefetchScalarGridSpec(
        num_scalar_prefetch=1, grid=(s, rh // tr),
        in_specs=[pl.BlockSpec((None, None, tr, cols), lambda q, i, c_ref: (q, c_ref[0], i, 0)),
                  pl.BlockSpec((None, tr, cols), lambda q, i, c_ref: (q, i, 0))],
        out_specs=pl.BlockSpec((None, tr, cols), lambda q, i, c_ref: (q, i, 0)))
    return pl.pallas_call(body, name=name, grid_spec=grid_spec, out_shape=_sds((s, rh, cols), BF16),
                          compiler_params=_params(("arbitrary", "arbitrary")))(c.reshape(1).astype(jnp.int32), full, received)


def _own_half(name, full, c, shard, received, tr=256):
    _, _, rh, cols = full.shape
    tr = _pick(rh, tr, SUBLANES)

    def body(i_ref, a_ref, b_ref, o_ref):
        o_ref[...] = a_ref[...] + b_ref[...]

    grid_spec = pltpu.PrefetchScalarGridSpec(
        num_scalar_prefetch=1, grid=(rh // tr,),
        in_specs=[pl.BlockSpec((None, None, tr, cols), lambda i, i_ref: (i_ref[1], i_ref[0], i, 0)),
                  pl.BlockSpec((None, tr, cols), lambda i, i_ref: (i_ref[1], i, 0))],
        out_specs=pl.BlockSpec((tr, cols), lambda i, i_ref: (i, 0)))
    return pl.pallas_call(body, name=name, grid_spec=grid_spec, out_shape=_sds((rh, cols), F32),
                          compiler_params=_params(("arbitrary",)))(jnp.stack([c, shard]).astype(jnp.int32), full, received)


LARGE = ("w_in", "ssm_glu_w", "w_out", "w_ffn_in", "w_ffn_out", "w_ple_gate", "w_ple_proj")
COLUMN_SHARDED = ("w_in", "w_ffn_in", "w_ple_proj")
SMALL = ("norm_mix_g", "ssm_lambda_re", "ssm_lambda_im", "ssm_log_step", "ssm_b_re", "ssm_b_im", "ssm_c_re", "ssm_c_im",
         "ssm_d", "ssm_glu_b", "sgu_ln_g", "sgu_ln_b", "sgu_w", "sgu_b", "out_norm_ssm_g", "out_norm_sgu_g", "norm_ffn_g",
         "norm_ple_g", "b_ple_gate", "final_norm_g")
WEIGHTS = ("norm_mix_g", "w_in", "ssm_lambda_re", "ssm_lambda_im", "ssm_log_step", "ssm_b_re", "ssm_b_im", "ssm_c_re",
           "ssm_c_im", "ssm_d", "ssm_glu_w", "ssm_glu_b", "sgu_ln_g", "sgu_ln_b", "sgu_w", "sgu_b", "out_norm_ssm_g",
           "out_norm_sgu_g", "w_out", "norm_ffn_g", "w_ffn_in", "w_ffn_out", "norm_ple_g", "w_ple_gate", "b_ple_gate",
           "w_ple_proj", "final_norm_g")
PACK_ROWS = SUBLANES * LANES


def _pack(arrays):
    parts = []
    for a in arrays:
        flat = a.reshape(-1).astype(F32)
        pad = -flat.shape[0] % PACK_ROWS
        parts.append(jnp.pad(flat, (0, pad)) if pad else flat)
    return jnp.concatenate(parts).reshape(-1, LANES)


def _unpack(packed, like):
    flat = packed.reshape(-1)
    out, at = [], 0
    for a in like:
        size = a.size
        out.append(flat[at:at + size].reshape(a.shape))
        at += size + (-size % PACK_ROWS)
    return out


def _reduce_large(grads, c, shard):
    names = list(grads)
    full = [grads[k].reshape(N_CHIPS, 2, grads[k].shape[1] // 2, grads[k].shape[2]) for k in names]
    received = _swap_halves(full)
    halves = [_add_halves("grad_add_halves_" + k, f, c, r) for k, f, r in zip(names, full, received)]
    quarters = _scatter_quarters(halves)
    own = [_own_half("grad_own_" + k, f, c, shard, r) for k, f, r in zip(names, full, received)]
    parts = [_sum_received("grad_sum_" + k, o, q, c) for k, q, o in zip(names, quarters, own)]
    joined = _join_halves(parts)
    return {k: j.reshape(2 * j.shape[1], j.shape[2]) for k, j in zip(names, joined)}


def kernel(x, p, norm_mix_g, w_in, ssm_lambda_re, ssm_lambda_im, ssm_log_step, ssm_b_re, ssm_b_im, ssm_c_re, ssm_c_im, ssm_d, ssm_glu_w, ssm_glu_b, sgu_ln_g, sgu_ln_b, sgu_w, sgu_b, out_norm_ssm_g, out_norm_sgu_g, w_out, norm_ffn_g, w_ffn_in, w_ffn_out, norm_ple_g, w_ple_gate, b_ple_gate, w_ple_proj, final_norm_g, loss_target, m_norm_mix_g, m_w_in, m_ssm_lambda_re, m_ssm_lambda_im, m_ssm_log_step, m_ssm_b_re, m_ssm_b_im, m_ssm_c_re, m_ssm_c_im, m_ssm_d, m_ssm_glu_w, m_ssm_glu_b, m_sgu_ln_g, m_sgu_ln_b, m_sgu_w, m_sgu_b, m_out_norm_ssm_g, m_out_norm_sgu_g, m_w_out, m_norm_ffn_g, m_w_ffn_in, m_w_ffn_out, m_norm_ple_g, m_w_ple_gate, m_b_ple_gate, m_w_ple_proj, m_final_norm_g, v_norm_mix_g, v_w_in, v_ssm_lambda_re, v_ssm_lambda_im, v_ssm_log_step, v_ssm_b_re, v_ssm_b_im, v_ssm_c_re, v_ssm_c_im, v_ssm_d, v_ssm_glu_w, v_ssm_glu_b, v_sgu_ln_g, v_sgu_ln_b, v_sgu_w, v_sgu_b, v_out_norm_ssm_g, v_out_norm_sgu_g, v_w_out, v_norm_ffn_g, v_w_ffn_in, v_w_ffn_out, v_norm_ple_g, v_w_ple_gate, v_b_ple_gate, v_w_ple_proj, v_final_norm_g):
    given = dict(locals())
    w = {k: given[k] for k in WEIGHTS}
    m = {k: given["m_" + k] for k in WEIGHTS}
    v = {k: given["v_" + k] for k in WEIGHTS}
    c = lax.axis_index("c")
    shard = 2 * lax.axis_index("x") + lax.axis_index("y")

    gathered = _allgather_weights([_cast_into_slot("cast_" + k, w[k].reshape(w[k].shape[1:]), shard) for k in LARGE])
    wg = {}
    for k, g in zip(LARGE, gathered):
        _, _, rh, cols = g.shape
        wg[k] = g.reshape(N_CHIPS, 2 * rh, cols) if k in COLUMN_SHARDED else g.reshape(N_CHIPS * 2 * rh, cols)

    unlayer = lambda a: a if a.ndim == 1 else a[0]
    sp = {k: unlayer(w[k]) for k in SMALL}
    n_tok, d_model = x.shape[1:]
    loss, grad_x, grads, small = _local_grads(x.reshape(n_tok, d_model), p.reshape(n_tok, p.shape[-1]),
                                              loss_target.reshape(n_tok, d_model), wg, sp)
    loss = lax.psum(loss, ("x", "y", "c"))

    shard_major = {}
    for k in LARGE:
        g = grads[k]
        shard_major[k] = g if k in COLUMN_SHARDED else g.reshape(N_CHIPS, g.shape[0] // N_CHIPS, g.shape[1])
    reduced = _reduce_large(shard_major, c, shard)

    grad_w, delta_w, new_m, new_v = {}, {}, {}, {}
    for k in LARGE:
        shape = w[k].shape
        two_d = lambda a: a.reshape(shape[1:])
        like = _sds(shape[1:], F32)
        d_k, m_k, v_k = _rowwise("adamw_" + k, _adamw, [two_d(w[k]), reduced[k], two_d(m[k]), two_d(v[k])], [], [like, like, like])
        grad_w[k], delta_w[k], new_m[k], new_v[k] = (a.reshape(shape) for a in (reduced[k], d_k, m_k, v_k))

    packed_g = _allreduce_small(_pack([small[k].reshape(w[k].shape) for k in SMALL]))
    like = _sds(packed_g.shape, F32)
    d_s, m_s, v_s = _rowwise("adamw_small", _adamw, [_pack([w[k] for k in SMALL]), packed_g, _pack([m[k] for k in SMALL]),
                                                     _pack([v[k] for k in SMALL])], [], [like, like, like])
    shapes = [w[k] for k in SMALL]
    for k, g_k, d_k, m_k, v_k in zip(SMALL, _unpack(packed_g, shapes), _unpack(d_s, shapes), _unpack(m_s, shapes), _unpack(v_s, shapes)):
        grad_w[k], delta_w[k], new_m[k], new_v[k] = g_k, d_k, m_k, v_k

    return (loss, grad_x.reshape(x.shape), *[grad_w[k] for k in WEIGHTS], *[delta_w[k] for k in WEIGHTS],
            *[new_m[k] for k in WEIGHTS], *[new_v[k] for k in WEIGHTS])
```

```python
import functools

import jax
import jax.numpy as jnp
from jax import lax
from jax.experimental import pallas as pl
from jax.experimental.pallas import tpu as pltpu

F32 = jnp.float32
BF16 = jnp.bfloat16

EPS = 1e-6
LAMBDA_RE_MAX = -1e-4
ADAM_LR = 0.001
ADAM_B1 = 0.9
ADAM_B2 = 0.999
ADAM_EPS = 1e-08
ADAM_WD = 0.01
ADAM_STEP = 10

N_CHIPS = 4
N_DEV = 8
SUBLANES = 8
LANES = 128
SSM_CH_BLOCK = 256
SCAN_LANES = 256
SCAN_BLOCKS = 2
VMEM_LIMIT = 56 * 1024 * 1024

MESH = pl.DeviceIdType.MESH


def _pick(n, pref, mult):
    if n <= pref:
        return n
    t = (pref // mult) * mult
    while t >= mult:
        if n % t == 0:
            return t
        t -= mult
    return n


def _params(semantics):
    return pltpu.CompilerParams(dimension_semantics=semantics, vmem_limit_bytes=VMEM_LIMIT)


class _Cols:
    def __init__(self, arr, width, blk):
        self.arr, self.width, self.blk = arr, width, blk


def _sds(shape, dtype):
    return jax.ShapeDtypeStruct(tuple(shape), dtype)


def _rowwise(name, fn, rows, params, row_outs, acc_outs=(), tr=256):
    rows = [r if isinstance(r, _Cols) else _Cols(r, r.shape[1], 0) for r in rows]
    m = rows[0].arr.shape[0]
    tr = _pick(m, tr, 16)
    n_in = len(rows) + len(params)
    n_ro = len(row_outs)

    def body(*refs):
        vals = fn(*[r[...] for r in refs[:n_in]])
        if not isinstance(vals, (tuple, list)):
            vals = (vals,)
        outs = refs[n_in:]
        for r, v in zip(outs[:n_ro], vals[:n_ro]):
            r[...] = v.astype(r.dtype)
        first = pl.program_id(0) == 0
        for r, v in zip(outs[n_ro:], vals[n_ro:]):
            @pl.when(first)
            def _():
                r[...] = jnp.zeros(r.shape, r.dtype)
            r[...] += v.astype(r.dtype).reshape(r.shape)

    in_specs = [pl.BlockSpec((tr, r.width), lambda i, b=r.blk: (i, b)) for r in rows]
    in_specs += [pl.BlockSpec(p.shape, lambda i, nd=p.ndim: (0,) * nd) for p in params]
    out_specs = [pl.BlockSpec((tr, o.shape[1]), lambda i: (i, 0)) for o in row_outs]
    out_specs += [pl.BlockSpec(o.shape, lambda i, nd=len(o.shape): (0,) * nd) for o in acc_outs]
    outs = pl.pallas_call(
        body, name=name, grid=(m // tr,), in_specs=in_specs, out_specs=out_specs,
        out_shape=[*row_outs, *acc_outs], compiler_params=_params(("arbitrary",)),
    )(*[r.arr for r in rows], *params)
    return outs


def _whole(name, fn, ins, outs):
    n_in = len(ins)

    def body(*refs):
        vals = fn(*[r[...] for r in refs[:n_in]])
        if not isinstance(vals, (tuple, list)):
            vals = (vals,)
        for r, v in zip(refs[n_in:], vals):
            r[...] = v.astype(r.dtype).reshape(r.shape)

    vm = pl.BlockSpec(memory_space=pltpu.VMEM)
    return pl.pallas_call(body, name=name, in_specs=[vm] * n_in, out_specs=[vm] * len(outs), out_shape=list(outs),
                          compiler_params=pltpu.CompilerParams(vmem_limit_bytes=VMEM_LIMIT))(*ins)


def _grid_order(swap):
    if not swap:
        return (lambda grid: grid), (lambda f: f)
    return (lambda grid: grid[::-1]), (lambda f: (lambda j, i: f(i, j)))


def _mm_nn(name, a, w, *, sharded=False, res=None, out_dtype=F32, tm=512, tn=512, w_resident=False):
    m, k = a.shape
    tm = _pick(m, tm, 16)
    order, ix = _grid_order(w_resident)
    if sharded:
        s, _, ns = w.shape
        n = s * ns
        tn = _pick(ns, tn, LANES)
        per = ns // tn
        w_spec = pl.BlockSpec((None, k, tn), ix(lambda i, j: (j // per, 0, j % per)))
    else:
        n = w.shape[1]
        tn = _pick(n, tn, LANES)
        w_spec = pl.BlockSpec((k, tn), ix(lambda i, j: (0, j)))

    def body(a_ref, w_ref, *rest):
        acc = jnp.dot(a_ref[...], w_ref[...], preferred_element_type=F32)
        if res is not None:
            acc = acc + rest[0][...]
        rest[-1][...] = acc.astype(out_dtype)

    in_specs = [pl.BlockSpec((tm, k), ix(lambda i, j: (i, 0))), w_spec]
    ops = [a, w]
    if res is not None:
        in_specs.append(pl.BlockSpec((tm, tn), ix(lambda i, j: (i, j))))
        ops.append(res)
    return pl.pallas_call(
        body, name=name, grid=order((m // tm, n // tn)), in_specs=in_specs,
        out_specs=pl.BlockSpec((tm, tn), ix(lambda i, j: (i, j))), out_shape=_sds((m, n), out_dtype),
        compiler_params=_params(("arbitrary", "arbitrary")),
    )(*ops)


def _mm_nt(name, g, w, *, sharded=False, tm=512, tk=512, w_resident=False):
    m, n = g.shape
    tm = _pick(m, tm, 16)
    order, ix = _grid_order(w_resident)
    dims = (((1,), (1,)), ((), ()))
    if sharded:
        s, k, ns = w.shape
        tk = _pick(k, tk, LANES)
        w_spec = pl.BlockSpec((s, tk, ns), ix(lambda i, j: (0, j, 0)))

        def body(g_ref, w_ref, o_ref):
            acc = lax.dot_general(g_ref[:, 0:ns], w_ref[0], dims, preferred_element_type=F32)
            for q in range(1, s):
                acc = acc + lax.dot_general(g_ref[:, q * ns:(q + 1) * ns], w_ref[q], dims, preferred_element_type=F32)
            o_ref[...] = acc
    else:
        k = w.shape[0]
        tk = _pick(k, tk, LANES)
        w_spec = pl.BlockSpec((tk, n), ix(lambda i, j: (j, 0)))

        def body(g_ref, w_ref, o_ref):
            o_ref[...] = lax.dot_general(g_ref[...], w_ref[...], dims, preferred_element_type=F32)

    return pl.pallas_call(
        body, name=name, grid=order((m // tm, k // tk)), in_specs=[pl.BlockSpec((tm, n), ix(lambda i, j: (i, 0))), w_spec],
        out_specs=pl.BlockSpec((tm, tk), ix(lambda i, j: (i, j))), out_shape=_sds((m, k), F32),
        compiler_params=_params(("arbitrary", "arbitrary")),
    )(g, w)


def _mm_tn(name, a, g, *, shards=0, tk=512, tn=512, g_resident=False):
    m, k = a.shape
    n = g.shape[1]
    tk = _pick(k, tk, LANES)
    order, ix = _grid_order(g_resident)
    dims = (((0,), (0,)), ((), ()))
    if shards:
        ns = n // shards
        tn = _pick(ns, tn, LANES)
        per = ns // tn
        out_spec = pl.BlockSpec((None, tk, tn), ix(lambda i, j: (j // per, i, j % per)))
        out_shape = _sds((shards, k, ns), F32)
    else:
        tn = _pick(n, tn, LANES)
        out_spec = pl.BlockSpec((tk, tn), ix(lambda i, j: (i, j)))
        out_shape = _sds((k, n), F32)

    def body(a_ref, g_ref, o_ref):
        o_ref[...] = lax.dot_general(a_ref[...], g_ref[...], dims, preferred_element_type=F32)

    return pl.pallas_call(
        body, name=name, grid=order((k // tk, n // tn)),
        in_specs=[pl.BlockSpec((m, tk), ix(lambda i, j: (0, i))), pl.BlockSpec((m, tn), ix(lambda i, j: (0, j)))],
        out_specs=out_spec, out_shape=out_shape, compiler_params=_params(("arbitrary", "arbitrary")),
    )(a, g)


def _rms(x, g):
    r = lax.rsqrt(jnp.mean(x * x, axis=-1, keepdims=True) + EPS)
    return (x * r) * g


def _glu_out(y_pre, q, glu_b, g_norm):
    ya0 = jax.nn.gelu(y_pre)
    return _rms(ya0 * jax.nn.sigmoid(q + glu_b), g_norm)


def _sgu_rows(zu, zv, ln_g, ln_b, w_s, b_st, g_norm):
    heads, t, _ = w_s.shape
    hd = zu.shape[1] // heads
    uu = jax.nn.gelu(zu)
    vv = jax.nn.gelu(zv)
    mu = jnp.mean(vv, axis=-1, keepdims=True)
    xc = vv - mu
    r = lax.rsqrt(jnp.mean(xc * xc, axis=-1, keepdims=True) + EPS)
    vn = (xc * r) * ln_g + ln_b
    row = lax.broadcasted_iota(jnp.int32, (t, t), 0)
    col = lax.broadcasted_iota(jnp.int32, (t, t), 1)
    causal = row >= col
    chunks = []
    for n in range(zu.shape[0] // t):
        blocks = []
        for h in range(heads):
            wm = jnp.where(causal, w_s[h], jnp.zeros_like(w_s[h])).astype(BF16)
            vb = vn[n * t:(n + 1) * t, h * hd:(h + 1) * hd].astype(BF16)
            blocks.append(jnp.dot(wm, vb, preferred_element_type=F32) + b_st[:, h:h + 1])
        chunks.append(jnp.concatenate(blocks, axis=1))
    s = jnp.concatenate(chunks, axis=0) if len(chunks) > 1 else chunks[0]
    return _rms(uu * s, g_norm)


def _swiglu(gate, up):
    return jax.nn.silu(gate) * up


def _head_loss(x2, gpre, pp, b_g, g_final, target):
    gate = jax.nn.sigmoid(gpre + b_g)
    out = _rms(x2 + gate * pp, g_final)
    err = jnp.square(out - target)
    return 0.5 * jnp.sum(jnp.mean(err, axis=-1))


def _ssm_disc(lam_re, lam_im, log_step_col):
    lr = jnp.minimum(lam_re, LAMBDA_RE_MAX)
    li = lam_im
    dt = jnp.exp(log_step_col)
    mag = jnp.exp(lr * dt)
    ang = li * dt
    abar_re = mag * jnp.cos(ang)
    abar_im = mag * jnp.sin(ang)
    nr = abar_re - 1.0
    ni = abar_im
    den = lr * lr + li * li
    q_re = (nr * lr + ni * li) / den
    q_im = (ni * lr - nr * li) / den
    return abar_re, abar_im, q_re, q_im


def _ssm_bbar(q_re_col, q_im_col, b_re, b_im):
    return q_re_col * b_re - q_im_col * b_im, q_re_col * b_im + q_im_col * b_re


def _adamw(w, g, m, v):
    m = ADAM_B1 * m + (1.0 - ADAM_B1) * g
    v = ADAM_B2 * v + (1.0 - ADAM_B2) * jnp.square(g)
    m_hat = m / (1.0 - ADAM_B1 ** ADAM_STEP)
    v_hat = v / (1.0 - ADAM_B2 ** ADAM_STEP)
    delta = -ADAM_LR * (m_hat / (jnp.sqrt(v_hat) + ADAM_EPS) + ADAM_WD * w)
    return delta, m, v


class _SsmDims:
    def __init__(self, groups, state, gch):
        self.g, self.p, self.h = groups, state, gch
        self.d = groups * gch
        self.cb = min(SSM_CH_BLOCK, self.d)
        self.gb = self.cb // gch
        self.ns = self.gb * state
        self.nb = self.d // self.cb


def _ssm_forward_params(sd, lam_re, lam_im, log_step, b_re, b_im):
    gp = sd.g * sd.p

    def disc(lr, li, ls):
        ar, ai, qr, qi = _ssm_disc(lr, li, ls)
        pr, pi_ = [ar], [ai]
        for _ in range(SUBLANES - 1):
            pr, pi_ = pr + [pr[-1] * ar - pi_[-1] * ai], pi_ + [pr[-1] * ai + pi_[-1] * ar]
        return ar, ai, qr, qi, jnp.concatenate(pr, axis=0), jnp.concatenate(pi_, axis=0)

    gp_s = _sds((sd.g, sd.p), F32)
    pw_s = _sds((SUBLANES * sd.g, sd.p), F32)
    ar, ai, qr, qi, pw_re, pw_im = _whole("ssm_disc", disc, [lam_re, lam_im, log_step.reshape(sd.g, 1)],
                                          [gp_s, gp_s, gp_s, gp_s, pw_s, pw_s])
    qr_col, qi_col = qr.reshape(gp, 1), qi.reshape(gp, 1)
    bb_s = _sds((gp, sd.h), F32)
    bbar_re, bbar_im = _whole("ssm_bbar", _ssm_bbar, [qr_col, qi_col, b_re.reshape(gp, sd.h), b_im.reshape(gp, sd.h)],
                              [bb_s, bb_s])
    return qr_col, qi_col, bbar_re, bbar_im, pw_re.reshape(SUBLANES, sd.g, sd.p), pw_im.reshape(SUBLANES, sd.g, sd.p)


def _blockdiag_in(sd, bbar):
    b = bbar.reshape(sd.nb, sd.gb, sd.p, sd.h).transpose(0, 1, 3, 2)
    eye = jnp.eye(sd.gb, dtype=bbar.dtype)
    return (b[:, :, :, None, :] * eye[None, :, None, :, None]).reshape(sd.nb, sd.cb, sd.ns)


def _blockdiag_out(sd, c):
    cc = c.reshape(sd.nb, sd.gb, sd.h, sd.p).transpose(0, 1, 3, 2)
    eye = jnp.eye(sd.gb, dtype=c.dtype)
    return (cc[:, :, :, None, :] * eye[None, :, None, :, None]).reshape(sd.nb, sd.ns, sd.cb)


def _diag_in(sd, dense):
    x = dense.reshape(sd.nb, sd.gb, sd.h, sd.gb, sd.p)
    return jnp.einsum("jghgp->jgph", x).reshape(sd.g * sd.p, sd.h)


def _diag_out(sd, dense):
    x = dense.reshape(sd.nb, sd.gb, sd.p, sd.gb, sd.h)
    return jnp.einsum("jgpgh->jghp", x).reshape(sd.g, sd.h, sd.p)


def _scan_consts(sd, pw_re, pw_im, reverse):
    pr = pw_re.reshape(SUBLANES, sd.nb, sd.ns)
    pi_ = pw_im.reshape(SUBLANES, sd.nb, sd.ns)
    if reverse:
        pi_ = -pi_
    rows = jnp.arange(SUBLANES)[None, :, None]
    parts = []
    for d in (1, 2, 4):
        keep = (rows < SUBLANES - d) if reverse else (rows >= d)
        parts += [jnp.where(keep, pr[d - 1][:, None, :], 0.0), jnp.where(keep, pi_[d - 1][:, None, :], 0.0)]
    cr, ci = pr.transpose(1, 0, 2), pi_.transpose(1, 0, 2)
    if reverse:
        cr, ci = cr[:, ::-1, :], ci[:, ::-1, :]
    return jnp.concatenate(parts + [cr, ci], axis=1).astype(F32)


def _block_scan(s_ref, cst_ref, carry_ref, sd, rows, reverse):
    ns = sd.ns
    nblk = rows // SUBLANES
    w = min(SCAN_LANES, ns)
    for c0 in range(0, ns, w):
        re_l, im_l = slice(c0, c0 + w), slice(ns + c0, ns + c0 + w)
        cst = [cst_ref[k * SUBLANES:(k + 1) * SUBLANES, c0:c0 + w] for k in range(8)]

        def step(k, carry, re_l=re_l, im_l=im_l, cst=cst):
            local = []
            for b in range(SCAN_BLOCKS):
                blk = SCAN_BLOCKS * k + b
                blk = (nblk - 1 - blk) if reverse else blk
                r0 = pl.multiple_of(blk * SUBLANES, SUBLANES)
                xr = s_ref[pl.ds(r0, SUBLANES), re_l]
                xi = s_ref[pl.ds(r0, SUBLANES), im_l]
                for n, d in enumerate((1, 2, 4)):
                    ar, ai = cst[2 * n], cst[2 * n + 1]
                    shift = (SUBLANES - d) if reverse else d
                    sr = pltpu.roll(xr, shift, 0)
                    si = pltpu.roll(xi, shift, 0)
                    xr, xi = xr + ar * sr - ai * si, xi + ar * si + ai * sr
                local.append((r0, xr, xi))
            cr, ci = carry
            edge = slice(0, 1) if reverse else slice(SUBLANES - 1, SUBLANES)
            for r0, xr, xi in local:
                br = jnp.broadcast_to(cr, xr.shape)
                bi = jnp.broadcast_to(ci, xi.shape)
                xr, xi = xr + cst[6] * br - cst[7] * bi, xi + cst[6] * bi + cst[7] * br
                s_ref[pl.ds(r0, SUBLANES), re_l] = xr
                s_ref[pl.ds(r0, SUBLANES), im_l] = xi
                cr, ci = xr[edge, :], xi[edge, :]
            return cr, ci

        cr, ci = lax.fori_loop(0, nblk // SCAN_BLOCKS, step, (carry_ref[0:1, re_l], carry_ref[0:1, im_l]))
        carry_ref[0:1, re_l] = cr
        carry_ref[0:1, im_l] = ci


def _ssm_fwd(sd, z, wb, wc, cst, d_row, tt=512):
    n_tok = z.shape[0]
    tt = _pick(n_tok, tt, SUBLANES)
    cb, ns2 = sd.cb, 2 * sd.ns

    def body(z_ref, wb_ref, wc_ref, cst_ref, d_ref, y_ref, s_ref, carry_ref):
        @pl.when(pl.program_id(1) == 0)
        def _():
            carry_ref[...] = jnp.zeros(carry_ref.shape, F32)
        u = z_ref[...]
        s_ref[...] = jnp.dot(u.astype(BF16), wb_ref[...], preferred_element_type=F32)
        _block_scan(s_ref, cst_ref, carry_ref, sd, tt, reverse=False)
        y = jnp.dot(s_ref[...].astype(BF16), wc_ref[...], preferred_element_type=F32)
        y_ref[...] = y + d_ref[...] * u

    return pl.pallas_call(
        body, name="ssm_fwd", grid=(sd.nb, n_tok // tt),
        in_specs=[pl.BlockSpec((tt, cb), lambda j, i: (i, j)),
                  pl.BlockSpec((None, cb, ns2), lambda j, i: (j, 0, 0)),
                  pl.BlockSpec((None, ns2, cb), lambda j, i: (j, 0, 0)),
                  pl.BlockSpec((None, 8 * SUBLANES, sd.ns), lambda j, i: (j, 0, 0)),
                  pl.BlockSpec((1, cb), lambda j, i: (0, j))],
        out_specs=[pl.BlockSpec((tt, cb), lambda j, i: (i, j)), pl.BlockSpec((tt, ns2), lambda j, i: (i, j))],
        out_shape=[_sds((n_tok, sd.d), F32), _sds((n_tok, sd.nb * ns2), F32)],
        scratch_shapes=[pltpu.VMEM((SUBLANES, ns2), F32)],
        compiler_params=_params(("arbitrary", "arbitrary")),
    )(z, wb, wc, cst, d_row)


def _ssm_bwd(sd, dy, z, states, wct, wbt, cst_rev, d_row, tt=512):
    n_tok = z.shape[0]
    tt = _pick(n_tok, tt, SUBLANES)
    nt = n_tok // tt
    cb, ns, ns2 = sd.cb, sd.ns, 2 * sd.ns
    blocks_per_tile = tt // SUBLANES
    tn_dims = (((0,), (0,)), ((), ()))

    def body(dy_ref, z_ref, s_ref, sp_ref, wct_ref, wbt_ref, cst_ref, d_ref,
             du_ref, dwb_ref, dwc_ref, da_ref, dd_ref, lam_ref, carry_ref):
        i = pl.program_id(1)

        @pl.when(i == 0)
        def _():
            carry_ref[...] = jnp.zeros(carry_ref.shape, F32)
            dwb_ref[...] = jnp.zeros(dwb_ref.shape, F32)
            dwc_ref[...] = jnp.zeros(dwc_ref.shape, F32)
            da_ref[...] = jnp.zeros(da_ref.shape, F32)
            dd_ref[...] = jnp.zeros(dd_ref.shape, F32)

        dy_t = dy_ref[...]
        u = z_ref[...]
        dy16 = dy_t.astype(BF16)
        lam_ref[...] = jnp.dot(dy16, wct_ref[...], preferred_element_type=F32)
        _block_scan(lam_ref, cst_ref, carry_ref, sd, tt, reverse=True)
        lam = lam_ref[...]
        lam16 = lam.astype(BF16)
        du_ref[...] = jnp.dot(lam16, wbt_ref[...], preferred_element_type=F32) + d_ref[...] * dy_t
        dd_ref[0:1, :] += jnp.sum(dy_t * u, axis=0, keepdims=True)
        dwb_ref[...] += lax.dot_general(u.astype(BF16), lam16, tn_dims, preferred_element_type=F32)
        s = s_ref[...]
        dwc_ref[...] += lax.dot_general(s.astype(BF16), dy16, tn_dims, preferred_element_type=F32)
        before = jnp.where(i == nt - 1, 0.0, 1.0) * sp_ref[SUBLANES - 1:SUBLANES, :]
        first_row = lax.broadcasted_iota(jnp.int32, s.shape, 0) == 0
        prev = jnp.where(first_row, jnp.broadcast_to(before, s.shape), pltpu.roll(s, 1, 0))
        lr, li = lam[:, :ns], lam[:, ns:]
        pr, pi_ = prev[:, :ns], prev[:, ns:]
        da_ref[0:1, 0:ns] += jnp.sum(lr * pr + li * pi_, axis=0, keepdims=True)
        da_ref[0:1, ns:ns2] += jnp.sum(li * pr - lr * pi_, axis=0, keepdims=True)

    rev = lambda i: nt - 1 - i
    return pl.pallas_call(
        body, name="ssm_bwd", grid=(sd.nb, nt),
        in_specs=[pl.BlockSpec((tt, cb), lambda j, i: (rev(i), j)),
                  pl.BlockSpec((tt, cb), lambda j, i: (rev(i), j)),
                  pl.BlockSpec((tt, ns2), lambda j, i: (rev(i), j)),
                  pl.BlockSpec((SUBLANES, ns2), lambda j, i: (jnp.maximum(rev(i) * blocks_per_tile - 1, 0), j)),
                  pl.BlockSpec((None, cb, ns2), lambda j, i: (j, 0, 0)),
                  pl.BlockSpec((None, ns2, cb), lambda j, i: (j, 0, 0)),
                  pl.BlockSpec((None, 8 * SUBLANES, ns), lambda j, i: (j, 0, 0)),
                  pl.BlockSpec((1, cb), lambda j, i: (0, j))],
        out_specs=[pl.BlockSpec((tt, cb), lambda j, i: (rev(i), j)),
                   pl.BlockSpec((None, cb, ns2), lambda j, i: (j, 0, 0)),
                   pl.BlockSpec((None, ns2, cb), lambda j, i: (j, 0, 0)),
                   pl.BlockSpec((None, SUBLANES, ns2), lambda j, i: (j, 0, 0)),
                   pl.BlockSpec((None, SUBLANES, cb), lambda j, i: (j, 0, 0))],
        out_shape=[_sds((n_tok, sd.d), F32), _sds((sd.nb, cb, ns2), F32), _sds((sd.nb, ns2, cb), F32),
                   _sds((sd.nb, SUBLANES, ns2), F32), _sds((sd.nb, SUBLANES, cb), F32)],
        scratch_shapes=[pltpu.VMEM((tt, ns2), F32), pltpu.VMEM((SUBLANES, ns2), F32)],
        compiler_params=_params(("arbitrary", "arbitrary")),
    )(dy, z, states, states, wct, wbt, cst_rev, d_row)


def _local_grads(x, p, target, wg, sp):
    n_tok, d_model = x.shape
    d_ssm = sp["ssm_d"].shape[0] * sp["ssm_d"].shape[1]
    d_sgu = sp["sgu_ln_g"].shape[-1]
    d_ffn = wg["w_ffn_out"].shape[0]
    sd = _SsmDims(sp["ssm_b_re"].shape[0], sp["ssm_b_re"].shape[1], sp["ssm_b_re"].shape[2])
    heads, chunk, _ = sp["sgu_w"].shape
    row = lambda v: v.reshape(1, -1)
    tok = lambda w, dt=F32: _sds((n_tok, w), dt)
    acc = lambda w: _sds((1, w), F32)

    g_mix = row(sp["norm_mix_g"])
    (h1,) = _rowwise("norm_mix", lambda a, g: _rms(a, g), [x], [g_mix], [tok(d_model, BF16)])
    z = _mm_nn("proj_in", h1, wg["w_in"], sharded=True, tn=768)

    qr_col, qi_col, bbar_re, bbar_im, pw_re, pw_im = _ssm_forward_params(
        sd, sp["ssm_lambda_re"], sp["ssm_lambda_im"], sp["ssm_log_step"], sp["ssm_b_re"], sp["ssm_b_im"])
    wb = jnp.concatenate([_blockdiag_in(sd, bbar_re), _blockdiag_in(sd, bbar_im)], axis=2).astype(BF16)
    wc = jnp.concatenate([_blockdiag_out(sd, sp["ssm_c_re"]), -_blockdiag_out(sd, sp["ssm_c_im"])], axis=1).astype(BF16)
    d_row = row(sp["ssm_d"])
    y_pre, states = _ssm_fwd(sd, z, wb, wc, _scan_consts(sd, pw_re, pw_im, False), d_row)

    (ya0_16,) = _rowwise("ssm_gelu", lambda a: jax.nn.gelu(a), [y_pre], [], [tok(d_ssm, BF16)])
    q = _mm_nn("ssm_glu", ya0_16, wg["ssm_glu_w"], tm=1024)
    glu_b, g_ossm = row(sp["ssm_glu_b"]), row(sp["out_norm_ssm_g"])
    (ya_n,) = _rowwise("ssm_glu_out", _glu_out, [y_pre, q], [glu_b, g_ossm], [tok(d_ssm, BF16)])

    assert d_ssm == d_sgu
    zu, zv = _Cols(z, d_sgu, 1), _Cols(z, d_sgu, 2)
    ln_g, ln_b, g_osgu = row(sp["sgu_ln_g"]), row(sp["sgu_ln_b"]), row(sp["out_norm_sgu_g"])
    b_st = sp["sgu_b"].T
    sgu_tr = 2 * chunk
    (yb_n,) = _rowwise("sgu", _sgu_rows, [zu, zv], [ln_g, ln_b, sp["sgu_w"], b_st, g_osgu], [tok(d_sgu, BF16)], tr=sgu_tr)

    ycat = jnp.concatenate([ya_n, yb_n], axis=1)
    x1 = _mm_nn("proj_out", ycat, wg["w_out"], res=x, tm=1024)

    g_ffn = row(sp["norm_ffn_g"])
    (h2,) = _rowwise("norm_ffn", lambda a, g: _rms(a, g), [x1], [g_ffn], [tok(d_model, BF16)])
    gu = _mm_nn("ffn_in", h2, wg["w_ffn_in"], sharded=True, tn=1408, w_resident=True)
    gate_c, up_c = _Cols(gu, d_ffn, 0), _Cols(gu, d_ffn, 1)
    (act,) = _rowwise("swiglu", _swiglu, [gate_c, up_c], [], [tok(d_ffn, BF16)], tr=128)
    x2 = _mm_nn("ffn_out", act, wg["w_ffn_out"], res=x1)

    g_ple = row(sp["norm_ple_g"])
    (h3,) = _rowwise("norm_ple", lambda a, g: _rms(a, g), [x2], [g_ple], [tok(d_model, BF16)])
    gpre = _mm_nn("ple_gate", h3, wg["w_ple_gate"], tm=1024)
    (p16,) = _rowwise("ple_cast", lambda a: a, [p], [], [tok(p.shape[1], BF16)])
    pp = _mm_nn("ple_proj", p16, wg["w_ple_proj"], sharded=True, tm=1024)

    b_g, g_fin = row(sp["b_ple_gate"]), row(sp["final_norm_g"])

    def head(x2_t, gpre_t, pp_t, tgt_t, b_g_v, g_fin_v):
        loss, grads = jax.value_and_grad(_head_loss, argnums=(0, 1, 2, 3, 4))(x2_t, gpre_t, pp_t, b_g_v, g_fin_v, tgt_t)
        dx2, dgpre, dpp, db, dg = grads
        return dx2, dgpre.astype(BF16), dpp.astype(BF16), jnp.full((1, LANES), loss, F32), db, dg

    dx2_head, dgpre16, dpp16, loss_row, d_b_g, d_g_fin = _rowwise(
        "head", head, [x2, gpre, pp, target], [b_g, g_fin],
        [tok(d_model), tok(d_model, BF16), tok(d_model, BF16)], [acc(LANES), acc(d_model), acc(d_model)])
    loss = loss_row[0, 0]

    grads = {}
    grads["w_ple_proj"] = _mm_tn("d_ple_proj", p16, dpp16, shards=N_CHIPS, tk=256)
    grads["w_ple_gate"] = _mm_tn("d_ple_gate", h3, dgpre16)
    dh3 = _mm_nt("d_h3", dgpre16, wg["w_ple_gate"], tm=1024)

    def norm_bwd(x_t, dres_t, dh_t, g_v):
        _, vjp = jax.vjp(_rms, x_t, g_v)
        dx, dg = vjp(dh_t)
        dx = dres_t + dx
        return dx, dx.astype(BF16), dg

    dx2, dx2_16, d_g_ple = _rowwise("d_norm_ple", norm_bwd, [x2, dx2_head, dh3], [g_ple],
                                    [tok(d_model), tok(d_model, BF16)], [acc(d_model)])
    grads["w_ffn_out"] = _mm_tn("d_ffn_out", act, dx2_16)
    dact = _mm_nt("d_act", dx2_16, wg["w_ffn_out"], tm=1024)

    def swiglu_bwd(gate_t, up_t, dact_t):
        _, vjp = jax.vjp(_swiglu, gate_t, up_t)
        dg, du = vjp(dact_t)
        return jnp.concatenate([dg, du], axis=1)

    (dgu16,) = _rowwise("d_swiglu", swiglu_bwd, [gate_c, up_c, dact], [], [tok(2 * d_ffn, BF16)], tr=128)
    grads["w_ffn_in"] = _mm_tn("d_ffn_in", h2, dgu16, shards=N_CHIPS, tn=1408, g_resident=True)
    dh2 = _mm_nt("d_h2", dgu16, wg["w_ffn_in"], sharded=True, tm=256, w_resident=True)
    dx1, dx1_16, d_g_ffn = _rowwise("d_norm_ffn", norm_bwd, [x1, dx2, dh2], [g_ffn],
                                    [tok(d_model), tok(d_model, BF16)], [acc(d_model)])
    grads["w_out"] = _mm_tn("d_proj_out", ycat, dx1_16)
    dycat = _mm_nt("d_ycat", dx1_16, wg["w_out"], tm=1024)

    def glu_out_bwd(y_pre_t, q_t, dy_t, glu_b_v, g_v):
        _, vjp = jax.vjp(_glu_out, y_pre_t, q_t, glu_b_v, g_v)
        dy_pre, dq, db, dg = vjp(dy_t)
        return dy_pre, dq.astype(BF16), db, dg

    dy_pre_a, dq16, d_glu_b, d_g_ossm = _rowwise(
        "d_ssm_glu_out", glu_out_bwd, [y_pre, q, _Cols(dycat, d_ssm, 0)], [glu_b, g_ossm],
        [tok(d_ssm), tok(d_ssm, BF16)], [acc(d_ssm), acc(d_ssm)])
    grads["ssm_glu_w"] = _mm_tn("d_ssm_glu", ya0_16, dq16)
    dya0 = _mm_nt("d_ya0", dq16, wg["ssm_glu_w"], tm=1024)

    def gelu_bwd(y_pre_t, dy_a_t, dya0_t):
        _, vjp = jax.vjp(jax.nn.gelu, y_pre_t)
        return dy_a_t + vjp(dya0_t)[0]

    (dy_pre,) = _rowwise("d_ssm_gelu", gelu_bwd, [y_pre, dy_pre_a, dya0], [], [tok(d_ssm)])

    wct, wbt = jnp.swapaxes(wc, 1, 2), jnp.swapaxes(wb, 1, 2)
    dz_ssm, dwb, dwc, da, dd = _ssm_bwd(sd, dy_pre, z, states, wct, wbt, _scan_consts(sd, pw_re, pw_im, True), d_row)

    def sgu_bwd(zu_t, zv_t, dy_t, ln_g_v, ln_b_v, w_v, b_v, g_v):
        _, vjp = jax.vjp(_sgu_rows, zu_t, zv_t, ln_g_v, ln_b_v, w_v, b_v, g_v)
        dzu, dzv, dlg, dlb, dw, db, dg = vjp(dy_t)
        return dzu, dzv, dlg, dlb, dw, db, dg

    dzu, dzv, d_ln_g, d_ln_b, d_sgu_w, d_b_st, d_g_osgu = _rowwise(
        "d_sgu", sgu_bwd, [zu, zv, _Cols(dycat, d_sgu, 1)], [ln_g, ln_b, sp["sgu_w"], b_st, g_osgu],
        [tok(d_sgu, BF16), tok(d_sgu, BF16)],
        [acc(d_sgu), acc(d_sgu), _sds(sp["sgu_w"].shape, F32), _sds(b_st.shape, F32), acc(d_sgu)], tr=sgu_tr)

    (dz_ssm16,) = _rowwise("d_ssm_cast", lambda a: a, [dz_ssm], [], [tok(d_ssm, BF16)])
    dz16 = jnp.concatenate([dz_ssm16, dzu, dzv], axis=1)
    grads["w_in"] = _mm_tn("d_proj_in", h1, dz16, shards=N_CHIPS, tn=768)
    dh1 = _mm_nt("d_h1", dz16, wg["w_in"], sharded=True, tm=1024)

    def norm_in_bwd(x_t, dres_t, dh_t, g_v):
        _, vjp = jax.vjp(_rms, x_t, g_v)
        dx, dg = vjp(dh_t)
        return dres_t + dx, dg

    grad_x, d_g_mix = _rowwise("d_norm_mix", norm_in_bwd, [x, dx1, dh1], [g_mix], [tok(d_model)], [acc(d_model)])

    gp = sd.g * sd.p
    dbbar_re = _diag_in(sd, dwb[:, :, :sd.ns])
    dbbar_im = _diag_in(sd, dwb[:, :, sd.ns:])
    d_c_re = _diag_out(sd, dwc[:, :sd.ns, :])
    d_c_im = -_diag_out(sd, dwc[:, sd.ns:, :])
    dabar_re = da[:, 0, :sd.ns].reshape(sd.g, sd.p)
    dabar_im = da[:, 0, sd.ns:].reshape(sd.g, sd.p)
    d_ssm_d = dd[:, 0, :].reshape(sd.g, sd.h)

    def bbar_bwd(qr, qi, b_re_v, b_im_v, dre, dim):
        _, vjp = jax.vjp(_ssm_bbar, qr, qi, b_re_v, b_im_v)
        return vjp((dre, dim))

    b_re2, b_im2 = sp["ssm_b_re"].reshape(gp, sd.h), sp["ssm_b_im"].reshape(gp, sd.h)
    dq_re, dq_im, d_b_re, d_b_im = _whole(
        "d_ssm_bbar", bbar_bwd, [qr_col, qi_col, b_re2, b_im2, dbbar_re, dbbar_im],
        [_sds((gp, 1), F32), _sds((gp, 1), F32), _sds((gp, sd.h), F32), _sds((gp, sd.h), F32)])

    def disc_bwd(lr, li, ls, dar, dai, dqr, dqi):
        _, vjp = jax.vjp(_ssm_disc, lr, li, ls)
        return vjp((dar, dai, dqr, dqi))

    gp_s = _sds((sd.g, sd.p), F32)
    d_lam_re, d_lam_im, d_log_step = _whole(
        "d_ssm_disc", disc_bwd,
        [sp["ssm_lambda_re"], sp["ssm_lambda_im"], sp["ssm_log_step"].reshape(sd.g, 1),
         dabar_re, dabar_im, dq_re.reshape(sd.g, sd.p), dq_im.reshape(sd.g, sd.p)],
        [gp_s, gp_s, _sds((sd.g, 1), F32)])

    small = {
        "norm_mix_g": d_g_mix, "ssm_lambda_re": d_lam_re, "ssm_lambda_im": d_lam_im, "ssm_log_step": d_log_step,
        "ssm_b_re": d_b_re, "ssm_b_im": d_b_im, "ssm_c_re": d_c_re, "ssm_c_im": d_c_im, "ssm_d": d_ssm_d,
        "ssm_glu_b": d_glu_b, "sgu_ln_g": d_ln_g, "sgu_ln_b": d_ln_b, "sgu_w": d_sgu_w, "sgu_b": d_b_st.T,
        "out_norm_ssm_g": d_g_ossm, "out_norm_sgu_g": d_g_osgu, "norm_ffn_g": d_g_ffn, "norm_ple_g": d_g_ple,
        "b_ple_gate": d_b_g, "final_norm_g": d_g_fin,
    }
    return loss, grad_x, grads, small


def _place():
    x, y, c = lax.axis_index("x"), lax.axis_index("y"), lax.axis_index("c")
    chips = [(1 - x, y), (x, 1 - y), (1 - x, 1 - y)]
    return x, y, c, chips


ANY = pl.BlockSpec(memory_space=pl.ANY)


def _cast_into_slot(name, w2d, shard, tr=256):
    rows, cols = w2d.shape
    rh = rows // 2
    tr = _pick(rh, tr, 16)
    per = rh // tr

    def body(s_ref, a_ref, o_ref):
        o_ref[...] = a_ref[...].astype(BF16)

    grid_spec = pltpu.PrefetchScalarGridSpec(
        num_scalar_prefetch=1, grid=(2, per),
        in_specs=[pl.BlockSpec((tr, cols), lambda h, i, s_ref: (h * per + i, 0))],
        out_specs=pl.BlockSpec((None, None, tr, cols), lambda h, i, s_ref: (s_ref[0], h, i, 0)))
    return pl.pallas_call(body, name=name, grid_spec=grid_spec, out_shape=_sds((N_CHIPS, 2, rh, cols), BF16),
                          compiler_params=_params(("arbitrary", "arbitrary")))(shard.reshape(1).astype(jnp.int32), w2d)


def _allgather_weights(slots):
    n = len(slots)

    def body(*refs):
        outs = refs[n:2 * n]
        send_sems, recv_sems = refs[2 * n:]
        x, y, c, chips = _place()
        sibling = (x, y, 1 - c)
        mine = 2 * x + y

        def remote(k, src, dst, to):
            return pltpu.make_async_remote_copy(src_ref=src, dst_ref=dst, send_sem=send_sems.at[k], recv_sem=recv_sems.at[k],
                                                device_id=to, device_id_type=MESH)

        sends = []
        for w in range(n):
            own = outs[w].at[mine, c]
            for j, chip in enumerate(chips):
                sends.append(remote(3 * w + j, own, own, (*chip, c)))
        for cp in sends:
            cp.start()
        passed = []
        for w in range(n):
            for j, (cx, cy) in enumerate(chips):
                theirs = outs[w].at[2 * cx + cy, c]
                remote(3 * w + j, theirs, theirs, (x, y, c)).wait_recv()
                fwd = remote(3 * n + 3 * w + j, theirs, theirs, sibling)
                fwd.start()
                passed.append(fwd)
        for w in range(n):
            for j, (cx, cy) in enumerate(chips):
                theirs = outs[w].at[2 * cx + cy, 1 - c]
                remote(3 * n + 3 * w + j, theirs, theirs, (x, y, c)).wait_recv()
        for cp in sends + passed:
            cp.wait_send()

    return pl.pallas_call(
        body, name="allgather_weights", in_specs=[ANY] * n, out_specs=[ANY] * n,
        out_shape=[_sds(s.shape, s.dtype) for s in slots], input_output_aliases={w: w for w in range(n)},
        scratch_shapes=[pltpu.SemaphoreType.DMA((6 * n,)), pltpu.SemaphoreType.DMA((6 * n,))],
    )(*slots)


def _swap_halves(grads):
    n = len(grads)

    def body(*refs):
        ins, outs = refs[:n], refs[n:2 * n]
        send_sems, recv_sems = refs[2 * n:]
        x, y, c, _ = _place()
        copies = [pltpu.make_async_remote_copy(src_ref=ins[w].at[:, 1 - c], dst_ref=outs[w], send_sem=send_sems.at[w],
                                               recv_sem=recv_sems.at[w], device_id=(x, y, 1 - c), device_id_type=MESH)
                  for w in range(n)]
        for cp in copies:
            cp.start()
        for cp in copies:
            cp.wait()

    return pl.pallas_call(
        body, name="grad_swap_halves", in_specs=[ANY] * n, out_specs=[ANY] * n,
        out_shape=[_sds((g.shape[0], *g.shape[2:]), g.dtype) for g in grads],
        scratch_shapes=[pltpu.SemaphoreType.DMA((n,)), pltpu.SemaphoreType.DMA((n,))],
    )(*grads)


def _scatter_quarters(halves):
    n = len(halves)

    def body(*refs):
        ins, outs = refs[:n], refs[n:2 * n]
        send_sems, recv_sems = refs[2 * n:]
        x, y, c, chips = _place()
        copies = []
        for w in range(n):
            for j, (cx, cy) in enumerate(chips):
                copies.append(pltpu.make_async_remote_copy(
                    src_ref=ins[w].at[2 * cx + cy], dst_ref=outs[w].at[j], send_sem=send_sems.at[3 * w + j],
                    recv_sem=recv_sems.at[3 * w + j], device_id=(cx, cy, c), device_id_type=MESH))
        for cp in copies:
            cp.start()
        for cp in copies:
            cp.wait()

    return pl.pallas_call(
        body, name="grad_scatter", in_specs=[ANY] * n, out_specs=[ANY] * n,
        out_shape=[_sds((3, *h.shape[1:]), h.dtype) for h in halves],
        scratch_shapes=[pltpu.SemaphoreType.DMA((3 * n,)), pltpu.SemaphoreType.DMA((3 * n,))],
    )(*halves)


def _join_halves(slots):
    n = len(slots)

    def body(*refs):
        outs = refs[n:2 * n]
        send_sems, recv_sems = refs[2 * n:]
        x, y, c, _ = _place()

        def copy(w, half, to):
            return pltpu.make_async_remote_copy(src_ref=outs[w].at[half], dst_ref=outs[w].at[half], send_sem=send_sems.at[w],
                                                recv_sem=recv_sems.at[w], device_id=to, device_id_type=MESH)

        copies = [copy(w, c, (x, y, 1 - c)) for w in range(n)]
        for cp in copies:
            cp.start()
        for w in range(n):
            copy(w, 1 - c, (x, y, c)).wait_recv()
        for cp in copies:
            cp.wait_send()

    return pl.pallas_call(
        body, name="grad_join_halves", in_specs=[ANY] * n, out_specs=[ANY] * n,
        out_shape=[_sds(s.shape, s.dtype) for s in slots], input_output_aliases={w: w for w in range(n)},
        scratch_shapes=[pltpu.SemaphoreType.DMA((n,)), pltpu.SemaphoreType.DMA((n,))],
    )(*slots)


def _allreduce_small(block, tr=256):
    rows, lanes = block.shape
    tr = _pick(rows, tr, SUBLANES)

    def body(x_ref, o_ref, buf, send_sems, recv_sems):
        x, y, c, chips = _place()
        me, sibling = (x, y, c), (x, y, 1 - c)

        def slot(px, py, pc):
            return buf.at[4 * px + 2 * py + pc]

        def copy(k, block_of, to):
            return pltpu.make_async_remote_copy(src_ref=slot(*block_of), dst_ref=slot(*block_of), send_sem=send_sems.at[k],
                                                recv_sem=recv_sems.at[k], device_id=to, device_id_type=MESH)

        slot(*me)[...] = x_ref[...]
        first = [copy(0, me, sibling)] + [copy(1 + j, me, (*chip, c)) for j, chip in enumerate(chips)]
        for cp in first:
            cp.start()
        passed = [copy(4 + j, (*chip, c), sibling) for j, chip in enumerate(chips)]
        for j, chip in enumerate(chips):
            copy(1 + j, (*chip, c), me).wait_recv()
            passed[j].start()
        copy(0, sibling, me).wait_recv()
        for j, chip in enumerate(chips):
            copy(4 + j, (*chip, 1 - c), me).wait_recv()
        for cp in first + passed:
            cp.wait_send()
        for r0 in range(0, rows, tr):
            acc = buf[0, r0:r0 + tr, :]
            for k in range(1, N_DEV):
                acc = acc + buf[k, r0:r0 + tr, :]
            o_ref[r0:r0 + tr, :] = acc

    vm = pl.BlockSpec(memory_space=pltpu.VMEM)
    return pl.pallas_call(
        body, name="allreduce_small", in_specs=[vm], out_specs=vm, out_shape=_sds((rows, lanes), block.dtype),
        scratch_shapes=[pltpu.VMEM((N_DEV, rows, lanes), block.dtype), pltpu.SemaphoreType.DMA((7,)), pltpu.SemaphoreType.DMA((7,))],
        compiler_params=pltpu.CompilerParams(vmem_limit_bytes=VMEM_LIMIT),
    )(block)


def _sum_received(name, own, received, c, tr=256):
    n, rows, cols = received.shape
    tr = _pick(rows, tr, 16)

    def body(c_ref, a_ref, s_ref, o_ref):
        acc = a_ref[...]
        for k in range(n):
            acc = acc + s_ref[k].astype(F32)
        o_ref[...] = acc

    grid_spec = pltpu.PrefetchScalarGridSpec(
        num_scalar_prefetch=1, grid=(rows // tr,),
        in_specs=[pl.BlockSpec((tr, cols), lambda i, c_ref: (i, 0)), pl.BlockSpec((n, tr, cols), lambda i, c_ref: (0, i, 0))],
        out_specs=pl.BlockSpec((None, tr, cols), lambda i, c_ref: (c_ref[0], i, 0)))
    return pl.pallas_call(body, name=name, grid_spec=grid_spec, out_shape=_sds((2, rows, cols), F32),
                          compiler_params=_params(("arbitrary",)))(c.reshape(1).astype(jnp.int32), own, received)


def _add_halves(name, full, c, received, tr=256):
    s, _, rh, cols = full.shape
    tr = _pick(rh, tr, 16)

    def body(c_ref, a_ref, b_ref, o_ref):
        o_ref[...] = (a_ref[...] + b_ref[...]).astype(BF16)

    grid_spec = pltpu.PrefetchScalarGridSpec(
        num_scalar_prefetch=1, grid=(s, rh // tr),
        in_specs=[pl.BlockSpec((None, None, tr, cols), lambda q, i, c_ref: (q, c_ref[0], i, 0)),
                  pl.BlockSpec((None, tr, cols), lambda q, i, c_ref: (q, i, 0))],
        out_specs=pl.BlockSpec((None, tr, cols), lambda q, i, c_ref: (q, i, 0)))
    return pl.pallas_call(body, name=name, grid_spec=grid_spec, out_shape=_sds((s, rh, cols), BF16),
                          compiler_params=_params(("arbitrary", "arbitrary")))(c.reshape(1).astype(jnp.int32), full, received)


def _own_half(name, full, c, shard, received, tr=256):
    _, _, rh, cols = full.shape
    tr = _pick(rh, tr, SUBLANES)

    def body(i_ref, a_ref, b_ref, o_ref):
        o_ref[...] = a_ref[...] + b_ref[...]

    grid_spec = pltpu.PrefetchScalarGridSpec(
        num_scalar_prefetch=1, grid=(rh // tr,),
        in_specs=[pl.BlockSpec((None, None, tr, cols), lambda i, i_ref: (i_ref[1], i_ref[0], i, 0)),
                  pl.BlockSpec((None, tr, cols), lambda i, i_ref: (i_ref[1], i, 0))],
        out_specs=pl.BlockSpec((tr, cols), lambda i, i_ref: (i, 0)))
    return pl.pallas_call(body, name=name, grid_spec=grid_spec, out_shape=_sds((rh, cols), F32),
                          compiler_params=_params(("arbitrary",)))(jnp.stack([c, shard]).astype(jnp.int32), full, received)


LARGE = ("w_in", "ssm_glu_w", "w_out", "w_ffn_in", "w_ffn_out", "w_ple_gate", "w_ple_proj")
COLUMN_SHARDED = ("w_in", "w_ffn_in", "w_ple_proj")
SMALL = ("norm_mix_g", "ssm_lambda_re", "ssm_lambda_im", "ssm_log_step", "ssm_b_re", "ssm_b_im", "ssm_c_re", "ssm_c_im",
         "ssm_d", "ssm_glu_b", "sgu_ln_g", "sgu_ln_b", "sgu_w", "sgu_b", "out_norm_ssm_g", "out_norm_sgu_g", "norm_ffn_g",
         "norm_ple_g", "b_ple_gate", "final_norm_g")
WEIGHTS = ("norm_mix_g", "w_in", "ssm_lambda_re", "ssm_lambda_im", "ssm_log_step", "ssm_b_re", "ssm_b_im", "ssm_c_re",
           "ssm_c_im", "ssm_d", "ssm_glu_w", "ssm_glu_b", "sgu_ln_g", "sgu_ln_b", "sgu_w", "sgu_b", "out_norm_ssm_g",
           "out_norm_sgu_g", "w_out", "norm_ffn_g", "w_ffn_in", "w_ffn_out", "norm_ple_g", "w_ple_gate", "b_ple_gate",
           "w_ple_proj", "final_norm_g")
PACK_ROWS = SUBLANES * LANES


def _pack(arrays):
    parts = []
    for a in arrays:
        flat = a.reshape(-1).astype(F32)
        pad = -flat.shape[0] % PACK_ROWS
        parts.append(jnp.pad(flat, (0, pad)) if pad else flat)
    return jnp.concatenate(parts).reshape(-1, LANES)


def _unpack(packed, like):
    flat = packed.reshape(-1)
    out, at = [], 0
    for a in like:
        size = a.size
        out.append(flat[at:at + size].reshape(a.shape))
        at += size + (-size % PACK_ROWS)
    return out


def _reduce_large(grads, c, shard):
    names = list(grads)
    full = [grads[k].reshape(N_CHIPS, 2, grads[k].shape[1] // 2, grads[k].shape[2]) for k in names]
    received = _swap_halves(full)
    halves = [_add_halves("grad_add_halves_" + k, f, c, r) for k, f, r in zip(names, full, received)]
    quarters = _scatter_quarters(halves)
    own = [_own_half("grad_own_" + k, f, c, shard, r) for k, f, r in zip(names, full, received)]
    parts = [_sum_received("grad_sum_" + k, o, q, c) for k, q, o in zip(names, quarters, own)]
    joined = _join_halves(parts)
    return {k: j.reshape(2 * j.shape[1], j.shape[2]) for k, j in zip(names, joined)}


def kernel(x, p, norm_mix_g, w_in, ssm_lambda_re, ssm_lambda_im, ssm_log_step, ssm_b_re, ssm_b_im, ssm_c_re, ssm_c_im, ssm_d, ssm_glu_w, ssm_glu_b, sgu_ln_g, sgu_ln_b, sgu_w, sgu_b, out_norm_ssm_g, out_norm_sgu_g, w_out, norm_ffn_g, w_ffn_in, w_ffn_out, norm_ple_g, w_ple_gate, b_ple_gate, w_ple_proj, final_norm_g, loss_target, m_norm_mix_g, m_w_in, m_ssm_lambda_re, m_ssm_lambda_im, m_ssm_log_step, m_ssm_b_re, m_ssm_b_im, m_ssm_c_re, m_ssm_c_im, m_ssm_d, m_ssm_glu_w, m_ssm_glu_b, m_sgu_ln_g, m_sgu_ln_b, m_sgu_w, m_sgu_b, m_out_norm_ssm_g, m_out_norm_sgu_g, m_w_out, m_norm_ffn_g, m_w_ffn_in, m_w_ffn_out, m_norm_ple_g, m_w_ple_gate, m_b_ple_gate, m_w_ple_proj, m_final_norm_g, v_norm_mix_g, v_w_in, v_ssm_lambda_re, v_ssm_lambda_im, v_ssm_log_step, v_ssm_b_re, v_ssm_b_im, v_ssm_c_re, v_ssm_c_im, v_ssm_d, v_ssm_glu_w, v_ssm_glu_b, v_sgu_ln_g, v_sgu_ln_b, v_sgu_w, v_sgu_b, v_out_norm_ssm_g, v_out_norm_sgu_g, v_w_out, v_norm_ffn_g, v_w_ffn_in, v_w_ffn_out, v_norm_ple_g, v_w_ple_gate, v_b_ple_gate, v_w_ple_proj, v_final_norm_g):
    given = dict(locals())
    w = {k: given[k] for k in WEIGHTS}
    m = {k: given["m_" + k] for k in WEIGHTS}
    v = {k: given["v_" + k] for k in WEIGHTS}
    c = lax.axis_index("c")
    shard = 2 * lax.axis_index("x") + lax.axis_index("y")

    gathered = _allgather_weights([_cast_into_slot("cast_" + k, w[k].reshape(w[k].shape[1:]), shard) for k in LARGE])
    wg = {}
    for k, g in zip(LARGE, gathered):
        _, _, rh, cols = g.shape
        wg[k] = g.reshape(N_CHIPS, 2 * rh, cols) if k in COLUMN_SHARDED else g.reshape(N_CHIPS * 2 * rh, cols)

    unlayer = lambda a: a if a.ndim == 1 else a[0]
    sp = {k: unlayer(w[k]) for k in SMALL}
    n_tok, d_model = x.shape[1:]
    loss, grad_x, grads, small = _local_grads(x.reshape(n_tok, d_model), p.reshape(n_tok, p.shape[-1]),
                                              loss_target.reshape(n_tok, d_model), wg, sp)
    loss = lax.psum(loss, ("x", "y", "c"))

    shard_major = {}
    for k in LARGE:
        g = grads[k]
        shard_major[k] = g if k in COLUMN_SHARDED else g.reshape(N_CHIPS, g.shape[0] // N_CHIPS, g.shape[1])
    reduced = _reduce_large(shard_major, c, shard)

    grad_w, delta_w, new_m, new_v = {}, {}, {}, {}
    for k in LARGE:
        shape = w[k].shape
        two_d = lambda a: a.reshape(shape[1:])
        like = _sds(shape[1:], F32)
        d_k, m_k, v_k = _rowwise("adamw_" + k, _adamw, [two_d(w[k]), reduced[k], two_d(m[k]), two_d(v[k])], [], [like, like, like])
        grad_w[k], delta_w[k], new_m[k], new_v[k] = (a.reshape(shape) for a in (reduced[k], d_k, m_k, v_k))

    packed_g = _allreduce_small(_pack([small[k].reshape(w[k].shape) for k in SMALL]))
    like = _sds(packed_g.shape, F32)
    d_s, m_s, v_s = _rowwise("adamw_small", _adamw, [_pack([w[k] for k in SMALL]), packed_g, _pack([m[k] for k in SMALL]),
                                                     _pack([v[k] for k in SMALL])], [], [like, like, like])
    shapes = [w[k] for k in SMALL]
    for k, g_k, d_k, m_k, v_k in zip(SMALL, _unpack(packed_g, shapes), _unpack(d_s, shapes), _unpack(m_s, shapes), _unpack(v_s, shapes)):
        grad_w[k], delta_w[k], new_m[k], new_v[k] = g_k, d_k, m_k, v_k

    return (loss, grad_x.reshape(x.shape), *[grad_w[k] for k in WEIGHTS], *[delta_w[k] for k in WEIGHTS],
            *[new_m[k] for k in WEIGHTS], *[new_v[k] for k in WEIGHTS])
```

```python
import functools

import jax
import jax.numpy as jnp
from jax import lax
from jax.experimental import pallas as pl
from jax.experimental.pallas import tpu as pltpu

F32 = jnp.float32
BF16 = jnp.bfloat16

EPS = 1e-6
LAMBDA_RE_MAX = -1e-4
ADAM_LR = 0.001
ADAM_B1 = 0.9
ADAM_B2 = 0.999
ADAM_EPS = 1e-08
ADAM_WD = 0.01
ADAM_STEP = 10

N_CHIPS = 4
N_DEV = 8
SUBLANES = 8
LANES = 128
SSM_CH_BLOCK = 256
SCAN_LANES = 256
SCAN_BLOCKS = 2
VMEM_LIMIT = 56 * 1024 * 1024

MESH = pl.DeviceIdType.MESH


def _pick(n, pref, mult):
    if n <= pref:
        return n
    t = (pref // mult) * mult
    while t >= mult:
        if n % t == 0:
            return t
        t -= mult
    return n


def _params(semantics):
    return pltpu.CompilerParams(dimension_semantics=semantics, vmem_limit_bytes=VMEM_LIMIT)


class _Cols:
    def __init__(self, arr, width, blk):
        self.arr, self.width, self.blk = arr, width, blk


def _sds(shape, dtype):
    return jax.ShapeDtypeStruct(tuple(shape), dtype)


ANY = pl.BlockSpec(memory_space=pl.ANY)


class _Side:
    def __init__(self, ins, out_shapes, n_sems, first, last, mid=None, aliases=None):
        self.ins, self.out_shapes, self.n_sems = list(ins), list(out_shapes), n_sems
        self.first, self.mid, self.last = first, mid, last
        self.aliases = dict(aliases or {})


def _call(body, side, operands, *, name, grid, in_specs, out_specs, out_shape, compiler_params, scratch_shapes=()):
    if side is None:
        return pl.pallas_call(body, name=name, grid=grid, in_specs=in_specs, out_specs=out_specs, out_shape=out_shape,
                              scratch_shapes=list(scratch_shapes), compiler_params=compiler_params)(*operands)
    single = not isinstance(out_specs, (list, tuple))
    out_specs = [out_specs] if single else list(out_specs)
    out_shape = [out_shape] if single else list(out_shape)
    n_in, n_out, n_scr = len(in_specs), len(out_specs), len(scratch_shapes)
    n_sin, n_sout = len(side.ins), len(side.out_shapes)
    steps = 1
    for g in grid:
        steps *= g

    def hosted(*refs):
        ins, s_ins = refs[:n_in], refs[n_in:n_in + n_sin]
        at = n_in + n_sin
        outs, s_outs = refs[at:at + n_out], refs[at + n_out:at + n_out + n_sout]
        scratch = refs[at + n_out + n_sout:at + n_out + n_sout + n_scr]
        sems = refs[-2:]
        step = pl.program_id(0)
        for d in range(1, len(grid)):
            step = step * grid[d] + pl.program_id(d)

        @pl.when(step == 0)
        def _():
            side.first(s_ins, s_outs, *sems)

        if side.mid is not None:
            @pl.when(step == (3 * steps) // 4)
            def _():
                side.mid(s_ins, s_outs, *sems)

        body(*ins, *outs, *scratch)

        @pl.when(step == steps - 1)
        def _():
            side.last(s_ins, s_outs, *sems)

    res = pl.pallas_call(
        hosted, name=name, grid=grid, in_specs=[*in_specs, *[ANY] * n_sin], out_specs=[*out_specs, *[ANY] * n_sout],
        out_shape=[*out_shape, *side.out_shapes], input_output_aliases={n_in + i: n_out + o for i, o in side.aliases.items()},
        scratch_shapes=[*scratch_shapes, pltpu.SemaphoreType.DMA((side.n_sems,)), pltpu.SemaphoreType.DMA((side.n_sems,))],
        compiler_params=compiler_params)(*operands, *side.ins)
    return (res[0] if single else list(res[:n_out])), list(res[n_out:])


def _rowwise(name, fn, rows, params, row_outs, acc_outs=(), tr=256, side=None):
    rows = [r if isinstance(r, _Cols) else _Cols(r, r.shape[1], 0) for r in rows]
    m = rows[0].arr.shape[0]
    tr = _pick(m, tr, 16)
    n_in = len(rows) + len(params)
    n_ro = len(row_outs)

    def body(*refs):
        vals = fn(*[r[...] for r in refs[:n_in]])
        if not isinstance(vals, (tuple, list)):
            vals = (vals,)
        outs = refs[n_in:]
        for r, v in zip(outs[:n_ro], vals[:n_ro]):
            r[...] = v.astype(r.dtype)
        first = pl.program_id(0) == 0
        for r, v in zip(outs[n_ro:], vals[n_ro:]):
            @pl.when(first)
            def _():
                r[...] = jnp.zeros(r.shape, r.dtype)
            r[...] += v.astype(r.dtype).reshape(r.shape)

    in_specs = [pl.BlockSpec((tr, r.width), lambda i, b=r.blk: (i, b)) for r in rows]
    in_specs += [pl.BlockSpec(p.shape, lambda i, nd=p.ndim: (0,) * nd) for p in params]
    out_specs = [pl.BlockSpec((tr, o.shape[1]), lambda i: (i, 0)) for o in row_outs]
    out_specs += [pl.BlockSpec(o.shape, lambda i, nd=len(o.shape): (0,) * nd) for o in acc_outs]
    return _call(body, side, [*[r.arr for r in rows], *params], name=name, grid=(m // tr,), in_specs=in_specs,
                 out_specs=out_specs, out_shape=[*row_outs, *acc_outs], compiler_params=_params(("arbitrary",)))


def _whole(name, fn, ins, outs):
    n_in = len(ins)

    def body(*refs):
        vals = fn(*[r[...] for r in refs[:n_in]])
        if not isinstance(vals, (tuple, list)):
            vals = (vals,)
        for r, v in zip(refs[n_in:], vals):
            r[...] = v.astype(r.dtype).reshape(r.shape)

    vm = pl.BlockSpec(memory_space=pltpu.VMEM)
    return pl.pallas_call(body, name=name, in_specs=[vm] * n_in, out_specs=[vm] * len(outs), out_shape=list(outs),
                          compiler_params=pltpu.CompilerParams(vmem_limit_bytes=VMEM_LIMIT))(*ins)


def _grid_order(swap):
    if not swap:
        return (lambda grid: grid), (lambda f: f)
    return (lambda grid: grid[::-1]), (lambda f: (lambda j, i: f(i, j)))


def _mm_nn(name, a, w, *, sharded=False, res=None, out_dtype=F32, tm=512, tn=512, w_resident=False, side=None):
    m, k = a.shape
    tm = _pick(m, tm, 16)
    order, ix = _grid_order(w_resident)
    if sharded:
        s, _, ns = w.shape
        n = s * ns
        tn = _pick(ns, tn, LANES)
        per = ns // tn
        w_spec = pl.BlockSpec((None, k, tn), ix(lambda i, j: (j // per, 0, j % per)))
    else:
        n = w.shape[1]
        tn = _pick(n, tn, LANES)
        w_spec = pl.BlockSpec((k, tn), ix(lambda i, j: (0, j)))

    def body(a_ref, w_ref, *rest):
        acc = jnp.dot(a_ref[...], w_ref[...], preferred_element_type=F32)
        if res is not None:
            acc = acc + rest[0][...]
        rest[-1][...] = acc.astype(out_dtype)

    in_specs = [pl.BlockSpec((tm, k), ix(lambda i, j: (i, 0))), w_spec]
    ops = [a, w]
    if res is not None:
        in_specs.append(pl.BlockSpec((tm, tn), ix(lambda i, j: (i, j))))
        ops.append(res)
    return _call(body, side, ops, name=name, grid=order((m // tm, n // tn)), in_specs=in_specs,
                 out_specs=pl.BlockSpec((tm, tn), ix(lambda i, j: (i, j))), out_shape=_sds((m, n), out_dtype),
                 compiler_params=_params(("arbitrary", "arbitrary")))


def _mm_nt(name, g, w, *, sharded=False, tm=512, tk=512, w_resident=False, side=None):
    m, n = g.shape
    tm = _pick(m, tm, 16)
    order, ix = _grid_order(w_resident)
    dims = (((1,), (1,)), ((), ()))
    if sharded:
        s, k, ns = w.shape
        tk = _pick(k, tk, LANES)
        w_spec = pl.BlockSpec((s, tk, ns), ix(lambda i, j: (0, j, 0)))

        def body(g_ref, w_ref, o_ref):
            acc = lax.dot_general(g_ref[:, 0:ns], w_ref[0], dims, preferred_element_type=F32)
            for q in range(1, s):
                acc = acc + lax.dot_general(g_ref[:, q * ns:(q + 1) * ns], w_ref[q], dims, preferred_element_type=F32)
            o_ref[...] = acc
    else:
        k = w.shape[0]
        tk = _pick(k, tk, LANES)
        w_spec = pl.BlockSpec((tk, n), ix(lambda i, j: (j, 0)))

        def body(g_ref, w_ref, o_ref):
            o_ref[...] = lax.dot_general(g_ref[...], w_ref[...], dims, preferred_element_type=F32)

    return _call(body, side, [g, w], name=name, grid=order((m // tm, k // tk)),
                 in_specs=[pl.BlockSpec((tm, n), ix(lambda i, j: (i, 0))), w_spec],
                 out_specs=pl.BlockSpec((tm, tk), ix(lambda i, j: (i, j))), out_shape=_sds((m, k), F32),
                 compiler_params=_params(("arbitrary", "arbitrary")))


def _mm_tn(name, a, g, *, shards=0, tk=512, tn=512, g_resident=False, side=None):
    m, k = a.shape
    n = g.shape[1]
    tk = _pick(k, tk, LANES)
    order, ix = _grid_order(g_resident)
    dims = (((0,), (0,)), ((), ()))
    if shards:
        ns = n // shards
        tn = _pick(ns, tn, LANES)
        per = ns // tn
        out_spec = pl.BlockSpec((None, tk, tn), ix(lambda i, j: (j // per, i, j % per)))
        out_shape = _sds((shards, k, ns), F32)
    else:
        tn = _pick(n, tn, LANES)
        out_spec = pl.BlockSpec((tk, tn), ix(lambda i, j: (i, j)))
        out_shape = _sds((k, n), F32)

    def body(a_ref, g_ref, o_ref):
        o_ref[...] = lax.dot_general(a_ref[...], g_ref[...], dims, preferred_element_type=F32)

    return _call(body, side, [a, g], name=name, grid=order((k // tk, n // tn)),
                 in_specs=[pl.BlockSpec((m, tk), ix(lambda i, j: (0, i))), pl.BlockSpec((m, tn), ix(lambda i, j: (0, j)))],
                 out_specs=out_spec, out_shape=out_shape, compiler_params=_params(("arbitrary", "arbitrary")))


def _rms(x, g):
    r = lax.rsqrt(jnp.mean(x * x, axis=-1, keepdims=True) + EPS)
    return (x * r) * g


def _glu_out(y_pre, q, glu_b, g_norm):
    ya0 = jax.nn.gelu(y_pre)
    return _rms(ya0 * jax.nn.sigmoid(q + glu_b), g_norm)


def _sgu_rows(zu, zv, ln_g, ln_b, w_s, b_st, g_norm):
    heads, t, _ = w_s.shape
    hd = zu.shape[1] // heads
    uu = jax.nn.gelu(zu)
    vv = jax.nn.gelu(zv)
    mu = jnp.mean(vv, axis=-1, keepdims=True)
    xc = vv - mu
    r = lax.rsqrt(jnp.mean(xc * xc, axis=-1, keepdims=True) + EPS)
    vn = (xc * r) * ln_g + ln_b
    row = lax.broadcasted_iota(jnp.int32, (t, t), 0)
    col = lax.broadcasted_iota(jnp.int32, (t, t), 1)
    causal = row >= col
    chunks = []
    for n in range(zu.shape[0] // t):
        blocks = []
        for h in range(heads):
            wm = jnp.where(causal, w_s[h], jnp.zeros_like(w_s[h])).astype(BF16)
            vb = vn[n * t:(n + 1) * t, h * hd:(h + 1) * hd].astype(BF16)
            blocks.append(jnp.dot(wm, vb, preferred_element_type=F32) + b_st[:, h:h + 1])
        chunks.append(jnp.concatenate(blocks, axis=1))
    s = jnp.concatenate(chunks, axis=0) if len(chunks) > 1 else chunks[0]
    return _rms(uu * s, g_norm)


def _swiglu(gate, up):
    return jax.nn.silu(gate) * up


def _head_loss(x2, gpre, pp, b_g, g_final, target):
    gate = jax.nn.sigmoid(gpre + b_g)
    out = _rms(x2 + gate * pp, g_final)
    err = jnp.square(out - target)
    return 0.5 * jnp.sum(jnp.mean(err, axis=-1))


def _ssm_disc(lam_re, lam_im, log_step_col):
    lr = jnp.minimum(lam_re, LAMBDA_RE_MAX)
    li = lam_im
    dt = jnp.exp(log_step_col)
    mag = jnp.exp(lr * dt)
    ang = li * dt
    abar_re = mag * jnp.cos(ang)
    abar_im = mag * jnp.sin(ang)
    nr = abar_re - 1.0
    ni = abar_im
    den = lr * lr + li * li
    q_re = (nr * lr + ni * li) / den
    q_im = (ni * lr - nr * li) / den
    return abar_re, abar_im, q_re, q_im


def _ssm_bbar(q_re_col, q_im_col, b_re, b_im):
    return q_re_col * b_re - q_im_col * b_im, q_re_col * b_im + q_im_col * b_re


def _adamw(w, g, m, v):
    m = ADAM_B1 * m + (1.0 - ADAM_B1) * g
    v = ADAM_B2 * v + (1.0 - ADAM_B2) * jnp.square(g)
    m_hat = m / (1.0 - ADAM_B1 ** ADAM_STEP)
    v_hat = v / (1.0 - ADAM_B2 ** ADAM_STEP)
    delta = -ADAM_LR * (m_hat / (jnp.sqrt(v_hat) + ADAM_EPS) + ADAM_WD * w)
    return delta, m, v


class _SsmDims:
    def __init__(self, groups, state, gch):
        self.g, self.p, self.h = groups, state, gch
        self.d = groups * gch
        self.cb = min(SSM_CH_BLOCK, self.d)
        self.gb = self.cb // gch
        self.ns = self.gb * state
        self.nb = self.d // self.cb


def _ssm_forward_params(sd, lam_re, lam_im, log_step, b_re, b_im):
    gp = sd.g * sd.p

    def disc(lr, li, ls):
        ar, ai, qr, qi = _ssm_disc(lr, li, ls)
        pr, pi_ = [ar], [ai]
        for _ in range(SUBLANES - 1):
            pr, pi_ = pr + [pr[-1] * ar - pi_[-1] * ai], pi_ + [pr[-1] * ai + pi_[-1] * ar]
        return ar, ai, qr, qi, jnp.concatenate(pr, axis=0), jnp.concatenate(pi_, axis=0)

    gp_s = _sds((sd.g, sd.p), F32)
    pw_s = _sds((SUBLANES * sd.g, sd.p), F32)
    ar, ai, qr, qi, pw_re, pw_im = _whole("ssm_disc", disc, [lam_re, lam_im, log_step.reshape(sd.g, 1)],
                                          [gp_s, gp_s, gp_s, gp_s, pw_s, pw_s])
    qr_col, qi_col = qr.reshape(gp, 1), qi.reshape(gp, 1)
    bb_s = _sds((gp, sd.h), F32)
    bbar_re, bbar_im = _whole("ssm_bbar", _ssm_bbar, [qr_col, qi_col, b_re.reshape(gp, sd.h), b_im.reshape(gp, sd.h)],
                              [bb_s, bb_s])
    return qr_col, qi_col, bbar_re, bbar_im, pw_re.reshape(SUBLANES, sd.g, sd.p), pw_im.reshape(SUBLANES, sd.g, sd.p)


def _blockdiag_in(sd, bbar):
    b = bbar.reshape(sd.nb, sd.gb, sd.p, sd.h).transpose(0, 1, 3, 2)
    eye = jnp.eye(sd.gb, dtype=bbar.dtype)
    return (b[:, :, :, None, :] * eye[None, :, None, :, None]).reshape(sd.nb, sd.cb, sd.ns)


def _blockdiag_out(sd, c):
    cc = c.reshape(sd.nb, sd.gb, sd.h, sd.p).transpose(0, 1, 3, 2)
    eye = jnp.eye(sd.gb, dtype=c.dtype)
    return (cc[:, :, :, None, :] * eye[None, :, None, :, None]).reshape(sd.nb, sd.ns, sd.cb)


def _diag_in(sd, dense):
    x = dense.reshape(sd.nb, sd.gb, sd.h, sd.gb, sd.p)
    return jnp.einsum("jghgp->jgph", x).reshape(sd.g * sd.p, sd.h)


def _diag_out(sd, dense):
    x = dense.reshape(sd.nb, sd.gb, sd.p, sd.gb, sd.h)
    return jnp.einsum("jgpgh->jghp", x).reshape(sd.g, sd.h, sd.p)


def _scan_consts(sd, pw_re, pw_im, reverse):
    pr = pw_re.reshape(SUBLANES, sd.nb, sd.ns)
    pi_ = pw_im.reshape(SUBLANES, sd.nb, sd.ns)
    if reverse:
        pi_ = -pi_
    rows = jnp.arange(SUBLANES)[None, :, None]
    parts = []
    for d in (1, 2, 4):
        keep = (rows < SUBLANES - d) if reverse else (rows >= d)
        parts += [jnp.where(keep, pr[d - 1][:, None, :], 0.0), jnp.where(keep, pi_[d - 1][:, None, :], 0.0)]
    cr, ci = pr.transpose(1, 0, 2), pi_.transpose(1, 0, 2)
    if reverse:
        cr, ci = cr[:, ::-1, :], ci[:, ::-1, :]
    return jnp.concatenate(parts + [cr, ci], axis=1).astype(F32)


def _block_scan(s_ref, cst_ref, carry_ref, sd, rows, reverse):
    ns = sd.ns
    nblk = rows // SUBLANES
    w = min(SCAN_LANES, ns)
    for c0 in range(0, ns, w):
        re_l, im_l = slice(c0, c0 + w), slice(ns + c0, ns + c0 + w)
        cst = [cst_ref[k * SUBLANES:(k + 1) * SUBLANES, c0:c0 + w] for k in range(8)]

        def step(k, carry, re_l=re_l, im_l=im_l, cst=cst):
            local = []
            for b in range(SCAN_BLOCKS):
                blk = SCAN_BLOCKS * k + b
                blk = (nblk - 1 - blk) if reverse else blk
                r0 = pl.multiple_of(blk * SUBLANES, SUBLANES)
                xr = s_ref[pl.ds(r0, SUBLANES), re_l]
                xi = s_ref[pl.ds(r0, SUBLANES), im_l]
                for n, d in enumerate((1, 2, 4)):
                    ar, ai = cst[2 * n], cst[2 * n + 1]
                    shift = (SUBLANES - d) if reverse else d
                    sr = pltpu.roll(xr, shift, 0)
                    si = pltpu.roll(xi, shift, 0)
                    xr, xi = xr + ar * sr - ai * si, xi + ar * si + ai * sr
                local.append((r0, xr, xi))
            cr, ci = carry
            edge = slice(0, 1) if reverse else slice(SUBLANES - 1, SUBLANES)
            for r0, xr, xi in local:
                br = jnp.broadcast_to(cr, xr.shape)
                bi = jnp.broadcast_to(ci, xi.shape)
                xr, xi = xr + cst[6] * br - cst[7] * bi, xi + cst[6] * bi + cst[7] * br
                s_ref[pl.ds(r0, SUBLANES), re_l] = xr
                s_ref[pl.ds(r0, SUBLANES), im_l] = xi
                cr, ci = xr[edge, :], xi[edge, :]
            return cr, ci

        cr, ci = lax.fori_loop(0, nblk // SCAN_BLOCKS, step, (carry_ref[0:1, re_l], carry_ref[0:1, im_l]))
        carry_ref[0:1, re_l] = cr
        carry_ref[0:1, im_l] = ci


def _ssm_fwd(name, sd, z, wb, wc, cst, d_row, tt=512, side=None):
    n_tok = z.shape[0]
    tt = _pick(n_tok, tt, SUBLANES)
    cb, ns2 = sd.cb, 2 * sd.ns

    def body(z_ref, wb_ref, wc_ref, cst_ref, d_ref, y_ref, s_ref, carry_ref):
        @pl.when(pl.program_id(1) == 0)
        def _():
            carry_ref[...] = jnp.zeros(carry_ref.shape, F32)
        u = z_ref[...]
        s_ref[...] = jnp.dot(u.astype(BF16), wb_ref[...], preferred_element_type=F32)
        _block_scan(s_ref, cst_ref, carry_ref, sd, tt, reverse=False)
        y = jnp.dot(s_ref[...].astype(BF16), wc_ref[...], preferred_element_type=F32)
        y_ref[...] = y + d_ref[...] * u

    return _call(
        body, side, [z, wb, wc, cst, d_row], name=name, grid=(sd.nb, n_tok // tt),
        in_specs=[pl.BlockSpec((tt, cb), lambda j, i: (i, j)),
                  pl.BlockSpec((None, cb, ns2), lambda j, i: (j, 0, 0)),
                  pl.BlockSpec((None, ns2, cb), lambda j, i: (j, 0, 0)),
                  pl.BlockSpec((None, 8 * SUBLANES, sd.ns), lambda j, i: (j, 0, 0)),
                  pl.BlockSpec((1, cb), lambda j, i: (0, j))],
        out_specs=[pl.BlockSpec((tt, cb), lambda j, i: (i, j)), pl.BlockSpec((tt, ns2), lambda j, i: (i, j))],
        out_shape=[_sds((n_tok, sd.d), F32), _sds((n_tok, sd.nb * ns2), F32)],
        scratch_shapes=[pltpu.VMEM((SUBLANES, ns2), F32)],
        compiler_params=_params(("arbitrary", "arbitrary")))


def _ssm_bwd(name, sd, dy, z, states, wct, wbt, cst_rev, d_row, tt=512, side=None):
    n_tok = z.shape[0]
    tt = _pick(n_tok, tt, SUBLANES)
    nt = n_tok // tt
    cb, ns, ns2 = sd.cb, sd.ns, 2 * sd.ns
    blocks_per_tile = tt // SUBLANES
    tn_dims = (((0,), (0,)), ((), ()))

    def body(dy_ref, z_ref, s_ref, sp_ref, wct_ref, wbt_ref, cst_ref, d_ref,
             du_ref, dwb_ref, dwc_ref, da_ref, dd_ref, lam_ref, carry_ref):
        i = pl.program_id(1)

        @pl.when(i == 0)
        def _():
            carry_ref[...] = jnp.zeros(carry_ref.shape, F32)
            dwb_ref[...] = jnp.zeros(dwb_ref.shape, F32)
            dwc_ref[...] = jnp.zeros(dwc_ref.shape, F32)
            da_ref[...] = jnp.zeros(da_ref.shape, F32)
            dd_ref[...] = jnp.zeros(dd_ref.shape, F32)

        dy_t = dy_ref[...]
        u = z_ref[...]
        dy16 = dy_t.astype(BF16)
        lam_ref[...] = jnp.dot(dy16, wct_ref[...], preferred_element_type=F32)
        _block_scan(lam_ref, cst_ref, carry_ref, sd, tt, reverse=True)
        lam = lam_ref[...]
        lam16 = lam.astype(BF16)
        du_ref[...] = jnp.dot(lam16, wbt_ref[...], preferred_element_type=F32) + d_ref[...] * dy_t
        dd_ref[0:1, :] += jnp.sum(dy_t * u, axis=0, keepdims=True)
        dwb_ref[...] += lax.dot_general(u.astype(BF16), lam16, tn_dims, preferred_element_type=F32)
        s = s_ref[...]
        dwc_ref[...] += lax.dot_general(s.astype(BF16), dy16, tn_dims, preferred_element_type=F32)
        before = jnp.where(i == nt - 1, 0.0, 1.0) * sp_ref[SUBLANES - 1:SUBLANES, :]
        first_row = lax.broadcasted_iota(jnp.int32, s.shape, 0) == 0
        prev = jnp.where(first_row, jnp.broadcast_to(before, s.shape), pltpu.roll(s, 1, 0))
        lr, li = lam[:, :ns], lam[:, ns:]
        pr, pi_ = prev[:, :ns], prev[:, ns:]
        da_ref[0:1, 0:ns] += jnp.sum(lr * pr + li * pi_, axis=0, keepdims=True)
        da_ref[0:1, ns:ns2] += jnp.sum(li * pr - lr * pi_, axis=0, keepdims=True)

    rev = lambda i: nt - 1 - i
    return _call(
        body, side, [dy, z, states, states, wct, wbt, cst_rev, d_row], name=name, grid=(sd.nb, nt),
        in_specs=[pl.BlockSpec((tt, cb), lambda j, i: (rev(i), j)),
                  pl.BlockSpec((tt, cb), lambda j, i: (rev(i), j)),
                  pl.BlockSpec((tt, ns2), lambda j, i: (rev(i), j)),
                  pl.BlockSpec((SUBLANES, ns2), lambda j, i: (jnp.maximum(rev(i) * blocks_per_tile - 1, 0), j)),
                  pl.BlockSpec((None, cb, ns2), lambda j, i: (j, 0, 0)),
                  pl.BlockSpec((None, ns2, cb), lambda j, i: (j, 0, 0)),
                  pl.BlockSpec((None, 8 * SUBLANES, ns), lambda j, i: (j, 0, 0)),
                  pl.BlockSpec((1, cb), lambda j, i: (0, j))],
        out_specs=[pl.BlockSpec((tt, cb), lambda j, i: (rev(i), j)),
                   pl.BlockSpec((None, cb, ns2), lambda j, i: (j, 0, 0)),
                   pl.BlockSpec((None, ns2, cb), lambda j, i: (j, 0, 0)),
                   pl.BlockSpec((None, SUBLANES, ns2), lambda j, i: (j, 0, 0)),
                   pl.BlockSpec((None, SUBLANES, cb), lambda j, i: (j, 0, 0))],
        out_shape=[_sds((n_tok, sd.d), F32), _sds((sd.nb, cb, ns2), F32), _sds((sd.nb, ns2, cb), F32),
                   _sds((sd.nb, SUBLANES, ns2), F32), _sds((sd.nb, SUBLANES, cb), F32)],
        scratch_shapes=[pltpu.VMEM((tt, ns2), F32), pltpu.VMEM((SUBLANES, ns2), F32)],
        compiler_params=_params(("arbitrary", "arbitrary")))


def _hosted(exch, fn, name, *args, **kw):
    side = exch.side(name)
    if side is None:
        return fn(name, *args, **kw)
    out, moved = fn(name, *args, side=side, **kw)
    exch.done(name, moved)
    return out


def _local_grads(x, p, target, sp, exch):
    n_tok, d_model = x.shape
    d_ssm = sp["ssm_d"].shape[0] * sp["ssm_d"].shape[1]
    d_sgu = sp["sgu_ln_g"].shape[-1]
    sd = _SsmDims(sp["ssm_b_re"].shape[0], sp["ssm_b_re"].shape[1], sp["ssm_b_re"].shape[2])
    heads, chunk, _ = sp["sgu_w"].shape
    row = lambda v: v.reshape(1, -1)
    tok = lambda w, dt=F32: _sds((n_tok, w), dt)
    acc = lambda w: _sds((1, w), F32)

    g_mix = row(sp["norm_mix_g"])
    (h1,) = _rowwise("norm_mix", lambda a, g: _rms(a, g), [x], [g_mix], [tok(d_model, BF16)])
    z = _hosted(exch, _mm_nn, "proj_in", h1, exch.weight("w_in"), sharded=True, tn=768)

    qr_col, qi_col, bbar_re, bbar_im, pw_re, pw_im = _ssm_forward_params(
        sd, sp["ssm_lambda_re"], sp["ssm_lambda_im"], sp["ssm_log_step"], sp["ssm_b_re"], sp["ssm_b_im"])
    wb = jnp.concatenate([_blockdiag_in(sd, bbar_re), _blockdiag_in(sd, bbar_im)], axis=2).astype(BF16)
    wc = jnp.concatenate([_blockdiag_out(sd, sp["ssm_c_re"]), -_blockdiag_out(sd, sp["ssm_c_im"])], axis=1).astype(BF16)
    d_row = row(sp["ssm_d"])
    y_pre, states = _hosted(exch, _ssm_fwd, "ssm_fwd", sd, z, wb, wc, _scan_consts(sd, pw_re, pw_im, False), d_row)

    (ya0_16,) = _rowwise("ssm_gelu", lambda a: jax.nn.gelu(a), [y_pre], [], [tok(d_ssm, BF16)])
    q = _mm_nn("ssm_glu", ya0_16, exch.weight("ssm_glu_w"), tm=1024)
    glu_b, g_ossm = row(sp["ssm_glu_b"]), row(sp["out_norm_ssm_g"])
    (ya_n,) = _rowwise("ssm_glu_out", _glu_out, [y_pre, q], [glu_b, g_ossm], [tok(d_ssm, BF16)])

    assert d_ssm == d_sgu
    zu, zv = _Cols(z, d_sgu, 1), _Cols(z, d_sgu, 2)
    ln_g, ln_b, g_osgu = row(sp["sgu_ln_g"]), row(sp["sgu_ln_b"]), row(sp["out_norm_sgu_g"])
    b_st = sp["sgu_b"].T
    sgu_tr = 2 * chunk
    (yb_n,) = _rowwise("sgu", _sgu_rows, [zu, zv], [ln_g, ln_b, sp["sgu_w"], b_st, g_osgu], [tok(d_sgu, BF16)], tr=sgu_tr)

    ycat = jnp.concatenate([ya_n, yb_n], axis=1)
    x1 = _mm_nn("proj_out", ycat, exch.weight("w_out"), res=x, tm=1024)

    g_ffn = row(sp["norm_ffn_g"])
    (h2,) = _rowwise("norm_ffn", lambda a, g: _rms(a, g), [x1], [g_ffn], [tok(d_model, BF16)])
    gu = _hosted(exch, _mm_nn, "ffn_in", h2, exch.weight("w_ffn_in"), sharded=True, tn=1408, w_resident=True)
    d_ffn = gu.shape[1] // 2
    gate_c, up_c = _Cols(gu, d_ffn, 0), _Cols(gu, d_ffn, 1)
    (act,) = _rowwise("swiglu", _swiglu, [gate_c, up_c], [], [tok(d_ffn, BF16)], tr=128)
    x2 = _mm_nn("ffn_out", act, exch.weight("w_ffn_out"), res=x1)

    g_ple = row(sp["norm_ple_g"])
    (h3,) = _rowwise("norm_ple", lambda a, g: _rms(a, g), [x2], [g_ple], [tok(d_model, BF16)])
    gpre = _mm_nn("ple_gate", h3, exch.weight("w_ple_gate"), tm=1024)
    (p16,) = _rowwise("ple_cast", lambda a: a, [p], [], [tok(p.shape[1], BF16)])
    pp = _mm_nn("ple_proj", p16, exch.weight("w_ple_proj"), sharded=True, tm=1024)

    b_g, g_fin = row(sp["b_ple_gate"]), row(sp["final_norm_g"])

    def head(x2_t, gpre_t, pp_t, tgt_t, b_g_v, g_fin_v):
        loss, grads = jax.value_and_grad(_head_loss, argnums=(0, 1, 2, 3, 4))(x2_t, gpre_t, pp_t, b_g_v, g_fin_v, tgt_t)
        dx2, dgpre, dpp, db, dg = grads
        return dx2, dgpre.astype(BF16), dpp.astype(BF16), jnp.full((1, LANES), loss, F32), db, dg

    dx2_head, dgpre16, dpp16, loss_row, d_b_g, d_g_fin = _rowwise(
        "head", head, [x2, gpre, pp, target], [b_g, g_fin],
        [tok(d_model), tok(d_model, BF16), tok(d_model, BF16)], [acc(LANES), acc(d_model), acc(d_model)])
    loss = loss_row[0, 0]

    exch.grad("w_ple_proj", _mm_tn("d_ple_proj", p16, dpp16, shards=N_CHIPS, tk=256))
    exch.grad("w_ple_gate", _mm_tn("d_ple_gate", h3, dgpre16))
    dh3 = _mm_nt("d_h3", dgpre16, exch.weight("w_ple_gate"), tm=1024)

    def norm_bwd(x_t, dres_t, dh_t, g_v):
        _, vjp = jax.vjp(_rms, x_t, g_v)
        dx, dg = vjp(dh_t)
        dx = dres_t + dx
        return dx, dx.astype(BF16), dg

    dx2, dx2_16, d_g_ple = _rowwise("d_norm_ple", norm_bwd, [x2, dx2_head, dh3], [g_ple],
                                    [tok(d_model), tok(d_model, BF16)], [acc(d_model)])
    exch.grad("w_ffn_out", _mm_tn("d_ffn_out", act, dx2_16))
    dact = _hosted(exch, _mm_nt, "d_act", dx2_16, exch.weight("w_ffn_out"), tm=1024)

    def swiglu_bwd(gate_t, up_t, dact_t):
        _, vjp = jax.vjp(_swiglu, gate_t, up_t)
        dg, du = vjp(dact_t)
        return jnp.concatenate([dg, du], axis=1)

    (dgu16,) = _rowwise("d_swiglu", swiglu_bwd, [gate_c, up_c, dact], [], [tok(2 * d_ffn, BF16)], tr=128)
    exch.grad("w_ffn_in", _hosted(exch, _mm_tn, "d_ffn_in", h2, dgu16, shards=N_CHIPS, tn=1408, g_resident=True))
    dh2 = _hosted(exch, _mm_nt, "d_h2", dgu16, exch.weight("w_ffn_in"), sharded=True, tm=256, w_resident=True)
    dx1, dx1_16, d_g_ffn = _rowwise("d_norm_ffn", norm_bwd, [x1, dx2, dh2], [g_ffn],
                                    [tok(d_model), tok(d_model, BF16)], [acc(d_model)])
    exch.grad("w_out", _mm_tn("d_proj_out", ycat, dx1_16))
    dycat = _mm_nt("d_ycat", dx1_16, exch.weight("w_out"), tm=1024)

    def glu_out_bwd(y_pre_t, q_t, dy_t, glu_b_v, g_v):
        _, vjp = jax.vjp(_glu_out, y_pre_t, q_t, glu_b_v, g_v)
        dy_pre, dq, db, dg = vjp(dy_t)
        return dy_pre, dq.astype(BF16), db, dg

    dy_pre_a, dq16, d_glu_b, d_g_ossm = _rowwise(
        "d_ssm_glu_out", glu_out_bwd, [y_pre, q, _Cols(dycat, d_ssm, 0)], [glu_b, g_ossm],
        [tok(d_ssm), tok(d_ssm, BF16)], [acc(d_ssm), acc(d_ssm)])
    exch.grad("ssm_glu_w", _mm_tn("d_ssm_glu", ya0_16, dq16))
    dya0 = _mm_nt("d_ya0", dq16, exch.weight("ssm_glu_w"), tm=1024)

    def gelu_bwd(y_pre_t, dy_a_t, dya0_t):
        _, vjp = jax.vjp(jax.nn.gelu, y_pre_t)
        return dy_a_t + vjp(dya0_t)[0]

    (dy_pre,) = _rowwise("d_ssm_gelu", gelu_bwd, [y_pre, dy_pre_a, dya0], [], [tok(d_ssm)])

    wct, wbt = jnp.swapaxes(wc, 1, 2), jnp.swapaxes(wb, 1, 2)
    dz_ssm, dwb, dwc, da, dd = _hosted(exch, _ssm_bwd, "ssm_bwd", sd, dy_pre, z, states, wct, wbt,
                                       _scan_consts(sd, pw_re, pw_im, True), d_row)

    def sgu_bwd(zu_t, zv_t, dy_t, ln_g_v, ln_b_v, w_v, b_v, g_v):
        _, vjp = jax.vjp(_sgu_rows, zu_t, zv_t, ln_g_v, ln_b_v, w_v, b_v, g_v)
        dzu, dzv, dlg, dlb, dw, db, dg = vjp(dy_t)
        return dzu, dzv, dlg, dlb, dw, db, dg

    dzu, dzv, d_ln_g, d_ln_b, d_sgu_w, d_b_st, d_g_osgu = _hosted(
        exch, _rowwise, "d_sgu", sgu_bwd, [zu, zv, _Cols(dycat, d_sgu, 1)], [ln_g, ln_b, sp["sgu_w"], b_st, g_osgu],
        [tok(d_sgu, BF16), tok(d_sgu, BF16)],
        [acc(d_sgu), acc(d_sgu), _sds(sp["sgu_w"].shape, F32), _sds(b_st.shape, F32), acc(d_sgu)], tr=sgu_tr)

    (dz_ssm16,) = _rowwise("d_ssm_cast", lambda a: a, [dz_ssm], [], [tok(d_ssm, BF16)])
    dz16 = jnp.concatenate([dz_ssm16, dzu, dzv], axis=1)
    exch.grad("w_in", _hosted(exch, _mm_tn, "d_proj_in", h1, dz16, shards=N_CHIPS, tn=768))
    dh1 = _hosted(exch, _mm_nt, "d_h1", dz16, exch.weight("w_in"), sharded=True, tm=1024)

    def norm_in_bwd(x_t, dres_t, dh_t, g_v):
        _, vjp = jax.vjp(_rms, x_t, g_v)
        dx, dg = vjp(dh_t)
        return dres_t + dx, dg

    grad_x, d_g_mix = _rowwise("d_norm_mix", norm_in_bwd, [x, dx1, dh1], [g_mix], [tok(d_model)], [acc(d_model)])

    gp = sd.g * sd.p
    dbbar_re = _diag_in(sd, dwb[:, :, :sd.ns])
    dbbar_im = _diag_in(sd, dwb[:, :, sd.ns:])
    d_c_re = _diag_out(sd, dwc[:, :sd.ns, :])
    d_c_im = -_diag_out(sd, dwc[:, sd.ns:, :])
    dabar_re = da[:, 0, :sd.ns].reshape(sd.g, sd.p)
    dabar_im = da[:, 0, sd.ns:].reshape(sd.g, sd.p)
    d_ssm_d = dd[:, 0, :].reshape(sd.g, sd.h)

    def bbar_bwd(qr, qi, b_re_v, b_im_v, dre, dim):
        _, vjp = jax.vjp(_ssm_bbar, qr, qi, b_re_v, b_im_v)
        return vjp((dre, dim))

    b_re2, b_im2 = sp["ssm_b_re"].reshape(gp, sd.h), sp["ssm_b_im"].reshape(gp, sd.h)
    dq_re, dq_im, d_b_re, d_b_im = _whole(
        "d_ssm_bbar", bbar_bwd, [qr_col, qi_col, b_re2, b_im2, dbbar_re, dbbar_im],
        [_sds((gp, 1), F32), _sds((gp, 1), F32), _sds((gp, sd.h), F32), _sds((gp, sd.h), F32)])

    def disc_bwd(lr, li, ls, dar, dai, dqr, dqi):
        _, vjp = jax.vjp(_ssm_disc, lr, li, ls)
        return vjp((dar, dai, dqr, dqi))

    gp_s = _sds((sd.g, sd.p), F32)
    d_lam_re, d_lam_im, d_log_step = _whole(
        "d_ssm_disc", disc_bwd,
        [sp["ssm_lambda_re"], sp["ssm_lambda_im"], sp["ssm_log_step"].reshape(sd.g, 1),
         dabar_re, dabar_im, dq_re.reshape(sd.g, sd.p), dq_im.reshape(sd.g, sd.p)],
        [gp_s, gp_s, _sds((sd.g, 1), F32)])

    small = {
        "norm_mix_g": d_g_mix, "ssm_lambda_re": d_lam_re, "ssm_lambda_im": d_lam_im, "ssm_log_step": d_log_step,
        "ssm_b_re": d_b_re, "ssm_b_im": d_b_im, "ssm_c_re": d_c_re, "ssm_c_im": d_c_im, "ssm_d": d_ssm_d,
        "ssm_glu_b": d_glu_b, "sgu_ln_g": d_ln_g, "sgu_ln_b": d_ln_b, "sgu_w": d_sgu_w, "sgu_b": d_b_st.T,
        "out_norm_ssm_g": d_g_ossm, "out_norm_sgu_g": d_g_osgu, "norm_ffn_g": d_g_ffn, "norm_ple_g": d_g_ple,
        "b_ple_gate": d_b_g, "final_norm_g": d_g_fin,
    }
    return loss, grad_x, small


def _place():
    x, y, c = lax.axis_index("x"), lax.axis_index("y"), lax.axis_index("c")
    chips = [(1 - x, y), (x, 1 - y), (1 - x, 1 - y)]
    return x, y, c, chips


def _cast_into_slot(name, w2d, shard, tr=256):
    rows, cols = w2d.shape
    rh = rows // 2
    tr = _pick(rh, tr, 16)
    per = rh // tr

    def body(s_ref, a_ref, o_ref):
        o_ref[...] = a_ref[...].astype(BF16)

    grid_spec = pltpu.PrefetchScalarGridSpec(
        num_scalar_prefetch=1, grid=(2, per),
        in_specs=[pl.BlockSpec((tr, cols), lambda h, i, s_ref: (h * per + i, 0))],
        out_specs=pl.BlockSpec((None, None, tr, cols), lambda h, i, s_ref: (s_ref[0], h, i, 0)))
    return pl.pallas_call(body, name=name, grid_spec=grid_spec, out_shape=_sds((N_CHIPS, 2, rh, cols), BF16),
                          compiler_params=_params(("arbitrary", "arbitrary")))(shard.reshape(1).astype(jnp.int32), w2d)


def _allgather_weights(slots):
    n = len(slots)

    def body(*refs):
        outs = refs[n:2 * n]
        send_sems, recv_sems = refs[2 * n:]
        x, y, c, chips = _place()
        sibling = (x, y, 1 - c)
        mine = 2 * x + y

        def remote(k, src, dst, to):
            return pltpu.make_async_remote_copy(src_ref=src, dst_ref=dst, send_sem=send_sems.at[k], recv_sem=recv_sems.at[k],
                                                device_id=to, device_id_type=MESH)

        sends = []
        for w in range(n):
            own = outs[w].at[mine, c]
            for j, chip in enumerate(chips):
                sends.append(remote(3 * w + j, own, own, (*chip, c)))
        for cp in sends:
            cp.start()
        passed = []
        for w in range(n):
            for j, (cx, cy) in enumerate(chips):
                theirs = outs[w].at[2 * cx + cy, c]
                remote(3 * w + j, theirs, theirs, (x, y, c)).wait_recv()
                fwd = remote(3 * n + 3 * w + j, theirs, theirs, sibling)
                fwd.start()
                passed.append(fwd)
        for w in range(n):
            for j, (cx, cy) in enumerate(chips):
                theirs = outs[w].at[2 * cx + cy, 1 - c]
                remote(3 * n + 3 * w + j, theirs, theirs, (x, y, c)).wait_recv()
        for cp in sends + passed:
            cp.wait_send()

    return pl.pallas_call(
        body, name="allgather_weights", in_specs=[ANY] * n, out_specs=[ANY] * n,
        out_shape=[_sds(s.shape, s.dtype) for s in slots], input_output_aliases={w: w for w in range(n)},
        scratch_shapes=[pltpu.SemaphoreType.DMA((6 * n,)), pltpu.SemaphoreType.DMA((6 * n,))],
    )(*slots)


def _gather_side(slots):
    n = len(slots)

    def copies(kind, outs, send_sems, recv_sems):
        x, y, c, chips = _place()

        def remote(k, ref, to):
            return pltpu.make_async_remote_copy(src_ref=ref, dst_ref=ref, send_sem=send_sems.at[k], recv_sem=recv_sems.at[k],
                                                device_id=to, device_id_type=MESH)

        pairs = [(w, j, 2 * cx + cy, (cx, cy)) for w in range(n) for j, (cx, cy) in enumerate(chips)]
        if kind == "sends":
            return [remote(3 * w + j, outs[w].at[2 * x + y, c], (*chip, c)) for w, j, _, chip in pairs]
        if kind == "arrivals":
            return [remote(3 * w + j, outs[w].at[s, c], (x, y, c)) for w, j, s, _ in pairs]
        if kind == "passed":
            return [remote(3 * n + 3 * w + j, outs[w].at[s, c], (x, y, 1 - c)) for w, j, s, _ in pairs]
        return [remote(3 * n + 3 * w + j, outs[w].at[s, 1 - c], (x, y, c)) for w, j, s, _ in pairs]

    def first(ins, outs, *sems):
        for cp in copies("sends", outs, *sems):
            cp.start()

    def mid(ins, outs, *sems):
        for arrived, onward in zip(copies("arrivals", outs, *sems), copies("passed", outs, *sems)):
            arrived.wait_recv()
            onward.start()

    def last(ins, outs, *sems):
        for cp in copies("from_sibling", outs, *sems):
            cp.wait_recv()
        for cp in copies("sends", outs, *sems) + copies("passed", outs, *sems):
            cp.wait_send()

    return _Side(slots, [_sds(s.shape, s.dtype) for s in slots], 6 * n, first, last, mid=mid, aliases={w: w for w in range(n)})


def _swap_side(grads):
    n = len(grads)

    def copies(ins, outs, send_sems, recv_sems):
        x, y, c, _ = _place()
        return [pltpu.make_async_remote_copy(src_ref=ins[w].at[:, 1 - c], dst_ref=outs[w], send_sem=send_sems.at[w],
                                             recv_sem=recv_sems.at[w], device_id=(x, y, 1 - c), device_id_type=MESH)
                for w in range(n)]

    def first(*refs):
        for cp in copies(*refs):
            cp.start()

    def last(*refs):
        for cp in copies(*refs):
            cp.wait()

    return _Side(grads, [_sds((g.shape[0], *g.shape[2:]), g.dtype) for g in grads], n, first, last)


def _scatter_side(halves):
    n = len(halves)

    def copies(ins, outs, send_sems, recv_sems):
        x, y, c, chips = _place()
        return [pltpu.make_async_remote_copy(
            src_ref=ins[w].at[2 * cx + cy], dst_ref=outs[w].at[j], send_sem=send_sems.at[3 * w + j],
            recv_sem=recv_sems.at[3 * w + j], device_id=(cx, cy, c), device_id_type=MESH)
            for w in range(n) for j, (cx, cy) in enumerate(chips)]

    def first(*refs):
        for cp in copies(*refs):
            cp.start()

    def last(*refs):
        for cp in copies(*refs):
            cp.wait()

    return _Side(halves, [_sds((3, *h.shape[1:]), h.dtype) for h in halves], 3 * n, first, last)


def _join_halves(name, slots):
    n = len(slots)

    def body(*refs):
        outs = refs[n:2 * n]
        send_sems, recv_sems = refs[2 * n:]
        x, y, c, _ = _place()

        def copy(w, half, to):
            return pltpu.make_async_remote_copy(src_ref=outs[w].at[half], dst_ref=outs[w].at[half], send_sem=send_sems.at[w],
                                                recv_sem=recv_sems.at[w], device_id=to, device_id_type=MESH)

        copies = [copy(w, c, (x, y, 1 - c)) for w in range(n)]
        for cp in copies:
            cp.start()
        for w in range(n):
            copy(w, 1 - c, (x, y, c)).wait_recv()
        for cp in copies:
            cp.wait_send()

    return pl.pallas_call(
        body, name=name, in_specs=[ANY] * n, out_specs=[ANY] * n,
        out_shape=[_sds(s.shape, s.dtype) for s in slots], input_output_aliases={w: w for w in range(n)},
        scratch_shapes=[pltpu.SemaphoreType.DMA((n,)), pltpu.SemaphoreType.DMA((n,))],
    )(*slots)


def _allreduce_small(block, tr=256):
    rows, lanes = block.shape
    tr = _pick(rows, tr, SUBLANES)

    def body(x_ref, o_ref, buf, send_sems, recv_sems):
        x, y, c, chips = _place()
        me, sibling = (x, y, c), (x, y, 1 - c)

        def slot(px, py, pc):
            return buf.at[4 * px + 2 * py + pc]

        def copy(k, block_of, to):
            return pltpu.make_async_remote_copy(src_ref=slot(*block_of), dst_ref=slot(*block_of), send_sem=send_sems.at[k],
                                                recv_sem=recv_sems.at[k], device_id=to, device_id_type=MESH)

        slot(*me)[...] = x_ref[...]
        first = [copy(0, me, sibling)] + [copy(1 + j, me, (*chip, c)) for j, chip in enumerate(chips)]
        for cp in first:
            cp.start()
        passed = [copy(4 + j, (*chip, c), sibling) for j, chip in enumerate(chips)]
        for j, chip in enumerate(chips):
            copy(1 + j, (*chip, c), me).wait_recv()
            passed[j].start()
        copy(0, sibling, me).wait_recv()
        for j, chip in enumerate(chips):
            copy(4 + j, (*chip, 1 - c), me).wait_recv()
        for cp in first + passed:
            cp.wait_send()
        for r0 in range(0, rows, tr):
            acc = buf[0, r0:r0 + tr, :]
            for k in range(1, N_DEV):
                acc = acc + buf[k, r0:r0 + tr, :]
            o_ref[r0:r0 + tr, :] = acc

    vm = pl.BlockSpec(memory_space=pltpu.VMEM)
    return pl.pallas_call(
        body, name="allreduce_small", in_specs=[vm], out_specs=vm, out_shape=_sds((rows, lanes), block.dtype),
        scratch_shapes=[pltpu.VMEM((N_DEV, rows, lanes), block.dtype), pltpu.SemaphoreType.DMA((7,)), pltpu.SemaphoreType.DMA((7,))],
        compiler_params=pltpu.CompilerParams(vmem_limit_bytes=VMEM_LIMIT),
    )(block)


def _sum_received(name, own, received, c, tr=256):
    n, rows, cols = received.shape
    tr = _pick(rows, tr, 16)

    def body(c_ref, a_ref, s_ref, o_ref):
        acc = a_ref[...]
        for k in range(n):
            acc = acc + s_ref[k].astype(F32)
        o_ref[...] = acc

    grid_spec = pltpu.PrefetchScalarGridSpec(
        num_scalar_prefetch=1, grid=(rows // tr,),
        in_specs=[pl.BlockSpec((tr, cols), lambda i, c_ref: (i, 0)), pl.BlockSpec((n, tr, cols), lambda i, c_ref: (0, i, 0))],
        out_specs=pl.BlockSpec((None, tr, cols), lambda i, c_ref: (c_ref[0], i, 0)))
    return pl.pallas_call(body, name=name, grid_spec=grid_spec, out_shape=_sds((2, rows, cols), F32),
                          compiler_params=_params(("arbitrary",)))(c.reshape(1).astype(jnp.int32), own, received)


def _add_halves(name, full, c, received, tr=256):
    s, _, rh, cols = full.shape
    tr = _pick(rh, tr, 16)

    def body(c_ref, a_ref, b_ref, o_ref):
        o_ref[...] = (a_ref[...] + b_ref[...]).astype(BF16)

    grid_spec = pltpu.PrefetchScalarGridSpec(
        num_scalar_prefetch=1, grid=(s, rh // tr),
        in_specs=[pl.BlockSpec((None, None, tr, cols), lambda q, i, c_ref: (q, c_ref[0], i, 0)),
                  pl.BlockSpec((None, tr, cols), lambda q, i, c_ref: (q, i, 0))],
        out_specs=pl.BlockSpec((None, tr, cols), lambda q, i, c_ref: (q, i, 0)))
    return pl.pallas_call(body, name=name, grid_spec=grid_spec, out_shape=_sds((s, rh, cols), BF16),
                          compiler_params=_params(("arbitrary", "arbitrary")))(c.reshape(1).astype(jnp.int32), full, received)


def _own_half(name, full, c, shard, received, tr=256):
    _, _, rh, cols = full.shape
    tr = _pick(rh, tr, SUBLANES)

    def body(i_ref, a_ref, b_ref, o_ref):
        o_ref[...] = a_ref[...] + b_ref[...]

    grid_spec = pltpu.PrefetchScalarGridSpec(
        num_scalar_prefetch=1, grid=(rh // tr,),
        in_specs=[pl.BlockSpec((None, None, tr, cols), lambda i, i_ref: (i_ref[1], i_ref[0], i, 0)),
                  pl.BlockSpec((None, tr, cols), lambda i, i_ref: (i_ref[1], i, 0))],
        out_specs=pl.BlockSpec((tr, cols), lambda i, i_ref: (i, 0)))
    return pl.pallas_call(body, name=name, grid_spec=grid_spec, out_shape=_sds((rh, cols), F32),
                          compiler_params=_params(("arbitrary",)))(jnp.stack([c, shard]).astype(jnp.int32), full, received)


LARGE = ("w_in", "ssm_glu_w", "w_out", "w_ffn_in", "w_ffn_out", "w_ple_gate", "w_ple_proj")
COLUMN_SHARDED = ("w_in", "w_ffn_in", "w_ple_proj")
SMALL = ("norm_mix_g", "ssm_lambda_re", "ssm_lambda_im", "ssm_log_step", "ssm_b_re", "ssm_b_im", "ssm_c_re", "ssm_c_im",
         "ssm_d", "ssm_glu_b", "sgu_ln_g", "sgu_ln_b", "sgu_w", "sgu_b", "out_norm_ssm_g", "out_norm_sgu_g", "norm_ffn_g",
         "norm_ple_g", "b_ple_gate", "final_norm_g")
WEIGHTS = ("norm_mix_g", "w_in", "ssm_lambda_re", "ssm_lambda_im", "ssm_log_step", "ssm_b_re", "ssm_b_im", "ssm_c_re",
           "ssm_c_im", "ssm_d", "ssm_glu_w", "ssm_glu_b", "sgu_ln_g", "sgu_ln_b", "sgu_w", "sgu_b", "out_norm_ssm_g",
           "out_norm_sgu_g", "w_out", "norm_ffn_g", "w_ffn_in", "w_ffn_out", "norm_ple_g", "w_ple_gate", "b_ple_gate",
           "w_ple_proj", "final_norm_g")
PACK_ROWS = SUBLANES * LANES


def _pack(arrays):
    parts = []
    for a in arrays:
        flat = a.reshape(-1).astype(F32)
        pad = -flat.shape[0] % PACK_ROWS
        parts.append(jnp.pad(flat, (0, pad)) if pad else flat)
    return jnp.concatenate(parts).reshape(-1, LANES)


def _unpack(packed, like):
    flat = packed.reshape(-1)
    out, at = [], 0
    for a in like:
        size = a.size
        out.append(flat[at:at + size].reshape(a.shape))
        at += size + (-size % PACK_ROWS)
    return out


class _NoExchange:
    def __init__(self, weights):
        self.weights, self.grads = weights, {}

    def weight(self, name):
        return self.weights[name]

    def grad(self, name, g):
        self.grads[name] = g

    def side(self, host):
        return None


class _MeshExchange:
    GATHER = {"proj_in": ("ssm_glu_w", "w_out"), "ssm_fwd": ("w_ffn_in",), "ffn_in": ("w_ffn_out", "w_ple_gate", "w_ple_proj")}
    SWAP = {"d_act": ("w_ple_proj", "w_ple_gate", "w_ffn_out"), "d_h2": ("w_ffn_in",), "d_sgu": ("w_out", "ssm_glu_w"),
            "d_h1": ("w_in",)}
    SCATTER = {"d_ffn_in": ("w_ple_proj", "w_ple_gate", "w_ffn_out"), "ssm_bwd": ("w_ffn_in",),
               "d_proj_in": ("w_out", "ssm_glu_w"), "adamw_w_ffn_in": ("w_in",)}
    GROUPS = (("w_ple_proj", "w_ple_gate", "w_ffn_out"), ("w_ffn_in",), ("w_out", "ssm_glu_w"), ("w_in",))

    def __init__(self, shards, c, shard):
        self.c, self.shard = c, shard
        self.slots = {k: _cast_into_slot("cast_" + k, shards[k], shard) for k in LARGE}
        (self.slots["w_in"],) = _allgather_weights([self.slots["w_in"]])
        self.full, self.received, self.halves, self.quarters = {}, {}, {}, {}

    def weight(self, name):
        g = self.slots[name]
        _, _, rh, cols = g.shape
        return g.reshape(N_CHIPS, 2 * rh, cols) if name in COLUMN_SHARDED else g.reshape(N_CHIPS * 2 * rh, cols)

    def grad(self, name, g):
        if name not in COLUMN_SHARDED:
            g = g.reshape(N_CHIPS, g.shape[0] // N_CHIPS, g.shape[1])
        self.full[name] = g.reshape(N_CHIPS, 2, g.shape[1] // 2, g.shape[2])

    def side(self, host):
        if host in self.GATHER:
            return _gather_side([self.slots[k] for k in self.GATHER[host]])
        if host in self.SWAP:
            return _swap_side([self.full[k] for k in self.SWAP[host]])
        if host in self.SCATTER:
            return _scatter_side([self.halves[k] for k in self.SCATTER[host]])
        return None

    def done(self, host, moved):
        if host in self.GATHER:
            self.slots.update(zip(self.GATHER[host], moved))
        elif host in self.SWAP:
            for k, r in zip(self.SWAP[host], moved):
                self.received[k] = r
                self.halves[k] = _add_halves("grad_add_halves_" + k, self.full[k], self.c, r)
        else:
            self.quarters.update(zip(self.SCATTER[host], moved))

    def reduced(self, group):
        own = [_own_half("grad_own_" + k, self.full[k], self.c, self.shard, self.received[k]) for k in group]
        parts = [_sum_received("grad_sum_" + k, o, self.quarters[k], self.c) for k, o in zip(group, own)]
        joined = _join_halves("grad_join_" + group[0], parts)
        return {k: j.reshape(2 * j.shape[1], j.shape[2]) for k, j in zip(group, joined)}


def kernel(x, p, norm_mix_g, w_in, ssm_lambda_re, ssm_lambda_im, ssm_log_step, ssm_b_re, ssm_b_im, ssm_c_re, ssm_c_im, ssm_d, ssm_glu_w, ssm_glu_b, sgu_ln_g, sgu_ln_b, sgu_w, sgu_b, out_norm_ssm_g, out_norm_sgu_g, w_out, norm_ffn_g, w_ffn_in, w_ffn_out, norm_ple_g, w_ple_gate, b_ple_gate, w_ple_proj, final_norm_g, loss_target, m_norm_mix_g, m_w_in, m_ssm_lambda_re, m_ssm_lambda_im, m_ssm_log_step, m_ssm_b_re, m_ssm_b_im, m_ssm_c_re, m_ssm_c_im, m_ssm_d, m_ssm_glu_w, m_ssm_glu_b, m_sgu_ln_g, m_sgu_ln_b, m_sgu_w, m_sgu_b, m_out_norm_ssm_g, m_out_norm_sgu_g, m_w_out, m_norm_ffn_g, m_w_ffn_in, m_w_ffn_out, m_norm_ple_g, m_w_ple_gate, m_b_ple_gate, m_w_ple_proj, m_final_norm_g, v_norm_mix_g, v_w_in, v_ssm_lambda_re, v_ssm_lambda_im, v_ssm_log_step, v_ssm_b_re, v_ssm_b_im, v_ssm_c_re, v_ssm_c_im, v_ssm_d, v_ssm_glu_w, v_ssm_glu_b, v_sgu_ln_g, v_sgu_ln_b, v_sgu_w, v_sgu_b, v_out_norm_ssm_g, v_out_norm_sgu_g, v_w_out, v_norm_ffn_g, v_w_ffn_in, v_w_ffn_out, v_norm_ple_g, v_w_ple_gate, v_b_ple_gate, v_w_ple_proj, v_final_norm_g):
    given = dict(locals())
    w = {k: given[k] for k in WEIGHTS}
    m = {k: given["m_" + k] for k in WEIGHTS}
    v = {k: given["v_" + k] for k in WEIGHTS}
    c = lax.axis_index("c")
    shard = 2 * lax.axis_index("x") + lax.axis_index("y")

    exch = _MeshExchange({k: w[k].reshape(w[k].shape[1:]) for k in LARGE}, c, shard)
    unlayer = lambda a: a if a.ndim == 1 else a[0]
    sp = {k: unlayer(w[k]) for k in SMALL}
    n_tok, d_model = x.shape[1:]
    loss, grad_x, small = _local_grads(x.reshape(n_tok, d_model), p.reshape(n_tok, p.shape[-1]),
                                       loss_target.reshape(n_tok, d_model), sp, exch)
    loss = lax.psum(loss, ("x", "y", "c"))

    grad_w, delta_w, new_m, new_v = {}, {}, {}, {}
    for group in exch.GROUPS:
        reduced = exch.reduced(group)
        for k in group:
            shape = w[k].shape
            two_d = lambda a: a.reshape(shape[1:])
            like = _sds(shape[1:], F32)
            d_k, m_k, v_k = _hosted(exch, _rowwise, "adamw_" + k, _adamw, [two_d(w[k]), reduced[k], two_d(m[k]), two_d(v[k])],
                                    [], [like, like, like])
            grad_w[k], delta_w[k], new_m[k], new_v[k] = (a.reshape(shape) for a in (reduced[k], d_k, m_k, v_k))

    packed_g = _allreduce_small(_pack([small[k].reshape(w[k].shape) for k in SMALL]))
    like = _sds(packed_g.shape, F32)
    d_s, m_s, v_s = _rowwise("adamw_small", _adamw, [_pack([w[k] for k in SMALL]), packed_g, _pack([m[k] for k in SMALL]),
                                                     _pack([v[k] for k in SMALL])], [], [like, like, like])
    shapes = [w[k] for k in SMALL]
    for k, g_k, d_k, m_k, v_k in zip(SMALL, _unpack(packed_g, shapes), _unpack(d_s, shapes), _unpack(m_s, shapes), _unpack(v_s, shapes)):
        grad_w[k], delta_w[k], new_m[k], new_v[k] = g_k, d_k, m_k, v_k

    return (loss, grad_x.reshape(x.shape), *[grad_w[k] for k in WEIGHTS], *[delta_w[k] for k in WEIGHTS],
            *[new_m[k] for k in WEIGHTS], *[new_v[k] for k in WEIGHTS])
```

```python
import functools

import jax
import jax.numpy as jnp
from jax import lax
from jax.experimental import pallas as pl
from jax.experimental.pallas import tpu as pltpu

F32 = jnp.float32
BF16 = jnp.bfloat16

EPS = 1e-6
LAMBDA_RE_MAX = -1e-4
ADAM_LR = 0.001
ADAM_B1 = 0.9
ADAM_B2 = 0.999
ADAM_EPS = 1e-08
ADAM_WD = 0.01
ADAM_STEP = 10

N_CHIPS = 4
N_DEV = 8
SUBLANES = 8
LANES = 128
SSM_CH_BLOCK = 256
SCAN_LANES = 256
SCAN_BLOCKS = 2
VMEM_LIMIT = 56 * 1024 * 1024

MESH = pl.DeviceIdType.MESH


def _pick(n, pref, mult):
    if n <= pref:
        return n
    t = (pref // mult) * mult
    while t >= mult:
        if n % t == 0:
            return t
        t -= mult
    return n


def _params(semantics):
    return pltpu.CompilerParams(dimension_semantics=semantics, vmem_limit_bytes=VMEM_LIMIT)


class _Cols:
    def __init__(self, arr, width, blk):
        self.arr, self.width, self.blk = arr, width, blk


def _sds(shape, dtype):
    return jax.ShapeDtypeStruct(tuple(shape), dtype)


ANY = pl.BlockSpec(memory_space=pl.ANY)


class _Side:
    def __init__(self, ins, out_shapes, n_sems, first, last, mid=None, aliases=None):
        self.ins, self.out_shapes, self.n_sems = list(ins), list(out_shapes), n_sems
        self.first, self.mid, self.last = first, mid, last
        self.aliases = dict(aliases or {})


def _call(body, side, operands, *, name, grid, in_specs, out_specs, out_shape, compiler_params, scratch_shapes=()):
    if side is None:
        return pl.pallas_call(body, name=name, grid=grid, in_specs=in_specs, out_specs=out_specs, out_shape=out_shape,
                              scratch_shapes=list(scratch_shapes), compiler_params=compiler_params)(*operands)
    single = not isinstance(out_specs, (list, tuple))
    out_specs = [out_specs] if single else list(out_specs)
    out_shape = [out_shape] if single else list(out_shape)
    n_in, n_out, n_scr = len(in_specs), len(out_specs), len(scratch_shapes)
    n_sin, n_sout = len(side.ins), len(side.out_shapes)
    steps = 1
    for g in grid:
        steps *= g

    def hosted(*refs):
        ins, s_ins = refs[:n_in], refs[n_in:n_in + n_sin]
        at = n_in + n_sin
        outs, s_outs = refs[at:at + n_out], refs[at + n_out:at + n_out + n_sout]
        scratch = refs[at + n_out + n_sout:at + n_out + n_sout + n_scr]
        sems = refs[-2:]
        step = pl.program_id(0)
        for d in range(1, len(grid)):
            step = step * grid[d] + pl.program_id(d)

        @pl.when(step == 0)
        def _():
            side.first(s_ins, s_outs, *sems)

        if side.mid is not None:
            @pl.when(step == (3 * steps) // 4)
            def _():
                side.mid(s_ins, s_outs, *sems)

        body(*ins, *outs, *scratch)

        @pl.when(step == steps - 1)
        def _():
            side.last(s_ins, s_outs, *sems)

    res = pl.pallas_call(
        hosted, name=name, grid=grid, in_specs=[*in_specs, *[ANY] * n_sin], out_specs=[*out_specs, *[ANY] * n_sout],
        out_shape=[*out_shape, *side.out_shapes], input_output_aliases={n_in + i: n_out + o for i, o in side.aliases.items()},
        scratch_shapes=[*scratch_shapes, pltpu.SemaphoreType.DMA((side.n_sems,)), pltpu.SemaphoreType.DMA((side.n_sems,))],
        compiler_params=compiler_params)(*operands, *side.ins)
    return (res[0] if single else list(res[:n_out])), list(res[n_out:])


def _rowwise(name, fn, rows, params, row_outs, acc_outs=(), tr=256, side=None):
    rows = [r if isinstance(r, _Cols) else _Cols(r, r.shape[1], 0) for r in rows]
    m = rows[0].arr.shape[0]
    tr = _pick(m, tr, 16)
    n_in = len(rows) + len(params)
    n_ro = len(row_outs)

    def body(*refs):
        vals = fn(*[r[...] for r in refs[:n_in]])
        if not isinstance(vals, (tuple, list)):
            vals = (vals,)
        outs = refs[n_in:]
        for r, v in zip(outs[:n_ro], vals[:n_ro]):
            r[...] = v.astype(r.dtype)
        first = pl.program_id(0) == 0
        for r, v in zip(outs[n_ro:], vals[n_ro:]):
            @pl.when(first)
            def _():
                r[...] = jnp.zeros(r.shape, r.dtype)
            r[...] += v.astype(r.dtype).reshape(r.shape)

    in_specs = [pl.BlockSpec((tr, r.width), lambda i, b=r.blk: (i, b)) for r in rows]
    in_specs += [pl.BlockSpec(p.shape, lambda i, nd=p.ndim: (0,) * nd) for p in params]
    out_specs = [pl.BlockSpec((tr, o.shape[1]), lambda i: (i, 0)) for o in row_outs]
    out_specs += [pl.BlockSpec(o.shape, lambda i, nd=len(o.shape): (0,) * nd) for o in acc_outs]
    return _call(body, side, [*[r.arr for r in rows], *params], name=name, grid=(m // tr,), in_specs=in_specs,
                 out_specs=out_specs, out_shape=[*row_outs, *acc_outs], compiler_params=_params(("arbitrary",)))


def _whole(name, fn, ins, outs):
    n_in = len(ins)

    def body(*refs):
        vals = fn(*[r[...] for r in refs[:n_in]])
        if not isinstance(vals, (tuple, list)):
            vals = (vals,)
        for r, v in zip(refs[n_in:], vals):
            r[...] = v.astype(r.dtype).reshape(r.shape)

    vm = pl.BlockSpec(memory_space=pltpu.VMEM)
    return pl.pallas_call(body, name=name, in_specs=[vm] * n_in, out_specs=[vm] * len(outs), out_shape=list(outs),
                          compiler_params=pltpu.CompilerParams(vmem_limit_bytes=VMEM_LIMIT))(*ins)


def _grid_order(swap):
    if not swap:
        return (lambda grid: grid), (lambda f: f)
    return (lambda grid: grid[::-1]), (lambda f: (lambda j, i: f(i, j)))


def _mm_nn(name, a, w, *, sharded=False, res=None, out_dtype=F32, tm=512, tn=512, w_resident=False, side=None):
    m, k = a.shape
    tm = _pick(m, tm, 16)
    order, ix = _grid_order(w_resident)
    if sharded:
        s, _, ns = w.shape
        n = s * ns
        tn = _pick(ns, tn, LANES)
        per = ns // tn
        w_spec = pl.BlockSpec((None, k, tn), ix(lambda i, j: (j // per, 0, j % per)))
    else:
        n = w.shape[1]
        tn = _pick(n, tn, LANES)
        w_spec = pl.BlockSpec((k, tn), ix(lambda i, j: (0, j)))

    def body(a_ref, w_ref, *rest):
        acc = jnp.dot(a_ref[...], w_ref[...], preferred_element_type=F32)
        if res is not None:
            acc = acc + rest[0][...]
        rest[-1][...] = acc.astype(out_dtype)

    in_specs = [pl.BlockSpec((tm, k), ix(lambda i, j: (i, 0))), w_spec]
    ops = [a, w]
    if res is not None:
        in_specs.append(pl.BlockSpec((tm, tn), ix(lambda i, j: (i, j))))
        ops.append(res)
    return _call(body, side, ops, name=name, grid=order((m // tm, n // tn)), in_specs=in_specs,
                 out_specs=pl.BlockSpec((tm, tn), ix(lambda i, j: (i, j))), out_shape=_sds((m, n), out_dtype),
                 compiler_params=_params(("arbitrary", "arbitrary")))


def _mm_nt(name, g, w, *, sharded=False, g_halves=False, tm=512, tk=512, w_resident=False, side=None):
    m, n = g.shape[-2:]
    tm = _pick(m, tm, 16)
    order, ix = _grid_order(w_resident)
    dims = (((1,), (1,)), ((), ()))
    g_spec = pl.BlockSpec((2, tm, n), ix(lambda i, j: (0, i, 0))) if g_halves else pl.BlockSpec((tm, n), ix(lambda i, j: (i, 0)))
    if sharded:
        s, k, ns = w.shape
        tk = _pick(k, tk, LANES)
        w_spec = pl.BlockSpec((s, tk, ns), ix(lambda i, j: (0, j, 0)))

        def columns(g_ref, q):
            if not g_halves:
                return g_ref[:, q * ns:(q + 1) * ns]
            half, at = divmod(q, s // 2)
            return g_ref[half, :, at * ns:(at + 1) * ns]

        def body(g_ref, w_ref, o_ref):
            acc = lax.dot_general(columns(g_ref, 0), w_ref[0], dims, preferred_element_type=F32)
            for q in range(1, s):
                acc = acc + lax.dot_general(columns(g_ref, q), w_ref[q], dims, preferred_element_type=F32)
            o_ref[...] = acc
    else:
        k = w.shape[0]
        tk = _pick(k, tk, LANES)
        w_spec = pl.BlockSpec((tk, n), ix(lambda i, j: (j, 0)))

        def body(g_ref, w_ref, o_ref):
            o_ref[...] = lax.dot_general(g_ref[...], w_ref[...], dims, preferred_element_type=F32)

    return _call(body, side, [g, w], name=name, grid=order((m // tm, k // tk)), in_specs=[g_spec, w_spec],
                 out_specs=pl.BlockSpec((tm, tk), ix(lambda i, j: (i, j))), out_shape=_sds((m, k), F32),
                 compiler_params=_params(("arbitrary", "arbitrary")))


def _mm_tn(name, a, g, *, shards=0, g_halves=False, tk=512, tn=512, g_resident=False, side=None):
    m, k = a.shape
    n = 2 * g.shape[2] if g_halves else g.shape[1]
    tk = _pick(k, tk, LANES)
    order, ix = _grid_order(g_resident)
    dims = (((0,), (0,)), ((), ()))
    if shards:
        ns = n // shards
        tn = _pick(ns, tn, LANES)
        per = ns // tn
        out_spec = pl.BlockSpec((None, tk, tn), ix(lambda i, j: (j // per, i, j % per)))
        out_shape = _sds((shards, k, ns), F32)
    else:
        tn = _pick(n, tn, LANES)
        out_spec = pl.BlockSpec((tk, tn), ix(lambda i, j: (i, j)))
        out_shape = _sds((k, n), F32)

    def body(a_ref, g_ref, o_ref):
        o_ref[...] = lax.dot_general(a_ref[...], g_ref[...], dims, preferred_element_type=F32)

    if g_halves:
        per_half = n // 2 // tn
        g_spec = pl.BlockSpec((None, m, tn), ix(lambda i, j: (j // per_half, 0, j % per_half)))
    else:
        g_spec = pl.BlockSpec((m, tn), ix(lambda i, j: (0, j)))
    return _call(body, side, [a, g], name=name, grid=order((k // tk, n // tn)),
                 in_specs=[pl.BlockSpec((m, tk), ix(lambda i, j: (0, i))), g_spec],
                 out_specs=out_spec, out_shape=out_shape, compiler_params=_params(("arbitrary", "arbitrary")))


def _ffn_in_swiglu(name, a, w, *, tm=512, tn=1408, side=None):
    m, k = a.shape
    s, _, ns = w.shape
    f = s * ns // 2
    tm = _pick(m, tm, 16)
    tn = _pick(ns, tn, LANES)
    per = ns // tn
    order, ix = _grid_order(True)

    def body(a_ref, wg_ref, wu_ref, act_ref, gu_ref):
        x = a_ref[...]
        gate = jnp.dot(x, wg_ref[...], preferred_element_type=F32)
        up = jnp.dot(x, wu_ref[...], preferred_element_type=F32)
        act_ref[...] = _swiglu(gate, up).astype(BF16)
        gu_ref[0] = gate.astype(BF16)
        gu_ref[1] = up.astype(BF16)

    return _call(body, side, [a, w, w], name=name, grid=order((m // tm, f // tn)),
                 in_specs=[pl.BlockSpec((tm, k), ix(lambda i, j: (i, 0))),
                           pl.BlockSpec((None, k, tn), ix(lambda i, j: (j // per, 0, j % per))),
                           pl.BlockSpec((None, k, tn), ix(lambda i, j: (s // 2 + j // per, 0, j % per)))],
                 out_specs=[pl.BlockSpec((tm, tn), ix(lambda i, j: (i, j))), pl.BlockSpec((2, tm, tn), ix(lambda i, j: (0, i, j)))],
                 out_shape=[_sds((m, f), BF16), _sds((2, m, f), BF16)], compiler_params=_params(("arbitrary", "arbitrary")))


def _d_act_swiglu(name, g, w, gu, *, tm=1024, tk=512, side=None):
    m, n = g.shape
    f = w.shape[0]
    tm = _pick(m, tm, 16)
    tk = _pick(f, tk, LANES)
    dims = (((1,), (1,)), ((), ()))

    def body(g_ref, w_ref, gu_ref, o_ref):
        dact = lax.dot_general(g_ref[...], w_ref[...], dims, preferred_element_type=F32)
        _, vjp = jax.vjp(_swiglu, gu_ref[0].astype(F32), gu_ref[1].astype(F32))
        dgate, dup = vjp(dact)
        o_ref[0] = dgate.astype(BF16)
        o_ref[1] = dup.astype(BF16)

    return _call(body, side, [g, w, gu], name=name, grid=(m // tm, f // tk),
                 in_specs=[pl.BlockSpec((tm, n), lambda i, j: (i, 0)), pl.BlockSpec((tk, n), lambda i, j: (j, 0)),
                           pl.BlockSpec((2, tm, tk), lambda i, j: (0, i, j))],
                 out_specs=pl.BlockSpec((2, tm, tk), lambda i, j: (0, i, j)), out_shape=_sds((2, m, f), BF16),
                 compiler_params=_params(("arbitrary", "arbitrary")))


def _rms(x, g):
    r = lax.rsqrt(jnp.mean(x * x, axis=-1, keepdims=True) + EPS)
    return (x * r) * g


def _glu_out(y_pre, q, glu_b, g_norm):
    ya0 = jax.nn.gelu(y_pre)
    return _rms(ya0 * jax.nn.sigmoid(q + glu_b), g_norm)


def _sgu_rows(zu, zv, ln_g, ln_b, w_s, b_st, g_norm):
    heads, t, _ = w_s.shape
    hd = zu.shape[1] // heads
    uu = jax.nn.gelu(zu)
    vv = jax.nn.gelu(zv)
    mu = jnp.mean(vv, axis=-1, keepdims=True)
    xc = vv - mu
    r = lax.rsqrt(jnp.mean(xc * xc, axis=-1, keepdims=True) + EPS)
    vn = (xc * r) * ln_g + ln_b
    row = lax.broadcasted_iota(jnp.int32, (t, t), 0)
    col = lax.broadcasted_iota(jnp.int32, (t, t), 1)
    causal = row >= col
    chunks = []
    for n in range(zu.shape[0] // t):
        blocks = []
        for h in range(heads):
            wm = jnp.where(causal, w_s[h], jnp.zeros_like(w_s[h])).astype(BF16)
            vb = vn[n * t:(n + 1) * t, h * hd:(h + 1) * hd].astype(BF16)
            blocks.append(jnp.dot(wm, vb, preferred_element_type=F32) + b_st[:, h:h + 1])
        chunks.append(jnp.concatenate(blocks, axis=1))
    s = jnp.concatenate(chunks, axis=0) if len(chunks) > 1 else chunks[0]
    return _rms(uu * s, g_norm)


def _swiglu(gate, up):
    return jax.nn.silu(gate) * up


def _head_loss(x2, gpre, pp, b_g, g_final, target):
    gate = jax.nn.sigmoid(gpre + b_g)
    out = _rms(x2 + gate * pp, g_final)
    err = jnp.square(out - target)
    return 0.5 * jnp.sum(jnp.mean(err, axis=-1))


def _ssm_disc(lam_re, lam_im, log_step_col):
    lr = jnp.minimum(lam_re, LAMBDA_RE_MAX)
    li = lam_im
    dt = jnp.exp(log_step_col)
    mag = jnp.exp(lr * dt)
    ang = li * dt
    abar_re = mag * jnp.cos(ang)
    abar_im = mag * jnp.sin(ang)
    nr = abar_re - 1.0
    ni = abar_im
    den = lr * lr + li * li
    q_re = (nr * lr + ni * li) / den
    q_im = (ni * lr - nr * li) / den
    return abar_re, abar_im, q_re, q_im


def _ssm_bbar(q_re_col, q_im_col, b_re, b_im):
    return q_re_col * b_re - q_im_col * b_im, q_re_col * b_im + q_im_col * b_re


def _adamw(w, g, m, v):
    m = ADAM_B1 * m + (1.0 - ADAM_B1) * g
    v = ADAM_B2 * v + (1.0 - ADAM_B2) * jnp.square(g)
    m_hat = m / (1.0 - ADAM_B1 ** ADAM_STEP)
    v_hat = v / (1.0 - ADAM_B2 ** ADAM_STEP)
    delta = -ADAM_LR * (m_hat / (jnp.sqrt(v_hat) + ADAM_EPS) + ADAM_WD * w)
    return delta, m, v


class _SsmDims:
    def __init__(self, groups, state, gch):
        self.g, self.p, self.h = groups, state, gch
        self.d = groups * gch
        self.cb = min(SSM_CH_BLOCK, self.d)
        self.gb = self.cb // gch
        self.ns = self.gb * state
        self.nb = self.d // self.cb


def _ssm_forward_params(sd, lam_re, lam_im, log_step, b_re, b_im):
    gp = sd.g * sd.p

    def disc(lr, li, ls):
        ar, ai, qr, qi = _ssm_disc(lr, li, ls)
        pr, pi_ = [ar], [ai]
        for _ in range(SUBLANES - 1):
            pr, pi_ = pr + [pr[-1] * ar - pi_[-1] * ai], pi_ + [pr[-1] * ai + pi_[-1] * ar]
        return ar, ai, qr, qi, jnp.concatenate(pr, axis=0), jnp.concatenate(pi_, axis=0)

    gp_s = _sds((sd.g, sd.p), F32)
    pw_s = _sds((SUBLANES * sd.g, sd.p), F32)
    ar, ai, qr, qi, pw_re, pw_im = _whole("ssm_disc", disc, [lam_re, lam_im, log_step.reshape(sd.g, 1)],
                                          [gp_s, gp_s, gp_s, gp_s, pw_s, pw_s])
    qr_col, qi_col = qr.reshape(gp, 1), qi.reshape(gp, 1)
    bb_s = _sds((gp, sd.h), F32)
    bbar_re, bbar_im = _whole("ssm_bbar", _ssm_bbar, [qr_col, qi_col, b_re.reshape(gp, sd.h), b_im.reshape(gp, sd.h)],
                              [bb_s, bb_s])
    return qr_col, qi_col, bbar_re, bbar_im, pw_re.reshape(SUBLANES, sd.g, sd.p), pw_im.reshape(SUBLANES, sd.g, sd.p)


def _blockdiag_in(sd, bbar):
    b = bbar.reshape(sd.nb, sd.gb, sd.p, sd.h).transpose(0, 1, 3, 2)
    eye = jnp.eye(sd.gb, dtype=bbar.dtype)
    return (b[:, :, :, None, :] * eye[None, :, None, :, None]).reshape(sd.nb, sd.cb, sd.ns)


def _blockdiag_out(sd, c):
    cc = c.reshape(sd.nb, sd.gb, sd.h, sd.p).transpose(0, 1, 3, 2)
    eye = jnp.eye(sd.gb, dtype=c.dtype)
    return (cc[:, :, :, None, :] * eye[None, :, None, :, None]).reshape(sd.nb, sd.ns, sd.cb)


def _diag_in(sd, dense):
    x = dense.reshape(sd.nb, sd.gb, sd.h, sd.gb, sd.p)
    return jnp.einsum("jghgp->jgph", x).reshape(sd.g * sd.p, sd.h)


def _diag_out(sd, dense):
    x = dense.reshape(sd.nb, sd.gb, sd.p, sd.gb, sd.h)
    return jnp.einsum("jgpgh->jghp", x).reshape(sd.g, sd.h, sd.p)


def _scan_consts(sd, pw_re, pw_im, reverse):
    pr = pw_re.reshape(SUBLANES, sd.nb, sd.ns)
    pi_ = pw_im.reshape(SUBLANES, sd.nb, sd.ns)
    if reverse:
        pi_ = -pi_
    rows = jnp.arange(SUBLANES)[None, :, None]
    parts = []
    for d in (1, 2, 4):
        keep = (rows < SUBLANES - d) if reverse else (rows >= d)
        parts += [jnp.where(keep, pr[d - 1][:, None, :], 0.0), jnp.where(keep, pi_[d - 1][:, None, :], 0.0)]
    cr, ci = pr.transpose(1, 0, 2), pi_.transpose(1, 0, 2)
    if reverse:
        cr, ci = cr[:, ::-1, :], ci[:, ::-1, :]
    return jnp.concatenate(parts + [cr, ci], axis=1).astype(F32)


def _block_scan(s_ref, cst_ref, carry_ref, sd, rows, reverse):
    ns = sd.ns
    nblk = rows // SUBLANES
    w = min(SCAN_LANES, ns)
    for c0 in range(0, ns, w):
        re_l, im_l = slice(c0, c0 + w), slice(ns + c0, ns + c0 + w)
        cst = [cst_ref[k * SUBLANES:(k + 1) * SUBLANES, c0:c0 + w] for k in range(8)]

        def step(k, carry, re_l=re_l, im_l=im_l, cst=cst):
            local = []
            for b in range(SCAN_BLOCKS):
                blk = SCAN_BLOCKS * k + b
                blk = (nblk - 1 - blk) if reverse else blk
                r0 = pl.multiple_of(blk * SUBLANES, SUBLANES)
                xr = s_ref[pl.ds(r0, SUBLANES), re_l]
                xi = s_ref[pl.ds(r0, SUBLANES), im_l]
                for n, d in enumerate((1, 2, 4)):
                    ar, ai = cst[2 * n], cst[2 * n + 1]
                    shift = (SUBLANES - d) if reverse else d
                    sr = pltpu.roll(xr, shift, 0)
                    si = pltpu.roll(xi, shift, 0)
                    xr, xi = xr + ar * sr - ai * si, xi + ar * si + ai * sr
                local.append((r0, xr, xi))
            cr, ci = carry
            edge = slice(0, 1) if reverse else slice(SUBLANES - 1, SUBLANES)
            for r0, xr, xi in local:
                br = jnp.broadcast_to(cr, xr.shape)
                bi = jnp.broadcast_to(ci, xi.shape)
                xr, xi = xr + cst[6] * br - cst[7] * bi, xi + cst[6] * bi + cst[7] * br
                s_ref[pl.ds(r0, SUBLANES), re_l] = xr
                s_ref[pl.ds(r0, SUBLANES), im_l] = xi
                cr, ci = xr[edge, :], xi[edge, :]
            return cr, ci

        cr, ci = lax.fori_loop(0, nblk // SCAN_BLOCKS, step, (carry_ref[0:1, re_l], carry_ref[0:1, im_l]))
        carry_ref[0:1, re_l] = cr
        carry_ref[0:1, im_l] = ci


def _ssm_fwd(name, sd, z, wb, wc, cst, d_row, tt=512, side=None):
    n_tok = z.shape[0]
    tt = _pick(n_tok, tt, SUBLANES)
    cb, ns2 = sd.cb, 2 * sd.ns

    def body(z_ref, wb_ref, wc_ref, cst_ref, d_ref, y_ref, s_ref, carry_ref):
        @pl.when(pl.program_id(1) == 0)
        def _():
            carry_ref[...] = jnp.zeros(carry_ref.shape, F32)
        u = z_ref[...]
        s_ref[...] = jnp.dot(u.astype(BF16), wb_ref[...], preferred_element_type=F32)
        _block_scan(s_ref, cst_ref, carry_ref, sd, tt, reverse=False)
        y = jnp.dot(s_ref[...].astype(BF16), wc_ref[...], preferred_element_type=F32)
        y_ref[...] = y + d_ref[...] * u

    return _call(
        body, side, [z, wb, wc, cst, d_row], name=name, grid=(sd.nb, n_tok // tt),
        in_specs=[pl.BlockSpec((tt, cb), lambda j, i: (i, j)),
                  pl.BlockSpec((None, cb, ns2), lambda j, i: (j, 0, 0)),
                  pl.BlockSpec((None, ns2, cb), lambda j, i: (j, 0, 0)),
                  pl.BlockSpec((None, 8 * SUBLANES, sd.ns), lambda j, i: (j, 0, 0)),
                  pl.BlockSpec((1, cb), lambda j, i: (0, j))],
        out_specs=[pl.BlockSpec((tt, cb), lambda j, i: (i, j)), pl.BlockSpec((tt, ns2), lambda j, i: (i, j))],
        out_shape=[_sds((n_tok, sd.d), F32), _sds((n_tok, sd.nb * ns2), F32)],
        scratch_shapes=[pltpu.VMEM((SUBLANES, ns2), F32)],
        compiler_params=_params(("arbitrary", "arbitrary")))


def _ssm_bwd(name, sd, dy, z, states, wct, wbt, cst_rev, d_row, tt=512, side=None):
    n_tok = z.shape[0]
    tt = _pick(n_tok, tt, SUBLANES)
    nt = n_tok // tt
    cb, ns, ns2 = sd.cb, sd.ns, 2 * sd.ns
    blocks_per_tile = tt // SUBLANES
    tn_dims = (((0,), (0,)), ((), ()))

    def body(dy_ref, z_ref, s_ref, sp_ref, wct_ref, wbt_ref, cst_ref, d_ref,
             du_ref, dwb_ref, dwc_ref, da_ref, dd_ref, lam_ref, carry_ref):
        i = pl.program_id(1)

        @pl.when(i == 0)
        def _():
            carry_ref[...] = jnp.zeros(carry_ref.shape, F32)
            dwb_ref[...] = jnp.zeros(dwb_ref.shape, F32)
            dwc_ref[...] = jnp.zeros(dwc_ref.shape, F32)
            da_ref[...] = jnp.zeros(da_ref.shape, F32)
            dd_ref[...] = jnp.zeros(dd_ref.shape, F32)

        dy_t = dy_ref[...]
        u = z_ref[...]
        dy16 = dy_t.astype(BF16)
        lam_ref[...] = jnp.dot(dy16, wct_ref[...], preferred_element_type=F32)
        _block_scan(lam_ref, cst_ref, carry_ref, sd, tt, reverse=True)
        lam = lam_ref[...]
        lam16 = lam.astype(BF16)
        du_ref[...] = jnp.dot(lam16, wbt_ref[...], preferred_element_type=F32) + d_ref[...] * dy_t
        dd_ref[0:1, :] += jnp.sum(dy_t * u, axis=0, keepdims=True)
        dwb_ref[...] += lax.dot_general(u.astype(BF16), lam16, tn_dims, preferred_element_type=F32)
        s = s_ref[...]
        dwc_ref[...] += lax.dot_general(s.astype(BF16), dy16, tn_dims, preferred_element_type=F32)
        before = jnp.where(i == nt - 1, 0.0, 1.0) * sp_ref[SUBLANES - 1:SUBLANES, :]
        first_row = lax.broadcasted_iota(jnp.int32, s.shape, 0) == 0
        prev = jnp.where(first_row, jnp.broadcast_to(before, s.shape), pltpu.roll(s, 1, 0))
        lr, li = lam[:, :ns], lam[:, ns:]
        pr, pi_ = prev[:, :ns], prev[:, ns:]
        da_ref[0:1, 0:ns] += jnp.sum(lr * pr + li * pi_, axis=0, keepdims=True)
        da_ref[0:1, ns:ns2] += jnp.sum(li * pr - lr * pi_, axis=0, keepdims=True)

    rev = lambda i: nt - 1 - i
    return _call(
        body, side, [dy, z, states, states, wct, wbt, cst_rev, d_row], name=name, grid=(sd.nb, nt),
        in_specs=[pl.BlockSpec((tt, cb), lambda j, i: (rev(i), j)),
                  pl.BlockSpec((tt, cb), lambda j, i: (rev(i), j)),
                  pl.BlockSpec((tt, ns2), lambda j, i: (rev(i), j)),
                  pl.BlockSpec((SUBLANES, ns2), lambda j, i: (jnp.maximum(rev(i) * blocks_per_tile - 1, 0), j)),
                  pl.BlockSpec((None, cb, ns2), lambda j, i: (j, 0, 0)),
                  pl.BlockSpec((None, ns2, cb), lambda j, i: (j, 0, 0)),
                  pl.BlockSpec((None, 8 * SUBLANES, ns), lambda j, i: (j, 0, 0)),
                  pl.BlockSpec((1, cb), lambda j, i: (0, j))],
        out_specs=[pl.BlockSpec((tt, cb), lambda j, i: (rev(i), j)),
                   pl.BlockSpec((None, cb, ns2), lambda j, i: (j, 0, 0)),
                   pl.BlockSpec((None, ns2, cb), lambda j, i: (j, 0, 0)),
                   pl.BlockSpec((None, SUBLANES, ns2), lambda j, i: (j, 0, 0)),
                   pl.BlockSpec((None, SUBLANES, cb), lambda j, i: (j, 0, 0))],
        out_shape=[_sds((n_tok, sd.d), F32), _sds((sd.nb, cb, ns2), F32), _sds((sd.nb, ns2, cb), F32),
                   _sds((sd.nb, SUBLANES, ns2), F32), _sds((sd.nb, SUBLANES, cb), F32)],
        scratch_shapes=[pltpu.VMEM((tt, ns2), F32), pltpu.VMEM((SUBLANES, ns2), F32)],
        compiler_params=_params(("arbitrary", "arbitrary")))


def _hosted(exch, fn, name, *args, **kw):
    side = exch.side(name)
    if side is None:
        return fn(name, *args, **kw)
    out, moved = fn(name, *args, side=side, **kw)
    exch.done(name, moved)
    return out


def _local_grads(x, p, target, sp, exch):
    n_tok, d_model = x.shape
    d_ssm = sp["ssm_d"].shape[0] * sp["ssm_d"].shape[1]
    d_sgu = sp["sgu_ln_g"].shape[-1]
    sd = _SsmDims(sp["ssm_b_re"].shape[0], sp["ssm_b_re"].shape[1], sp["ssm_b_re"].shape[2])
    heads, chunk, _ = sp["sgu_w"].shape
    row = lambda v: v.reshape(1, -1)
    tok = lambda w, dt=F32: _sds((n_tok, w), dt)
    acc = lambda w: _sds((1, w), F32)

    g_mix = row(sp["norm_mix_g"])
    (h1,) = _rowwise("norm_mix", lambda a, g: _rms(a, g), [x], [g_mix], [tok(d_model, BF16)])
    z = _hosted(exch, _mm_nn, "proj_in", h1, exch.weight("w_in"), sharded=True, tn=768)

    qr_col, qi_col, bbar_re, bbar_im, pw_re, pw_im = _ssm_forward_params(
        sd, sp["ssm_lambda_re"], sp["ssm_lambda_im"], sp["ssm_log_step"], sp["ssm_b_re"], sp["ssm_b_im"])
    wb = jnp.concatenate([_blockdiag_in(sd, bbar_re), _blockdiag_in(sd, bbar_im)], axis=2).astype(BF16)
    wc = jnp.concatenate([_blockdiag_out(sd, sp["ssm_c_re"]), -_blockdiag_out(sd, sp["ssm_c_im"])], axis=1).astype(BF16)
    d_row = row(sp["ssm_d"])
    y_pre, states = _hosted(exch, _ssm_fwd, "ssm_fwd", sd, z, wb, wc, _scan_consts(sd, pw_re, pw_im, False), d_row)

    (ya0_16,) = _rowwise("ssm_gelu", lambda a: jax.nn.gelu(a), [y_pre], [], [tok(d_ssm, BF16)])
    q = _mm_nn("ssm_glu", ya0_16, exch.weight("ssm_glu_w"), tm=1024)
    glu_b, g_ossm = row(sp["ssm_glu_b"]), row(sp["out_norm_ssm_g"])
    (ya_n,) = _rowwise("ssm_glu_out", _glu_out, [y_pre, q], [glu_b, g_ossm], [tok(d_ssm, BF16)])

    assert d_ssm == d_sgu
    zu, zv = _Cols(z, d_sgu, 1), _Cols(z, d_sgu, 2)
    ln_g, ln_b, g_osgu = row(sp["sgu_ln_g"]), row(sp["sgu_ln_b"]), row(sp["out_norm_sgu_g"])
    b_st = sp["sgu_b"].T
    sgu_tr = 2 * chunk
    (yb_n,) = _rowwise("sgu", _sgu_rows, [zu, zv], [ln_g, ln_b, sp["sgu_w"], b_st, g_osgu], [tok(d_sgu, BF16)], tr=sgu_tr)

    ycat = jnp.concatenate([ya_n, yb_n], axis=1)
    x1 = _mm_nn("proj_out", ycat, exch.weight("w_out"), res=x, tm=1024)

    g_ffn = row(sp["norm_ffn_g"])
    (h2,) = _rowwise("norm_ffn", lambda a, g: _rms(a, g), [x1], [g_ffn], [tok(d_model, BF16)])
    act, gu16 = _hosted(exch, _ffn_in_swiglu, "ffn_in", h2, exch.weight("w_ffn_in"))
    x2 = _mm_nn("ffn_out", act, exch.weight("w_ffn_out"), res=x1)

    g_ple = row(sp["norm_ple_g"])
    (h3,) = _rowwise("norm_ple", lambda a, g: _rms(a, g), [x2], [g_ple], [tok(d_model, BF16)])
    gpre = _mm_nn("ple_gate", h3, exch.weight("w_ple_gate"), tm=1024)
    (p16,) = _rowwise("ple_cast", lambda a: a, [p], [], [tok(p.shape[1], BF16)])
    pp = _mm_nn("ple_proj", p16, exch.weight("w_ple_proj"), sharded=True, tm=1024)

    b_g, g_fin = row(sp["b_ple_gate"]), row(sp["final_norm_g"])

    def head(x2_t, gpre_t, pp_t, tgt_t, b_g_v, g_fin_v):
        loss, grads = jax.value_and_grad(_head_loss, argnums=(0, 1, 2, 3, 4))(x2_t, gpre_t, pp_t, b_g_v, g_fin_v, tgt_t)
        dx2, dgpre, dpp, db, dg = grads
        return dx2, dgpre.astype(BF16), dpp.astype(BF16), jnp.full((1, LANES), loss, F32), db, dg

    dx2_head, dgpre16, dpp16, loss_row, d_b_g, d_g_fin = _rowwise(
        "head", head, [x2, gpre, pp, target], [b_g, g_fin],
        [tok(d_model), tok(d_model, BF16), tok(d_model, BF16)], [acc(LANES), acc(d_model), acc(d_model)])
    loss = loss_row[0, 0]

    exch.grad("w_ple_proj", _mm_tn("d_ple_proj", p16, dpp16, shards=N_CHIPS, tk=256))
    exch.grad("w_ple_gate", _mm_tn("d_ple_gate", h3, dgpre16))
    dh3 = _mm_nt("d_h3", dgpre16, exch.weight("w_ple_gate"), tm=1024)

    def norm_bwd(x_t, dres_t, dh_t, g_v):
        _, vjp = jax.vjp(_rms, x_t, g_v)
        dx, dg = vjp(dh_t)
        dx = dres_t + dx
        return dx, dx.astype(BF16), dg

    dx2, dx2_16, d_g_ple = _rowwise("d_norm_ple", norm_bwd, [x2, dx2_head, dh3], [g_ple],
                                    [tok(d_model), tok(d_model, BF16)], [acc(d_model)])
    exch.grad("w_ffn_out", _mm_tn("d_ffn_out", act, dx2_16))
    dgu16 = _hosted(exch, _d_act_swiglu, "d_act", dx2_16, exch.weight("w_ffn_out"), gu16)
    exch.grad("w_ffn_in", _hosted(exch, _mm_tn, "d_ffn_in", h2, dgu16, shards=N_CHIPS, g_halves=True, tn=1408, g_resident=True))
    dh2 = _hosted(exch, _mm_nt, "d_h2", dgu16, exch.weight("w_ffn_in"), sharded=True, g_halves=True, tm=256, w_resident=True)
    dx1, dx1_16, d_g_ffn = _rowwise("d_norm_ffn", norm_bwd, [x1, dx2, dh2], [g_ffn],
                                    [tok(d_model), tok(d_model, BF16)], [acc(d_model)])
    exch.grad("w_out", _mm_tn("d_proj_out", ycat, dx1_16))
    dycat = _mm_nt("d_ycat", dx1_16, exch.weight("w_out"), tm=1024)

    def glu_out_bwd(y_pre_t, q_t, dy_t, glu_b_v, g_v):
        _, vjp = jax.vjp(_glu_out, y_pre_t, q_t, glu_b_v, g_v)
        dy_pre, dq, db, dg = vjp(dy_t)
        return dy_pre, dq.astype(BF16), db, dg

    dy_pre_a, dq16, d_glu_b, d_g_ossm = _rowwise(
        "d_ssm_glu_out", glu_out_bwd, [y_pre, q, _Cols(dycat, d_ssm, 0)], [glu_b, g_ossm],
        [tok(d_ssm), tok(d_ssm, BF16)], [acc(d_ssm), acc(d_ssm)])
    exch.grad("ssm_glu_w", _mm_tn("d_ssm_glu", ya0_16, dq16))
    dya0 = _mm_nt("d_ya0", dq16, exch.weight("ssm_glu_w"), tm=1024)

    def gelu_bwd(y_pre_t, dy_a_t, dya0_t):
        _, vjp = jax.vjp(jax.nn.gelu, y_pre_t)
        return dy_a_t + vjp(dya0_t)[0]

    (dy_pre,) = _rowwise("d_ssm_gelu", gelu_bwd, [y_pre, dy_pre_a, dya0], [], [tok(d_ssm)])

    wct, wbt = jnp.swapaxes(wc, 1, 2), jnp.swapaxes(wb, 1, 2)
    dz_ssm, dwb, dwc, da, dd = _hosted(exch, _ssm_bwd, "ssm_bwd", sd, dy_pre, z, states, wct, wbt,
                                       _scan_consts(sd, pw_re, pw_im, True), d_row)

    def sgu_bwd(zu_t, zv_t, dy_t, ln_g_v, ln_b_v, w_v, b_v, g_v):
        _, vjp = jax.vjp(_sgu_rows, zu_t, zv_t, ln_g_v, ln_b_v, w_v, b_v, g_v)
        dzu, dzv, dlg, dlb, dw, db, dg = vjp(dy_t)
        return dzu, dzv, dlg, dlb, dw, db, dg

    dzu, dzv, d_ln_g, d_ln_b, d_sgu_w, d_b_st, d_g_osgu = _hosted(
        exch, _rowwise, "d_sgu", sgu_bwd, [zu, zv, _Cols(dycat, d_sgu, 1)], [ln_g, ln_b, sp["sgu_w"], b_st, g_osgu],
        [tok(d_sgu, BF16), tok(d_sgu, BF16)],
        [acc(d_sgu), acc(d_sgu), _sds(sp["sgu_w"].shape, F32), _sds(b_st.shape, F32), acc(d_sgu)], tr=sgu_tr)

    (dz_ssm16,) = _rowwise("d_ssm_cast", lambda a: a, [dz_ssm], [], [tok(d_ssm, BF16)])
    dz16 = jnp.concatenate([dz_ssm16, dzu, dzv], axis=1)

    gp = sd.g * sd.p
    dbbar_re = _diag_in(sd, dwb[:, :, :sd.ns])
    dbbar_im = _diag_in(sd, dwb[:, :, sd.ns:])
    d_c_re = _diag_out(sd, dwc[:, :sd.ns, :])
    d_c_im = -_diag_out(sd, dwc[:, sd.ns:, :])
    dabar_re = da[:, 0, :sd.ns].reshape(sd.g, sd.p)
    dabar_im = da[:, 0, sd.ns:].reshape(sd.g, sd.p)
    d_ssm_d = dd[:, 0, :].reshape(sd.g, sd.h)

    def bbar_bwd(qr, qi, b_re_v, b_im_v, dre, dim):
        _, vjp = jax.vjp(_ssm_bbar, qr, qi, b_re_v, b_im_v)
        return vjp((dre, dim))

    b_re2, b_im2 = sp["ssm_b_re"].reshape(gp, sd.h), sp["ssm_b_im"].reshape(gp, sd.h)
    dq_re, dq_im, d_b_re, d_b_im = _whole(
        "d_ssm_bbar", bbar_bwd, [qr_col, qi_col, b_re2, b_im2, dbbar_re, dbbar_im],
        [_sds((gp, 1), F32), _sds((gp, 1), F32), _sds((gp, sd.h), F32), _sds((gp, sd.h), F32)])

    def disc_bwd(lr, li, ls, dar, dai, dqr, dqi):
        _, vjp = jax.vjp(_ssm_disc, lr, li, ls)
        return vjp((dar, dai, dqr, dqi))

    gp_s = _sds((sd.g, sd.p), F32)
    d_lam_re, d_lam_im, d_log_step = _whole(
        "d_ssm_disc", disc_bwd,
        [sp["ssm_lambda_re"], sp["ssm_lambda_im"], sp["ssm_log_step"].reshape(sd.g, 1),
         dabar_re, dabar_im, dq_re.reshape(sd.g, sd.p), dq_im.reshape(sd.g, sd.p)],
        [gp_s, gp_s, _sds((sd.g, 1), F32)])

    exch.small_grads({
        "ssm_lambda_re": d_lam_re, "ssm_lambda_im": d_lam_im, "ssm_log_step": d_log_step,
        "ssm_b_re": d_b_re, "ssm_b_im": d_b_im, "ssm_c_re": d_c_re, "ssm_c_im": d_c_im, "ssm_d": d_ssm_d,
        "ssm_glu_b": d_glu_b, "sgu_ln_g": d_ln_g, "sgu_ln_b": d_ln_b, "sgu_w": d_sgu_w, "sgu_b": d_b_st.T,
        "out_norm_ssm_g": d_g_ossm, "out_norm_sgu_g": d_g_osgu, "norm_ffn_g": d_g_ffn, "norm_ple_g": d_g_ple,
        "b_ple_gate": d_b_g, "final_norm_g": d_g_fin,
    })

    dh1 = _hosted(exch, _mm_nt, "d_h1", dz16, exch.weight("w_in"), sharded=True, tm=1024)
    exch.grad("w_in", _hosted(exch, _mm_tn, "d_proj_in", h1, dz16, shards=N_CHIPS, tn=768))

    def norm_in_bwd(x_t, dres_t, dh_t, g_v):
        _, vjp = jax.vjp(_rms, x_t, g_v)
        dx, dg = vjp(dh_t)
        return dres_t + dx, dg

    grad_x, d_g_mix = _hosted(exch, _rowwise, "d_norm_mix", norm_in_bwd, [x, dx1, dh1], [g_mix], [tok(d_model)], [acc(d_model)])
    exch.small_grads({"norm_mix_g": d_g_mix})
    return loss, grad_x


def _place():
    x, y, c = lax.axis_index("x"), lax.axis_index("y"), lax.axis_index("c")
    chips = [(1 - x, y), (x, 1 - y), (1 - x, 1 - y)]
    return x, y, c, chips


def _cast_into_slot(name, w2d, shard, tr=256):
    rows, cols = w2d.shape
    rh = rows // 2
    tr = _pick(rh, tr, 16)
    per = rh // tr

    def body(s_ref, a_ref, o_ref):
        o_ref[...] = a_ref[...].astype(BF16)

    grid_spec = pltpu.PrefetchScalarGridSpec(
        num_scalar_prefetch=1, grid=(2, per),
        in_specs=[pl.BlockSpec((tr, cols), lambda h, i, s_ref: (h * per + i, 0))],
        out_specs=pl.BlockSpec((None, None, tr, cols), lambda h, i, s_ref: (s_ref[0], h, i, 0)))
    return pl.pallas_call(body, name=name, grid_spec=grid_spec, out_shape=_sds((N_CHIPS, 2, rh, cols), BF16),
                          compiler_params=_params(("arbitrary", "arbitrary")))(shard.reshape(1).astype(jnp.int32), w2d)


def _exchange_alone(name, side):
    n_in, n_out = len(side.ins), len(side.out_shapes)

    def body(*refs):
        ins, outs, sems = refs[:n_in], refs[n_in:n_in + n_out], refs[n_in + n_out:]
        side.first(ins, outs, *sems)
        if side.mid is not None:
            side.mid(ins, outs, *sems)
        side.last(ins, outs, *sems)

    return pl.pallas_call(
        body, name=name, in_specs=[ANY] * n_in, out_specs=[ANY] * n_out, out_shape=side.out_shapes,
        input_output_aliases=side.aliases,
        scratch_shapes=[pltpu.SemaphoreType.DMA((side.n_sems,)), pltpu.SemaphoreType.DMA((side.n_sems,))],
    )(*side.ins)


def _gather_side(slots):
    n = len(slots)

    def copies(kind, outs, send_sems, recv_sems):
        x, y, c, chips = _place()

        def remote(k, ref, to):
            return pltpu.make_async_remote_copy(src_ref=ref, dst_ref=ref, send_sem=send_sems.at[k], recv_sem=recv_sems.at[k],
                                                device_id=to, device_id_type=MESH)

        pairs = [(w, j, 2 * cx + cy, (cx, cy)) for w in range(n) for j, (cx, cy) in enumerate(chips)]
        if kind == "sends":
            return [remote(3 * w + j, outs[w].at[2 * x + y, c], (*chip, c)) for w, j, _, chip in pairs]
        if kind == "arrivals":
            return [remote(3 * w + j, outs[w].at[s, c], (x, y, c)) for w, j, s, _ in pairs]
        if kind == "passed":
            return [remote(3 * n + 3 * w + j, outs[w].at[s, c], (x, y, 1 - c)) for w, j, s, _ in pairs]
        return [remote(3 * n + 3 * w + j, outs[w].at[s, 1 - c], (x, y, c)) for w, j, s, _ in pairs]

    def first(ins, outs, *sems):
        for cp in copies("sends", outs, *sems):
            cp.start()

    def mid(ins, outs, *sems):
        for arrived, onward in zip(copies("arrivals", outs, *sems), copies("passed", outs, *sems)):
            arrived.wait_recv()
            onward.start()

    def last(ins, outs, *sems):
        for cp in copies("from_sibling", outs, *sems):
            cp.wait_recv()
        for cp in copies("sends", outs, *sems) + copies("passed", outs, *sems):
            cp.wait_send()

    return _Side(slots, [_sds(s.shape, s.dtype) for s in slots], 6 * n, first, last, mid=mid, aliases={w: w for w in range(n)})


def _swap_side(grads):
    n = len(grads)

    def copies(ins, outs, send_sems, recv_sems):
        x, y, c, _ = _place()
        return [pltpu.make_async_remote_copy(src_ref=ins[w].at[:, 1 - c], dst_ref=outs[w], send_sem=send_sems.at[w],
                                             recv_sem=recv_sems.at[w], device_id=(x, y, 1 - c), device_id_type=MESH)
                for w in range(n)]

    def first(*refs):
        for cp in copies(*refs):
            cp.start()

    def last(*refs):
        for cp in copies(*refs):
            cp.wait()

    return _Side(grads, [_sds((g.shape[0], *g.shape[2:]), g.dtype) for g in grads], n, first, last)


def _scatter_side(halves):
    n = len(halves)

    def copies(ins, outs, send_sems, recv_sems):
        x, y, c, chips = _place()
        return [pltpu.make_async_remote_copy(
            src_ref=ins[w].at[2 * cx + cy], dst_ref=outs[w].at[j], send_sem=send_sems.at[3 * w + j],
            recv_sem=recv_sems.at[3 * w + j], device_id=(cx, cy, c), device_id_type=MESH)
            for w in range(n) for j, (cx, cy) in enumerate(chips)]

    def first(*refs):
        for cp in copies(*refs):
            cp.start()

    def last(*refs):
        for cp in copies(*refs):
            cp.wait()

    return _Side(halves, [_sds((3, *h.shape[1:]), h.dtype) for h in halves], 3 * n, first, last)


def _join_halves(name, slots):
    n = len(slots)

    def body(*refs):
        outs = refs[n:2 * n]
        send_sems, recv_sems = refs[2 * n:]
        x, y, c, _ = _place()

        def copy(w, half, to):
            return pltpu.make_async_remote_copy(src_ref=outs[w].at[half], dst_ref=outs[w].at[half], send_sem=send_sems.at[w],
                                                recv_sem=recv_sems.at[w], device_id=to, device_id_type=MESH)

        copies = [copy(w, c, (x, y, 1 - c)) for w in range(n)]
        for cp in copies:
            cp.start()
        for w in range(n):
            copy(w, 1 - c, (x, y, c)).wait_recv()
        for cp in copies:
            cp.wait_send()

    return pl.pallas_call(
        body, name=name, in_specs=[ANY] * n, out_specs=[ANY] * n,
        out_shape=[_sds(s.shape, s.dtype) for s in slots], input_output_aliases={w: w for w in range(n)},
        scratch_shapes=[pltpu.SemaphoreType.DMA((n,)), pltpu.SemaphoreType.DMA((n,))],
    )(*slots)


def _allreduce_small(block, tr=256):
    rows, lanes = block.shape
    tr = _pick(rows, tr, SUBLANES)

    def body(x_ref, o_ref, buf, send_sems, recv_sems):
        x, y, c, chips = _place()
        me, sibling = (x, y, c), (x, y, 1 - c)

        def slot(px, py, pc):
            return buf.at[4 * px + 2 * py + pc]

        def copy(k, block_of, to):
            return pltpu.make_async_remote_copy(src_ref=slot(*block_of), dst_ref=slot(*block_of), send_sem=send_sems.at[k],
                                                recv_sem=recv_sems.at[k], device_id=to, device_id_type=MESH)

        slot(*me)[...] = x_ref[...]
        first = [copy(0, me, sibling)] + [copy(1 + j, me, (*chip, c)) for j, chip in enumerate(chips)]
        for cp in first:
            cp.start()
        passed = [copy(4 + j, (*chip, c), sibling) for j, chip in enumerate(chips)]
        for j, chip in enumerate(chips):
            copy(1 + j, (*chip, c), me).wait_recv()
            passed[j].start()
        copy(0, sibling, me).wait_recv()
        for j, chip in enumerate(chips):
            copy(4 + j, (*chip, 1 - c), me).wait_recv()
        for cp in first + passed:
            cp.wait_send()
        for r0 in range(0, rows, tr):
            acc = buf[0, r0:r0 + tr, :]
            for k in range(1, N_DEV):
                acc = acc + buf[k, r0:r0 + tr, :]
            o_ref[r0:r0 + tr, :] = acc

    vm = pl.BlockSpec(memory_space=pltpu.VMEM)
    return pl.pallas_call(
        body, name="allreduce_small", in_specs=[vm], out_specs=vm, out_shape=_sds((rows, lanes), block.dtype),
        scratch_shapes=[pltpu.VMEM((N_DEV, rows, lanes), block.dtype), pltpu.SemaphoreType.DMA((7,)), pltpu.SemaphoreType.DMA((7,))],
        compiler_params=pltpu.CompilerParams(vmem_limit_bytes=VMEM_LIMIT),
    )(block)


def _small_gather_side(block):
    def copy(kind, j, ins, outs, send_sems, recv_sems):
        x, y, c, chips = _place()
        chip = chips[j] if j is not None else None
        slot = lambda px, py, pc: outs[0].at[4 * px + 2 * py + pc]

        def remote(k, src, dst, to):
            return pltpu.make_async_remote_copy(src_ref=src, dst_ref=dst, send_sem=send_sems.at[k], recv_sem=recv_sems.at[k],
                                                device_id=to, device_id_type=MESH)

        if kind == "to_sibling":
            return remote(0, ins[0], slot(x, y, c), (x, y, 1 - c))
        if kind == "from_sibling":
            return remote(0, ins[0], slot(x, y, 1 - c), (x, y, c))
        if kind == "to_chip":
            return remote(1 + j, ins[0], slot(x, y, c), (*chip, c))
        if kind == "from_chip":
            return remote(1 + j, ins[0], slot(*chip, c), (x, y, c))
        if kind == "pass_on":
            return remote(4 + j, slot(*chip, c), slot(*chip, c), (x, y, 1 - c))
        return remote(4 + j, slot(*chip, 1 - c), slot(*chip, 1 - c), (x, y, c))

    def first(*refs):
        copy("to_sibling", None, *refs).start()
        for j in range(3):
            copy("to_chip", j, *refs).start()

    def mid(*refs):
        for j in range(3):
            copy("from_chip", j, *refs).wait_recv()
            copy("pass_on", j, *refs).start()

    def last(*refs):
        copy("from_sibling", None, *refs).wait_recv()
        for j in range(3):
            copy("passed_on", j, *refs).wait_recv()
        copy("to_sibling", None, *refs).wait_send()
        for j in range(3):
            copy("to_chip", j, *refs).wait_send()
            copy("pass_on", j, *refs).wait_send()

    return _Side([block], [_sds((N_DEV, *block.shape), block.dtype)], 7, first, last, mid=mid)


def _sum_slots(name, own, gathered, me, tr=512):
    n, rows, cols = gathered.shape
    tr = _pick(rows, tr, SUBLANES)

    def body(me_ref, own_ref, g_ref, o_ref):
        mine = own_ref[...]
        acc = jnp.where(me_ref[0] == 0, mine, g_ref[0])
        for k in range(1, n):
            acc = acc + jnp.where(me_ref[0] == k, mine, g_ref[k])
        o_ref[...] = acc

    grid_spec = pltpu.PrefetchScalarGridSpec(
        num_scalar_prefetch=1, grid=(rows // tr,),
        in_specs=[pl.BlockSpec((tr, cols), lambda i, me_ref: (i, 0)), pl.BlockSpec((n, tr, cols), lambda i, me_ref: (0, i, 0))],
        out_specs=pl.BlockSpec((tr, cols), lambda i, me_ref: (i, 0)))
    return pl.pallas_call(body, name=name, grid_spec=grid_spec, out_shape=_sds((rows, cols), own.dtype),
                          compiler_params=_params(("arbitrary",)))(me.reshape(1).astype(jnp.int32), own, gathered)


def _sum_received(name, own, received, c, tr=256):
    n, rows, cols = received.shape
    tr = _pick(rows, tr, 16)

    def body(c_ref, a_ref, s_ref, o_ref):
        acc = a_ref[...]
        for k in range(n):
            acc = acc + s_ref[k].astype(F32)
        o_ref[...] = acc

    grid_spec = pltpu.PrefetchScalarGridSpec(
        num_scalar_prefetch=1, grid=(rows // tr,),
        in_specs=[pl.BlockSpec((tr, cols), lambda i, c_ref: (i, 0)), pl.BlockSpec((n, tr, cols), lambda i, c_ref: (0, i, 0))],
        out_specs=pl.BlockSpec((None, tr, cols), lambda i, c_ref: (c_ref[0], i, 0)))
    return pl.pallas_call(body, name=name, grid_spec=grid_spec, out_shape=_sds((2, rows, cols), F32),
                          compiler_params=_params(("arbitrary",)))(c.reshape(1).astype(jnp.int32), own, received)


def _add_halves(name, full, c, received, tr=256):
    s, _, rh, cols = full.shape
    tr = _pick(rh, tr, 16)

    def body(c_ref, a_ref, b_ref, o_ref):
        o_ref[...] = (a_ref[...] + b_ref[...]).astype(BF16)

    grid_spec = pltpu.PrefetchScalarGridSpec(
        num_scalar_prefetch=1, grid=(s, rh // tr),
        in_specs=[pl.BlockSpec((None, None, tr, cols), lambda q, i, c_ref: (q, c_ref[0], i, 0)),
                  pl.BlockSpec((None, tr, cols), lambda q, i, c_ref: (q, i, 0))],
        out_specs=pl.BlockSpec((None, tr, cols), lambda q, i, c_ref: (q, i, 0)))
    return pl.pallas_call(body, name=name, grid_spec=grid_spec, out_shape=_sds((s, rh, cols), BF16),
                          compiler_params=_params(("arbitrary", "arbitrary")))(c.reshape(1).astype(jnp.int32), full, received)


def _own_half(name, full, c, shard, received, tr=256):
    _, _, rh, cols = full.shape
    tr = _pick(rh, tr, SUBLANES)

    def body(i_ref, a_ref, b_ref, o_ref):
        o_ref[...] = a_ref[...] + b_ref[...]

    grid_spec = pltpu.PrefetchScalarGridSpec(
        num_scalar_prefetch=1, grid=(rh // tr,),
        in_specs=[pl.BlockSpec((None, None, tr, cols), lambda i, i_ref: (i_ref[1], i_ref[0], i, 0)),
                  pl.BlockSpec((None, tr, cols), lambda i, i_ref: (i_ref[1], i, 0))],
        out_specs=pl.BlockSpec((tr, cols), lambda i, i_ref: (i, 0)))
    return pl.pallas_call(body, name=name, grid_spec=grid_spec, out_shape=_sds((rh, cols), F32),
                          compiler_params=_params(("arbitrary",)))(jnp.stack([c, shard]).astype(jnp.int32), full, received)


LARGE = ("w_in", "ssm_glu_w", "w_out", "w_ffn_in", "w_ffn_out", "w_ple_gate", "w_ple_proj")
COLUMN_SHARDED = ("w_in", "w_ffn_in", "w_ple_proj")
SMALL = ("norm_mix_g", "ssm_lambda_re", "ssm_lambda_im", "ssm_log_step", "ssm_b_re", "ssm_b_im", "ssm_c_re", "ssm_c_im",
         "ssm_d", "ssm_glu_b", "sgu_ln_g", "sgu_ln_b", "sgu_w", "sgu_b", "out_norm_ssm_g", "out_norm_sgu_g", "norm_ffn_g",
         "norm_ple_g", "b_ple_gate", "final_norm_g")
WEIGHTS = ("norm_mix_g", "w_in", "ssm_lambda_re", "ssm_lambda_im", "ssm_log_step", "ssm_b_re", "ssm_b_im", "ssm_c_re",
           "ssm_c_im", "ssm_d", "ssm_glu_w", "ssm_glu_b", "sgu_ln_g", "sgu_ln_b", "sgu_w", "sgu_b", "out_norm_ssm_g",
           "out_norm_sgu_g", "w_out", "norm_ffn_g", "w_ffn_in", "w_ffn_out", "norm_ple_g", "w_ple_gate", "b_ple_gate",
           "w_ple_proj", "final_norm_g")
PACK_ROWS = SUBLANES * LANES


def _pack(arrays):
    parts = []
    for a in arrays:
        flat = a.reshape(-1).astype(F32)
        pad = -flat.shape[0] % PACK_ROWS
        parts.append(jnp.pad(flat, (0, pad)) if pad else flat)
    return jnp.concatenate(parts).reshape(-1, LANES)


def _unpack(packed, like):
    flat = packed.reshape(-1)
    out, at = [], 0
    for a in like:
        size = a.size
        out.append(flat[at:at + size].reshape(a.shape))
        at += size + (-size % PACK_ROWS)
    return out


class _NoExchange:
    def __init__(self, weights):
        self.weights, self.grads, self.small = weights, {}, {}

    def weight(self, name):
        return self.weights[name]

    def grad(self, name, g):
        self.grads[name] = g

    def small_grads(self, grads):
        self.small.update(grads)

    def side(self, host):
        return None


class _MeshExchange:
    GATHER = {"proj_in": ("ssm_glu_w", "w_out"), "ssm_fwd": ("w_ffn_in",), "ffn_in": ("w_ffn_out", "w_ple_gate", "w_ple_proj")}
    SWAP = {"d_act": ("w_ple_proj", "w_ple_gate", "w_ffn_out"), "d_h2": ("w_ffn_in",), "d_sgu": ("w_out", "ssm_glu_w")}
    SWAP_ALONE = ("w_in",)
    SCATTER = {"d_ffn_in": ("w_ple_proj", "w_ple_gate", "w_ffn_out"), "ssm_bwd": ("w_ffn_in",),
               "d_h1": ("w_out", "ssm_glu_w"), "d_norm_mix": ("w_in",)}
    SMALL_GATHER = "d_proj_in"
    GROUPS = (("w_ple_proj", "w_ple_gate", "w_ffn_out"), ("w_ffn_in",), ("w_out", "ssm_glu_w"), ("w_in",))

    def __init__(self, shards, small_like, c, shard, me):
        self.c, self.shard, self.me, self.small_like = c, shard, me, small_like
        self.slots = {k: _cast_into_slot("cast_" + k, shards[k], shard) for k in LARGE}
        (self.slots["w_in"],) = _exchange_alone("allgather_w_in", _gather_side([self.slots["w_in"]]))
        self.full, self.received, self.halves, self.quarters, self.small = {}, {}, {}, {}, {}

    def weight(self, name):
        g = self.slots[name]
        _, _, rh, cols = g.shape
        return g.reshape(N_CHIPS, 2 * rh, cols) if name in COLUMN_SHARDED else g.reshape(N_CHIPS * 2 * rh, cols)

    def grad(self, name, g):
        if name not in COLUMN_SHARDED:
            g = g.reshape(N_CHIPS, g.shape[0] // N_CHIPS, g.shape[1])
        self.full[name] = g.reshape(N_CHIPS, 2, g.shape[1] // 2, g.shape[2])
        if name in self.SWAP_ALONE:
            self._swapped((name,), _exchange_alone("grad_swap_" + name, _swap_side([self.full[name]])))

    def _swapped(self, names, received):
        for k, r in zip(names, received):
            self.received[k] = r
            self.halves[k] = _add_halves("grad_add_halves_" + k, self.full[k], self.c, r)

    def small_grads(self, grads):
        self.small.update(grads)

    def _packed(self, names):
        return _pack([self.small[k].reshape(self.small_like[k].shape) for k in names])

    def side(self, host):
        if host in self.GATHER:
            return _gather_side([self.slots[k] for k in self.GATHER[host]])
        if host in self.SWAP:
            return _swap_side([self.full[k] for k in self.SWAP[host]])
        if host in self.SCATTER:
            return _scatter_side([self.halves[k] for k in self.SCATTER[host]])
        if host == self.SMALL_GATHER:
            self.packed_early = self._packed(SMALL[1:])
            return _small_gather_side(self.packed_early)
        return None

    def done(self, host, moved):
        if host in self.GATHER:
            self.slots.update(zip(self.GATHER[host], moved))
        elif host in self.SWAP:
            self._swapped(self.SWAP[host], moved)
        elif host in self.SCATTER:
            self.quarters.update(zip(self.SCATTER[host], moved))
        else:
            (self.gathered_early,) = moved

    def small_reduced(self):
        early = _sum_slots("small_sum", self.packed_early, self.gathered_early, self.me)
        late = _allreduce_small(self._packed(SMALL[:1]))
        return jnp.concatenate([late, early], axis=0)

    def reduced(self, group):
        own = [_own_half("grad_own_" + k, self.full[k], self.c, self.shard, self.received[k]) for k in group]
        parts = [_sum_received("grad_sum_" + k, o, self.quarters[k], self.c) for k, o in zip(group, own)]
        joined = _join_halves("grad_join_" + group[0], parts)
        return {k: j.reshape(2 * j.shape[1], j.shape[2]) for k, j in zip(group, joined)}


def kernel(x, p, norm_mix_g, w_in, ssm_lambda_re, ssm_lambda_im, ssm_log_step, ssm_b_re, ssm_b_im, ssm_c_re, ssm_c_im, ssm_d, ssm_glu_w, ssm_glu_b, sgu_ln_g, sgu_ln_b, sgu_w, sgu_b, out_norm_ssm_g, out_norm_sgu_g, w_out, norm_ffn_g, w_ffn_in, w_ffn_out, norm_ple_g, w_ple_gate, b_ple_gate, w_ple_proj, final_norm_g, loss_target, m_norm_mix_g, m_w_in, m_ssm_lambda_re, m_ssm_lambda_im, m_ssm_log_step, m_ssm_b_re, m_ssm_b_im, m_ssm_c_re, m_ssm_c_im, m_ssm_d, m_ssm_glu_w, m_ssm_glu_b, m_sgu_ln_g, m_sgu_ln_b, m_sgu_w, m_sgu_b, m_out_norm_ssm_g, m_out_norm_sgu_g, m_w_out, m_norm_ffn_g, m_w_ffn_in, m_w_ffn_out, m_norm_ple_g, m_w_ple_gate, m_b_ple_gate, m_w_ple_proj, m_final_norm_g, v_norm_mix_g, v_w_in, v_ssm_lambda_re, v_ssm_lambda_im, v_ssm_log_step, v_ssm_b_re, v_ssm_b_im, v_ssm_c_re, v_ssm_c_im, v_ssm_d, v_ssm_glu_w, v_ssm_glu_b, v_sgu_ln_g, v_sgu_ln_b, v_sgu_w, v_sgu_b, v_out_norm_ssm_g, v_out_norm_sgu_g, v_w_out, v_norm_ffn_g, v_w_ffn_in, v_w_ffn_out, v_norm_ple_g, v_w_ple_gate, v_b_ple_gate, v_w_ple_proj, v_final_norm_g):
    given = dict(locals())
    w = {k: given[k] for k in WEIGHTS}
    m = {k: given["m_" + k] for k in WEIGHTS}
    v = {k: given["v_" + k] for k in WEIGHTS}
    c = lax.axis_index("c")
    shard = 2 * lax.axis_index("x") + lax.axis_index("y")

    exch = _MeshExchange({k: w[k].reshape(w[k].shape[1:]) for k in LARGE}, {k: w[k] for k in SMALL}, c, shard, 2 * shard + c)
    unlayer = lambda a: a if a.ndim == 1 else a[0]
    sp = {k: unlayer(w[k]) for k in SMALL}
    n_tok, d_model = x.shape[1:]
    loss, grad_x = _local_grads(x.reshape(n_tok, d_model), p.reshape(n_tok, p.shape[-1]),
                                loss_target.reshape(n_tok, d_model), sp, exch)
    loss = lax.psum(loss, ("x", "y", "c"))

    grad_w, delta_w, new_m, new_v = {}, {}, {}, {}
    for group in exch.GROUPS:
        reduced = exch.reduced(group)
        for k in group:
            shape = w[k].shape
            two_d = lambda a: a.reshape(shape[1:])
            like = _sds(shape[1:], F32)
            d_k, m_k, v_k = _hosted(exch, _rowwise, "adamw_" + k, _adamw, [two_d(w[k]), reduced[k], two_d(m[k]), two_d(v[k])],
                                    [], [like, like, like])
            grad_w[k], delta_w[k], new_m[k], new_v[k] = (a.reshape(shape) for a in (reduced[k], d_k, m_k, v_k))

    packed_g = exch.small_reduced()
    like = _sds(packed_g.shape, F32)
    d_s, m_s, v_s = _rowwise("adamw_small", _adamw, [_pack([w[k] for k in SMALL]), packed_g, _pack([m[k] for k in SMALL]),
                                                     _pack([v[k] for k in SMALL])], [], [like, like, like])
    shapes = [w[k] for k in SMALL]
    for k, g_k, d_k, m_k, v_k in zip(SMALL, _unpack(packed_g, shapes), _unpack(d_s, shapes), _unpack(m_s, shapes), _unpack(v_s, shapes)):
        grad_w[k], delta_w[k], new_m[k], new_v[k] = g_k, d_k, m_k, v_k

    return (loss, grad_x.reshape(x.shape), *[grad_w[k] for k in WEIGHTS], *[delta_w[k] for k in WEIGHTS],
            *[new_m[k] for k in WEIGHTS], *[new_v[k] for k in WEIGHTS])
```

```python
import functools

import jax
import jax.numpy as jnp
from jax import lax
from jax.experimental import pallas as pl
from jax.experimental.pallas import tpu as pltpu

F32 = jnp.float32
BF16 = jnp.bfloat16

EPS = 1e-6
LAMBDA_RE_MAX = -1e-4
ADAM_LR = 0.001
ADAM_B1 = 0.9
ADAM_B2 = 0.999
ADAM_EPS = 1e-08
ADAM_WD = 0.01
ADAM_STEP = 10

N_CHIPS = 4
N_DEV = 8
SUBLANES = 8
LANES = 128
SSM_CH_BLOCK = 256
SCAN_LANES = 256
SCAN_BLOCKS = 2
VMEM_LIMIT = 56 * 1024 * 1024

MESH = pl.DeviceIdType.MESH


def _pick(n, pref, mult):
    if n <= pref:
        return n
    t = (pref // mult) * mult
    while t >= mult:
        if n % t == 0:
            return t
        t -= mult
    return n


def _params(semantics):
    return pltpu.CompilerParams(dimension_semantics=semantics, vmem_limit_bytes=VMEM_LIMIT)


class _Cols:
    def __init__(self, arr, width, blk):
        self.arr, self.width, self.blk = arr, width, blk


def _sds(shape, dtype):
    return jax.ShapeDtypeStruct(tuple(shape), dtype)


ANY = pl.BlockSpec(memory_space=pl.ANY)


class _Side:
    def __init__(self, ins, out_shapes, n_sems, first, last, mid=None, aliases=None):
        self.ins, self.out_shapes, self.n_sems = list(ins), list(out_shapes), n_sems
        self.first, self.mid, self.last = first, mid, last
        self.aliases = dict(aliases or {})


def _call(body, side, operands, *, name, grid, in_specs, out_specs, out_shape, compiler_params, scratch_shapes=()):
    if side is None:
        return pl.pallas_call(body, name=name, grid=grid, in_specs=in_specs, out_specs=out_specs, out_shape=out_shape,
                              scratch_shapes=list(scratch_shapes), compiler_params=compiler_params)(*operands)
    single = not isinstance(out_specs, (list, tuple))
    out_specs = [out_specs] if single else list(out_specs)
    out_shape = [out_shape] if single else list(out_shape)
    n_in, n_out, n_scr = len(in_specs), len(out_specs), len(scratch_shapes)
    n_sin, n_sout = len(side.ins), len(side.out_shapes)
    steps = 1
    for g in grid:
        steps *= g

    def hosted(*refs):
        ins, s_ins = refs[:n_in], refs[n_in:n_in + n_sin]
        at = n_in + n_sin
        outs, s_outs = refs[at:at + n_out], refs[at + n_out:at + n_out + n_sout]
        scratch = refs[at + n_out + n_sout:at + n_out + n_sout + n_scr]
        sems = refs[-2:]
        step = pl.program_id(0)
        for d in range(1, len(grid)):
            step = step * grid[d] + pl.program_id(d)

        @pl.when(step == 0)
        def _():
            side.first(s_ins, s_outs, *sems)

        if side.mid is not None:
            @pl.when(step == (3 * steps) // 4)
            def _():
                side.mid(s_ins, s_outs, *sems)

        body(*ins, *outs, *scratch)

        @pl.when(step == steps - 1)
        def _():
            side.last(s_ins, s_outs, *sems)

    res = pl.pallas_call(
        hosted, name=name, grid=grid, in_specs=[*in_specs, *[ANY] * n_sin], out_specs=[*out_specs, *[ANY] * n_sout],
        out_shape=[*out_shape, *side.out_shapes], input_output_aliases={n_in + i: n_out + o for i, o in side.aliases.items()},
        scratch_shapes=[*scratch_shapes, pltpu.SemaphoreType.DMA((side.n_sems,)), pltpu.SemaphoreType.DMA((side.n_sems,))],
        compiler_params=compiler_params)(*operands, *side.ins)
    return (res[0] if single else list(res[:n_out])), list(res[n_out:])


def _rowwise(name, fn, rows, params, row_outs, acc_outs=(), tr=256, side=None):
    rows = [r if isinstance(r, _Cols) else _Cols(r, r.shape[1], 0) for r in rows]
    m = rows[0].arr.shape[0]
    tr = _pick(m, tr, 16)
    n_in = len(rows) + len(params)
    n_ro = len(row_outs)

    def body(*refs):
        vals = fn(*[r[...] for r in refs[:n_in]])
        if not isinstance(vals, (tuple, list)):
            vals = (vals,)
        outs = refs[n_in:]
        for r, v in zip(outs[:n_ro], vals[:n_ro]):
            r[...] = v.astype(r.dtype)
        first = pl.program_id(0) == 0
        for r, v in zip(outs[n_ro:], vals[n_ro:]):
            @pl.when(first)
            def _():
                r[...] = jnp.zeros(r.shape, r.dtype)
            r[...] += v.astype(r.dtype).reshape(r.shape)

    in_specs = [pl.BlockSpec((tr, r.width), lambda i, b=r.blk: (i, b)) for r in rows]
    in_specs += [pl.BlockSpec(p.shape, lambda i, nd=p.ndim: (0,) * nd) for p in params]
    out_specs = [pl.BlockSpec((tr, o.shape[1]), lambda i: (i, 0)) for o in row_outs]
    out_specs += [pl.BlockSpec(o.shape, lambda i, nd=len(o.shape): (0,) * nd) for o in acc_outs]
    return _call(body, side, [*[r.arr for r in rows], *params], name=name, grid=(m // tr,), in_specs=in_specs,
                 out_specs=out_specs, out_shape=[*row_outs, *acc_outs], compiler_params=_params(("arbitrary",)))


def _grid_order(swap):
    if not swap:
        return (lambda grid: grid), (lambda f: f)
    return (lambda grid: grid[::-1]), (lambda f: (lambda j, i: f(i, j)))


def _mm_nn(name, a, w, *, sharded=False, res=None, out_dtype=F32, tm=512, tn=512, w_resident=False, side=None):
    m, k = a.shape
    tm = _pick(m, tm, 16)
    order, ix = _grid_order(w_resident)
    if sharded:
        s, _, ns = w.shape
        n = s * ns
        tn = _pick(ns, tn, LANES)
        per = ns // tn
        w_spec = pl.BlockSpec((None, k, tn), ix(lambda i, j: (j // per, 0, j % per)))
    else:
        n = w.shape[1]
        tn = _pick(n, tn, LANES)
        w_spec = pl.BlockSpec((k, tn), ix(lambda i, j: (0, j)))

    def body(a_ref, w_ref, *rest):
        acc = jnp.dot(a_ref[...], w_ref[...], preferred_element_type=F32)
        if res is not None:
            acc = acc + rest[0][...]
        rest[-1][...] = acc.astype(out_dtype)

    in_specs = [pl.BlockSpec((tm, k), ix(lambda i, j: (i, 0))), w_spec]
    ops = [a, w]
    if res is not None:
        in_specs.append(pl.BlockSpec((tm, tn), ix(lambda i, j: (i, j))))
        ops.append(res)
    return _call(body, side, ops, name=name, grid=order((m // tm, n // tn)), in_specs=in_specs,
                 out_specs=pl.BlockSpec((tm, tn), ix(lambda i, j: (i, j))), out_shape=_sds((m, n), out_dtype),
                 compiler_params=_params(("arbitrary", "arbitrary")))


def _mm_nt(name, g, w, *, sharded=False, g_halves=False, tm=512, tk=512, w_resident=False, side=None):
    m, n = g.shape[-2:]
    tm = _pick(m, tm, 16)
    order, ix = _grid_order(w_resident)
    dims = (((1,), (1,)), ((), ()))
    g_spec = pl.BlockSpec((2, tm, n), ix(lambda i, j: (0, i, 0))) if g_halves else pl.BlockSpec((tm, n), ix(lambda i, j: (i, 0)))
    if sharded:
        s, k, ns = w.shape
        tk = _pick(k, tk, LANES)
        w_spec = pl.BlockSpec((s, tk, ns), ix(lambda i, j: (0, j, 0)))

        def columns(g_ref, q):
            if not g_halves:
                return g_ref[:, q * ns:(q + 1) * ns]
            half, at = divmod(q, s // 2)
            return g_ref[half, :, at * ns:(at + 1) * ns]

        def body(g_ref, w_ref, o_ref):
            acc = lax.dot_general(columns(g_ref, 0), w_ref[0], dims, preferred_element_type=F32)
            for q in range(1, s):
                acc = acc + lax.dot_general(columns(g_ref, q), w_ref[q], dims, preferred_element_type=F32)
            o_ref[...] = acc
    else:
        k = w.shape[0]
        tk = _pick(k, tk, LANES)
        w_spec = pl.BlockSpec((tk, n), ix(lambda i, j: (j, 0)))

        def body(g_ref, w_ref, o_ref):
            o_ref[...] = lax.dot_general(g_ref[...], w_ref[...], dims, preferred_element_type=F32)

    return _call(body, side, [g, w], name=name, grid=order((m // tm, k // tk)), in_specs=[g_spec, w_spec],
                 out_specs=pl.BlockSpec((tm, tk), ix(lambda i, j: (i, j))), out_shape=_sds((m, k), F32),
                 compiler_params=_params(("arbitrary", "arbitrary")))


def _mm_tn(name, a, g, *, shards=0, g_halves=False, tk=512, tn=512, g_resident=False, side=None):
    m, k = a.shape
    n = 2 * g.shape[2] if g_halves else g.shape[1]
    tk = _pick(k, tk, LANES)
    order, ix = _grid_order(g_resident)
    dims = (((0,), (0,)), ((), ()))
    if shards:
        ns = n // shards
        tn = _pick(ns, tn, LANES)
        per = ns // tn
        out_spec = pl.BlockSpec((None, tk, tn), ix(lambda i, j: (j // per, i, j % per)))
        out_shape = _sds((shards, k, ns), F32)
    else:
        tn = _pick(n, tn, LANES)
        out_spec = pl.BlockSpec((tk, tn), ix(lambda i, j: (i, j)))
        out_shape = _sds((k, n), F32)

    def body(a_ref, g_ref, o_ref):
        o_ref[...] = lax.dot_general(a_ref[...], g_ref[...], dims, preferred_element_type=F32)

    if g_halves:
        per_half = n // 2 // tn
        g_spec = pl.BlockSpec((None, m, tn), ix(lambda i, j: (j // per_half, 0, j % per_half)))
    else:
        g_spec = pl.BlockSpec((m, tn), ix(lambda i, j: (0, j)))
    return _call(body, side, [a, g], name=name, grid=order((k // tk, n // tn)),
                 in_specs=[pl.BlockSpec((m, tk), ix(lambda i, j: (0, i))), g_spec],
                 out_specs=out_spec, out_shape=out_shape, compiler_params=_params(("arbitrary", "arbitrary")))


def _ffn_in_swiglu(name, a, w, *, tm=512, tn=1408, side=None):
    m, k = a.shape
    s, _, ns = w.shape
    f = s * ns // 2
    tm = _pick(m, tm, 16)
    tn = _pick(ns, tn, LANES)
    per = ns // tn
    order, ix = _grid_order(True)

    def body(a_ref, wg_ref, wu_ref, act_ref, gu_ref):
        x = a_ref[...]
        gate = jnp.dot(x, wg_ref[...], preferred_element_type=F32)
        up = jnp.dot(x, wu_ref[...], preferred_element_type=F32)
        act_ref[...] = _swiglu(gate, up).astype(BF16)
        gu_ref[0] = gate.astype(BF16)
        gu_ref[1] = up.astype(BF16)

    return _call(body, side, [a, w, w], name=name, grid=order((m // tm, f // tn)),
                 in_specs=[pl.BlockSpec((tm, k), ix(lambda i, j: (i, 0))),
                           pl.BlockSpec((None, k, tn), ix(lambda i, j: (j // per, 0, j % per))),
                           pl.BlockSpec((None, k, tn), ix(lambda i, j: (s // 2 + j // per, 0, j % per)))],
                 out_specs=[pl.BlockSpec((tm, tn), ix(lambda i, j: (i, j))), pl.BlockSpec((2, tm, tn), ix(lambda i, j: (0, i, j)))],
                 out_shape=[_sds((m, f), BF16), _sds((2, m, f), BF16)], compiler_params=_params(("arbitrary", "arbitrary")))


def _d_act_swiglu(name, g, w, gu, *, tm=1024, tk=512, side=None):
    m, n = g.shape
    f = w.shape[0]
    tm = _pick(m, tm, 16)
    tk = _pick(f, tk, LANES)
    dims = (((1,), (1,)), ((), ()))

    def body(g_ref, w_ref, gu_ref, o_ref):
        dact = lax.dot_general(g_ref[...], w_ref[...], dims, preferred_element_type=F32)
        _, vjp = jax.vjp(_swiglu, gu_ref[0].astype(F32), gu_ref[1].astype(F32))
        dgate, dup = vjp(dact)
        o_ref[0] = dgate.astype(BF16)
        o_ref[1] = dup.astype(BF16)

    return _call(body, side, [g, w, gu], name=name, grid=(m // tm, f // tk),
                 in_specs=[pl.BlockSpec((tm, n), lambda i, j: (i, 0)), pl.BlockSpec((tk, n), lambda i, j: (j, 0)),
                           pl.BlockSpec((2, tm, tk), lambda i, j: (0, i, j))],
                 out_specs=pl.BlockSpec((2, tm, tk), lambda i, j: (0, i, j)), out_shape=_sds((2, m, f), BF16),
                 compiler_params=_params(("arbitrary", "arbitrary")))


def _rms(x, g):
    r = lax.rsqrt(jnp.mean(x * x, axis=-1, keepdims=True) + EPS)
    return (x * r) * g


def _glu_out(y_pre, q, glu_b, g_norm):
    ya0 = jax.nn.gelu(y_pre)
    return _rms(ya0 * jax.nn.sigmoid(q + glu_b), g_norm)


def _sgu_rows(zu, zv, ln_g, ln_b, w_s, b_st, g_norm):
    heads, t, _ = w_s.shape
    hd = zu.shape[1] // heads
    uu = jax.nn.gelu(zu)
    vv = jax.nn.gelu(zv)
    mu = jnp.mean(vv, axis=-1, keepdims=True)
    xc = vv - mu
    r = lax.rsqrt(jnp.mean(xc * xc, axis=-1, keepdims=True) + EPS)
    vn = (xc * r) * ln_g + ln_b
    row = lax.broadcasted_iota(jnp.int32, (t, t), 0)
    col = lax.broadcasted_iota(jnp.int32, (t, t), 1)
    causal = row >= col
    chunks = []
    for n in range(zu.shape[0] // t):
        blocks = []
        for h in range(heads):
            wm = jnp.where(causal, w_s[h], jnp.zeros_like(w_s[h])).astype(BF16)
            vb = vn[n * t:(n + 1) * t, h * hd:(h + 1) * hd].astype(BF16)
            blocks.append(jnp.dot(wm, vb, preferred_element_type=F32) + b_st[:, h:h + 1])
        chunks.append(jnp.concatenate(blocks, axis=1))
    s = jnp.concatenate(chunks, axis=0) if len(chunks) > 1 else chunks[0]
    return _rms(uu * s, g_norm)


def _swiglu(gate, up):
    return jax.nn.silu(gate) * up


def _head_loss(x2, gpre, pp, b_g, g_final, target):
    gate = jax.nn.sigmoid(gpre + b_g)
    out = _rms(x2 + gate * pp, g_final)
    err = jnp.square(out - target)
    return 0.5 * jnp.sum(jnp.mean(err, axis=-1))


def _ssm_disc(lam_re, lam_im, log_step):
    lr = jnp.minimum(lam_re, LAMBDA_RE_MAX)
    li = lam_im
    dt = jnp.exp(log_step)
    mag = jnp.exp(lr * dt)
    ang = li * dt
    abar_re = mag * jnp.cos(ang)
    abar_im = mag * jnp.sin(ang)
    nr = abar_re - 1.0
    ni = abar_im
    den = lr * lr + li * li
    q_re = (nr * lr + ni * li) / den
    q_im = (ni * lr - nr * li) / den
    return abar_re, abar_im, q_re, q_im


def _ssm_bbar(q_re, q_im, b_re, b_im):
    return q_re * b_re - q_im * b_im, q_re * b_im + q_im * b_re


def _ssm_discretised(lam_re, lam_im, log_step, bt_re, bt_im):
    ar, ai, qr, qi = _ssm_disc(lam_re, lam_im, log_step)
    return (ar, ai, *_ssm_bbar(qr, qi, bt_re, bt_im))


def _adamw(w, g, m, v):
    m = ADAM_B1 * m + (1.0 - ADAM_B1) * g
    v = ADAM_B2 * v + (1.0 - ADAM_B2) * jnp.square(g)
    m_hat = m / (1.0 - ADAM_B1 ** ADAM_STEP)
    v_hat = v / (1.0 - ADAM_B2 ** ADAM_STEP)
    delta = -ADAM_LR * (m_hat / (jnp.sqrt(v_hat) + ADAM_EPS) + ADAM_WD * w)
    return delta, m, v


class _SsmDims:
    def __init__(self, groups, state, gch):
        self.g, self.p, self.h = groups, state, gch
        self.d = groups * gch
        self.cb = min(SSM_CH_BLOCK, self.d)
        self.gb = self.cb // gch
        self.ns = self.gb * state
        self.nb = self.d // self.cb


def _ssm_rows(sd, sp):
    gp = sd.g * sd.p
    log_step = jnp.broadcast_to(sp["ssm_log_step"][:, None], (sd.g, sd.p)).reshape(1, gp)
    bt = [sp[k].reshape(gp, sd.h).T for k in ("ssm_b_re", "ssm_b_im")]
    ct = [sp[k].transpose(1, 0, 2).reshape(sd.h, gp) for k in ("ssm_c_re", "ssm_c_im")]
    return (sp["ssm_lambda_re"].reshape(1, gp), sp["ssm_lambda_im"].reshape(1, gp), log_step, *bt, *ct)


def _block_mask(sd):
    row = lax.broadcasted_iota(jnp.int32, (sd.cb, sd.ns), 0) // sd.h
    col = lax.broadcasted_iota(jnp.int32, (sd.cb, sd.ns), 1) // sd.p
    return row == col


def _scan_consts(pr, pi_, reverse):
    if reverse:
        pi_ = [-v for v in pi_]
    shape = (SUBLANES, pr[0].shape[1])
    rows = lax.broadcasted_iota(jnp.int32, shape, 0)
    parts = []
    for d in (1, 2, 4):
        keep = (rows < SUBLANES - d) if reverse else (rows >= d)
        parts += [jnp.where(keep, jnp.broadcast_to(v[d - 1], shape), 0.0) for v in (pr, pi_)]
    order = range(SUBLANES - 1, -1, -1) if reverse else range(SUBLANES)
    parts += [jnp.concatenate([v[t] for t in order], axis=0) for v in (pr, pi_)]
    return jnp.concatenate(parts, axis=0)


def _ssm_operands(sd, rows):
    cb, ns, nb = sd.cb, sd.ns, sd.nb

    def body(lam_re, lam_im, log_step, bt_re, bt_im, ct_re, ct_im, wb_ref, wbt_ref, wc_ref, wct_ref, cst_f_ref, cst_r_ref):
        ar, ai, bbar_re, bbar_im = _ssm_discretised(lam_re[...], lam_im[...], log_step[...], bt_re[...], bt_im[...])
        pr, pi_ = [ar], [ai]
        for _ in range(SUBLANES - 1):
            pr, pi_ = pr + [pr[-1] * ar - pi_[-1] * ai], pi_ + [pr[-1] * ai + pi_[-1] * ar]
        mask = _block_mask(sd)
        spread = lambda src: jnp.where(mask, jnp.concatenate([src] * sd.gb, axis=0), 0.0)
        for j in range(nb):
            at = slice(j * ns, (j + 1) * ns)
            w = jnp.concatenate([spread(bbar_re[:, at]), spread(bbar_im[:, at])], axis=1)
            v = jnp.concatenate([spread(ct_re[:, at]), -spread(ct_im[:, at])], axis=1)
            wb_ref[j] = w.astype(BF16)
            wbt_ref[j] = w.T.astype(BF16)
            wct_ref[j] = v.astype(BF16)
            wc_ref[j] = v.T.astype(BF16)
            pj, qj = [u[:, at] for u in pr], [u[:, at] for u in pi_]
            cst_f_ref[j] = _scan_consts(pj, qj, False)
            cst_r_ref[j] = _scan_consts(pj, qj, True)

    wide, tall = _sds((nb, cb, 2 * ns), BF16), _sds((nb, 2 * ns, cb), BF16)
    cst = _sds((nb, 8 * SUBLANES, ns), F32)
    vm = pl.BlockSpec(memory_space=pltpu.VMEM)
    return pl.pallas_call(body, name="ssm_operands", in_specs=[vm] * 7, out_specs=[vm] * 6,
                          out_shape=[wide, tall, tall, wide, cst, cst],
                          compiler_params=pltpu.CompilerParams(vmem_limit_bytes=VMEM_LIMIT))(*rows)


def _ssm_param_grads(sd, rows, dwb, dwc, da):
    ns, nb, gp = sd.ns, sd.nb, sd.g * sd.p

    def body(lam_re, lam_im, log_step, bt_re, bt_im, dwb_v, dwc_v, da_v, *outs):
        mask = _block_mask(sd)

        def fold(dense):
            kept = jnp.where(mask, dense, 0.0)
            acc = kept[0:sd.h]
            for gl in range(1, sd.gb):
                acc = acc + kept[gl * sd.h:(gl + 1) * sd.h]
            return acc

        lanes = lambda parts: jnp.concatenate(parts, axis=1) if len(parts) > 1 else parts[0]
        dbbar_re = lanes([fold(dwb_v[j][:, :ns]) for j in range(nb)])
        dbbar_im = lanes([fold(dwb_v[j][:, ns:]) for j in range(nb)])
        dwct = [dwc_v[j].T for j in range(nb)]
        d_ct_re = lanes([fold(t[:, :ns]) for t in dwct])
        d_ct_im = -lanes([fold(t[:, ns:]) for t in dwct])
        dabar_re = lanes([da_v[j][0:1, :ns] for j in range(nb)])
        dabar_im = lanes([da_v[j][0:1, ns:] for j in range(nb)])
        _, vjp = jax.vjp(_ssm_discretised, lam_re[...], lam_im[...], log_step[...], bt_re[...], bt_im[...])
        d_lr, d_li, d_ls, d_bt_re, d_bt_im = vjp((dabar_re, dabar_im, dbbar_re, dbbar_im))
        group = (lax.broadcasted_iota(jnp.int32, (gp, sd.g), 0) // sd.p == lax.broadcasted_iota(jnp.int32, (gp, sd.g), 1))
        d_log_step = jnp.dot(d_ls, group.astype(F32), precision=lax.Precision.HIGHEST, preferred_element_type=F32)
        for ref, val in zip(outs, (d_lr, d_li, d_log_step, d_bt_re, d_bt_im, d_ct_re, d_ct_im)):
            ref[...] = val

    row, mat = _sds((1, gp), F32), _sds((sd.h, gp), F32)
    vm = pl.BlockSpec(memory_space=pltpu.VMEM)
    return pl.pallas_call(body, name="ssm_param_grads", in_specs=[vm] * 8, out_specs=[vm] * 7,
                          out_shape=[row, row, _sds((1, sd.g), F32), mat, mat, mat, mat],
                          compiler_params=pltpu.CompilerParams(vmem_limit_bytes=VMEM_LIMIT))(*rows[:5], dwb, dwc, da)


def _block_scan(s_ref, cst_ref, carry_ref, sd, rows, reverse):
    ns = sd.ns
    nblk = rows // SUBLANES
    w = min(SCAN_LANES, ns)
    for c0 in range(0, ns, w):
        re_l, im_l = slice(c0, c0 + w), slice(ns + c0, ns + c0 + w)
        cst = [cst_ref[k * SUBLANES:(k + 1) * SUBLANES, c0:c0 + w] for k in range(8)]

        def step(k, carry, re_l=re_l, im_l=im_l, cst=cst):
            local = []
            for b in range(SCAN_BLOCKS):
                blk = SCAN_BLOCKS * k + b
                blk = (nblk - 1 - blk) if reverse else blk
                r0 = pl.multiple_of(blk * SUBLANES, SUBLANES)
                xr = s_ref[pl.ds(r0, SUBLANES), re_l]
                xi = s_ref[pl.ds(r0, SUBLANES), im_l]
                for n, d in enumerate((1, 2, 4)):
                    ar, ai = cst[2 * n], cst[2 * n + 1]
                    shift = (SUBLANES - d) if reverse else d
                    sr = pltpu.roll(xr, shift, 0)
                    si = pltpu.roll(xi, shift, 0)
                    xr, xi = xr + ar * sr - ai * si, xi + ar * si + ai * sr
                local.append((r0, xr, xi))
            cr, ci = carry
            edge = slice(0, 1) if reverse else slice(SUBLANES - 1, SUBLANES)
            for r0, xr, xi in local:
                br = jnp.broadcast_to(cr, xr.shape)
                bi = jnp.broadcast_to(ci, xi.shape)
                xr, xi = xr + cst[6] * br - cst[7] * bi, xi + cst[6] * bi + cst[7] * br
                s_ref[pl.ds(r0, SUBLANES), re_l] = xr
                s_ref[pl.ds(r0, SUBLANES), im_l] = xi
                cr, ci = xr[edge, :], xi[edge, :]
            return cr, ci

        cr, ci = lax.fori_loop(0, nblk // SCAN_BLOCKS, step, (carry_ref[0:1, re_l], carry_ref[0:1, im_l]))
        carry_ref[0:1, re_l] = cr
        carry_ref[0:1, im_l] = ci


def _ssm_fwd(name, sd, z, wb, wc, cst, d_row, tt=512, side=None):
    n_tok = z.shape[0]
    tt = _pick(n_tok, tt, 16)
    cb, ns2 = sd.cb, 2 * sd.ns

    def body(z_ref, wb_ref, wc_ref, cst_ref, d_ref, y_ref, s_ref, a0_ref, carry_ref):
        @pl.when(pl.program_id(1) == 0)
        def _():
            carry_ref[...] = jnp.zeros(carry_ref.shape, F32)
        u = z_ref[...]
        s_ref[...] = jnp.dot(u.astype(BF16), wb_ref[...], preferred_element_type=F32)
        _block_scan(s_ref, cst_ref, carry_ref, sd, tt, reverse=False)
        y = jnp.dot(s_ref[...].astype(BF16), wc_ref[...], preferred_element_type=F32) + d_ref[...] * u
        y_ref[...] = y
        a0_ref[...] = jax.nn.gelu(y).astype(BF16)

    return _call(
        body, side, [z, wb, wc, cst, d_row], name=name, grid=(sd.nb, n_tok // tt),
        in_specs=[pl.BlockSpec((tt, cb), lambda j, i: (i, j)),
                  pl.BlockSpec((None, cb, ns2), lambda j, i: (j, 0, 0)),
                  pl.BlockSpec((None, ns2, cb), lambda j, i: (j, 0, 0)),
                  pl.BlockSpec((None, 8 * SUBLANES, sd.ns), lambda j, i: (j, 0, 0)),
                  pl.BlockSpec((1, cb), lambda j, i: (0, j))],
        out_specs=[pl.BlockSpec((tt, cb), lambda j, i: (i, j)), pl.BlockSpec((tt, ns2), lambda j, i: (i, j)),
                   pl.BlockSpec((tt, cb), lambda j, i: (i, j))],
        out_shape=[_sds((n_tok, sd.d), F32), _sds((n_tok, sd.nb * ns2), F32), _sds((n_tok, sd.d), BF16)],
        scratch_shapes=[pltpu.VMEM((SUBLANES, ns2), F32)],
        compiler_params=_params(("arbitrary", "arbitrary")))


def _ssm_bwd(name, sd, y_pre, dy_direct, dya0, z, states, wct, wbt, cst_rev, d_row, tt=512, side=None):
    n_tok = z.shape[0]
    tt = _pick(n_tok, tt, 16)
    nt = n_tok // tt
    cb, ns, ns2 = sd.cb, sd.ns, 2 * sd.ns
    blocks_per_tile = tt // SUBLANES
    tn_dims = (((0,), (0,)), ((), ()))

    def body(y_ref, dyd_ref, dya0_ref, z_ref, s_ref, sp_ref, wct_ref, wbt_ref, cst_ref, d_ref,
             du_ref, dwb_ref, dwc_ref, da_ref, dd_ref, lam_ref, carry_ref):
        i = pl.program_id(1)

        @pl.when(i == 0)
        def _():
            carry_ref[...] = jnp.zeros(carry_ref.shape, F32)
            dwb_ref[...] = jnp.zeros(dwb_ref.shape, F32)
            dwc_ref[...] = jnp.zeros(dwc_ref.shape, F32)
            da_ref[...] = jnp.zeros(da_ref.shape, F32)
            dd_ref[...] = jnp.zeros(dd_ref.shape, F32)

        _, gelu_vjp = jax.vjp(jax.nn.gelu, y_ref[...])
        dy_t = dyd_ref[...] + gelu_vjp(dya0_ref[...])[0]
        u = z_ref[...]
        dy16 = dy_t.astype(BF16)
        lam_ref[...] = jnp.dot(dy16, wct_ref[...], preferred_element_type=F32)
        _block_scan(lam_ref, cst_ref, carry_ref, sd, tt, reverse=True)
        lam = lam_ref[...]
        lam16 = lam.astype(BF16)
        du_ref[...] = (jnp.dot(lam16, wbt_ref[...], preferred_element_type=F32) + d_ref[...] * dy_t).astype(BF16)
        dd_ref[0:1, :] += jnp.sum(dy_t * u, axis=0, keepdims=True)
        dwb_ref[...] += lax.dot_general(u.astype(BF16), lam16, tn_dims, preferred_element_type=F32)
        s = s_ref[...]
        dwc_ref[...] += lax.dot_general(s.astype(BF16), dy16, tn_dims, preferred_element_type=F32)
        before = jnp.where(i == nt - 1, 0.0, 1.0) * sp_ref[SUBLANES - 1:SUBLANES, :]
        first_row = lax.broadcasted_iota(jnp.int32, s.shape, 0) == 0
        prev = jnp.where(first_row, jnp.broadcast_to(before, s.shape), pltpu.roll(s, 1, 0))
        lr, li = lam[:, :ns], lam[:, ns:]
        pr, pi_ = prev[:, :ns], prev[:, ns:]
        da_ref[0:1, 0:ns] += jnp.sum(lr * pr + li * pi_, axis=0, keepdims=True)
        da_ref[0:1, ns:ns2] += jnp.sum(li * pr - lr * pi_, axis=0, keepdims=True)

    rev = lambda i: nt - 1 - i
    return _call(
        body, side, [y_pre, dy_direct, dya0, z, states, states, wct, wbt, cst_rev, d_row], name=name, grid=(sd.nb, nt),
        in_specs=[pl.BlockSpec((tt, cb), lambda j, i: (rev(i), j)),
                  pl.BlockSpec((tt, cb), lambda j, i: (rev(i), j)),
                  pl.BlockSpec((tt, cb), lambda j, i: (rev(i), j)),
                  pl.BlockSpec((tt, cb), lambda j, i: (rev(i), j)),
                  pl.BlockSpec((tt, ns2), lambda j, i: (rev(i), j)),
                  pl.BlockSpec((SUBLANES, ns2), lambda j, i: (jnp.maximum(rev(i) * blocks_per_tile - 1, 0), j)),
                  pl.BlockSpec((None, cb, ns2), lambda j, i: (j, 0, 0)),
                  pl.BlockSpec((None, ns2, cb), lambda j, i: (j, 0, 0)),
                  pl.BlockSpec((None, 8 * SUBLANES, ns), lambda j, i: (j, 0, 0)),
                  pl.BlockSpec((1, cb), lambda j, i: (0, j))],
        out_specs=[pl.BlockSpec((tt, cb), lambda j, i: (rev(i), j)),
                   pl.BlockSpec((None, cb, ns2), lambda j, i: (j, 0, 0)),
                   pl.BlockSpec((None, ns2, cb), lambda j, i: (j, 0, 0)),
                   pl.BlockSpec((None, SUBLANES, ns2), lambda j, i: (j, 0, 0)),
                   pl.BlockSpec((None, SUBLANES, cb), lambda j, i: (j, 0, 0))],
        out_shape=[_sds((n_tok, sd.d), BF16), _sds((sd.nb, cb, ns2), F32), _sds((sd.nb, ns2, cb), F32),
                   _sds((sd.nb, SUBLANES, ns2), F32), _sds((sd.nb, SUBLANES, cb), F32)],
        scratch_shapes=[pltpu.VMEM((tt, ns2), F32), pltpu.VMEM((SUBLANES, ns2), F32)],
        compiler_params=_params(("arbitrary", "arbitrary")))


def _hosted(exch, fn, name, *args, **kw):
    side = exch.side(name)
    if side is None:
        return fn(name, *args, **kw)
    out, moved = fn(name, *args, side=side, **kw)
    exch.done(name, moved)
    return out


def _local_grads(x, p, target, sp, exch):
    n_tok, d_model = x.shape
    d_ssm = sp["ssm_d"].shape[0] * sp["ssm_d"].shape[1]
    d_sgu = sp["sgu_ln_g"].shape[-1]
    sd = _SsmDims(sp["ssm_b_re"].shape[0], sp["ssm_b_re"].shape[1], sp["ssm_b_re"].shape[2])
    heads, chunk, _ = sp["sgu_w"].shape
    row = lambda v: v.reshape(1, -1)
    tok = lambda w, dt=F32: _sds((n_tok, w), dt)
    acc = lambda w: _sds((1, w), F32)

    g_mix = row(sp["norm_mix_g"])
    (h1,) = _rowwise("norm_mix", lambda a, g: _rms(a, g), [x], [g_mix], [tok(d_model, BF16)])
    z = _hosted(exch, _mm_nn, "proj_in", h1, exch.weight("w_in"), sharded=True, tn=768)

    ssm_rows = _ssm_rows(sd, sp)
    wb, wbt, wc, wct, cst_fwd, cst_rev = _ssm_operands(sd, ssm_rows)
    d_row = row(sp["ssm_d"])
    y_pre, states, ya0_16 = _hosted(exch, _ssm_fwd, "ssm_fwd", sd, z, wb, wc, cst_fwd, d_row)
    q = _mm_nn("ssm_glu", ya0_16, exch.weight("ssm_glu_w"), tm=1024)
    glu_b, g_ossm = row(sp["ssm_glu_b"]), row(sp["out_norm_ssm_g"])
    (ya_n,) = _rowwise("ssm_glu_out", _glu_out, [y_pre, q], [glu_b, g_ossm], [tok(d_ssm, BF16)])

    assert d_ssm == d_sgu
    zu, zv = _Cols(z, d_sgu, 1), _Cols(z, d_sgu, 2)
    ln_g, ln_b, g_osgu = row(sp["sgu_ln_g"]), row(sp["sgu_ln_b"]), row(sp["out_norm_sgu_g"])
    b_st = sp["sgu_b"].T
    sgu_tr = 2 * chunk
    (yb_n,) = _rowwise("sgu", _sgu_rows, [zu, zv], [ln_g, ln_b, sp["sgu_w"], b_st, g_osgu], [tok(d_sgu, BF16)], tr=sgu_tr)

    ycat = jnp.concatenate([ya_n, yb_n], axis=1)
    x1 = _mm_nn("proj_out", ycat, exch.weight("w_out"), res=x, tm=1024)

    g_ffn = row(sp["norm_ffn_g"])
    (h2,) = _rowwise("norm_ffn", lambda a, g: _rms(a, g), [x1], [g_ffn], [tok(d_model, BF16)])
    act, gu16 = _hosted(exch, _ffn_in_swiglu, "ffn_in", h2, exch.weight("w_ffn_in"))
    x2 = _mm_nn("ffn_out", act, exch.weight("w_ffn_out"), res=x1)

    g_ple = row(sp["norm_ple_g"])
    (h3,) = _rowwise("norm_ple", lambda a, g: _rms(a, g), [x2], [g_ple], [tok(d_model, BF16)])
    gpre = _mm_nn("ple_gate", h3, exch.weight("w_ple_gate"), tm=1024)
    (p16,) = _rowwise("ple_cast", lambda a: a, [p], [], [tok(p.shape[1], BF16)])
    pp = _mm_nn("ple_proj", p16, exch.weight("w_ple_proj"), sharded=True, tm=1024)

    b_g, g_fin = row(sp["b_ple_gate"]), row(sp["final_norm_g"])

    def head(x2_t, gpre_t, pp_t, tgt_t, b_g_v, g_fin_v):
        loss, grads = jax.value_and_grad(_head_loss, argnums=(0, 1, 2, 3, 4))(x2_t, gpre_t, pp_t, b_g_v, g_fin_v, tgt_t)
        dx2, dgpre, dpp, db, dg = grads
        return dx2, dgpre.astype(BF16), dpp.astype(BF16), jnp.full((1, LANES), loss, F32), db, dg

    dx2_head, dgpre16, dpp16, loss_row, d_b_g, d_g_fin = _rowwise(
        "head", head, [x2, gpre, pp, target], [b_g, g_fin],
        [tok(d_model), tok(d_model, BF16), tok(d_model, BF16)], [acc(LANES), acc(d_model), acc(d_model)])
    loss = loss_row[0, 0]

    exch.grad("w_ple_proj", _mm_tn("d_ple_proj", p16, dpp16, shards=N_CHIPS, tk=256))
    exch.grad("w_ple_gate", _mm_tn("d_ple_gate", h3, dgpre16))
    dh3 = _mm_nt("d_h3", dgpre16, exch.weight("w_ple_gate"), tm=1024)

    def norm_bwd(x_t, dres_t, dh_t, g_v):
        _, vjp = jax.vjp(_rms, x_t, g_v)
        dx, dg = vjp(dh_t)
        dx = dres_t + dx
        return dx, dx.astype(BF16), dg

    dx2, dx2_16, d_g_ple = _rowwise("d_norm_ple", norm_bwd, [x2, dx2_head, dh3], [g_ple],
                                    [tok(d_model), tok(d_model, BF16)], [acc(d_model)])
    exch.grad("w_ffn_out", _mm_tn("d_ffn_out", act, dx2_16))
    dgu16 = _hosted(exch, _d_act_swiglu, "d_act", dx2_16, exch.weight("w_ffn_out"), gu16)
    exch.grad("w_ffn_in", _hosted(exch, _mm_tn, "d_ffn_in", h2, dgu16, shards=N_CHIPS, g_halves=True, tn=1408, g_resident=True))
    dh2 = _hosted(exch, _mm_nt, "d_h2", dgu16, exch.weight("w_ffn_in"), sharded=True, g_halves=True, tm=256, w_resident=True)
    dx1, dx1_16, d_g_ffn = _rowwise("d_norm_ffn", norm_bwd, [x1, dx2, dh2], [g_ffn],
                                    [tok(d_model), tok(d_model, BF16)], [acc(d_model)])
    exch.grad("w_out", _mm_tn("d_proj_out", ycat, dx1_16))
    dycat = _mm_nt("d_ycat", dx1_16, exch.weight("w_out"), tm=1024)

    def glu_out_bwd(y_pre_t, q_t, dy_t, glu_b_v, g_v):
        _, vjp = jax.vjp(_glu_out, y_pre_t, q_t, glu_b_v, g_v)
        dy_pre, dq, db, dg = vjp(dy_t)
        return dy_pre, dq.astype(BF16), db, dg

    dy_pre_a, dq16, d_glu_b, d_g_ossm = _rowwise(
        "d_ssm_glu_out", glu_out_bwd, [y_pre, q, _Cols(dycat, d_ssm, 0)], [glu_b, g_ossm],
        [tok(d_ssm), tok(d_ssm, BF16)], [acc(d_ssm), acc(d_ssm)])
    exch.grad("ssm_glu_w", _mm_tn("d_ssm_glu", ya0_16, dq16))
    dya0 = _hosted(exch, _mm_nt, "d_ya0", dq16, exch.weight("ssm_glu_w"), tm=1024)

    dz_ssm16, dwb, dwc, da, dd = _hosted(exch, _ssm_bwd, "ssm_bwd", sd, y_pre, dy_pre_a, dya0, z, states, wct, wbt,
                                         cst_rev, d_row)

    def sgu_bwd(zu_t, zv_t, dy_t, ln_g_v, ln_b_v, w_v, b_v, g_v):
        _, vjp = jax.vjp(_sgu_rows, zu_t, zv_t, ln_g_v, ln_b_v, w_v, b_v, g_v)
        dzu, dzv, dlg, dlb, dw, db, dg = vjp(dy_t)
        return dzu, dzv, dlg, dlb, dw, db, dg

    dzu, dzv, d_ln_g, d_ln_b, d_sgu_w, d_b_st, d_g_osgu = _hosted(
        exch, _rowwise, "d_sgu", sgu_bwd, [zu, zv, _Cols(dycat, d_sgu, 1)], [ln_g, ln_b, sp["sgu_w"], b_st, g_osgu],
        [tok(d_sgu, BF16), tok(d_sgu, BF16)],
        [acc(d_sgu), acc(d_sgu), _sds(sp["sgu_w"].shape, F32), _sds(b_st.shape, F32), acc(d_sgu)], tr=sgu_tr)

    dz16 = jnp.concatenate([dz_ssm16, dzu, dzv], axis=1)

    d_lam_re, d_lam_im, d_log_step, d_bt_re, d_bt_im, d_ct_re, d_ct_im = _ssm_param_grads(sd, ssm_rows, dwb, dwc, da)
    d_b_re, d_b_im = d_bt_re.T, d_bt_im.T
    d_c_re, d_c_im = (t.reshape(sd.h, sd.g, sd.p).transpose(1, 0, 2) for t in (d_ct_re, d_ct_im))
    d_ssm_d = dd[:, 0, :].reshape(sd.g, sd.h)

    exch.small_grads({
        "ssm_lambda_re": d_lam_re, "ssm_lambda_im": d_lam_im, "ssm_log_step": d_log_step,
        "ssm_b_re": d_b_re, "ssm_b_im": d_b_im, "ssm_c_re": d_c_re, "ssm_c_im": d_c_im, "ssm_d": d_ssm_d,
        "ssm_glu_b": d_glu_b, "sgu_ln_g": d_ln_g, "sgu_ln_b": d_ln_b, "sgu_w": d_sgu_w, "sgu_b": d_b_st.T,
        "out_norm_ssm_g": d_g_ossm, "out_norm_sgu_g": d_g_osgu, "norm_ffn_g": d_g_ffn, "norm_ple_g": d_g_ple,
        "b_ple_gate": d_b_g, "final_norm_g": d_g_fin,
    })

    exch.grad("w_in", _hosted(exch, _mm_tn, "d_proj_in", h1, dz16, shards=N_CHIPS, tn=768))
    dh1 = _hosted(exch, _mm_nt, "d_h1", dz16, exch.weight("w_in"), sharded=True, tm=1024)

    def norm_in_bwd(x_t, dres_t, dh_t, g_v):
        _, vjp = jax.vjp(_rms, x_t, g_v)
        dx, dg = vjp(dh_t)
        return dres_t + dx, dg

    grad_x, d_g_mix = _hosted(exch, _rowwise, "d_norm_mix", norm_in_bwd, [x, dx1, dh1], [g_mix], [tok(d_model)], [acc(d_model)])
    exch.small_grads({"norm_mix_g": d_g_mix})
    return loss, grad_x


def _place():
    x, y, c = lax.axis_index("x"), lax.axis_index("y"), lax.axis_index("c")
    chips = [(1 - x, y), (x, 1 - y), (1 - x, 1 - y)]
    return x, y, c, chips


def _cast_into_slot(name, w2d, shard, tr=256):
    rows, cols = w2d.shape
    rh = rows // 2
    tr = _pick(rh, tr, 16)
    per = rh // tr

    def body(s_ref, a_ref, o_ref):
        o_ref[...] = a_ref[...].astype(BF16)

    grid_spec = pltpu.PrefetchScalarGridSpec(
        num_scalar_prefetch=1, grid=(2, per),
        in_specs=[pl.BlockSpec((tr, cols), lambda h, i, s_ref: (h * per + i, 0))],
        out_specs=pl.BlockSpec((None, None, tr, cols), lambda h, i, s_ref: (s_ref[0], h, i, 0)))
    return pl.pallas_call(body, name=name, grid_spec=grid_spec, out_shape=_sds((N_CHIPS, 2, rh, cols), BF16),
                          compiler_params=_params(("arbitrary", "arbitrary")))(shard.reshape(1).astype(jnp.int32), w2d)


def _exchange_alone(name, side):
    n_in, n_out = len(side.ins), len(side.out_shapes)

    def body(*refs):
        ins, outs, sems = refs[:n_in], refs[n_in:n_in + n_out], refs[n_in + n_out:]
        side.first(ins, outs, *sems)
        if side.mid is not None:
            side.mid(ins, outs, *sems)
        side.last(ins, outs, *sems)

    return pl.pallas_call(
        body, name=name, in_specs=[ANY] * n_in, out_specs=[ANY] * n_out, out_shape=side.out_shapes,
        input_output_aliases=side.aliases,
        scratch_shapes=[pltpu.SemaphoreType.DMA((side.n_sems,)), pltpu.SemaphoreType.DMA((side.n_sems,))],
    )(*side.ins)


def _gather_side(slots):
    n = len(slots)

    def copies(kind, outs, send_sems, recv_sems):
        x, y, c, chips = _place()

        def remote(k, ref, to):
            return pltpu.make_async_remote_copy(src_ref=ref, dst_ref=ref, send_sem=send_sems.at[k], recv_sem=recv_sems.at[k],
                                                device_id=to, device_id_type=MESH)

        pairs = [(w, j, 2 * cx + cy, (cx, cy)) for w in range(n) for j, (cx, cy) in enumerate(chips)]
        if kind == "sends":
            return [remote(3 * w + j, outs[w].at[2 * x + y, c], (*chip, c)) for w, j, _, chip in pairs]
        if kind == "arrivals":
            return [remote(3 * w + j, outs[w].at[s, c], (x, y, c)) for w, j, s, _ in pairs]
        if kind == "passed":
            return [remote(3 * n + 3 * w + j, outs[w].at[s, c], (x, y, 1 - c)) for w, j, s, _ in pairs]
        return [remote(3 * n + 3 * w + j, outs[w].at[s, 1 - c], (x, y, c)) for w, j, s, _ in pairs]

    def first(ins, outs, *sems):
        for cp in copies("sends", outs, *sems):
            cp.start()

    def mid(ins, outs, *sems):
        for arrived, onward in zip(copies("arrivals", outs, *sems), copies("passed", outs, *sems)):
            arrived.wait_recv()
            onward.start()

    def last(ins, outs, *sems):
        for cp in copies("from_sibling", outs, *sems):
            cp.wait_recv()
        for cp in copies("sends", outs, *sems) + copies("passed", outs, *sems):
            cp.wait_send()

    return _Side(slots, [_sds(s.shape, s.dtype) for s in slots], 6 * n, first, last, mid=mid, aliases={w: w for w in range(n)})


def _swap_side(grads):
    n = len(grads)

    def copies(ins, outs, send_sems, recv_sems):
        x, y, c, _ = _place()
        return [pltpu.make_async_remote_copy(src_ref=ins[w].at[:, 1 - c], dst_ref=outs[w], send_sem=send_sems.at[w],
                                             recv_sem=recv_sems.at[w], device_id=(x, y, 1 - c), device_id_type=MESH)
                for w in range(n)]

    def first(*refs):
        for cp in copies(*refs):
            cp.start()

    def last(*refs):
        for cp in copies(*refs):
            cp.wait()

    return _Side(grads, [_sds((g.shape[0], *g.shape[2:]), g.dtype) for g in grads], n, first, last)


def _scatter_side(halves):
    n = len(halves)

    def copies(ins, outs, send_sems, recv_sems):
        x, y, c, chips = _place()
        return [pltpu.make_async_remote_copy(
            src_ref=ins[w].at[2 * cx + cy], dst_ref=outs[w].at[j], send_sem=send_sems.at[3 * w + j],
            recv_sem=recv_sems.at[3 * w + j], device_id=(cx, cy, c), device_id_type=MESH)
            for w in range(n) for j, (cx, cy) in enumerate(chips)]

    def first(*refs):
        for cp in copies(*refs):
            cp.start()

    def last(*refs):
        for cp in copies(*refs):
            cp.wait()

    return _Side(halves, [_sds((3, *h.shape[1:]), h.dtype) for h in halves], 3 * n, first, last)


def _join_halves(name, slots):
    n = len(slots)

    def body(*refs):
        outs = refs[n:2 * n]
        send_sems, recv_sems = refs[2 * n:]
        x, y, c, _ = _place()

        def copy(w, half, to):
            return pltpu.make_async_remote_copy(src_ref=outs[w].at[half], dst_ref=outs[w].at[half], send_sem=send_sems.at[w],
                                                recv_sem=recv_sems.at[w], device_id=to, device_id_type=MESH)

        copies = [copy(w, c, (x, y, 1 - c)) for w in range(n)]
        for cp in copies:
            cp.start()
        for w in range(n):
            copy(w, 1 - c, (x, y, c)).wait_recv()
        for cp in copies:
            cp.wait_send()

    return pl.pallas_call(
        body, name=name, in_specs=[ANY] * n, out_specs=[ANY] * n,
        out_shape=[_sds(s.shape, s.dtype) for s in slots], input_output_aliases={w: w for w in range(n)},
        scratch_shapes=[pltpu.SemaphoreType.DMA((n,)), pltpu.SemaphoreType.DMA((n,))],
    )(*slots)


def _allreduce_small(block, tr=256):
    rows, lanes = block.shape
    tr = _pick(rows, tr, SUBLANES)

    def body(x_ref, o_ref, buf, send_sems, recv_sems):
        x, y, c, chips = _place()
        me, sibling = (x, y, c), (x, y, 1 - c)

        def slot(px, py, pc):
            return buf.at[4 * px + 2 * py + pc]

        def copy(k, block_of, to):
            return pltpu.make_async_remote_copy(src_ref=slot(*block_of), dst_ref=slot(*block_of), send_sem=send_sems.at[k],
                                                recv_sem=recv_sems.at[k], device_id=to, device_id_type=MESH)

        slot(*me)[...] = x_ref[...]
        first = [copy(0, me, sibling)] + [copy(1 + j, me, (*chip, c)) for j, chip in enumerate(chips)]
        for cp in first:
            cp.start()
        passed = [copy(4 + j, (*chip, c), sibling) for j, chip in enumerate(chips)]
        for j, chip in enumerate(chips):
            copy(1 + j, (*chip, c), me).wait_recv()
            passed[j].start()
        copy(0, sibling, me).wait_recv()
        for j, chip in enumerate(chips):
            copy(4 + j, (*chip, 1 - c), me).wait_recv()
        for cp in first + passed:
            cp.wait_send()
        for r0 in range(0, rows, tr):
            acc = buf[0, r0:r0 + tr, :]
            for k in range(1, N_DEV):
                acc = acc + buf[k, r0:r0 + tr, :]
            o_ref[r0:r0 + tr, :] = acc

    vm = pl.BlockSpec(memory_space=pltpu.VMEM)
    return pl.pallas_call(
        body, name="allreduce_small", in_specs=[vm], out_specs=vm, out_shape=_sds((rows, lanes), block.dtype),
        scratch_shapes=[pltpu.VMEM((N_DEV, rows, lanes), block.dtype), pltpu.SemaphoreType.DMA((7,)), pltpu.SemaphoreType.DMA((7,))],
        compiler_params=pltpu.CompilerParams(vmem_limit_bytes=VMEM_LIMIT),
    )(block)


def _small_gather_side(block):
    def copy(kind, j, ins, outs, send_sems, recv_sems):
        x, y, c, chips = _place()
        chip = chips[j] if j is not None else None
        slot = lambda px, py, pc: outs[0].at[4 * px + 2 * py + pc]

        def remote(k, src, dst, to):
            return pltpu.make_async_remote_copy(src_ref=src, dst_ref=dst, send_sem=send_sems.at[k], recv_sem=recv_sems.at[k],
                                                device_id=to, device_id_type=MESH)

        if kind == "to_sibling":
            return remote(0, ins[0], slot(x, y, c), (x, y, 1 - c))
        if kind == "from_sibling":
            return remote(0, ins[0], slot(x, y, 1 - c), (x, y, c))
        if kind == "to_chip":
            return remote(1 + j, ins[0], slot(x, y, c), (*chip, c))
        if kind == "from_chip":
            return remote(1 + j, ins[0], slot(*chip, c), (x, y, c))
        if kind == "pass_on":
            return remote(4 + j, slot(*chip, c), slot(*chip, c), (x, y, 1 - c))
        return remote(4 + j, slot(*chip, 1 - c), slot(*chip, 1 - c), (x, y, c))

    def first(*refs):
        copy("to_sibling", None, *refs).start()
        for j in range(3):
            copy("to_chip", j, *refs).start()

    def mid(*refs):
        for j in range(3):
            copy("from_chip", j, *refs).wait_recv()
            copy("pass_on", j, *refs).start()

    def last(*refs):
        copy("from_sibling", None, *refs).wait_recv()
        for j in range(3):
            copy("passed_on", j, *refs).wait_recv()
        copy("to_sibling", None, *refs).wait_send()
        for j in range(3):
            copy("to_chip", j, *refs).wait_send()
            copy("pass_on", j, *refs).wait_send()

    return _Side([block], [_sds((N_DEV, *block.shape), block.dtype)], 7, first, last, mid=mid)


def _sum_slots(name, own, gathered, me, tr=512):
    n, rows, cols = gathered.shape
    tr = _pick(rows, tr, SUBLANES)

    def body(me_ref, own_ref, g_ref, o_ref):
        mine = own_ref[...]
        acc = jnp.where(me_ref[0] == 0, mine, g_ref[0])
        for k in range(1, n):
            acc = acc + jnp.where(me_ref[0] == k, mine, g_ref[k])
        o_ref[...] = acc

    grid_spec = pltpu.PrefetchScalarGridSpec(
        num_scalar_prefetch=1, grid=(rows // tr,),
        in_specs=[pl.BlockSpec((tr, cols), lambda i, me_ref: (i, 0)), pl.BlockSpec((n, tr, cols), lambda i, me_ref: (0, i, 0))],
        out_specs=pl.BlockSpec((tr, cols), lambda i, me_ref: (i, 0)))
    return pl.pallas_call(body, name=name, grid_spec=grid_spec, out_shape=_sds((rows, cols), own.dtype),
                          compiler_params=_params(("arbitrary",)))(me.reshape(1).astype(jnp.int32), own, gathered)


def _sum_received(name, own, received, c, tr=256):
    n, rows, cols = received.shape
    tr = _pick(rows, tr, 16)

    def body(c_ref, a_ref, s_ref, o_ref):
        acc = a_ref[...]
        for k in range(n):
            acc = acc + s_ref[k].astype(F32)
        o_ref[...] = acc

    grid_spec = pltpu.PrefetchScalarGridSpec(
        num_scalar_prefetch=1, grid=(rows // tr,),
        in_specs=[pl.BlockSpec((tr, cols), lambda i, c_ref: (i, 0)), pl.BlockSpec((n, tr, cols), lambda i, c_ref: (0, i, 0))],
        out_specs=pl.BlockSpec((None, tr, cols), lambda i, c_ref: (c_ref[0], i, 0)))
    return pl.pallas_call(body, name=name, grid_spec=grid_spec, out_shape=_sds((2, rows, cols), F32),
                          compiler_params=_params(("arbitrary",)))(c.reshape(1).astype(jnp.int32), own, received)


def _add_halves(name, full, c, received, tr=256):
    s, _, rh, cols = full.shape
    tr = _pick(rh, tr, 16)

    def body(c_ref, a_ref, b_ref, o_ref):
        o_ref[...] = (a_ref[...] + b_ref[...]).astype(BF16)

    grid_spec = pltpu.PrefetchScalarGridSpec(
        num_scalar_prefetch=1, grid=(s, rh // tr),
        in_specs=[pl.BlockSpec((None, None, tr, cols), lambda q, i, c_ref: (q, c_ref[0], i, 0)),
                  pl.BlockSpec((None, tr, cols), lambda q, i, c_ref: (q, i, 0))],
        out_specs=pl.BlockSpec((None, tr, cols), lambda q, i, c_ref: (q, i, 0)))
    return pl.pallas_call(body, name=name, grid_spec=grid_spec, out_shape=_sds((s, rh, cols), BF16),
                          compiler_params=_params(("arbitrary", "arbitrary")))(c.reshape(1).astype(jnp.int32), full, received)


def _own_half(name, full, c, shard, received, tr=256):
    _, _, rh, cols = full.shape
    tr = _pick(rh, tr, SUBLANES)

    def body(i_ref, a_ref, b_ref, o_ref):
        o_ref[...] = a_ref[...] + b_ref[...]

    grid_spec = pltpu.PrefetchScalarGridSpec(
        num_scalar_prefetch=1, grid=(rh // tr,),
        in_specs=[pl.BlockSpec((None, None, tr, cols), lambda i, i_ref: (i_ref[1], i_ref[0], i, 0)),
                  pl.BlockSpec((None, tr, cols), lambda i, i_ref: (i_ref[1], i, 0))],
        out_specs=pl.BlockSpec((tr, cols), lambda i, i_ref: (i, 0)))
    return pl.pallas_call(body, name=name, grid_spec=grid_spec, out_shape=_sds((rh, cols), F32),
                          compiler_params=_params(("arbitrary",)))(jnp.stack([c, shard]).astype(jnp.int32), full, received)


LARGE = ("w_in", "ssm_glu_w", "w_out", "w_ffn_in", "w_ffn_out", "w_ple_gate", "w_ple_proj")
COLUMN_SHARDED = ("w_in", "w_ffn_in", "w_ple_proj")
SMALL = ("norm_mix_g", "ssm_lambda_re", "ssm_lambda_im", "ssm_log_step", "ssm_b_re", "ssm_b_im", "ssm_c_re", "ssm_c_im",
         "ssm_d", "ssm_glu_b", "sgu_ln_g", "sgu_ln_b", "sgu_w", "sgu_b", "out_norm_ssm_g", "out_norm_sgu_g", "norm_ffn_g",
         "norm_ple_g", "b_ple_gate", "final_norm_g")
WEIGHTS = ("norm_mix_g", "w_in", "ssm_lambda_re", "ssm_lambda_im", "ssm_log_step", "ssm_b_re", "ssm_b_im", "ssm_c_re",
           "ssm_c_im", "ssm_d", "ssm_glu_w", "ssm_glu_b", "sgu_ln_g", "sgu_ln_b", "sgu_w", "sgu_b", "out_norm_ssm_g",
           "out_norm_sgu_g", "w_out", "norm_ffn_g", "w_ffn_in", "w_ffn_out", "norm_ple_g", "w_ple_gate", "b_ple_gate",
           "w_ple_proj", "final_norm_g")
PACK_ROWS = SUBLANES * LANES


def _pack(arrays):
    parts = []
    for a in arrays:
        flat = a.reshape(-1).astype(F32)
        pad = -flat.shape[0] % PACK_ROWS
        parts.append(jnp.pad(flat, (0, pad)) if pad else flat)
    return jnp.concatenate(parts).reshape(-1, LANES)


def _unpack(packed, like):
    flat = packed.reshape(-1)
    out, at = [], 0
    for a in like:
        size = a.size
        out.append(flat[at:at + size].reshape(a.shape))
        at += size + (-size % PACK_ROWS)
    return out


class _NoExchange:
    def __init__(self, weights):
        self.weights, self.grads, self.small = weights, {}, {}

    def weight(self, name):
        return self.weights[name]

    def grad(self, name, g):
        self.grads[name] = g

    def small_grads(self, grads):
        self.small.update(grads)

    def side(self, host):
        return None


class _MeshExchange:
    GATHER = {"proj_in": ("ssm_glu_w", "w_out"), "ssm_fwd": ("w_ffn_in",), "ffn_in": ("w_ffn_out", "w_ple_gate", "w_ple_proj")}
    SWAP = {"d_act": ("w_ple_proj", "w_ple_gate", "w_ffn_out"), "d_h2": ("w_ffn_in",), "d_ya0": ("w_out", "ssm_glu_w"),
            "d_h1": ("w_in",)}
    SWAP_ALONE = ()
    SCATTER = {"d_ffn_in": ("w_ple_proj", "w_ple_gate", "w_ffn_out"), "ssm_bwd": ("w_ffn_in",),
               "d_sgu": ("w_out", "ssm_glu_w"), "d_norm_mix": ("w_in",)}
    SMALL_GATHER = "d_proj_in"
    GROUPS = (("w_ple_proj", "w_ple_gate", "w_ffn_out"), ("w_ffn_in",), ("w_out", "ssm_glu_w"), ("w_in",))

    def __init__(self, shards, small_like, c, shard, me):
        self.c, self.shard, self.me, self.small_like = c, shard, me, small_like
        self.slots = {k: _cast_into_slot("cast_" + k, shards[k], shard) for k in LARGE}
        (self.slots["w_in"],) = _exchange_alone("allgather_w_in", _gather_side([self.slots["w_in"]]))
        self.full, self.received, self.halves, self.quarters, self.small = {}, {}, {}, {}, {}

    def weight(self, name):
        g = self.slots[name]
        _, _, rh, cols = g.shape
        return g.reshape(N_CHIPS, 2 * rh, cols) if name in COLUMN_SHARDED else g.reshape(N_CHIPS * 2 * rh, cols)

    def grad(self, name, g):
        if name not in COLUMN_SHARDED:
            g = g.reshape(N_CHIPS, g.shape[0] // N_CHIPS, g.shape[1])
        self.full[name] = g.reshape(N_CHIPS, 2, g.shape[1] // 2, g.shape[2])
        if name in self.SWAP_ALONE:
            self._swapped((name,), _exchange_alone("grad_swap_" + name, _swap_side([self.full[name]])))

    def _swapped(self, names, received):
        for k, r in zip(names, received):
            self.received[k] = r
            self.halves[k] = _add_halves("grad_add_halves_" + k, self.full[k], self.c, r)

    def small_grads(self, grads):
        self.small.update(grads)

    def _packed(self, names):
        return _pack([self.small[k].reshape(self.small_like[k].shape) for k in names])

    def side(self, host):
        if host in self.GATHER:
            return _gather_side([self.slots[k] for k in self.GATHER[host]])
        if host in self.SWAP:
            return _swap_side([self.full[k] for k in self.SWAP[host]])
        if host in self.SCATTER:
            return _scatter_side([self.halves[k] for k in self.SCATTER[host]])
        if host == self.SMALL_GATHER:
            self.packed_early = self._packed(SMALL[1:])
            return _small_gather_side(self.packed_early)
        return None

    def done(self, host, moved):
        if host in self.GATHER:
            self.slots.update(zip(self.GATHER[host], moved))
        elif host in self.SWAP:
            self._swapped(self.SWAP[host], moved)
        elif host in self.SCATTER:
            self.quarters.update(zip(self.SCATTER[host], moved))
        else:
            (self.gathered_early,) = moved

    def small_reduced(self):
        early = _sum_slots("small_sum", self.packed_early, self.gathered_early, self.me)
        late = _allreduce_small(self._packed(SMALL[:1]))
        return jnp.concatenate([late, early], axis=0)

    def reduced(self, group):
        own = [_own_half("grad_own_" + k, self.full[k], self.c, self.shard, self.received[k]) for k in group]
        parts = [_sum_received("grad_sum_" + k, o, self.quarters[k], self.c) for k, o in zip(group, own)]
        joined = _join_halves("grad_join_" + group[0], parts)
        return {k: j.reshape(2 * j.shape[1], j.shape[2]) for k, j in zip(group, joined)}


def kernel(x, p, norm_mix_g, w_in, ssm_lambda_re, ssm_lambda_im, ssm_log_step, ssm_b_re, ssm_b_im, ssm_c_re, ssm_c_im, ssm_d, ssm_glu_w, ssm_glu_b, sgu_ln_g, sgu_ln_b, sgu_w, sgu_b, out_norm_ssm_g, out_norm_sgu_g, w_out, norm_ffn_g, w_ffn_in, w_ffn_out, norm_ple_g, w_ple_gate, b_ple_gate, w_ple_proj, final_norm_g, loss_target, m_norm_mix_g, m_w_in, m_ssm_lambda_re, m_ssm_lambda_im, m_ssm_log_step, m_ssm_b_re, m_ssm_b_im, m_ssm_c_re, m_ssm_c_im, m_ssm_d, m_ssm_glu_w, m_ssm_glu_b, m_sgu_ln_g, m_sgu_ln_b, m_sgu_w, m_sgu_b, m_out_norm_ssm_g, m_out_norm_sgu_g, m_w_out, m_norm_ffn_g, m_w_ffn_in, m_w_ffn_out, m_norm_ple_g, m_w_ple_gate, m_b_ple_gate, m_w_ple_proj, m_final_norm_g, v_norm_mix_g, v_w_in, v_ssm_lambda_re, v_ssm_lambda_im, v_ssm_log_step, v_ssm_b_re, v_ssm_b_im, v_ssm_c_re, v_ssm_c_im, v_ssm_d, v_ssm_glu_w, v_ssm_glu_b, v_sgu_ln_g, v_sgu_ln_b, v_sgu_w, v_sgu_b, v_out_norm_ssm_g, v_out_norm_sgu_g, v_w_out, v_norm_ffn_g, v_w_ffn_in, v_w_ffn_out, v_norm_ple_g, v_w_ple_gate, v_b_ple_gate, v_w_ple_proj, v_final_norm_g):
    given = dict(locals())
    w = {k: given[k] for k in WEIGHTS}
    m = {k: given["m_" + k] for k in WEIGHTS}
    v = {k: given["v_" + k] for k in WEIGHTS}
    c = lax.axis_index("c")
    shard = 2 * lax.axis_index("x") + lax.axis_index("y")

    exch = _MeshExchange({k: w[k].reshape(w[k].shape[1:]) for k in LARGE}, {k: w[k] for k in SMALL}, c, shard, 2 * shard + c)
    unlayer = lambda a: a if a.ndim == 1 else a[0]
    sp = {k: unlayer(w[k]) for k in SMALL}
    n_tok, d_model = x.shape[1:]
    loss, grad_x = _local_grads(x.reshape(n_tok, d_model), p.reshape(n_tok, p.shape[-1]),
                                loss_target.reshape(n_tok, d_model), sp, exch)
    loss = lax.psum(loss, ("x", "y", "c"))

    grad_w, delta_w, new_m, new_v = {}, {}, {}, {}
    for group in exch.GROUPS:
        reduced = exch.reduced(group)
        for k in group:
            shape = w[k].shape
            two_d = lambda a: a.reshape(shape[1:])
            like = _sds(shape[1:], F32)
            d_k, m_k, v_k = _hosted(exch, _rowwise, "adamw_" + k, _adamw, [two_d(w[k]), reduced[k], two_d(m[k]), two_d(v[k])],
                                    [], [like, like, like])
            grad_w[k], delta_w[k], new_m[k], new_v[k] = (a.reshape(shape) for a in (reduced[k], d_k, m_k, v_k))

    packed_g = exch.small_reduced()
    like = _sds(packed_g.shape, F32)
    d_s, m_s, v_s = _rowwise("adamw_small", _adamw, [_pack([w[k] for k in SMALL]), packed_g, _pack([m[k] for k in SMALL]),
                                                     _pack([v[k] for k in SMALL])], [], [like, like, like])
    shapes = [w[k] for k in SMALL]
    for k, g_k, d_k, m_k, v_k in zip(SMALL, _unpack(packed_g, shapes), _unpack(d_s, shapes), _unpack(m_s, shapes), _unpack(v_s, shapes)):
        grad_w[k], delta_w[k], new_m[k], new_v[k] = g_k, d_k, m_k, v_k

    return (loss, grad_x.reshape(x.shape), *[grad_w[k] for k in WEIGHTS], *[delta_w[k] for k in WEIGHTS],
            *[new_m[k] for k in WEIGHTS], *[new_v[k] for k in WEIGHTS])
```

```python
import functools

import jax
import jax.numpy as jnp
from jax import lax
from jax.experimental import pallas as pl
from jax.experimental.pallas import tpu as pltpu

F32 = jnp.float32
BF16 = jnp.bfloat16

EPS = 1e-6
LAMBDA_RE_MAX = -1e-4
ADAM_LR = 0.001
ADAM_B1 = 0.9
ADAM_B2 = 0.999
ADAM_EPS = 1e-08
ADAM_WD = 0.01
ADAM_STEP = 10

N_CHIPS = 4
N_DEV = 8
SUBLANES = 8
LANES = 128
SSM_CH_BLOCK = 256
SCAN_LANES = 256
SCAN_BLOCKS = 2
VMEM_LIMIT = 56 * 1024 * 1024

MESH = pl.DeviceIdType.MESH


def _pick(n, pref, mult):
    if n <= pref:
        return n
    t = (pref // mult) * mult
    while t >= mult:
        if n % t == 0:
            return t
        t -= mult
    return n


def _params(semantics):
    return pltpu.CompilerParams(dimension_semantics=semantics, vmem_limit_bytes=VMEM_LIMIT)


class _Cols:
    def __init__(self, arr, width, blk):
        self.arr, self.width, self.blk = arr, width, blk


def _sds(shape, dtype):
    return jax.ShapeDtypeStruct(tuple(shape), dtype)


ANY = pl.BlockSpec(memory_space=pl.ANY)


class _Side:
    def __init__(self, ins, out_shapes, n_sems, first, last, mid=None, aliases=None, mid_late=False):
        self.ins, self.out_shapes, self.n_sems = list(ins), list(out_shapes), n_sems
        self.first, self.mid, self.last, self.mid_late = first, mid, last, mid_late
        self.aliases = dict(aliases or {})


def _call(body, side, operands, *, name, grid, in_specs, out_specs, out_shape, compiler_params, scratch_shapes=()):
    if side is None:
        return pl.pallas_call(body, name=name, grid=grid, in_specs=in_specs, out_specs=out_specs, out_shape=out_shape,
                              scratch_shapes=list(scratch_shapes), compiler_params=compiler_params)(*operands)
    single = not isinstance(out_specs, (list, tuple))
    out_specs = [out_specs] if single else list(out_specs)
    out_shape = [out_shape] if single else list(out_shape)
    n_in, n_out, n_scr = len(in_specs), len(out_specs), len(scratch_shapes)
    n_sin, n_sout = len(side.ins), len(side.out_shapes)
    steps = 1
    for g in grid:
        steps *= g

    def hosted(*refs):
        ins, s_ins = refs[:n_in], refs[n_in:n_in + n_sin]
        at = n_in + n_sin
        outs, s_outs = refs[at:at + n_out], refs[at + n_out:at + n_out + n_sout]
        scratch = refs[at + n_out + n_sout:at + n_out + n_sout + n_scr]
        sems = refs[-2:]
        step = pl.program_id(0)
        for d in range(1, len(grid)):
            step = step * grid[d] + pl.program_id(d)

        @pl.when(step == 0)
        def _():
            side.first(s_ins, s_outs, *sems)

        if side.mid is not None:
            @pl.when(step == (steps - 1 if side.mid_late else (3 * steps) // 4))
            def _():
                side.mid(s_ins, s_outs, *sems)

        body(*ins, *outs, *scratch)

        @pl.when(step == steps - 1)
        def _():
            side.last(s_ins, s_outs, *sems)

    res = pl.pallas_call(
        hosted, name=name, grid=grid, in_specs=[*in_specs, *[ANY] * n_sin], out_specs=[*out_specs, *[ANY] * n_sout],
        out_shape=[*out_shape, *side.out_shapes], input_output_aliases={n_in + i: n_out + o for i, o in side.aliases.items()},
        scratch_shapes=[*scratch_shapes, pltpu.SemaphoreType.DMA((side.n_sems,)), pltpu.SemaphoreType.DMA((side.n_sems,))],
        compiler_params=compiler_params)(*operands, *side.ins)
    return (res[0] if single else list(res[:n_out])), list(res[n_out:])


def _rowwise(name, fn, rows, params, row_outs, acc_outs=(), tr=256, side=None):
    rows = [r if isinstance(r, _Cols) else _Cols(r, r.shape[1], 0) for r in rows]
    m = rows[0].arr.shape[0]
    tr = _pick(m, tr, 16)
    n_in = len(rows) + len(params)
    n_ro = len(row_outs)

    def body(*refs):
        vals = fn(*[r[...] for r in refs[:n_in]])
        if not isinstance(vals, (tuple, list)):
            vals = (vals,)
        outs = refs[n_in:]
        for r, v in zip(outs[:n_ro], vals[:n_ro]):
            r[...] = v.astype(r.dtype)
        first = pl.program_id(0) == 0
        for r, v in zip(outs[n_ro:], vals[n_ro:]):
            @pl.when(first)
            def _():
                r[...] = jnp.zeros(r.shape, r.dtype)
            r[...] += v.astype(r.dtype).reshape(r.shape)

    in_specs = [pl.BlockSpec((tr, r.width), lambda i, b=r.blk: (i, b)) for r in rows]
    in_specs += [pl.BlockSpec(p.shape, lambda i, nd=p.ndim: (0,) * nd) for p in params]
    out_specs = [pl.BlockSpec((tr, o.shape[1]), lambda i: (i, 0)) for o in row_outs]
    out_specs += [pl.BlockSpec(o.shape, lambda i, nd=len(o.shape): (0,) * nd) for o in acc_outs]
    return _call(body, side, [*[r.arr for r in rows], *params], name=name, grid=(m // tr,), in_specs=in_specs,
                 out_specs=out_specs, out_shape=[*row_outs, *acc_outs], compiler_params=_params(("arbitrary",)))


def _grid_order(swap):
    if not swap:
        return (lambda grid: grid), (lambda f: f)
    return (lambda grid: grid[::-1]), (lambda f: (lambda j, i: f(i, j)))


def _mm_nn(name, a, w, *, sharded=False, res=None, out_dtype=F32, tm=512, tn=512, w_resident=False, side=None):
    m, k = a.shape
    tm = _pick(m, tm, 16)
    order, ix = _grid_order(w_resident)
    if sharded:
        s, _, ns = w.shape
        n = s * ns
        tn = _pick(ns, tn, LANES)
        per = ns // tn
        w_spec = pl.BlockSpec((None, k, tn), ix(lambda i, j: (j // per, 0, j % per)))
    else:
        n = w.shape[1]
        tn = _pick(n, tn, LANES)
        w_spec = pl.BlockSpec((k, tn), ix(lambda i, j: (0, j)))

    def body(a_ref, w_ref, *rest):
        acc = jnp.dot(a_ref[...], w_ref[...], preferred_element_type=F32)
        if res is not None:
            acc = acc + rest[0][...]
        rest[-1][...] = acc.astype(out_dtype)

    in_specs = [pl.BlockSpec((tm, k), ix(lambda i, j: (i, 0))), w_spec]
    ops = [a, w]
    if res is not None:
        in_specs.append(pl.BlockSpec((tm, tn), ix(lambda i, j: (i, j))))
        ops.append(res)
    return _call(body, side, ops, name=name, grid=order((m // tm, n // tn)), in_specs=in_specs,
                 out_specs=pl.BlockSpec((tm, tn), ix(lambda i, j: (i, j))), out_shape=_sds((m, n), out_dtype),
                 compiler_params=_params(("arbitrary", "arbitrary")))


def _mm_nt(name, g, w, *, sharded=False, g_halves=False, out_dtype=F32, tm=512, tk=512, w_resident=False, side=None):
    m, n = g.shape[-2:]
    tm = _pick(m, tm, 16)
    order, ix = _grid_order(w_resident)
    dims = (((1,), (1,)), ((), ()))
    g_spec = pl.BlockSpec((2, tm, n), ix(lambda i, j: (0, i, 0))) if g_halves else pl.BlockSpec((tm, n), ix(lambda i, j: (i, 0)))
    if sharded:
        s, k, ns = w.shape
        tk = _pick(k, tk, LANES)
        w_spec = pl.BlockSpec((s, tk, ns), ix(lambda i, j: (0, j, 0)))

        def columns(g_ref, q):
            if not g_halves:
                return g_ref[:, q * ns:(q + 1) * ns]
            half, at = divmod(q, s // 2)
            return g_ref[half, :, at * ns:(at + 1) * ns]

        def body(g_ref, w_ref, o_ref):
            acc = lax.dot_general(columns(g_ref, 0), w_ref[0], dims, preferred_element_type=F32)
            for q in range(1, s):
                acc = acc + lax.dot_general(columns(g_ref, q), w_ref[q], dims, preferred_element_type=F32)
            o_ref[...] = acc.astype(out_dtype)
    else:
        k = w.shape[0]
        tk = _pick(k, tk, LANES)
        w_spec = pl.BlockSpec((tk, n), ix(lambda i, j: (j, 0)))

        def body(g_ref, w_ref, o_ref):
            o_ref[...] = lax.dot_general(g_ref[...], w_ref[...], dims, preferred_element_type=F32).astype(out_dtype)

    return _call(body, side, [g, w], name=name, grid=order((m // tm, k // tk)), in_specs=[g_spec, w_spec],
                 out_specs=pl.BlockSpec((tm, tk), ix(lambda i, j: (i, j))), out_shape=_sds((m, k), out_dtype),
                 compiler_params=_params(("arbitrary", "arbitrary")))


def _mm_tn(name, a, g, *, shards=0, g_halves=False, tk=512, tn=512, g_resident=False, side=None):
    m, k = a.shape
    n = 2 * g.shape[2] if g_halves else g.shape[1]
    tk = _pick(k, tk, LANES)
    order, ix = _grid_order(g_resident)
    dims = (((0,), (0,)), ((), ()))
    if shards:
        ns = n // shards
        tn = _pick(ns, tn, LANES)
        per = ns // tn
        out_spec = pl.BlockSpec((None, tk, tn), ix(lambda i, j: (j // per, i, j % per)))
        out_shape = _sds((shards, k, ns), F32)
    else:
        tn = _pick(n, tn, LANES)
        out_spec = pl.BlockSpec((tk, tn), ix(lambda i, j: (i, j)))
        out_shape = _sds((k, n), F32)

    def body(a_ref, g_ref, o_ref):
        o_ref[...] = lax.dot_general(a_ref[...], g_ref[...], dims, preferred_element_type=F32)

    if g_halves:
        per_half = n // 2 // tn
        g_spec = pl.BlockSpec((None, m, tn), ix(lambda i, j: (j // per_half, 0, j % per_half)))
    else:
        g_spec = pl.BlockSpec((m, tn), ix(lambda i, j: (0, j)))
    return _call(body, side, [a, g], name=name, grid=order((k // tk, n // tn)),
                 in_specs=[pl.BlockSpec((m, tk), ix(lambda i, j: (0, i))), g_spec],
                 out_specs=out_spec, out_shape=out_shape, compiler_params=_params(("arbitrary", "arbitrary")))


def _ffn_in_swiglu(name, a, w, *, tm=512, tn=1408, side=None):
    m, k = a.shape
    s, _, ns = w.shape
    f = s * ns // 2
    tm = _pick(m, tm, 16)
    tn = _pick(ns, tn, LANES)
    per = ns // tn
    order, ix = _grid_order(True)

    def body(a_ref, wg_ref, wu_ref, act_ref, gu_ref):
        x = a_ref[...]
        gate = jnp.dot(x, wg_ref[...], preferred_element_type=F32)
        up = jnp.dot(x, wu_ref[...], preferred_element_type=F32)
        act_ref[...] = _swiglu(gate, up).astype(BF16)
        gu_ref[0] = gate.astype(BF16)
        gu_ref[1] = up.astype(BF16)

    return _call(body, side, [a, w, w], name=name, grid=order((m // tm, f // tn)),
                 in_specs=[pl.BlockSpec((tm, k), ix(lambda i, j: (i, 0))),
                           pl.BlockSpec((None, k, tn), ix(lambda i, j: (j // per, 0, j % per))),
                           pl.BlockSpec((None, k, tn), ix(lambda i, j: (s // 2 + j // per, 0, j % per)))],
                 out_specs=[pl.BlockSpec((tm, tn), ix(lambda i, j: (i, j))), pl.BlockSpec((2, tm, tn), ix(lambda i, j: (0, i, j)))],
                 out_shape=[_sds((m, f), BF16), _sds((2, m, f), BF16)], compiler_params=_params(("arbitrary", "arbitrary")))


def _d_act_swiglu(name, g, w, gu, *, tm=1024, tk=512, side=None):
    m, n = g.shape
    f = w.shape[0]
    tm = _pick(m, tm, 16)
    tk = _pick(f, tk, LANES)
    dims = (((1,), (1,)), ((), ()))

    def body(g_ref, w_ref, gu_ref, o_ref):
        dact = lax.dot_general(g_ref[...], w_ref[...], dims, preferred_element_type=F32)
        _, vjp = jax.vjp(_swiglu, gu_ref[0].astype(F32), gu_ref[1].astype(F32))
        dgate, dup = vjp(dact)
        o_ref[0] = dgate.astype(BF16)
        o_ref[1] = dup.astype(BF16)

    return _call(body, side, [g, w, gu], name=name, grid=(m // tm, f // tk),
                 in_specs=[pl.BlockSpec((tm, n), lambda i, j: (i, 0)), pl.BlockSpec((tk, n), lambda i, j: (j, 0)),
                           pl.BlockSpec((2, tm, tk), lambda i, j: (0, i, j))],
                 out_specs=pl.BlockSpec((2, tm, tk), lambda i, j: (0, i, j)), out_shape=_sds((2, m, f), BF16),
                 compiler_params=_params(("arbitrary", "arbitrary")))


def _rms(x, g):
    r = lax.rsqrt(jnp.mean(x * x, axis=-1, keepdims=True) + EPS)
    return (x * r) * g


def _glu_out(y_pre, q, glu_b, g_norm):
    ya0 = jax.nn.gelu(y_pre)
    return _rms(ya0 * jax.nn.sigmoid(q + glu_b), g_norm)


def _sgu_rows(zu, zv, ln_g, ln_b, w_s, b_st, g_norm):
    heads, t, _ = w_s.shape
    hd = zu.shape[1] // heads
    uu = jax.nn.gelu(zu)
    vv = jax.nn.gelu(zv)
    mu = jnp.mean(vv, axis=-1, keepdims=True)
    xc = vv - mu
    r = lax.rsqrt(jnp.mean(xc * xc, axis=-1, keepdims=True) + EPS)
    vn = (xc * r) * ln_g + ln_b
    row = lax.broadcasted_iota(jnp.int32, (t, t), 0)
    col = lax.broadcasted_iota(jnp.int32, (t, t), 1)
    causal = row >= col
    chunks = []
    for n in range(zu.shape[0] // t):
        blocks = []
        for h in range(heads):
            wm = jnp.where(causal, w_s[h], jnp.zeros_like(w_s[h])).astype(BF16)
            vb = vn[n * t:(n + 1) * t, h * hd:(h + 1) * hd].astype(BF16)
            blocks.append(jnp.dot(wm, vb, preferred_element_type=F32) + b_st[:, h:h + 1])
        chunks.append(jnp.concatenate(blocks, axis=1))
    s = jnp.concatenate(chunks, axis=0) if len(chunks) > 1 else chunks[0]
    return _rms(uu * s, g_norm)


def _swiglu(gate, up):
    return jax.nn.silu(gate) * up


def _head_loss(x2, gpre, pp, b_g, g_final, target):
    gate = jax.nn.sigmoid(gpre + b_g)
    out = _rms(x2 + gate * pp, g_final)
    err = jnp.square(out - target)
    return 0.5 * jnp.sum(jnp.mean(err, axis=-1))


def _ssm_disc(lam_re, lam_im, log_step):
    lr = jnp.minimum(lam_re, LAMBDA_RE_MAX)
    li = lam_im
    dt = jnp.exp(log_step)
    mag = jnp.exp(lr * dt)
    ang = li * dt
    abar_re = mag * jnp.cos(ang)
    abar_im = mag * jnp.sin(ang)
    nr = abar_re - 1.0
    ni = abar_im
    den = lr * lr + li * li
    q_re = (nr * lr + ni * li) / den
    q_im = (ni * lr - nr * li) / den
    return abar_re, abar_im, q_re, q_im


def _ssm_bbar(q_re, q_im, b_re, b_im):
    return q_re * b_re - q_im * b_im, q_re * b_im + q_im * b_re


def _ssm_discretised(lam_re, lam_im, log_step, bt_re, bt_im):
    ar, ai, qr, qi = _ssm_disc(lam_re, lam_im, log_step)
    return (ar, ai, *_ssm_bbar(qr, qi, bt_re, bt_im))


def _adamw(w, g, m, v):
    m = ADAM_B1 * m + (1.0 - ADAM_B1) * g
    v = ADAM_B2 * v + (1.0 - ADAM_B2) * jnp.square(g)
    m_hat = m / (1.0 - ADAM_B1 ** ADAM_STEP)
    v_hat = v / (1.0 - ADAM_B2 ** ADAM_STEP)
    delta = -ADAM_LR * (m_hat / (jnp.sqrt(v_hat) + ADAM_EPS) + ADAM_WD * w)
    return delta, m, v


class _SsmDims:
    def __init__(self, groups, state, gch):
        self.g, self.p, self.h = groups, state, gch
        self.d = groups * gch
        self.cb = min(SSM_CH_BLOCK, self.d)
        self.gb = self.cb // gch
        self.ns = self.gb * state
        self.nb = self.d // self.cb


def _ssm_rows(sd, sp):
    gp = sd.g * sd.p
    log_step = jnp.broadcast_to(sp["ssm_log_step"][:, None], (sd.g, sd.p)).reshape(1, gp)
    bt = [sp[k].reshape(gp, sd.h).T for k in ("ssm_b_re", "ssm_b_im")]
    ct = [sp[k].transpose(1, 0, 2).reshape(sd.h, gp) for k in ("ssm_c_re", "ssm_c_im")]
    return (sp["ssm_lambda_re"].reshape(1, gp), sp["ssm_lambda_im"].reshape(1, gp), log_step, *bt, *ct)


def _block_mask(sd):
    row = lax.broadcasted_iota(jnp.int32, (sd.cb, sd.ns), 0) // sd.h
    col = lax.broadcasted_iota(jnp.int32, (sd.cb, sd.ns), 1) // sd.p
    return row == col


def _scan_consts(pr, pi_, reverse):
    if reverse:
        pi_ = [-v for v in pi_]
    shape = (SUBLANES, pr[0].shape[1])
    rows = lax.broadcasted_iota(jnp.int32, shape, 0)
    parts = []
    for d in (1, 2, 4):
        keep = (rows < SUBLANES - d) if reverse else (rows >= d)
        parts += [jnp.where(keep, jnp.broadcast_to(v[d - 1], shape), 0.0) for v in (pr, pi_)]
    order = range(SUBLANES - 1, -1, -1) if reverse else range(SUBLANES)
    parts += [jnp.concatenate([v[t] for t in order], axis=0) for v in (pr, pi_)]
    return jnp.concatenate(parts, axis=0)


def _ssm_operands(sd, rows):
    cb, ns, nb = sd.cb, sd.ns, sd.nb

    def body(lam_re, lam_im, log_step, bt_re, bt_im, ct_re, ct_im, wb_ref, wbt_ref, wc_ref, wct_ref, cst_f_ref, cst_r_ref):
        ar, ai, bbar_re, bbar_im = _ssm_discretised(lam_re[...], lam_im[...], log_step[...], bt_re[...], bt_im[...])
        pr, pi_ = [ar], [ai]
        for _ in range(SUBLANES - 1):
            pr, pi_ = pr + [pr[-1] * ar - pi_[-1] * ai], pi_ + [pr[-1] * ai + pi_[-1] * ar]
        mask = _block_mask(sd)
        spread = lambda src: jnp.where(mask, jnp.concatenate([src] * sd.gb, axis=0), 0.0)
        for j in range(nb):
            at = slice(j * ns, (j + 1) * ns)
            w = jnp.concatenate([spread(bbar_re[:, at]), spread(bbar_im[:, at])], axis=1)
            v = jnp.concatenate([spread(ct_re[:, at]), -spread(ct_im[:, at])], axis=1)
            wb_ref[j] = w.astype(BF16)
            wbt_ref[j] = w.T.astype(BF16)
            wct_ref[j] = v.astype(BF16)
            wc_ref[j] = v.T.astype(BF16)
            pj, qj = [u[:, at] for u in pr], [u[:, at] for u in pi_]
            cst_f_ref[j] = _scan_consts(pj, qj, False)
            cst_r_ref[j] = _scan_consts(pj, qj, True)

    wide, tall = _sds((nb, cb, 2 * ns), BF16), _sds((nb, 2 * ns, cb), BF16)
    cst = _sds((nb, 8 * SUBLANES, ns), F32)
    vm = pl.BlockSpec(memory_space=pltpu.VMEM)
    return pl.pallas_call(body, name="ssm_operands", in_specs=[vm] * 7, out_specs=[vm] * 6,
                          out_shape=[wide, tall, tall, wide, cst, cst],
                          compiler_params=pltpu.CompilerParams(vmem_limit_bytes=VMEM_LIMIT))(*rows)


def _ssm_param_grads(sd, rows, dwb, dwc, da):
    ns, nb, gp = sd.ns, sd.nb, sd.g * sd.p

    def body(lam_re, lam_im, log_step, bt_re, bt_im, dwb_v, dwc_v, da_v, *outs):
        mask = _block_mask(sd)

        def fold(dense):
            kept = jnp.where(mask, dense, 0.0)
            acc = kept[0:sd.h]
            for gl in range(1, sd.gb):
                acc = acc + kept[gl * sd.h:(gl + 1) * sd.h]
            return acc

        lanes = lambda parts: jnp.concatenate(parts, axis=1) if len(parts) > 1 else parts[0]
        dbbar_re = lanes([fold(dwb_v[j][:, :ns]) for j in range(nb)])
        dbbar_im = lanes([fold(dwb_v[j][:, ns:]) for j in range(nb)])
        dwct = [dwc_v[j].T for j in range(nb)]
        d_ct_re = lanes([fold(t[:, :ns]) for t in dwct])
        d_ct_im = -lanes([fold(t[:, ns:]) for t in dwct])
        dabar_re = lanes([da_v[j][0:1, :ns] for j in range(nb)])
        dabar_im = lanes([da_v[j][0:1, ns:] for j in range(nb)])
        _, vjp = jax.vjp(_ssm_discretised, lam_re[...], lam_im[...], log_step[...], bt_re[...], bt_im[...])
        d_lr, d_li, d_ls, d_bt_re, d_bt_im = vjp((dabar_re, dabar_im, dbbar_re, dbbar_im))
        group = (lax.broadcasted_iota(jnp.int32, (gp, sd.g), 0) // sd.p == lax.broadcasted_iota(jnp.int32, (gp, sd.g), 1))
        d_log_step = jnp.dot(d_ls, group.astype(F32), precision=lax.Precision.HIGHEST, preferred_element_type=F32)
        for ref, val in zip(outs, (d_lr, d_li, d_log_step, d_bt_re, d_bt_im, d_ct_re, d_ct_im)):
            ref[...] = val

    row, mat = _sds((1, gp), F32), _sds((sd.h, gp), F32)
    vm = pl.BlockSpec(memory_space=pltpu.VMEM)
    return pl.pallas_call(body, name="ssm_param_grads", in_specs=[vm] * 8, out_specs=[vm] * 7,
                          out_shape=[row, row, _sds((1, sd.g), F32), mat, mat, mat, mat],
                          compiler_params=pltpu.CompilerParams(vmem_limit_bytes=VMEM_LIMIT))(*rows[:5], dwb, dwc, da)


def _block_scan(s_ref, cst_ref, carry_ref, sd, rows, reverse):
    ns = sd.ns
    nblk = rows // SUBLANES
    w = min(SCAN_LANES, ns)
    for c0 in range(0, ns, w):
        re_l, im_l = slice(c0, c0 + w), slice(ns + c0, ns + c0 + w)
        cst = [cst_ref[k * SUBLANES:(k + 1) * SUBLANES, c0:c0 + w] for k in range(8)]

        def step(k, carry, re_l=re_l, im_l=im_l, cst=cst):
            local = []
            for b in range(SCAN_BLOCKS):
                blk = SCAN_BLOCKS * k + b
                blk = (nblk - 1 - blk) if reverse else blk
                r0 = pl.multiple_of(blk * SUBLANES, SUBLANES)
                xr = s_ref[pl.ds(r0, SUBLANES), re_l]
                xi = s_ref[pl.ds(r0, SUBLANES), im_l]
                for n, d in enumerate((1, 2, 4)):
                    ar, ai = cst[2 * n], cst[2 * n + 1]
                    shift = (SUBLANES - d) if reverse else d
                    sr = pltpu.roll(xr, shift, 0)
                    si = pltpu.roll(xi, shift, 0)
                    xr, xi = xr + ar * sr - ai * si, xi + ar * si + ai * sr
                local.append((r0, xr, xi))
            cr, ci = carry
            edge = slice(0, 1) if reverse else slice(SUBLANES - 1, SUBLANES)
            for r0, xr, xi in local:
                br = jnp.broadcast_to(cr, xr.shape)
                bi = jnp.broadcast_to(ci, xi.shape)
                xr, xi = xr + cst[6] * br - cst[7] * bi, xi + cst[6] * bi + cst[7] * br
                s_ref[pl.ds(r0, SUBLANES), re_l] = xr
                s_ref[pl.ds(r0, SUBLANES), im_l] = xi
                cr, ci = xr[edge, :], xi[edge, :]
            return cr, ci

        cr, ci = lax.fori_loop(0, nblk // SCAN_BLOCKS, step, (carry_ref[0:1, re_l], carry_ref[0:1, im_l]))
        carry_ref[0:1, re_l] = cr
        carry_ref[0:1, im_l] = ci


def _ssm_fwd(name, sd, z, wb, wc, cst, d_row, tt=512, side=None):
    n_tok = z.shape[0]
    tt = _pick(n_tok, tt, 16)
    cb, ns2 = sd.cb, 2 * sd.ns

    def body(z_ref, wb_ref, wc_ref, cst_ref, d_ref, y_ref, s_ref, a0_ref, carry_ref):
        @pl.when(pl.program_id(1) == 0)
        def _():
            carry_ref[...] = jnp.zeros(carry_ref.shape, F32)
        u = z_ref[...]
        s_ref[...] = jnp.dot(u.astype(BF16), wb_ref[...], preferred_element_type=F32)
        _block_scan(s_ref, cst_ref, carry_ref, sd, tt, reverse=False)
        y = jnp.dot(s_ref[...].astype(BF16), wc_ref[...], preferred_element_type=F32) + d_ref[...] * u
        y_ref[...] = y
        a0_ref[...] = jax.nn.gelu(y).astype(BF16)

    return _call(
        body, side, [z, wb, wc, cst, d_row], name=name, grid=(sd.nb, n_tok // tt),
        in_specs=[pl.BlockSpec((tt, cb), lambda j, i: (i, j)),
                  pl.BlockSpec((None, cb, ns2), lambda j, i: (j, 0, 0)),
                  pl.BlockSpec((None, ns2, cb), lambda j, i: (j, 0, 0)),
                  pl.BlockSpec((None, 8 * SUBLANES, sd.ns), lambda j, i: (j, 0, 0)),
                  pl.BlockSpec((1, cb), lambda j, i: (0, j))],
        out_specs=[pl.BlockSpec((tt, cb), lambda j, i: (i, j)), pl.BlockSpec((tt, ns2), lambda j, i: (i, j)),
                   pl.BlockSpec((tt, cb), lambda j, i: (i, j))],
        out_shape=[_sds((n_tok, sd.d), F32), _sds((n_tok, sd.nb * ns2), F32), _sds((n_tok, sd.d), BF16)],
        scratch_shapes=[pltpu.VMEM((SUBLANES, ns2), F32)],
        compiler_params=_params(("arbitrary", "arbitrary")))


def _ssm_bwd(name, sd, y_pre, dy_direct, dya0, z, states, wct, wbt, cst_rev, d_row, tt=512, side=None):
    n_tok = z.shape[0]
    tt = _pick(n_tok, tt, 16)
    nt = n_tok // tt
    cb, ns, ns2 = sd.cb, sd.ns, 2 * sd.ns
    blocks_per_tile = tt // SUBLANES
    tn_dims = (((0,), (0,)), ((), ()))

    def body(y_ref, dyd_ref, dya0_ref, z_ref, s_ref, sp_ref, wct_ref, wbt_ref, cst_ref, d_ref,
             du_ref, dwb_ref, dwc_ref, da_ref, dd_ref, lam_ref, carry_ref):
        i = pl.program_id(1)

        @pl.when(i == 0)
        def _():
            carry_ref[...] = jnp.zeros(carry_ref.shape, F32)
            dwb_ref[...] = jnp.zeros(dwb_ref.shape, F32)
            dwc_ref[...] = jnp.zeros(dwc_ref.shape, F32)
            da_ref[...] = jnp.zeros(da_ref.shape, F32)
            dd_ref[...] = jnp.zeros(dd_ref.shape, F32)

        _, gelu_vjp = jax.vjp(jax.nn.gelu, y_ref[...])
        dy_t = dyd_ref[...] + gelu_vjp(dya0_ref[...].astype(F32))[0]
        u = z_ref[...]
        dy16 = dy_t.astype(BF16)
        lam_ref[...] = jnp.dot(dy16, wct_ref[...], preferred_element_type=F32)
        _block_scan(lam_ref, cst_ref, carry_ref, sd, tt, reverse=True)
        lam = lam_ref[...]
        lam16 = lam.astype(BF16)
        du_ref[...] = (jnp.dot(lam16, wbt_ref[...], preferred_element_type=F32) + d_ref[...] * dy_t).astype(BF16)
        dd_ref[0:1, :] += jnp.sum(dy_t * u, axis=0, keepdims=True)
        dwb_ref[...] += lax.dot_general(u.astype(BF16), lam16, tn_dims, preferred_element_type=F32)
        s = s_ref[...]
        dwc_ref[...] += lax.dot_general(s.astype(BF16), dy16, tn_dims, preferred_element_type=F32)
        before = jnp.where(i == nt - 1, 0.0, 1.0) * sp_ref[SUBLANES - 1:SUBLANES, :]
        first_row = lax.broadcasted_iota(jnp.int32, s.shape, 0) == 0
        prev = jnp.where(first_row, jnp.broadcast_to(before, s.shape), pltpu.roll(s, 1, 0))
        lr, li = lam[:, :ns], lam[:, ns:]
        pr, pi_ = prev[:, :ns], prev[:, ns:]
        da_ref[0:1, 0:ns] += jnp.sum(lr * pr + li * pi_, axis=0, keepdims=True)
        da_ref[0:1, ns:ns2] += jnp.sum(li * pr - lr * pi_, axis=0, keepdims=True)

    rev = lambda i: nt - 1 - i
    return _call(
        body, side, [y_pre, dy_direct, dya0, z, states, states, wct, wbt, cst_rev, d_row], name=name, grid=(sd.nb, nt),
        in_specs=[pl.BlockSpec((tt, cb), lambda j, i: (rev(i), j)),
                  pl.BlockSpec((tt, cb), lambda j, i: (rev(i), j)),
                  pl.BlockSpec((tt, cb), lambda j, i: (rev(i), j)),
                  pl.BlockSpec((tt, cb), lambda j, i: (rev(i), j)),
                  pl.BlockSpec((tt, ns2), lambda j, i: (rev(i), j)),
                  pl.BlockSpec((SUBLANES, ns2), lambda j, i: (jnp.maximum(rev(i) * blocks_per_tile - 1, 0), j)),
                  pl.BlockSpec((None, cb, ns2), lambda j, i: (j, 0, 0)),
                  pl.BlockSpec((None, ns2, cb), lambda j, i: (j, 0, 0)),
                  pl.BlockSpec((None, 8 * SUBLANES, ns), lambda j, i: (j, 0, 0)),
                  pl.BlockSpec((1, cb), lambda j, i: (0, j))],
        out_specs=[pl.BlockSpec((tt, cb), lambda j, i: (rev(i), j)),
                   pl.BlockSpec((None, cb, ns2), lambda j, i: (j, 0, 0)),
                   pl.BlockSpec((None, ns2, cb), lambda j, i: (j, 0, 0)),
                   pl.BlockSpec((None, SUBLANES, ns2), lambda j, i: (j, 0, 0)),
                   pl.BlockSpec((None, SUBLANES, cb), lambda j, i: (j, 0, 0))],
        out_shape=[_sds((n_tok, sd.d), BF16), _sds((sd.nb, cb, ns2), F32), _sds((sd.nb, ns2, cb), F32),
                   _sds((sd.nb, SUBLANES, ns2), F32), _sds((sd.nb, SUBLANES, cb), F32)],
        scratch_shapes=[pltpu.VMEM((tt, ns2), F32), pltpu.VMEM((SUBLANES, ns2), F32)],
        compiler_params=_params(("arbitrary", "arbitrary")))


def _hosted(exch, fn, name, *args, **kw):
    side = exch.side(name)
    if side is None:
        return fn(name, *args, **kw)
    out, moved = fn(name, *args, side=side, **kw)
    exch.done(name, moved)
    return out


def _local_grads(x, p, target, sp, exch):
    n_tok, d_model = x.shape
    d_ssm = sp["ssm_d"].shape[0] * sp["ssm_d"].shape[1]
    d_sgu = sp["sgu_ln_g"].shape[-1]
    sd = _SsmDims(sp["ssm_b_re"].shape[0], sp["ssm_b_re"].shape[1], sp["ssm_b_re"].shape[2])
    heads, chunk, _ = sp["sgu_w"].shape
    row = lambda v: v.reshape(1, -1)
    tok = lambda w, dt=F32: _sds((n_tok, w), dt)
    acc = lambda w: _sds((1, w), F32)

    g_mix = row(sp["norm_mix_g"])
    (h1,) = _hosted(exch, _rowwise, "norm_mix", lambda a, g: _rms(a, g), [x], [g_mix], [tok(d_model, BF16)])
    z = _hosted(exch, _mm_nn, "proj_in", h1, exch.weight("w_in"), sharded=True, tn=768)

    ssm_rows = _ssm_rows(sd, sp)
    wb, wbt, wc, wct, cst_fwd, cst_rev = _ssm_operands(sd, ssm_rows)
    d_row = row(sp["ssm_d"])
    y_pre, states, ya0_16 = _hosted(exch, _ssm_fwd, "ssm_fwd", sd, z, wb, wc, cst_fwd, d_row)
    q = _mm_nn("ssm_glu", ya0_16, exch.weight("ssm_glu_w"), tm=1024)
    glu_b, g_ossm = row(sp["ssm_glu_b"]), row(sp["out_norm_ssm_g"])
    (ya_n,) = _rowwise("ssm_glu_out", _glu_out, [y_pre, q], [glu_b, g_ossm], [tok(d_ssm, BF16)])

    assert d_ssm == d_sgu
    zu, zv = _Cols(z, d_sgu, 1), _Cols(z, d_sgu, 2)
    ln_g, ln_b, g_osgu = row(sp["sgu_ln_g"]), row(sp["sgu_ln_b"]), row(sp["out_norm_sgu_g"])
    b_st = sp["sgu_b"].T
    sgu_tr = 2 * chunk

    def sgu_joined(ya_t, zu_t, zv_t, *params):
        return jnp.concatenate([ya_t, _sgu_rows(zu_t, zv_t, *params).astype(BF16)], axis=1)

    (ycat,) = _rowwise("sgu", sgu_joined, [ya_n, zu, zv], [ln_g, ln_b, sp["sgu_w"], b_st, g_osgu],
                       [tok(d_ssm + d_sgu, BF16)], tr=sgu_tr)
    x1 = _mm_nn("proj_out", ycat, exch.weight("w_out"), res=x, tm=1024)

    g_ffn = row(sp["norm_ffn_g"])
    (h2,) = _rowwise("norm_ffn", lambda a, g: _rms(a, g), [x1], [g_ffn], [tok(d_model, BF16)])
    act, gu16 = _hosted(exch, _ffn_in_swiglu, "ffn_in", h2, exch.weight("w_ffn_in"))
    x2 = _mm_nn("ffn_out", act, exch.weight("w_ffn_out"), res=x1)

    g_ple = row(sp["norm_ple_g"])
    (h3,) = _rowwise("norm_ple", lambda a, g: _rms(a, g), [x2], [g_ple], [tok(d_model, BF16)])
    gpre = _mm_nn("ple_gate", h3, exch.weight("w_ple_gate"), tm=1024)
    (p16,) = _rowwise("ple_cast", lambda a: a, [p], [], [tok(p.shape[1], BF16)])
    pp = _mm_nn("ple_proj", p16, exch.weight("w_ple_proj"), sharded=True, tm=1024)

    b_g, g_fin = row(sp["b_ple_gate"]), row(sp["final_norm_g"])

    def head(x2_t, gpre_t, pp_t, tgt_t, b_g_v, g_fin_v):
        loss, grads = jax.value_and_grad(_head_loss, argnums=(0, 1, 2, 3, 4))(x2_t, gpre_t, pp_t, b_g_v, g_fin_v, tgt_t)
        dx2, dgpre, dpp, db, dg = grads
        return dx2, dgpre.astype(BF16), dpp.astype(BF16), jnp.full((1, LANES), loss, F32), db, dg

    dx2_head, dgpre16, dpp16, loss_row, d_b_g, d_g_fin = _rowwise(
        "head", head, [x2, gpre, pp, target], [b_g, g_fin],
        [tok(d_model), tok(d_model, BF16), tok(d_model, BF16)], [acc(LANES), acc(d_model), acc(d_model)])
    loss = loss_row[0, 0]

    exch.grad("w_ple_proj", _mm_tn("d_ple_proj", p16, dpp16, shards=N_CHIPS, tk=256))
    exch.grad("w_ple_gate", _mm_tn("d_ple_gate", h3, dgpre16))
    dh3 = _mm_nt("d_h3", dgpre16, exch.weight("w_ple_gate"), out_dtype=BF16, tm=1024)

    def norm_bwd(x_t, dres_t, dh_t, g_v):
        _, vjp = jax.vjp(_rms, x_t, g_v)
        dx, dg = vjp(dh_t.astype(F32))
        dx = dres_t + dx
        return dx, dx.astype(BF16), dg

    dx2, dx2_16, d_g_ple = _rowwise("d_norm_ple", norm_bwd, [x2, dx2_head, dh3], [g_ple],
                                    [tok(d_model), tok(d_model, BF16)], [acc(d_model)])
    exch.grad("w_ffn_out", _mm_tn("d_ffn_out", act, dx2_16))
    dgu16 = _hosted(exch, _d_act_swiglu, "d_act", dx2_16, exch.weight("w_ffn_out"), gu16)
    exch.grad("w_ffn_in", _hosted(exch, _mm_tn, "d_ffn_in", h2, dgu16, shards=N_CHIPS, g_halves=True, tn=1408, g_resident=True))
    dh2 = _hosted(exch, _mm_nt, "d_h2", dgu16, exch.weight("w_ffn_in"), sharded=True, g_halves=True, out_dtype=BF16, tm=256, w_resident=True)
    dx1, dx1_16, d_g_ffn = _rowwise("d_norm_ffn", norm_bwd, [x1, dx2, dh2], [g_ffn],
                                    [tok(d_model), tok(d_model, BF16)], [acc(d_model)])
    exch.grad("w_out", _mm_tn("d_proj_out", ycat, dx1_16))
    dycat = _mm_nt("d_ycat", dx1_16, exch.weight("w_out"), out_dtype=BF16, tm=1024)

    def glu_out_bwd(y_pre_t, q_t, dy_t, glu_b_v, g_v):
        _, vjp = jax.vjp(_glu_out, y_pre_t, q_t, glu_b_v, g_v)
        dy_pre, dq, db, dg = vjp(dy_t.astype(F32))
        return dy_pre, dq.astype(BF16), db, dg

    dy_pre_a, dq16, d_glu_b, d_g_ossm = _rowwise(
        "d_ssm_glu_out", glu_out_bwd, [y_pre, q, _Cols(dycat, d_ssm, 0)], [glu_b, g_ossm],
        [tok(d_ssm), tok(d_ssm, BF16)], [acc(d_ssm), acc(d_ssm)])
    exch.grad("ssm_glu_w", _mm_tn("d_ssm_glu", ya0_16, dq16))
    dya0 = _hosted(exch, _mm_nt, "d_ya0", dq16, exch.weight("ssm_glu_w"), out_dtype=BF16, tm=1024)

    dz_ssm16, dwb, dwc, da, dd = _hosted(exch, _ssm_bwd, "ssm_bwd", sd, y_pre, dy_pre_a, dya0, z, states, wct, wbt,
                                         cst_rev, d_row)

    def sgu_bwd(dz_ssm_t, zu_t, zv_t, dy_t, ln_g_v, ln_b_v, w_v, b_v, g_v):
        _, vjp = jax.vjp(_sgu_rows, zu_t, zv_t, ln_g_v, ln_b_v, w_v, b_v, g_v)
        dzu, dzv, dlg, dlb, dw, db, dg = vjp(dy_t.astype(F32))
        return jnp.concatenate([dz_ssm_t, dzu.astype(BF16), dzv.astype(BF16)], axis=1), dlg, dlb, dw, db, dg

    dz16, d_ln_g, d_ln_b, d_sgu_w, d_b_st, d_g_osgu = _hosted(
        exch, _rowwise, "d_sgu", sgu_bwd, [dz_ssm16, zu, zv, _Cols(dycat, d_sgu, 1)], [ln_g, ln_b, sp["sgu_w"], b_st, g_osgu],
        [tok(d_ssm + 2 * d_sgu, BF16)],
        [acc(d_sgu), acc(d_sgu), _sds(sp["sgu_w"].shape, F32), _sds(b_st.shape, F32), acc(d_sgu)], tr=sgu_tr)

    d_lam_re, d_lam_im, d_log_step, d_bt_re, d_bt_im, d_ct_re, d_ct_im = _ssm_param_grads(sd, ssm_rows, dwb, dwc, da)
    d_b_re, d_b_im = d_bt_re.T, d_bt_im.T
    d_c_re, d_c_im = (t.reshape(sd.h, sd.g, sd.p).transpose(1, 0, 2) for t in (d_ct_re, d_ct_im))
    d_ssm_d = dd[:, 0, :].reshape(sd.g, sd.h)

    exch.small_grads({
        "ssm_lambda_re": d_lam_re, "ssm_lambda_im": d_lam_im, "ssm_log_step": d_log_step,
        "ssm_b_re": d_b_re, "ssm_b_im": d_b_im, "ssm_c_re": d_c_re, "ssm_c_im": d_c_im, "ssm_d": d_ssm_d,
        "ssm_glu_b": d_glu_b, "sgu_ln_g": d_ln_g, "sgu_ln_b": d_ln_b, "sgu_w": d_sgu_w, "sgu_b": d_b_st.T,
        "out_norm_ssm_g": d_g_ossm, "out_norm_sgu_g": d_g_osgu, "norm_ffn_g": d_g_ffn, "norm_ple_g": d_g_ple,
        "b_ple_gate": d_b_g, "final_norm_g": d_g_fin,
    })

    exch.grad("w_in", _hosted(exch, _mm_tn, "d_proj_in", h1, dz16, shards=N_CHIPS, tn=768))
    dh1 = _hosted(exch, _mm_nt, "d_h1", dz16, exch.weight("w_in"), sharded=True, out_dtype=BF16, tm=1024)

    def norm_in_bwd(x_t, dres_t, dh_t, g_v):
        _, vjp = jax.vjp(_rms, x_t, g_v)
        dx, dg = vjp(dh_t.astype(F32))
        return dres_t + dx, dg

    grad_x, d_g_mix = _hosted(exch, _rowwise, "d_norm_mix", norm_in_bwd, [x, dx1, dh1], [g_mix], [tok(d_model)], [acc(d_model)])
    exch.small_grads({"norm_mix_g": d_g_mix})
    return loss, grad_x


def _place():
    x, y, c = lax.axis_index("x"), lax.axis_index("y"), lax.axis_index("c")
    chips = [(1 - x, y), (x, 1 - y), (1 - x, 1 - y)]
    return x, y, c, chips


def _cast_into_slot(name, w2d, shard, tr=256):
    rows, cols = w2d.shape
    rh = rows // 2
    tr = _pick(rh, tr, 16)
    per = rh // tr

    def body(s_ref, a_ref, o_ref):
        o_ref[...] = a_ref[...].astype(BF16)

    grid_spec = pltpu.PrefetchScalarGridSpec(
        num_scalar_prefetch=1, grid=(2, per),
        in_specs=[pl.BlockSpec((tr, cols), lambda h, i, s_ref: (h * per + i, 0))],
        out_specs=pl.BlockSpec((None, None, tr, cols), lambda h, i, s_ref: (s_ref[0], h, i, 0)))
    return pl.pallas_call(body, name=name, grid_spec=grid_spec, out_shape=_sds((N_CHIPS, 2, rh, cols), BF16),
                          compiler_params=_params(("arbitrary", "arbitrary")))(shard.reshape(1).astype(jnp.int32), w2d)


def _exchange_alone(name, side):
    n_in, n_out = len(side.ins), len(side.out_shapes)

    def body(*refs):
        ins, outs, sems = refs[:n_in], refs[n_in:n_in + n_out], refs[n_in + n_out:]
        side.first(ins, outs, *sems)
        if side.mid is not None:
            side.mid(ins, outs, *sems)
        side.last(ins, outs, *sems)

    return pl.pallas_call(
        body, name=name, in_specs=[ANY] * n_in, out_specs=[ANY] * n_out, out_shape=side.out_shapes,
        input_output_aliases=side.aliases,
        scratch_shapes=[pltpu.SemaphoreType.DMA((side.n_sems,)), pltpu.SemaphoreType.DMA((side.n_sems,))],
    )(*side.ins)


def _gather_side(slots, mid_late=False):
    n = len(slots)

    def copies(kind, outs, send_sems, recv_sems):
        x, y, c, chips = _place()

        def remote(k, ref, to):
            return pltpu.make_async_remote_copy(src_ref=ref, dst_ref=ref, send_sem=send_sems.at[k], recv_sem=recv_sems.at[k],
                                                device_id=to, device_id_type=MESH)

        pairs = [(w, j, 2 * cx + cy, (cx, cy)) for w in range(n) for j, (cx, cy) in enumerate(chips)]
        if kind == "sends":
            return [remote(3 * w + j, outs[w].at[2 * x + y, c], (*chip, c)) for w, j, _, chip in pairs]
        if kind == "arrivals":
            return [remote(3 * w + j, outs[w].at[s, c], (x, y, c)) for w, j, s, _ in pairs]
        if kind == "passed":
            return [remote(3 * n + 3 * w + j, outs[w].at[s, c], (x, y, 1 - c)) for w, j, s, _ in pairs]
        return [remote(3 * n + 3 * w + j, outs[w].at[s, 1 - c], (x, y, c)) for w, j, s, _ in pairs]

    def first(ins, outs, *sems):
        for cp in copies("sends", outs, *sems):
            cp.start()

    def mid(ins, outs, *sems):
        for arrived, onward in zip(copies("arrivals", outs, *sems), copies("passed", outs, *sems)):
            arrived.wait_recv()
            onward.start()

    def last(ins, outs, *sems):
        for cp in copies("from_sibling", outs, *sems):
            cp.wait_recv()
        for cp in copies("sends", outs, *sems) + copies("passed", outs, *sems):
            cp.wait_send()

    return _Side(slots, [_sds(s.shape, s.dtype) for s in slots], 6 * n, first, last, mid=mid, aliases={w: w for w in range(n)},
                 mid_late=mid_late)


def _swap_side(grads):
    n = len(grads)

    def copies(ins, outs, send_sems, recv_sems):
        x, y, c, _ = _place()
        return [pltpu.make_async_remote_copy(src_ref=ins[w].at[:, 1 - c], dst_ref=outs[w], send_sem=send_sems.at[w],
                                             recv_sem=recv_sems.at[w], device_id=(x, y, 1 - c), device_id_type=MESH)
                for w in range(n)]

    def first(*refs):
        for cp in copies(*refs):
            cp.start()

    def last(*refs):
        for cp in copies(*refs):
            cp.wait()

    return _Side(grads, [_sds((g.shape[0], *g.shape[2:]), g.dtype) for g in grads], n, first, last)


def _scatter_side(halves):
    n = len(halves)

    def copies(ins, outs, send_sems, recv_sems):
        x, y, c, chips = _place()
        return [pltpu.make_async_remote_copy(
            src_ref=ins[w].at[2 * cx + cy], dst_ref=outs[w].at[j], send_sem=send_sems.at[3 * w + j],
            recv_sem=recv_sems.at[3 * w + j], device_id=(cx, cy, c), device_id_type=MESH)
            for w in range(n) for j, (cx, cy) in enumerate(chips)]

    def first(*refs):
        for cp in copies(*refs):
            cp.start()

    def last(*refs):
        for cp in copies(*refs):
            cp.wait()

    return _Side(halves, [_sds((3, *h.shape[1:]), h.dtype) for h in halves], 3 * n, first, last)


def _join_halves(name, slots):
    n = len(slots)

    def body(*refs):
        outs = refs[n:2 * n]
        send_sems, recv_sems = refs[2 * n:]
        x, y, c, _ = _place()

        def copy(w, half, to):
            return pltpu.make_async_remote_copy(src_ref=outs[w].at[half], dst_ref=outs[w].at[half], send_sem=send_sems.at[w],
                                                recv_sem=recv_sems.at[w], device_id=to, device_id_type=MESH)

        copies = [copy(w, c, (x, y, 1 - c)) for w in range(n)]
        for cp in copies:
            cp.start()
        for w in range(n):
            copy(w, 1 - c, (x, y, c)).wait_recv()
        for cp in copies:
            cp.wait_send()

    return pl.pallas_call(
        body, name=name, in_specs=[ANY] * n, out_specs=[ANY] * n,
        out_shape=[_sds(s.shape, s.dtype) for s in slots], input_output_aliases={w: w for w in range(n)},
        scratch_shapes=[pltpu.SemaphoreType.DMA((n,)), pltpu.SemaphoreType.DMA((n,))],
    )(*slots)


def _allreduce_small(block, tr=256):
    rows, lanes = block.shape
    tr = _pick(rows, tr, SUBLANES)

    def body(x_ref, o_ref, buf, send_sems, recv_sems):
        x, y, c, chips = _place()
        me, sibling = (x, y, c), (x, y, 1 - c)

        def slot(px, py, pc):
            return buf.at[4 * px + 2 * py + pc]

        def copy(k, block_of, to):
            return pltpu.make_async_remote_copy(src_ref=slot(*block_of), dst_ref=slot(*block_of), send_sem=send_sems.at[k],
                                                recv_sem=recv_sems.at[k], device_id=to, device_id_type=MESH)

        slot(*me)[...] = x_ref[...]
        first = [copy(0, me, sibling)] + [copy(1 + j, me, (*chip, c)) for j, chip in enumerate(chips)]
        for cp in first:
            cp.start()
        passed = [copy(4 + j, (*chip, c), sibling) for j, chip in enumerate(chips)]
        for j, chip in enumerate(chips):
            copy(1 + j, (*chip, c), me).wait_recv()
            passed[j].start()
        copy(0, sibling, me).wait_recv()
        for j, chip in enumerate(chips):
            copy(4 + j, (*chip, 1 - c), me).wait_recv()
        for cp in first + passed:
            cp.wait_send()
        for r0 in range(0, rows, tr):
            acc = buf[0, r0:r0 + tr, :]
            for k in range(1, N_DEV):
                acc = acc + buf[k, r0:r0 + tr, :]
            o_ref[r0:r0 + tr, :] = acc

    vm = pl.BlockSpec(memory_space=pltpu.VMEM)
    return pl.pallas_call(
        body, name="allreduce_small", in_specs=[vm], out_specs=vm, out_shape=_sds((rows, lanes), block.dtype),
        scratch_shapes=[pltpu.VMEM((N_DEV, rows, lanes), block.dtype), pltpu.SemaphoreType.DMA((7,)), pltpu.SemaphoreType.DMA((7,))],
        compiler_params=pltpu.CompilerParams(vmem_limit_bytes=VMEM_LIMIT),
    )(block)


def _small_gather_side(block):
    def copy(kind, j, ins, outs, send_sems, recv_sems):
        x, y, c, chips = _place()
        chip = chips[j] if j is not None else None
        slot = lambda px, py, pc: outs[0].at[4 * px + 2 * py + pc]

        def remote(k, src, dst, to):
            return pltpu.make_async_remote_copy(src_ref=src, dst_ref=dst, send_sem=send_sems.at[k], recv_sem=recv_sems.at[k],
                                                device_id=to, device_id_type=MESH)

        if kind == "to_sibling":
            return remote(0, ins[0], slot(x, y, c), (x, y, 1 - c))
        if kind == "from_sibling":
            return remote(0, ins[0], slot(x, y, 1 - c), (x, y, c))
        if kind == "to_chip":
            return remote(1 + j, ins[0], slot(x, y, c), (*chip, c))
        if kind == "from_chip":
            return remote(1 + j, ins[0], slot(*chip, c), (x, y, c))
        if kind == "pass_on":
            return remote(4 + j, slot(*chip, c), slot(*chip, c), (x, y, 1 - c))
        return remote(4 + j, slot(*chip, 1 - c), slot(*chip, 1 - c), (x, y, c))

    def first(*refs):
        copy("to_sibling", None, *refs).start()
        for j in range(3):
            copy("to_chip", j, *refs).start()

    def mid(*refs):
        for j in range(3):
            copy("from_chip", j, *refs).wait_recv()
            copy("pass_on", j, *refs).start()

    def last(*refs):
        copy("from_sibling", None, *refs).wait_recv()
        for j in range(3):
            copy("passed_on", j, *refs).wait_recv()
        copy("to_sibling", None, *refs).wait_send()
        for j in range(3):
            copy("to_chip", j, *refs).wait_send()
            copy("pass_on", j, *refs).wait_send()

    return _Side([block], [_sds((N_DEV, *block.shape), block.dtype)], 7, first, last, mid=mid, mid_late=True)


def _sum_slots(name, own, gathered, me, tr=512):
    n, rows, cols = gathered.shape
    tr = _pick(rows, tr, SUBLANES)

    def body(me_ref, own_ref, g_ref, o_ref):
        mine = own_ref[...]
        acc = jnp.where(me_ref[0] == 0, mine, g_ref[0])
        for k in range(1, n):
            acc = acc + jnp.where(me_ref[0] == k, mine, g_ref[k])
        o_ref[...] = acc

    grid_spec = pltpu.PrefetchScalarGridSpec(
        num_scalar_prefetch=1, grid=(rows // tr,),
        in_specs=[pl.BlockSpec((tr, cols), lambda i, me_ref: (i, 0)), pl.BlockSpec((n, tr, cols), lambda i, me_ref: (0, i, 0))],
        out_specs=pl.BlockSpec((tr, cols), lambda i, me_ref: (i, 0)))
    return pl.pallas_call(body, name=name, grid_spec=grid_spec, out_shape=_sds((rows, cols), own.dtype),
                          compiler_params=_params(("arbitrary",)))(me.reshape(1).astype(jnp.int32), own, gathered)


def _sum_received(name, full, c, shard, swapped, received, tr=256):
    n, rows, cols = received.shape
    tr = _pick(rows, tr, 16)

    def body(i_ref, a_ref, b_ref, s_ref, o_ref):
        acc = a_ref[...] + b_ref[...]
        for k in range(n):
            acc = acc + s_ref[k].astype(F32)
        o_ref[...] = acc

    grid_spec = pltpu.PrefetchScalarGridSpec(
        num_scalar_prefetch=1, grid=(rows // tr,),
        in_specs=[pl.BlockSpec((None, None, tr, cols), lambda i, i_ref: (i_ref[1], i_ref[0], i, 0)),
                  pl.BlockSpec((None, tr, cols), lambda i, i_ref: (i_ref[1], i, 0)),
                  pl.BlockSpec((n, tr, cols), lambda i, i_ref: (0, i, 0))],
        out_specs=pl.BlockSpec((None, tr, cols), lambda i, i_ref: (i_ref[0], i, 0)))
    return pl.pallas_call(body, name=name, grid_spec=grid_spec, out_shape=_sds((2, rows, cols), F32),
                          compiler_params=_params(("arbitrary",)))(jnp.stack([c, shard]).astype(jnp.int32), full, swapped, received)


def _add_halves(name, full, c, received, tr=256):
    s, _, rh, cols = full.shape
    tr = _pick(rh, tr, 16)

    def body(c_ref, a_ref, b_ref, o_ref):
        o_ref[...] = (a_ref[...] + b_ref[...]).astype(BF16)

    grid_spec = pltpu.PrefetchScalarGridSpec(
        num_scalar_prefetch=1, grid=(s, rh // tr),
        in_specs=[pl.BlockSpec((None, None, tr, cols), lambda q, i, c_ref: (q, c_ref[0], i, 0)),
                  pl.BlockSpec((None, tr, cols), lambda q, i, c_ref: (q, i, 0))],
        out_specs=pl.BlockSpec((None, tr, cols), lambda q, i, c_ref: (q, i, 0)))
    return pl.pallas_call(body, name=name, grid_spec=grid_spec, out_shape=_sds((s, rh, cols), BF16),
                          compiler_params=_params(("arbitrary", "arbitrary")))(c.reshape(1).astype(jnp.int32), full, received)


LARGE = ("w_in", "ssm_glu_w", "w_out", "w_ffn_in", "w_ffn_out", "w_ple_gate", "w_ple_proj")
COLUMN_SHARDED = ("w_in", "w_ffn_in", "w_ple_proj")
SMALL = ("norm_mix_g", "ssm_lambda_re", "ssm_lambda_im", "ssm_log_step", "ssm_b_re", "ssm_b_im", "ssm_c_re", "ssm_c_im",
         "ssm_d", "ssm_glu_b", "sgu_ln_g", "sgu_ln_b", "sgu_w", "sgu_b", "out_norm_ssm_g", "out_norm_sgu_g", "norm_ffn_g",
         "norm_ple_g", "b_ple_gate", "final_norm_g")
WEIGHTS = ("norm_mix_g", "w_in", "ssm_lambda_re", "ssm_lambda_im", "ssm_log_step", "ssm_b_re", "ssm_b_im", "ssm_c_re",
           "ssm_c_im", "ssm_d", "ssm_glu_w", "ssm_glu_b", "sgu_ln_g", "sgu_ln_b", "sgu_w", "sgu_b", "out_norm_ssm_g",
           "out_norm_sgu_g", "w_out", "norm_ffn_g", "w_ffn_in", "w_ffn_out", "norm_ple_g", "w_ple_gate", "b_ple_gate",
           "w_ple_proj", "final_norm_g")
PACK_ROWS = SUBLANES * LANES


def _pack(arrays):
    parts = []
    for a in arrays:
        flat = a.reshape(-1).astype(F32)
        pad = -flat.shape[0] % PACK_ROWS
        parts.append(jnp.pad(flat, (0, pad)) if pad else flat)
    return jnp.concatenate(parts).reshape(-1, LANES)


def _unpack(packed, like):
    flat = packed.reshape(-1)
    out, at = [], 0
    for a in like:
        size = a.size
        out.append(flat[at:at + size].reshape(a.shape))
        at += size + (-size % PACK_ROWS)
    return out


class _NoExchange:
    def __init__(self, weights):
        self.weights, self.grads, self.small = weights, {}, {}

    def weight(self, name):
        return self.weights[name]

    def grad(self, name, g):
        self.grads[name] = g

    def small_grads(self, grads):
        self.small.update(grads)

    def side(self, host):
        return None


class _MeshExchange:
    GATHER = {"norm_mix": ("w_in",), "proj_in": ("ssm_glu_w", "w_out"), "ssm_fwd": ("w_ffn_in",),
              "ffn_in": ("w_ffn_out", "w_ple_gate", "w_ple_proj")}
    GATHER_LONG = ("norm_mix", "ssm_fwd")
    SWAP = {"d_act": ("w_ple_proj", "w_ple_gate", "w_ffn_out"), "d_h2": ("w_ffn_in",), "d_ya0": ("w_out", "ssm_glu_w")}
    SWAP_ALONE = ("w_in",)
    SCATTER = {"d_ffn_in": ("w_ple_proj", "w_ple_gate", "w_ffn_out"), "ssm_bwd": ("w_ffn_in",),
               "d_sgu": ("w_out", "ssm_glu_w"), "d_h1": ("w_in",)}
    SMALL_GATHER = "d_proj_in"
    GROUPS = (("w_ple_proj", "w_ple_gate", "w_ffn_out"), ("w_ffn_in",), ("w_out", "ssm_glu_w"), ("w_in",))

    def __init__(self, shards, small_like, c, shard, me):
        self.c, self.shard, self.me, self.small_like = c, shard, me, small_like
        self.slots = {k: _cast_into_slot("cast_" + k, shards[k], shard) for k in LARGE}
        self.full, self.received, self.halves, self.quarters, self.small = {}, {}, {}, {}, {}

    def weight(self, name):
        g = self.slots[name]
        _, _, rh, cols = g.shape
        return g.reshape(N_CHIPS, 2 * rh, cols) if name in COLUMN_SHARDED else g.reshape(N_CHIPS * 2 * rh, cols)

    def grad(self, name, g):
        if name not in COLUMN_SHARDED:
            g = g.reshape(N_CHIPS, g.shape[0] // N_CHIPS, g.shape[1])
        self.full[name] = g.reshape(N_CHIPS, 2, g.shape[1] // 2, g.shape[2])
        if name in self.SWAP_ALONE:
            self._swapped((name,), _exchange_alone("grad_swap_" + name, _swap_side([self.full[name]])))

    def _swapped(self, names, received):
        for k, r in zip(names, received):
            self.received[k] = r
            self.halves[k] = _add_halves("grad_add_halves_" + k, self.full[k], self.c, r)

    def small_grads(self, grads):
        self.small.update(grads)

    def _packed(self, names):
        return _pack([self.small[k].reshape(self.small_like[k].shape) for k in names])

    def side(self, host):
        if host in self.GATHER:
            return _gather_side([self.slots[k] for k in self.GATHER[host]], mid_late=host in self.GATHER_LONG)
        if host in self.SWAP:
            return _swap_side([self.full[k] for k in self.SWAP[host]])
        if host in self.SCATTER:
            return _scatter_side([self.halves[k] for k in self.SCATTER[host]])
        if host == self.SMALL_GATHER:
            self.packed_early = self._packed(SMALL[1:])
            return _small_gather_side(self.packed_early)
        return None

    def done(self, host, moved):
        if host in self.GATHER:
            self.slots.update(zip(self.GATHER[host], moved))
        elif host in self.SWAP:
            self._swapped(self.SWAP[host], moved)
        elif host in self.SCATTER:
            self.quarters.update(zip(self.SCATTER[host], moved))
        else:
            (self.gathered_early,) = moved

    def small_reduced(self):
        early = _sum_slots("small_sum", self.packed_early, self.gathered_early, self.me)
        late = _allreduce_small(self._packed(SMALL[:1]))
        return jnp.concatenate([late, early], axis=0)

    def reduced(self, group):
        parts = [_sum_received("grad_sum_" + k, self.full[k], self.c, self.shard, self.received[k], self.quarters[k]) for k in group]
        joined = _join_halves("grad_join_" + group[0], parts)
        return {k: j.reshape(2 * j.shape[1], j.shape[2]) for k, j in zip(group, joined)}


def kernel(x, p, norm_mix_g, w_in, ssm_lambda_re, ssm_lambda_im, ssm_log_step, ssm_b_re, ssm_b_im, ssm_c_re, ssm_c_im, ssm_d, ssm_glu_w, ssm_glu_b, sgu_ln_g, sgu_ln_b, sgu_w, sgu_b, out_norm_ssm_g, out_norm_sgu_g, w_out, norm_ffn_g, w_ffn_in, w_ffn_out, norm_ple_g, w_ple_gate, b_ple_gate, w_ple_proj, final_norm_g, loss_target, m_norm_mix_g, m_w_in, m_ssm_lambda_re, m_ssm_lambda_im, m_ssm_log_step, m_ssm_b_re, m_ssm_b_im, m_ssm_c_re, m_ssm_c_im, m_ssm_d, m_ssm_glu_w, m_ssm_glu_b, m_sgu_ln_g, m_sgu_ln_b, m_sgu_w, m_sgu_b, m_out_norm_ssm_g, m_out_norm_sgu_g, m_w_out, m_norm_ffn_g, m_w_ffn_in, m_w_ffn_out, m_norm_ple_g, m_w_ple_gate, m_b_ple_gate, m_w_ple_proj, m_final_norm_g, v_norm_mix_g, v_w_in, v_ssm_lambda_re, v_ssm_lambda_im, v_ssm_log_step, v_ssm_b_re, v_ssm_b_im, v_ssm_c_re, v_ssm_c_im, v_ssm_d, v_ssm_glu_w, v_ssm_glu_b, v_sgu_ln_g, v_sgu_ln_b, v_sgu_w, v_sgu_b, v_out_norm_ssm_g, v_out_norm_sgu_g, v_w_out, v_norm_ffn_g, v_w_ffn_in, v_w_ffn_out, v_norm_ple_g, v_w_ple_gate, v_b_ple_gate, v_w_ple_proj, v_final_norm_g):
    given = dict(locals())
    w = {k: given[k] for k in WEIGHTS}
    m = {k: given["m_" + k] for k in WEIGHTS}
    v = {k: given["v_" + k] for k in WEIGHTS}
    c = lax.axis_index("c")
    shard = 2 * lax.axis_index("x") + lax.axis_index("y")

    exch = _MeshExchange({k: w[k].reshape(w[k].shape[1:]) for k in LARGE}, {k: w[k] for k in SMALL}, c, shard, 2 * shard + c)
    unlayer = lambda a: a if a.ndim == 1 else a[0]
    sp = {k: unlayer(w[k]) for k in SMALL}
    n_tok, d_model = x.shape[1:]
    loss, grad_x = _local_grads(x.reshape(n_tok, d_model), p.reshape(n_tok, p.shape[-1]),
                                loss_target.reshape(n_tok, d_model), sp, exch)
    loss = lax.psum(loss, ("x", "y", "c"))

    grad_w, delta_w, new_m, new_v = {}, {}, {}, {}
    for group in exch.GROUPS:
        reduced = exch.reduced(group)
        for k in group:
            shape = w[k].shape
            two_d = lambda a: a.reshape(shape[1:])
            like = _sds(shape[1:], F32)
            d_k, m_k, v_k = _hosted(exch, _rowwise, "adamw_" + k, _adamw, [two_d(w[k]), reduced[k], two_d(m[k]), two_d(v[k])],
                                    [], [like, like, like])
            grad_w[k], delta_w[k], new_m[k], new_v[k] = (a.reshape(shape) for a in (reduced[k], d_k, m_k, v_k))

    packed_g = exch.small_reduced()
    like = _sds(packed_g.shape, F32)
    d_s, m_s, v_s = _rowwise("adamw_small", _adamw, [_pack([w[k] for k in SMALL]), packed_g, _pack([m[k] for k in SMALL]),
                                                     _pack([v[k] for k in SMALL])], [], [like, like, like])
    shapes = [w[k] for k in SMALL]
    for k, g_k, d_k, m_k, v_k in zip(SMALL, _unpack(packed_g, shapes), _unpack(d_s, shapes), _unpack(m_s, shapes), _unpack(v_s, shapes)):
        grad_w[k], delta_w[k], new_m[k], new_v[k] = g_k, d_k, m_k, v_k

    return (loss, grad_x.reshape(x.shape), *[grad_w[k] for k in WEIGHTS], *[delta_w[k] for k in WEIGHTS],
            *[new_m[k] for k in WEIGHTS], *[new_v[k] for k in WEIGHTS])
```

```python
import functools

import jax
import jax.numpy as jnp
from jax import lax
from jax.experimental import pallas as pl
from jax.experimental.pallas import tpu as pltpu

F32 = jnp.float32
BF16 = jnp.bfloat16

EPS = 1e-6
LAMBDA_RE_MAX = -1e-4
ADAM_LR = 0.001
ADAM_B1 = 0.9
ADAM_B2 = 0.999
ADAM_EPS = 1e-08
ADAM_WD = 0.01
ADAM_STEP = 10

N_CHIPS = 4
N_DEV = 8
SUBLANES = 8
LANES = 128
SSM_CH_BLOCK = 256
SCAN_LANES = 256
SCAN_BLOCKS = 4
VMEM_LIMIT = 56 * 1024 * 1024

MESH = pl.DeviceIdType.MESH


def _pick(n, pref, mult):
    if n <= pref:
        return n
    t = (pref // mult) * mult
    while t >= mult:
        if n % t == 0:
            return t
        t -= mult
    return n


def _params(semantics):
    return pltpu.CompilerParams(dimension_semantics=semantics, vmem_limit_bytes=VMEM_LIMIT)


class _Cols:
    def __init__(self, arr, width, blk):
        self.arr, self.width, self.blk = arr, width, blk


def _sds(shape, dtype):
    return jax.ShapeDtypeStruct(tuple(shape), dtype)


ANY = pl.BlockSpec(memory_space=pl.ANY)


class _Side:
    def __init__(self, ins, out_shapes, n_sems, first, last, mid=None, aliases=None, mid_late=False):
        self.ins, self.out_shapes, self.n_sems = list(ins), list(out_shapes), n_sems
        self.first, self.mid, self.last, self.mid_late = first, mid, last, mid_late
        self.aliases = dict(aliases or {})


def _call(body, side, operands, *, name, grid, in_specs, out_specs, out_shape, compiler_params, scratch_shapes=()):
    if side is None:
        return pl.pallas_call(body, name=name, grid=grid, in_specs=in_specs, out_specs=out_specs, out_shape=out_shape,
                              scratch_shapes=list(scratch_shapes), compiler_params=compiler_params)(*operands)
    single = not isinstance(out_specs, (list, tuple))
    out_specs = [out_specs] if single else list(out_specs)
    out_shape = [out_shape] if single else list(out_shape)
    n_in, n_out, n_scr = len(in_specs), len(out_specs), len(scratch_shapes)
    n_sin, n_sout = len(side.ins), len(side.out_shapes)
    steps = 1
    for g in grid:
        steps *= g

    def hosted(*refs):
        ins, s_ins = refs[:n_in], refs[n_in:n_in + n_sin]
        at = n_in + n_sin
        outs, s_outs = refs[at:at + n_out], refs[at + n_out:at + n_out + n_sout]
        scratch = refs[at + n_out + n_sout:at + n_out + n_sout + n_scr]
        sems = refs[-2:]
        step = pl.program_id(0)
        for d in range(1, len(grid)):
            step = step * grid[d] + pl.program_id(d)

        @pl.when(step == 0)
        def _():
            side.first(s_ins, s_outs, *sems)

        if side.mid is not None:
            @pl.when(step == (steps - 1 if side.mid_late else (3 * steps) // 4))
            def _():
                side.mid(s_ins, s_outs, *sems)

        body(*ins, *outs, *scratch)

        @pl.when(step == steps - 1)
        def _():
            side.last(s_ins, s_outs, *sems)

    res = pl.pallas_call(
        hosted, name=name, grid=grid, in_specs=[*in_specs, *[ANY] * n_sin], out_specs=[*out_specs, *[ANY] * n_sout],
        out_shape=[*out_shape, *side.out_shapes], input_output_aliases={n_in + i: n_out + o for i, o in side.aliases.items()},
        scratch_shapes=[*scratch_shapes, pltpu.SemaphoreType.DMA((side.n_sems,)), pltpu.SemaphoreType.DMA((side.n_sems,))],
        compiler_params=compiler_params)(*operands, *side.ins)
    return (res[0] if single else list(res[:n_out])), list(res[n_out:])


def _rowwise(name, fn, rows, params, row_outs, acc_outs=(), tr=256, side=None):
    rows = [r if isinstance(r, _Cols) else _Cols(r, r.shape[1], 0) for r in rows]
    m = rows[0].arr.shape[0]
    tr = _pick(m, tr, 16)
    n_in = len(rows) + len(params)
    n_ro = len(row_outs)

    def body(*refs):
        vals = fn(*[r[...] for r in refs[:n_in]])
        if not isinstance(vals, (tuple, list)):
            vals = (vals,)
        outs = refs[n_in:]
        for r, v in zip(outs[:n_ro], vals[:n_ro]):
            r[...] = v.astype(r.dtype)
        first = pl.program_id(0) == 0
        for r, v in zip(outs[n_ro:], vals[n_ro:]):
            @pl.when(first)
            def _():
                r[...] = jnp.zeros(r.shape, r.dtype)
            r[...] += v.astype(r.dtype).reshape(r.shape)

    in_specs = [pl.BlockSpec((tr, r.width), lambda i, b=r.blk: (i, b)) for r in rows]
    in_specs += [pl.BlockSpec(p.shape, lambda i, nd=p.ndim: (0,) * nd) for p in params]
    out_specs = [pl.BlockSpec((tr, o.shape[1]), lambda i: (i, 0)) for o in row_outs]
    out_specs += [pl.BlockSpec(o.shape, lambda i, nd=len(o.shape): (0,) * nd) for o in acc_outs]
    return _call(body, side, [*[r.arr for r in rows], *params], name=name, grid=(m // tr,), in_specs=in_specs,
                 out_specs=out_specs, out_shape=[*row_outs, *acc_outs], compiler_params=_params(("arbitrary",)))


def _grid_order(swap):
    if not swap:
        return (lambda grid: grid), (lambda f: f)
    return (lambda grid: grid[::-1]), (lambda f: (lambda j, i: f(i, j)))


def _mm_nn(name, a, w, *, sharded=False, res=None, out_dtype=F32, tm=512, tn=512, w_resident=False, side=None):
    m, k = a.shape
    tm = _pick(m, tm, 16)
    order, ix = _grid_order(w_resident)
    if sharded:
        s, _, ns = w.shape
        n = s * ns
        tn = _pick(ns, tn, LANES)
        per = ns // tn
        w_spec = pl.BlockSpec((None, k, tn), ix(lambda i, j: (j // per, 0, j % per)))
    else:
        n = w.shape[1]
        tn = _pick(n, tn, LANES)
        w_spec = pl.BlockSpec((k, tn), ix(lambda i, j: (0, j)))

    def body(a_ref, w_ref, *rest):
        acc = jnp.dot(a_ref[...], w_ref[...], preferred_element_type=F32)
        if res is not None:
            acc = acc + rest[0][...]
        rest[-1][...] = acc.astype(out_dtype)

    in_specs = [pl.BlockSpec((tm, k), ix(lambda i, j: (i, 0))), w_spec]
    ops = [a, w]
    if res is not None:
        in_specs.append(pl.BlockSpec((tm, tn), ix(lambda i, j: (i, j))))
        ops.append(res)
    return _call(body, side, ops, name=name, grid=order((m // tm, n // tn)), in_specs=in_specs,
                 out_specs=pl.BlockSpec((tm, tn), ix(lambda i, j: (i, j))), out_shape=_sds((m, n), out_dtype),
                 compiler_params=_params(("arbitrary", "arbitrary")))


def _mm_nt(name, g, w, *, sharded=False, g_halves=False, out_dtype=F32, tm=512, tk=512, w_resident=False, side=None):
    m, n = g.shape[-2:]
    tm = _pick(m, tm, 16)
    order, ix = _grid_order(w_resident)
    dims = (((1,), (1,)), ((), ()))
    g_spec = pl.BlockSpec((2, tm, n), ix(lambda i, j: (0, i, 0))) if g_halves else pl.BlockSpec((tm, n), ix(lambda i, j: (i, 0)))
    if sharded:
        s, k, ns = w.shape
        tk = _pick(k, tk, LANES)
        w_spec = pl.BlockSpec((s, tk, ns), ix(lambda i, j: (0, j, 0)))

        def columns(g_ref, q):
            if not g_halves:
                return g_ref[:, q * ns:(q + 1) * ns]
            half, at = divmod(q, s // 2)
            return g_ref[half, :, at * ns:(at + 1) * ns]

        def body(g_ref, w_ref, o_ref):
            acc = lax.dot_general(columns(g_ref, 0), w_ref[0], dims, preferred_element_type=F32)
            for q in range(1, s):
                acc = acc + lax.dot_general(columns(g_ref, q), w_ref[q], dims, preferred_element_type=F32)
            o_ref[...] = acc.astype(out_dtype)
    else:
        k = w.shape[0]
        tk = _pick(k, tk, LANES)
        w_spec = pl.BlockSpec((tk, n), ix(lambda i, j: (j, 0)))

        def body(g_ref, w_ref, o_ref):
            o_ref[...] = lax.dot_general(g_ref[...], w_ref[...], dims, preferred_element_type=F32).astype(out_dtype)

    return _call(body, side, [g, w], name=name, grid=order((m // tm, k // tk)), in_specs=[g_spec, w_spec],
                 out_specs=pl.BlockSpec((tm, tk), ix(lambda i, j: (i, j))), out_shape=_sds((m, k), out_dtype),
                 compiler_params=_params(("arbitrary", "arbitrary")))


def _mm_tn(name, a, g, *, shards=0, g_halves=False, tk=512, tn=512, g_resident=False, side=None):
    m, k = a.shape
    n = 2 * g.shape[2] if g_halves else g.shape[1]
    tk = _pick(k, tk, LANES)
    order, ix = _grid_order(g_resident)
    dims = (((0,), (0,)), ((), ()))
    if shards:
        ns = n // shards
        tn = _pick(ns, tn, LANES)
        per = ns // tn
        out_spec = pl.BlockSpec((None, tk, tn), ix(lambda i, j: (j // per, i, j % per)))
        out_shape = _sds((shards, k, ns), F32)
    else:
        tn = _pick(n, tn, LANES)
        out_spec = pl.BlockSpec((tk, tn), ix(lambda i, j: (i, j)))
        out_shape = _sds((k, n), F32)

    def body(a_ref, g_ref, o_ref):
        o_ref[...] = lax.dot_general(a_ref[...], g_ref[...], dims, preferred_element_type=F32)

    if g_halves:
        per_half = n // 2 // tn
        g_spec = pl.BlockSpec((None, m, tn), ix(lambda i, j: (j // per_half, 0, j % per_half)))
    else:
        g_spec = pl.BlockSpec((m, tn), ix(lambda i, j: (0, j)))
    return _call(body, side, [a, g], name=name, grid=order((k // tk, n // tn)),
                 in_specs=[pl.BlockSpec((m, tk), ix(lambda i, j: (0, i))), g_spec],
                 out_specs=out_spec, out_shape=out_shape, compiler_params=_params(("arbitrary", "arbitrary")))


def _ffn_in_swiglu(name, a, w, *, tm=512, tn=1408, side=None):
    m, k = a.shape
    s, _, ns = w.shape
    f = s * ns // 2
    tm = _pick(m, tm, 16)
    tn = _pick(ns, tn, LANES)
    per = ns // tn
    order, ix = _grid_order(True)

    def body(a_ref, wg_ref, wu_ref, act_ref, gu_ref):
        x = a_ref[...]
        gate = jnp.dot(x, wg_ref[...], preferred_element_type=F32)
        up = jnp.dot(x, wu_ref[...], preferred_element_type=F32)
        act_ref[...] = _swiglu(gate, up).astype(BF16)
        gu_ref[0] = gate.astype(BF16)
        gu_ref[1] = up.astype(BF16)

    return _call(body, side, [a, w, w], name=name, grid=order((m // tm, f // tn)),
                 in_specs=[pl.BlockSpec((tm, k), ix(lambda i, j: (i, 0))),
                           pl.BlockSpec((None, k, tn), ix(lambda i, j: (j // per, 0, j % per))),
                           pl.BlockSpec((None, k, tn), ix(lambda i, j: (s // 2 + j // per, 0, j % per)))],
                 out_specs=[pl.BlockSpec((tm, tn), ix(lambda i, j: (i, j))), pl.BlockSpec((2, tm, tn), ix(lambda i, j: (0, i, j)))],
                 out_shape=[_sds((m, f), BF16), _sds((2, m, f), BF16)], compiler_params=_params(("arbitrary", "arbitrary")))


def _d_act_swiglu(name, g, w, gu, *, tm=1024, tk=512, side=None):
    m, n = g.shape
    f = w.shape[0]
    tm = _pick(m, tm, 16)
    tk = _pick(f, tk, LANES)
    dims = (((1,), (1,)), ((), ()))

    def body(g_ref, w_ref, gu_ref, o_ref):
        dact = lax.dot_general(g_ref[...], w_ref[...], dims, preferred_element_type=F32)
        _, vjp = jax.vjp(_swiglu, gu_ref[0].astype(F32), gu_ref[1].astype(F32))
        dgate, dup = vjp(dact)
        o_ref[0] = dgate.astype(BF16)
        o_ref[1] = dup.astype(BF16)

    return _call(body, side, [g, w, gu], name=name, grid=(m // tm, f // tk),
                 in_specs=[pl.BlockSpec((tm, n), lambda i, j: (i, 0)), pl.BlockSpec((tk, n), lambda i, j: (j, 0)),
                           pl.BlockSpec((2, tm, tk), lambda i, j: (0, i, j))],
                 out_specs=pl.BlockSpec((2, tm, tk), lambda i, j: (0, i, j)), out_shape=_sds((2, m, f), BF16),
                 compiler_params=_params(("arbitrary", "arbitrary")))


def _rms(x, g):
    r = lax.rsqrt(jnp.mean(x * x, axis=-1, keepdims=True) + EPS)
    return (x * r) * g


def _glu_out(y_pre, q, glu_b, g_norm):
    ya0 = jax.nn.gelu(y_pre)
    return _rms(ya0 * jax.nn.sigmoid(q + glu_b), g_norm)


def _sgu_rows(zu, zv, ln_g, ln_b, w_s, b_st, g_norm):
    heads, t, _ = w_s.shape
    hd = zu.shape[1] // heads
    uu = jax.nn.gelu(zu)
    vv = jax.nn.gelu(zv)
    mu = jnp.mean(vv, axis=-1, keepdims=True)
    xc = vv - mu
    r = lax.rsqrt(jnp.mean(xc * xc, axis=-1, keepdims=True) + EPS)
    vn = (xc * r) * ln_g + ln_b
    row = lax.broadcasted_iota(jnp.int32, (t, t), 0)
    col = lax.broadcasted_iota(jnp.int32, (t, t), 1)
    causal = row >= col
    chunks = []
    for n in range(zu.shape[0] // t):
        blocks = []
        for h in range(heads):
            wm = jnp.where(causal, w_s[h], jnp.zeros_like(w_s[h])).astype(BF16)
            vb = vn[n * t:(n + 1) * t, h * hd:(h + 1) * hd].astype(BF16)
            blocks.append(jnp.dot(wm, vb, preferred_element_type=F32) + b_st[:, h:h + 1])
        chunks.append(jnp.concatenate(blocks, axis=1))
    s = jnp.concatenate(chunks, axis=0) if len(chunks) > 1 else chunks[0]
    return _rms(uu * s, g_norm)


def _swiglu(gate, up):
    return jax.nn.silu(gate) * up


def _head_loss(x2, gpre, pp, b_g, g_final, target):
    gate = jax.nn.sigmoid(gpre + b_g)
    out = _rms(x2 + gate * pp, g_final)
    err = jnp.square(out - target)
    return 0.5 * jnp.sum(jnp.mean(err, axis=-1))


def _ssm_disc(lam_re, lam_im, log_step):
    lr = jnp.minimum(lam_re, LAMBDA_RE_MAX)
    li = lam_im
    dt = jnp.exp(log_step)
    mag = jnp.exp(lr * dt)
    ang = li * dt
    abar_re = mag * jnp.cos(ang)
    abar_im = mag * jnp.sin(ang)
    nr = abar_re - 1.0
    ni = abar_im
    den = lr * lr + li * li
    q_re = (nr * lr + ni * li) / den
    q_im = (ni * lr - nr * li) / den
    return abar_re, abar_im, q_re, q_im


def _ssm_bbar(q_re, q_im, b_re, b_im):
    return q_re * b_re - q_im * b_im, q_re * b_im + q_im * b_re


def _ssm_discretised(lam_re, lam_im, log_step, bt_re, bt_im):
    ar, ai, qr, qi = _ssm_disc(lam_re, lam_im, log_step)
    return (ar, ai, *_ssm_bbar(qr, qi, bt_re, bt_im))


def _adamw(w, g, m, v):
    m = ADAM_B1 * m + (1.0 - ADAM_B1) * g
    v = ADAM_B2 * v + (1.0 - ADAM_B2) * jnp.square(g)
    m_hat = m / (1.0 - ADAM_B1 ** ADAM_STEP)
    v_hat = v / (1.0 - ADAM_B2 ** ADAM_STEP)
    delta = -ADAM_LR * (m_hat / (jnp.sqrt(v_hat) + ADAM_EPS) + ADAM_WD * w)
    return delta, m, v


class _SsmDims:
    def __init__(self, groups, state, gch):
        self.g, self.p, self.h = groups, state, gch
        self.d = groups * gch
        self.cb = min(SSM_CH_BLOCK, self.d)
        self.gb = self.cb // gch
        self.ns = self.gb * state
        self.nb = self.d // self.cb


def _ssm_rows(sd, sp):
    gp = sd.g * sd.p
    log_step = jnp.broadcast_to(sp["ssm_log_step"][:, None], (sd.g, sd.p)).reshape(1, gp)
    bt = [sp[k].reshape(gp, sd.h).T for k in ("ssm_b_re", "ssm_b_im")]
    ct = [sp[k].transpose(1, 0, 2).reshape(sd.h, gp) for k in ("ssm_c_re", "ssm_c_im")]
    return (sp["ssm_lambda_re"].reshape(1, gp), sp["ssm_lambda_im"].reshape(1, gp), log_step, *bt, *ct)


def _block_mask(sd):
    row = lax.broadcasted_iota(jnp.int32, (sd.cb, sd.ns), 0) // sd.h
    col = lax.broadcasted_iota(jnp.int32, (sd.cb, sd.ns), 1) // sd.p
    return row == col


def _scan_consts(pr, pi_, reverse):
    if reverse:
        pi_ = [-v for v in pi_]
    shape = (SUBLANES, pr[0].shape[1])
    rows = lax.broadcasted_iota(jnp.int32, shape, 0)
    parts = []
    for d in (1, 2, 4):
        keep = (rows < SUBLANES - d) if reverse else (rows >= d)
        parts += [jnp.where(keep, jnp.broadcast_to(v[d - 1], shape), 0.0) for v in (pr, pi_)]
    order = range(SUBLANES - 1, -1, -1) if reverse else range(SUBLANES)
    parts += [jnp.concatenate([v[t] for t in order], axis=0) for v in (pr, pi_)]
    return jnp.concatenate(parts, axis=0)


def _ssm_operands(sd, rows):
    cb, ns, nb = sd.cb, sd.ns, sd.nb

    def body(lam_re, lam_im, log_step, bt_re, bt_im, ct_re, ct_im, wb_ref, wbt_ref, wc_ref, wct_ref, cst_f_ref, cst_r_ref):
        ar, ai, bbar_re, bbar_im = _ssm_discretised(lam_re[...], lam_im[...], log_step[...], bt_re[...], bt_im[...])
        pr, pi_ = [ar], [ai]
        for _ in range(SUBLANES - 1):
            pr, pi_ = pr + [pr[-1] * ar - pi_[-1] * ai], pi_ + [pr[-1] * ai + pi_[-1] * ar]
        mask = _block_mask(sd)
        spread = lambda src: jnp.where(mask, jnp.concatenate([src] * sd.gb, axis=0), 0.0)
        for j in range(nb):
            at = slice(j * ns, (j + 1) * ns)
            w = jnp.concatenate([spread(bbar_re[:, at]), spread(bbar_im[:, at])], axis=1)
            v = jnp.concatenate([spread(ct_re[:, at]), -spread(ct_im[:, at])], axis=1)
            wb_ref[j] = w.astype(BF16)
            wbt_ref[j] = w.T.astype(BF16)
            wct_ref[j] = v.astype(BF16)
            wc_ref[j] = v.T.astype(BF16)
            pj, qj = [u[:, at] for u in pr], [u[:, at] for u in pi_]
            cst_f_ref[j] = _scan_consts(pj, qj, False)
            cst_r_ref[j] = _scan_consts(pj, qj, True)

    wide, tall = _sds((nb, cb, 2 * ns), BF16), _sds((nb, 2 * ns, cb), BF16)
    cst = _sds((nb, 8 * SUBLANES, ns), F32)
    vm = pl.BlockSpec(memory_space=pltpu.VMEM)
    return pl.pallas_call(body, name="ssm_operands", in_specs=[vm] * 7, out_specs=[vm] * 6,
                          out_shape=[wide, tall, tall, wide, cst, cst],
                          compiler_params=pltpu.CompilerParams(vmem_limit_bytes=VMEM_LIMIT))(*rows)


def _ssm_param_grads(sd, rows, dwb, dwc, da):
    ns, nb, gp = sd.ns, sd.nb, sd.g * sd.p

    def body(lam_re, lam_im, log_step, bt_re, bt_im, dwb_v, dwc_v, da_v, *outs):
        mask = _block_mask(sd)

        def fold(dense):
            kept = jnp.where(mask, dense, 0.0)
            acc = kept[0:sd.h]
            for gl in range(1, sd.gb):
                acc = acc + kept[gl * sd.h:(gl + 1) * sd.h]
            return acc

        lanes = lambda parts: jnp.concatenate(parts, axis=1) if len(parts) > 1 else parts[0]
        dbbar_re = lanes([fold(dwb_v[j][:, :ns]) for j in range(nb)])
        dbbar_im = lanes([fold(dwb_v[j][:, ns:]) for j in range(nb)])
        dwct = [dwc_v[j] for j in range(nb)]
        d_ct_re = lanes([fold(t[:, :ns]) for t in dwct])
        d_ct_im = -lanes([fold(t[:, ns:]) for t in dwct])
        dabar_re = lanes([da_v[j][0:1, :ns] for j in range(nb)])
        dabar_im = lanes([da_v[j][0:1, ns:] for j in range(nb)])
        _, vjp = jax.vjp(_ssm_discretised, lam_re[...], lam_im[...], log_step[...], bt_re[...], bt_im[...])
        d_lr, d_li, d_ls, d_bt_re, d_bt_im = vjp((dabar_re, dabar_im, dbbar_re, dbbar_im))
        group = (lax.broadcasted_iota(jnp.int32, (gp, sd.g), 0) // sd.p == lax.broadcasted_iota(jnp.int32, (gp, sd.g), 1))
        d_log_step = jnp.dot(d_ls, group.astype(F32), precision=lax.Precision.HIGHEST, preferred_element_type=F32)
        for ref, val in zip(outs, (d_lr, d_li, d_log_step, d_bt_re, d_bt_im, d_ct_re, d_ct_im)):
            ref[...] = val

    row, mat = _sds((1, gp), F32), _sds((sd.h, gp), F32)
    vm = pl.BlockSpec(memory_space=pltpu.VMEM)
    return pl.pallas_call(body, name="ssm_param_grads", in_specs=[vm] * 8, out_specs=[vm] * 7,
                          out_shape=[row, row, _sds((1, sd.g), F32), mat, mat, mat, mat],
                          compiler_params=pltpu.CompilerParams(vmem_limit_bytes=VMEM_LIMIT))(*rows[:5], dwb, dwc, da)


def _block_scan(s_ref, cst_ref, carry_ref, sd, rows, reverse):
    ns = sd.ns
    nblk = rows // SUBLANES
    w = min(SCAN_LANES, ns)
    for c0 in range(0, ns, w):
        re_l, im_l = slice(c0, c0 + w), slice(ns + c0, ns + c0 + w)
        cst = [cst_ref[k * SUBLANES:(k + 1) * SUBLANES, c0:c0 + w] for k in range(8)]

        def step(k, carry, re_l=re_l, im_l=im_l, cst=cst):
            local = []
            for b in range(SCAN_BLOCKS):
                blk = SCAN_BLOCKS * k + b
                blk = (nblk - 1 - blk) if reverse else blk
                r0 = pl.multiple_of(blk * SUBLANES, SUBLANES)
                xr = s_ref[pl.ds(r0, SUBLANES), re_l]
                xi = s_ref[pl.ds(r0, SUBLANES), im_l]
                for n, d in enumerate((1, 2, 4)):
                    ar, ai = cst[2 * n], cst[2 * n + 1]
                    shift = (SUBLANES - d) if reverse else d
                    sr = pltpu.roll(xr, shift, 0)
                    si = pltpu.roll(xi, shift, 0)
                    xr, xi = xr + ar * sr - ai * si, xi + ar * si + ai * sr
                local.append((r0, xr, xi))
            cr, ci = carry
            edge = slice(0, 1) if reverse else slice(SUBLANES - 1, SUBLANES)
            for r0, xr, xi in local:
                br = jnp.broadcast_to(cr, xr.shape)
                bi = jnp.broadcast_to(ci, xi.shape)
                xr, xi = xr + cst[6] * br - cst[7] * bi, xi + cst[6] * bi + cst[7] * br
                s_ref[pl.ds(r0, SUBLANES), re_l] = xr
                s_ref[pl.ds(r0, SUBLANES), im_l] = xi
                cr, ci = xr[edge, :], xi[edge, :]
            return cr, ci

        cr, ci = lax.fori_loop(0, nblk // SCAN_BLOCKS, step, (carry_ref[0:1, re_l], carry_ref[0:1, im_l]))
        carry_ref[0:1, re_l] = cr
        carry_ref[0:1, im_l] = ci


def _ssm_fwd(name, sd, z, wb, wc, cst, d_row, tt=512, side=None):
    n_tok = z.shape[0]
    tt = _pick(n_tok, tt, 16)
    cb, ns2 = sd.cb, 2 * sd.ns

    def body(z_ref, wb_ref, wc_ref, cst_ref, d_ref, y_ref, s_ref, a0_ref, carry_ref):
        @pl.when(pl.program_id(1) == 0)
        def _():
            carry_ref[...] = jnp.zeros(carry_ref.shape, F32)
        u = z_ref[...]
        s_ref[...] = jnp.dot(u.astype(BF16), wb_ref[...], preferred_element_type=F32)
        _block_scan(s_ref, cst_ref, carry_ref, sd, tt, reverse=False)
        y = jnp.dot(s_ref[...].astype(BF16), wc_ref[...], preferred_element_type=F32) + d_ref[...] * u
        y_ref[...] = y
        a0_ref[...] = jax.nn.gelu(y).astype(BF16)

    return _call(
        body, side, [z, wb, wc, cst, d_row], name=name, grid=(sd.nb, n_tok // tt),
        in_specs=[pl.BlockSpec((tt, cb), lambda j, i: (i, j)),
                  pl.BlockSpec((None, cb, ns2), lambda j, i: (j, 0, 0)),
                  pl.BlockSpec((None, ns2, cb), lambda j, i: (j, 0, 0)),
                  pl.BlockSpec((None, 8 * SUBLANES, sd.ns), lambda j, i: (j, 0, 0)),
                  pl.BlockSpec((1, cb), lambda j, i: (0, j))],
        out_specs=[pl.BlockSpec((tt, cb), lambda j, i: (i, j)), pl.BlockSpec((tt, ns2), lambda j, i: (i, j)),
                   pl.BlockSpec((tt, cb), lambda j, i: (i, j))],
        out_shape=[_sds((n_tok, sd.d), F32), _sds((n_tok, sd.nb * ns2), F32), _sds((n_tok, sd.d), BF16)],
        scratch_shapes=[pltpu.VMEM((SUBLANES, ns2), F32)],
        compiler_params=_params(("arbitrary", "arbitrary")))


def _ssm_bwd(name, sd, y_pre, dy_direct, dya0, z, states, wct, wbt, cst_rev, d_row, tt=512, side=None):
    n_tok = z.shape[0]
    tt = _pick(n_tok, tt, 16)
    nt = n_tok // tt
    cb, ns, ns2 = sd.cb, sd.ns, 2 * sd.ns
    blocks_per_tile = tt // SUBLANES
    tn_dims = (((0,), (0,)), ((), ()))

    def body(y_ref, dyd_ref, dya0_ref, z_ref, s_ref, sp_ref, wct_ref, wbt_ref, cst_ref, d_ref,
             du_ref, dwb_ref, dwc_ref, da_ref, dd_ref, lam_ref, carry_ref):
        i = pl.program_id(1)

        @pl.when(i == 0)
        def _():
            carry_ref[...] = jnp.zeros(carry_ref.shape, F32)
            dwb_ref[...] = jnp.zeros(dwb_ref.shape, F32)
            dwc_ref[...] = jnp.zeros(dwc_ref.shape, F32)
            da_ref[...] = jnp.zeros(da_ref.shape, F32)
            dd_ref[...] = jnp.zeros(dd_ref.shape, F32)

        _, gelu_vjp = jax.vjp(jax.nn.gelu, y_ref[...])
        dy_t = dyd_ref[...] + gelu_vjp(dya0_ref[...].astype(F32))[0]
        u = z_ref[...]
        dy16 = dy_t.astype(BF16)
        lam_ref[...] = jnp.dot(dy16, wct_ref[...], preferred_element_type=F32)
        _block_scan(lam_ref, cst_ref, carry_ref, sd, tt, reverse=True)
        lam = lam_ref[...]
        lam16 = lam.astype(BF16)
        du_ref[...] = (jnp.dot(lam16, wbt_ref[...], preferred_element_type=F32) + d_ref[...] * dy_t).astype(BF16)
        dd_ref[0:1, :] += jnp.sum(dy_t * u, axis=0, keepdims=True)
        dwb_ref[...] += lax.dot_general(u.astype(BF16), lam16, tn_dims, preferred_element_type=F32)
        s = s_ref[...]
        dwc_ref[...] += lax.dot_general(dy16, s.astype(BF16), tn_dims, preferred_element_type=F32)
        before = jnp.where(i == nt - 1, 0.0, 1.0) * sp_ref[SUBLANES - 1:SUBLANES, :]
        first_row = lax.broadcasted_iota(jnp.int32, s.shape, 0) == 0
        prev = jnp.where(first_row, jnp.broadcast_to(before, s.shape), pltpu.roll(s, 1, 0))
        lr, li = lam[:, :ns], lam[:, ns:]
        pr, pi_ = prev[:, :ns], prev[:, ns:]
        da_ref[0:1, 0:ns] += jnp.sum(lr * pr + li * pi_, axis=0, keepdims=True)
        da_ref[0:1, ns:ns2] += jnp.sum(li * pr - lr * pi_, axis=0, keepdims=True)

    rev = lambda i: nt - 1 - i
    return _call(
        body, side, [y_pre, dy_direct, dya0, z, states, states, wct, wbt, cst_rev, d_row], name=name, grid=(sd.nb, nt),
        in_specs=[pl.BlockSpec((tt, cb), lambda j, i: (rev(i), j)),
                  pl.BlockSpec((tt, cb), lambda j, i: (rev(i), j)),
                  pl.BlockSpec((tt, cb), lambda j, i: (rev(i), j)),
                  pl.BlockSpec((tt, cb), lambda j, i: (rev(i), j)),
                  pl.BlockSpec((tt, ns2), lambda j, i: (rev(i), j)),
                  pl.BlockSpec((SUBLANES, ns2), lambda j, i: (jnp.maximum(rev(i) * blocks_per_tile - 1, 0), j)),
                  pl.BlockSpec((None, cb, ns2), lambda j, i: (j, 0, 0)),
                  pl.BlockSpec((None, ns2, cb), lambda j, i: (j, 0, 0)),
                  pl.BlockSpec((None, 8 * SUBLANES, ns), lambda j, i: (j, 0, 0)),
                  pl.BlockSpec((1, cb), lambda j, i: (0, j))],
        out_specs=[pl.BlockSpec((tt, cb), lambda j, i: (rev(i), j)),
                   pl.BlockSpec((None, cb, ns2), lambda j, i: (j, 0, 0)),
                   pl.BlockSpec((None, cb, ns2), lambda j, i: (j, 0, 0)),
                   pl.BlockSpec((None, SUBLANES, ns2), lambda j, i: (j, 0, 0)),
                   pl.BlockSpec((None, SUBLANES, cb), lambda j, i: (j, 0, 0))],
        out_shape=[_sds((n_tok, sd.d), BF16), _sds((sd.nb, cb, ns2), F32), _sds((sd.nb, cb, ns2), F32),
                   _sds((sd.nb, SUBLANES, ns2), F32), _sds((sd.nb, SUBLANES, cb), F32)],
        scratch_shapes=[pltpu.VMEM((tt, ns2), F32), pltpu.VMEM((SUBLANES, ns2), F32)],
        compiler_params=_params(("arbitrary", "arbitrary")))


def _hosted(exch, fn, name, *args, **kw):
    side = exch.side(name)
    if side is None:
        return fn(name, *args, **kw)
    out, moved = fn(name, *args, side=side, **kw)
    exch.done(name, moved)
    return out


def _local_grads(x, p, target, sp, exch):
    n_tok, d_model = x.shape
    d_ssm = sp["ssm_d"].shape[0] * sp["ssm_d"].shape[1]
    d_sgu = sp["sgu_ln_g"].shape[-1]
    sd = _SsmDims(sp["ssm_b_re"].shape[0], sp["ssm_b_re"].shape[1], sp["ssm_b_re"].shape[2])
    heads, chunk, _ = sp["sgu_w"].shape
    row = lambda v: v.reshape(1, -1)
    tok = lambda w, dt=F32: _sds((n_tok, w), dt)
    acc = lambda w: _sds((1, w), F32)

    g_mix = row(sp["norm_mix_g"])
    (h1,) = _hosted(exch, _rowwise, "norm_mix", lambda a, g: _rms(a, g), [x], [g_mix], [tok(d_model, BF16)])
    z = _hosted(exch, _mm_nn, "proj_in", h1, exch.weight("w_in"), sharded=True, tn=768)

    ssm_rows = _ssm_rows(sd, sp)
    wb, wbt, wc, wct, cst_fwd, cst_rev = _ssm_operands(sd, ssm_rows)
    d_row = row(sp["ssm_d"])
    y_pre, states, ya0_16 = _hosted(exch, _ssm_fwd, "ssm_fwd", sd, z, wb, wc, cst_fwd, d_row)
    q = _mm_nn("ssm_glu", ya0_16, exch.weight("ssm_glu_w"), tm=1024)
    glu_b, g_ossm = row(sp["ssm_glu_b"]), row(sp["out_norm_ssm_g"])
    (ya_n,) = _rowwise("ssm_glu_out", _glu_out, [y_pre, q], [glu_b, g_ossm], [tok(d_ssm, BF16)])

    assert d_ssm == d_sgu
    zu, zv = _Cols(z, d_sgu, 1), _Cols(z, d_sgu, 2)
    ln_g, ln_b, g_osgu = row(sp["sgu_ln_g"]), row(sp["sgu_ln_b"]), row(sp["out_norm_sgu_g"])
    b_st = sp["sgu_b"].T
    sgu_tr = 2 * chunk

    def sgu_joined(ya_t, zu_t, zv_t, *params):
        return jnp.concatenate([ya_t, _sgu_rows(zu_t, zv_t, *params).astype(BF16)], axis=1)

    (ycat,) = _rowwise("sgu", sgu_joined, [ya_n, zu, zv], [ln_g, ln_b, sp["sgu_w"], b_st, g_osgu],
                       [tok(d_ssm + d_sgu, BF16)], tr=sgu_tr)
    x1 = _mm_nn("proj_out", ycat, exch.weight("w_out"), res=x, tm=1024)

    g_ffn = row(sp["norm_ffn_g"])
    (h2,) = _rowwise("norm_ffn", lambda a, g: _rms(a, g), [x1], [g_ffn], [tok(d_model, BF16)])
    act, gu16 = _hosted(exch, _ffn_in_swiglu, "ffn_in", h2, exch.weight("w_ffn_in"))
    x2 = _mm_nn("ffn_out", act, exch.weight("w_ffn_out"), res=x1)

    g_ple = row(sp["norm_ple_g"])
    (h3,) = _rowwise("norm_ple", lambda a, g: _rms(a, g), [x2], [g_ple], [tok(d_model, BF16)])
    gpre = _mm_nn("ple_gate", h3, exch.weight("w_ple_gate"), tm=1024)
    (p16,) = _rowwise("ple_cast", lambda a: a, [p], [], [tok(p.shape[1], BF16)])
    pp = _mm_nn("ple_proj", p16, exch.weight("w_ple_proj"), sharded=True, tm=1024)

    b_g, g_fin = row(sp["b_ple_gate"]), row(sp["final_norm_g"])

    def head(x2_t, gpre_t, pp_t, tgt_t, b_g_v, g_fin_v):
        loss, grads = jax.value_and_grad(_head_loss, argnums=(0, 1, 2, 3, 4))(x2_t, gpre_t, pp_t, b_g_v, g_fin_v, tgt_t)
        dx2, dgpre, dpp, db, dg = grads
        return dx2, dgpre.astype(BF16), dpp.astype(BF16), jnp.full((1, LANES), loss, F32), db, dg

    dx2_head, dgpre16, dpp16, loss_row, d_b_g, d_g_fin = _rowwise(
        "head", head, [x2, gpre, pp, target], [b_g, g_fin],
        [tok(d_model), tok(d_model, BF16), tok(d_model, BF16)], [acc(LANES), acc(d_model), acc(d_model)])
    loss = loss_row[0, 0]

    exch.grad("w_ple_proj", _mm_tn("d_ple_proj", p16, dpp16, shards=N_CHIPS, tk=256))
    exch.grad("w_ple_gate", _mm_tn("d_ple_gate", h3, dgpre16))
    dh3 = _mm_nt("d_h3", dgpre16, exch.weight("w_ple_gate"), out_dtype=BF16, tm=1024)

    def norm_bwd(x_t, dres_t, dh_t, g_v):
        _, vjp = jax.vjp(_rms, x_t, g_v)
        dx, dg = vjp(dh_t.astype(F32))
        dx = dres_t + dx
        return dx, dx.astype(BF16), dg

    dx2, dx2_16, d_g_ple = _rowwise("d_norm_ple", norm_bwd, [x2, dx2_head, dh3], [g_ple],
                                    [tok(d_model), tok(d_model, BF16)], [acc(d_model)])
    exch.grad("w_ffn_out", _mm_tn("d_ffn_out", act, dx2_16))
    dgu16 = _hosted(exch, _d_act_swiglu, "d_act", dx2_16, exch.weight("w_ffn_out"), gu16)
    exch.grad("w_ffn_in", _hosted(exch, _mm_tn, "d_ffn_in", h2, dgu16, shards=N_CHIPS, g_halves=True, tn=1408, g_resident=True))
    dh2 = _hosted(exch, _mm_nt, "d_h2", dgu16, exch.weight("w_ffn_in"), sharded=True, g_halves=True, out_dtype=BF16, tm=256, w_resident=True)
    dx1, dx1_16, d_g_ffn = _rowwise("d_norm_ffn", norm_bwd, [x1, dx2, dh2], [g_ffn],
                                    [tok(d_model), tok(d_model, BF16)], [acc(d_model)])
    exch.grad("w_out", _mm_tn("d_proj_out", ycat, dx1_16))
    dycat = _mm_nt("d_ycat", dx1_16, exch.weight("w_out"), out_dtype=BF16, tm=1024)

    def glu_out_bwd(y_pre_t, q_t, dy_t, glu_b_v, g_v):
        _, vjp = jax.vjp(_glu_out, y_pre_t, q_t, glu_b_v, g_v)
        dy_pre, dq, db, dg = vjp(dy_t.astype(F32))
        return dy_pre, dq.astype(BF16), db, dg

    dy_pre_a, dq16, d_glu_b, d_g_ossm = _rowwise(
        "d_ssm_glu_out", glu_out_bwd, [y_pre, q, _Cols(dycat, d_ssm, 0)], [glu_b, g_ossm],
        [tok(d_ssm), tok(d_ssm, BF16)], [acc(d_ssm), acc(d_ssm)])
    exch.grad("ssm_glu_w", _mm_tn("d_ssm_glu", ya0_16, dq16))
    dya0 = _hosted(exch, _mm_nt, "d_ya0", dq16, exch.weight("ssm_glu_w"), out_dtype=BF16, tm=1024)

    dz_ssm16, dwb, dwc, da, dd = _hosted(exch, _ssm_bwd, "ssm_bwd", sd, y_pre, dy_pre_a, dya0, z, states, wct, wbt,
                                         cst_rev, d_row)

    def sgu_bwd(dz_ssm_t, zu_t, zv_t, dy_t, ln_g_v, ln_b_v, w_v, b_v, g_v):
        _, vjp = jax.vjp(_sgu_rows, zu_t, zv_t, ln_g_v, ln_b_v, w_v, b_v, g_v)
        dzu, dzv, dlg, dlb, dw, db, dg = vjp(dy_t.astype(F32))
        return jnp.concatenate([dz_ssm_t, dzu.astype(BF16), dzv.astype(BF16)], axis=1), dlg, dlb, dw, db, dg

    dz16, d_ln_g, d_ln_b, d_sgu_w, d_b_st, d_g_osgu = _hosted(
        exch, _rowwise, "d_sgu", sgu_bwd, [dz_ssm16, zu, zv, _Cols(dycat, d_sgu, 1)], [ln_g, ln_b, sp["sgu_w"], b_st, g_osgu],
        [tok(d_ssm + 2 * d_sgu, BF16)],
        [acc(d_sgu), acc(d_sgu), _sds(sp["sgu_w"].shape, F32), _sds(b_st.shape, F32), acc(d_sgu)], tr=sgu_tr)

    d_lam_re, d_lam_im, d_log_step, d_bt_re, d_bt_im, d_ct_re, d_ct_im = _ssm_param_grads(sd, ssm_rows, dwb, dwc, da)
    d_b_re, d_b_im = d_bt_re.T, d_bt_im.T
    d_c_re, d_c_im = (t.reshape(sd.h, sd.g, sd.p).transpose(1, 0, 2) for t in (d_ct_re, d_ct_im))
    d_ssm_d = dd[:, 0, :].reshape(sd.g, sd.h)

    exch.small_grads({
        "ssm_lambda_re": d_lam_re, "ssm_lambda_im": d_lam_im, "ssm_log_step": d_log_step,
        "ssm_b_re": d_b_re, "ssm_b_im": d_b_im, "ssm_c_re": d_c_re, "ssm_c_im": d_c_im, "ssm_d": d_ssm_d,
        "ssm_glu_b": d_glu_b, "sgu_ln_g": d_ln_g, "sgu_ln_b": d_ln_b, "sgu_w": d_sgu_w, "sgu_b": d_b_st.T,
        "out_norm_ssm_g": d_g_ossm, "out_norm_sgu_g": d_g_osgu, "norm_ffn_g": d_g_ffn, "norm_ple_g": d_g_ple,
        "b_ple_gate": d_b_g, "final_norm_g": d_g_fin,
    })

    exch.grad("w_in", _hosted(exch, _mm_tn, "d_proj_in", h1, dz16, shards=N_CHIPS, tn=768))
    dh1 = _hosted(exch, _mm_nt, "d_h1", dz16, exch.weight("w_in"), sharded=True, out_dtype=BF16, tm=1024)

    def norm_in_bwd(x_t, dres_t, dh_t, g_v):
        _, vjp = jax.vjp(_rms, x_t, g_v)
        dx, dg = vjp(dh_t.astype(F32))
        return dres_t + dx, dg

    grad_x, d_g_mix = _hosted(exch, _rowwise, "d_norm_mix", norm_in_bwd, [x, dx1, dh1], [g_mix], [tok(d_model)], [acc(d_model)])
    exch.small_grads({"norm_mix_g": d_g_mix})
    return loss, grad_x


def _place():
    x, y, c = lax.axis_index("x"), lax.axis_index("y"), lax.axis_index("c")
    chips = [(1 - x, y), (x, 1 - y), (1 - x, 1 - y)]
    return x, y, c, chips


def _cast_into_slot(name, w2d, shard, tr=256):
    rows, cols = w2d.shape
    rh = rows // 2
    tr = _pick(rh, tr, 16)
    per = rh // tr

    def body(s_ref, a_ref, o_ref):
        o_ref[...] = a_ref[...].astype(BF16)

    grid_spec = pltpu.PrefetchScalarGridSpec(
        num_scalar_prefetch=1, grid=(2, per),
        in_specs=[pl.BlockSpec((tr, cols), lambda h, i, s_ref: (h * per + i, 0))],
        out_specs=pl.BlockSpec((None, None, tr, cols), lambda h, i, s_ref: (s_ref[0], h, i, 0)))
    return pl.pallas_call(body, name=name, grid_spec=grid_spec, out_shape=_sds((N_CHIPS, 2, rh, cols), BF16),
                          compiler_params=_params(("arbitrary", "arbitrary")))(shard.reshape(1).astype(jnp.int32), w2d)


def _exchange_alone(name, side):
    n_in, n_out = len(side.ins), len(side.out_shapes)

    def body(*refs):
        ins, outs, sems = refs[:n_in], refs[n_in:n_in + n_out], refs[n_in + n_out:]
        side.first(ins, outs, *sems)
        if side.mid is not None:
            side.mid(ins, outs, *sems)
        side.last(ins, outs, *sems)

    return pl.pallas_call(
        body, name=name, in_specs=[ANY] * n_in, out_specs=[ANY] * n_out, out_shape=side.out_shapes,
        input_output_aliases=side.aliases,
        scratch_shapes=[pltpu.SemaphoreType.DMA((side.n_sems,)), pltpu.SemaphoreType.DMA((side.n_sems,))],
    )(*side.ins)


def _gather_side(slots, mid_late=False):
    n = len(slots)

    def copies(kind, outs, send_sems, recv_sems):
        x, y, c, chips = _place()

        def remote(k, ref, to):
            return pltpu.make_async_remote_copy(src_ref=ref, dst_ref=ref, send_sem=send_sems.at[k], recv_sem=recv_sems.at[k],
                                                device_id=to, device_id_type=MESH)

        pairs = [(w, j, 2 * cx + cy, (cx, cy)) for w in range(n) for j, (cx, cy) in enumerate(chips)]
        if kind == "sends":
            return [remote(3 * w + j, outs[w].at[2 * x + y, c], (*chip, c)) for w, j, _, chip in pairs]
        if kind == "arrivals":
            return [remote(3 * w + j, outs[w].at[s, c], (x, y, c)) for w, j, s, _ in pairs]
        if kind == "passed":
            return [remote(3 * n + 3 * w + j, outs[w].at[s, c], (x, y, 1 - c)) for w, j, s, _ in pairs]
        return [remote(3 * n + 3 * w + j, outs[w].at[s, 1 - c], (x, y, c)) for w, j, s, _ in pairs]

    def first(ins, outs, *sems):
        for cp in copies("sends", outs, *sems):
            cp.start()

    def mid(ins, outs, *sems):
        for arrived, onward in zip(copies("arrivals", outs, *sems), copies("passed", outs, *sems)):
            arrived.wait_recv()
            onward.start()

    def last(ins, outs, *sems):
        for cp in copies("from_sibling", outs, *sems):
            cp.wait_recv()
        for cp in copies("sends", outs, *sems) + copies("passed", outs, *sems):
            cp.wait_send()

    return _Side(slots, [_sds(s.shape, s.dtype) for s in slots], 6 * n, first, last, mid=mid, aliases={w: w for w in range(n)},
                 mid_late=mid_late)


def _swap_side(grads):
    n = len(grads)

    def copies(ins, outs, send_sems, recv_sems):
        x, y, c, _ = _place()
        return [pltpu.make_async_remote_copy(src_ref=ins[w].at[:, 1 - c], dst_ref=outs[w], send_sem=send_sems.at[w],
                                             recv_sem=recv_sems.at[w], device_id=(x, y, 1 - c), device_id_type=MESH)
                for w in range(n)]

    def first(*refs):
        for cp in copies(*refs):
            cp.start()

    def last(*refs):
        for cp in copies(*refs):
            cp.wait()

    return _Side(grads, [_sds((g.shape[0], *g.shape[2:]), g.dtype) for g in grads], n, first, last)


def _scatter_side(halves):
    n = len(halves)

    def copies(ins, outs, send_sems, recv_sems):
        x, y, c, chips = _place()
        return [pltpu.make_async_remote_copy(
            src_ref=ins[w].at[2 * cx + cy], dst_ref=outs[w].at[j], send_sem=send_sems.at[3 * w + j],
            recv_sem=recv_sems.at[3 * w + j], device_id=(cx, cy, c), device_id_type=MESH)
            for w in range(n) for j, (cx, cy) in enumerate(chips)]

    def first(*refs):
        for cp in copies(*refs):
            cp.start()

    def last(*refs):
        for cp in copies(*refs):
            cp.wait()

    return _Side(halves, [_sds((3, *h.shape[1:]), h.dtype) for h in halves], 3 * n, first, last)


def _join_halves(name, slots):
    n = len(slots)

    def body(*refs):
        outs = refs[n:2 * n]
        send_sems, recv_sems = refs[2 * n:]
        x, y, c, _ = _place()

        def copy(w, half, to):
            return pltpu.make_async_remote_copy(src_ref=outs[w].at[half], dst_ref=outs[w].at[half], send_sem=send_sems.at[w],
                                                recv_sem=recv_sems.at[w], device_id=to, device_id_type=MESH)

        copies = [copy(w, c, (x, y, 1 - c)) for w in range(n)]
        for cp in copies:
            cp.start()
        for w in range(n):
            copy(w, 1 - c, (x, y, c)).wait_recv()
        for cp in copies:
            cp.wait_send()

    return pl.pallas_call(
        body, name=name, in_specs=[ANY] * n, out_specs=[ANY] * n,
        out_shape=[_sds(s.shape, s.dtype) for s in slots], input_output_aliases={w: w for w in range(n)},
        scratch_shapes=[pltpu.SemaphoreType.DMA((n,)), pltpu.SemaphoreType.DMA((n,))],
    )(*slots)


def _allreduce_small(block, tr=256):
    rows, lanes = block.shape
    tr = _pick(rows, tr, SUBLANES)

    def body(x_ref, o_ref, buf, send_sems, recv_sems):
        x, y, c, chips = _place()
        me, sibling = (x, y, c), (x, y, 1 - c)

        def slot(px, py, pc):
            return buf.at[4 * px + 2 * py + pc]

        def copy(k, block_of, to):
            return pltpu.make_async_remote_copy(src_ref=slot(*block_of), dst_ref=slot(*block_of), send_sem=send_sems.at[k],
                                                recv_sem=recv_sems.at[k], device_id=to, device_id_type=MESH)

        slot(*me)[...] = x_ref[...]
        first = [copy(0, me, sibling)] + [copy(1 + j, me, (*chip, c)) for j, chip in enumerate(chips)]
        for cp in first:
            cp.start()
        passed = [copy(4 + j, (*chip, c), sibling) for j, chip in enumerate(chips)]
        for j, chip in enumerate(chips):
            copy(1 + j, (*chip, c), me).wait_recv()
            passed[j].start()
        copy(0, sibling, me).wait_recv()
        for j, chip in enumerate(chips):
            copy(4 + j, (*chip, 1 - c), me).wait_recv()
        for cp in first + passed:
            cp.wait_send()
        for r0 in range(0, rows, tr):
            acc = buf[0, r0:r0 + tr, :]
            for k in range(1, N_DEV):
                acc = acc + buf[k, r0:r0 + tr, :]
            o_ref[r0:r0 + tr, :] = acc

    vm = pl.BlockSpec(memory_space=pltpu.VMEM)
    return pl.pallas_call(
        body, name="allreduce_small", in_specs=[vm], out_specs=vm, out_shape=_sds((rows, lanes), block.dtype),
        scratch_shapes=[pltpu.VMEM((N_DEV, rows, lanes), block.dtype), pltpu.SemaphoreType.DMA((7,)), pltpu.SemaphoreType.DMA((7,))],
        compiler_params=pltpu.CompilerParams(vmem_limit_bytes=VMEM_LIMIT),
    )(block)


def _small_gather_side(block):
    def copy(kind, j, ins, outs, send_sems, recv_sems):
        x, y, c, chips = _place()
        chip = chips[j] if j is not None else None
        slot = lambda px, py, pc: outs[0].at[4 * px + 2 * py + pc]

        def remote(k, src, dst, to):
            return pltpu.make_async_remote_copy(src_ref=src, dst_ref=dst, send_sem=send_sems.at[k], recv_sem=recv_sems.at[k],
                                                device_id=to, device_id_type=MESH)

        if kind == "to_sibling":
            return remote(0, ins[0], slot(x, y, c), (x, y, 1 - c))
        if kind == "from_sibling":
            return remote(0, ins[0], slot(x, y, 1 - c), (x, y, c))
        if kind == "to_chip":
            return remote(1 + j, ins[0], slot(x, y, c), (*chip, c))
        if kind == "from_chip":
            return remote(1 + j, ins[0], slot(*chip, c), (x, y, c))
        if kind == "pass_on":
            return remote(4 + j, slot(*chip, c), slot(*chip, c), (x, y, 1 - c))
        return remote(4 + j, slot(*chip, 1 - c), slot(*chip, 1 - c), (x, y, c))

    def first(*refs):
        copy("to_sibling", None, *refs).start()
        for j in range(3):
            copy("to_chip", j, *refs).start()

    def mid(*refs):
        for j in range(3):
            copy("from_chip", j, *refs).wait_recv()
            copy("pass_on", j, *refs).start()

    def last(*refs):
        copy("from_sibling", None, *refs).wait_recv()
        for j in range(3):
            copy("passed_on", j, *refs).wait_recv()
        copy("to_sibling", None, *refs).wait_send()
        for j in range(3):
            copy("to_chip", j, *refs).wait_send()
            copy("pass_on", j, *refs).wait_send()

    return _Side([block], [_sds((N_DEV, *block.shape), block.dtype)], 7, first, last, mid=mid, mid_late=True)


def _sum_slots(name, own, gathered, me, tr=512):
    n, rows, cols = gathered.shape
    tr = _pick(rows, tr, SUBLANES)

    def body(me_ref, own_ref, g_ref, o_ref):
        mine = own_ref[...]
        acc = jnp.where(me_ref[0] == 0, mine, g_ref[0])
        for k in range(1, n):
            acc = acc + jnp.where(me_ref[0] == k, mine, g_ref[k])
        o_ref[...] = acc

    grid_spec = pltpu.PrefetchScalarGridSpec(
        num_scalar_prefetch=1, grid=(rows // tr,),
        in_specs=[pl.BlockSpec((tr, cols), lambda i, me_ref: (i, 0)), pl.BlockSpec((n, tr, cols), lambda i, me_ref: (0, i, 0))],
        out_specs=pl.BlockSpec((tr, cols), lambda i, me_ref: (i, 0)))
    return pl.pallas_call(body, name=name, grid_spec=grid_spec, out_shape=_sds((rows, cols), own.dtype),
                          compiler_params=_params(("arbitrary",)))(me.reshape(1).astype(jnp.int32), own, gathered)


def _sum_received(name, full, c, shard, swapped, received, tr=256):
    n, rows, cols = received.shape
    tr = _pick(rows, tr, 16)

    def body(i_ref, a_ref, b_ref, s_ref, o_ref):
        acc = a_ref[...] + b_ref[...]
        for k in range(n):
            acc = acc + s_ref[k].astype(F32)
        o_ref[...] = acc

    grid_spec = pltpu.PrefetchScalarGridSpec(
        num_scalar_prefetch=1, grid=(rows // tr,),
        in_specs=[pl.BlockSpec((None, None, tr, cols), lambda i, i_ref: (i_ref[1], i_ref[0], i, 0)),
                  pl.BlockSpec((None, tr, cols), lambda i, i_ref: (i_ref[1], i, 0)),
                  pl.BlockSpec((n, tr, cols), lambda i, i_ref: (0, i, 0))],
        out_specs=pl.BlockSpec((None, tr, cols), lambda i, i_ref: (i_ref[0], i, 0)))
    return pl.pallas_call(body, name=name, grid_spec=grid_spec, out_shape=_sds((2, rows, cols), F32),
                          compiler_params=_params(("arbitrary",)))(jnp.stack([c, shard]).astype(jnp.int32), full, swapped, received)


def _add_halves(name, full, c, shard, received, tr=256):
    s, _, rh, cols = full.shape
    tr = _pick(rh, tr, 16)

    def body(i_ref, a_ref, b_ref, o_ref):
        o_ref[...] = (a_ref[...] + b_ref[...]).astype(BF16)

    other = lambda q, i_ref: (i_ref[1] + 1 + q) % s
    grid_spec = pltpu.PrefetchScalarGridSpec(
        num_scalar_prefetch=1, grid=(s - 1, rh // tr),
        in_specs=[pl.BlockSpec((None, None, tr, cols), lambda q, i, i_ref: (other(q, i_ref), i_ref[0], i, 0)),
                  pl.BlockSpec((None, tr, cols), lambda q, i, i_ref: (other(q, i_ref), i, 0))],
        out_specs=pl.BlockSpec((None, tr, cols), lambda q, i, i_ref: (other(q, i_ref), i, 0)))
    return pl.pallas_call(body, name=name, grid_spec=grid_spec, out_shape=_sds((s, rh, cols), BF16),
                          compiler_params=_params(("arbitrary", "arbitrary")))(jnp.stack([c, shard]).astype(jnp.int32), full, received)


LARGE = ("w_in", "ssm_glu_w", "w_out", "w_ffn_in", "w_ffn_out", "w_ple_gate", "w_ple_proj")
COLUMN_SHARDED = ("w_in", "w_ffn_in", "w_ple_proj")
SMALL = ("norm_mix_g", "ssm_lambda_re", "ssm_lambda_im", "ssm_log_step", "ssm_b_re", "ssm_b_im", "ssm_c_re", "ssm_c_im",
         "ssm_d", "ssm_glu_b", "sgu_ln_g", "sgu_ln_b", "sgu_w", "sgu_b", "out_norm_ssm_g", "out_norm_sgu_g", "norm_ffn_g",
         "norm_ple_g", "b_ple_gate", "final_norm_g")
WEIGHTS = ("norm_mix_g", "w_in", "ssm_lambda_re", "ssm_lambda_im", "ssm_log_step", "ssm_b_re", "ssm_b_im", "ssm_c_re",
           "ssm_c_im", "ssm_d", "ssm_glu_w", "ssm_glu_b", "sgu_ln_g", "sgu_ln_b", "sgu_w", "sgu_b", "out_norm_ssm_g",
           "out_norm_sgu_g", "w_out", "norm_ffn_g", "w_ffn_in", "w_ffn_out", "norm_ple_g", "w_ple_gate", "b_ple_gate",
           "w_ple_proj", "final_norm_g")
PACK_ROWS = SUBLANES * LANES


def _pack(arrays):
    parts = []
    for a in arrays:
        flat = a.reshape(-1).astype(F32)
        pad = -flat.shape[0] % PACK_ROWS
        parts.append(jnp.pad(flat, (0, pad)) if pad else flat)
    return jnp.concatenate(parts).reshape(-1, LANES)


def _unpack(packed, like):
    flat = packed.reshape(-1)
    out, at = [], 0
    for a in like:
        size = a.size
        out.append(flat[at:at + size].reshape(a.shape))
        at += size + (-size % PACK_ROWS)
    return out


class _NoExchange:
    def __init__(self, weights):
        self.weights, self.grads, self.small = weights, {}, {}

    def weight(self, name):
        return self.weights[name]

    def grad(self, name, g):
        self.grads[name] = g

    def small_grads(self, grads):
        self.small.update(grads)

    def side(self, host):
        return None


class _MeshExchange:
    GATHER = {"norm_mix": ("w_in",), "proj_in": ("ssm_glu_w", "w_out"), "ssm_fwd": ("w_ffn_in",),
              "ffn_in": ("w_ffn_out", "w_ple_gate", "w_ple_proj")}
    GATHER_LONG = ("norm_mix", "ssm_fwd")
    SWAP = {"d_act": ("w_ple_proj", "w_ple_gate", "w_ffn_out"), "d_h2": ("w_ffn_in",), "d_ya0": ("w_out", "ssm_glu_w")}
    SWAP_ALONE = ("w_in",)
    SCATTER = {"d_ffn_in": ("w_ple_proj", "w_ple_gate", "w_ffn_out"), "ssm_bwd": ("w_ffn_in",),
               "d_sgu": ("w_out", "ssm_glu_w"), "d_h1": ("w_in",)}
    SMALL_GATHER = "d_proj_in"
    GROUPS = (("w_ple_proj", "w_ple_gate", "w_ffn_out"), ("w_ffn_in",), ("w_out", "ssm_glu_w"), ("w_in",))

    def __init__(self, shards, small_like, c, shard, me):
        self.c, self.shard, self.me, self.small_like = c, shard, me, small_like
        self.slots = {k: _cast_into_slot("cast_" + k, shards[k], shard) for k in LARGE}
        self.full, self.received, self.halves, self.quarters, self.small = {}, {}, {}, {}, {}

    def weight(self, name):
        g = self.slots[name]
        _, _, rh, cols = g.shape
        return g.reshape(N_CHIPS, 2 * rh, cols) if name in COLUMN_SHARDED else g.reshape(N_CHIPS * 2 * rh, cols)

    def grad(self, name, g):
        if name not in COLUMN_SHARDED:
            g = g.reshape(N_CHIPS, g.shape[0] // N_CHIPS, g.shape[1])
        self.full[name] = g.reshape(N_CHIPS, 2, g.shape[1] // 2, g.shape[2])
        if name in self.SWAP_ALONE:
            self._swapped((name,), _exchange_alone("grad_swap_" + name, _swap_side([self.full[name]])))

    def _swapped(self, names, received):
        for k, r in zip(names, received):
            self.received[k] = r
            self.halves[k] = _add_halves("grad_add_halves_" + k, self.full[k], self.c, self.shard, r)

    def small_grads(self, grads):
        self.small.update(grads)

    def _packed(self, names):
        return _pack([self.small[k].reshape(self.small_like[k].shape) for k in names])

    def side(self, host):
        if host in self.GATHER:
            return _gather_side([self.slots[k] for k in self.GATHER[host]], mid_late=host in self.GATHER_LONG)
        if host in self.SWAP:
            return _swap_side([self.full[k] for k in self.SWAP[host]])
        if host in self.SCATTER:
            return _scatter_side([self.halves[k] for k in self.SCATTER[host]])
        if host == self.SMALL_GATHER:
            self.packed_early = self._packed(SMALL[1:])
            return _small_gather_side(self.packed_early)
        return None

    def done(self, host, moved):
        if host in self.GATHER:
            self.slots.update(zip(self.GATHER[host], moved))
        elif host in self.SWAP:
            self._swapped(self.SWAP[host], moved)
        elif host in self.SCATTER:
            self.quarters.update(zip(self.SCATTER[host], moved))
        else:
            (self.gathered_early,) = moved

    def small_reduced(self):
        early = _sum_slots("small_sum", self.packed_early, self.gathered_early, self.me)
        late = _allreduce_small(self._packed(SMALL[:1]))
        return jnp.concatenate([late, early], axis=0)

    def reduced(self, group):
        parts = [_sum_received("grad_sum_" + k, self.full[k], self.c, self.shard, self.received[k], self.quarters[k]) for k in group]
        joined = _join_halves("grad_join_" + group[0], parts)
        return {k: j.reshape(2 * j.shape[1], j.shape[2]) for k, j in zip(group, joined)}


def kernel(x, p, norm_mix_g, w_in, ssm_lambda_re, ssm_lambda_im, ssm_log_step, ssm_b_re, ssm_b_im, ssm_c_re, ssm_c_im, ssm_d, ssm_glu_w, ssm_glu_b, sgu_ln_g, sgu_ln_b, sgu_w, sgu_b, out_norm_ssm_g, out_norm_sgu_g, w_out, norm_ffn_g, w_ffn_in, w_ffn_out, norm_ple_g, w_ple_gate, b_ple_gate, w_ple_proj, final_norm_g, loss_target, m_norm_mix_g, m_w_in, m_ssm_lambda_re, m_ssm_lambda_im, m_ssm_log_step, m_ssm_b_re, m_ssm_b_im, m_ssm_c_re, m_ssm_c_im, m_ssm_d, m_ssm_glu_w, m_ssm_glu_b, m_sgu_ln_g, m_sgu_ln_b, m_sgu_w, m_sgu_b, m_out_norm_ssm_g, m_out_norm_sgu_g, m_w_out, m_norm_ffn_g, m_w_ffn_in, m_w_ffn_out, m_norm_ple_g, m_w_ple_gate, m_b_ple_gate, m_w_ple_proj, m_final_norm_g, v_norm_mix_g, v_w_in, v_ssm_lambda_re, v_ssm_lambda_im, v_ssm_log_step, v_ssm_b_re, v_ssm_b_im, v_ssm_c_re, v_ssm_c_im, v_ssm_d, v_ssm_glu_w, v_ssm_glu_b, v_sgu_ln_g, v_sgu_ln_b, v_sgu_w, v_sgu_b, v_out_norm_ssm_g, v_out_norm_sgu_g, v_w_out, v_norm_ffn_g, v_w_ffn_in, v_w_ffn_out, v_norm_ple_g, v_w_ple_gate, v_b_ple_gate, v_w_ple_proj, v_final_norm_g):
    given = dict(locals())
    w = {k: given[k] for k in WEIGHTS}
    m = {k: given["m_" + k] for k in WEIGHTS}
    v = {k: given["v_" + k] for k in WEIGHTS}
    c = lax.axis_index("c")
    shard = 2 * lax.axis_index("x") + lax.axis_index("y")

    exch = _MeshExchange({k: w[k].reshape(w[k].shape[1:]) for k in LARGE}, {k: w[k] for k in SMALL}, c, shard, 2 * shard + c)
    unlayer = lambda a: a if a.ndim == 1 else a[0]
    sp = {k: unlayer(w[k]) for k in SMALL}
    n_tok, d_model = x.shape[1:]
    loss, grad_x = _local_grads(x.reshape(n_tok, d_model), p.reshape(n_tok, p.shape[-1]),
                                loss_target.reshape(n_tok, d_model), sp, exch)
    loss = lax.psum(loss, ("x", "y", "c"))

    grad_w, delta_w, new_m, new_v = {}, {}, {}, {}
    for group in exch.GROUPS:
        reduced = exch.reduced(group)
        for k in group:
            shape = w[k].shape
            two_d = lambda a: a.reshape(shape[1:])
            like = _sds(shape[1:], F32)
            d_k, m_k, v_k = _hosted(exch, _rowwise, "adamw_" + k, _adamw, [two_d(w[k]), reduced[k], two_d(m[k]), two_d(v[k])],
                                    [], [like, like, like])
            grad_w[k], delta_w[k], new_m[k], new_v[k] = (a.reshape(shape) for a in (reduced[k], d_k, m_k, v_k))

    packed_g = exch.small_reduced()
    like = _sds(packed_g.shape, F32)
    d_s, m_s, v_s = _rowwise("adamw_small", _adamw, [_pack([w[k] for k in SMALL]), packed_g, _pack([m[k] for k in SMALL]),
                                                     _pack([v[k] for k in SMALL])], [], [like, like, like])
    shapes = [w[k] for k in SMALL]
    for k, g_k, d_k, m_k, v_k in zip(SMALL, _unpack(packed_g, shapes), _unpack(d_s, shapes), _unpack(m_s, shapes), _unpack(v_s, shapes)):
        grad_w[k], delta_w[k], new_m[k], new_v[k] = g_k, d_k, m_k, v_k

    return (loss, grad_x.reshape(x.shape), *[grad_w[k] for k in WEIGHTS], *[delta_w[k] for k in WEIGHTS],
            *[new_m[k] for k in WEIGHTS], *[new_v[k] for k in WEIGHTS])
```

```python
import functools

import jax
import jax.numpy as jnp
from jax import lax
from jax.experimental import pallas as pl
from jax.experimental.pallas import tpu as pltpu

F32 = jnp.float32
BF16 = jnp.bfloat16

EPS = 1e-6
LAMBDA_RE_MAX = -1e-4
ADAM_LR = 0.001
ADAM_B1 = 0.9
ADAM_B2 = 0.999
ADAM_EPS = 1e-08
ADAM_WD = 0.01
ADAM_STEP = 10

N_CHIPS = 4
N_DEV = 8
SUBLANES = 8
LANES = 128
SSM_CH_BLOCK = 256
SCAN_LANES = 256
SCAN_BLOCKS = 4
GATHER_PARTS = 16
VMEM_LIMIT = 56 * 1024 * 1024

MESH = pl.DeviceIdType.MESH


def _pick(n, pref, mult):
    if n <= pref:
        return n
    t = (pref // mult) * mult
    while t >= mult:
        if n % t == 0:
            return t
        t -= mult
    return n


def _params(semantics):
    return pltpu.CompilerParams(dimension_semantics=semantics, vmem_limit_bytes=VMEM_LIMIT)


class _Cols:
    def __init__(self, arr, width, blk):
        self.arr, self.width, self.blk = arr, width, blk


def _sds(shape, dtype):
    return jax.ShapeDtypeStruct(tuple(shape), dtype)


ANY = pl.BlockSpec(memory_space=pl.ANY)


class _Side:
    def __init__(self, ins, out_shapes, n_sems, first, last, mid=None, aliases=None, mid_late=False):
        self.ins, self.out_shapes, self.n_sems = list(ins), list(out_shapes), n_sems
        self.first, self.mid, self.last, self.mid_late = first, mid, last, mid_late
        self.aliases = dict(aliases or {})


def _call(body, side, operands, *, name, grid, in_specs, out_specs, out_shape, compiler_params, scratch_shapes=()):
    if side is None:
        return pl.pallas_call(body, name=name, grid=grid, in_specs=in_specs, out_specs=out_specs, out_shape=out_shape,
                              scratch_shapes=list(scratch_shapes), compiler_params=compiler_params)(*operands)
    single = not isinstance(out_specs, (list, tuple))
    out_specs = [out_specs] if single else list(out_specs)
    out_shape = [out_shape] if single else list(out_shape)
    n_in, n_out, n_scr = len(in_specs), len(out_specs), len(scratch_shapes)
    n_sin, n_sout = len(side.ins), len(side.out_shapes)
    steps = 1
    for g in grid:
        steps *= g

    def hosted(*refs):
        ins, s_ins = refs[:n_in], refs[n_in:n_in + n_sin]
        at = n_in + n_sin
        outs, s_outs = refs[at:at + n_out], refs[at + n_out:at + n_out + n_sout]
        scratch = refs[at + n_out + n_sout:at + n_out + n_sout + n_scr]
        sems = refs[-2:]
        step = pl.program_id(0)
        for d in range(1, len(grid)):
            step = step * grid[d] + pl.program_id(d)

        @pl.when(step == 0)
        def _():
            side.first(s_ins, s_outs, *sems)

        if side.mid is not None:
            @pl.when(step == (steps - 1 if side.mid_late else (3 * steps) // 4))
            def _():
                side.mid(s_ins, s_outs, *sems)

        body(*ins, *outs, *scratch)

        @pl.when(step == steps - 1)
        def _():
            side.last(s_ins, s_outs, *sems)

    res = pl.pallas_call(
        hosted, name=name, grid=grid, in_specs=[*in_specs, *[ANY] * n_sin], out_specs=[*out_specs, *[ANY] * n_sout],
        out_shape=[*out_shape, *side.out_shapes], input_output_aliases={n_in + i: n_out + o for i, o in side.aliases.items()},
        scratch_shapes=[*scratch_shapes, pltpu.SemaphoreType.DMA((side.n_sems,)), pltpu.SemaphoreType.DMA((side.n_sems,))],
        compiler_params=compiler_params)(*operands, *side.ins)
    return (res[0] if single else list(res[:n_out])), list(res[n_out:])


def _rowwise(name, fn, rows, params, row_outs, acc_outs=(), tr=256, side=None):
    rows = [r if isinstance(r, _Cols) else _Cols(r, r.shape[1], 0) for r in rows]
    m = rows[0].arr.shape[0]
    tr = _pick(m, tr, 16)
    n_in = len(rows) + len(params)
    n_ro = len(row_outs)

    def body(*refs):
        vals = fn(*[r[...] for r in refs[:n_in]])
        if not isinstance(vals, (tuple, list)):
            vals = (vals,)
        outs = refs[n_in:]
        for r, v in zip(outs[:n_ro], vals[:n_ro]):
            r[...] = v.astype(r.dtype)
        first = pl.program_id(0) == 0
        for r, v in zip(outs[n_ro:], vals[n_ro:]):
            @pl.when(first)
            def _():
                r[...] = jnp.zeros(r.shape, r.dtype)
            r[...] += v.astype(r.dtype).reshape(r.shape)

    in_specs = [pl.BlockSpec((tr, r.width), lambda i, b=r.blk: (i, b)) for r in rows]
    in_specs += [pl.BlockSpec(p.shape, lambda i, nd=p.ndim: (0,) * nd) for p in params]
    out_specs = [pl.BlockSpec((tr, o.shape[1]), lambda i: (i, 0)) for o in row_outs]
    out_specs += [pl.BlockSpec(o.shape, lambda i, nd=len(o.shape): (0,) * nd) for o in acc_outs]
    return _call(body, side, [*[r.arr for r in rows], *params], name=name, grid=(m // tr,), in_specs=in_specs,
                 out_specs=out_specs, out_shape=[*row_outs, *acc_outs], compiler_params=_params(("arbitrary",)))


def _grid_order(swap):
    if not swap:
        return (lambda grid: grid), (lambda f: f)
    return (lambda grid: grid[::-1]), (lambda f: (lambda j, i: f(i, j)))


def _mm_nn(name, a, w, *, sharded=False, res=None, out_dtype=F32, tm=512, tn=512, w_resident=False, side=None):
    m, k = a.shape
    tm = _pick(m, tm, 16)
    order, ix = _grid_order(w_resident)
    if sharded:
        s, _, ns = w.shape
        n = s * ns
        tn = _pick(ns, tn, LANES)
        per = ns // tn
        w_spec = pl.BlockSpec((None, k, tn), ix(lambda i, j: (j // per, 0, j % per)))
    else:
        n = w.shape[1]
        tn = _pick(n, tn, LANES)
        w_spec = pl.BlockSpec((k, tn), ix(lambda i, j: (0, j)))

    def body(a_ref, w_ref, *rest):
        acc = jnp.dot(a_ref[...], w_ref[...], preferred_element_type=F32)
        if res is not None:
            acc = acc + rest[0][...]
        rest[-1][...] = acc.astype(out_dtype)

    in_specs = [pl.BlockSpec((tm, k), ix(lambda i, j: (i, 0))), w_spec]
    ops = [a, w]
    if res is not None:
        in_specs.append(pl.BlockSpec((tm, tn), ix(lambda i, j: (i, j))))
        ops.append(res)
    return _call(body, side, ops, name=name, grid=order((m // tm, n // tn)), in_specs=in_specs,
                 out_specs=pl.BlockSpec((tm, tn), ix(lambda i, j: (i, j))), out_shape=_sds((m, n), out_dtype),
                 compiler_params=_params(("arbitrary", "arbitrary")))


def _mm_nt(name, g, w, *, sharded=False, g_halves=False, out_dtype=F32, tm=512, tk=512, w_resident=False, side=None):
    m, n = g.shape[-2:]
    tm = _pick(m, tm, 16)
    order, ix = _grid_order(w_resident)
    dims = (((1,), (1,)), ((), ()))
    g_spec = pl.BlockSpec((2, tm, n), ix(lambda i, j: (0, i, 0))) if g_halves else pl.BlockSpec((tm, n), ix(lambda i, j: (i, 0)))
    if sharded:
        s, k, ns = w.shape
        tk = _pick(k, tk, LANES)
        w_spec = pl.BlockSpec((s, tk, ns), ix(lambda i, j: (0, j, 0)))

        def columns(g_ref, q):
            if not g_halves:
                return g_ref[:, q * ns:(q + 1) * ns]
            half, at = divmod(q, s // 2)
            return g_ref[half, :, at * ns:(at + 1) * ns]

        def body(g_ref, w_ref, o_ref):
            acc = lax.dot_general(columns(g_ref, 0), w_ref[0], dims, preferred_element_type=F32)
            for q in range(1, s):
                acc = acc + lax.dot_general(columns(g_ref, q), w_ref[q], dims, preferred_element_type=F32)
            o_ref[...] = acc.astype(out_dtype)
    else:
        k = w.shape[0]
        tk = _pick(k, tk, LANES)
        w_spec = pl.BlockSpec((tk, n), ix(lambda i, j: (j, 0)))

        def body(g_ref, w_ref, o_ref):
            o_ref[...] = lax.dot_general(g_ref[...], w_ref[...], dims, preferred_element_type=F32).astype(out_dtype)

    return _call(body, side, [g, w], name=name, grid=order((m // tm, k // tk)), in_specs=[g_spec, w_spec],
                 out_specs=pl.BlockSpec((tm, tk), ix(lambda i, j: (i, j))), out_shape=_sds((m, k), out_dtype),
                 compiler_params=_params(("arbitrary", "arbitrary")))


def _mm_tn(name, a, g, *, shards=0, g_halves=False, tk=512, tn=512, g_resident=False, side=None):
    m, k = a.shape
    n = 2 * g.shape[2] if g_halves else g.shape[1]
    tk = _pick(k, tk, LANES)
    order, ix = _grid_order(g_resident)
    dims = (((0,), (0,)), ((), ()))
    if shards:
        ns = n // shards
        tn = _pick(ns, tn, LANES)
        per = ns // tn
        out_spec = pl.BlockSpec((None, tk, tn), ix(lambda i, j: (j // per, i, j % per)))
        out_shape = _sds((shards, k, ns), F32)
    else:
        tn = _pick(n, tn, LANES)
        out_spec = pl.BlockSpec((tk, tn), ix(lambda i, j: (i, j)))
        out_shape = _sds((k, n), F32)

    def body(a_ref, g_ref, o_ref):
        o_ref[...] = lax.dot_general(a_ref[...], g_ref[...], dims, preferred_element_type=F32)

    if g_halves:
        per_half = n // 2 // tn
        g_spec = pl.BlockSpec((None, m, tn), ix(lambda i, j: (j // per_half, 0, j % per_half)))
    else:
        g_spec = pl.BlockSpec((m, tn), ix(lambda i, j: (0, j)))
    return _call(body, side, [a, g], name=name, grid=order((k // tk, n // tn)),
                 in_specs=[pl.BlockSpec((m, tk), ix(lambda i, j: (0, i))), g_spec],
                 out_specs=out_spec, out_shape=out_shape, compiler_params=_params(("arbitrary", "arbitrary")))


def _ffn_in_swiglu(name, a, w, *, tm=512, tn=1408, side=None):
    m, k = a.shape
    s, _, ns = w.shape
    f = s * ns // 2
    tm = _pick(m, tm, 16)
    tn = _pick(ns, tn, LANES)
    per = ns // tn
    order, ix = _grid_order(True)

    def body(a_ref, wg_ref, wu_ref, act_ref, gu_ref):
        x = a_ref[...]
        gate = jnp.dot(x, wg_ref[...], preferred_element_type=F32)
        up = jnp.dot(x, wu_ref[...], preferred_element_type=F32)
        act_ref[...] = _swiglu(gate, up).astype(BF16)
        gu_ref[0] = gate.astype(BF16)
        gu_ref[1] = up.astype(BF16)

    return _call(body, side, [a, w, w], name=name, grid=order((m // tm, f // tn)),
                 in_specs=[pl.BlockSpec((tm, k), ix(lambda i, j: (i, 0))),
                           pl.BlockSpec((None, k, tn), ix(lambda i, j: (j // per, 0, j % per))),
                           pl.BlockSpec((None, k, tn), ix(lambda i, j: (s // 2 + j // per, 0, j % per)))],
                 out_specs=[pl.BlockSpec((tm, tn), ix(lambda i, j: (i, j))), pl.BlockSpec((2, tm, tn), ix(lambda i, j: (0, i, j)))],
                 out_shape=[_sds((m, f), BF16), _sds((2, m, f), BF16)], compiler_params=_params(("arbitrary", "arbitrary")))


def _d_act_swiglu(name, g, w, gu, *, tm=1024, tk=512, side=None):
    m, n = g.shape
    f = w.shape[0]
    tm = _pick(m, tm, 16)
    tk = _pick(f, tk, LANES)
    dims = (((1,), (1,)), ((), ()))

    def body(g_ref, w_ref, gu_ref, o_ref):
        dact = lax.dot_general(g_ref[...], w_ref[...], dims, preferred_element_type=F32)
        _, vjp = jax.vjp(_swiglu, gu_ref[0].astype(F32), gu_ref[1].astype(F32))
        dgate, dup = vjp(dact)
        o_ref[0] = dgate.astype(BF16)
        o_ref[1] = dup.astype(BF16)

    return _call(body, side, [g, w, gu], name=name, grid=(m // tm, f // tk),
                 in_specs=[pl.BlockSpec((tm, n), lambda i, j: (i, 0)), pl.BlockSpec((tk, n), lambda i, j: (j, 0)),
                           pl.BlockSpec((2, tm, tk), lambda i, j: (0, i, j))],
                 out_specs=pl.BlockSpec((2, tm, tk), lambda i, j: (0, i, j)), out_shape=_sds((2, m, f), BF16),
                 compiler_params=_params(("arbitrary", "arbitrary")))


def _rms(x, g):
    r = lax.rsqrt(jnp.mean(x * x, axis=-1, keepdims=True) + EPS)
    return (x * r) * g


def _glu_out(y_pre, q, glu_b, g_norm):
    ya0 = jax.nn.gelu(y_pre)
    return _rms(ya0 * jax.nn.sigmoid(q + glu_b), g_norm)


def _sgu_rows(zu, zv, ln_g, ln_b, w_s, b_st, g_norm):
    heads, t, _ = w_s.shape
    hd = zu.shape[1] // heads
    uu = jax.nn.gelu(zu)
    vv = jax.nn.gelu(zv)
    mu = jnp.mean(vv, axis=-1, keepdims=True)
    xc = vv - mu
    r = lax.rsqrt(jnp.mean(xc * xc, axis=-1, keepdims=True) + EPS)
    vn = (xc * r) * ln_g + ln_b
    row = lax.broadcasted_iota(jnp.int32, (t, t), 0)
    col = lax.broadcasted_iota(jnp.int32, (t, t), 1)
    causal = row >= col
    chunks = []
    for n in range(zu.shape[0] // t):
        blocks = []
        for h in range(heads):
            wm = jnp.where(causal, w_s[h], jnp.zeros_like(w_s[h])).astype(BF16)
            vb = vn[n * t:(n + 1) * t, h * hd:(h + 1) * hd].astype(BF16)
            blocks.append(jnp.dot(wm, vb, preferred_element_type=F32) + b_st[:, h:h + 1])
        chunks.append(jnp.concatenate(blocks, axis=1))
    s = jnp.concatenate(chunks, axis=0) if len(chunks) > 1 else chunks[0]
    return _rms(uu * s, g_norm)


def _swiglu(gate, up):
    return jax.nn.silu(gate) * up


def _head_loss(x2, gpre, pp, b_g, g_final, target):
    gate = jax.nn.sigmoid(gpre + b_g)
    out = _rms(x2 + gate * pp, g_final)
    err = jnp.square(out - target)
    return 0.5 * jnp.sum(jnp.mean(err, axis=-1))


def _ssm_disc(lam_re, lam_im, log_step):
    lr = jnp.minimum(lam_re, LAMBDA_RE_MAX)
    li = lam_im
    dt = jnp.exp(log_step)
    mag = jnp.exp(lr * dt)
    ang = li * dt
    abar_re = mag * jnp.cos(ang)
    abar_im = mag * jnp.sin(ang)
    nr = abar_re - 1.0
    ni = abar_im
    den = lr * lr + li * li
    q_re = (nr * lr + ni * li) / den
    q_im = (ni * lr - nr * li) / den
    return abar_re, abar_im, q_re, q_im


def _ssm_bbar(q_re, q_im, b_re, b_im):
    return q_re * b_re - q_im * b_im, q_re * b_im + q_im * b_re


def _ssm_discretised(lam_re, lam_im, log_step, bt_re, bt_im):
    ar, ai, qr, qi = _ssm_disc(lam_re, lam_im, log_step)
    return (ar, ai, *_ssm_bbar(qr, qi, bt_re, bt_im))


def _adamw(w, g, m, v):
    m = ADAM_B1 * m + (1.0 - ADAM_B1) * g
    v = ADAM_B2 * v + (1.0 - ADAM_B2) * jnp.square(g)
    m_hat = m / (1.0 - ADAM_B1 ** ADAM_STEP)
    v_hat = v / (1.0 - ADAM_B2 ** ADAM_STEP)
    delta = -ADAM_LR * (m_hat / (jnp.sqrt(v_hat) + ADAM_EPS) + ADAM_WD * w)
    return delta, m, v


class _SsmDims:
    def __init__(self, groups, state, gch):
        self.g, self.p, self.h = groups, state, gch
        self.d = groups * gch
        self.cb = min(SSM_CH_BLOCK, self.d)
        self.gb = self.cb // gch
        self.ns = self.gb * state
        self.nb = self.d // self.cb


def _ssm_rows(sd, sp):
    gp = sd.g * sd.p
    log_step = jnp.broadcast_to(sp["ssm_log_step"][:, None], (sd.g, sd.p)).reshape(1, gp)
    bt = [sp[k].reshape(gp, sd.h).T for k in ("ssm_b_re", "ssm_b_im")]
    ct = [sp[k].transpose(1, 0, 2).reshape(sd.h, gp) for k in ("ssm_c_re", "ssm_c_im")]
    return (sp["ssm_lambda_re"].reshape(1, gp), sp["ssm_lambda_im"].reshape(1, gp), log_step, *bt, *ct)


def _block_mask(sd):
    row = lax.broadcasted_iota(jnp.int32, (sd.cb, sd.ns), 0) // sd.h
    col = lax.broadcasted_iota(jnp.int32, (sd.cb, sd.ns), 1) // sd.p
    return row == col


def _scan_consts(pr, pi_, reverse):
    if reverse:
        pi_ = [-v for v in pi_]
    shape = (SUBLANES, pr[0].shape[1])
    rows = lax.broadcasted_iota(jnp.int32, shape, 0)
    parts = []
    for d in (1, 2, 4):
        keep = (rows < SUBLANES - d) if reverse else (rows >= d)
        parts += [jnp.where(keep, jnp.broadcast_to(v[d - 1], shape), 0.0) for v in (pr, pi_)]
    order = range(SUBLANES - 1, -1, -1) if reverse else range(SUBLANES)
    parts += [jnp.concatenate([v[t] for t in order], axis=0) for v in (pr, pi_)]
    return jnp.concatenate(parts, axis=0)


def _ssm_operands(sd, rows):
    cb, ns, nb = sd.cb, sd.ns, sd.nb

    def body(lam_re, lam_im, log_step, bt_re, bt_im, ct_re, ct_im, wb_ref, wbt_ref, wc_ref, wct_ref, cst_f_ref, cst_r_ref):
        ar, ai, bbar_re, bbar_im = _ssm_discretised(lam_re[...], lam_im[...], log_step[...], bt_re[...], bt_im[...])
        pr, pi_ = [ar], [ai]
        for _ in range(SUBLANES - 1):
            pr, pi_ = pr + [pr[-1] * ar - pi_[-1] * ai], pi_ + [pr[-1] * ai + pi_[-1] * ar]
        mask = _block_mask(sd)
        spread = lambda src: jnp.where(mask, jnp.concatenate([src] * sd.gb, axis=0), 0.0)
        for j in range(nb):
            at = slice(j * ns, (j + 1) * ns)
            w = jnp.concatenate([spread(bbar_re[:, at]), spread(bbar_im[:, at])], axis=1)
            v = jnp.concatenate([spread(ct_re[:, at]), -spread(ct_im[:, at])], axis=1)
            wb_ref[j] = w.astype(BF16)
            wbt_ref[j] = w.T.astype(BF16)
            wct_ref[j] = v.astype(BF16)
            wc_ref[j] = v.T.astype(BF16)
            pj, qj = [u[:, at] for u in pr], [u[:, at] for u in pi_]
            cst_f_ref[j] = _scan_consts(pj, qj, False)
            cst_r_ref[j] = _scan_consts(pj, qj, True)

    wide, tall = _sds((nb, cb, 2 * ns), BF16), _sds((nb, 2 * ns, cb), BF16)
    cst = _sds((nb, 8 * SUBLANES, ns), F32)
    vm = pl.BlockSpec(memory_space=pltpu.VMEM)
    return pl.pallas_call(body, name="ssm_operands", in_specs=[vm] * 7, out_specs=[vm] * 6,
                          out_shape=[wide, tall, tall, wide, cst, cst],
                          compiler_params=pltpu.CompilerParams(vmem_limit_bytes=VMEM_LIMIT))(*rows)


def _ssm_param_grads(sd, rows, dwb, dwc, da):
    ns, nb, gp = sd.ns, sd.nb, sd.g * sd.p

    def body(lam_re, lam_im, log_step, bt_re, bt_im, dwb_v, dwc_v, da_v, *outs):
        mask = _block_mask(sd)

        def fold(dense):
            kept = jnp.where(mask, dense, 0.0)
            acc = kept[0:sd.h]
            for gl in range(1, sd.gb):
                acc = acc + kept[gl * sd.h:(gl + 1) * sd.h]
            return acc

        lanes = lambda parts: jnp.concatenate(parts, axis=1) if len(parts) > 1 else parts[0]
        dbbar_re = lanes([fold(dwb_v[j][:, :ns]) for j in range(nb)])
        dbbar_im = lanes([fold(dwb_v[j][:, ns:]) for j in range(nb)])
        dwct = [dwc_v[j] for j in range(nb)]
        d_ct_re = lanes([fold(t[:, :ns]) for t in dwct])
        d_ct_im = -lanes([fold(t[:, ns:]) for t in dwct])
        dabar_re = lanes([da_v[j][0:1, :ns] for j in range(nb)])
        dabar_im = lanes([da_v[j][0:1, ns:] for j in range(nb)])
        _, vjp = jax.vjp(_ssm_discretised, lam_re[...], lam_im[...], log_step[...], bt_re[...], bt_im[...])
        d_lr, d_li, d_ls, d_bt_re, d_bt_im = vjp((dabar_re, dabar_im, dbbar_re, dbbar_im))
        group = (lax.broadcasted_iota(jnp.int32, (gp, sd.g), 0) // sd.p == lax.broadcasted_iota(jnp.int32, (gp, sd.g), 1))
        d_log_step = jnp.dot(d_ls, group.astype(F32), precision=lax.Precision.HIGHEST, preferred_element_type=F32)
        for ref, val in zip(outs, (d_lr, d_li, d_log_step, d_bt_re, d_bt_im, d_ct_re, d_ct_im)):
            ref[...] = val

    row, mat = _sds((1, gp), F32), _sds((sd.h, gp), F32)
    vm = pl.BlockSpec(memory_space=pltpu.VMEM)
    return pl.pallas_call(body, name="ssm_param_grads", in_specs=[vm] * 8, out_specs=[vm] * 7,
                          out_shape=[row, row, _sds((1, sd.g), F32), mat, mat, mat, mat],
                          compiler_params=pltpu.CompilerParams(vmem_limit_bytes=VMEM_LIMIT))(*rows[:5], dwb, dwc, da)


def _block_scan(s_ref, cst_ref, carry_ref, sd, rows, reverse):
    ns = sd.ns
    nblk = rows // SUBLANES
    w = min(SCAN_LANES, ns)
    for c0 in range(0, ns, w):
        re_l, im_l = slice(c0, c0 + w), slice(ns + c0, ns + c0 + w)
        cst = [cst_ref[k * SUBLANES:(k + 1) * SUBLANES, c0:c0 + w] for k in range(8)]

        def step(k, carry, re_l=re_l, im_l=im_l, cst=cst):
            local = []
            for b in range(SCAN_BLOCKS):
                blk = SCAN_BLOCKS * k + b
                blk = (nblk - 1 - blk) if reverse else blk
                r0 = pl.multiple_of(blk * SUBLANES, SUBLANES)
                xr = s_ref[pl.ds(r0, SUBLANES), re_l]
                xi = s_ref[pl.ds(r0, SUBLANES), im_l]
                for n, d in enumerate((1, 2, 4)):
                    ar, ai = cst[2 * n], cst[2 * n + 1]
                    shift = (SUBLANES - d) if reverse else d
                    sr = pltpu.roll(xr, shift, 0)
                    si = pltpu.roll(xi, shift, 0)
                    xr, xi = xr + ar * sr - ai * si, xi + ar * si + ai * sr
                local.append((r0, xr, xi))
            cr, ci = carry
            edge = slice(0, 1) if reverse else slice(SUBLANES - 1, SUBLANES)
            for r0, xr, xi in local:
                br = jnp.broadcast_to(cr, xr.shape)
                bi = jnp.broadcast_to(ci, xi.shape)
                xr, xi = xr + cst[6] * br - cst[7] * bi, xi + cst[6] * bi + cst[7] * br
                s_ref[pl.ds(r0, SUBLANES), re_l] = xr
                s_ref[pl.ds(r0, SUBLANES), im_l] = xi
                cr, ci = xr[edge, :], xi[edge, :]
            return cr, ci

        cr, ci = lax.fori_loop(0, nblk // SCAN_BLOCKS, step, (carry_ref[0:1, re_l], carry_ref[0:1, im_l]))
        carry_ref[0:1, re_l] = cr
        carry_ref[0:1, im_l] = ci


def _ssm_fwd(name, sd, z, wb, wc, cst, d_row, tt=512, side=None):
    n_tok = z.shape[0]
    tt = _pick(n_tok, tt, 16)
    cb, ns2 = sd.cb, 2 * sd.ns

    def body(z_ref, wb_ref, wc_ref, cst_ref, d_ref, y_ref, s_ref, a0_ref, carry_ref):
        @pl.when(pl.program_id(1) == 0)
        def _():
            carry_ref[...] = jnp.zeros(carry_ref.shape, F32)
        u = z_ref[...]
        s_ref[...] = jnp.dot(u.astype(BF16), wb_ref[...], preferred_element_type=F32)
        _block_scan(s_ref, cst_ref, carry_ref, sd, tt, reverse=False)
        y = jnp.dot(s_ref[...].astype(BF16), wc_ref[...], preferred_element_type=F32) + d_ref[...] * u
        y_ref[...] = y
        a0_ref[...] = jax.nn.gelu(y).astype(BF16)

    return _call(
        body, side, [z, wb, wc, cst, d_row], name=name, grid=(sd.nb, n_tok // tt),
        in_specs=[pl.BlockSpec((tt, cb), lambda j, i: (i, j)),
                  pl.BlockSpec((None, cb, ns2), lambda j, i: (j, 0, 0)),
                  pl.BlockSpec((None, ns2, cb), lambda j, i: (j, 0, 0)),
                  pl.BlockSpec((None, 8 * SUBLANES, sd.ns), lambda j, i: (j, 0, 0)),
                  pl.BlockSpec((1, cb), lambda j, i: (0, j))],
        out_specs=[pl.BlockSpec((tt, cb), lambda j, i: (i, j)), pl.BlockSpec((tt, ns2), lambda j, i: (i, j)),
                   pl.BlockSpec((tt, cb), lambda j, i: (i, j))],
        out_shape=[_sds((n_tok, sd.d), F32), _sds((n_tok, sd.nb * ns2), F32), _sds((n_tok, sd.d), BF16)],
        scratch_shapes=[pltpu.VMEM((SUBLANES, ns2), F32)],
        compiler_params=_params(("arbitrary", "arbitrary")))


def _ssm_bwd(name, sd, y_pre, dy_direct, dya0, z, states, wct, wbt, cst_rev, d_row, tt=512, side=None):
    n_tok = z.shape[0]
    tt = _pick(n_tok, tt, 16)
    nt = n_tok // tt
    cb, ns, ns2 = sd.cb, sd.ns, 2 * sd.ns
    blocks_per_tile = tt // SUBLANES
    tn_dims = (((0,), (0,)), ((), ()))

    def body(y_ref, dyd_ref, dya0_ref, z_ref, s_ref, sp_ref, wct_ref, wbt_ref, cst_ref, d_ref,
             du_ref, dwb_ref, dwc_ref, da_ref, dd_ref, lam_ref, carry_ref):
        i = pl.program_id(1)

        @pl.when(i == 0)
        def _():
            carry_ref[...] = jnp.zeros(carry_ref.shape, F32)
            dwb_ref[...] = jnp.zeros(dwb_ref.shape, F32)
            dwc_ref[...] = jnp.zeros(dwc_ref.shape, F32)
            da_ref[...] = jnp.zeros(da_ref.shape, F32)
            dd_ref[...] = jnp.zeros(dd_ref.shape, F32)

        _, gelu_vjp = jax.vjp(jax.nn.gelu, y_ref[...])
        dy_t = dyd_ref[...] + gelu_vjp(dya0_ref[...].astype(F32))[0]
        u = z_ref[...]
        dy16 = dy_t.astype(BF16)
        lam_ref[...] = jnp.dot(dy16, wct_ref[...], preferred_element_type=F32)
        _block_scan(lam_ref, cst_ref, carry_ref, sd, tt, reverse=True)
        lam = lam_ref[...]
        lam16 = lam.astype(BF16)
        du_ref[...] = (jnp.dot(lam16, wbt_ref[...], preferred_element_type=F32) + d_ref[...] * dy_t).astype(BF16)
        dd_ref[0:1, :] += jnp.sum(dy_t * u, axis=0, keepdims=True)
        dwb_ref[...] += lax.dot_general(u.astype(BF16), lam16, tn_dims, preferred_element_type=F32)
        s = s_ref[...]
        dwc_ref[...] += lax.dot_general(dy16, s.astype(BF16), tn_dims, preferred_element_type=F32)
        before = jnp.where(i == nt - 1, 0.0, 1.0) * sp_ref[SUBLANES - 1:SUBLANES, :]
        first_row = lax.broadcasted_iota(jnp.int32, s.shape, 0) == 0
        prev = jnp.where(first_row, jnp.broadcast_to(before, s.shape), pltpu.roll(s, 1, 0))
        lr, li = lam[:, :ns], lam[:, ns:]
        pr, pi_ = prev[:, :ns], prev[:, ns:]
        da_ref[0:1, 0:ns] += jnp.sum(lr * pr + li * pi_, axis=0, keepdims=True)
        da_ref[0:1, ns:ns2] += jnp.sum(li * pr - lr * pi_, axis=0, keepdims=True)

    rev = lambda i: nt - 1 - i
    return _call(
        body, side, [y_pre, dy_direct, dya0, z, states, states, wct, wbt, cst_rev, d_row], name=name, grid=(sd.nb, nt),
        in_specs=[pl.BlockSpec((tt, cb), lambda j, i: (rev(i), j)),
                  pl.BlockSpec((tt, cb), lambda j, i: (rev(i), j)),
                  pl.BlockSpec((tt, cb), lambda j, i: (rev(i), j)),
                  pl.BlockSpec((tt, cb), lambda j, i: (rev(i), j)),
                  pl.BlockSpec((tt, ns2), lambda j, i: (rev(i), j)),
                  pl.BlockSpec((SUBLANES, ns2), lambda j, i: (jnp.maximum(rev(i) * blocks_per_tile - 1, 0), j)),
                  pl.BlockSpec((None, cb, ns2), lambda j, i: (j, 0, 0)),
                  pl.BlockSpec((None, ns2, cb), lambda j, i: (j, 0, 0)),
                  pl.BlockSpec((None, 8 * SUBLANES, ns), lambda j, i: (j, 0, 0)),
                  pl.BlockSpec((1, cb), lambda j, i: (0, j))],
        out_specs=[pl.BlockSpec((tt, cb), lambda j, i: (rev(i), j)),
                   pl.BlockSpec((None, cb, ns2), lambda j, i: (j, 0, 0)),
                   pl.BlockSpec((None, cb, ns2), lambda j, i: (j, 0, 0)),
                   pl.BlockSpec((None, SUBLANES, ns2), lambda j, i: (j, 0, 0)),
                   pl.BlockSpec((None, SUBLANES, cb), lambda j, i: (j, 0, 0))],
        out_shape=[_sds((n_tok, sd.d), BF16), _sds((sd.nb, cb, ns2), F32), _sds((sd.nb, cb, ns2), F32),
                   _sds((sd.nb, SUBLANES, ns2), F32), _sds((sd.nb, SUBLANES, cb), F32)],
        scratch_shapes=[pltpu.VMEM((tt, ns2), F32), pltpu.VMEM((SUBLANES, ns2), F32)],
        compiler_params=_params(("arbitrary", "arbitrary")))


def _hosted(exch, fn, name, *args, **kw):
    side = exch.side(name)
    if side is None:
        return fn(name, *args, **kw)
    out, moved = fn(name, *args, side=side, **kw)
    exch.done(name, moved)
    return out


def _local_grads(x, p, target, sp, exch):
    n_tok, d_model = x.shape
    d_ssm = sp["ssm_d"].shape[0] * sp["ssm_d"].shape[1]
    d_sgu = sp["sgu_ln_g"].shape[-1]
    sd = _SsmDims(sp["ssm_b_re"].shape[0], sp["ssm_b_re"].shape[1], sp["ssm_b_re"].shape[2])
    heads, chunk, _ = sp["sgu_w"].shape
    row = lambda v: v.reshape(1, -1)
    tok = lambda w, dt=F32: _sds((n_tok, w), dt)
    acc = lambda w: _sds((1, w), F32)

    g_mix = row(sp["norm_mix_g"])
    (h1,) = _hosted(exch, _rowwise, "norm_mix", lambda a, g: _rms(a, g), [x], [g_mix], [tok(d_model, BF16)])
    z = _hosted(exch, _mm_nn, "proj_in", h1, exch.weight("w_in"), sharded=True, tn=768)

    ssm_rows = _ssm_rows(sd, sp)
    wb, wbt, wc, wct, cst_fwd, cst_rev = _ssm_operands(sd, ssm_rows)
    d_row = row(sp["ssm_d"])
    y_pre, states, ya0_16 = _hosted(exch, _ssm_fwd, "ssm_fwd", sd, z, wb, wc, cst_fwd, d_row)
    q = _mm_nn("ssm_glu", ya0_16, exch.weight("ssm_glu_w"), tm=1024)
    glu_b, g_ossm = row(sp["ssm_glu_b"]), row(sp["out_norm_ssm_g"])
    (ya_n,) = _rowwise("ssm_glu_out", _glu_out, [y_pre, q], [glu_b, g_ossm], [tok(d_ssm, BF16)])

    assert d_ssm == d_sgu
    zu, zv = _Cols(z, d_sgu, 1), _Cols(z, d_sgu, 2)
    ln_g, ln_b, g_osgu = row(sp["sgu_ln_g"]), row(sp["sgu_ln_b"]), row(sp["out_norm_sgu_g"])
    b_st = sp["sgu_b"].T
    sgu_tr = 2 * chunk

    def sgu_joined(ya_t, zu_t, zv_t, *params):
        return jnp.concatenate([ya_t, _sgu_rows(zu_t, zv_t, *params).astype(BF16)], axis=1)

    (ycat,) = _rowwise("sgu", sgu_joined, [ya_n, zu, zv], [ln_g, ln_b, sp["sgu_w"], b_st, g_osgu],
                       [tok(d_ssm + d_sgu, BF16)], tr=sgu_tr)
    x1 = _hosted(exch, _mm_nn, "proj_out", ycat, exch.weight("w_out"), res=x, tm=1024)

    g_ffn = row(sp["norm_ffn_g"])
    (h2,) = _rowwise("norm_ffn", lambda a, g: _rms(a, g), [x1], [g_ffn], [tok(d_model, BF16)])
    act, gu16 = _hosted(exch, _ffn_in_swiglu, "ffn_in", h2, exch.weight("w_ffn_in"))
    x2 = _mm_nn("ffn_out", act, exch.weight("w_ffn_out"), res=x1)

    g_ple = row(sp["norm_ple_g"])
    (h3,) = _rowwise("norm_ple", lambda a, g: _rms(a, g), [x2], [g_ple], [tok(d_model, BF16)])
    gpre = _mm_nn("ple_gate", h3, exch.weight("w_ple_gate"), tm=1024)
    (p16,) = _rowwise("ple_cast", lambda a: a, [p], [], [tok(p.shape[1], BF16)])
    pp = _mm_nn("ple_proj", p16, exch.weight("w_ple_proj"), sharded=True, tm=1024)

    b_g, g_fin = row(sp["b_ple_gate"]), row(sp["final_norm_g"])

    def head(x2_t, gpre_t, pp_t, tgt_t, b_g_v, g_fin_v):
        loss, grads = jax.value_and_grad(_head_loss, argnums=(0, 1, 2, 3, 4))(x2_t, gpre_t, pp_t, b_g_v, g_fin_v, tgt_t)
        dx2, dgpre, dpp, db, dg = grads
        return dx2, dgpre.astype(BF16), dpp.astype(BF16), jnp.full((1, LANES), loss, F32), db, dg

    dx2_head, dgpre16, dpp16, loss_row, d_b_g, d_g_fin = _rowwise(
        "head", head, [x2, gpre, pp, target], [b_g, g_fin],
        [tok(d_model), tok(d_model, BF16), tok(d_model, BF16)], [acc(LANES), acc(d_model), acc(d_model)])
    loss = loss_row[0, 0]

    exch.grad("w_ple_proj", _mm_tn("d_ple_proj", p16, dpp16, shards=N_CHIPS, tk=256))
    exch.grad("w_ple_gate", _mm_tn("d_ple_gate", h3, dgpre16))
    dh3 = _mm_nt("d_h3", dgpre16, exch.weight("w_ple_gate"), out_dtype=BF16, tm=1024)

    def norm_bwd(x_t, dres_t, dh_t, g_v):
        _, vjp = jax.vjp(_rms, x_t, g_v)
        dx, dg = vjp(dh_t.astype(F32))
        dx = dres_t + dx
        return dx, dx.astype(BF16), dg

    dx2, dx2_16, d_g_ple = _rowwise("d_norm_ple", norm_bwd, [x2, dx2_head, dh3], [g_ple],
                                    [tok(d_model), tok(d_model, BF16)], [acc(d_model)])
    exch.grad("w_ffn_out", _mm_tn("d_ffn_out", act, dx2_16))
    dgu16 = _hosted(exch, _d_act_swiglu, "d_act", dx2_16, exch.weight("w_ffn_out"), gu16)
    exch.grad("w_ffn_in", _hosted(exch, _mm_tn, "d_ffn_in", h2, dgu16, shards=N_CHIPS, g_halves=True, tn=1408, g_resident=True))
    dh2 = _hosted(exch, _mm_nt, "d_h2", dgu16, exch.weight("w_ffn_in"), sharded=True, g_halves=True, out_dtype=BF16, tm=256, w_resident=True)
    dx1, dx1_16, d_g_ffn = _rowwise("d_norm_ffn", norm_bwd, [x1, dx2, dh2], [g_ffn],
                                    [tok(d_model), tok(d_model, BF16)], [acc(d_model)])
    exch.grad("w_out", _mm_tn("d_proj_out", ycat, dx1_16))
    dycat = _mm_nt("d_ycat", dx1_16, exch.weight("w_out"), out_dtype=BF16, tm=1024)

    def glu_out_bwd(y_pre_t, q_t, dy_t, glu_b_v, g_v):
        _, vjp = jax.vjp(_glu_out, y_pre_t, q_t, glu_b_v, g_v)
        dy_pre, dq, db, dg = vjp(dy_t.astype(F32))
        return dy_pre, dq.astype(BF16), db, dg

    dy_pre_a, dq16, d_glu_b, d_g_ossm = _rowwise(
        "d_ssm_glu_out", glu_out_bwd, [y_pre, q, _Cols(dycat, d_ssm, 0)], [glu_b, g_ossm],
        [tok(d_ssm), tok(d_ssm, BF16)], [acc(d_ssm), acc(d_ssm)])
    exch.grad("ssm_glu_w", _mm_tn("d_ssm_glu", ya0_16, dq16))
    dya0 = _hosted(exch, _mm_nt, "d_ya0", dq16, exch.weight("ssm_glu_w"), out_dtype=BF16, tm=1024)

    dz_ssm16, dwb, dwc, da, dd = _hosted(exch, _ssm_bwd, "ssm_bwd", sd, y_pre, dy_pre_a, dya0, z, states, wct, wbt,
                                         cst_rev, d_row)

    def sgu_bwd(dz_ssm_t, zu_t, zv_t, dy_t, ln_g_v, ln_b_v, w_v, b_v, g_v):
        _, vjp = jax.vjp(_sgu_rows, zu_t, zv_t, ln_g_v, ln_b_v, w_v, b_v, g_v)
        dzu, dzv, dlg, dlb, dw, db, dg = vjp(dy_t.astype(F32))
        return jnp.concatenate([dz_ssm_t, dzu.astype(BF16), dzv.astype(BF16)], axis=1), dlg, dlb, dw, db, dg

    dz16, d_ln_g, d_ln_b, d_sgu_w, d_b_st, d_g_osgu = _hosted(
        exch, _rowwise, "d_sgu", sgu_bwd, [dz_ssm16, zu, zv, _Cols(dycat, d_sgu, 1)], [ln_g, ln_b, sp["sgu_w"], b_st, g_osgu],
        [tok(d_ssm + 2 * d_sgu, BF16)],
        [acc(d_sgu), acc(d_sgu), _sds(sp["sgu_w"].shape, F32), _sds(b_st.shape, F32), acc(d_sgu)], tr=sgu_tr)

    d_lam_re, d_lam_im, d_log_step, d_bt_re, d_bt_im, d_ct_re, d_ct_im = _ssm_param_grads(sd, ssm_rows, dwb, dwc, da)
    d_b_re, d_b_im = d_bt_re.T, d_bt_im.T
    d_c_re, d_c_im = (t.reshape(sd.h, sd.g, sd.p).transpose(1, 0, 2) for t in (d_ct_re, d_ct_im))
    d_ssm_d = dd[:, 0, :].reshape(sd.g, sd.h)

    exch.small_grads({
        "ssm_lambda_re": d_lam_re, "ssm_lambda_im": d_lam_im, "ssm_log_step": d_log_step,
        "ssm_b_re": d_b_re, "ssm_b_im": d_b_im, "ssm_c_re": d_c_re, "ssm_c_im": d_c_im, "ssm_d": d_ssm_d,
        "ssm_glu_b": d_glu_b, "sgu_ln_g": d_ln_g, "sgu_ln_b": d_ln_b, "sgu_w": d_sgu_w, "sgu_b": d_b_st.T,
        "out_norm_ssm_g": d_g_ossm, "out_norm_sgu_g": d_g_osgu, "norm_ffn_g": d_g_ffn, "norm_ple_g": d_g_ple,
        "b_ple_gate": d_b_g, "final_norm_g": d_g_fin,
    })

    exch.grad("w_in", _hosted(exch, _mm_tn, "d_proj_in", h1, dz16, shards=N_CHIPS, tn=768))
    dh1 = _hosted(exch, _mm_nt, "d_h1", dz16, exch.weight("w_in"), sharded=True, out_dtype=BF16, tm=1024)

    def norm_in_bwd(x_t, dres_t, dh_t, g_v):
        _, vjp = jax.vjp(_rms, x_t, g_v)
        dx, dg = vjp(dh_t.astype(F32))
        return dres_t + dx, dg

    grad_x, d_g_mix = _hosted(exch, _rowwise, "d_norm_mix", norm_in_bwd, [x, dx1, dh1], [g_mix], [tok(d_model)], [acc(d_model)])
    exch.small_grads({"norm_mix_g": d_g_mix})
    return loss, grad_x


def _place():
    x, y, c = lax.axis_index("x"), lax.axis_index("y"), lax.axis_index("c")
    chips = [(1 - x, y), (x, 1 - y), (1 - x, 1 - y)]
    return x, y, c, chips


def _cast_into_slot(name, w2d, shard, tr=256):
    rows, cols = w2d.shape
    rh = rows // 2
    tr = _pick(rh, tr, 16)
    per = rh // tr

    def body(s_ref, a_ref, o_ref):
        o_ref[...] = a_ref[...].astype(BF16)

    grid_spec = pltpu.PrefetchScalarGridSpec(
        num_scalar_prefetch=1, grid=(2, per),
        in_specs=[pl.BlockSpec((tr, cols), lambda h, i, s_ref: (h * per + i, 0))],
        out_specs=pl.BlockSpec((None, None, tr, cols), lambda h, i, s_ref: (s_ref[0], h, i, 0)))
    return pl.pallas_call(body, name=name, grid_spec=grid_spec, out_shape=_sds((N_CHIPS, 2, rh, cols), BF16),
                          compiler_params=_params(("arbitrary", "arbitrary")))(shard.reshape(1).astype(jnp.int32), w2d)


def _exchange_alone(name, side):
    n_in, n_out = len(side.ins), len(side.out_shapes)

    def body(*refs):
        ins, outs, sems = refs[:n_in], refs[n_in:n_in + n_out], refs[n_in + n_out:]
        side.first(ins, outs, *sems)
        if side.mid is not None:
            side.mid(ins, outs, *sems)
        side.last(ins, outs, *sems)

    return pl.pallas_call(
        body, name=name, in_specs=[ANY] * n_in, out_specs=[ANY] * n_out, out_shape=side.out_shapes,
        input_output_aliases=side.aliases,
        scratch_shapes=[pltpu.SemaphoreType.DMA((side.n_sems,)), pltpu.SemaphoreType.DMA((side.n_sems,))],
    )(*side.ins)


def _gather_side(slots, parts=None, mid_late=False):
    n = len(slots)
    parts = parts or [(0, GATHER_PARTS)] * n

    def copies(kind, outs, send_sems, recv_sems):
        x, y, c, chips = _place()

        def remote(k, w, shard, half, to):
            unit = outs[w].shape[2] // GATHER_PARTS
            lo, hi = parts[w]
            ref = outs[w].at[shard, half, pl.ds(lo * unit, (hi - lo) * unit), :]
            return pltpu.make_async_remote_copy(src_ref=ref, dst_ref=ref, send_sem=send_sems.at[k], recv_sem=recv_sems.at[k],
                                                device_id=to, device_id_type=MESH)

        pairs = [(w, j, 2 * cx + cy, (cx, cy)) for w in range(n) for j, (cx, cy) in enumerate(chips)]
        if kind == "sends":
            return [remote(3 * w + j, w, 2 * x + y, c, (*chip, c)) for w, j, _, chip in pairs]
        if kind == "arrivals":
            return [remote(3 * w + j, w, s, c, (x, y, c)) for w, j, s, _ in pairs]
        if kind == "passed":
            return [remote(3 * n + 3 * w + j, w, s, c, (x, y, 1 - c)) for w, j, s, _ in pairs]
        return [remote(3 * n + 3 * w + j, w, s, 1 - c, (x, y, c)) for w, j, s, _ in pairs]

    def first(ins, outs, *sems):
        for cp in copies("sends", outs, *sems):
            cp.start()

    def mid(ins, outs, *sems):
        for arrived, onward in zip(copies("arrivals", outs, *sems), copies("passed", outs, *sems)):
            arrived.wait_recv()
            onward.start()

    def last(ins, outs, *sems):
        for cp in copies("from_sibling", outs, *sems):
            cp.wait_recv()
        for cp in copies("sends", outs, *sems) + copies("passed", outs, *sems):
            cp.wait_send()

    return _Side(slots, [_sds(s.shape, s.dtype) for s in slots], 6 * n, first, last, mid=mid, aliases={w: w for w in range(n)},
                 mid_late=mid_late)


def _swap_side(grads):
    n = len(grads)

    def copies(ins, outs, send_sems, recv_sems):
        x, y, c, _ = _place()
        return [pltpu.make_async_remote_copy(src_ref=ins[w].at[:, 1 - c], dst_ref=outs[w], send_sem=send_sems.at[w],
                                             recv_sem=recv_sems.at[w], device_id=(x, y, 1 - c), device_id_type=MESH)
                for w in range(n)]

    def first(*refs):
        for cp in copies(*refs):
            cp.start()

    def last(*refs):
        for cp in copies(*refs):
            cp.wait()

    return _Side(grads, [_sds((g.shape[0], *g.shape[2:]), g.dtype) for g in grads], n, first, last)


def _scatter_side(halves):
    n = len(halves)

    def copies(ins, outs, send_sems, recv_sems):
        x, y, c, chips = _place()
        return [pltpu.make_async_remote_copy(
            src_ref=ins[w].at[2 * cx + cy], dst_ref=outs[w].at[j], send_sem=send_sems.at[3 * w + j],
            recv_sem=recv_sems.at[3 * w + j], device_id=(cx, cy, c), device_id_type=MESH)
            for w in range(n) for j, (cx, cy) in enumerate(chips)]

    def first(*refs):
        for cp in copies(*refs):
            cp.start()

    def last(*refs):
        for cp in copies(*refs):
            cp.wait()

    return _Side(halves, [_sds((3, *h.shape[1:]), h.dtype) for h in halves], 3 * n, first, last)


def _join_halves(name, slots):
    n = len(slots)

    def body(*refs):
        outs = refs[n:2 * n]
        send_sems, recv_sems = refs[2 * n:]
        x, y, c, _ = _place()

        def copy(w, half, to):
            return pltpu.make_async_remote_copy(src_ref=outs[w].at[half], dst_ref=outs[w].at[half], send_sem=send_sems.at[w],
                                                recv_sem=recv_sems.at[w], device_id=to, device_id_type=MESH)

        copies = [copy(w, c, (x, y, 1 - c)) for w in range(n)]
        for cp in copies:
            cp.start()
        for w in range(n):
            copy(w, 1 - c, (x, y, c)).wait_recv()
        for cp in copies:
            cp.wait_send()

    return pl.pallas_call(
        body, name=name, in_specs=[ANY] * n, out_specs=[ANY] * n,
        out_shape=[_sds(s.shape, s.dtype) for s in slots], input_output_aliases={w: w for w in range(n)},
        scratch_shapes=[pltpu.SemaphoreType.DMA((n,)), pltpu.SemaphoreType.DMA((n,))],
    )(*slots)


def _allreduce_small(block, tr=256):
    rows, lanes = block.shape
    tr = _pick(rows, tr, SUBLANES)

    def body(x_ref, o_ref, buf, send_sems, recv_sems):
        x, y, c, chips = _place()
        me, sibling = (x, y, c), (x, y, 1 - c)

        def slot(px, py, pc):
            return buf.at[4 * px + 2 * py + pc]

        def copy(k, block_of, to):
            return pltpu.make_async_remote_copy(src_ref=slot(*block_of), dst_ref=slot(*block_of), send_sem=send_sems.at[k],
                                                recv_sem=recv_sems.at[k], device_id=to, device_id_type=MESH)

        slot(*me)[...] = x_ref[...]
        first = [copy(0, me, sibling)] + [copy(1 + j, me, (*chip, c)) for j, chip in enumerate(chips)]
        for cp in first:
            cp.start()
        passed = [copy(4 + j, (*chip, c), sibling) for j, chip in enumerate(chips)]
        for j, chip in enumerate(chips):
            copy(1 + j, (*chip, c), me).wait_recv()
            passed[j].start()
        copy(0, sibling, me).wait_recv()
        for j, chip in enumerate(chips):
            copy(4 + j, (*chip, 1 - c), me).wait_recv()
        for cp in first + passed:
            cp.wait_send()
        for r0 in range(0, rows, tr):
            acc = buf[0, r0:r0 + tr, :]
            for k in range(1, N_DEV):
                acc = acc + buf[k, r0:r0 + tr, :]
            o_ref[r0:r0 + tr, :] = acc

    vm = pl.BlockSpec(memory_space=pltpu.VMEM)
    return pl.pallas_call(
        body, name="allreduce_small", in_specs=[vm], out_specs=vm, out_shape=_sds((rows, lanes), block.dtype),
        scratch_shapes=[pltpu.VMEM((N_DEV, rows, lanes), block.dtype), pltpu.SemaphoreType.DMA((7,)), pltpu.SemaphoreType.DMA((7,))],
        compiler_params=pltpu.CompilerParams(vmem_limit_bytes=VMEM_LIMIT),
    )(block)


def _small_gather_side(block):
    def copy(kind, j, ins, outs, send_sems, recv_sems):
        x, y, c, chips = _place()
        chip = chips[j] if j is not None else None
        slot = lambda px, py, pc: outs[0].at[4 * px + 2 * py + pc]

        def remote(k, src, dst, to):
            return pltpu.make_async_remote_copy(src_ref=src, dst_ref=dst, send_sem=send_sems.at[k], recv_sem=recv_sems.at[k],
                                                device_id=to, device_id_type=MESH)

        if kind == "to_sibling":
            return remote(0, ins[0], slot(x, y, c), (x, y, 1 - c))
        if kind == "from_sibling":
            return remote(0, ins[0], slot(x, y, 1 - c), (x, y, c))
        if kind == "to_chip":
            return remote(1 + j, ins[0], slot(x, y, c), (*chip, c))
        if kind == "from_chip":
            return remote(1 + j, ins[0], slot(*chip, c), (x, y, c))
        if kind == "pass_on":
            return remote(4 + j, slot(*chip, c), slot(*chip, c), (x, y, 1 - c))
        return remote(4 + j, slot(*chip, 1 - c), slot(*chip, 1 - c), (x, y, c))

    def first(*refs):
        copy("to_sibling", None, *refs).start()
        for j in range(3):
            copy("to_chip", j, *refs).start()

    def mid(*refs):
        for j in range(3):
            copy("from_chip", j, *refs).wait_recv()
            copy("pass_on", j, *refs).start()

    def last(*refs):
        copy("from_sibling", None, *refs).wait_recv()
        for j in range(3):
            copy("passed_on", j, *refs).wait_recv()
        copy("to_sibling", None, *refs).wait_send()
        for j in range(3):
            copy("to_chip", j, *refs).wait_send()
            copy("pass_on", j, *refs).wait_send()

    return _Side([block], [_sds((N_DEV, *block.shape), block.dtype)], 7, first, last, mid=mid, mid_late=True)


def _sum_slots(name, own, gathered, me, tr=512):
    n, rows, cols = gathered.shape
    tr = _pick(rows, tr, SUBLANES)

    def body(me_ref, own_ref, g_ref, o_ref):
        mine = own_ref[...]
        acc = jnp.where(me_ref[0] == 0, mine, g_ref[0])
        for k in range(1, n):
            acc = acc + jnp.where(me_ref[0] == k, mine, g_ref[k])
        o_ref[...] = acc

    grid_spec = pltpu.PrefetchScalarGridSpec(
        num_scalar_prefetch=1, grid=(rows // tr,),
        in_specs=[pl.BlockSpec((tr, cols), lambda i, me_ref: (i, 0)), pl.BlockSpec((n, tr, cols), lambda i, me_ref: (0, i, 0))],
        out_specs=pl.BlockSpec((tr, cols), lambda i, me_ref: (i, 0)))
    return pl.pallas_call(body, name=name, grid_spec=grid_spec, out_shape=_sds((rows, cols), own.dtype),
                          compiler_params=_params(("arbitrary",)))(me.reshape(1).astype(jnp.int32), own, gathered)


def _sum_received(name, full, c, shard, swapped, received, tr=256):
    n, rows, cols = received.shape
    tr = _pick(rows, tr, 16)

    def body(i_ref, a_ref, b_ref, s_ref, o_ref):
        acc = a_ref[...] + b_ref[...]
        for k in range(n):
            acc = acc + s_ref[k].astype(F32)
        o_ref[...] = acc

    grid_spec = pltpu.PrefetchScalarGridSpec(
        num_scalar_prefetch=1, grid=(rows // tr,),
        in_specs=[pl.BlockSpec((None, None, tr, cols), lambda i, i_ref: (i_ref[1], i_ref[0], i, 0)),
                  pl.BlockSpec((None, tr, cols), lambda i, i_ref: (i_ref[1], i, 0)),
                  pl.BlockSpec((n, tr, cols), lambda i, i_ref: (0, i, 0))],
        out_specs=pl.BlockSpec((None, tr, cols), lambda i, i_ref: (i_ref[0], i, 0)))
    return pl.pallas_call(body, name=name, grid_spec=grid_spec, out_shape=_sds((2, rows, cols), F32),
                          compiler_params=_params(("arbitrary",)))(jnp.stack([c, shard]).astype(jnp.int32), full, swapped, received)


def _add_halves(name, full, c, shard, received, tr=256):
    s, _, rh, cols = full.shape
    tr = _pick(rh, tr, 16)

    def body(i_ref, a_ref, b_ref, o_ref):
        o_ref[...] = (a_ref[...] + b_ref[...]).astype(BF16)

    other = lambda q, i_ref: (i_ref[1] + 1 + q) % s
    grid_spec = pltpu.PrefetchScalarGridSpec(
        num_scalar_prefetch=1, grid=(s - 1, rh // tr),
        in_specs=[pl.BlockSpec((None, None, tr, cols), lambda q, i, i_ref: (other(q, i_ref), i_ref[0], i, 0)),
                  pl.BlockSpec((None, tr, cols), lambda q, i, i_ref: (other(q, i_ref), i, 0))],
        out_specs=pl.BlockSpec((None, tr, cols), lambda q, i, i_ref: (other(q, i_ref), i, 0)))
    return pl.pallas_call(body, name=name, grid_spec=grid_spec, out_shape=_sds((s, rh, cols), BF16),
                          compiler_params=_params(("arbitrary", "arbitrary")))(jnp.stack([c, shard]).astype(jnp.int32), full, received)


LARGE = ("w_in", "ssm_glu_w", "w_out", "w_ffn_in", "w_ffn_out", "w_ple_gate", "w_ple_proj")
COLUMN_SHARDED = ("w_in", "w_ffn_in", "w_ple_proj")
SMALL = ("norm_mix_g", "ssm_lambda_re", "ssm_lambda_im", "ssm_log_step", "ssm_b_re", "ssm_b_im", "ssm_c_re", "ssm_c_im",
         "ssm_d", "ssm_glu_b", "sgu_ln_g", "sgu_ln_b", "sgu_w", "sgu_b", "out_norm_ssm_g", "out_norm_sgu_g", "norm_ffn_g",
         "norm_ple_g", "b_ple_gate", "final_norm_g")
WEIGHTS = ("norm_mix_g", "w_in", "ssm_lambda_re", "ssm_lambda_im", "ssm_log_step", "ssm_b_re", "ssm_b_im", "ssm_c_re",
           "ssm_c_im", "ssm_d", "ssm_glu_w", "ssm_glu_b", "sgu_ln_g", "sgu_ln_b", "sgu_w", "sgu_b", "out_norm_ssm_g",
           "out_norm_sgu_g", "w_out", "norm_ffn_g", "w_ffn_in", "w_ffn_out", "norm_ple_g", "w_ple_gate", "b_ple_gate",
           "w_ple_proj", "final_norm_g")
PACK_ROWS = SUBLANES * LANES


def _pack(arrays):
    parts = []
    for a in arrays:
        flat = a.reshape(-1).astype(F32)
        pad = -flat.shape[0] % PACK_ROWS
        parts.append(jnp.pad(flat, (0, pad)) if pad else flat)
    return jnp.concatenate(parts).reshape(-1, LANES)


def _unpack(packed, like):
    flat = packed.reshape(-1)
    out, at = [], 0
    for a in like:
        size = a.size
        out.append(flat[at:at + size].reshape(a.shape))
        at += size + (-size % PACK_ROWS)
    return out


class _NoExchange:
    def __init__(self, weights):
        self.weights, self.grads, self.small = weights, {}, {}

    def weight(self, name):
        return self.weights[name]

    def grad(self, name, g):
        self.grads[name] = g

    def small_grads(self, grads):
        self.small.update(grads)

    def side(self, host):
        return None


class _MeshExchange:
    GATHER = {"norm_mix": (("w_in", 0, 16),),
              "proj_in": (("ssm_glu_w", 0, 16), ("w_out", 0, 16), ("w_ffn_in", 0, 1)),
              "ssm_fwd": (("w_ffn_in", 1, 13),),
              "proj_out": (("w_ffn_in", 13, 16),),
              "ffn_in": (("w_ffn_out", 0, 16), ("w_ple_gate", 0, 16), ("w_ple_proj", 0, 16))}
    GATHER_LONG = ("norm_mix", "proj_in", "ssm_fwd", "proj_out")
    SWAP = {"d_act": ("w_ple_proj", "w_ple_gate", "w_ffn_out"), "d_h2": ("w_ffn_in",), "d_ya0": ("w_out", "ssm_glu_w")}
    SWAP_ALONE = ("w_in",)
    SCATTER = {"d_ffn_in": ("w_ple_proj", "w_ple_gate", "w_ffn_out"), "ssm_bwd": ("w_ffn_in",),
               "d_sgu": ("w_out", "ssm_glu_w"), "d_h1": ("w_in",)}
    SMALL_GATHER = "d_proj_in"

    def __init__(self, shards, small_like, c, shard, me):
        self.c, self.shard, self.me, self.small_like = c, shard, me, small_like
        self.slots = {k: _cast_into_slot("cast_" + k, shards[k], shard) for k in LARGE}
        self.full, self.received, self.halves, self.quarters, self.small = {}, {}, {}, {}, {}

    def weight(self, name):
        g = self.slots[name]
        _, _, rh, cols = g.shape
        return g.reshape(N_CHIPS, 2 * rh, cols) if name in COLUMN_SHARDED else g.reshape(N_CHIPS * 2 * rh, cols)

    def grad(self, name, g):
        if name not in COLUMN_SHARDED:
            g = g.reshape(N_CHIPS, g.shape[0] // N_CHIPS, g.shape[1])
        self.full[name] = g.reshape(N_CHIPS, 2, g.shape[1] // 2, g.shape[2])
        if name in self.SWAP_ALONE:
            self._swapped((name,), _exchange_alone("grad_swap_" + name, _swap_side([self.full[name]])))

    def _swapped(self, names, received):
        for k, r in zip(names, received):
            self.received[k] = r
            self.halves[k] = _add_halves("grad_add_halves_" + k, self.full[k], self.c, self.shard, r)

    def small_grads(self, grads):
        self.small.update(grads)

    def _packed(self, names):
        return _pack([self.small[k].reshape(self.small_like[k].shape) for k in names])

    def side(self, host):
        if host in self.GATHER:
            return _gather_side([self.slots[k] for k, _, _ in self.GATHER[host]], [(lo, hi) for _, lo, hi in self.GATHER[host]],
                                mid_late=host in self.GATHER_LONG)
        if host in self.SWAP:
            return _swap_side([self.full[k] for k in self.SWAP[host]])
        if host in self.SCATTER:
            return _scatter_side([self.halves[k] for k in self.SCATTER[host]])
        if host == self.SMALL_GATHER:
            self.packed_early = self._packed(SMALL[1:])
            return _small_gather_side(self.packed_early)
        return None

    def done(self, host, moved):
        if host in self.GATHER:
            self.slots.update(zip([k for k, _, _ in self.GATHER[host]], moved))
        elif host in self.SWAP:
            self._swapped(self.SWAP[host], moved)
        elif host in self.SCATTER:
            self.quarters.update(zip(self.SCATTER[host], moved))
        else:
            (self.gathered_early,) = moved

    def small_reduced(self):
        early = _sum_slots("small_sum", self.packed_early, self.gathered_early, self.me)
        late = _allreduce_small(self._packed(SMALL[:1]))
        return jnp.concatenate([late, early], axis=0)

    def reduced(self):
        parts = [_sum_received("grad_sum_" + k, self.full[k], self.c, self.shard, self.received[k], self.quarters[k]) for k in LARGE]
        joined = _join_halves("grad_join", parts)
        return {k: j.reshape(2 * j.shape[1], j.shape[2]) for k, j in zip(LARGE, joined)}


def kernel(x, p, norm_mix_g, w_in, ssm_lambda_re, ssm_lambda_im, ssm_log_step, ssm_b_re, ssm_b_im, ssm_c_re, ssm_c_im, ssm_d, ssm_glu_w, ssm_glu_b, sgu_ln_g, sgu_ln_b, sgu_w, sgu_b, out_norm_ssm_g, out_norm_sgu_g, w_out, norm_ffn_g, w_ffn_in, w_ffn_out, norm_ple_g, w_ple_gate, b_ple_gate, w_ple_proj, final_norm_g, loss_target, m_norm_mix_g, m_w_in, m_ssm_lambda_re, m_ssm_lambda_im, m_ssm_log_step, m_ssm_b_re, m_ssm_b_im, m_ssm_c_re, m_ssm_c_im, m_ssm_d, m_ssm_glu_w, m_ssm_glu_b, m_sgu_ln_g, m_sgu_ln_b, m_sgu_w, m_sgu_b, m_out_norm_ssm_g, m_out_norm_sgu_g, m_w_out, m_norm_ffn_g, m_w_ffn_in, m_w_ffn_out, m_norm_ple_g, m_w_ple_gate, m_b_ple_gate, m_w_ple_proj, m_final_norm_g, v_norm_mix_g, v_w_in, v_ssm_lambda_re, v_ssm_lambda_im, v_ssm_log_step, v_ssm_b_re, v_ssm_b_im, v_ssm_c_re, v_ssm_c_im, v_ssm_d, v_ssm_glu_w, v_ssm_glu_b, v_sgu_ln_g, v_sgu_ln_b, v_sgu_w, v_sgu_b, v_out_norm_ssm_g, v_out_norm_sgu_g, v_w_out, v_norm_ffn_g, v_w_ffn_in, v_w_ffn_out, v_norm_ple_g, v_w_ple_gate, v_b_ple_gate, v_w_ple_proj, v_final_norm_g):
    given = dict(locals())
    w = {k: given[k] for k in WEIGHTS}
    m = {k: given["m_" + k] for k in WEIGHTS}
    v = {k: given["v_" + k] for k in WEIGHTS}
    c = lax.axis_index("c")
    shard = 2 * lax.axis_index("x") + lax.axis_index("y")

    exch = _MeshExchange({k: w[k].reshape(w[k].shape[1:]) for k in LARGE}, {k: w[k] for k in SMALL}, c, shard, 2 * shard + c)
    unlayer = lambda a: a if a.ndim == 1 else a[0]
    sp = {k: unlayer(w[k]) for k in SMALL}
    n_tok, d_model = x.shape[1:]
    loss, grad_x = _local_grads(x.reshape(n_tok, d_model), p.reshape(n_tok, p.shape[-1]),
                                loss_target.reshape(n_tok, d_model), sp, exch)
    loss = lax.psum(loss, ("x", "y", "c"))

    grad_w, delta_w, new_m, new_v = {}, {}, {}, {}
    reduced = exch.reduced()
    for k in LARGE:
        shape = w[k].shape
        two_d = lambda a: a.reshape(shape[1:])
        like = _sds(shape[1:], F32)
        d_k, m_k, v_k = _rowwise("adamw_" + k, _adamw, [two_d(w[k]), reduced[k], two_d(m[k]), two_d(v[k])], [], [like, like, like])
        grad_w[k], delta_w[k], new_m[k], new_v[k] = (a.reshape(shape) for a in (reduced[k], d_k, m_k, v_k))

    packed_g = exch.small_reduced()
    like = _sds(packed_g.shape, F32)
    d_s, m_s, v_s = _rowwise("adamw_small", _adamw, [_pack([w[k] for k in SMALL]), packed_g, _pack([m[k] for k in SMALL]),
                                                     _pack([v[k] for k in SMALL])], [], [like, like, like])
    shapes = [w[k] for k in SMALL]
    for k, g_k, d_k, m_k, v_k in zip(SMALL, _unpack(packed_g, shapes), _unpack(d_s, shapes), _unpack(m_s, shapes), _unpack(v_s, shapes)):
        grad_w[k], delta_w[k], new_m[k], new_v[k] = g_k, d_k, m_k, v_k

    return (loss, grad_x.reshape(x.shape), *[grad_w[k] for k in WEIGHTS], *[delta_w[k] for k in WEIGHTS],
            *[new_m[k] for k in WEIGHTS], *[new_v[k] for k in WEIGHTS])
```

```python
import functools

import jax
import jax.numpy as jnp
from jax import lax
from jax.experimental import pallas as pl
from jax.experimental.pallas import tpu as pltpu

F32 = jnp.float32
BF16 = jnp.bfloat16

EPS = 1e-6
LAMBDA_RE_MAX = -1e-4
ADAM_LR = 0.001
ADAM_B1 = 0.9
ADAM_B2 = 0.999
ADAM_EPS = 1e-08
ADAM_WD = 0.01
ADAM_STEP = 10

N_CHIPS = 4
N_DEV = 8
SUBLANES = 8
LANES = 128
SSM_CH_BLOCK = 256
SCAN_LANES = 256
SCAN_BLOCKS = 4
GATHER_PARTS = 16
VMEM_LIMIT = 56 * 1024 * 1024

MESH = pl.DeviceIdType.MESH


def _pick(n, pref, mult):
    if n <= pref:
        return n
    t = (pref // mult) * mult
    while t >= mult:
        if n % t == 0:
            return t
        t -= mult
    return n


def _params(semantics):
    return pltpu.CompilerParams(dimension_semantics=semantics, vmem_limit_bytes=VMEM_LIMIT)


class _Cols:
    def __init__(self, arr, width, blk):
        self.arr, self.width, self.blk = arr, width, blk


def _sds(shape, dtype):
    return jax.ShapeDtypeStruct(tuple(shape), dtype)


ANY = pl.BlockSpec(memory_space=pl.ANY)


class _Side:
    def __init__(self, ins, out_shapes, n_sems, first, last, mid=None, aliases=None, mid_late=False):
        self.ins, self.out_shapes, self.n_sems = list(ins), list(out_shapes), n_sems
        self.first, self.mid, self.last, self.mid_late = first, mid, last, mid_late
        self.aliases = dict(aliases or {})


def _call(body, side, operands, *, name, grid, in_specs, out_specs, out_shape, compiler_params, scratch_shapes=()):
    if side is None:
        return pl.pallas_call(body, name=name, grid=grid, in_specs=in_specs, out_specs=out_specs, out_shape=out_shape,
                              scratch_shapes=list(scratch_shapes), compiler_params=compiler_params)(*operands)
    single = not isinstance(out_specs, (list, tuple))
    out_specs = [out_specs] if single else list(out_specs)
    out_shape = [out_shape] if single else list(out_shape)
    n_in, n_out, n_scr = len(in_specs), len(out_specs), len(scratch_shapes)
    n_sin, n_sout = len(side.ins), len(side.out_shapes)
    steps = 1
    for g in grid:
        steps *= g

    def hosted(*refs):
        ins, s_ins = refs[:n_in], refs[n_in:n_in + n_sin]
        at = n_in + n_sin
        outs, s_outs = refs[at:at + n_out], refs[at + n_out:at + n_out + n_sout]
        scratch = refs[at + n_out + n_sout:at + n_out + n_sout + n_scr]
        sems = refs[-2:]
        step = pl.program_id(0)
        for d in range(1, len(grid)):
            step = step * grid[d] + pl.program_id(d)

        @pl.when(step == 0)
        def _():
            side.first(s_ins, s_outs, *sems)

        if side.mid is not None:
            @pl.when(step == (steps - 1 if side.mid_late else (3 * steps) // 4))
            def _():
                side.mid(s_ins, s_outs, *sems)

        body(*ins, *outs, *scratch)

        @pl.when(step == steps - 1)
        def _():
            side.last(s_ins, s_outs, *sems)

    res = pl.pallas_call(
        hosted, name=name, grid=grid, in_specs=[*in_specs, *[ANY] * n_sin], out_specs=[*out_specs, *[ANY] * n_sout],
        out_shape=[*out_shape, *side.out_shapes], input_output_aliases={n_in + i: n_out + o for i, o in side.aliases.items()},
        scratch_shapes=[*scratch_shapes, pltpu.SemaphoreType.DMA((side.n_sems,)), pltpu.SemaphoreType.DMA((side.n_sems,))],
        compiler_params=compiler_params)(*operands, *side.ins)
    return (res[0] if single else list(res[:n_out])), list(res[n_out:])


def _rowwise(name, fn, rows, params, row_outs, acc_outs=(), tr=256, side=None):
    rows = [r if isinstance(r, _Cols) else _Cols(r, r.shape[1], 0) for r in rows]
    m = rows[0].arr.shape[0]
    tr = _pick(m, tr, 16)
    n_in = len(rows) + len(params)
    n_ro = len(row_outs)

    def body(*refs):
        vals = fn(*[r[...] for r in refs[:n_in]])
        if not isinstance(vals, (tuple, list)):
            vals = (vals,)
        outs = refs[n_in:]
        for r, v in zip(outs[:n_ro], vals[:n_ro]):
            r[...] = v.astype(r.dtype)
        first = pl.program_id(0) == 0
        for r, v in zip(outs[n_ro:], vals[n_ro:]):
            @pl.when(first)
            def _():
                r[...] = jnp.zeros(r.shape, r.dtype)
            r[...] += v.astype(r.dtype).reshape(r.shape)

    in_specs = [pl.BlockSpec((tr, r.width), lambda i, b=r.blk: (i, b)) for r in rows]
    in_specs += [pl.BlockSpec(p.shape, lambda i, nd=p.ndim: (0,) * nd) for p in params]
    out_specs = [pl.BlockSpec((tr, o.shape[1]), lambda i: (i, 0)) for o in row_outs]
    out_specs += [pl.BlockSpec(o.shape, lambda i, nd=len(o.shape): (0,) * nd) for o in acc_outs]
    return _call(body, side, [*[r.arr for r in rows], *params], name=name, grid=(m // tr,), in_specs=in_specs,
                 out_specs=out_specs, out_shape=[*row_outs, *acc_outs], compiler_params=_params(("arbitrary",)))


def _grid_order(swap):
    if not swap:
        return (lambda grid: grid), (lambda f: f)
    return (lambda grid: grid[::-1]), (lambda f: (lambda j, i: f(i, j)))


def _mm_nn(name, a, w, *, sharded=False, res=None, out_dtype=F32, tm=512, tn=512, w_resident=False, side=None):
    m, k = a.shape
    tm = _pick(m, tm, 16)
    order, ix = _grid_order(w_resident)
    if sharded:
        s, _, ns = w.shape
        n = s * ns
        tn = _pick(ns, tn, LANES)
        per = ns // tn
        w_spec = pl.BlockSpec((None, k, tn), ix(lambda i, j: (j // per, 0, j % per)))
    else:
        n = w.shape[1]
        tn = _pick(n, tn, LANES)
        w_spec = pl.BlockSpec((k, tn), ix(lambda i, j: (0, j)))

    def body(a_ref, w_ref, *rest):
        acc = jnp.dot(a_ref[...], w_ref[...], preferred_element_type=F32)
        if res is not None:
            acc = acc + rest[0][...]
        rest[-1][...] = acc.astype(out_dtype)

    in_specs = [pl.BlockSpec((tm, k), ix(lambda i, j: (i, 0))), w_spec]
    ops = [a, w]
    if res is not None:
        in_specs.append(pl.BlockSpec((tm, tn), ix(lambda i, j: (i, j))))
        ops.append(res)
    return _call(body, side, ops, name=name, grid=order((m // tm, n // tn)), in_specs=in_specs,
                 out_specs=pl.BlockSpec((tm, tn), ix(lambda i, j: (i, j))), out_shape=_sds((m, n), out_dtype),
                 compiler_params=_params(("arbitrary", "arbitrary")))


def _mm_nt(name, g, w, *, sharded=False, g_halves=False, out_dtype=F32, tm=512, tk=512, w_resident=False, side=None):
    m, n = g.shape[-2:]
    tm = _pick(m, tm, 16)
    order, ix = _grid_order(w_resident)
    dims = (((1,), (1,)), ((), ()))
    g_spec = pl.BlockSpec((2, tm, n), ix(lambda i, j: (0, i, 0))) if g_halves else pl.BlockSpec((tm, n), ix(lambda i, j: (i, 0)))
    if sharded:
        s, k, ns = w.shape
        tk = _pick(k, tk, LANES)
        w_spec = pl.BlockSpec((s, tk, ns), ix(lambda i, j: (0, j, 0)))

        def columns(g_ref, q):
            if not g_halves:
                return g_ref[:, q * ns:(q + 1) * ns]
            half, at = divmod(q, s // 2)
            return g_ref[half, :, at * ns:(at + 1) * ns]

        def body(g_ref, w_ref, o_ref):
            acc = lax.dot_general(columns(g_ref, 0), w_ref[0], dims, preferred_element_type=F32)
            for q in range(1, s):
                acc = acc + lax.dot_general(columns(g_ref, q), w_ref[q], dims, preferred_element_type=F32)
            o_ref[...] = acc.astype(out_dtype)
    else:
        k = w.shape[0]
        tk = _pick(k, tk, LANES)
        w_spec = pl.BlockSpec((tk, n), ix(lambda i, j: (j, 0)))

        def body(g_ref, w_ref, o_ref):
            o_ref[...] = lax.dot_general(g_ref[...], w_ref[...], dims, preferred_element_type=F32).astype(out_dtype)

    return _call(body, side, [g, w], name=name, grid=order((m // tm, k // tk)), in_specs=[g_spec, w_spec],
                 out_specs=pl.BlockSpec((tm, tk), ix(lambda i, j: (i, j))), out_shape=_sds((m, k), out_dtype),
                 compiler_params=_params(("arbitrary", "arbitrary")))


def _mm_tn(name, a, g, *, shards=0, g_halves=False, tk=512, tn=512, g_resident=False, side=None):
    m, k = a.shape
    n = 2 * g.shape[2] if g_halves else g.shape[1]
    tk = _pick(k, tk, LANES)
    order, ix = _grid_order(g_resident)
    dims = (((0,), (0,)), ((), ()))
    if shards:
        ns = n // shards
        tn = _pick(ns, tn, LANES)
        per = ns // tn
        out_spec = pl.BlockSpec((None, tk, tn), ix(lambda i, j: (j // per, i, j % per)))
        out_shape = _sds((shards, k, ns), F32)
    else:
        tn = _pick(n, tn, LANES)
        out_spec = pl.BlockSpec((tk, tn), ix(lambda i, j: (i, j)))
        out_shape = _sds((k, n), F32)

    def body(a_ref, g_ref, o_ref):
        o_ref[...] = lax.dot_general(a_ref[...], g_ref[...], dims, preferred_element_type=F32)

    if g_halves:
        per_half = n // 2 // tn
        g_spec = pl.BlockSpec((None, m, tn), ix(lambda i, j: (j // per_half, 0, j % per_half)))
    else:
        g_spec = pl.BlockSpec((m, tn), ix(lambda i, j: (0, j)))
    return _call(body, side, [a, g], name=name, grid=order((k // tk, n // tn)),
                 in_specs=[pl.BlockSpec((m, tk), ix(lambda i, j: (0, i))), g_spec],
                 out_specs=out_spec, out_shape=out_shape, compiler_params=_params(("arbitrary", "arbitrary")))


def _ffn_in_swiglu(name, a, w, *, tm=512, tn=1408, side=None):
    m, k = a.shape
    s, _, ns = w.shape
    f = s * ns // 2
    tm = _pick(m, tm, 16)
    tn = _pick(ns, tn, LANES)
    per = ns // tn
    order, ix = _grid_order(True)

    def body(a_ref, wg_ref, wu_ref, act_ref, gu_ref):
        x = a_ref[...]
        gate = jnp.dot(x, wg_ref[...], preferred_element_type=F32)
        up = jnp.dot(x, wu_ref[...], preferred_element_type=F32)
        act_ref[...] = _swiglu(gate, up).astype(BF16)
        gu_ref[0] = gate.astype(BF16)
        gu_ref[1] = up.astype(BF16)

    return _call(body, side, [a, w, w], name=name, grid=order((m // tm, f // tn)),
                 in_specs=[pl.BlockSpec((tm, k), ix(lambda i, j: (i, 0))),
                           pl.BlockSpec((None, k, tn), ix(lambda i, j: (j // per, 0, j % per))),
                           pl.BlockSpec((None, k, tn), ix(lambda i, j: (s // 2 + j // per, 0, j % per)))],
                 out_specs=[pl.BlockSpec((tm, tn), ix(lambda i, j: (i, j))), pl.BlockSpec((2, tm, tn), ix(lambda i, j: (0, i, j)))],
                 out_shape=[_sds((m, f), BF16), _sds((2, m, f), BF16)], compiler_params=_params(("arbitrary", "arbitrary")))


def _d_act_swiglu(name, g, w, gu, *, tm=1024, tk=512, side=None):
    m, n = g.shape
    f = w.shape[0]
    tm = _pick(m, tm, 16)
    tk = _pick(f, tk, LANES)
    dims = (((1,), (1,)), ((), ()))

    def body(g_ref, w_ref, gu_ref, o_ref):
        dact = lax.dot_general(g_ref[...], w_ref[...], dims, preferred_element_type=F32)
        _, vjp = jax.vjp(_swiglu, gu_ref[0].astype(F32), gu_ref[1].astype(F32))
        dgate, dup = vjp(dact)
        o_ref[0] = dgate.astype(BF16)
        o_ref[1] = dup.astype(BF16)

    return _call(body, side, [g, w, gu], name=name, grid=(m // tm, f // tk),
                 in_specs=[pl.BlockSpec((tm, n), lambda i, j: (i, 0)), pl.BlockSpec((tk, n), lambda i, j: (j, 0)),
                           pl.BlockSpec((2, tm, tk), lambda i, j: (0, i, j))],
                 out_specs=pl.BlockSpec((2, tm, tk), lambda i, j: (0, i, j)), out_shape=_sds((2, m, f), BF16),
                 compiler_params=_params(("arbitrary", "arbitrary")))


def _rms(x, g):
    r = lax.rsqrt(jnp.mean(x * x, axis=-1, keepdims=True) + EPS)
    return (x * r) * g


def _glu_out(y_pre, q, glu_b, g_norm):
    ya0 = jax.nn.gelu(y_pre)
    return _rms(ya0 * jax.nn.sigmoid(q + glu_b), g_norm)


def _sgu_rows(zu, zv, ln_g, ln_b, w_s, b_st, g_norm):
    heads, t, _ = w_s.shape
    hd = zu.shape[1] // heads
    uu = jax.nn.gelu(zu)
    vv = jax.nn.gelu(zv)
    mu = jnp.mean(vv, axis=-1, keepdims=True)
    xc = vv - mu
    r = lax.rsqrt(jnp.mean(xc * xc, axis=-1, keepdims=True) + EPS)
    vn = (xc * r) * ln_g + ln_b
    row = lax.broadcasted_iota(jnp.int32, (t, t), 0)
    col = lax.broadcasted_iota(jnp.int32, (t, t), 1)
    causal = row >= col
    chunks = []
    for n in range(zu.shape[0] // t):
        blocks = []
        for h in range(heads):
            wm = jnp.where(causal, w_s[h], jnp.zeros_like(w_s[h])).astype(BF16)
            vb = vn[n * t:(n + 1) * t, h * hd:(h + 1) * hd].astype(BF16)
            blocks.append(jnp.dot(wm, vb, preferred_element_type=F32) + b_st[:, h:h + 1])
        chunks.append(jnp.concatenate(blocks, axis=1))
    s = jnp.concatenate(chunks, axis=0) if len(chunks) > 1 else chunks[0]
    return _rms(uu * s, g_norm)


def _swiglu(gate, up):
    return jax.nn.silu(gate) * up


def _head_loss(x2, gpre, pp, b_g, g_final, target):
    gate = jax.nn.sigmoid(gpre + b_g)
    out = _rms(x2 + gate * pp, g_final)
    err = jnp.square(out - target)
    return 0.5 * jnp.sum(jnp.mean(err, axis=-1))


def _ssm_disc(lam_re, lam_im, log_step):
    lr = jnp.minimum(lam_re, LAMBDA_RE_MAX)
    li = lam_im
    dt = jnp.exp(log_step)
    mag = jnp.exp(lr * dt)
    ang = li * dt
    abar_re = mag * jnp.cos(ang)
    abar_im = mag * jnp.sin(ang)
    nr = abar_re - 1.0
    ni = abar_im
    den = lr * lr + li * li
    q_re = (nr * lr + ni * li) / den
    q_im = (ni * lr - nr * li) / den
    return abar_re, abar_im, q_re, q_im


def _ssm_bbar(q_re, q_im, b_re, b_im):
    return q_re * b_re - q_im * b_im, q_re * b_im + q_im * b_re


def _ssm_discretised(lam_re, lam_im, log_step, bt_re, bt_im):
    ar, ai, qr, qi = _ssm_disc(lam_re, lam_im, log_step)
    return (ar, ai, *_ssm_bbar(qr, qi, bt_re, bt_im))


def _adamw(w, g, m, v):
    m = ADAM_B1 * m + (1.0 - ADAM_B1) * g
    v = ADAM_B2 * v + (1.0 - ADAM_B2) * jnp.square(g)
    m_hat = m / (1.0 - ADAM_B1 ** ADAM_STEP)
    v_hat = v / (1.0 - ADAM_B2 ** ADAM_STEP)
    delta = -ADAM_LR * (m_hat / (jnp.sqrt(v_hat) + ADAM_EPS) + ADAM_WD * w)
    return delta, m, v


class _SsmDims:
    def __init__(self, groups, state, gch):
        self.g, self.p, self.h = groups, state, gch
        self.d = groups * gch
        self.cb = min(SSM_CH_BLOCK, self.d)
        self.gb = self.cb // gch
        self.ns = self.gb * state
        self.nb = self.d // self.cb


def _ssm_rows(sd, sp):
    gp = sd.g * sd.p
    log_step = jnp.broadcast_to(sp["ssm_log_step"][:, None], (sd.g, sd.p)).reshape(1, gp)
    bt = [sp[k].reshape(gp, sd.h).T for k in ("ssm_b_re", "ssm_b_im")]
    ct = [sp[k].transpose(1, 0, 2).reshape(sd.h, gp) for k in ("ssm_c_re", "ssm_c_im")]
    return (sp["ssm_lambda_re"].reshape(1, gp), sp["ssm_lambda_im"].reshape(1, gp), log_step, *bt, *ct)


def _block_mask(sd):
    row = lax.broadcasted_iota(jnp.int32, (sd.cb, sd.ns), 0) // sd.h
    col = lax.broadcasted_iota(jnp.int32, (sd.cb, sd.ns), 1) // sd.p
    return row == col


def _scan_consts(pr, pi_, reverse):
    if reverse:
        pi_ = [-v for v in pi_]
    shape = (SUBLANES, pr[0].shape[1])
    rows = lax.broadcasted_iota(jnp.int32, shape, 0)
    parts = []
    for d in (1, 2, 4):
        keep = (rows < SUBLANES - d) if reverse else (rows >= d)
        parts += [jnp.where(keep, jnp.broadcast_to(v[d - 1], shape), 0.0) for v in (pr, pi_)]
    order = range(SUBLANES - 1, -1, -1) if reverse else range(SUBLANES)
    parts += [jnp.concatenate([v[t] for t in order], axis=0) for v in (pr, pi_)]
    return jnp.concatenate(parts, axis=0)


def _ssm_operands(sd, rows):
    cb, ns, nb = sd.cb, sd.ns, sd.nb

    def body(lam_re, lam_im, log_step, bt_re, bt_im, ct_re, ct_im, wb_ref, wbt_ref, wc_ref, wct_ref, cst_f_ref, cst_r_ref):
        ar, ai, bbar_re, bbar_im = _ssm_discretised(lam_re[...], lam_im[...], log_step[...], bt_re[...], bt_im[...])
        pr, pi_ = [ar], [ai]
        for _ in range(SUBLANES - 1):
            pr, pi_ = pr + [pr[-1] * ar - pi_[-1] * ai], pi_ + [pr[-1] * ai + pi_[-1] * ar]
        mask = _block_mask(sd)
        spread = lambda src: jnp.where(mask, jnp.concatenate([src] * sd.gb, axis=0), 0.0)
        for j in range(nb):
            at = slice(j * ns, (j + 1) * ns)
            w = jnp.concatenate([spread(bbar_re[:, at]), spread(bbar_im[:, at])], axis=1)
            v = jnp.concatenate([spread(ct_re[:, at]), -spread(ct_im[:, at])], axis=1)
            wb_ref[j] = w.astype(BF16)
            wbt_ref[j] = w.T.astype(BF16)
            wct_ref[j] = v.astype(BF16)
            wc_ref[j] = v.T.astype(BF16)
            pj, qj = [u[:, at] for u in pr], [u[:, at] for u in pi_]
            cst_f_ref[j] = _scan_consts(pj, qj, False)
            cst_r_ref[j] = _scan_consts(pj, qj, True)

    wide, tall = _sds((nb, cb, 2 * ns), BF16), _sds((nb, 2 * ns, cb), BF16)
    cst = _sds((nb, 8 * SUBLANES, ns), F32)
    vm = pl.BlockSpec(memory_space=pltpu.VMEM)
    return pl.pallas_call(body, name="ssm_operands", in_specs=[vm] * 7, out_specs=[vm] * 6,
                          out_shape=[wide, tall, tall, wide, cst, cst],
                          compiler_params=pltpu.CompilerParams(vmem_limit_bytes=VMEM_LIMIT))(*rows)


def _ssm_param_grads(sd, rows, dwb, dwc, da):
    ns, nb, gp = sd.ns, sd.nb, sd.g * sd.p

    def body(lam_re, lam_im, log_step, bt_re, bt_im, dwb_v, dwc_v, da_v, *outs):
        mask = _block_mask(sd)

        def fold(dense):
            kept = jnp.where(mask, dense, 0.0)
            acc = kept[0:sd.h]
            for gl in range(1, sd.gb):
                acc = acc + kept[gl * sd.h:(gl + 1) * sd.h]
            return acc

        lanes = lambda parts: jnp.concatenate(parts, axis=1) if len(parts) > 1 else parts[0]
        dbbar_re = lanes([fold(dwb_v[j][:, :ns]) for j in range(nb)])
        dbbar_im = lanes([fold(dwb_v[j][:, ns:]) for j in range(nb)])
        dwct = [dwc_v[j] for j in range(nb)]
        d_ct_re = lanes([fold(t[:, :ns]) for t in dwct])
        d_ct_im = -lanes([fold(t[:, ns:]) for t in dwct])
        dabar_re = lanes([da_v[j][0:1, :ns] for j in range(nb)])
        dabar_im = lanes([da_v[j][0:1, ns:] for j in range(nb)])
        _, vjp = jax.vjp(_ssm_discretised, lam_re[...], lam_im[...], log_step[...], bt_re[...], bt_im[...])
        d_lr, d_li, d_ls, d_bt_re, d_bt_im = vjp((dabar_re, dabar_im, dbbar_re, dbbar_im))
        group = (lax.broadcasted_iota(jnp.int32, (gp, sd.g), 0) // sd.p == lax.broadcasted_iota(jnp.int32, (gp, sd.g), 1))
        d_log_step = jnp.dot(d_ls, group.astype(F32), precision=lax.Precision.HIGHEST, preferred_element_type=F32)
        for ref, val in zip(outs, (d_lr, d_li, d_log_step, d_bt_re, d_bt_im, d_ct_re, d_ct_im)):
            ref[...] = val

    row, mat = _sds((1, gp), F32), _sds((sd.h, gp), F32)
    vm = pl.BlockSpec(memory_space=pltpu.VMEM)
    return pl.pallas_call(body, name="ssm_param_grads", in_specs=[vm] * 8, out_specs=[vm] * 7,
                          out_shape=[row, row, _sds((1, sd.g), F32), mat, mat, mat, mat],
                          compiler_params=pltpu.CompilerParams(vmem_limit_bytes=VMEM_LIMIT))(*rows[:5], dwb, dwc, da)


def _block_scan(s_ref, cst_ref, carry_ref, sd, rows, reverse):
    ns = sd.ns
    nblk = rows // SUBLANES
    w = min(SCAN_LANES, ns)
    for c0 in range(0, ns, w):
        re_l, im_l = slice(c0, c0 + w), slice(ns + c0, ns + c0 + w)
        cst = [cst_ref[k * SUBLANES:(k + 1) * SUBLANES, c0:c0 + w] for k in range(8)]

        def step(k, carry, re_l=re_l, im_l=im_l, cst=cst):
            local = []
            for b in range(SCAN_BLOCKS):
                blk = SCAN_BLOCKS * k + b
                blk = (nblk - 1 - blk) if reverse else blk
                r0 = pl.multiple_of(blk * SUBLANES, SUBLANES)
                xr = s_ref[pl.ds(r0, SUBLANES), re_l]
                xi = s_ref[pl.ds(r0, SUBLANES), im_l]
                for n, d in enumerate((1, 2, 4)):
                    ar, ai = cst[2 * n], cst[2 * n + 1]
                    shift = (SUBLANES - d) if reverse else d
                    sr = pltpu.roll(xr, shift, 0)
                    si = pltpu.roll(xi, shift, 0)
                    xr, xi = xr + ar * sr - ai * si, xi + ar * si + ai * sr
                local.append((r0, xr, xi))
            cr, ci = carry
            edge = slice(0, 1) if reverse else slice(SUBLANES - 1, SUBLANES)
            for r0, xr, xi in local:
                br = jnp.broadcast_to(cr, xr.shape)
                bi = jnp.broadcast_to(ci, xi.shape)
                xr, xi = xr + cst[6] * br - cst[7] * bi, xi + cst[6] * bi + cst[7] * br
                s_ref[pl.ds(r0, SUBLANES), re_l] = xr
                s_ref[pl.ds(r0, SUBLANES), im_l] = xi
                cr, ci = xr[edge, :], xi[edge, :]
            return cr, ci

        cr, ci = lax.fori_loop(0, nblk // SCAN_BLOCKS, step, (carry_ref[0:1, re_l], carry_ref[0:1, im_l]))
        carry_ref[0:1, re_l] = cr
        carry_ref[0:1, im_l] = ci


def _ssm_fwd(name, sd, z, wb, wc, cst, d_row, tt=512, side=None):
    n_tok = z.shape[0]
    tt = _pick(n_tok, tt, 16)
    cb, ns2 = sd.cb, 2 * sd.ns

    def body(z_ref, wb_ref, wc_ref, cst_ref, d_ref, y_ref, s_ref, a0_ref, carry_ref):
        @pl.when(pl.program_id(1) == 0)
        def _():
            carry_ref[...] = jnp.zeros(carry_ref.shape, F32)
        u = z_ref[...]
        s_ref[...] = jnp.dot(u.astype(BF16), wb_ref[...], preferred_element_type=F32)
        _block_scan(s_ref, cst_ref, carry_ref, sd, tt, reverse=False)
        y = jnp.dot(s_ref[...].astype(BF16), wc_ref[...], preferred_element_type=F32) + d_ref[...] * u
        y_ref[...] = y
        a0_ref[...] = jax.nn.gelu(y).astype(BF16)

    return _call(
        body, side, [z, wb, wc, cst, d_row], name=name, grid=(sd.nb, n_tok // tt),
        in_specs=[pl.BlockSpec((tt, cb), lambda j, i: (i, j)),
                  pl.BlockSpec((None, cb, ns2), lambda j, i: (j, 0, 0)),
                  pl.BlockSpec((None, ns2, cb), lambda j, i: (j, 0, 0)),
                  pl.BlockSpec((None, 8 * SUBLANES, sd.ns), lambda j, i: (j, 0, 0)),
                  pl.BlockSpec((1, cb), lambda j, i: (0, j))],
        out_specs=[pl.BlockSpec((tt, cb), lambda j, i: (i, j)), pl.BlockSpec((tt, ns2), lambda j, i: (i, j)),
                   pl.BlockSpec((tt, cb), lambda j, i: (i, j))],
        out_shape=[_sds((n_tok, sd.d), F32), _sds((n_tok, sd.nb * ns2), F32), _sds((n_tok, sd.d), BF16)],
        scratch_shapes=[pltpu.VMEM((SUBLANES, ns2), F32)],
        compiler_params=_params(("arbitrary", "arbitrary")))


def _ssm_bwd(name, sd, y_pre, dy_direct, dya0, z, states, wct, wbt, cst_rev, d_row, tt=512, side=None):
    n_tok = z.shape[0]
    tt = _pick(n_tok, tt, 16)
    nt = n_tok // tt
    cb, ns, ns2 = sd.cb, sd.ns, 2 * sd.ns
    blocks_per_tile = tt // SUBLANES
    tn_dims = (((0,), (0,)), ((), ()))

    def body(y_ref, dyd_ref, dya0_ref, z_ref, s_ref, sp_ref, wct_ref, wbt_ref, cst_ref, d_ref,
             du_ref, dwb_ref, dwc_ref, da_ref, dd_ref, lam_ref, carry_ref):
        i = pl.program_id(1)

        @pl.when(i == 0)
        def _():
            carry_ref[...] = jnp.zeros(carry_ref.shape, F32)
            dwb_ref[...] = jnp.zeros(dwb_ref.shape, F32)
            dwc_ref[...] = jnp.zeros(dwc_ref.shape, F32)
            da_ref[...] = jnp.zeros(da_ref.shape, F32)
            dd_ref[...] = jnp.zeros(dd_ref.shape, F32)

        _, gelu_vjp = jax.vjp(jax.nn.gelu, y_ref[...])
        dy_t = dyd_ref[...] + gelu_vjp(dya0_ref[...].astype(F32))[0]
        u = z_ref[...]
        dy16 = dy_t.astype(BF16)
        lam_ref[...] = jnp.dot(dy16, wct_ref[...], preferred_element_type=F32)
        _block_scan(lam_ref, cst_ref, carry_ref, sd, tt, reverse=True)
        lam = lam_ref[...]
        lam16 = lam.astype(BF16)
        du_ref[...] = (jnp.dot(lam16, wbt_ref[...], preferred_element_type=F32) + d_ref[...] * dy_t).astype(BF16)
        dd_ref[0:1, :] += jnp.sum(dy_t * u, axis=0, keepdims=True)
        dwb_ref[...] += lax.dot_general(u.astype(BF16), lam16, tn_dims, preferred_element_type=F32)
        s = s_ref[...]
        dwc_ref[...] += lax.dot_general(dy16, s.astype(BF16), tn_dims, preferred_element_type=F32)
        before = jnp.where(i == nt - 1, 0.0, 1.0) * sp_ref[SUBLANES - 1:SUBLANES, :]
        first_row = lax.broadcasted_iota(jnp.int32, s.shape, 0) == 0
        prev = jnp.where(first_row, jnp.broadcast_to(before, s.shape), pltpu.roll(s, 1, 0))
        lr, li = lam[:, :ns], lam[:, ns:]
        pr, pi_ = prev[:, :ns], prev[:, ns:]
        da_ref[0:1, 0:ns] += jnp.sum(lr * pr + li * pi_, axis=0, keepdims=True)
        da_ref[0:1, ns:ns2] += jnp.sum(li * pr - lr * pi_, axis=0, keepdims=True)

    rev = lambda i: nt - 1 - i
    return _call(
        body, side, [y_pre, dy_direct, dya0, z, states, states, wct, wbt, cst_rev, d_row], name=name, grid=(sd.nb, nt),
        in_specs=[pl.BlockSpec((tt, cb), lambda j, i: (rev(i), j)),
                  pl.BlockSpec((tt, cb), lambda j, i: (rev(i), j)),
                  pl.BlockSpec((tt, cb), lambda j, i: (rev(i), j)),
                  pl.BlockSpec((tt, cb), lambda j, i: (rev(i), j)),
                  pl.BlockSpec((tt, ns2), lambda j, i: (rev(i), j)),
                  pl.BlockSpec((SUBLANES, ns2), lambda j, i: (jnp.maximum(rev(i) * blocks_per_tile - 1, 0), j)),
                  pl.BlockSpec((None, cb, ns2), lambda j, i: (j, 0, 0)),
                  pl.BlockSpec((None, ns2, cb), lambda j, i: (j, 0, 0)),
                  pl.BlockSpec((None, 8 * SUBLANES, ns), lambda j, i: (j, 0, 0)),
                  pl.BlockSpec((1, cb), lambda j, i: (0, j))],
        out_specs=[pl.BlockSpec((tt, cb), lambda j, i: (rev(i), j)),
                   pl.BlockSpec((None, cb, ns2), lambda j, i: (j, 0, 0)),
                   pl.BlockSpec((None, cb, ns2), lambda j, i: (j, 0, 0)),
                   pl.BlockSpec((None, SUBLANES, ns2), lambda j, i: (j, 0, 0)),
                   pl.BlockSpec((None, SUBLANES, cb), lambda j, i: (j, 0, 0))],
        out_shape=[_sds((n_tok, sd.d), BF16), _sds((sd.nb, cb, ns2), F32), _sds((sd.nb, cb, ns2), F32),
                   _sds((sd.nb, SUBLANES, ns2), F32), _sds((sd.nb, SUBLANES, cb), F32)],
        scratch_shapes=[pltpu.VMEM((tt, ns2), F32), pltpu.VMEM((SUBLANES, ns2), F32)],
        compiler_params=_params(("arbitrary", "arbitrary")))


def _hosted(exch, fn, name, *args, **kw):
    side = exch.side(name)
    if side is None:
        return fn(name, *args, **kw)
    out, moved = fn(name, *args, side=side, **kw)
    exch.done(name, moved)
    return out


def _local_grads(x, p, target, sp, exch):
    n_tok, d_model = x.shape
    d_ssm = sp["ssm_d"].shape[0] * sp["ssm_d"].shape[1]
    d_sgu = sp["sgu_ln_g"].shape[-1]
    sd = _SsmDims(sp["ssm_b_re"].shape[0], sp["ssm_b_re"].shape[1], sp["ssm_b_re"].shape[2])
    heads, chunk, _ = sp["sgu_w"].shape
    row = lambda v: v.reshape(1, -1)
    tok = lambda w, dt=F32: _sds((n_tok, w), dt)
    acc = lambda w: _sds((1, w), F32)

    g_mix = row(sp["norm_mix_g"])
    (h1,) = _hosted(exch, _rowwise, "norm_mix", lambda a, g: _rms(a, g), [x], [g_mix], [tok(d_model, BF16)])
    z = _hosted(exch, _mm_nn, "proj_in", h1, exch.weight("w_in"), sharded=True, tn=768)

    ssm_rows = _ssm_rows(sd, sp)
    wb, wbt, wc, wct, cst_fwd, cst_rev = _ssm_operands(sd, ssm_rows)
    d_row = row(sp["ssm_d"])
    y_pre, states, ya0_16 = _hosted(exch, _ssm_fwd, "ssm_fwd", sd, z, wb, wc, cst_fwd, d_row)
    q = _mm_nn("ssm_glu", ya0_16, exch.weight("ssm_glu_w"), tm=1024)
    glu_b, g_ossm = row(sp["ssm_glu_b"]), row(sp["out_norm_ssm_g"])
    (ya_n,) = _rowwise("ssm_glu_out", _glu_out, [y_pre, q], [glu_b, g_ossm], [tok(d_ssm, BF16)])

    assert d_ssm == d_sgu
    zu, zv = _Cols(z, d_sgu, 1), _Cols(z, d_sgu, 2)
    ln_g, ln_b, g_osgu = row(sp["sgu_ln_g"]), row(sp["sgu_ln_b"]), row(sp["out_norm_sgu_g"])
    b_st = sp["sgu_b"].T
    sgu_tr = 2 * chunk

    def sgu_joined(ya_t, zu_t, zv_t, *params):
        return jnp.concatenate([ya_t, _sgu_rows(zu_t, zv_t, *params).astype(BF16)], axis=1)

    (ycat,) = _rowwise("sgu", sgu_joined, [ya_n, zu, zv], [ln_g, ln_b, sp["sgu_w"], b_st, g_osgu],
                       [tok(d_ssm + d_sgu, BF16)], tr=sgu_tr)
    x1 = _hosted(exch, _mm_nn, "proj_out", ycat, exch.weight("w_out"), res=x, tm=1024)

    g_ffn = row(sp["norm_ffn_g"])
    (h2,) = _rowwise("norm_ffn", lambda a, g: _rms(a, g), [x1], [g_ffn], [tok(d_model, BF16)])
    act, gu16 = _hosted(exch, _ffn_in_swiglu, "ffn_in", h2, exch.weight("w_ffn_in"))
    x2 = _mm_nn("ffn_out", act, exch.weight("w_ffn_out"), res=x1)

    g_ple = row(sp["norm_ple_g"])
    (h3,) = _rowwise("norm_ple", lambda a, g: _rms(a, g), [x2], [g_ple], [tok(d_model, BF16)])
    gpre = _mm_nn("ple_gate", h3, exch.weight("w_ple_gate"), tm=1024)
    (p16,) = _rowwise("ple_cast", lambda a: a, [p], [], [tok(p.shape[1], BF16)])
    pp = _mm_nn("ple_proj", p16, exch.weight("w_ple_proj"), sharded=True, tm=1024)

    b_g, g_fin = row(sp["b_ple_gate"]), row(sp["final_norm_g"])

    def head(x2_t, gpre_t, pp_t, tgt_t, b_g_v, g_fin_v):
        loss, grads = jax.value_and_grad(_head_loss, argnums=(0, 1, 2, 3, 4))(x2_t, gpre_t, pp_t, b_g_v, g_fin_v, tgt_t)
        dx2, dgpre, dpp, db, dg = grads
        return dx2, dgpre.astype(BF16), dpp.astype(BF16), jnp.full((1, LANES), loss, F32), db, dg

    dx2_head, dgpre16, dpp16, loss_row, d_b_g, d_g_fin = _rowwise(
        "head", head, [x2, gpre, pp, target], [b_g, g_fin],
        [tok(d_model), tok(d_model, BF16), tok(d_model, BF16)], [acc(LANES), acc(d_model), acc(d_model)])
    loss = loss_row[0, 0]
    exch.small_grads({"loss": loss_row})

    exch.grad("w_ple_proj", _mm_tn("d_ple_proj", p16, dpp16, shards=N_CHIPS, tk=256))
    exch.grad("w_ple_gate", _mm_tn("d_ple_gate", h3, dgpre16))
    dh3 = _mm_nt("d_h3", dgpre16, exch.weight("w_ple_gate"), out_dtype=BF16, tm=1024)

    def norm_bwd(x_t, dres_t, dh_t, g_v):
        _, vjp = jax.vjp(_rms, x_t, g_v)
        dx, dg = vjp(dh_t.astype(F32))
        dx = dres_t + dx
        return dx, dx.astype(BF16), dg

    dx2, dx2_16, d_g_ple = _rowwise("d_norm_ple", norm_bwd, [x2, dx2_head, dh3], [g_ple],
                                    [tok(d_model), tok(d_model, BF16)], [acc(d_model)])
    exch.grad("w_ffn_out", _mm_tn("d_ffn_out", act, dx2_16))
    dgu16 = _hosted(exch, _d_act_swiglu, "d_act", dx2_16, exch.weight("w_ffn_out"), gu16)
    exch.grad("w_ffn_in", _hosted(exch, _mm_tn, "d_ffn_in", h2, dgu16, shards=N_CHIPS, g_halves=True, tn=1408, g_resident=True))
    dh2 = _hosted(exch, _mm_nt, "d_h2", dgu16, exch.weight("w_ffn_in"), sharded=True, g_halves=True, out_dtype=BF16, tm=256, w_resident=True)
    dx1, dx1_16, d_g_ffn = _rowwise("d_norm_ffn", norm_bwd, [x1, dx2, dh2], [g_ffn],
                                    [tok(d_model), tok(d_model, BF16)], [acc(d_model)])
    exch.grad("w_out", _mm_tn("d_proj_out", ycat, dx1_16))
    dycat = _mm_nt("d_ycat", dx1_16, exch.weight("w_out"), out_dtype=BF16, tm=1024)

    def glu_out_bwd(y_pre_t, q_t, dy_t, glu_b_v, g_v):
        _, vjp = jax.vjp(_glu_out, y_pre_t, q_t, glu_b_v, g_v)
        dy_pre, dq, db, dg = vjp(dy_t.astype(F32))
        return dy_pre, dq.astype(BF16), db, dg

    dy_pre_a, dq16, d_glu_b, d_g_ossm = _rowwise(
        "d_ssm_glu_out", glu_out_bwd, [y_pre, q, _Cols(dycat, d_ssm, 0)], [glu_b, g_ossm],
        [tok(d_ssm), tok(d_ssm, BF16)], [acc(d_ssm), acc(d_ssm)])
    exch.grad("ssm_glu_w", _mm_tn("d_ssm_glu", ya0_16, dq16))
    dya0 = _hosted(exch, _mm_nt, "d_ya0", dq16, exch.weight("ssm_glu_w"), out_dtype=BF16, tm=1024)

    dz_ssm16, dwb, dwc, da, dd = _hosted(exch, _ssm_bwd, "ssm_bwd", sd, y_pre, dy_pre_a, dya0, z, states, wct, wbt,
                                         cst_rev, d_row)

    def sgu_bwd(dz_ssm_t, zu_t, zv_t, dy_t, ln_g_v, ln_b_v, w_v, b_v, g_v):
        _, vjp = jax.vjp(_sgu_rows, zu_t, zv_t, ln_g_v, ln_b_v, w_v, b_v, g_v)
        dzu, dzv, dlg, dlb, dw, db, dg = vjp(dy_t.astype(F32))
        return jnp.concatenate([dz_ssm_t, dzu.astype(BF16), dzv.astype(BF16)], axis=1), dlg, dlb, dw, db, dg

    dz16, d_ln_g, d_ln_b, d_sgu_w, d_b_st, d_g_osgu = _hosted(
        exch, _rowwise, "d_sgu", sgu_bwd, [dz_ssm16, zu, zv, _Cols(dycat, d_sgu, 1)], [ln_g, ln_b, sp["sgu_w"], b_st, g_osgu],
        [tok(d_ssm + 2 * d_sgu, BF16)],
        [acc(d_sgu), acc(d_sgu), _sds(sp["sgu_w"].shape, F32), _sds(b_st.shape, F32), acc(d_sgu)], tr=sgu_tr)

    d_lam_re, d_lam_im, d_log_step, d_bt_re, d_bt_im, d_ct_re, d_ct_im = _ssm_param_grads(sd, ssm_rows, dwb, dwc, da)
    d_b_re, d_b_im = d_bt_re.T, d_bt_im.T
    d_c_re, d_c_im = (t.reshape(sd.h, sd.g, sd.p).transpose(1, 0, 2) for t in (d_ct_re, d_ct_im))
    d_ssm_d = dd[:, 0, :].reshape(sd.g, sd.h)

    exch.small_grads({
        "ssm_lambda_re": d_lam_re, "ssm_lambda_im": d_lam_im, "ssm_log_step": d_log_step,
        "ssm_b_re": d_b_re, "ssm_b_im": d_b_im, "ssm_c_re": d_c_re, "ssm_c_im": d_c_im, "ssm_d": d_ssm_d,
        "ssm_glu_b": d_glu_b, "sgu_ln_g": d_ln_g, "sgu_ln_b": d_ln_b, "sgu_w": d_sgu_w, "sgu_b": d_b_st.T,
        "out_norm_ssm_g": d_g_ossm, "out_norm_sgu_g": d_g_osgu, "norm_ffn_g": d_g_ffn, "norm_ple_g": d_g_ple,
        "b_ple_gate": d_b_g, "final_norm_g": d_g_fin,
    })

    exch.grad("w_in", _hosted(exch, _mm_tn, "d_proj_in", h1, dz16, shards=N_CHIPS, tn=768))
    dh1 = _hosted(exch, _mm_nt, "d_h1", dz16, exch.weight("w_in"), sharded=True, out_dtype=BF16, tm=1024)

    def norm_in_bwd(x_t, dres_t, dh_t, g_v):
        _, vjp = jax.vjp(_rms, x_t, g_v)
        dx, dg = vjp(dh_t.astype(F32))
        return dres_t + dx, dg

    grad_x, d_g_mix = _hosted(exch, _rowwise, "d_norm_mix", norm_in_bwd, [x, dx1, dh1], [g_mix], [tok(d_model)], [acc(d_model)])
    exch.small_grads({"norm_mix_g": d_g_mix})
    return loss, grad_x


def _place():
    x, y, c = lax.axis_index("x"), lax.axis_index("y"), lax.axis_index("c")
    chips = [(1 - x, y), (x, 1 - y), (1 - x, 1 - y)]
    return x, y, c, chips


def _cast_into_slot(name, w2d, shard, tr=256):
    rows, cols = w2d.shape
    rh = rows // 2
    tr = _pick(rh, tr, 16)
    per = rh // tr

    def body(s_ref, a_ref, o_ref):
        o_ref[...] = a_ref[...].astype(BF16)

    grid_spec = pltpu.PrefetchScalarGridSpec(
        num_scalar_prefetch=1, grid=(2, per),
        in_specs=[pl.BlockSpec((tr, cols), lambda h, i, s_ref: (h * per + i, 0))],
        out_specs=pl.BlockSpec((None, None, tr, cols), lambda h, i, s_ref: (s_ref[0], h, i, 0)))
    return pl.pallas_call(body, name=name, grid_spec=grid_spec, out_shape=_sds((N_CHIPS, 2, rh, cols), BF16),
                          compiler_params=_params(("arbitrary", "arbitrary")))(shard.reshape(1).astype(jnp.int32), w2d)


def _exchange_alone(name, side):
    n_in, n_out = len(side.ins), len(side.out_shapes)

    def body(*refs):
        ins, outs, sems = refs[:n_in], refs[n_in:n_in + n_out], refs[n_in + n_out:]
        side.first(ins, outs, *sems)
        if side.mid is not None:
            side.mid(ins, outs, *sems)
        side.last(ins, outs, *sems)

    return pl.pallas_call(
        body, name=name, in_specs=[ANY] * n_in, out_specs=[ANY] * n_out, out_shape=side.out_shapes,
        input_output_aliases=side.aliases,
        scratch_shapes=[pltpu.SemaphoreType.DMA((side.n_sems,)), pltpu.SemaphoreType.DMA((side.n_sems,))],
    )(*side.ins)


def _gather_side(slots, parts=None, mid_late=False):
    n = len(slots)
    parts = parts or [(0, GATHER_PARTS)] * n

    def copies(kind, outs, send_sems, recv_sems):
        x, y, c, chips = _place()

        def remote(k, w, shard, half, to):
            unit = outs[w].shape[2] // GATHER_PARTS
            lo, hi = parts[w]
            ref = outs[w].at[shard, half, pl.ds(lo * unit, (hi - lo) * unit), :]
            return pltpu.make_async_remote_copy(src_ref=ref, dst_ref=ref, send_sem=send_sems.at[k], recv_sem=recv_sems.at[k],
                                                device_id=to, device_id_type=MESH)

        pairs = [(w, j, 2 * cx + cy, (cx, cy)) for w in range(n) for j, (cx, cy) in enumerate(chips)]
        if kind == "sends":
            return [remote(3 * w + j, w, 2 * x + y, c, (*chip, c)) for w, j, _, chip in pairs]
        if kind == "arrivals":
            return [remote(3 * w + j, w, s, c, (x, y, c)) for w, j, s, _ in pairs]
        if kind == "passed":
            return [remote(3 * n + 3 * w + j, w, s, c, (x, y, 1 - c)) for w, j, s, _ in pairs]
        return [remote(3 * n + 3 * w + j, w, s, 1 - c, (x, y, c)) for w, j, s, _ in pairs]

    def first(ins, outs, *sems):
        for cp in copies("sends", outs, *sems):
            cp.start()

    def mid(ins, outs, *sems):
        for arrived, onward in zip(copies("arrivals", outs, *sems), copies("passed", outs, *sems)):
            arrived.wait_recv()
            onward.start()

    def last(ins, outs, *sems):
        for cp in copies("from_sibling", outs, *sems):
            cp.wait_recv()
        for cp in copies("sends", outs, *sems) + copies("passed", outs, *sems):
            cp.wait_send()

    return _Side(slots, [_sds(s.shape, s.dtype) for s in slots], 6 * n, first, last, mid=mid, aliases={w: w for w in range(n)},
                 mid_late=mid_late)


def _swap_side(grads):
    n = len(grads)

    def copies(ins, outs, send_sems, recv_sems):
        x, y, c, _ = _place()
        return [pltpu.make_async_remote_copy(src_ref=ins[w].at[:, 1 - c], dst_ref=outs[w], send_sem=send_sems.at[w],
                                             recv_sem=recv_sems.at[w], device_id=(x, y, 1 - c), device_id_type=MESH)
                for w in range(n)]

    def first(*refs):
        for cp in copies(*refs):
            cp.start()

    def last(*refs):
        for cp in copies(*refs):
            cp.wait()

    return _Side(grads, [_sds((g.shape[0], *g.shape[2:]), g.dtype) for g in grads], n, first, last)


def _scatter_side(halves):
    n = len(halves)

    def copies(ins, outs, send_sems, recv_sems):
        x, y, c, chips = _place()
        return [pltpu.make_async_remote_copy(
            src_ref=ins[w].at[2 * cx + cy], dst_ref=outs[w].at[j], send_sem=send_sems.at[3 * w + j],
            recv_sem=recv_sems.at[3 * w + j], device_id=(cx, cy, c), device_id_type=MESH)
            for w in range(n) for j, (cx, cy) in enumerate(chips)]

    def first(*refs):
        for cp in copies(*refs):
            cp.start()

    def last(*refs):
        for cp in copies(*refs):
            cp.wait()

    return _Side(halves, [_sds((3, *h.shape[1:]), h.dtype) for h in halves], 3 * n, first, last)


def _join_halves(name, slots):
    n = len(slots)

    def body(*refs):
        outs = refs[n:2 * n]
        send_sems, recv_sems = refs[2 * n:]
        x, y, c, _ = _place()

        def copy(w, half, to):
            return pltpu.make_async_remote_copy(src_ref=outs[w].at[half], dst_ref=outs[w].at[half], send_sem=send_sems.at[w],
                                                recv_sem=recv_sems.at[w], device_id=to, device_id_type=MESH)

        copies = [copy(w, c, (x, y, 1 - c)) for w in range(n)]
        for cp in copies:
            cp.start()
        for w in range(n):
            copy(w, 1 - c, (x, y, c)).wait_recv()
        for cp in copies:
            cp.wait_send()

    return pl.pallas_call(
        body, name=name, in_specs=[ANY] * n, out_specs=[ANY] * n,
        out_shape=[_sds(s.shape, s.dtype) for s in slots], input_output_aliases={w: w for w in range(n)},
        scratch_shapes=[pltpu.SemaphoreType.DMA((n,)), pltpu.SemaphoreType.DMA((n,))],
    )(*slots)


def _allreduce_small(block, tr=256):
    rows, lanes = block.shape
    tr = _pick(rows, tr, SUBLANES)

    def body(x_ref, o_ref, buf, send_sems, recv_sems):
        x, y, c, chips = _place()
        me, sibling = (x, y, c), (x, y, 1 - c)

        def slot(px, py, pc):
            return buf.at[4 * px + 2 * py + pc]

        def copy(k, block_of, to):
            return pltpu.make_async_remote_copy(src_ref=slot(*block_of), dst_ref=slot(*block_of), send_sem=send_sems.at[k],
                                                recv_sem=recv_sems.at[k], device_id=to, device_id_type=MESH)

        slot(*me)[...] = x_ref[...]
        first = [copy(0, me, sibling)] + [copy(1 + j, me, (*chip, c)) for j, chip in enumerate(chips)]
        for cp in first:
            cp.start()
        passed = [copy(4 + j, (*chip, c), sibling) for j, chip in enumerate(chips)]
        for j, chip in enumerate(chips):
            copy(1 + j, (*chip, c), me).wait_recv()
            passed[j].start()
        copy(0, sibling, me).wait_recv()
        for j, chip in enumerate(chips):
            copy(4 + j, (*chip, 1 - c), me).wait_recv()
        for cp in first + passed:
            cp.wait_send()
        for r0 in range(0, rows, tr):
            acc = buf[0, r0:r0 + tr, :]
            for k in range(1, N_DEV):
                acc = acc + buf[k, r0:r0 + tr, :]
            o_ref[r0:r0 + tr, :] = acc

    vm = pl.BlockSpec(memory_space=pltpu.VMEM)
    return pl.pallas_call(
        body, name="allreduce_small", in_specs=[vm], out_specs=vm, out_shape=_sds((rows, lanes), block.dtype),
        scratch_shapes=[pltpu.VMEM((N_DEV, rows, lanes), block.dtype), pltpu.SemaphoreType.DMA((7,)), pltpu.SemaphoreType.DMA((7,))],
        compiler_params=pltpu.CompilerParams(vmem_limit_bytes=VMEM_LIMIT),
    )(block)


def _small_gather_side(block):
    def copy(kind, j, ins, outs, send_sems, recv_sems):
        x, y, c, chips = _place()
        chip = chips[j] if j is not None else None
        slot = lambda px, py, pc: outs[0].at[4 * px + 2 * py + pc]

        def remote(k, src, dst, to):
            return pltpu.make_async_remote_copy(src_ref=src, dst_ref=dst, send_sem=send_sems.at[k], recv_sem=recv_sems.at[k],
                                                device_id=to, device_id_type=MESH)

        if kind == "to_sibling":
            return remote(0, ins[0], slot(x, y, c), (x, y, 1 - c))
        if kind == "from_sibling":
            return remote(0, ins[0], slot(x, y, 1 - c), (x, y, c))
        if kind == "to_chip":
            return remote(1 + j, ins[0], slot(x, y, c), (*chip, c))
        if kind == "from_chip":
            return remote(1 + j, ins[0], slot(*chip, c), (x, y, c))
        if kind == "pass_on":
            return remote(4 + j, slot(*chip, c), slot(*chip, c), (x, y, 1 - c))
        return remote(4 + j, slot(*chip, 1 - c), slot(*chip, 1 - c), (x, y, c))

    def first(*refs):
        copy("to_sibling", None, *refs).start()
        for j in range(3):
            copy("to_chip", j, *refs).start()

    def mid(*refs):
        for j in range(3):
            copy("from_chip", j, *refs).wait_recv()
            copy("pass_on", j, *refs).start()

    def last(*refs):
        copy("from_sibling", None, *refs).wait_recv()
        for j in range(3):
            copy("passed_on", j, *refs).wait_recv()
        copy("to_sibling", None, *refs).wait_send()
        for j in range(3):
            copy("to_chip", j, *refs).wait_send()
            copy("pass_on", j, *refs).wait_send()

    return _Side([block], [_sds((N_DEV, *block.shape), block.dtype)], 7, first, last, mid=mid, mid_late=True)


def _sum_slots(name, own, gathered, me, tr=512):
    n, rows, cols = gathered.shape
    tr = _pick(rows, tr, SUBLANES)

    def body(me_ref, own_ref, g_ref, o_ref):
        mine = own_ref[...]
        acc = jnp.where(me_ref[0] == 0, mine, g_ref[0])
        for k in range(1, n):
            acc = acc + jnp.where(me_ref[0] == k, mine, g_ref[k])
        o_ref[...] = acc

    grid_spec = pltpu.PrefetchScalarGridSpec(
        num_scalar_prefetch=1, grid=(rows // tr,),
        in_specs=[pl.BlockSpec((tr, cols), lambda i, me_ref: (i, 0)), pl.BlockSpec((n, tr, cols), lambda i, me_ref: (0, i, 0))],
        out_specs=pl.BlockSpec((tr, cols), lambda i, me_ref: (i, 0)))
    return pl.pallas_call(body, name=name, grid_spec=grid_spec, out_shape=_sds((rows, cols), own.dtype),
                          compiler_params=_params(("arbitrary",)))(me.reshape(1).astype(jnp.int32), own, gathered)


def _sum_received(name, full, c, shard, swapped, received, tr=256):
    n, rows, cols = received.shape
    tr = _pick(rows, tr, 16)

    def body(i_ref, a_ref, b_ref, s_ref, o_ref):
        acc = a_ref[...] + b_ref[...]
        for k in range(n):
            acc = acc + s_ref[k].astype(F32)
        o_ref[...] = acc

    grid_spec = pltpu.PrefetchScalarGridSpec(
        num_scalar_prefetch=1, grid=(rows // tr,),
        in_specs=[pl.BlockSpec((None, None, tr, cols), lambda i, i_ref: (i_ref[1], i_ref[0], i, 0)),
                  pl.BlockSpec((None, tr, cols), lambda i, i_ref: (i_ref[1], i, 0)),
                  pl.BlockSpec((n, tr, cols), lambda i, i_ref: (0, i, 0))],
        out_specs=pl.BlockSpec((None, tr, cols), lambda i, i_ref: (i_ref[0], i, 0)))
    return pl.pallas_call(body, name=name, grid_spec=grid_spec, out_shape=_sds((2, rows, cols), F32),
                          compiler_params=_params(("arbitrary",)))(jnp.stack([c, shard]).astype(jnp.int32), full, swapped, received)


def _add_halves(name, full, c, shard, received, tr=256):
    s, _, rh, cols = full.shape
    tr = _pick(rh, tr, 16)

    def body(i_ref, a_ref, b_ref, o_ref):
        o_ref[...] = (a_ref[...] + b_ref[...]).astype(BF16)

    other = lambda q, i_ref: (i_ref[1] + 1 + q) % s
    grid_spec = pltpu.PrefetchScalarGridSpec(
        num_scalar_prefetch=1, grid=(s - 1, rh // tr),
        in_specs=[pl.BlockSpec((None, None, tr, cols), lambda q, i, i_ref: (other(q, i_ref), i_ref[0], i, 0)),
                  pl.BlockSpec((None, tr, cols), lambda q, i, i_ref: (other(q, i_ref), i, 0))],
        out_specs=pl.BlockSpec((None, tr, cols), lambda q, i, i_ref: (other(q, i_ref), i, 0)))
    return pl.pallas_call(body, name=name, grid_spec=grid_spec, out_shape=_sds((s, rh, cols), BF16),
                          compiler_params=_params(("arbitrary", "arbitrary")))(jnp.stack([c, shard]).astype(jnp.int32), full, received)


LARGE = ("w_in", "ssm_glu_w", "w_out", "w_ffn_in", "w_ffn_out", "w_ple_gate", "w_ple_proj")
COLUMN_SHARDED = ("w_in", "w_ffn_in", "w_ple_proj")
SMALL = ("norm_mix_g", "ssm_lambda_re", "ssm_lambda_im", "ssm_log_step", "ssm_b_re", "ssm_b_im", "ssm_c_re", "ssm_c_im",
         "ssm_d", "ssm_glu_b", "sgu_ln_g", "sgu_ln_b", "sgu_w", "sgu_b", "out_norm_ssm_g", "out_norm_sgu_g", "norm_ffn_g",
         "norm_ple_g", "b_ple_gate", "final_norm_g")
WEIGHTS = ("norm_mix_g", "w_in", "ssm_lambda_re", "ssm_lambda_im", "ssm_log_step", "ssm_b_re", "ssm_b_im", "ssm_c_re",
           "ssm_c_im", "ssm_d", "ssm_glu_w", "ssm_glu_b", "sgu_ln_g", "sgu_ln_b", "sgu_w", "sgu_b", "out_norm_ssm_g",
           "out_norm_sgu_g", "w_out", "norm_ffn_g", "w_ffn_in", "w_ffn_out", "norm_ple_g", "w_ple_gate", "b_ple_gate",
           "w_ple_proj", "final_norm_g")
PACK_ROWS = SUBLANES * LANES


def _pack(arrays):
    parts = []
    for a in arrays:
        flat = a.reshape(-1).astype(F32)
        pad = -flat.shape[0] % PACK_ROWS
        parts.append(jnp.pad(flat, (0, pad)) if pad else flat)
    return jnp.concatenate(parts).reshape(-1, LANES)


def _unpack(packed, like):
    flat = packed.reshape(-1)
    out, at = [], 0
    for a in like:
        size = a.size
        out.append(flat[at:at + size].reshape(a.shape))
        at += size + (-size % PACK_ROWS)
    return out


class _NoExchange:
    def __init__(self, weights):
        self.weights, self.grads, self.small = weights, {}, {}

    def weight(self, name):
        return self.weights[name]

    def grad(self, name, g):
        self.grads[name] = g

    def small_grads(self, grads):
        self.small.update(grads)

    def side(self, host):
        return None


class _MeshExchange:
    GATHER = {"norm_mix": (("w_in", 0, 16),),
              "proj_in": (("ssm_glu_w", 0, 16), ("w_out", 0, 16), ("w_ffn_in", 0, 1)),
              "ssm_fwd": (("w_ffn_in", 1, 13),),
              "proj_out": (("w_ffn_in", 13, 16),),
              "ffn_in": (("w_ffn_out", 0, 16), ("w_ple_gate", 0, 16), ("w_ple_proj", 0, 16))}
    GATHER_LONG = ("norm_mix", "proj_in", "ssm_fwd", "proj_out")
    SWAP = {"d_act": ("w_ple_proj", "w_ple_gate", "w_ffn_out"), "d_h2": ("w_ffn_in",), "d_ya0": ("w_out", "ssm_glu_w")}
    SWAP_ALONE = ("w_in",)
    SCATTER = {"d_ffn_in": ("w_ple_proj", "w_ple_gate", "w_ffn_out"), "ssm_bwd": ("w_ffn_in",),
               "d_sgu": ("w_out", "ssm_glu_w"), "d_h1": ("w_in",)}
    SMALL_GATHER = "d_proj_in"

    def __init__(self, shards, small_like, c, shard, me):
        self.c, self.shard, self.me, self.small_like = c, shard, me, small_like
        self.slots = {k: _cast_into_slot("cast_" + k, shards[k], shard) for k in LARGE}
        self.full, self.received, self.halves, self.quarters, self.small = {}, {}, {}, {}, {}

    def weight(self, name):
        g = self.slots[name]
        _, _, rh, cols = g.shape
        return g.reshape(N_CHIPS, 2 * rh, cols) if name in COLUMN_SHARDED else g.reshape(N_CHIPS * 2 * rh, cols)

    def grad(self, name, g):
        if name not in COLUMN_SHARDED:
            g = g.reshape(N_CHIPS, g.shape[0] // N_CHIPS, g.shape[1])
        self.full[name] = g.reshape(N_CHIPS, 2, g.shape[1] // 2, g.shape[2])
        if name in self.SWAP_ALONE:
            self._swapped((name,), _exchange_alone("grad_swap_" + name, _swap_side([self.full[name]])))

    def _swapped(self, names, received):
        for k, r in zip(names, received):
            self.received[k] = r
            self.halves[k] = _add_halves("grad_add_halves_" + k, self.full[k], self.c, self.shard, r)

    def small_grads(self, grads):
        self.small.update(grads)

    def _packed(self, names):
        return _pack([self.small[k].reshape(self.small_like[k].shape) for k in names])

    def side(self, host):
        if host in self.GATHER:
            return _gather_side([self.slots[k] for k, _, _ in self.GATHER[host]], [(lo, hi) for _, lo, hi in self.GATHER[host]],
                                mid_late=host in self.GATHER_LONG)
        if host in self.SWAP:
            return _swap_side([self.full[k] for k in self.SWAP[host]])
        if host in self.SCATTER:
            return _scatter_side([self.halves[k] for k in self.SCATTER[host]])
        if host == self.SMALL_GATHER:
            self.packed_early = self._packed(SMALL[1:] + ("loss",))
            return _small_gather_side(self.packed_early)
        return None

    def done(self, host, moved):
        if host in self.GATHER:
            self.slots.update(zip([k for k, _, _ in self.GATHER[host]], moved))
        elif host in self.SWAP:
            self._swapped(self.SWAP[host], moved)
        elif host in self.SCATTER:
            self.quarters.update(zip(self.SCATTER[host], moved))
        else:
            (self.gathered_early,) = moved

    def small_reduced(self):
        early = _sum_slots("small_sum", self.packed_early, self.gathered_early, self.me)
        late = _allreduce_small(self._packed(SMALL[:1]))
        loss_at = early.shape[0] - PACK_ROWS // LANES
        return jnp.concatenate([late, early[:loss_at]], axis=0), early[loss_at, 0]

    def reduced(self):
        parts = [_sum_received("grad_sum_" + k, self.full[k], self.c, self.shard, self.received[k], self.quarters[k]) for k in LARGE]
        joined = _join_halves("grad_join", parts)
        return {k: j.reshape(2 * j.shape[1], j.shape[2]) for k, j in zip(LARGE, joined)}


def kernel(x, p, norm_mix_g, w_in, ssm_lambda_re, ssm_lambda_im, ssm_log_step, ssm_b_re, ssm_b_im, ssm_c_re, ssm_c_im, ssm_d, ssm_glu_w, ssm_glu_b, sgu_ln_g, sgu_ln_b, sgu_w, sgu_b, out_norm_ssm_g, out_norm_sgu_g, w_out, norm_ffn_g, w_ffn_in, w_ffn_out, norm_ple_g, w_ple_gate, b_ple_gate, w_ple_proj, final_norm_g, loss_target, m_norm_mix_g, m_w_in, m_ssm_lambda_re, m_ssm_lambda_im, m_ssm_log_step, m_ssm_b_re, m_ssm_b_im, m_ssm_c_re, m_ssm_c_im, m_ssm_d, m_ssm_glu_w, m_ssm_glu_b, m_sgu_ln_g, m_sgu_ln_b, m_sgu_w, m_sgu_b, m_out_norm_ssm_g, m_out_norm_sgu_g, m_w_out, m_norm_ffn_g, m_w_ffn_in, m_w_ffn_out, m_norm_ple_g, m_w_ple_gate, m_b_ple_gate, m_w_ple_proj, m_final_norm_g, v_norm_mix_g, v_w_in, v_ssm_lambda_re, v_ssm_lambda_im, v_ssm_log_step, v_ssm_b_re, v_ssm_b_im, v_ssm_c_re, v_ssm_c_im, v_ssm_d, v_ssm_glu_w, v_ssm_glu_b, v_sgu_ln_g, v_sgu_ln_b, v_sgu_w, v_sgu_b, v_out_norm_ssm_g, v_out_norm_sgu_g, v_w_out, v_norm_ffn_g, v_w_ffn_in, v_w_ffn_out, v_norm_ple_g, v_w_ple_gate, v_b_ple_gate, v_w_ple_proj, v_final_norm_g):
    given = dict(locals())
    w = {k: given[k] for k in WEIGHTS}
    m = {k: given["m_" + k] for k in WEIGHTS}
    v = {k: given["v_" + k] for k in WEIGHTS}
    c = lax.axis_index("c")
    shard = 2 * lax.axis_index("x") + lax.axis_index("y")

    small_like = {k: w[k] for k in SMALL}
    small_like["loss"] = _sds((1, LANES), F32)
    exch = _MeshExchange({k: w[k].reshape(w[k].shape[1:]) for k in LARGE}, small_like, c, shard, 2 * shard + c)
    unlayer = lambda a: a if a.ndim == 1 else a[0]
    sp = {k: unlayer(w[k]) for k in SMALL}
    n_tok, d_model = x.shape[1:]
    _, grad_x = _local_grads(x.reshape(n_tok, d_model), p.reshape(n_tok, p.shape[-1]),
                             loss_target.reshape(n_tok, d_model), sp, exch)

    grad_w, delta_w, new_m, new_v = {}, {}, {}, {}
    reduced = exch.reduced()
    for k in LARGE:
        shape = w[k].shape
        two_d = lambda a: a.reshape(shape[1:])
        like = _sds(shape[1:], F32)
        update = lambda w_t, g_t, m_t, v_t: (g_t, *_adamw(w_t, g_t, m_t, v_t))
        outs = _rowwise("adamw_" + k, update, [two_d(w[k]), reduced[k], two_d(m[k]), two_d(v[k])], [], [like, like, like, like])
        grad_w[k], delta_w[k], new_m[k], new_v[k] = (a.reshape(shape) for a in outs)

    packed_g, loss = exch.small_reduced()
    like = _sds(packed_g.shape, F32)
    d_s, m_s, v_s = _rowwise("adamw_small", _adamw, [_pack([w[k] for k in SMALL]), packed_g, _pack([m[k] for k in SMALL]),
                                                     _pack([v[k] for k in SMALL])], [], [like, like, like])
    shapes = [w[k] for k in SMALL]
    for k, g_k, d_k, m_k, v_k in zip(SMALL, _unpack(packed_g, shapes), _unpack(d_s, shapes), _unpack(m_s, shapes), _unpack(v_s, shapes)):
        grad_w[k], delta_w[k], new_m[k], new_v[k] = g_k, d_k, m_k, v_k

    return (loss, grad_x.reshape(x.shape), *[grad_w[k] for k in WEIGHTS], *[delta_w[k] for k in WEIGHTS],
            *[new_m[k] for k in WEIGHTS], *[new_v[k] for k in WEIGHTS])
```

```python
import functools

import jax
import jax.numpy as jnp
from jax import lax
from jax.experimental import pallas as pl
from jax.experimental.pallas import tpu as pltpu

F32 = jnp.float32
BF16 = jnp.bfloat16

EPS = 1e-6
LAMBDA_RE_MAX = -1e-4
ADAM_LR = 0.001
ADAM_B1 = 0.9
ADAM_B2 = 0.999
ADAM_EPS = 1e-08
ADAM_WD = 0.01
ADAM_STEP = 10

N_CHIPS = 4
N_DEV = 8
SUBLANES = 8
LANES = 128
SSM_CH_BLOCK = 256
SCAN_LANES = 256
SCAN_BLOCKS = 4
GATHER_PARTS = 16
VMEM_LIMIT = 56 * 1024 * 1024

MESH = pl.DeviceIdType.MESH


def _pick(n, pref, mult):
    if n <= pref:
        return n
    t = (pref // mult) * mult
    while t >= mult:
        if n % t == 0:
            return t
        t -= mult
    return n


def _params(semantics):
    return pltpu.CompilerParams(dimension_semantics=semantics, vmem_limit_bytes=VMEM_LIMIT)


class _Cols:
    def __init__(self, arr, width, blk):
        self.arr, self.width, self.blk = arr, width, blk


def _sds(shape, dtype):
    return jax.ShapeDtypeStruct(tuple(shape), dtype)


ANY = pl.BlockSpec(memory_space=pl.ANY)


class _Side:
    def __init__(self, ins, out_shapes, n_sems, first, last, mid=None, aliases=None, mid_late=False):
        self.ins, self.out_shapes, self.n_sems = list(ins), list(out_shapes), n_sems
        self.first, self.mid, self.last, self.mid_late = first, mid, last, mid_late
        self.aliases = dict(aliases or {})


def _call(body, side, operands, *, name, grid, in_specs, out_specs, out_shape, compiler_params, scratch_shapes=()):
    if side is None:
        return pl.pallas_call(body, name=name, grid=grid, in_specs=in_specs, out_specs=out_specs, out_shape=out_shape,
                              scratch_shapes=list(scratch_shapes), compiler_params=compiler_params)(*operands)
    single = not isinstance(out_specs, (list, tuple))
    out_specs = [out_specs] if single else list(out_specs)
    out_shape = [out_shape] if single else list(out_shape)
    n_in, n_out, n_scr = len(in_specs), len(out_specs), len(scratch_shapes)
    n_sin, n_sout = len(side.ins), len(side.out_shapes)
    steps = 1
    for g in grid:
        steps *= g

    def hosted(*refs):
        ins, s_ins = refs[:n_in], refs[n_in:n_in + n_sin]
        at = n_in + n_sin
        outs, s_outs = refs[at:at + n_out], refs[at + n_out:at + n_out + n_sout]
        scratch = refs[at + n_out + n_sout:at + n_out + n_sout + n_scr]
        sems = refs[-2:]
        step = pl.program_id(0)
        for d in range(1, len(grid)):
            step = step * grid[d] + pl.program_id(d)

        @pl.when(step == 0)
        def _():
            side.first(s_ins, s_outs, *sems)

        if side.mid is not None:
            @pl.when(step == (steps - 1 if side.mid_late else (3 * steps) // 4))
            def _():
                side.mid(s_ins, s_outs, *sems)

        body(*ins, *outs, *scratch)

        @pl.when(step == steps - 1)
        def _():
            side.last(s_ins, s_outs, *sems)

    res = pl.pallas_call(
        hosted, name=name, grid=grid, in_specs=[*in_specs, *[ANY] * n_sin], out_specs=[*out_specs, *[ANY] * n_sout],
        out_shape=[*out_shape, *side.out_shapes], input_output_aliases={n_in + i: n_out + o for i, o in side.aliases.items()},
        scratch_shapes=[*scratch_shapes, pltpu.SemaphoreType.DMA((side.n_sems,)), pltpu.SemaphoreType.DMA((side.n_sems,))],
        compiler_params=compiler_params)(*operands, *side.ins)
    return (res[0] if single else list(res[:n_out])), list(res[n_out:])


def _rowwise(name, fn, rows, params, row_outs, acc_outs=(), tr=256, side=None):
    rows = [r if isinstance(r, _Cols) else _Cols(r, r.shape[1], 0) for r in rows]
    m = rows[0].arr.shape[0]
    tr = _pick(m, tr, 16)
    n_in = len(rows) + len(params)
    n_ro = len(row_outs)

    def body(*refs):
        vals = fn(*[r[...] for r in refs[:n_in]])
        if not isinstance(vals, (tuple, list)):
            vals = (vals,)
        outs = refs[n_in:]
        for r, v in zip(outs[:n_ro], vals[:n_ro]):
            r[...] = v.astype(r.dtype)
        first = pl.program_id(0) == 0
        for r, v in zip(outs[n_ro:], vals[n_ro:]):
            @pl.when(first)
            def _():
                r[...] = jnp.zeros(r.shape, r.dtype)
            r[...] += v.astype(r.dtype).reshape(r.shape)

    in_specs = [pl.BlockSpec((tr, r.width), lambda i, b=r.blk: (i, b)) for r in rows]
    in_specs += [pl.BlockSpec(p.shape, lambda i, nd=p.ndim: (0,) * nd) for p in params]
    out_specs = [pl.BlockSpec((tr, o.shape[1]), lambda i: (i, 0)) for o in row_outs]
    out_specs += [pl.BlockSpec(o.shape, lambda i, nd=len(o.shape): (0,) * nd) for o in acc_outs]
    return _call(body, side, [*[r.arr for r in rows], *params], name=name, grid=(m // tr,), in_specs=in_specs,
                 out_specs=out_specs, out_shape=[*row_outs, *acc_outs], compiler_params=_params(("arbitrary",)))


def _grid_order(swap):
    if not swap:
        return (lambda grid: grid), (lambda f: f)
    return (lambda grid: grid[::-1]), (lambda f: (lambda j, i: f(i, j)))


def _mm_nn(name, a, w, *, sharded=False, res=None, out_dtype=F32, tm=512, tn=512, w_resident=False, side=None):
    m, k = a.shape
    tm = _pick(m, tm, 16)
    order, ix = _grid_order(w_resident)
    if sharded:
        s, _, ns = w.shape
        n = s * ns
        tn = _pick(ns, tn, LANES)
        per = ns // tn
        w_spec = pl.BlockSpec((None, k, tn), ix(lambda i, j: (j // per, 0, j % per)))
    else:
        n = w.shape[1]
        tn = _pick(n, tn, LANES)
        w_spec = pl.BlockSpec((k, tn), ix(lambda i, j: (0, j)))

    def body(a_ref, w_ref, *rest):
        acc = jnp.dot(a_ref[...], w_ref[...], preferred_element_type=F32)
        if res is not None:
            acc = acc + rest[0][...]
        rest[-1][...] = acc.astype(out_dtype)

    in_specs = [pl.BlockSpec((tm, k), ix(lambda i, j: (i, 0))), w_spec]
    ops = [a, w]
    if res is not None:
        in_specs.append(pl.BlockSpec((tm, tn), ix(lambda i, j: (i, j))))
        ops.append(res)
    return _call(body, side, ops, name=name, grid=order((m // tm, n // tn)), in_specs=in_specs,
                 out_specs=pl.BlockSpec((tm, tn), ix(lambda i, j: (i, j))), out_shape=_sds((m, n), out_dtype),
                 compiler_params=_params(("arbitrary", "arbitrary")))


def _mm_nt(name, g, w, *, sharded=False, g_halves=False, out_dtype=F32, tm=512, tk=512, w_resident=False, side=None):
    m, n = g.shape[-2:]
    tm = _pick(m, tm, 16)
    order, ix = _grid_order(w_resident)
    dims = (((1,), (1,)), ((), ()))
    g_spec = pl.BlockSpec((2, tm, n), ix(lambda i, j: (0, i, 0))) if g_halves else pl.BlockSpec((tm, n), ix(lambda i, j: (i, 0)))
    if sharded:
        s, k, ns = w.shape
        tk = _pick(k, tk, LANES)
        w_spec = pl.BlockSpec((s, tk, ns), ix(lambda i, j: (0, j, 0)))

        def columns(g_ref, q):
            if not g_halves:
                return g_ref[:, q * ns:(q + 1) * ns]
            half, at = divmod(q, s // 2)
            return g_ref[half, :, at * ns:(at + 1) * ns]

        def body(g_ref, w_ref, o_ref):
            acc = lax.dot_general(columns(g_ref, 0), w_ref[0], dims, preferred_element_type=F32)
            for q in range(1, s):
                acc = acc + lax.dot_general(columns(g_ref, q), w_ref[q], dims, preferred_element_type=F32)
            o_ref[...] = acc.astype(out_dtype)
    else:
        k = w.shape[0]
        tk = _pick(k, tk, LANES)
        w_spec = pl.BlockSpec((tk, n), ix(lambda i, j: (j, 0)))

        def body(g_ref, w_ref, o_ref):
            o_ref[...] = lax.dot_general(g_ref[...], w_ref[...], dims, preferred_element_type=F32).astype(out_dtype)

    return _call(body, side, [g, w], name=name, grid=order((m // tm, k // tk)), in_specs=[g_spec, w_spec],
                 out_specs=pl.BlockSpec((tm, tk), ix(lambda i, j: (i, j))), out_shape=_sds((m, k), out_dtype),
                 compiler_params=_params(("arbitrary", "arbitrary")))


def _mm_tn(name, a, g, *, shards=0, g_halves=False, tk=512, tn=512, g_resident=False, side=None):
    m, k = a.shape
    n = 2 * g.shape[2] if g_halves else g.shape[1]
    tk = _pick(k, tk, LANES)
    order, ix = _grid_order(g_resident)
    dims = (((0,), (0,)), ((), ()))
    if shards:
        ns = n // shards
        tn = _pick(ns, tn, LANES)
        per = ns // tn
        out_spec = pl.BlockSpec((None, tk, tn), ix(lambda i, j: (j // per, i, j % per)))
        out_shape = _sds((shards, k, ns), F32)
    else:
        tn = _pick(n, tn, LANES)
        out_spec = pl.BlockSpec((tk, tn), ix(lambda i, j: (i, j)))
        out_shape = _sds((k, n), F32)

    def body(a_ref, g_ref, o_ref):
        o_ref[...] = lax.dot_general(a_ref[...], g_ref[...], dims, preferred_element_type=F32)

    if g_halves:
        per_half = n // 2 // tn
        g_spec = pl.BlockSpec((None, m, tn), ix(lambda i, j: (j // per_half, 0, j % per_half)))
    else:
        g_spec = pl.BlockSpec((m, tn), ix(lambda i, j: (0, j)))
    return _call(body, side, [a, g], name=name, grid=order((k // tk, n // tn)),
                 in_specs=[pl.BlockSpec((m, tk), ix(lambda i, j: (0, i))), g_spec],
                 out_specs=out_spec, out_shape=out_shape, compiler_params=_params(("arbitrary", "arbitrary")))


def _ffn_in_swiglu(name, a, w, *, tm=512, tn=1408, side=None):
    m, k = a.shape
    s, _, ns = w.shape
    f = s * ns // 2
    tm = _pick(m, tm, 16)
    tn = _pick(ns, tn, LANES)
    per = ns // tn
    order, ix = _grid_order(True)

    def body(a_ref, wg_ref, wu_ref, act_ref, gu_ref):
        x = a_ref[...]
        gate = jnp.dot(x, wg_ref[...], preferred_element_type=F32)
        up = jnp.dot(x, wu_ref[...], preferred_element_type=F32)
        act_ref[...] = _swiglu(gate, up).astype(BF16)
        gu_ref[0] = gate.astype(BF16)
        gu_ref[1] = up.astype(BF16)

    return _call(body, side, [a, w, w], name=name, grid=order((m // tm, f // tn)),
                 in_specs=[pl.BlockSpec((tm, k), ix(lambda i, j: (i, 0))),
                           pl.BlockSpec((None, k, tn), ix(lambda i, j: (j // per, 0, j % per))),
                           pl.BlockSpec((None, k, tn), ix(lambda i, j: (s // 2 + j // per, 0, j % per)))],
                 out_specs=[pl.BlockSpec((tm, tn), ix(lambda i, j: (i, j))), pl.BlockSpec((2, tm, tn), ix(lambda i, j: (0, i, j)))],
                 out_shape=[_sds((m, f), BF16), _sds((2, m, f), BF16)], compiler_params=_params(("arbitrary", "arbitrary")))


def _d_act_swiglu(name, g, w, gu, *, tm=1024, tk=512, side=None):
    m, n = g.shape
    f = w.shape[0]
    tm = _pick(m, tm, 16)
    tk = _pick(f, tk, LANES)
    dims = (((1,), (1,)), ((), ()))

    def body(g_ref, w_ref, gu_ref, o_ref):
        dact = lax.dot_general(g_ref[...], w_ref[...], dims, preferred_element_type=F32)
        _, vjp = jax.vjp(_swiglu, gu_ref[0].astype(F32), gu_ref[1].astype(F32))
        dgate, dup = vjp(dact)
        o_ref[0] = dgate.astype(BF16)
        o_ref[1] = dup.astype(BF16)

    return _call(body, side, [g, w, gu], name=name, grid=(m // tm, f // tk),
                 in_specs=[pl.BlockSpec((tm, n), lambda i, j: (i, 0)), pl.BlockSpec((tk, n), lambda i, j: (j, 0)),
                           pl.BlockSpec((2, tm, tk), lambda i, j: (0, i, j))],
                 out_specs=pl.BlockSpec((2, tm, tk), lambda i, j: (0, i, j)), out_shape=_sds((2, m, f), BF16),
                 compiler_params=_params(("arbitrary", "arbitrary")))


def _rms(x, g):
    r = lax.rsqrt(jnp.mean(x * x, axis=-1, keepdims=True) + EPS)
    return (x * r) * g


def _glu_out(y_pre, q, glu_b, g_norm):
    ya0 = jax.nn.gelu(y_pre)
    return _rms(ya0 * jax.nn.sigmoid(q + glu_b), g_norm)


def _sgu_rows(zu, zv, ln_g, ln_b, w_s, b_st, g_norm):
    heads, t, _ = w_s.shape
    hd = zu.shape[1] // heads
    uu = jax.nn.gelu(zu)
    vv = jax.nn.gelu(zv)
    mu = jnp.mean(vv, axis=-1, keepdims=True)
    xc = vv - mu
    r = lax.rsqrt(jnp.mean(xc * xc, axis=-1, keepdims=True) + EPS)
    vn = (xc * r) * ln_g + ln_b
    row = lax.broadcasted_iota(jnp.int32, (t, t), 0)
    col = lax.broadcasted_iota(jnp.int32, (t, t), 1)
    causal = row >= col
    chunks = []
    for n in range(zu.shape[0] // t):
        blocks = []
        for h in range(heads):
            wm = jnp.where(causal, w_s[h], jnp.zeros_like(w_s[h])).astype(BF16)
            vb = vn[n * t:(n + 1) * t, h * hd:(h + 1) * hd].astype(BF16)
            blocks.append(jnp.dot(wm, vb, preferred_element_type=F32) + b_st[:, h:h + 1])
        chunks.append(jnp.concatenate(blocks, axis=1))
    s = jnp.concatenate(chunks, axis=0) if len(chunks) > 1 else chunks[0]
    return _rms(uu * s, g_norm)


def _swiglu(gate, up):
    return jax.nn.silu(gate) * up


def _head_loss(x2, gpre, pp, b_g, g_final, target):
    gate = jax.nn.sigmoid(gpre + b_g)
    out = _rms(x2 + gate * pp, g_final)
    err = jnp.square(out - target)
    return 0.5 * jnp.sum(jnp.mean(err, axis=-1))


def _ssm_disc(lam_re, lam_im, log_step):
    lr = jnp.minimum(lam_re, LAMBDA_RE_MAX)
    li = lam_im
    dt = jnp.exp(log_step)
    mag = jnp.exp(lr * dt)
    ang = li * dt
    abar_re = mag * jnp.cos(ang)
    abar_im = mag * jnp.sin(ang)
    nr = abar_re - 1.0
    ni = abar_im
    den = lr * lr + li * li
    q_re = (nr * lr + ni * li) / den
    q_im = (ni * lr - nr * li) / den
    return abar_re, abar_im, q_re, q_im


def _ssm_bbar(q_re, q_im, b_re, b_im):
    return q_re * b_re - q_im * b_im, q_re * b_im + q_im * b_re


def _ssm_discretised(lam_re, lam_im, log_step, bt_re, bt_im):
    ar, ai, qr, qi = _ssm_disc(lam_re, lam_im, log_step)
    return (ar, ai, *_ssm_bbar(qr, qi, bt_re, bt_im))


def _adamw(w, g, m, v):
    m = ADAM_B1 * m + (1.0 - ADAM_B1) * g
    v = ADAM_B2 * v + (1.0 - ADAM_B2) * jnp.square(g)
    m_hat = m / (1.0 - ADAM_B1 ** ADAM_STEP)
    v_hat = v / (1.0 - ADAM_B2 ** ADAM_STEP)
    delta = -ADAM_LR * (m_hat / (jnp.sqrt(v_hat) + ADAM_EPS) + ADAM_WD * w)
    return delta, m, v


class _SsmDims:
    def __init__(self, groups, state, gch):
        self.g, self.p, self.h = groups, state, gch
        self.d = groups * gch
        self.cb = min(SSM_CH_BLOCK, self.d)
        self.gb = self.cb // gch
        self.ns = self.gb * state
        self.nb = self.d // self.cb


def _ssm_rows(sd, sp):
    gp = sd.g * sd.p
    log_step = jnp.broadcast_to(sp["ssm_log_step"][:, None], (sd.g, sd.p)).reshape(1, gp)
    bt = [sp[k].reshape(gp, sd.h).T for k in ("ssm_b_re", "ssm_b_im")]
    ct = [sp[k].transpose(1, 0, 2).reshape(sd.h, gp) for k in ("ssm_c_re", "ssm_c_im")]
    return (sp["ssm_lambda_re"].reshape(1, gp), sp["ssm_lambda_im"].reshape(1, gp), log_step, *bt, *ct)


def _block_mask(sd):
    row = lax.broadcasted_iota(jnp.int32, (sd.cb, sd.ns), 0) // sd.h
    col = lax.broadcasted_iota(jnp.int32, (sd.cb, sd.ns), 1) // sd.p
    return row == col


def _scan_consts(pr, pi_, reverse):
    if reverse:
        pi_ = [-v for v in pi_]
    shape = (SUBLANES, pr[0].shape[1])
    rows = lax.broadcasted_iota(jnp.int32, shape, 0)
    parts = []
    for d in (1, 2, 4):
        keep = (rows < SUBLANES - d) if reverse else (rows >= d)
        parts += [jnp.where(keep, jnp.broadcast_to(v[d - 1], shape), 0.0) for v in (pr, pi_)]
    order = range(SUBLANES - 1, -1, -1) if reverse else range(SUBLANES)
    parts += [jnp.concatenate([v[t] for t in order], axis=0) for v in (pr, pi_)]
    return jnp.concatenate(parts, axis=0)


def _ssm_operands(sd, rows):
    cb, ns, nb = sd.cb, sd.ns, sd.nb

    def body(lam_re, lam_im, log_step, bt_re, bt_im, ct_re, ct_im, wb_ref, wbt_ref, wc_ref, wct_ref, cst_f_ref, cst_r_ref):
        ar, ai, bbar_re, bbar_im = _ssm_discretised(lam_re[...], lam_im[...], log_step[...], bt_re[...], bt_im[...])
        pr, pi_ = [ar], [ai]
        for _ in range(SUBLANES - 1):
            pr, pi_ = pr + [pr[-1] * ar - pi_[-1] * ai], pi_ + [pr[-1] * ai + pi_[-1] * ar]
        mask = _block_mask(sd)
        spread = lambda src: jnp.where(mask, jnp.concatenate([src] * sd.gb, axis=0), 0.0)
        for j in range(nb):
            at = slice(j * ns, (j + 1) * ns)
            w = jnp.concatenate([spread(bbar_re[:, at]), spread(bbar_im[:, at])], axis=1)
            v = jnp.concatenate([spread(ct_re[:, at]), -spread(ct_im[:, at])], axis=1)
            wb_ref[j] = w.astype(BF16)
            wbt_ref[j] = w.T.astype(BF16)
            wct_ref[j] = v.astype(BF16)
            wc_ref[j] = v.T.astype(BF16)
            pj, qj = [u[:, at] for u in pr], [u[:, at] for u in pi_]
            cst_f_ref[j] = _scan_consts(pj, qj, False)
            cst_r_ref[j] = _scan_consts(pj, qj, True)

    wide, tall = _sds((nb, cb, 2 * ns), BF16), _sds((nb, 2 * ns, cb), BF16)
    cst = _sds((nb, 8 * SUBLANES, ns), F32)
    vm = pl.BlockSpec(memory_space=pltpu.VMEM)
    return pl.pallas_call(body, name="ssm_operands", in_specs=[vm] * 7, out_specs=[vm] * 6,
                          out_shape=[wide, tall, tall, wide, cst, cst],
                          compiler_params=pltpu.CompilerParams(vmem_limit_bytes=VMEM_LIMIT))(*rows)


def _ssm_param_grads(sd, rows, dwb, dwc, da):
    ns, nb, gp = sd.ns, sd.nb, sd.g * sd.p

    def body(lam_re, lam_im, log_step, bt_re, bt_im, dwb_v, dwc_v, da_v, *outs):
        mask = _block_mask(sd)

        def fold(dense):
            kept = jnp.where(mask, dense, 0.0)
            acc = kept[0:sd.h]
            for gl in range(1, sd.gb):
                acc = acc + kept[gl * sd.h:(gl + 1) * sd.h]
            return acc

        lanes = lambda parts: jnp.concatenate(parts, axis=1) if len(parts) > 1 else parts[0]
        dbbar_re = lanes([fold(dwb_v[j][:, :ns]) for j in range(nb)])
        dbbar_im = lanes([fold(dwb_v[j][:, ns:]) for j in range(nb)])
        dwct = [dwc_v[j] for j in range(nb)]
        d_ct_re = lanes([fold(t[:, :ns]) for t in dwct])
        d_ct_im = -lanes([fold(t[:, ns:]) for t in dwct])
        dabar_re = lanes([da_v[j][0:1, :ns] for j in range(nb)])
        dabar_im = lanes([da_v[j][0:1, ns:] for j in range(nb)])
        _, vjp = jax.vjp(_ssm_discretised, lam_re[...], lam_im[...], log_step[...], bt_re[...], bt_im[...])
        d_lr, d_li, d_ls, d_bt_re, d_bt_im = vjp((dabar_re, dabar_im, dbbar_re, dbbar_im))
        group = (lax.broadcasted_iota(jnp.int32, (gp, sd.g), 0) // sd.p == lax.broadcasted_iota(jnp.int32, (gp, sd.g), 1))
        d_log_step = jnp.dot(d_ls, group.astype(F32), precision=lax.Precision.HIGHEST, preferred_element_type=F32)
        for ref, val in zip(outs, (d_lr, d_li, d_log_step, d_bt_re, d_bt_im, d_ct_re, d_ct_im)):
            ref[...] = val

    row, mat = _sds((1, gp), F32), _sds((sd.h, gp), F32)
    vm = pl.BlockSpec(memory_space=pltpu.VMEM)
    return pl.pallas_call(body, name="ssm_param_grads", in_specs=[vm] * 8, out_specs=[vm] * 7,
                          out_shape=[row, row, _sds((1, sd.g), F32), mat, mat, mat, mat],
                          compiler_params=pltpu.CompilerParams(vmem_limit_bytes=VMEM_LIMIT))(*rows[:5], dwb, dwc, da)


def _block_scan(s_ref, cst_ref, carry_ref, sd, rows, reverse):
    ns = sd.ns
    nblk = rows // SUBLANES
    w = min(SCAN_LANES, ns)
    for c0 in range(0, ns, w):
        re_l, im_l = slice(c0, c0 + w), slice(ns + c0, ns + c0 + w)
        cst = [cst_ref[k * SUBLANES:(k + 1) * SUBLANES, c0:c0 + w] for k in range(8)]

        def step(k, carry, re_l=re_l, im_l=im_l, cst=cst):
            local = []
            for b in range(SCAN_BLOCKS):
                blk = SCAN_BLOCKS * k + b
                blk = (nblk - 1 - blk) if reverse else blk
                r0 = pl.multiple_of(blk * SUBLANES, SUBLANES)
                xr = s_ref[pl.ds(r0, SUBLANES), re_l]
                xi = s_ref[pl.ds(r0, SUBLANES), im_l]
                for n, d in enumerate((1, 2, 4)):
                    ar, ai = cst[2 * n], cst[2 * n + 1]
                    shift = (SUBLANES - d) if reverse else d
                    sr = pltpu.roll(xr, shift, 0)
                    si = pltpu.roll(xi, shift, 0)
                    xr, xi = xr + ar * sr - ai * si, xi + ar * si + ai * sr
                local.append((r0, xr, xi))
            cr, ci = carry
            edge = slice(0, 1) if reverse else slice(SUBLANES - 1, SUBLANES)
            for r0, xr, xi in local:
                br = jnp.broadcast_to(cr, xr.shape)
                bi = jnp.broadcast_to(ci, xi.shape)
                xr, xi = xr + cst[6] * br - cst[7] * bi, xi + cst[6] * bi + cst[7] * br
                s_ref[pl.ds(r0, SUBLANES), re_l] = xr
                s_ref[pl.ds(r0, SUBLANES), im_l] = xi
                cr, ci = xr[edge, :], xi[edge, :]
            return cr, ci

        cr, ci = lax.fori_loop(0, nblk // SCAN_BLOCKS, step, (carry_ref[0:1, re_l], carry_ref[0:1, im_l]))
        carry_ref[0:1, re_l] = cr
        carry_ref[0:1, im_l] = ci


def _ssm_fwd(name, sd, z, wb, wc, cst, d_row, tt=512, side=None):
    n_tok = z.shape[0]
    tt = _pick(n_tok, tt, 16)
    cb, ns2 = sd.cb, 2 * sd.ns

    def body(z_ref, wb_ref, wc_ref, cst_ref, d_ref, y_ref, s_ref, a0_ref, carry_ref):
        @pl.when(pl.program_id(1) == 0)
        def _():
            carry_ref[...] = jnp.zeros(carry_ref.shape, F32)
        u = z_ref[...]
        s_ref[...] = jnp.dot(u.astype(BF16), wb_ref[...], preferred_element_type=F32)
        _block_scan(s_ref, cst_ref, carry_ref, sd, tt, reverse=False)
        y = jnp.dot(s_ref[...].astype(BF16), wc_ref[...], preferred_element_type=F32) + d_ref[...] * u
        y_ref[...] = y
        a0_ref[...] = jax.nn.gelu(y).astype(BF16)

    return _call(
        body, side, [z, wb, wc, cst, d_row], name=name, grid=(sd.nb, n_tok // tt),
        in_specs=[pl.BlockSpec((tt, cb), lambda j, i: (i, j)),
                  pl.BlockSpec((None, cb, ns2), lambda j, i: (j, 0, 0)),
                  pl.BlockSpec((None, ns2, cb), lambda j, i: (j, 0, 0)),
                  pl.BlockSpec((None, 8 * SUBLANES, sd.ns), lambda j, i: (j, 0, 0)),
                  pl.BlockSpec((1, cb), lambda j, i: (0, j))],
        out_specs=[pl.BlockSpec((tt, cb), lambda j, i: (i, j)), pl.BlockSpec((tt, ns2), lambda j, i: (i, j)),
                   pl.BlockSpec((tt, cb), lambda j, i: (i, j))],
        out_shape=[_sds((n_tok, sd.d), F32), _sds((n_tok, sd.nb * ns2), F32), _sds((n_tok, sd.d), BF16)],
        scratch_shapes=[pltpu.VMEM((SUBLANES, ns2), F32)],
        compiler_params=_params(("arbitrary", "arbitrary")))


def _ssm_bwd(name, sd, y_pre, dy_direct, dya0, z, states, wct, wbt, cst_rev, d_row, tt=512, side=None):
    n_tok = z.shape[0]
    tt = _pick(n_tok, tt, 16)
    nt = n_tok // tt
    cb, ns, ns2 = sd.cb, sd.ns, 2 * sd.ns
    blocks_per_tile = tt // SUBLANES
    tn_dims = (((0,), (0,)), ((), ()))

    def body(y_ref, dyd_ref, dya0_ref, z_ref, s_ref, sp_ref, wct_ref, wbt_ref, cst_ref, d_ref,
             du_ref, dwb_ref, dwc_ref, da_ref, dd_ref, lam_ref, carry_ref):
        i = pl.program_id(1)

        @pl.when(i == 0)
        def _():
            carry_ref[...] = jnp.zeros(carry_ref.shape, F32)
            dwb_ref[...] = jnp.zeros(dwb_ref.shape, F32)
            dwc_ref[...] = jnp.zeros(dwc_ref.shape, F32)
            da_ref[...] = jnp.zeros(da_ref.shape, F32)
            dd_ref[...] = jnp.zeros(dd_ref.shape, F32)

        _, gelu_vjp = jax.vjp(jax.nn.gelu, y_ref[...])
        dy_t = dyd_ref[...] + gelu_vjp(dya0_ref[...].astype(F32))[0]
        u = z_ref[...]
        dy16 = dy_t.astype(BF16)
        lam_ref[...] = jnp.dot(dy16, wct_ref[...], preferred_element_type=F32)
        _block_scan(lam_ref, cst_ref, carry_ref, sd, tt, reverse=True)
        lam = lam_ref[...]
        lam16 = lam.astype(BF16)
        du_ref[...] = (jnp.dot(lam16, wbt_ref[...], preferred_element_type=F32) + d_ref[...] * dy_t).astype(BF16)
        dd_ref[0:1, :] += jnp.sum(dy_t * u, axis=0, keepdims=True)
        dwb_ref[...] += lax.dot_general(u.astype(BF16), lam16, tn_dims, preferred_element_type=F32)
        s = s_ref[...]
        dwc_ref[...] += lax.dot_general(dy16, s.astype(BF16), tn_dims, preferred_element_type=F32)
        before = jnp.where(i == nt - 1, 0.0, 1.0) * sp_ref[SUBLANES - 1:SUBLANES, :]
        first_row = lax.broadcasted_iota(jnp.int32, s.shape, 0) == 0
        prev = jnp.where(first_row, jnp.broadcast_to(before, s.shape), pltpu.roll(s, 1, 0))
        lr, li = lam[:, :ns], lam[:, ns:]
        pr, pi_ = prev[:, :ns], prev[:, ns:]
        da_ref[0:1, 0:ns] += jnp.sum(lr * pr + li * pi_, axis=0, keepdims=True)
        da_ref[0:1, ns:ns2] += jnp.sum(li * pr - lr * pi_, axis=0, keepdims=True)

    rev = lambda i: nt - 1 - i
    return _call(
        body, side, [y_pre, dy_direct, dya0, z, states, states, wct, wbt, cst_rev, d_row], name=name, grid=(sd.nb, nt),
        in_specs=[pl.BlockSpec((tt, cb), lambda j, i: (rev(i), j)),
                  pl.BlockSpec((tt, cb), lambda j, i: (rev(i), j)),
                  pl.BlockSpec((tt, cb), lambda j, i: (rev(i), j)),
                  pl.BlockSpec((tt, cb), lambda j, i: (rev(i), j)),
                  pl.BlockSpec((tt, ns2), lambda j, i: (rev(i), j)),
                  pl.BlockSpec((SUBLANES, ns2), lambda j, i: (jnp.maximum(rev(i) * blocks_per_tile - 1, 0), j)),
                  pl.BlockSpec((None, cb, ns2), lambda j, i: (j, 0, 0)),
                  pl.BlockSpec((None, ns2, cb), lambda j, i: (j, 0, 0)),
                  pl.BlockSpec((None, 8 * SUBLANES, ns), lambda j, i: (j, 0, 0)),
                  pl.BlockSpec((1, cb), lambda j, i: (0, j))],
        out_specs=[pl.BlockSpec((tt, cb), lambda j, i: (rev(i), j)),
                   pl.BlockSpec((None, cb, ns2), lambda j, i: (j, 0, 0)),
                   pl.BlockSpec((None, cb, ns2), lambda j, i: (j, 0, 0)),
                   pl.BlockSpec((None, SUBLANES, ns2), lambda j, i: (j, 0, 0)),
                   pl.BlockSpec((None, SUBLANES, cb), lambda j, i: (j, 0, 0))],
        out_shape=[_sds((n_tok, sd.d), BF16), _sds((sd.nb, cb, ns2), F32), _sds((sd.nb, cb, ns2), F32),
                   _sds((sd.nb, SUBLANES, ns2), F32), _sds((sd.nb, SUBLANES, cb), F32)],
        scratch_shapes=[pltpu.VMEM((tt, ns2), F32), pltpu.VMEM((SUBLANES, ns2), F32)],
        compiler_params=_params(("arbitrary", "arbitrary")))


def _hosted(exch, fn, name, *args, **kw):
    side = exch.side(name)
    if side is None:
        return fn(name, *args, **kw)
    out, moved = fn(name, *args, side=side, **kw)
    exch.done(name, moved)
    return out


def _local_grads(x, p, target, sp, exch):
    n_tok, d_model = x.shape
    d_ssm = sp["ssm_d"].shape[0] * sp["ssm_d"].shape[1]
    d_sgu = sp["sgu_ln_g"].shape[-1]
    sd = _SsmDims(sp["ssm_b_re"].shape[0], sp["ssm_b_re"].shape[1], sp["ssm_b_re"].shape[2])
    heads, chunk, _ = sp["sgu_w"].shape
    row = lambda v: v.reshape(1, -1)
    tok = lambda w, dt=F32: _sds((n_tok, w), dt)
    acc = lambda w: _sds((1, w), F32)

    g_mix = row(sp["norm_mix_g"])
    (h1,) = _hosted(exch, _rowwise, "norm_mix", lambda a, g: _rms(a, g), [x], [g_mix], [tok(d_model, BF16)])
    z = _hosted(exch, _mm_nn, "proj_in", h1, exch.weight("w_in"), sharded=True, tn=768)

    ssm_rows = _ssm_rows(sd, sp)
    wb, wbt, wc, wct, cst_fwd, cst_rev = _ssm_operands(sd, ssm_rows)
    d_row = row(sp["ssm_d"])
    y_pre, states, ya0_16 = _hosted(exch, _ssm_fwd, "ssm_fwd", sd, z, wb, wc, cst_fwd, d_row)
    q = _mm_nn("ssm_glu", ya0_16, exch.weight("ssm_glu_w"), tm=1024)
    glu_b, g_ossm = row(sp["ssm_glu_b"]), row(sp["out_norm_ssm_g"])
    (ya_n,) = _rowwise("ssm_glu_out", _glu_out, [y_pre, q], [glu_b, g_ossm], [tok(d_ssm, BF16)])

    assert d_ssm == d_sgu
    zu, zv = _Cols(z, d_sgu, 1), _Cols(z, d_sgu, 2)
    ln_g, ln_b, g_osgu = row(sp["sgu_ln_g"]), row(sp["sgu_ln_b"]), row(sp["out_norm_sgu_g"])
    b_st = sp["sgu_b"].T
    sgu_tr = 2 * chunk

    def sgu_joined(ya_t, zu_t, zv_t, *params):
        return jnp.concatenate([ya_t, _sgu_rows(zu_t, zv_t, *params).astype(BF16)], axis=1)

    (ycat,) = _rowwise("sgu", sgu_joined, [ya_n, zu, zv], [ln_g, ln_b, sp["sgu_w"], b_st, g_osgu],
                       [tok(d_ssm + d_sgu, BF16)], tr=sgu_tr)
    x1 = _hosted(exch, _mm_nn, "proj_out", ycat, exch.weight("w_out"), res=x, tm=1024)

    g_ffn = row(sp["norm_ffn_g"])
    (h2,) = _rowwise("norm_ffn", lambda a, g: _rms(a, g), [x1], [g_ffn], [tok(d_model, BF16)])
    act, gu16 = _hosted(exch, _ffn_in_swiglu, "ffn_in", h2, exch.weight("w_ffn_in"))
    x2 = _mm_nn("ffn_out", act, exch.weight("w_ffn_out"), res=x1)

    g_ple = row(sp["norm_ple_g"])
    (h3,) = _rowwise("norm_ple", lambda a, g: _rms(a, g), [x2], [g_ple], [tok(d_model, BF16)])
    gpre = _mm_nn("ple_gate", h3, exch.weight("w_ple_gate"), tm=1024)
    (p16,) = _rowwise("ple_cast", lambda a: a, [p], [], [tok(p.shape[1], BF16)])
    pp = _mm_nn("ple_proj", p16, exch.weight("w_ple_proj"), sharded=True, tm=1024)

    b_g, g_fin = row(sp["b_ple_gate"]), row(sp["final_norm_g"])

    def head(x2_t, gpre_t, pp_t, tgt_t, b_g_v, g_fin_v):
        loss, grads = jax.value_and_grad(_head_loss, argnums=(0, 1, 2, 3, 4))(x2_t, gpre_t, pp_t, b_g_v, g_fin_v, tgt_t)
        dx2, dgpre, dpp, db, dg = grads
        return dx2, dgpre.astype(BF16), dpp.astype(BF16), jnp.full((1, LANES), loss, F32), db, dg

    dx2_head, dgpre16, dpp16, loss_row, d_b_g, d_g_fin = _rowwise(
        "head", head, [x2, gpre, pp, target], [b_g, g_fin],
        [tok(d_model), tok(d_model, BF16), tok(d_model, BF16)], [acc(LANES), acc(d_model), acc(d_model)])
    loss = loss_row[0, 0]
    exch.small_grads({"loss": loss_row})

    exch.grad("w_ple_proj", _mm_tn("d_ple_proj", p16, dpp16, shards=N_CHIPS, tk=256))
    exch.grad("w_ple_gate", _mm_tn("d_ple_gate", h3, dgpre16))
    dh3 = _mm_nt("d_h3", dgpre16, exch.weight("w_ple_gate"), out_dtype=BF16, tm=1024)

    def norm_bwd(x_t, dres_t, dh_t, g_v):
        _, vjp = jax.vjp(_rms, x_t, g_v)
        dx, dg = vjp(dh_t.astype(F32))
        dx = dres_t + dx
        return dx, dx.astype(BF16), dg

    dx2, dx2_16, d_g_ple = _rowwise("d_norm_ple", norm_bwd, [x2, dx2_head, dh3], [g_ple],
                                    [tok(d_model), tok(d_model, BF16)], [acc(d_model)])
    exch.grad("w_ffn_out", _mm_tn("d_ffn_out", act, dx2_16))
    dgu16 = _hosted(exch, _d_act_swiglu, "d_act", dx2_16, exch.weight("w_ffn_out"), gu16)
    exch.grad("w_ffn_in", _hosted(exch, _mm_tn, "d_ffn_in", h2, dgu16, shards=N_CHIPS, g_halves=True, tn=1408, g_resident=True))
    dh2 = _hosted(exch, _mm_nt, "d_h2", dgu16, exch.weight("w_ffn_in"), sharded=True, g_halves=True, out_dtype=BF16, tm=256, w_resident=True)
    dx1, dx1_16, d_g_ffn = _rowwise("d_norm_ffn", norm_bwd, [x1, dx2, dh2], [g_ffn],
                                    [tok(d_model), tok(d_model, BF16)], [acc(d_model)])
    exch.grad("w_out", _mm_tn("d_proj_out", ycat, dx1_16))
    dycat = _mm_nt("d_ycat", dx1_16, exch.weight("w_out"), out_dtype=BF16, tm=1024)

    def glu_out_bwd(y_pre_t, q_t, dy_t, glu_b_v, g_v):
        _, vjp = jax.vjp(_glu_out, y_pre_t, q_t, glu_b_v, g_v)
        dy_pre, dq, db, dg = vjp(dy_t.astype(F32))
        return dy_pre, dq.astype(BF16), db, dg

    dy_pre_a, dq16, d_glu_b, d_g_ossm = _rowwise(
        "d_ssm_glu_out", glu_out_bwd, [y_pre, q, _Cols(dycat, d_ssm, 0)], [glu_b, g_ossm],
        [tok(d_ssm), tok(d_ssm, BF16)], [acc(d_ssm), acc(d_ssm)])
    exch.grad("ssm_glu_w", _mm_tn("d_ssm_glu", ya0_16, dq16))
    dya0 = _hosted(exch, _mm_nt, "d_ya0", dq16, exch.weight("ssm_glu_w"), out_dtype=BF16, tm=1024)

    dz_ssm16, dwb, dwc, da, dd = _hosted(exch, _ssm_bwd, "ssm_bwd", sd, y_pre, dy_pre_a, dya0, z, states, wct, wbt,
                                         cst_rev, d_row)

    def sgu_bwd(dz_ssm_t, zu_t, zv_t, dy_t, ln_g_v, ln_b_v, w_v, b_v, g_v):
        _, vjp = jax.vjp(_sgu_rows, zu_t, zv_t, ln_g_v, ln_b_v, w_v, b_v, g_v)
        dzu, dzv, dlg, dlb, dw, db, dg = vjp(dy_t.astype(F32))
        return jnp.concatenate([dz_ssm_t, dzu.astype(BF16), dzv.astype(BF16)], axis=1), dlg, dlb, dw, db, dg

    dz16, d_ln_g, d_ln_b, d_sgu_w, d_b_st, d_g_osgu = _hosted(
        exch, _rowwise, "d_sgu", sgu_bwd, [dz_ssm16, zu, zv, _Cols(dycat, d_sgu, 1)], [ln_g, ln_b, sp["sgu_w"], b_st, g_osgu],
        [tok(d_ssm + 2 * d_sgu, BF16)],
        [acc(d_sgu), acc(d_sgu), _sds(sp["sgu_w"].shape, F32), _sds(b_st.shape, F32), acc(d_sgu)], tr=sgu_tr)

    d_lam_re, d_lam_im, d_log_step, d_bt_re, d_bt_im, d_ct_re, d_ct_im = _ssm_param_grads(sd, ssm_rows, dwb, dwc, da)
    d_b_re, d_b_im = d_bt_re.T, d_bt_im.T
    d_c_re, d_c_im = (t.reshape(sd.h, sd.g, sd.p).transpose(1, 0, 2) for t in (d_ct_re, d_ct_im))
    d_ssm_d = dd[:, 0, :].reshape(sd.g, sd.h)

    exch.small_grads({
        "ssm_lambda_re": d_lam_re, "ssm_lambda_im": d_lam_im, "ssm_log_step": d_log_step,
        "ssm_b_re": d_b_re, "ssm_b_im": d_b_im, "ssm_c_re": d_c_re, "ssm_c_im": d_c_im, "ssm_d": d_ssm_d,
        "ssm_glu_b": d_glu_b, "sgu_ln_g": d_ln_g, "sgu_ln_b": d_ln_b, "sgu_w": d_sgu_w, "sgu_b": d_b_st.T,
        "out_norm_ssm_g": d_g_ossm, "out_norm_sgu_g": d_g_osgu, "norm_ffn_g": d_g_ffn, "norm_ple_g": d_g_ple,
        "b_ple_gate": d_b_g, "final_norm_g": d_g_fin,
    })

    exch.grad("w_in", _hosted(exch, _mm_tn, "d_proj_in", h1, dz16, shards=N_CHIPS, tn=768))
    dh1 = _hosted(exch, _mm_nt, "d_h1", dz16, exch.weight("w_in"), sharded=True, out_dtype=BF16, tm=1024)

    def norm_in_bwd(x_t, dres_t, dh_t, g_v):
        _, vjp = jax.vjp(_rms, x_t, g_v)
        dx, dg = vjp(dh_t.astype(F32))
        return dres_t + dx, dg

    grad_x, d_g_mix = _hosted(exch, _rowwise, "d_norm_mix", norm_in_bwd, [x, dx1, dh1], [g_mix], [tok(d_model)], [acc(d_model)])
    exch.small_grads({"norm_mix_g": d_g_mix})
    return loss, grad_x


def _place():
    x, y, c = lax.axis_index("x"), lax.axis_index("y"), lax.axis_index("c")
    chips = [(1 - x, y), (x, 1 - y), (1 - x, 1 - y)]
    return x, y, c, chips


def _cast_into_slot(name, w2d, shard, tr=256):
    rows, cols = w2d.shape
    rh = rows // 2
    tr = _pick(rh, tr, 16)
    per = rh // tr

    def body(s_ref, a_ref, o_ref):
        o_ref[...] = a_ref[...].astype(BF16)

    grid_spec = pltpu.PrefetchScalarGridSpec(
        num_scalar_prefetch=1, grid=(2, per),
        in_specs=[pl.BlockSpec((tr, cols), lambda h, i, s_ref: (h * per + i, 0))],
        out_specs=pl.BlockSpec((None, None, tr, cols), lambda h, i, s_ref: (s_ref[0], h, i, 0)))
    return pl.pallas_call(body, name=name, grid_spec=grid_spec, out_shape=_sds((N_CHIPS, 2, rh, cols), BF16),
                          compiler_params=_params(("arbitrary", "arbitrary")))(shard.reshape(1).astype(jnp.int32), w2d)


def _exchange_alone(name, side):
    n_in, n_out = len(side.ins), len(side.out_shapes)

    def body(*refs):
        ins, outs, sems = refs[:n_in], refs[n_in:n_in + n_out], refs[n_in + n_out:]
        side.first(ins, outs, *sems)
        if side.mid is not None:
            side.mid(ins, outs, *sems)
        side.last(ins, outs, *sems)

    return pl.pallas_call(
        body, name=name, in_specs=[ANY] * n_in, out_specs=[ANY] * n_out, out_shape=side.out_shapes,
        input_output_aliases=side.aliases,
        scratch_shapes=[pltpu.SemaphoreType.DMA((side.n_sems,)), pltpu.SemaphoreType.DMA((side.n_sems,))],
    )(*side.ins)


def _gather_side(slots, parts=None, mid_late=False):
    n = len(slots)
    parts = parts or [(0, GATHER_PARTS)] * n

    def copies(kind, outs, send_sems, recv_sems):
        x, y, c, chips = _place()

        def remote(k, w, shard, half, to):
            unit = outs[w].shape[2] // GATHER_PARTS
            lo, hi = parts[w]
            ref = outs[w].at[shard, half, pl.ds(lo * unit, (hi - lo) * unit), :]
            return pltpu.make_async_remote_copy(src_ref=ref, dst_ref=ref, send_sem=send_sems.at[k], recv_sem=recv_sems.at[k],
                                                device_id=to, device_id_type=MESH)

        pairs = [(w, j, 2 * cx + cy, (cx, cy)) for w in range(n) for j, (cx, cy) in enumerate(chips)]
        if kind == "sends":
            return [remote(3 * w + j, w, 2 * x + y, c, (*chip, c)) for w, j, _, chip in pairs]
        if kind == "arrivals":
            return [remote(3 * w + j, w, s, c, (x, y, c)) for w, j, s, _ in pairs]
        if kind == "passed":
            return [remote(3 * n + 3 * w + j, w, s, c, (x, y, 1 - c)) for w, j, s, _ in pairs]
        return [remote(3 * n + 3 * w + j, w, s, 1 - c, (x, y, c)) for w, j, s, _ in pairs]

    def first(ins, outs, *sems):
        for cp in copies("sends", outs, *sems):
            cp.start()

    def mid(ins, outs, *sems):
        for arrived, onward in zip(copies("arrivals", outs, *sems), copies("passed", outs, *sems)):
            arrived.wait_recv()
            onward.start()

    def last(ins, outs, *sems):
        for cp in copies("from_sibling", outs, *sems):
            cp.wait_recv()
        for cp in copies("sends", outs, *sems) + copies("passed", outs, *sems):
            cp.wait_send()

    return _Side(slots, [_sds(s.shape, s.dtype) for s in slots], 6 * n, first, last, mid=mid, aliases={w: w for w in range(n)},
                 mid_late=mid_late)


def _swap_side(grads):
    n = len(grads)

    def copies(ins, outs, send_sems, recv_sems):
        x, y, c, _ = _place()
        return [pltpu.make_async_remote_copy(src_ref=ins[w].at[:, 1 - c], dst_ref=outs[w], send_sem=send_sems.at[w],
                                             recv_sem=recv_sems.at[w], device_id=(x, y, 1 - c), device_id_type=MESH)
                for w in range(n)]

    def first(*refs):
        for cp in copies(*refs):
            cp.start()

    def last(*refs):
        for cp in copies(*refs):
            cp.wait()

    return _Side(grads, [_sds((g.shape[0], *g.shape[2:]), g.dtype) for g in grads], n, first, last)


def _scatter_side(halves):
    n = len(halves)

    def copies(ins, outs, send_sems, recv_sems):
        x, y, c, chips = _place()
        return [pltpu.make_async_remote_copy(
            src_ref=ins[w].at[2 * cx + cy], dst_ref=outs[w].at[j], send_sem=send_sems.at[3 * w + j],
            recv_sem=recv_sems.at[3 * w + j], device_id=(cx, cy, c), device_id_type=MESH)
            for w in range(n) for j, (cx, cy) in enumerate(chips)]

    def first(*refs):
        for cp in copies(*refs):
            cp.start()

    def last(*refs):
        for cp in copies(*refs):
            cp.wait()

    return _Side(halves, [_sds((3, *h.shape[1:]), h.dtype) for h in halves], 3 * n, first, last)


def _join_halves(name, slots):
    n = len(slots)

    def body(*refs):
        outs = refs[n:2 * n]
        send_sems, recv_sems = refs[2 * n:]
        x, y, c, _ = _place()

        def copy(w, half, to):
            return pltpu.make_async_remote_copy(src_ref=outs[w].at[half], dst_ref=outs[w].at[half], send_sem=send_sems.at[w],
                                                recv_sem=recv_sems.at[w], device_id=to, device_id_type=MESH)

        copies = [copy(w, c, (x, y, 1 - c)) for w in range(n)]
        for cp in copies:
            cp.start()
        for w in range(n):
            copy(w, 1 - c, (x, y, c)).wait_recv()
        for cp in copies:
            cp.wait_send()

    return pl.pallas_call(
        body, name=name, in_specs=[ANY] * n, out_specs=[ANY] * n,
        out_shape=[_sds(s.shape, s.dtype) for s in slots], input_output_aliases={w: w for w in range(n)},
        scratch_shapes=[pltpu.SemaphoreType.DMA((n,)), pltpu.SemaphoreType.DMA((n,))],
    )(*slots)


def _allreduce_small(block, tr=256):
    rows, lanes = block.shape
    tr = _pick(rows, tr, SUBLANES)

    def body(x_ref, o_ref, buf, send_sems, recv_sems):
        x, y, c, chips = _place()
        me, sibling = (x, y, c), (x, y, 1 - c)

        def slot(px, py, pc):
            return buf.at[4 * px + 2 * py + pc]

        def copy(k, block_of, to):
            return pltpu.make_async_remote_copy(src_ref=slot(*block_of), dst_ref=slot(*block_of), send_sem=send_sems.at[k],
                                                recv_sem=recv_sems.at[k], device_id=to, device_id_type=MESH)

        slot(*me)[...] = x_ref[...]
        first = [copy(0, me, sibling)] + [copy(1 + j, me, (*chip, c)) for j, chip in enumerate(chips)]
        for cp in first:
            cp.start()
        passed = [copy(4 + j, (*chip, c), sibling) for j, chip in enumerate(chips)]
        for j, chip in enumerate(chips):
            copy(1 + j, (*chip, c), me).wait_recv()
            passed[j].start()
        copy(0, sibling, me).wait_recv()
        for j, chip in enumerate(chips):
            copy(4 + j, (*chip, 1 - c), me).wait_recv()
        for cp in first + passed:
            cp.wait_send()
        for r0 in range(0, rows, tr):
            acc = buf[0, r0:r0 + tr, :]
            for k in range(1, N_DEV):
                acc = acc + buf[k, r0:r0 + tr, :]
            o_ref[r0:r0 + tr, :] = acc

    vm = pl.BlockSpec(memory_space=pltpu.VMEM)
    return pl.pallas_call(
        body, name="allreduce_small", in_specs=[vm], out_specs=vm, out_shape=_sds((rows, lanes), block.dtype),
        scratch_shapes=[pltpu.VMEM((N_DEV, rows, lanes), block.dtype), pltpu.SemaphoreType.DMA((7,)), pltpu.SemaphoreType.DMA((7,))],
        compiler_params=pltpu.CompilerParams(vmem_limit_bytes=VMEM_LIMIT),
    )(block)


def _small_gather_side(block):
    def copy(kind, j, ins, outs, send_sems, recv_sems):
        x, y, c, chips = _place()
        chip = chips[j] if j is not None else None
        slot = lambda px, py, pc: outs[0].at[4 * px + 2 * py + pc]

        def remote(k, src, dst, to):
            return pltpu.make_async_remote_copy(src_ref=src, dst_ref=dst, send_sem=send_sems.at[k], recv_sem=recv_sems.at[k],
                                                device_id=to, device_id_type=MESH)

        if kind == "to_sibling":
            return remote(0, ins[0], slot(x, y, c), (x, y, 1 - c))
        if kind == "from_sibling":
            return remote(0, ins[0], slot(x, y, 1 - c), (x, y, c))
        if kind == "to_chip":
            return remote(1 + j, ins[0], slot(x, y, c), (*chip, c))
        if kind == "from_chip":
            return remote(1 + j, ins[0], slot(*chip, c), (x, y, c))
        if kind == "pass_on":
            return remote(4 + j, slot(*chip, c), slot(*chip, c), (x, y, 1 - c))
        return remote(4 + j, slot(*chip, 1 - c), slot(*chip, 1 - c), (x, y, c))

    def first(*refs):
        copy("to_sibling", None, *refs).start()
        for j in range(3):
            copy("to_chip", j, *refs).start()

    def mid(*refs):
        for j in range(3):
            copy("from_chip", j, *refs).wait_recv()
            copy("pass_on", j, *refs).start()

    def last(*refs):
        copy("from_sibling", None, *refs).wait_recv()
        for j in range(3):
            copy("passed_on", j, *refs).wait_recv()
        copy("to_sibling", None, *refs).wait_send()
        for j in range(3):
            copy("to_chip", j, *refs).wait_send()
            copy("pass_on", j, *refs).wait_send()

    return _Side([block], [_sds((N_DEV, *block.shape), block.dtype)], 7, first, last, mid=mid, mid_late=True)


def _sum_slots(name, own, gathered, me, tr=512):
    n, rows, cols = gathered.shape
    tr = _pick(rows, tr, SUBLANES)
    if tr < 64:
        tr = rows

    def body(me_ref, own_ref, g_ref, o_ref):
        mine = own_ref[...]
        acc = jnp.where(me_ref[0] == 0, mine, g_ref[0])
        for k in range(1, n):
            acc = acc + jnp.where(me_ref[0] == k, mine, g_ref[k])
        o_ref[...] = acc

    grid_spec = pltpu.PrefetchScalarGridSpec(
        num_scalar_prefetch=1, grid=(rows // tr,),
        in_specs=[pl.BlockSpec((tr, cols), lambda i, me_ref: (i, 0)), pl.BlockSpec((n, tr, cols), lambda i, me_ref: (0, i, 0))],
        out_specs=pl.BlockSpec((tr, cols), lambda i, me_ref: (i, 0)))
    return pl.pallas_call(body, name=name, grid_spec=grid_spec, out_shape=_sds((rows, cols), own.dtype),
                          compiler_params=_params(("arbitrary",)))(me.reshape(1).astype(jnp.int32), own, gathered)


def _sum_received(name, full, c, shard, swapped, received, tr=256):
    n, rows, cols = received.shape
    tr = _pick(rows, tr, 16)

    def body(i_ref, a_ref, b_ref, s_ref, o_ref):
        acc = a_ref[...] + b_ref[...]
        for k in range(n):
            acc = acc + s_ref[k].astype(F32)
        o_ref[...] = acc

    grid_spec = pltpu.PrefetchScalarGridSpec(
        num_scalar_prefetch=1, grid=(rows // tr,),
        in_specs=[pl.BlockSpec((None, None, tr, cols), lambda i, i_ref: (i_ref[1], i_ref[0], i, 0)),
                  pl.BlockSpec((None, tr, cols), lambda i, i_ref: (i_ref[1], i, 0)),
                  pl.BlockSpec((n, tr, cols), lambda i, i_ref: (0, i, 0))],
        out_specs=pl.BlockSpec((None, tr, cols), lambda i, i_ref: (i_ref[0], i, 0)))
    return pl.pallas_call(body, name=name, grid_spec=grid_spec, out_shape=_sds((2, rows, cols), F32),
                          compiler_params=_params(("arbitrary",)))(jnp.stack([c, shard]).astype(jnp.int32), full, swapped, received)


def _add_halves(name, full, c, shard, received, tr=256):
    s, _, rh, cols = full.shape
    tr = _pick(rh, tr, 16)

    def body(i_ref, a_ref, b_ref, o_ref):
        o_ref[...] = (a_ref[...] + b_ref[...]).astype(BF16)

    other = lambda q, i_ref: (i_ref[1] + 1 + q) % s
    grid_spec = pltpu.PrefetchScalarGridSpec(
        num_scalar_prefetch=1, grid=(s - 1, rh // tr),
        in_specs=[pl.BlockSpec((None, None, tr, cols), lambda q, i, i_ref: (other(q, i_ref), i_ref[0], i, 0)),
                  pl.BlockSpec((None, tr, cols), lambda q, i, i_ref: (other(q, i_ref), i, 0))],
        out_specs=pl.BlockSpec((None, tr, cols), lambda q, i, i_ref: (other(q, i_ref), i, 0)))
    return pl.pallas_call(body, name=name, grid_spec=grid_spec, out_shape=_sds((s, rh, cols), BF16),
                          compiler_params=_params(("arbitrary", "arbitrary")))(jnp.stack([c, shard]).astype(jnp.int32), full, received)


LARGE = ("w_in", "ssm_glu_w", "w_out", "w_ffn_in", "w_ffn_out", "w_ple_gate", "w_ple_proj")
COLUMN_SHARDED = ("w_in", "w_ffn_in", "w_ple_proj")
SMALL = ("norm_mix_g", "ssm_lambda_re", "ssm_lambda_im", "ssm_log_step", "ssm_b_re", "ssm_b_im", "ssm_c_re", "ssm_c_im",
         "ssm_d", "ssm_glu_b", "sgu_ln_g", "sgu_ln_b", "sgu_w", "sgu_b", "out_norm_ssm_g", "out_norm_sgu_g", "norm_ffn_g",
         "norm_ple_g", "b_ple_gate", "final_norm_g")
WEIGHTS = ("norm_mix_g", "w_in", "ssm_lambda_re", "ssm_lambda_im", "ssm_log_step", "ssm_b_re", "ssm_b_im", "ssm_c_re",
           "ssm_c_im", "ssm_d", "ssm_glu_w", "ssm_glu_b", "sgu_ln_g", "sgu_ln_b", "sgu_w", "sgu_b", "out_norm_ssm_g",
           "out_norm_sgu_g", "w_out", "norm_ffn_g", "w_ffn_in", "w_ffn_out", "norm_ple_g", "w_ple_gate", "b_ple_gate",
           "w_ple_proj", "final_norm_g")
PACK_ROWS = SUBLANES * LANES


def _pack(arrays):
    parts = []
    for a in arrays:
        flat = a.reshape(-1).astype(F32)
        pad = -flat.shape[0] % PACK_ROWS
        parts.append(jnp.pad(flat, (0, pad)) if pad else flat)
    return jnp.concatenate(parts).reshape(-1, LANES)


def _unpack(packed, like):
    flat = packed.reshape(-1)
    out, at = [], 0
    for a in like:
        size = a.size
        out.append(flat[at:at + size].reshape(a.shape))
        at += size + (-size % PACK_ROWS)
    return out


class _NoExchange:
    def __init__(self, weights):
        self.weights, self.grads, self.small = weights, {}, {}

    def weight(self, name):
        return self.weights[name]

    def grad(self, name, g):
        self.grads[name] = g

    def small_grads(self, grads):
        self.small.update(grads)

    def side(self, host):
        return None


class _MeshExchange:
    GATHER = {"norm_mix": (("w_in", 0, 16),),
              "proj_in": (("ssm_glu_w", 0, 16), ("w_out", 0, 16), ("w_ffn_in", 0, 1)),
              "ssm_fwd": (("w_ffn_in", 1, 13),),
              "proj_out": (("w_ffn_in", 13, 16),),
              "ffn_in": (("w_ffn_out", 0, 16), ("w_ple_gate", 0, 16), ("w_ple_proj", 0, 16))}
    GATHER_LONG = ("norm_mix", "proj_in", "ssm_fwd", "proj_out")
    SWAP = {"d_act": ("w_ple_proj", "w_ple_gate", "w_ffn_out"), "d_h2": ("w_ffn_in",), "d_ya0": ("w_out", "ssm_glu_w")}
    SWAP_ALONE = ("w_in",)
    SCATTER = {"d_ffn_in": ("w_ple_proj", "w_ple_gate", "w_ffn_out"), "ssm_bwd": ("w_ffn_in",),
               "d_sgu": ("w_out", "ssm_glu_w"), "d_h1": ("w_in",)}
    SMALL_GATHER = "d_proj_in"

    def __init__(self, shards, small_like, c, shard, me):
        self.c, self.shard, self.me, self.small_like = c, shard, me, small_like
        self.slots = {k: _cast_into_slot("cast_" + k, shards[k], shard) for k in LARGE}
        self.full, self.received, self.halves, self.quarters, self.small = {}, {}, {}, {}, {}

    def weight(self, name):
        g = self.slots[name]
        _, _, rh, cols = g.shape
        return g.reshape(N_CHIPS, 2 * rh, cols) if name in COLUMN_SHARDED else g.reshape(N_CHIPS * 2 * rh, cols)

    def grad(self, name, g):
        if name not in COLUMN_SHARDED:
            g = g.reshape(N_CHIPS, g.shape[0] // N_CHIPS, g.shape[1])
        self.full[name] = g.reshape(N_CHIPS, 2, g.shape[1] // 2, g.shape[2])
        if name in self.SWAP_ALONE:
            self._swapped((name,), _exchange_alone("grad_swap_" + name, _swap_side([self.full[name]])))

    def _swapped(self, names, received):
        for k, r in zip(names, received):
            self.received[k] = r
            self.halves[k] = _add_halves("grad_add_halves_" + k, self.full[k], self.c, self.shard, r)

    def small_grads(self, grads):
        self.small.update(grads)

    def _packed(self, names):
        return _pack([self.small[k].reshape(self.small_like[k].shape) for k in names])

    def side(self, host):
        if host in self.GATHER:
            return _gather_side([self.slots[k] for k, _, _ in self.GATHER[host]], [(lo, hi) for _, lo, hi in self.GATHER[host]],
                                mid_late=host in self.GATHER_LONG)
        if host in self.SWAP:
            return _swap_side([self.full[k] for k in self.SWAP[host]])
        if host in self.SCATTER:
            return _scatter_side([self.halves[k] for k in self.SCATTER[host]])
        if host == self.SMALL_GATHER:
            self.packed_early = self._packed(SMALL[1:] + ("loss",))
            return _small_gather_side(self.packed_early)
        return None

    def done(self, host, moved):
        if host in self.GATHER:
            self.slots.update(zip([k for k, _, _ in self.GATHER[host]], moved))
        elif host in self.SWAP:
            self._swapped(self.SWAP[host], moved)
        elif host in self.SCATTER:
            self.quarters.update(zip(self.SCATTER[host], moved))
        else:
            (self.gathered_early,) = moved

    def small_reduced(self):
        early = _sum_slots("small_sum", self.packed_early, self.gathered_early, self.me)
        late = _allreduce_small(self._packed(SMALL[:1]))
        loss_at = early.shape[0] - PACK_ROWS // LANES
        return jnp.concatenate([late, early[:loss_at]], axis=0), early[loss_at, 0]

    def reduced(self):
        parts = [_sum_received("grad_sum_" + k, self.full[k], self.c, self.shard, self.received[k], self.quarters[k]) for k in LARGE]
        joined = _join_halves("grad_join", parts)
        return {k: j.reshape(2 * j.shape[1], j.shape[2]) for k, j in zip(LARGE, joined)}


def kernel(x, p, norm_mix_g, w_in, ssm_lambda_re, ssm_lambda_im, ssm_log_step, ssm_b_re, ssm_b_im, ssm_c_re, ssm_c_im, ssm_d, ssm_glu_w, ssm_glu_b, sgu_ln_g, sgu_ln_b, sgu_w, sgu_b, out_norm_ssm_g, out_norm_sgu_g, w_out, norm_ffn_g, w_ffn_in, w_ffn_out, norm_ple_g, w_ple_gate, b_ple_gate, w_ple_proj, final_norm_g, loss_target, m_norm_mix_g, m_w_in, m_ssm_lambda_re, m_ssm_lambda_im, m_ssm_log_step, m_ssm_b_re, m_ssm_b_im, m_ssm_c_re, m_ssm_c_im, m_ssm_d, m_ssm_glu_w, m_ssm_glu_b, m_sgu_ln_g, m_sgu_ln_b, m_sgu_w, m_sgu_b, m_out_norm_ssm_g, m_out_norm_sgu_g, m_w_out, m_norm_ffn_g, m_w_ffn_in, m_w_ffn_out, m_norm_ple_g, m_w_ple_gate, m_b_ple_gate, m_w_ple_proj, m_final_norm_g, v_norm_mix_g, v_w_in, v_ssm_lambda_re, v_ssm_lambda_im, v_ssm_log_step, v_ssm_b_re, v_ssm_b_im, v_ssm_c_re, v_ssm_c_im, v_ssm_d, v_ssm_glu_w, v_ssm_glu_b, v_sgu_ln_g, v_sgu_ln_b, v_sgu_w, v_sgu_b, v_out_norm_ssm_g, v_out_norm_sgu_g, v_w_out, v_norm_ffn_g, v_w_ffn_in, v_w_ffn_out, v_norm_ple_g, v_w_ple_gate, v_b_ple_gate, v_w_ple_proj, v_final_norm_g):
    given = dict(locals())
    w = {k: given[k] for k in WEIGHTS}
    m = {k: given["m_" + k] for k in WEIGHTS}
    v = {k: given["v_" + k] for k in WEIGHTS}
    c = lax.axis_index("c")
    shard = 2 * lax.axis_index("x") + lax.axis_index("y")

    small_like = {k: w[k] for k in SMALL}
    small_like["loss"] = _sds((1, LANES), F32)
    exch = _MeshExchange({k: w[k].reshape(w[k].shape[1:]) for k in LARGE}, small_like, c, shard, 2 * shard + c)
    unlayer = lambda a: a if a.ndim == 1 else a[0]
    sp = {k: unlayer(w[k]) for k in SMALL}
    n_tok, d_model = x.shape[1:]
    _, grad_x = _local_grads(x.reshape(n_tok, d_model), p.reshape(n_tok, p.shape[-1]),
                             loss_target.reshape(n_tok, d_model), sp, exch)

    grad_w, delta_w, new_m, new_v = {}, {}, {}, {}
    reduced = exch.reduced()
    for k in LARGE:
        shape = w[k].shape
        two_d = lambda a: a.reshape(shape[1:])
        like = _sds(shape[1:], F32)
        update = lambda w_t, g_t, m_t, v_t: (g_t, *_adamw(w_t, g_t, m_t, v_t))
        outs = _rowwise("adamw_" + k, update, [two_d(w[k]), reduced[k], two_d(m[k]), two_d(v[k])], [], [like, like, like, like])
        grad_w[k], delta_w[k], new_m[k], new_v[k] = (a.reshape(shape) for a in outs)

    packed_g, loss = exch.small_reduced()
    like = _sds(packed_g.shape, F32)
    d_s, m_s, v_s = _rowwise("adamw_small", _adamw, [_pack([w[k] for k in SMALL]), packed_g, _pack([m[k] for k in SMALL]),
                                                     _pack([v[k] for k in SMALL])], [], [like, like, like])
    shapes = [w[k] for k in SMALL]
    for k, g_k, d_k, m_k, v_k in zip(SMALL, _unpack(packed_g, shapes), _unpack(d_s, shapes), _unpack(m_s, shapes), _unpack(v_s, shapes)):
        grad_w[k], delta_w[k], new_m[k], new_v[k] = g_k, d_k, m_k, v_k

    return (loss, grad_x.reshape(x.shape), *[grad_w[k] for k in WEIGHTS], *[delta_w[k] for k in WEIGHTS],
            *[new_m[k] for k in WEIGHTS], *[new_v[k] for k in WEIGHTS])
```

```python
import functools

import jax
import jax.numpy as jnp
from jax import lax
from jax.experimental import pallas as pl
from jax.experimental.pallas import tpu as pltpu

F32 = jnp.float32
BF16 = jnp.bfloat16

EPS = 1e-6
LAMBDA_RE_MAX = -1e-4
ADAM_LR = 0.001
ADAM_B1 = 0.9
ADAM_B2 = 0.999
ADAM_EPS = 1e-08
ADAM_WD = 0.01
ADAM_STEP = 10

N_CHIPS = 4
N_DEV = 8
SUBLANES = 8
LANES = 128
SSM_CH_BLOCK = 256
SCAN_LANES = 256
SCAN_BLOCKS = 4
GATHER_PARTS = 16
VMEM_LIMIT = 56 * 1024 * 1024

MESH = pl.DeviceIdType.MESH


def _pick(n, pref, mult):
    if n <= pref:
        return n
    t = (pref // mult) * mult
    while t >= mult:
        if n % t == 0:
            return t
        t -= mult
    return n


def _params(semantics):
    return pltpu.CompilerParams(dimension_semantics=semantics, vmem_limit_bytes=VMEM_LIMIT)


class _Cols:
    def __init__(self, arr, width, blk):
        self.arr, self.width, self.blk = arr, width, blk


def _sds(shape, dtype):
    return jax.ShapeDtypeStruct(tuple(shape), dtype)


ANY = pl.BlockSpec(memory_space=pl.ANY)


class _Side:
    def __init__(self, ins, out_shapes, n_sems, first, last, mid=None, aliases=None, mid_late=False):
        self.ins, self.out_shapes, self.n_sems = list(ins), list(out_shapes), n_sems
        self.first, self.mid, self.last, self.mid_late = first, mid, last, mid_late
        self.aliases = dict(aliases or {})


def _call(body, side, operands, *, name, grid, in_specs, out_specs, out_shape, compiler_params, scratch_shapes=()):
    if side is None:
        return pl.pallas_call(body, name=name, grid=grid, in_specs=in_specs, out_specs=out_specs, out_shape=out_shape,
                              scratch_shapes=list(scratch_shapes), compiler_params=compiler_params)(*operands)
    single = not isinstance(out_specs, (list, tuple))
    out_specs = [out_specs] if single else list(out_specs)
    out_shape = [out_shape] if single else list(out_shape)
    n_in, n_out, n_scr = len(in_specs), len(out_specs), len(scratch_shapes)
    n_sin, n_sout = len(side.ins), len(side.out_shapes)
    steps = 1
    for g in grid:
        steps *= g

    def hosted(*refs):
        ins, s_ins = refs[:n_in], refs[n_in:n_in + n_sin]
        at = n_in + n_sin
        outs, s_outs = refs[at:at + n_out], refs[at + n_out:at + n_out + n_sout]
        scratch = refs[at + n_out + n_sout:at + n_out + n_sout + n_scr]
        sems = refs[-2:]
        step = pl.program_id(0)
        for d in range(1, len(grid)):
            step = step * grid[d] + pl.program_id(d)

        @pl.when(step == 0)
        def _():
            side.first(s_ins, s_outs, *sems)

        if side.mid is not None:
            @pl.when(step == (steps - 1 if side.mid_late else (3 * steps) // 4))
            def _():
                side.mid(s_ins, s_outs, *sems)

        body(*ins, *outs, *scratch)

        @pl.when(step == steps - 1)
        def _():
            side.last(s_ins, s_outs, *sems)

    res = pl.pallas_call(
        hosted, name=name, grid=grid, in_specs=[*in_specs, *[ANY] * n_sin], out_specs=[*out_specs, *[ANY] * n_sout],
        out_shape=[*out_shape, *side.out_shapes], input_output_aliases={n_in + i: n_out + o for i, o in side.aliases.items()},
        scratch_shapes=[*scratch_shapes, pltpu.SemaphoreType.DMA((side.n_sems,)), pltpu.SemaphoreType.DMA((side.n_sems,))],
        compiler_params=compiler_params)(*operands, *side.ins)
    return (res[0] if single else list(res[:n_out])), list(res[n_out:])


def _rowwise(name, fn, rows, params, row_outs, acc_outs=(), tr=256, side=None):
    rows = [r if isinstance(r, _Cols) else _Cols(r, r.shape[1], 0) for r in rows]
    m = rows[0].arr.shape[0]
    tr = _pick(m, tr, 16)
    n_in = len(rows) + len(params)
    n_ro = len(row_outs)

    def body(*refs):
        vals = fn(*[r[...] for r in refs[:n_in]])
        if not isinstance(vals, (tuple, list)):
            vals = (vals,)
        outs = refs[n_in:]
        for r, v in zip(outs[:n_ro], vals[:n_ro]):
            r[...] = v.astype(r.dtype)
        first = pl.program_id(0) == 0
        for r, v in zip(outs[n_ro:], vals[n_ro:]):
            @pl.when(first)
            def _():
                r[...] = jnp.zeros(r.shape, r.dtype)
            r[...] += v.astype(r.dtype).reshape(r.shape)

    in_specs = [pl.BlockSpec((tr, r.width), lambda i, b=r.blk: (i, b)) for r in rows]
    in_specs += [pl.BlockSpec(p.shape, lambda i, nd=p.ndim: (0,) * nd) for p in params]
    out_specs = [pl.BlockSpec((tr, o.shape[1]), lambda i: (i, 0)) for o in row_outs]
    out_specs += [pl.BlockSpec(o.shape, lambda i, nd=len(o.shape): (0,) * nd) for o in acc_outs]
    return _call(body, side, [*[r.arr for r in rows], *params], name=name, grid=(m // tr,), in_specs=in_specs,
                 out_specs=out_specs, out_shape=[*row_outs, *acc_outs], compiler_params=_params(("arbitrary",)))


def _grid_order(swap):
    if not swap:
        return (lambda grid: grid), (lambda f: f)
    return (lambda grid: grid[::-1]), (lambda f: (lambda j, i: f(i, j)))


def _mm_nn(name, a, w, *, sharded=False, res=None, out_dtype=F32, tm=512, tn=512, w_resident=False, side=None):
    m, k = a.shape
    tm = _pick(m, tm, 16)
    order, ix = _grid_order(w_resident)
    if sharded:
        s, _, ns = w.shape
        n = s * ns
        tn = _pick(ns, tn, LANES)
        per = ns // tn
        w_spec = pl.BlockSpec((None, k, tn), ix(lambda i, j: (j // per, 0, j % per)))
    else:
        n = w.shape[1]
        tn = _pick(n, tn, LANES)
        w_spec = pl.BlockSpec((k, tn), ix(lambda i, j: (0, j)))

    def body(a_ref, w_ref, *rest):
        acc = jnp.dot(a_ref[...], w_ref[...], preferred_element_type=F32)
        if res is not None:
            acc = acc + rest[0][...]
        rest[-1][...] = acc.astype(out_dtype)

    in_specs = [pl.BlockSpec((tm, k), ix(lambda i, j: (i, 0))), w_spec]
    ops = [a, w]
    if res is not None:
        in_specs.append(pl.BlockSpec((tm, tn), ix(lambda i, j: (i, j))))
        ops.append(res)
    return _call(body, side, ops, name=name, grid=order((m // tm, n // tn)), in_specs=in_specs,
                 out_specs=pl.BlockSpec((tm, tn), ix(lambda i, j: (i, j))), out_shape=_sds((m, n), out_dtype),
                 compiler_params=_params(("arbitrary", "arbitrary")))


def _mm_nt(name, g, w, *, sharded=False, g_halves=False, out_dtype=F32, tm=512, tk=512, w_resident=False, side=None):
    m, n = g.shape[-2:]
    tm = _pick(m, tm, 16)
    order, ix = _grid_order(w_resident)
    dims = (((1,), (1,)), ((), ()))
    g_spec = pl.BlockSpec((2, tm, n), ix(lambda i, j: (0, i, 0))) if g_halves else pl.BlockSpec((tm, n), ix(lambda i, j: (i, 0)))
    if sharded:
        s, k, ns = w.shape
        tk = _pick(k, tk, LANES)
        w_spec = pl.BlockSpec((s, tk, ns), ix(lambda i, j: (0, j, 0)))

        def columns(g_ref, q):
            if not g_halves:
                return g_ref[:, q * ns:(q + 1) * ns]
            half, at = divmod(q, s // 2)
            return g_ref[half, :, at * ns:(at + 1) * ns]

        def body(g_ref, w_ref, o_ref):
            acc = lax.dot_general(columns(g_ref, 0), w_ref[0], dims, preferred_element_type=F32)
            for q in range(1, s):
                acc = acc + lax.dot_general(columns(g_ref, q), w_ref[q], dims, preferred_element_type=F32)
            o_ref[...] = acc.astype(out_dtype)
    else:
        k = w.shape[0]
        tk = _pick(k, tk, LANES)
        w_spec = pl.BlockSpec((tk, n), ix(lambda i, j: (j, 0)))

        def body(g_ref, w_ref, o_ref):
            o_ref[...] = lax.dot_general(g_ref[...], w_ref[...], dims, preferred_element_type=F32).astype(out_dtype)

    return _call(body, side, [g, w], name=name, grid=order((m // tm, k // tk)), in_specs=[g_spec, w_spec],
                 out_specs=pl.BlockSpec((tm, tk), ix(lambda i, j: (i, j))), out_shape=_sds((m, k), out_dtype),
                 compiler_params=_params(("arbitrary", "arbitrary")))


def _mm_tn(name, a, g, *, shards=0, g_halves=False, tk=512, tn=512, g_resident=False, side=None):
    m, k = a.shape
    n = 2 * g.shape[2] if g_halves else g.shape[1]
    tk = _pick(k, tk, LANES)
    order, ix = _grid_order(g_resident)
    dims = (((0,), (0,)), ((), ()))
    if shards:
        ns = n // shards
        tn = _pick(ns, tn, LANES)
        per = ns // tn
        out_spec = pl.BlockSpec((None, tk, tn), ix(lambda i, j: (j // per, i, j % per)))
        out_shape = _sds((shards, k, ns), F32)
    else:
        tn = _pick(n, tn, LANES)
        out_spec = pl.BlockSpec((tk, tn), ix(lambda i, j: (i, j)))
        out_shape = _sds((k, n), F32)

    def body(a_ref, g_ref, o_ref):
        o_ref[...] = lax.dot_general(a_ref[...], g_ref[...], dims, preferred_element_type=F32)

    if g_halves:
        per_half = n // 2 // tn
        g_spec = pl.BlockSpec((None, m, tn), ix(lambda i, j: (j // per_half, 0, j % per_half)))
    else:
        g_spec = pl.BlockSpec((m, tn), ix(lambda i, j: (0, j)))
    return _call(body, side, [a, g], name=name, grid=order((k // tk, n // tn)),
                 in_specs=[pl.BlockSpec((m, tk), ix(lambda i, j: (0, i))), g_spec],
                 out_specs=out_spec, out_shape=out_shape, compiler_params=_params(("arbitrary", "arbitrary")))


def _ffn_in_swiglu(name, a, w, *, tm=512, tn=1408, side=None):
    m, k = a.shape
    s, _, ns = w.shape
    f = s * ns // 2
    tm = _pick(m, tm, 16)
    tn = _pick(ns, tn, LANES)
    per = ns // tn
    order, ix = _grid_order(True)

    def body(a_ref, wg_ref, wu_ref, act_ref, gu_ref):
        x = a_ref[...]
        gate = jnp.dot(x, wg_ref[...], preferred_element_type=F32)
        up = jnp.dot(x, wu_ref[...], preferred_element_type=F32)
        act_ref[...] = _swiglu(gate, up).astype(BF16)
        gu_ref[0] = gate.astype(BF16)
        gu_ref[1] = up.astype(BF16)

    return _call(body, side, [a, w, w], name=name, grid=order((m // tm, f // tn)),
                 in_specs=[pl.BlockSpec((tm, k), ix(lambda i, j: (i, 0))),
                           pl.BlockSpec((None, k, tn), ix(lambda i, j: (j // per, 0, j % per))),
                           pl.BlockSpec((None, k, tn), ix(lambda i, j: (s // 2 + j // per, 0, j % per)))],
                 out_specs=[pl.BlockSpec((tm, tn), ix(lambda i, j: (i, j))), pl.BlockSpec((2, tm, tn), ix(lambda i, j: (0, i, j)))],
                 out_shape=[_sds((m, f), BF16), _sds((2, m, f), BF16)], compiler_params=_params(("arbitrary", "arbitrary")))


def _d_act_swiglu(name, g, w, gu, *, tm=1024, tk=512, side=None):
    m, n = g.shape
    f = w.shape[0]
    tm = _pick(m, tm, 16)
    tk = _pick(f, tk, LANES)
    dims = (((1,), (1,)), ((), ()))

    def body(g_ref, w_ref, gu_ref, o_ref):
        dact = lax.dot_general(g_ref[...], w_ref[...], dims, preferred_element_type=F32)
        _, vjp = jax.vjp(_swiglu, gu_ref[0].astype(F32), gu_ref[1].astype(F32))
        dgate, dup = vjp(dact)
        o_ref[0] = dgate.astype(BF16)
        o_ref[1] = dup.astype(BF16)

    return _call(body, side, [g, w, gu], name=name, grid=(m // tm, f // tk),
                 in_specs=[pl.BlockSpec((tm, n), lambda i, j: (i, 0)), pl.BlockSpec((tk, n), lambda i, j: (j, 0)),
                           pl.BlockSpec((2, tm, tk), lambda i, j: (0, i, j))],
                 out_specs=pl.BlockSpec((2, tm, tk), lambda i, j: (0, i, j)), out_shape=_sds((2, m, f), BF16),
                 compiler_params=_params(("arbitrary", "arbitrary")))


def _rms(x, g):
    r = lax.rsqrt(jnp.mean(x * x, axis=-1, keepdims=True) + EPS)
    return (x * r) * g


def _glu_out(y_pre, q, glu_b, g_norm):
    ya0 = jax.nn.gelu(y_pre)
    return _rms(ya0 * jax.nn.sigmoid(q + glu_b), g_norm)


def _sgu_rows(zu, zv, ln_g, ln_b, w_s, b_st, g_norm):
    heads, t, _ = w_s.shape
    hd = zu.shape[1] // heads
    uu = jax.nn.gelu(zu)
    vv = jax.nn.gelu(zv)
    mu = jnp.mean(vv, axis=-1, keepdims=True)
    xc = vv - mu
    r = lax.rsqrt(jnp.mean(xc * xc, axis=-1, keepdims=True) + EPS)
    vn = (xc * r) * ln_g + ln_b
    row = lax.broadcasted_iota(jnp.int32, (t, t), 0)
    col = lax.broadcasted_iota(jnp.int32, (t, t), 1)
    causal = row >= col
    chunks = []
    for n in range(zu.shape[0] // t):
        blocks = []
        for h in range(heads):
            wm = jnp.where(causal, w_s[h], jnp.zeros_like(w_s[h])).astype(BF16)
            vb = vn[n * t:(n + 1) * t, h * hd:(h + 1) * hd].astype(BF16)
            blocks.append(jnp.dot(wm, vb, preferred_element_type=F32) + b_st[:, h:h + 1])
        chunks.append(jnp.concatenate(blocks, axis=1))
    s = jnp.concatenate(chunks, axis=0) if len(chunks) > 1 else chunks[0]
    return _rms(uu * s, g_norm)


def _swiglu(gate, up):
    return jax.nn.silu(gate) * up


def _head_loss(x2, gpre, pp, b_g, g_final, target):
    gate = jax.nn.sigmoid(gpre + b_g)
    out = _rms(x2 + gate * pp, g_final)
    err = jnp.square(out - target)
    return 0.5 * jnp.sum(jnp.mean(err, axis=-1))


def _ssm_disc(lam_re, lam_im, log_step):
    lr = jnp.minimum(lam_re, LAMBDA_RE_MAX)
    li = lam_im
    dt = jnp.exp(log_step)
    mag = jnp.exp(lr * dt)
    ang = li * dt
    abar_re = mag * jnp.cos(ang)
    abar_im = mag * jnp.sin(ang)
    nr = abar_re - 1.0
    ni = abar_im
    den = lr * lr + li * li
    q_re = (nr * lr + ni * li) / den
    q_im = (ni * lr - nr * li) / den
    return abar_re, abar_im, q_re, q_im


def _ssm_bbar(q_re, q_im, b_re, b_im):
    return q_re * b_re - q_im * b_im, q_re * b_im + q_im * b_re


def _ssm_discretised(lam_re, lam_im, log_step, bt_re, bt_im):
    ar, ai, qr, qi = _ssm_disc(lam_re, lam_im, log_step)
    return (ar, ai, *_ssm_bbar(qr, qi, bt_re, bt_im))


def _adamw(w, g, m, v):
    m = ADAM_B1 * m + (1.0 - ADAM_B1) * g
    v = ADAM_B2 * v + (1.0 - ADAM_B2) * jnp.square(g)
    m_hat = m / (1.0 - ADAM_B1 ** ADAM_STEP)
    v_hat = v / (1.0 - ADAM_B2 ** ADAM_STEP)
    delta = -ADAM_LR * (m_hat / (jnp.sqrt(v_hat) + ADAM_EPS) + ADAM_WD * w)
    return delta, m, v


class _SsmDims:
    def __init__(self, groups, state, gch):
        self.g, self.p, self.h = groups, state, gch
        self.d = groups * gch
        self.cb = min(SSM_CH_BLOCK, self.d)
        self.gb = self.cb // gch
        self.ns = self.gb * state
        self.nb = self.d // self.cb


def _ssm_rows(sd, sp):
    gp = sd.g * sd.p
    log_step = jnp.broadcast_to(sp["ssm_log_step"][:, None], (sd.g, sd.p)).reshape(1, gp)
    bt = [sp[k].reshape(gp, sd.h).T for k in ("ssm_b_re", "ssm_b_im")]
    ct = [sp[k].transpose(1, 0, 2).reshape(sd.h, gp) for k in ("ssm_c_re", "ssm_c_im")]
    return (sp["ssm_lambda_re"].reshape(1, gp), sp["ssm_lambda_im"].reshape(1, gp), log_step, *bt, *ct)


def _block_mask(sd):
    row = lax.broadcasted_iota(jnp.int32, (sd.cb, sd.ns), 0) // sd.h
    col = lax.broadcasted_iota(jnp.int32, (sd.cb, sd.ns), 1) // sd.p
    return row == col


def _scan_consts(pr, pi_, reverse):
    if reverse:
        pi_ = [-v for v in pi_]
    shape = (SUBLANES, pr[0].shape[1])
    rows = lax.broadcasted_iota(jnp.int32, shape, 0)
    parts = []
    for d in (1, 2, 4):
        keep = (rows < SUBLANES - d) if reverse else (rows >= d)
        parts += [jnp.where(keep, jnp.broadcast_to(v[d - 1], shape), 0.0) for v in (pr, pi_)]
    order = range(SUBLANES - 1, -1, -1) if reverse else range(SUBLANES)
    parts += [jnp.concatenate([v[t] for t in order], axis=0) for v in (pr, pi_)]
    return jnp.concatenate(parts, axis=0)


def _ssm_operands(sd, rows):
    cb, ns, nb = sd.cb, sd.ns, sd.nb

    def body(lam_re, lam_im, log_step, bt_re, bt_im, ct_re, ct_im, wb_ref, wbt_ref, wc_ref, wct_ref, cst_f_ref, cst_r_ref):
        ar, ai, bbar_re, bbar_im = _ssm_discretised(lam_re[...], lam_im[...], log_step[...], bt_re[...], bt_im[...])
        pr, pi_ = [ar], [ai]
        for _ in range(SUBLANES - 1):
            pr, pi_ = pr + [pr[-1] * ar - pi_[-1] * ai], pi_ + [pr[-1] * ai + pi_[-1] * ar]
        mask = _block_mask(sd)
        spread = lambda src: jnp.where(mask, jnp.concatenate([src] * sd.gb, axis=0), 0.0)
        for j in range(nb):
            at = slice(j * ns, (j + 1) * ns)
            w = jnp.concatenate([spread(bbar_re[:, at]), spread(bbar_im[:, at])], axis=1)
            v = jnp.concatenate([spread(ct_re[:, at]), -spread(ct_im[:, at])], axis=1)
            wb_ref[j] = w.astype(BF16)
            wbt_ref[j] = w.T.astype(BF16)
            wct_ref[j] = v.astype(BF16)
            wc_ref[j] = v.T.astype(BF16)
            pj, qj = [u[:, at] for u in pr], [u[:, at] for u in pi_]
            cst_f_ref[j] = _scan_consts(pj, qj, False)
            cst_r_ref[j] = _scan_consts(pj, qj, True)

    wide, tall = _sds((nb, cb, 2 * ns), BF16), _sds((nb, 2 * ns, cb), BF16)
    cst = _sds((nb, 8 * SUBLANES, ns), F32)
    vm = pl.BlockSpec(memory_space=pltpu.VMEM)
    return pl.pallas_call(body, name="ssm_operands", in_specs=[vm] * 7, out_specs=[vm] * 6,
                          out_shape=[wide, tall, tall, wide, cst, cst],
                          compiler_params=pltpu.CompilerParams(vmem_limit_bytes=VMEM_LIMIT))(*rows)


def _ssm_param_grads(sd, rows, dwb, dwc, da):
    ns, nb, gp = sd.ns, sd.nb, sd.g * sd.p

    def body(lam_re, lam_im, log_step, bt_re, bt_im, dwb_v, dwc_v, da_v, *outs):
        mask = _block_mask(sd)

        def fold(dense):
            kept = jnp.where(mask, dense, 0.0)
            acc = kept[0:sd.h]
            for gl in range(1, sd.gb):
                acc = acc + kept[gl * sd.h:(gl + 1) * sd.h]
            return acc

        lanes = lambda parts: jnp.concatenate(parts, axis=1) if len(parts) > 1 else parts[0]
        dbbar_re = lanes([fold(dwb_v[j][:, :ns]) for j in range(nb)])
        dbbar_im = lanes([fold(dwb_v[j][:, ns:]) for j in range(nb)])
        dwct = [dwc_v[j] for j in range(nb)]
        d_ct_re = lanes([fold(t[:, :ns]) for t in dwct])
        d_ct_im = -lanes([fold(t[:, ns:]) for t in dwct])
        dabar_re = lanes([da_v[j][0:1, :ns] for j in range(nb)])
        dabar_im = lanes([da_v[j][0:1, ns:] for j in range(nb)])
        _, vjp = jax.vjp(_ssm_discretised, lam_re[...], lam_im[...], log_step[...], bt_re[...], bt_im[...])
        d_lr, d_li, d_ls, d_bt_re, d_bt_im = vjp((dabar_re, dabar_im, dbbar_re, dbbar_im))
        group = (lax.broadcasted_iota(jnp.int32, (gp, sd.g), 0) // sd.p == lax.broadcasted_iota(jnp.int32, (gp, sd.g), 1))
        d_log_step = jnp.dot(d_ls, group.astype(F32), precision=lax.Precision.HIGHEST, preferred_element_type=F32)
        for ref, val in zip(outs, (d_lr, d_li, d_log_step, d_bt_re, d_bt_im, d_ct_re, d_ct_im)):
            ref[...] = val

    row, mat = _sds((1, gp), F32), _sds((sd.h, gp), F32)
    vm = pl.BlockSpec(memory_space=pltpu.VMEM)
    return pl.pallas_call(body, name="ssm_param_grads", in_specs=[vm] * 8, out_specs=[vm] * 7,
                          out_shape=[row, row, _sds((1, sd.g), F32), mat, mat, mat, mat],
                          compiler_params=pltpu.CompilerParams(vmem_limit_bytes=VMEM_LIMIT))(*rows[:5], dwb, dwc, da)


def _block_scan(s_ref, cst_ref, carry_ref, sd, rows, reverse):
    ns = sd.ns
    nblk = rows // SUBLANES
    w = min(SCAN_LANES, ns)
    for c0 in range(0, ns, w):
        re_l, im_l = slice(c0, c0 + w), slice(ns + c0, ns + c0 + w)
        cst = [cst_ref[k * SUBLANES:(k + 1) * SUBLANES, c0:c0 + w] for k in range(8)]

        def step(k, carry, re_l=re_l, im_l=im_l, cst=cst):
            local = []
            for b in range(SCAN_BLOCKS):
                blk = SCAN_BLOCKS * k + b
                blk = (nblk - 1 - blk) if reverse else blk
                r0 = pl.multiple_of(blk * SUBLANES, SUBLANES)
                xr = s_ref[pl.ds(r0, SUBLANES), re_l]
                xi = s_ref[pl.ds(r0, SUBLANES), im_l]
                for n, d in enumerate((1, 2, 4)):
                    ar, ai = cst[2 * n], cst[2 * n + 1]
                    shift = (SUBLANES - d) if reverse else d
                    sr = pltpu.roll(xr, shift, 0)
                    si = pltpu.roll(xi, shift, 0)
                    xr, xi = xr + ar * sr - ai * si, xi + ar * si + ai * sr
                local.append((r0, xr, xi))
            cr, ci = carry
            edge = slice(0, 1) if reverse else slice(SUBLANES - 1, SUBLANES)
            for r0, xr, xi in local:
                br = jnp.broadcast_to(cr, xr.shape)
                bi = jnp.broadcast_to(ci, xi.shape)
                xr, xi = xr + cst[6] * br - cst[7] * bi, xi + cst[6] * bi + cst[7] * br
                s_ref[pl.ds(r0, SUBLANES), re_l] = xr
                s_ref[pl.ds(r0, SUBLANES), im_l] = xi
                cr, ci = xr[edge, :], xi[edge, :]
            return cr, ci

        cr, ci = lax.fori_loop(0, nblk // SCAN_BLOCKS, step, (carry_ref[0:1, re_l], carry_ref[0:1, im_l]))
        carry_ref[0:1, re_l] = cr
        carry_ref[0:1, im_l] = ci


def _ssm_fwd(name, sd, z, wb, wc, cst, d_row, tt=512, side=None):
    n_tok = z.shape[0]
    tt = _pick(n_tok, tt, 16)
    cb, ns2 = sd.cb, 2 * sd.ns

    def body(z_ref, wb_ref, wc_ref, cst_ref, d_ref, y_ref, s_ref, a0_ref, carry_ref):
        @pl.when(pl.program_id(1) == 0)
        def _():
            carry_ref[...] = jnp.zeros(carry_ref.shape, F32)
        u = z_ref[...]
        s_ref[...] = jnp.dot(u.astype(BF16), wb_ref[...], preferred_element_type=F32)
        _block_scan(s_ref, cst_ref, carry_ref, sd, tt, reverse=False)
        y = jnp.dot(s_ref[...].astype(BF16), wc_ref[...], preferred_element_type=F32) + d_ref[...] * u
        y_ref[...] = y
        a0_ref[...] = jax.nn.gelu(y).astype(BF16)

    return _call(
        body, side, [z, wb, wc, cst, d_row], name=name, grid=(sd.nb, n_tok // tt),
        in_specs=[pl.BlockSpec((tt, cb), lambda j, i: (i, j)),
                  pl.BlockSpec((None, cb, ns2), lambda j, i: (j, 0, 0)),
                  pl.BlockSpec((None, ns2, cb), lambda j, i: (j, 0, 0)),
                  pl.BlockSpec((None, 8 * SUBLANES, sd.ns), lambda j, i: (j, 0, 0)),
                  pl.BlockSpec((1, cb), lambda j, i: (0, j))],
        out_specs=[pl.BlockSpec((tt, cb), lambda j, i: (i, j)), pl.BlockSpec((tt, ns2), lambda j, i: (i, j)),
                   pl.BlockSpec((tt, cb), lambda j, i: (i, j))],
        out_shape=[_sds((n_tok, sd.d), F32), _sds((n_tok, sd.nb * ns2), F32), _sds((n_tok, sd.d), BF16)],
        scratch_shapes=[pltpu.VMEM((SUBLANES, ns2), F32)],
        compiler_params=_params(("arbitrary", "arbitrary")))


def _ssm_bwd(name, sd, y_pre, dy_direct, dya0, z, states, wct, wbt, cst_rev, d_row, tt=512, side=None):
    n_tok = z.shape[0]
    tt = _pick(n_tok, tt, 16)
    nt = n_tok // tt
    cb, ns, ns2 = sd.cb, sd.ns, 2 * sd.ns
    blocks_per_tile = tt // SUBLANES
    tn_dims = (((0,), (0,)), ((), ()))

    def body(y_ref, dyd_ref, dya0_ref, z_ref, s_ref, sp_ref, wct_ref, wbt_ref, cst_ref, d_ref,
             du_ref, dwb_ref, dwc_ref, da_ref, dd_ref, lam_ref, carry_ref):
        i = pl.program_id(1)

        @pl.when(i == 0)
        def _():
            carry_ref[...] = jnp.zeros(carry_ref.shape, F32)
            dwb_ref[...] = jnp.zeros(dwb_ref.shape, F32)
            dwc_ref[...] = jnp.zeros(dwc_ref.shape, F32)
            da_ref[...] = jnp.zeros(da_ref.shape, F32)
            dd_ref[...] = jnp.zeros(dd_ref.shape, F32)

        _, gelu_vjp = jax.vjp(jax.nn.gelu, y_ref[...])
        dy_t = dyd_ref[...] + gelu_vjp(dya0_ref[...].astype(F32))[0]
        u = z_ref[...]
        dy16 = dy_t.astype(BF16)
        lam_ref[...] = jnp.dot(dy16, wct_ref[...], preferred_element_type=F32)
        _block_scan(lam_ref, cst_ref, carry_ref, sd, tt, reverse=True)
        lam = lam_ref[...]
        lam16 = lam.astype(BF16)
        du_ref[...] = (jnp.dot(lam16, wbt_ref[...], preferred_element_type=F32) + d_ref[...] * dy_t).astype(BF16)
        dd_ref[0:1, :] += jnp.sum(dy_t * u, axis=0, keepdims=True)
        dwb_ref[...] += lax.dot_general(u.astype(BF16), lam16, tn_dims, preferred_element_type=F32)
        s = s_ref[...]
        dwc_ref[...] += lax.dot_general(dy16, s.astype(BF16), tn_dims, preferred_element_type=F32)
        before = jnp.where(i == nt - 1, 0.0, 1.0) * sp_ref[SUBLANES - 1:SUBLANES, :]
        first_row = lax.broadcasted_iota(jnp.int32, s.shape, 0) == 0
        prev = jnp.where(first_row, jnp.broadcast_to(before, s.shape), pltpu.roll(s, 1, 0))
        lr, li = lam[:, :ns], lam[:, ns:]
        pr, pi_ = prev[:, :ns], prev[:, ns:]
        da_ref[0:1, 0:ns] += jnp.sum(lr * pr + li * pi_, axis=0, keepdims=True)
        da_ref[0:1, ns:ns2] += jnp.sum(li * pr - lr * pi_, axis=0, keepdims=True)

    rev = lambda i: nt - 1 - i
    return _call(
        body, side, [y_pre, dy_direct, dya0, z, states, states, wct, wbt, cst_rev, d_row], name=name, grid=(sd.nb, nt),
        in_specs=[pl.BlockSpec((tt, cb), lambda j, i: (rev(i), j)),
                  pl.BlockSpec((tt, cb), lambda j, i: (rev(i), j)),
                  pl.BlockSpec((tt, cb), lambda j, i: (rev(i), j)),
                  pl.BlockSpec((tt, cb), lambda j, i: (rev(i), j)),
                  pl.BlockSpec((tt, ns2), lambda j, i: (rev(i), j)),
                  pl.BlockSpec((SUBLANES, ns2), lambda j, i: (jnp.maximum(rev(i) * blocks_per_tile - 1, 0), j)),
                  pl.BlockSpec((None, cb, ns2), lambda j, i: (j, 0, 0)),
                  pl.BlockSpec((None, ns2, cb), lambda j, i: (j, 0, 0)),
                  pl.BlockSpec((None, 8 * SUBLANES, ns), lambda j, i: (j, 0, 0)),
                  pl.BlockSpec((1, cb), lambda j, i: (0, j))],
        out_specs=[pl.BlockSpec((tt, cb), lambda j, i: (rev(i), j)),
                   pl.BlockSpec((None, cb, ns2), lambda j, i: (j, 0, 0)),
                   pl.BlockSpec((None, cb, ns2), lambda j, i: (j, 0, 0)),
                   pl.BlockSpec((None, SUBLANES, ns2), lambda j, i: (j, 0, 0)),
                   pl.BlockSpec((None, SUBLANES, cb), lambda j, i: (j, 0, 0))],
        out_shape=[_sds((n_tok, sd.d), BF16), _sds((sd.nb, cb, ns2), F32), _sds((sd.nb, cb, ns2), F32),
                   _sds((sd.nb, SUBLANES, ns2), F32), _sds((sd.nb, SUBLANES, cb), F32)],
        scratch_shapes=[pltpu.VMEM((tt, ns2), F32), pltpu.VMEM((SUBLANES, ns2), F32)],
        compiler_params=_params(("arbitrary", "arbitrary")))


def _hosted(exch, fn, name, *args, **kw):
    side = exch.side(name)
    if side is None:
        return fn(name, *args, **kw)
    out, moved = fn(name, *args, side=side, **kw)
    exch.done(name, moved)
    return out


def _local_grads(x, p, target, sp, exch):
    n_tok, d_model = x.shape
    d_ssm = sp["ssm_d"].shape[0] * sp["ssm_d"].shape[1]
    d_sgu = sp["sgu_ln_g"].shape[-1]
    sd = _SsmDims(sp["ssm_b_re"].shape[0], sp["ssm_b_re"].shape[1], sp["ssm_b_re"].shape[2])
    heads, chunk, _ = sp["sgu_w"].shape
    row = lambda v: v.reshape(1, -1)
    tok = lambda w, dt=F32: _sds((n_tok, w), dt)
    acc = lambda w: _sds((1, w), F32)

    g_mix = row(sp["norm_mix_g"])
    (h1,) = _hosted(exch, _rowwise, "norm_mix", lambda a, g: _rms(a, g), [x], [g_mix], [tok(d_model, BF16)])
    z = _hosted(exch, _mm_nn, "proj_in", h1, exch.weight("w_in"), sharded=True, tn=768)

    ssm_rows = _ssm_rows(sd, sp)
    wb, wbt, wc, wct, cst_fwd, cst_rev = _ssm_operands(sd, ssm_rows)
    d_row = row(sp["ssm_d"])
    y_pre, states, ya0_16 = _hosted(exch, _ssm_fwd, "ssm_fwd", sd, z, wb, wc, cst_fwd, d_row)
    q = _mm_nn("ssm_glu", ya0_16, exch.weight("ssm_glu_w"), tm=1024)
    glu_b, g_ossm = row(sp["ssm_glu_b"]), row(sp["out_norm_ssm_g"])
    (ya_n,) = _rowwise("ssm_glu_out", _glu_out, [y_pre, q], [glu_b, g_ossm], [tok(d_ssm, BF16)])

    assert d_ssm == d_sgu
    zu, zv = _Cols(z, d_sgu, 1), _Cols(z, d_sgu, 2)
    ln_g, ln_b, g_osgu = row(sp["sgu_ln_g"]), row(sp["sgu_ln_b"]), row(sp["out_norm_sgu_g"])
    b_st = sp["sgu_b"].T
    sgu_tr = 2 * chunk

    def sgu_joined(ya_t, zu_t, zv_t, *params):
        return jnp.concatenate([ya_t, _sgu_rows(zu_t, zv_t, *params).astype(BF16)], axis=1)

    (ycat,) = _rowwise("sgu", sgu_joined, [ya_n, zu, zv], [ln_g, ln_b, sp["sgu_w"], b_st, g_osgu],
                       [tok(d_ssm + d_sgu, BF16)], tr=sgu_tr)
    x1 = _hosted(exch, _mm_nn, "proj_out", ycat, exch.weight("w_out"), res=x, tm=1024)

    g_ffn = row(sp["norm_ffn_g"])
    (h2,) = _rowwise("norm_ffn", lambda a, g: _rms(a, g), [x1], [g_ffn], [tok(d_model, BF16)])
    act, gu16 = _hosted(exch, _ffn_in_swiglu, "ffn_in", h2, exch.weight("w_ffn_in"))
    x2 = _mm_nn("ffn_out", act, exch.weight("w_ffn_out"), res=x1)

    g_ple = row(sp["norm_ple_g"])
    (h3,) = _rowwise("norm_ple", lambda a, g: _rms(a, g), [x2], [g_ple], [tok(d_model, BF16)])
    gpre = _mm_nn("ple_gate", h3, exch.weight("w_ple_gate"), tm=1024)
    (p16,) = _rowwise("ple_cast", lambda a: a, [p], [], [tok(p.shape[1], BF16)])
    pp = _mm_nn("ple_proj", p16, exch.weight("w_ple_proj"), sharded=True, tm=1024)

    b_g, g_fin = row(sp["b_ple_gate"]), row(sp["final_norm_g"])

    def head(x2_t, gpre_t, pp_t, tgt_t, b_g_v, g_fin_v):
        loss, grads = jax.value_and_grad(_head_loss, argnums=(0, 1, 2, 3, 4))(x2_t, gpre_t, pp_t, b_g_v, g_fin_v, tgt_t)
        dx2, dgpre, dpp, db, dg = grads
        return dx2, dgpre.astype(BF16), dpp.astype(BF16), jnp.full((1, LANES), loss, F32), db, dg

    dx2_head, dgpre16, dpp16, loss_row, d_b_g, d_g_fin = _rowwise(
        "head", head, [x2, gpre, pp, target], [b_g, g_fin],
        [tok(d_model), tok(d_model, BF16), tok(d_model, BF16)], [acc(LANES), acc(d_model), acc(d_model)])
    loss = loss_row[0, 0]
    exch.small_grads({"loss": loss_row})

    exch.grad("w_ple_proj", _mm_tn("d_ple_proj", p16, dpp16, shards=N_CHIPS, tk=256))
    exch.grad("w_ple_gate", _mm_tn("d_ple_gate", h3, dgpre16))
    dh3 = _mm_nt("d_h3", dgpre16, exch.weight("w_ple_gate"), out_dtype=BF16, tm=1024)

    def norm_bwd(x_t, dres_t, dh_t, g_v):
        _, vjp = jax.vjp(_rms, x_t, g_v)
        dx, dg = vjp(dh_t.astype(F32))
        dx = dres_t + dx
        return dx, dx.astype(BF16), dg

    dx2, dx2_16, d_g_ple = _rowwise("d_norm_ple", norm_bwd, [x2, dx2_head, dh3], [g_ple],
                                    [tok(d_model), tok(d_model, BF16)], [acc(d_model)])
    exch.grad("w_ffn_out", _mm_tn("d_ffn_out", act, dx2_16))
    dgu16 = _hosted(exch, _d_act_swiglu, "d_act", dx2_16, exch.weight("w_ffn_out"), gu16)
    exch.grad("w_ffn_in", _hosted(exch, _mm_tn, "d_ffn_in", h2, dgu16, shards=N_CHIPS, g_halves=True, tn=1408, g_resident=True))
    dh2 = _hosted(exch, _mm_nt, "d_h2", dgu16, exch.weight("w_ffn_in"), sharded=True, g_halves=True, out_dtype=BF16, tm=256, w_resident=True)
    dx1, dx1_16, d_g_ffn = _rowwise("d_norm_ffn", norm_bwd, [x1, dx2, dh2], [g_ffn],
                                    [tok(d_model), tok(d_model, BF16)], [acc(d_model)])
    exch.grad("w_out", _mm_tn("d_proj_out", ycat, dx1_16))
    dycat = _mm_nt("d_ycat", dx1_16, exch.weight("w_out"), out_dtype=BF16, tm=1024)

    def glu_out_bwd(y_pre_t, q_t, dy_t, glu_b_v, g_v):
        _, vjp = jax.vjp(_glu_out, y_pre_t, q_t, glu_b_v, g_v)
        dy_pre, dq, db, dg = vjp(dy_t.astype(F32))
        return dy_pre, dq.astype(BF16), db, dg

    dy_pre_a, dq16, d_glu_b, d_g_ossm = _rowwise(
        "d_ssm_glu_out", glu_out_bwd, [y_pre, q, _Cols(dycat, d_ssm, 0)], [glu_b, g_ossm],
        [tok(d_ssm), tok(d_ssm, BF16)], [acc(d_ssm), acc(d_ssm)])
    exch.grad("ssm_glu_w", _mm_tn("d_ssm_glu", ya0_16, dq16))
    dya0 = _hosted(exch, _mm_nt, "d_ya0", dq16, exch.weight("ssm_glu_w"), out_dtype=BF16, tm=1024)

    dz_ssm16, dwb, dwc, da, dd = _hosted(exch, _ssm_bwd, "ssm_bwd", sd, y_pre, dy_pre_a, dya0, z, states, wct, wbt,
                                         cst_rev, d_row)

    def sgu_bwd(dz_ssm_t, zu_t, zv_t, dy_t, ln_g_v, ln_b_v, w_v, b_v, g_v):
        _, vjp = jax.vjp(_sgu_rows, zu_t, zv_t, ln_g_v, ln_b_v, w_v, b_v, g_v)
        dzu, dzv, dlg, dlb, dw, db, dg = vjp(dy_t.astype(F32))
        return jnp.concatenate([dz_ssm_t, dzu.astype(BF16), dzv.astype(BF16)], axis=1), dlg, dlb, dw, db, dg

    dz16, d_ln_g, d_ln_b, d_sgu_w, d_b_st, d_g_osgu = _hosted(
        exch, _rowwise, "d_sgu", sgu_bwd, [dz_ssm16, zu, zv, _Cols(dycat, d_sgu, 1)], [ln_g, ln_b, sp["sgu_w"], b_st, g_osgu],
        [tok(d_ssm + 2 * d_sgu, BF16)],
        [acc(d_sgu), acc(d_sgu), _sds(sp["sgu_w"].shape, F32), _sds(b_st.shape, F32), acc(d_sgu)], tr=sgu_tr)

    d_lam_re, d_lam_im, d_log_step, d_bt_re, d_bt_im, d_ct_re, d_ct_im = _ssm_param_grads(sd, ssm_rows, dwb, dwc, da)
    d_b_re, d_b_im = d_bt_re.T, d_bt_im.T
    d_c_re, d_c_im = (t.reshape(sd.h, sd.g, sd.p).transpose(1, 0, 2) for t in (d_ct_re, d_ct_im))
    d_ssm_d = dd[:, 0, :].reshape(sd.g, sd.h)

    exch.small_grads({
        "ssm_lambda_re": d_lam_re, "ssm_lambda_im": d_lam_im, "ssm_log_step": d_log_step,
        "ssm_b_re": d_b_re, "ssm_b_im": d_b_im, "ssm_c_re": d_c_re, "ssm_c_im": d_c_im, "ssm_d": d_ssm_d,
        "ssm_glu_b": d_glu_b, "sgu_ln_g": d_ln_g, "sgu_ln_b": d_ln_b, "sgu_w": d_sgu_w, "sgu_b": d_b_st.T,
        "out_norm_ssm_g": d_g_ossm, "out_norm_sgu_g": d_g_osgu, "norm_ffn_g": d_g_ffn, "norm_ple_g": d_g_ple,
        "b_ple_gate": d_b_g, "final_norm_g": d_g_fin,
    })

    exch.grad("w_in", _hosted(exch, _mm_tn, "d_proj_in", h1, dz16, shards=N_CHIPS, tn=768))
    dh1 = _hosted(exch, _mm_nt, "d_h1", dz16, exch.weight("w_in"), sharded=True, out_dtype=BF16, tm=1024)

    def norm_in_bwd(x_t, dres_t, dh_t, g_v):
        _, vjp = jax.vjp(_rms, x_t, g_v)
        dx, dg = vjp(dh_t.astype(F32))
        return dres_t + dx, dg

    grad_x, d_g_mix = _hosted(exch, _rowwise, "d_norm_mix", norm_in_bwd, [x, dx1, dh1], [g_mix], [tok(d_model)], [acc(d_model)])
    exch.small_grads({"norm_mix_g": d_g_mix})
    return loss, grad_x


def _place():
    x, y, c = lax.axis_index("x"), lax.axis_index("y"), lax.axis_index("c")
    chips = [(1 - x, y), (x, 1 - y), (1 - x, 1 - y)]
    return x, y, c, chips


def _cast_into_slot(name, w2d, shard, tr=256):
    rows, cols = w2d.shape
    rh = rows // 2
    tr = _pick(rh, tr, 16)
    per = rh // tr

    def body(s_ref, a_ref, o_ref):
        o_ref[...] = a_ref[...].astype(BF16)

    grid_spec = pltpu.PrefetchScalarGridSpec(
        num_scalar_prefetch=1, grid=(2, per),
        in_specs=[pl.BlockSpec((tr, cols), lambda h, i, s_ref: (h * per + i, 0))],
        out_specs=pl.BlockSpec((None, None, tr, cols), lambda h, i, s_ref: (s_ref[0], h, i, 0)))
    return pl.pallas_call(body, name=name, grid_spec=grid_spec, out_shape=_sds((N_CHIPS, 2, rh, cols), BF16),
                          compiler_params=_params(("arbitrary", "arbitrary")))(shard.reshape(1).astype(jnp.int32), w2d)


def _exchange_alone(name, side):
    n_in, n_out = len(side.ins), len(side.out_shapes)

    def body(*refs):
        ins, outs, sems = refs[:n_in], refs[n_in:n_in + n_out], refs[n_in + n_out:]
        side.first(ins, outs, *sems)
        if side.mid is not None:
            side.mid(ins, outs, *sems)
        side.last(ins, outs, *sems)

    return pl.pallas_call(
        body, name=name, in_specs=[ANY] * n_in, out_specs=[ANY] * n_out, out_shape=side.out_shapes,
        input_output_aliases=side.aliases,
        scratch_shapes=[pltpu.SemaphoreType.DMA((side.n_sems,)), pltpu.SemaphoreType.DMA((side.n_sems,))],
    )(*side.ins)


def _gather_side(slots, parts=None, mid_late=False):
    n = len(slots)
    parts = parts or [(0, GATHER_PARTS)] * n

    def copies(kind, outs, send_sems, recv_sems):
        x, y, c, chips = _place()

        def remote(k, w, shard, half, to):
            unit = outs[w].shape[2] // GATHER_PARTS
            lo, hi = parts[w]
            ref = outs[w].at[shard, half, pl.ds(lo * unit, (hi - lo) * unit), :]
            return pltpu.make_async_remote_copy(src_ref=ref, dst_ref=ref, send_sem=send_sems.at[k], recv_sem=recv_sems.at[k],
                                                device_id=to, device_id_type=MESH)

        pairs = [(w, j, 2 * cx + cy, (cx, cy)) for w in range(n) for j, (cx, cy) in enumerate(chips)]
        if kind == "sends":
            return [remote(3 * w + j, w, 2 * x + y, c, (*chip, c)) for w, j, _, chip in pairs]
        if kind == "arrivals":
            return [remote(3 * w + j, w, s, c, (x, y, c)) for w, j, s, _ in pairs]
        if kind == "passed":
            return [remote(3 * n + 3 * w + j, w, s, c, (x, y, 1 - c)) for w, j, s, _ in pairs]
        return [remote(3 * n + 3 * w + j, w, s, 1 - c, (x, y, c)) for w, j, s, _ in pairs]

    def first(ins, outs, *sems):
        for cp in copies("sends", outs, *sems):
            cp.start()

    def mid(ins, outs, *sems):
        for arrived, onward in zip(copies("arrivals", outs, *sems), copies("passed", outs, *sems)):
            arrived.wait_recv()
            onward.start()

    def last(ins, outs, *sems):
        for cp in copies("from_sibling", outs, *sems):
            cp.wait_recv()
        for cp in copies("sends", outs, *sems) + copies("passed", outs, *sems):
            cp.wait_send()

    return _Side(slots, [_sds(s.shape, s.dtype) for s in slots], 6 * n, first, last, mid=mid, aliases={w: w for w in range(n)},
                 mid_late=mid_late)


def _swap_side(grads):
    n = len(grads)

    def copies(ins, outs, send_sems, recv_sems):
        x, y, c, _ = _place()
        return [pltpu.make_async_remote_copy(src_ref=ins[w].at[:, 1 - c], dst_ref=outs[w], send_sem=send_sems.at[w],
                                             recv_sem=recv_sems.at[w], device_id=(x, y, 1 - c), device_id_type=MESH)
                for w in range(n)]

    def first(*refs):
        for cp in copies(*refs):
            cp.start()

    def last(*refs):
        for cp in copies(*refs):
            cp.wait()

    return _Side(grads, [_sds((g.shape[0], *g.shape[2:]), g.dtype) for g in grads], n, first, last)


def _scatter_side(halves):
    n = len(halves)

    def copies(ins, outs, send_sems, recv_sems):
        x, y, c, chips = _place()
        return [pltpu.make_async_remote_copy(
            src_ref=ins[w].at[2 * cx + cy], dst_ref=outs[w].at[j], send_sem=send_sems.at[3 * w + j],
            recv_sem=recv_sems.at[3 * w + j], device_id=(cx, cy, c), device_id_type=MESH)
            for w in range(n) for j, (cx, cy) in enumerate(chips)]

    def first(*refs):
        for cp in copies(*refs):
            cp.start()

    def last(*refs):
        for cp in copies(*refs):
            cp.wait()

    return _Side(halves, [_sds((3, *h.shape[1:]), h.dtype) for h in halves], 3 * n, first, last)


def _join_side(slots):
    n = len(slots)

    def copy(outs, send_sems, recv_sems, w, half, to):
        return pltpu.make_async_remote_copy(src_ref=outs[w].at[half], dst_ref=outs[w].at[half], send_sem=send_sems.at[w],
                                            recv_sem=recv_sems.at[w], device_id=to, device_id_type=MESH)

    def first(ins, outs, *sems):
        x, y, c, _ = _place()
        for w in range(n):
            copy(outs, *sems, w, c, (x, y, 1 - c)).start()

    def last(ins, outs, *sems):
        x, y, c, _ = _place()
        for w in range(n):
            copy(outs, *sems, w, 1 - c, (x, y, c)).wait_recv()
        for w in range(n):
            copy(outs, *sems, w, c, (x, y, 1 - c)).wait_send()

    return _Side(slots, [_sds(s.shape, s.dtype) for s in slots], n, first, last, aliases={w: w for w in range(n)})


def _allreduce_small(block, tr=256):
    rows, lanes = block.shape
    tr = _pick(rows, tr, SUBLANES)

    def body(x_ref, o_ref, buf, send_sems, recv_sems):
        x, y, c, chips = _place()
        me, sibling = (x, y, c), (x, y, 1 - c)

        def slot(px, py, pc):
            return buf.at[4 * px + 2 * py + pc]

        def copy(k, block_of, to):
            return pltpu.make_async_remote_copy(src_ref=slot(*block_of), dst_ref=slot(*block_of), send_sem=send_sems.at[k],
                                                recv_sem=recv_sems.at[k], device_id=to, device_id_type=MESH)

        slot(*me)[...] = x_ref[...]
        first = [copy(0, me, sibling)] + [copy(1 + j, me, (*chip, c)) for j, chip in enumerate(chips)]
        for cp in first:
            cp.start()
        passed = [copy(4 + j, (*chip, c), sibling) for j, chip in enumerate(chips)]
        for j, chip in enumerate(chips):
            copy(1 + j, (*chip, c), me).wait_recv()
            passed[j].start()
        copy(0, sibling, me).wait_recv()
        for j, chip in enumerate(chips):
            copy(4 + j, (*chip, 1 - c), me).wait_recv()
        for cp in first + passed:
            cp.wait_send()
        for r0 in range(0, rows, tr):
            acc = buf[0, r0:r0 + tr, :]
            for k in range(1, N_DEV):
                acc = acc + buf[k, r0:r0 + tr, :]
            o_ref[r0:r0 + tr, :] = acc

    vm = pl.BlockSpec(memory_space=pltpu.VMEM)
    return pl.pallas_call(
        body, name="allreduce_small", in_specs=[vm], out_specs=vm, out_shape=_sds((rows, lanes), block.dtype),
        scratch_shapes=[pltpu.VMEM((N_DEV, rows, lanes), block.dtype), pltpu.SemaphoreType.DMA((7,)), pltpu.SemaphoreType.DMA((7,))],
        compiler_params=pltpu.CompilerParams(vmem_limit_bytes=VMEM_LIMIT),
    )(block)


def _small_gather_side(block):
    def copy(kind, j, ins, outs, send_sems, recv_sems):
        x, y, c, chips = _place()
        chip = chips[j] if j is not None else None
        slot = lambda px, py, pc: outs[0].at[4 * px + 2 * py + pc]

        def remote(k, src, dst, to):
            return pltpu.make_async_remote_copy(src_ref=src, dst_ref=dst, send_sem=send_sems.at[k], recv_sem=recv_sems.at[k],
                                                device_id=to, device_id_type=MESH)

        if kind == "to_sibling":
            return remote(0, ins[0], slot(x, y, c), (x, y, 1 - c))
        if kind == "from_sibling":
            return remote(0, ins[0], slot(x, y, 1 - c), (x, y, c))
        if kind == "to_chip":
            return remote(1 + j, ins[0], slot(x, y, c), (*chip, c))
        if kind == "from_chip":
            return remote(1 + j, ins[0], slot(*chip, c), (x, y, c))
        if kind == "pass_on":
            return remote(4 + j, slot(*chip, c), slot(*chip, c), (x, y, 1 - c))
        return remote(4 + j, slot(*chip, 1 - c), slot(*chip, 1 - c), (x, y, c))

    def first(*refs):
        copy("to_sibling", None, *refs).start()
        for j in range(3):
            copy("to_chip", j, *refs).start()

    def mid(*refs):
        for j in range(3):
            copy("from_chip", j, *refs).wait_recv()
            copy("pass_on", j, *refs).start()

    def last(*refs):
        copy("from_sibling", None, *refs).wait_recv()
        for j in range(3):
            copy("passed_on", j, *refs).wait_recv()
        copy("to_sibling", None, *refs).wait_send()
        for j in range(3):
            copy("to_chip", j, *refs).wait_send()
            copy("pass_on", j, *refs).wait_send()

    return _Side([block], [_sds((N_DEV, *block.shape), block.dtype)], 7, first, last, mid=mid, mid_late=True)


def _sum_slots(name, own, gathered, me, tr=512):
    n, rows, cols = gathered.shape
    tr = _pick(rows, tr, SUBLANES)
    if tr < 64:
        tr = rows

    def body(me_ref, own_ref, g_ref, o_ref):
        mine = own_ref[...]
        acc = jnp.where(me_ref[0] == 0, mine, g_ref[0])
        for k in range(1, n):
            acc = acc + jnp.where(me_ref[0] == k, mine, g_ref[k])
        o_ref[...] = acc

    grid_spec = pltpu.PrefetchScalarGridSpec(
        num_scalar_prefetch=1, grid=(rows // tr,),
        in_specs=[pl.BlockSpec((tr, cols), lambda i, me_ref: (i, 0)), pl.BlockSpec((n, tr, cols), lambda i, me_ref: (0, i, 0))],
        out_specs=pl.BlockSpec((tr, cols), lambda i, me_ref: (i, 0)))
    return pl.pallas_call(body, name=name, grid_spec=grid_spec, out_shape=_sds((rows, cols), own.dtype),
                          compiler_params=_params(("arbitrary",)))(me.reshape(1).astype(jnp.int32), own, gathered)


def _sum_received(name, full, c, shard, swapped, received, tr=256):
    n, rows, cols = received.shape
    tr = _pick(rows, tr, 16)

    def body(i_ref, a_ref, b_ref, s_ref, o_ref):
        acc = a_ref[...] + b_ref[...]
        for k in range(n):
            acc = acc + s_ref[k].astype(F32)
        o_ref[...] = acc

    grid_spec = pltpu.PrefetchScalarGridSpec(
        num_scalar_prefetch=1, grid=(rows // tr,),
        in_specs=[pl.BlockSpec((None, None, tr, cols), lambda i, i_ref: (i_ref[1], i_ref[0], i, 0)),
                  pl.BlockSpec((None, tr, cols), lambda i, i_ref: (i_ref[1], i, 0)),
                  pl.BlockSpec((n, tr, cols), lambda i, i_ref: (0, i, 0))],
        out_specs=pl.BlockSpec((None, tr, cols), lambda i, i_ref: (i_ref[0], i, 0)))
    return pl.pallas_call(body, name=name, grid_spec=grid_spec, out_shape=_sds((2, rows, cols), F32),
                          compiler_params=_params(("arbitrary",)))(jnp.stack([c, shard]).astype(jnp.int32), full, swapped, received)


def _add_halves(name, full, c, shard, received, tr=256):
    s, _, rh, cols = full.shape
    tr = _pick(rh, tr, 16)

    def body(i_ref, a_ref, b_ref, o_ref):
        o_ref[...] = (a_ref[...] + b_ref[...]).astype(BF16)

    other = lambda q, i_ref: (i_ref[1] + 1 + q) % s
    grid_spec = pltpu.PrefetchScalarGridSpec(
        num_scalar_prefetch=1, grid=(s - 1, rh // tr),
        in_specs=[pl.BlockSpec((None, None, tr, cols), lambda q, i, i_ref: (other(q, i_ref), i_ref[0], i, 0)),
                  pl.BlockSpec((None, tr, cols), lambda q, i, i_ref: (other(q, i_ref), i, 0))],
        out_specs=pl.BlockSpec((None, tr, cols), lambda q, i, i_ref: (other(q, i_ref), i, 0)))
    return pl.pallas_call(body, name=name, grid_spec=grid_spec, out_shape=_sds((s, rh, cols), BF16),
                          compiler_params=_params(("arbitrary", "arbitrary")))(jnp.stack([c, shard]).astype(jnp.int32), full, received)


LARGE = ("w_in", "ssm_glu_w", "w_out", "w_ffn_in", "w_ffn_out", "w_ple_gate", "w_ple_proj")
COLUMN_SHARDED = ("w_in", "w_ffn_in", "w_ple_proj")
SMALL = ("norm_mix_g", "ssm_lambda_re", "ssm_lambda_im", "ssm_log_step", "ssm_b_re", "ssm_b_im", "ssm_c_re", "ssm_c_im",
         "ssm_d", "ssm_glu_b", "sgu_ln_g", "sgu_ln_b", "sgu_w", "sgu_b", "out_norm_ssm_g", "out_norm_sgu_g", "norm_ffn_g",
         "norm_ple_g", "b_ple_gate", "final_norm_g")
WEIGHTS = ("norm_mix_g", "w_in", "ssm_lambda_re", "ssm_lambda_im", "ssm_log_step", "ssm_b_re", "ssm_b_im", "ssm_c_re",
           "ssm_c_im", "ssm_d", "ssm_glu_w", "ssm_glu_b", "sgu_ln_g", "sgu_ln_b", "sgu_w", "sgu_b", "out_norm_ssm_g",
           "out_norm_sgu_g", "w_out", "norm_ffn_g", "w_ffn_in", "w_ffn_out", "norm_ple_g", "w_ple_gate", "b_ple_gate",
           "w_ple_proj", "final_norm_g")
PACK_ROWS = SUBLANES * LANES


def _pack(arrays):
    parts = []
    for a in arrays:
        flat = a.reshape(-1).astype(F32)
        pad = -flat.shape[0] % PACK_ROWS
        parts.append(jnp.pad(flat, (0, pad)) if pad else flat)
    return jnp.concatenate(parts).reshape(-1, LANES)


def _unpack(packed, like):
    flat = packed.reshape(-1)
    out, at = [], 0
    for a in like:
        size = a.size
        out.append(flat[at:at + size].reshape(a.shape))
        at += size + (-size % PACK_ROWS)
    return out


class _NoExchange:
    def __init__(self, weights):
        self.weights, self.grads, self.small = weights, {}, {}

    def weight(self, name):
        return self.weights[name]

    def grad(self, name, g):
        self.grads[name] = g

    def small_grads(self, grads):
        self.small.update(grads)

    def side(self, host):
        return None


class _MeshExchange:
    GATHER = {"norm_mix": (("w_in", 0, 16),),
              "proj_in": (("ssm_glu_w", 0, 16), ("w_out", 0, 16), ("w_ffn_in", 0, 1)),
              "ssm_fwd": (("w_ffn_in", 1, 13),),
              "proj_out": (("w_ffn_in", 13, 16),),
              "ffn_in": (("w_ffn_out", 0, 16), ("w_ple_gate", 0, 16), ("w_ple_proj", 0, 16))}
    GATHER_LONG = ("norm_mix", "proj_in", "ssm_fwd", "proj_out")
    SWAP = {"d_act": ("w_ple_proj", "w_ple_gate", "w_ffn_out"), "d_h2": ("w_ffn_in",), "d_ya0": ("w_out", "ssm_glu_w")}
    SWAP_ALONE = ("w_in",)
    SCATTER = {"d_ffn_in": ("w_ple_proj", "w_ple_gate", "w_ffn_out"), "ssm_bwd": ("w_ffn_in",),
               "d_sgu": ("w_out", "ssm_glu_w"), "d_h1": ("w_in",)}
    SMALL_GATHER = "d_proj_in"

    def __init__(self, shards, small_like, c, shard, me):
        self.c, self.shard, self.me, self.small_like = c, shard, me, small_like
        self.slots = {k: _cast_into_slot("cast_" + k, shards[k], shard) for k in LARGE}
        self.full, self.received, self.halves, self.quarters, self.small = {}, {}, {}, {}, {}

    def weight(self, name):
        g = self.slots[name]
        _, _, rh, cols = g.shape
        return g.reshape(N_CHIPS, 2 * rh, cols) if name in COLUMN_SHARDED else g.reshape(N_CHIPS * 2 * rh, cols)

    def grad(self, name, g):
        if name not in COLUMN_SHARDED:
            g = g.reshape(N_CHIPS, g.shape[0] // N_CHIPS, g.shape[1])
        self.full[name] = g.reshape(N_CHIPS, 2, g.shape[1] // 2, g.shape[2])
        if name in self.SWAP_ALONE:
            self._swapped((name,), _exchange_alone("grad_swap_" + name, _swap_side([self.full[name]])))

    def _swapped(self, names, received):
        for k, r in zip(names, received):
            self.received[k] = r
            self.halves[k] = _add_halves("grad_add_halves_" + k, self.full[k], self.c, self.shard, r)

    def small_grads(self, grads):
        self.small.update(grads)

    def _packed(self, names):
        return _pack([self.small[k].reshape(self.small_like[k].shape) for k in names])

    def side(self, host):
        if host in self.GATHER:
            return _gather_side([self.slots[k] for k, _, _ in self.GATHER[host]], [(lo, hi) for _, lo, hi in self.GATHER[host]],
                                mid_late=host in self.GATHER_LONG)
        if host in self.SWAP:
            return _swap_side([self.full[k] for k in self.SWAP[host]])
        if host in self.SCATTER:
            return _scatter_side([self.halves[k] for k in self.SCATTER[host]])
        if host == self.SMALL_GATHER:
            self.packed_early = self._packed(SMALL[1:] + ("loss",))
            return _small_gather_side(self.packed_early)
        return None

    def done(self, host, moved):
        if host in self.GATHER:
            self.slots.update(zip([k for k, _, _ in self.GATHER[host]], moved))
        elif host in self.SWAP:
            self._swapped(self.SWAP[host], moved)
        elif host in self.SCATTER:
            self.quarters.update(zip(self.SCATTER[host], moved))
        else:
            (self.gathered_early,) = moved

    def small_reduced(self):
        early = _sum_slots("small_sum", self.packed_early, self.gathered_early, self.me)
        late = _allreduce_small(self._packed(SMALL[:1]))
        loss_at = early.shape[0] - PACK_ROWS // LANES
        return jnp.concatenate([late, early[:loss_at]], axis=0), early[loss_at, 0]

    def summed(self):
        return [_sum_received("grad_sum_" + k, self.full[k], self.c, self.shard, self.received[k], self.quarters[k]) for k in LARGE]


def kernel(x, p, norm_mix_g, w_in, ssm_lambda_re, ssm_lambda_im, ssm_log_step, ssm_b_re, ssm_b_im, ssm_c_re, ssm_c_im, ssm_d, ssm_glu_w, ssm_glu_b, sgu_ln_g, sgu_ln_b, sgu_w, sgu_b, out_norm_ssm_g, out_norm_sgu_g, w_out, norm_ffn_g, w_ffn_in, w_ffn_out, norm_ple_g, w_ple_gate, b_ple_gate, w_ple_proj, final_norm_g, loss_target, m_norm_mix_g, m_w_in, m_ssm_lambda_re, m_ssm_lambda_im, m_ssm_log_step, m_ssm_b_re, m_ssm_b_im, m_ssm_c_re, m_ssm_c_im, m_ssm_d, m_ssm_glu_w, m_ssm_glu_b, m_sgu_ln_g, m_sgu_ln_b, m_sgu_w, m_sgu_b, m_out_norm_ssm_g, m_out_norm_sgu_g, m_w_out, m_norm_ffn_g, m_w_ffn_in, m_w_ffn_out, m_norm_ple_g, m_w_ple_gate, m_b_ple_gate, m_w_ple_proj, m_final_norm_g, v_norm_mix_g, v_w_in, v_ssm_lambda_re, v_ssm_lambda_im, v_ssm_log_step, v_ssm_b_re, v_ssm_b_im, v_ssm_c_re, v_ssm_c_im, v_ssm_d, v_ssm_glu_w, v_ssm_glu_b, v_sgu_ln_g, v_sgu_ln_b, v_sgu_w, v_sgu_b, v_out_norm_ssm_g, v_out_norm_sgu_g, v_w_out, v_norm_ffn_g, v_w_ffn_in, v_w_ffn_out, v_norm_ple_g, v_w_ple_gate, v_b_ple_gate, v_w_ple_proj, v_final_norm_g):
    given = dict(locals())
    w = {k: given[k] for k in WEIGHTS}
    m = {k: given["m_" + k] for k in WEIGHTS}
    v = {k: given["v_" + k] for k in WEIGHTS}
    c = lax.axis_index("c")
    shard = 2 * lax.axis_index("x") + lax.axis_index("y")

    small_like = {k: w[k] for k in SMALL}
    small_like["loss"] = _sds((1, LANES), F32)
    exch = _MeshExchange({k: w[k].reshape(w[k].shape[1:]) for k in LARGE}, small_like, c, shard, 2 * shard + c)
    unlayer = lambda a: a if a.ndim == 1 else a[0]
    sp = {k: unlayer(w[k]) for k in SMALL}
    n_tok, d_model = x.shape[1:]
    _, grad_x = _local_grads(x.reshape(n_tok, d_model), p.reshape(n_tok, p.shape[-1]),
                             loss_target.reshape(n_tok, d_model), sp, exch)

    grad_w, delta_w, new_m, new_v = {}, {}, {}, {}
    halves = exch.summed()
    packed_g, loss = exch.small_reduced()
    like = _sds(packed_g.shape, F32)
    (d_s, m_s, v_s), joined = _rowwise(
        "adamw_small", _adamw, [_pack([w[k] for k in SMALL]), packed_g, _pack([m[k] for k in SMALL]), _pack([v[k] for k in SMALL])],
        [], [like, like, like], side=_join_side(halves))
    shapes = [w[k] for k in SMALL]
    for k, g_k, d_k, m_k, v_k in zip(SMALL, _unpack(packed_g, shapes), _unpack(d_s, shapes), _unpack(m_s, shapes), _unpack(v_s, shapes)):
        grad_w[k], delta_w[k], new_m[k], new_v[k] = g_k, d_k, m_k, v_k

    reduced = {k: j.reshape(2 * j.shape[1], j.shape[2]) for k, j in zip(LARGE, joined)}
    for k in LARGE:
        shape = w[k].shape
        two_d = lambda a: a.reshape(shape[1:])
        like = _sds(shape[1:], F32)
        update = lambda w_t, g_t, m_t, v_t: (g_t, *_adamw(w_t, g_t, m_t, v_t))
        outs = _rowwise("adamw_" + k, update, [two_d(w[k]), reduced[k], two_d(m[k]), two_d(v[k])], [], [like, like, like, like])
        grad_w[k], delta_w[k], new_m[k], new_v[k] = (a.reshape(shape) for a in outs)

    return (loss, grad_x.reshape(x.shape), *[grad_w[k] for k in WEIGHTS], *[delta_w[k] for k in WEIGHTS],
            *[new_m[k] for k in WEIGHTS], *[new_v[k] for k in WEIGHTS])
```

```python
import functools

import jax
import jax.numpy as jnp
from jax import lax
from jax.experimental import pallas as pl
from jax.experimental.pallas import tpu as pltpu

F32 = jnp.float32
BF16 = jnp.bfloat16

EPS = 1e-6
LAMBDA_RE_MAX = -1e-4
ADAM_LR = 0.001
ADAM_B1 = 0.9
ADAM_B2 = 0.999
ADAM_EPS = 1e-08
ADAM_WD = 0.01
ADAM_STEP = 10

N_CHIPS = 4
N_DEV = 8
SUBLANES = 8
LANES = 128
SSM_CH_BLOCK = 256
SCAN_LANES = 256
SCAN_BLOCKS = 4
GATHER_PARTS = 16
VMEM_LIMIT = 56 * 1024 * 1024

MESH = pl.DeviceIdType.MESH


def _pick(n, pref, mult):
    if n <= pref:
        return n
    t = (pref // mult) * mult
    while t >= mult:
        if n % t == 0:
            return t
        t -= mult
    return n


def _params(semantics):
    return pltpu.CompilerParams(dimension_semantics=semantics, vmem_limit_bytes=VMEM_LIMIT)


class _Cols:
    def __init__(self, arr, width, blk):
        self.arr, self.width, self.blk = arr, width, blk


def _sds(shape, dtype):
    return jax.ShapeDtypeStruct(tuple(shape), dtype)


ANY = pl.BlockSpec(memory_space=pl.ANY)


class _Side:
    def __init__(self, ins, out_shapes, n_sems, first, last, mid=None, aliases=None, mid_late=False):
        self.ins, self.out_shapes, self.n_sems = list(ins), list(out_shapes), n_sems
        self.first, self.mid, self.last, self.mid_late = first, mid, last, mid_late
        self.aliases = dict(aliases or {})


def _call(body, side, operands, *, name, grid, in_specs, out_specs, out_shape, compiler_params, scratch_shapes=()):
    if side is None:
        return pl.pallas_call(body, name=name, grid=grid, in_specs=in_specs, out_specs=out_specs, out_shape=out_shape,
                              scratch_shapes=list(scratch_shapes), compiler_params=compiler_params)(*operands)
    single = not isinstance(out_specs, (list, tuple))
    out_specs = [out_specs] if single else list(out_specs)
    out_shape = [out_shape] if single else list(out_shape)
    n_in, n_out, n_scr = len(in_specs), len(out_specs), len(scratch_shapes)
    n_sin, n_sout = len(side.ins), len(side.out_shapes)
    steps = 1
    for g in grid:
        steps *= g

    def hosted(*refs):
        ins, s_ins = refs[:n_in], refs[n_in:n_in + n_sin]
        at = n_in + n_sin
        outs, s_outs = refs[at:at + n_out], refs[at + n_out:at + n_out + n_sout]
        scratch = refs[at + n_out + n_sout:at + n_out + n_sout + n_scr]
        sems = refs[-2:]
        step = pl.program_id(0)
        for d in range(1, len(grid)):
            step = step * grid[d] + pl.program_id(d)

        @pl.when(step == 0)
        def _():
            side.first(s_ins, s_outs, *sems)

        if side.mid is not None:
            @pl.when(step == (steps - 1 if side.mid_late else (3 * steps) // 4))
            def _():
                side.mid(s_ins, s_outs, *sems)

        body(*ins, *outs, *scratch)

        @pl.when(step == steps - 1)
        def _():
            side.last(s_ins, s_outs, *sems)

    res = pl.pallas_call(
        hosted, name=name, grid=grid, in_specs=[*in_specs, *[ANY] * n_sin], out_specs=[*out_specs, *[ANY] * n_sout],
        out_shape=[*out_shape, *side.out_shapes], input_output_aliases={n_in + i: n_out + o for i, o in side.aliases.items()},
        scratch_shapes=[*scratch_shapes, pltpu.SemaphoreType.DMA((side.n_sems,)), pltpu.SemaphoreType.DMA((side.n_sems,))],
        compiler_params=compiler_params)(*operands, *side.ins)
    return (res[0] if single else list(res[:n_out])), list(res[n_out:])


def _rowwise(name, fn, rows, params, row_outs, acc_outs=(), tr=256, side=None):
    rows = [r if isinstance(r, _Cols) else _Cols(r, r.shape[1], 0) for r in rows]
    m = rows[0].arr.shape[0]
    tr = _pick(m, tr, 16)
    n_in = len(rows) + len(params)
    n_ro = len(row_outs)

    def body(*refs):
        vals = fn(*[r[...] for r in refs[:n_in]])
        if not isinstance(vals, (tuple, list)):
            vals = (vals,)
        outs = refs[n_in:]
        for r, v in zip(outs[:n_ro], vals[:n_ro]):
            r[...] = v.astype(r.dtype)
        first = pl.program_id(0) == 0
        for r, v in zip(outs[n_ro:], vals[n_ro:]):
            @pl.when(first)
            def _():
                r[...] = jnp.zeros(r.shape, r.dtype)
            r[...] += v.astype(r.dtype).reshape(r.shape)

    in_specs = [pl.BlockSpec((tr, r.width), lambda i, b=r.blk: (i, b)) for r in rows]
    in_specs += [pl.BlockSpec(p.shape, lambda i, nd=p.ndim: (0,) * nd) for p in params]
    out_specs = [pl.BlockSpec((tr, o.shape[1]), lambda i: (i, 0)) for o in row_outs]
    out_specs += [pl.BlockSpec(o.shape, lambda i, nd=len(o.shape): (0,) * nd) for o in acc_outs]
    return _call(body, side, [*[r.arr for r in rows], *params], name=name, grid=(m // tr,), in_specs=in_specs,
                 out_specs=out_specs, out_shape=[*row_outs, *acc_outs], compiler_params=_params(("arbitrary",)))


def _grid_order(swap):
    if not swap:
        return (lambda grid: grid), (lambda f: f)
    return (lambda grid: grid[::-1]), (lambda f: (lambda j, i: f(i, j)))


def _mm_nn(name, a, w, *, sharded=False, res=None, out_dtype=F32, tm=512, tn=512, w_resident=False, side=None):
    m, k = a.shape
    tm = _pick(m, tm, 16)
    order, ix = _grid_order(w_resident)
    if sharded:
        s, _, ns = w.shape
        n = s * ns
        tn = _pick(ns, tn, LANES)
        per = ns // tn
        w_spec = pl.BlockSpec((None, k, tn), ix(lambda i, j: (j // per, 0, j % per)))
    else:
        n = w.shape[1]
        tn = _pick(n, tn, LANES)
        w_spec = pl.BlockSpec((k, tn), ix(lambda i, j: (0, j)))

    def body(a_ref, w_ref, *rest):
        acc = jnp.dot(a_ref[...], w_ref[...], preferred_element_type=F32)
        if res is not None:
            acc = acc + rest[0][...]
        rest[-1][...] = acc.astype(out_dtype)

    in_specs = [pl.BlockSpec((tm, k), ix(lambda i, j: (i, 0))), w_spec]
    ops = [a, w]
    if res is not None:
        in_specs.append(pl.BlockSpec((tm, tn), ix(lambda i, j: (i, j))))
        ops.append(res)
    return _call(body, side, ops, name=name, grid=order((m // tm, n // tn)), in_specs=in_specs,
                 out_specs=pl.BlockSpec((tm, tn), ix(lambda i, j: (i, j))), out_shape=_sds((m, n), out_dtype),
                 compiler_params=_params(("arbitrary", "arbitrary")))


def _mm_nt(name, g, w, *, sharded=False, g_halves=False, out_dtype=F32, tm=512, tk=512, w_resident=False, side=None):
    m, n = g.shape[-2:]
    tm = _pick(m, tm, 16)
    order, ix = _grid_order(w_resident)
    dims = (((1,), (1,)), ((), ()))
    g_spec = pl.BlockSpec((2, tm, n), ix(lambda i, j: (0, i, 0))) if g_halves else pl.BlockSpec((tm, n), ix(lambda i, j: (i, 0)))
    if sharded:
        s, k, ns = w.shape
        tk = _pick(k, tk, LANES)
        w_spec = pl.BlockSpec((s, tk, ns), ix(lambda i, j: (0, j, 0)))

        def columns(g_ref, q):
            if not g_halves:
                return g_ref[:, q * ns:(q + 1) * ns]
            half, at = divmod(q, s // 2)
            return g_ref[half, :, at * ns:(at + 1) * ns]

        def body(g_ref, w_ref, o_ref):
            acc = lax.dot_general(columns(g_ref, 0), w_ref[0], dims, preferred_element_type=F32)
            for q in range(1, s):
                acc = acc + lax.dot_general(columns(g_ref, q), w_ref[q], dims, preferred_element_type=F32)
            o_ref[...] = acc.astype(out_dtype)
    else:
        k = w.shape[0]
        tk = _pick(k, tk, LANES)
        w_spec = pl.BlockSpec((tk, n), ix(lambda i, j: (j, 0)))

        def body(g_ref, w_ref, o_ref):
            o_ref[...] = lax.dot_general(g_ref[...], w_ref[...], dims, preferred_element_type=F32).astype(out_dtype)

    return _call(body, side, [g, w], name=name, grid=order((m // tm, k // tk)), in_specs=[g_spec, w_spec],
                 out_specs=pl.BlockSpec((tm, tk), ix(lambda i, j: (i, j))), out_shape=_sds((m, k), out_dtype),
                 compiler_params=_params(("arbitrary", "arbitrary")))


def _mm_tn(name, a, g, *, shards=0, g_halves=False, tk=512, tn=512, g_resident=False, side=None):
    m, k = a.shape
    n = 2 * g.shape[2] if g_halves else g.shape[1]
    tk = _pick(k, tk, LANES)
    order, ix = _grid_order(g_resident)
    dims = (((0,), (0,)), ((), ()))
    if shards:
        ns = n // shards
        tn = _pick(ns, tn, LANES)
        per = ns // tn
        out_spec = pl.BlockSpec((None, tk, tn), ix(lambda i, j: (j // per, i, j % per)))
        out_shape = _sds((shards, k, ns), F32)
    else:
        tn = _pick(n, tn, LANES)
        out_spec = pl.BlockSpec((tk, tn), ix(lambda i, j: (i, j)))
        out_shape = _sds((k, n), F32)

    def body(a_ref, g_ref, o_ref):
        o_ref[...] = lax.dot_general(a_ref[...], g_ref[...], dims, preferred_element_type=F32)

    if g_halves:
        per_half = n // 2 // tn
        g_spec = pl.BlockSpec((None, m, tn), ix(lambda i, j: (j // per_half, 0, j % per_half)))
    else:
        g_spec = pl.BlockSpec((m, tn), ix(lambda i, j: (0, j)))
    return _call(body, side, [a, g], name=name, grid=order((k // tk, n // tn)),
                 in_specs=[pl.BlockSpec((m, tk), ix(lambda i, j: (0, i))), g_spec],
                 out_specs=out_spec, out_shape=out_shape, compiler_params=_params(("arbitrary", "arbitrary")))


def _ffn_in_swiglu(name, a, w, *, tm=512, tn=1408, side=None):
    m, k = a.shape
    s, _, ns = w.shape
    f = s * ns // 2
    tm = _pick(m, tm, 16)
    tn = _pick(ns, tn, LANES)
    per = ns // tn
    order, ix = _grid_order(True)

    def body(a_ref, wg_ref, wu_ref, act_ref, gu_ref):
        x = a_ref[...]
        gate = jnp.dot(x, wg_ref[...], preferred_element_type=F32)
        up = jnp.dot(x, wu_ref[...], preferred_element_type=F32)
        act_ref[...] = _swiglu(gate, up).astype(BF16)
        gu_ref[0] = gate.astype(BF16)
        gu_ref[1] = up.astype(BF16)

    return _call(body, side, [a, w, w], name=name, grid=order((m // tm, f // tn)),
                 in_specs=[pl.BlockSpec((tm, k), ix(lambda i, j: (i, 0))),
                           pl.BlockSpec((None, k, tn), ix(lambda i, j: (j // per, 0, j % per))),
                           pl.BlockSpec((None, k, tn), ix(lambda i, j: (s // 2 + j // per, 0, j % per)))],
                 out_specs=[pl.BlockSpec((tm, tn), ix(lambda i, j: (i, j))), pl.BlockSpec((2, tm, tn), ix(lambda i, j: (0, i, j)))],
                 out_shape=[_sds((m, f), BF16), _sds((2, m, f), BF16)], compiler_params=_params(("arbitrary", "arbitrary")))


def _d_act_swiglu(name, g, w, gu, *, tm=1024, tk=512, side=None):
    m, n = g.shape
    f = w.shape[0]
    tm = _pick(m, tm, 16)
    tk = _pick(f, tk, LANES)
    dims = (((1,), (1,)), ((), ()))

    def body(g_ref, w_ref, gu_ref, o_ref):
        dact = lax.dot_general(g_ref[...], w_ref[...], dims, preferred_element_type=F32)
        _, vjp = jax.vjp(_swiglu, gu_ref[0].astype(F32), gu_ref[1].astype(F32))
        dgate, dup = vjp(dact)
        o_ref[0] = dgate.astype(BF16)
        o_ref[1] = dup.astype(BF16)

    return _call(body, side, [g, w, gu], name=name, grid=(m // tm, f // tk),
                 in_specs=[pl.BlockSpec((tm, n), lambda i, j: (i, 0)), pl.BlockSpec((tk, n), lambda i, j: (j, 0)),
                           pl.BlockSpec((2, tm, tk), lambda i, j: (0, i, j))],
                 out_specs=pl.BlockSpec((2, tm, tk), lambda i, j: (0, i, j)), out_shape=_sds((2, m, f), BF16),
                 compiler_params=_params(("arbitrary", "arbitrary")))


def _rms(x, g):
    r = lax.rsqrt(jnp.mean(x * x, axis=-1, keepdims=True) + EPS)
    return (x * r) * g


def _glu_out(y_pre, q, glu_b, g_norm):
    ya0 = jax.nn.gelu(y_pre)
    return _rms(ya0 * jax.nn.sigmoid(q + glu_b), g_norm)


def _sgu_rows(zu, zv, ln_g, ln_b, w_s, b_st, g_norm):
    heads, t, _ = w_s.shape
    hd = zu.shape[1] // heads
    uu = jax.nn.gelu(zu)
    vv = jax.nn.gelu(zv)
    mu = jnp.mean(vv, axis=-1, keepdims=True)
    xc = vv - mu
    r = lax.rsqrt(jnp.mean(xc * xc, axis=-1, keepdims=True) + EPS)
    vn = (xc * r) * ln_g + ln_b
    row = lax.broadcasted_iota(jnp.int32, (t, t), 0)
    col = lax.broadcasted_iota(jnp.int32, (t, t), 1)
    causal = row >= col
    chunks = []
    for n in range(zu.shape[0] // t):
        blocks = []
        for h in range(heads):
            wm = jnp.where(causal, w_s[h], jnp.zeros_like(w_s[h])).astype(BF16)
            vb = vn[n * t:(n + 1) * t, h * hd:(h + 1) * hd].astype(BF16)
            blocks.append(jnp.dot(wm, vb, preferred_element_type=F32) + b_st[:, h:h + 1])
        chunks.append(jnp.concatenate(blocks, axis=1))
    s = jnp.concatenate(chunks, axis=0) if len(chunks) > 1 else chunks[0]
    return _rms(uu * s, g_norm)


def _swiglu(gate, up):
    return jax.nn.silu(gate) * up


def _head_loss(x2, gpre, pp, b_g, g_final, target):
    gate = jax.nn.sigmoid(gpre + b_g)
    out = _rms(x2 + gate * pp, g_final)
    err = jnp.square(out - target)
    return 0.5 * jnp.sum(jnp.mean(err, axis=-1))


def _ssm_disc(lam_re, lam_im, log_step):
    lr = jnp.minimum(lam_re, LAMBDA_RE_MAX)
    li = lam_im
    dt = jnp.exp(log_step)
    mag = jnp.exp(lr * dt)
    ang = li * dt
    abar_re = mag * jnp.cos(ang)
    abar_im = mag * jnp.sin(ang)
    nr = abar_re - 1.0
    ni = abar_im
    den = lr * lr + li * li
    q_re = (nr * lr + ni * li) / den
    q_im = (ni * lr - nr * li) / den
    return abar_re, abar_im, q_re, q_im


def _ssm_bbar(q_re, q_im, b_re, b_im):
    return q_re * b_re - q_im * b_im, q_re * b_im + q_im * b_re


def _ssm_discretised(lam_re, lam_im, log_step, bt_re, bt_im):
    ar, ai, qr, qi = _ssm_disc(lam_re, lam_im, log_step)
    return (ar, ai, *_ssm_bbar(qr, qi, bt_re, bt_im))


def _adamw(w, g, m, v):
    m = ADAM_B1 * m + (1.0 - ADAM_B1) * g
    v = ADAM_B2 * v + (1.0 - ADAM_B2) * jnp.square(g)
    m_hat = m / (1.0 - ADAM_B1 ** ADAM_STEP)
    v_hat = v / (1.0 - ADAM_B2 ** ADAM_STEP)
    delta = -ADAM_LR * (m_hat / (jnp.sqrt(v_hat) + ADAM_EPS) + ADAM_WD * w)
    return delta, m, v


class _SsmDims:
    def __init__(self, groups, state, gch):
        self.g, self.p, self.h = groups, state, gch
        self.d = groups * gch
        self.cb = min(SSM_CH_BLOCK, self.d)
        self.gb = self.cb // gch
        self.ns = self.gb * state
        self.nb = self.d // self.cb


def _ssm_rows(sd, sp):
    gp = sd.g * sd.p
    log_step = jnp.broadcast_to(sp["ssm_log_step"][:, None], (sd.g, sd.p)).reshape(1, gp)
    bt = [sp[k].reshape(gp, sd.h).T for k in ("ssm_b_re", "ssm_b_im")]
    ct = [sp[k].transpose(1, 0, 2).reshape(sd.h, gp) for k in ("ssm_c_re", "ssm_c_im")]
    return (sp["ssm_lambda_re"].reshape(1, gp), sp["ssm_lambda_im"].reshape(1, gp), log_step, *bt, *ct)


def _block_mask(sd):
    row = lax.broadcasted_iota(jnp.int32, (sd.cb, sd.ns), 0) // sd.h
    col = lax.broadcasted_iota(jnp.int32, (sd.cb, sd.ns), 1) // sd.p
    return row == col


def _scan_consts(pr, pi_, reverse):
    if reverse:
        pi_ = [-v for v in pi_]
    shape = (SUBLANES, pr[0].shape[1])
    rows = lax.broadcasted_iota(jnp.int32, shape, 0)
    parts = []
    for d in (1, 2, 4):
        keep = (rows < SUBLANES - d) if reverse else (rows >= d)
        parts += [jnp.where(keep, jnp.broadcast_to(v[d - 1], shape), 0.0) for v in (pr, pi_)]
    order = range(SUBLANES - 1, -1, -1) if reverse else range(SUBLANES)
    parts += [jnp.concatenate([v[t] for t in order], axis=0) for v in (pr, pi_)]
    return jnp.concatenate(parts, axis=0)


def _ssm_operands(sd, rows):
    cb, ns, nb = sd.cb, sd.ns, sd.nb

    def body(lam_re, lam_im, log_step, bt_re, bt_im, ct_re, ct_im, wb_ref, wbt_ref, wc_ref, wct_ref, cst_f_ref, cst_r_ref):
        ar, ai, bbar_re, bbar_im = _ssm_discretised(lam_re[...], lam_im[...], log_step[...], bt_re[...], bt_im[...])
        pr, pi_ = [ar], [ai]
        for _ in range(SUBLANES - 1):
            pr, pi_ = pr + [pr[-1] * ar - pi_[-1] * ai], pi_ + [pr[-1] * ai + pi_[-1] * ar]
        mask = _block_mask(sd)
        spread = lambda src: jnp.where(mask, jnp.concatenate([src] * sd.gb, axis=0), 0.0)
        for j in range(nb):
            at = slice(j * ns, (j + 1) * ns)
            w = jnp.concatenate([spread(bbar_re[:, at]), spread(bbar_im[:, at])], axis=1)
            v = jnp.concatenate([spread(ct_re[:, at]), -spread(ct_im[:, at])], axis=1)
            wb_ref[j] = w.astype(BF16)
            wbt_ref[j] = w.T.astype(BF16)
            wct_ref[j] = v.astype(BF16)
            wc_ref[j] = v.T.astype(BF16)
            pj, qj = [u[:, at] for u in pr], [u[:, at] for u in pi_]
            cst_f_ref[j] = _scan_consts(pj, qj, False)
            cst_r_ref[j] = _scan_consts(pj, qj, True)

    wide, tall = _sds((nb, cb, 2 * ns), BF16), _sds((nb, 2 * ns, cb), BF16)
    cst = _sds((nb, 8 * SUBLANES, ns), F32)
    vm = pl.BlockSpec(memory_space=pltpu.VMEM)
    return pl.pallas_call(body, name="ssm_operands", in_specs=[vm] * 7, out_specs=[vm] * 6,
                          out_shape=[wide, tall, tall, wide, cst, cst],
                          compiler_params=pltpu.CompilerParams(vmem_limit_bytes=VMEM_LIMIT))(*rows)


def _ssm_param_grads(sd, rows, dwb, dwc, da):
    ns, nb, gp = sd.ns, sd.nb, sd.g * sd.p

    def body(lam_re, lam_im, log_step, bt_re, bt_im, dwb_v, dwc_v, da_v, *outs):
        mask = _block_mask(sd)

        def fold(dense):
            kept = jnp.where(mask, dense, 0.0)
            acc = kept[0:sd.h]
            for gl in range(1, sd.gb):
                acc = acc + kept[gl * sd.h:(gl + 1) * sd.h]
            return acc

        lanes = lambda parts: jnp.concatenate(parts, axis=1) if len(parts) > 1 else parts[0]
        dbbar_re = lanes([fold(dwb_v[j][:, :ns]) for j in range(nb)])
        dbbar_im = lanes([fold(dwb_v[j][:, ns:]) for j in range(nb)])
        dwct = [dwc_v[j] for j in range(nb)]
        d_ct_re = lanes([fold(t[:, :ns]) for t in dwct])
        d_ct_im = -lanes([fold(t[:, ns:]) for t in dwct])
        dabar_re = lanes([da_v[j][0:1, :ns] for j in range(nb)])
        dabar_im = lanes([da_v[j][0:1, ns:] for j in range(nb)])
        _, vjp = jax.vjp(_ssm_discretised, lam_re[...], lam_im[...], log_step[...], bt_re[...], bt_im[...])
        d_lr, d_li, d_ls, d_bt_re, d_bt_im = vjp((dabar_re, dabar_im, dbbar_re, dbbar_im))
        group = (lax.broadcasted_iota(jnp.int32, (gp, sd.g), 0) // sd.p == lax.broadcasted_iota(jnp.int32, (gp, sd.g), 1))
        d_log_step = jnp.dot(d_ls, group.astype(F32), precision=lax.Precision.HIGHEST, preferred_element_type=F32)
        for ref, val in zip(outs, (d_lr, d_li, d_log_step, d_bt_re, d_bt_im, d_ct_re, d_ct_im)):
            ref[...] = val

    row, mat = _sds((1, gp), F32), _sds((sd.h, gp), F32)
    vm = pl.BlockSpec(memory_space=pltpu.VMEM)
    return pl.pallas_call(body, name="ssm_param_grads", in_specs=[vm] * 8, out_specs=[vm] * 7,
                          out_shape=[row, row, _sds((1, sd.g), F32), mat, mat, mat, mat],
                          compiler_params=pltpu.CompilerParams(vmem_limit_bytes=VMEM_LIMIT))(*rows[:5], dwb, dwc, da)


def _block_scan(s_ref, cst_ref, carry_ref, sd, rows, reverse):
    ns = sd.ns
    nblk = rows // SUBLANES
    w = min(SCAN_LANES, ns)
    for c0 in range(0, ns, w):
        re_l, im_l = slice(c0, c0 + w), slice(ns + c0, ns + c0 + w)
        cst = [cst_ref[k * SUBLANES:(k + 1) * SUBLANES, c0:c0 + w] for k in range(8)]

        def step(k, carry, re_l=re_l, im_l=im_l, cst=cst):
            local = []
            for b in range(SCAN_BLOCKS):
                blk = SCAN_BLOCKS * k + b
                blk = (nblk - 1 - blk) if reverse else blk
                r0 = pl.multiple_of(blk * SUBLANES, SUBLANES)
                xr = s_ref[pl.ds(r0, SUBLANES), re_l]
                xi = s_ref[pl.ds(r0, SUBLANES), im_l]
                for n, d in enumerate((1, 2, 4)):
                    ar, ai = cst[2 * n], cst[2 * n + 1]
                    shift = (SUBLANES - d) if reverse else d
                    sr = pltpu.roll(xr, shift, 0)
                    si = pltpu.roll(xi, shift, 0)
                    xr, xi = xr + ar * sr - ai * si, xi + ar * si + ai * sr
                local.append((r0, xr, xi))
            cr, ci = carry
            edge = slice(0, 1) if reverse else slice(SUBLANES - 1, SUBLANES)
            for r0, xr, xi in local:
                br = jnp.broadcast_to(cr, xr.shape)
                bi = jnp.broadcast_to(ci, xi.shape)
                xr, xi = xr + cst[6] * br - cst[7] * bi, xi + cst[6] * bi + cst[7] * br
                s_ref[pl.ds(r0, SUBLANES), re_l] = xr
                s_ref[pl.ds(r0, SUBLANES), im_l] = xi
                cr, ci = xr[edge, :], xi[edge, :]
            return cr, ci

        cr, ci = lax.fori_loop(0, nblk // SCAN_BLOCKS, step, (carry_ref[0:1, re_l], carry_ref[0:1, im_l]))
        carry_ref[0:1, re_l] = cr
        carry_ref[0:1, im_l] = ci


def _ssm_fwd(name, sd, z, wb, wc, cst, d_row, tt=512, side=None):
    n_tok = z.shape[0]
    tt = _pick(n_tok, tt, 16)
    cb, ns2 = sd.cb, 2 * sd.ns

    def body(z_ref, wb_ref, wc_ref, cst_ref, d_ref, y_ref, s_ref, a0_ref, carry_ref):
        @pl.when(pl.program_id(1) == 0)
        def _():
            carry_ref[...] = jnp.zeros(carry_ref.shape, F32)
        u = z_ref[...]
        s_ref[...] = jnp.dot(u.astype(BF16), wb_ref[...], preferred_element_type=F32)
        _block_scan(s_ref, cst_ref, carry_ref, sd, tt, reverse=False)
        y = jnp.dot(s_ref[...].astype(BF16), wc_ref[...], preferred_element_type=F32) + d_ref[...] * u
        y_ref[...] = y
        a0_ref[...] = jax.nn.gelu(y).astype(BF16)

    return _call(
        body, side, [z, wb, wc, cst, d_row], name=name, grid=(sd.nb, n_tok // tt),
        in_specs=[pl.BlockSpec((tt, cb), lambda j, i: (i, j)),
                  pl.BlockSpec((None, cb, ns2), lambda j, i: (j, 0, 0)),
                  pl.BlockSpec((None, ns2, cb), lambda j, i: (j, 0, 0)),
                  pl.BlockSpec((None, 8 * SUBLANES, sd.ns), lambda j, i: (j, 0, 0)),
                  pl.BlockSpec((1, cb), lambda j, i: (0, j))],
        out_specs=[pl.BlockSpec((tt, cb), lambda j, i: (i, j)), pl.BlockSpec((tt, ns2), lambda j, i: (i, j)),
                   pl.BlockSpec((tt, cb), lambda j, i: (i, j))],
        out_shape=[_sds((n_tok, sd.d), F32), _sds((n_tok, sd.nb * ns2), F32), _sds((n_tok, sd.d), BF16)],
        scratch_shapes=[pltpu.VMEM((SUBLANES, ns2), F32)],
        compiler_params=_params(("arbitrary", "arbitrary")))


def _ssm_bwd(name, sd, y_pre, dy_direct, dya0, z, states, wct, wbt, cst_rev, d_row, tt=512, side=None):
    n_tok = z.shape[0]
    tt = _pick(n_tok, tt, 16)
    nt = n_tok // tt
    cb, ns, ns2 = sd.cb, sd.ns, 2 * sd.ns
    blocks_per_tile = tt // SUBLANES
    tn_dims = (((0,), (0,)), ((), ()))

    def body(y_ref, dyd_ref, dya0_ref, z_ref, s_ref, sp_ref, wct_ref, wbt_ref, cst_ref, d_ref,
             du_ref, dwb_ref, dwc_ref, da_ref, dd_ref, lam_ref, carry_ref):
        i = pl.program_id(1)

        @pl.when(i == 0)
        def _():
            carry_ref[...] = jnp.zeros(carry_ref.shape, F32)
            dwb_ref[...] = jnp.zeros(dwb_ref.shape, F32)
            dwc_ref[...] = jnp.zeros(dwc_ref.shape, F32)
            da_ref[...] = jnp.zeros(da_ref.shape, F32)
            dd_ref[...] = jnp.zeros(dd_ref.shape, F32)

        _, gelu_vjp = jax.vjp(jax.nn.gelu, y_ref[...])
        dy_t = dyd_ref[...] + gelu_vjp(dya0_ref[...].astype(F32))[0]
        u = z_ref[...]
        dy16 = dy_t.astype(BF16)
        lam_ref[...] = jnp.dot(dy16, wct_ref[...], preferred_element_type=F32)
        _block_scan(lam_ref, cst_ref, carry_ref, sd, tt, reverse=True)
        lam = lam_ref[...]
        lam16 = lam.astype(BF16)
        du_ref[...] = (jnp.dot(lam16, wbt_ref[...], preferred_element_type=F32) + d_ref[...] * dy_t).astype(BF16)
        dd_ref[0:1, :] += jnp.sum(dy_t * u, axis=0, keepdims=True)
        dwb_ref[...] += lax.dot_general(u.astype(BF16), lam16, tn_dims, preferred_element_type=F32)
        s = s_ref[...]
        dwc_ref[...] += lax.dot_general(dy16, s.astype(BF16), tn_dims, preferred_element_type=F32)
        before = jnp.where(i == nt - 1, 0.0, 1.0) * sp_ref[SUBLANES - 1:SUBLANES, :]
        first_row = lax.broadcasted_iota(jnp.int32, s.shape, 0) == 0
        prev = jnp.where(first_row, jnp.broadcast_to(before, s.shape), pltpu.roll(s, 1, 0))
        lr, li = lam[:, :ns], lam[:, ns:]
        pr, pi_ = prev[:, :ns], prev[:, ns:]
        da_ref[0:1, 0:ns] += jnp.sum(lr * pr + li * pi_, axis=0, keepdims=True)
        da_ref[0:1, ns:ns2] += jnp.sum(li * pr - lr * pi_, axis=0, keepdims=True)

    rev = lambda i: nt - 1 - i
    return _call(
        body, side, [y_pre, dy_direct, dya0, z, states, states, wct, wbt, cst_rev, d_row], name=name, grid=(sd.nb, nt),
        in_specs=[pl.BlockSpec((tt, cb), lambda j, i: (rev(i), j)),
                  pl.BlockSpec((tt, cb), lambda j, i: (rev(i), j)),
                  pl.BlockSpec((tt, cb), lambda j, i: (rev(i), j)),
                  pl.BlockSpec((tt, cb), lambda j, i: (rev(i), j)),
                  pl.BlockSpec((tt, ns2), lambda j, i: (rev(i), j)),
                  pl.BlockSpec((SUBLANES, ns2), lambda j, i: (jnp.maximum(rev(i) * blocks_per_tile - 1, 0), j)),
                  pl.BlockSpec((None, cb, ns2), lambda j, i: (j, 0, 0)),
                  pl.BlockSpec((None, ns2, cb), lambda j, i: (j, 0, 0)),
                  pl.BlockSpec((None, 8 * SUBLANES, ns), lambda j, i: (j, 0, 0)),
                  pl.BlockSpec((1, cb), lambda j, i: (0, j))],
        out_specs=[pl.BlockSpec((tt, cb), lambda j, i: (rev(i), j)),
                   pl.BlockSpec((None, cb, ns2), lambda j, i: (j, 0, 0)),
                   pl.BlockSpec((None, cb, ns2), lambda j, i: (j, 0, 0)),
                   pl.BlockSpec((None, SUBLANES, ns2), lambda j, i: (j, 0, 0)),
                   pl.BlockSpec((None, SUBLANES, cb), lambda j, i: (j, 0, 0))],
        out_shape=[_sds((n_tok, sd.d), BF16), _sds((sd.nb, cb, ns2), F32), _sds((sd.nb, cb, ns2), F32),
                   _sds((sd.nb, SUBLANES, ns2), F32), _sds((sd.nb, SUBLANES, cb), F32)],
        scratch_shapes=[pltpu.VMEM((tt, ns2), F32), pltpu.VMEM((SUBLANES, ns2), F32)],
        compiler_params=_params(("arbitrary", "arbitrary")))


def _hosted(exch, fn, name, *args, **kw):
    side = exch.side(name)
    if side is None:
        return fn(name, *args, **kw)
    out, moved = fn(name, *args, side=side, **kw)
    exch.done(name, moved)
    return out


def _local_grads(x, p, target, sp, exch):
    n_tok, d_model = x.shape
    d_ssm = sp["ssm_d"].shape[0] * sp["ssm_d"].shape[1]
    d_sgu = sp["sgu_ln_g"].shape[-1]
    sd = _SsmDims(sp["ssm_b_re"].shape[0], sp["ssm_b_re"].shape[1], sp["ssm_b_re"].shape[2])
    heads, chunk, _ = sp["sgu_w"].shape
    row = lambda v: v.reshape(1, -1)
    tok = lambda w, dt=F32: _sds((n_tok, w), dt)
    acc = lambda w: _sds((1, w), F32)

    g_mix = row(sp["norm_mix_g"])
    (h1,) = _hosted(exch, _rowwise, "norm_mix", lambda a, g: _rms(a, g), [x], [g_mix], [tok(d_model, BF16)])
    z = _hosted(exch, _mm_nn, "proj_in", h1, exch.weight("w_in"), sharded=True, tn=768)

    ssm_rows = _ssm_rows(sd, sp)
    wb, wbt, wc, wct, cst_fwd, cst_rev = _ssm_operands(sd, ssm_rows)
    d_row = row(sp["ssm_d"])
    y_pre, states, ya0_16 = _hosted(exch, _ssm_fwd, "ssm_fwd", sd, z, wb, wc, cst_fwd, d_row)
    q = _mm_nn("ssm_glu", ya0_16, exch.weight("ssm_glu_w"), tm=1024)
    glu_b, g_ossm = row(sp["ssm_glu_b"]), row(sp["out_norm_ssm_g"])
    (ya_n,) = _rowwise("ssm_glu_out", _glu_out, [y_pre, q], [glu_b, g_ossm], [tok(d_ssm, BF16)])

    assert d_ssm == d_sgu
    zu, zv = _Cols(z, d_sgu, 1), _Cols(z, d_sgu, 2)
    ln_g, ln_b, g_osgu = row(sp["sgu_ln_g"]), row(sp["sgu_ln_b"]), row(sp["out_norm_sgu_g"])
    b_st = sp["sgu_b"].T
    sgu_tr = 2 * chunk

    def sgu_joined(ya_t, zu_t, zv_t, *params):
        return jnp.concatenate([ya_t, _sgu_rows(zu_t, zv_t, *params).astype(BF16)], axis=1)

    (ycat,) = _rowwise("sgu", sgu_joined, [ya_n, zu, zv], [ln_g, ln_b, sp["sgu_w"], b_st, g_osgu],
                       [tok(d_ssm + d_sgu, BF16)], tr=sgu_tr)
    x1 = _hosted(exch, _mm_nn, "proj_out", ycat, exch.weight("w_out"), res=x, tm=1024, tn=1024)

    g_ffn = row(sp["norm_ffn_g"])
    (h2,) = _rowwise("norm_ffn", lambda a, g: _rms(a, g), [x1], [g_ffn], [tok(d_model, BF16)])
    act, gu16 = _hosted(exch, _ffn_in_swiglu, "ffn_in", h2, exch.weight("w_ffn_in"))
    x2 = _mm_nn("ffn_out", act, exch.weight("w_ffn_out"), res=x1)

    g_ple = row(sp["norm_ple_g"])
    (h3,) = _rowwise("norm_ple", lambda a, g: _rms(a, g), [x2], [g_ple], [tok(d_model, BF16)])
    gpre = _mm_nn("ple_gate", h3, exch.weight("w_ple_gate"), tm=1024, tn=1024)
    (p16,) = _rowwise("ple_cast", lambda a: a, [p], [], [tok(p.shape[1], BF16)])
    pp = _mm_nn("ple_proj", p16, exch.weight("w_ple_proj"), sharded=True, tm=1024)

    b_g, g_fin = row(sp["b_ple_gate"]), row(sp["final_norm_g"])

    def head(x2_t, gpre_t, pp_t, tgt_t, b_g_v, g_fin_v):
        loss, grads = jax.value_and_grad(_head_loss, argnums=(0, 1, 2, 3, 4))(x2_t, gpre_t, pp_t, b_g_v, g_fin_v, tgt_t)
        dx2, dgpre, dpp, db, dg = grads
        return dx2, dgpre.astype(BF16), dpp.astype(BF16), jnp.full((1, LANES), loss, F32), db, dg

    dx2_head, dgpre16, dpp16, loss_row, d_b_g, d_g_fin = _rowwise(
        "head", head, [x2, gpre, pp, target], [b_g, g_fin],
        [tok(d_model), tok(d_model, BF16), tok(d_model, BF16)], [acc(LANES), acc(d_model), acc(d_model)])
    loss = loss_row[0, 0]
    exch.small_grads({"loss": loss_row})

    exch.grad("w_ple_proj", _mm_tn("d_ple_proj", p16, dpp16, shards=N_CHIPS, tk=256))
    exch.grad("w_ple_gate", _mm_tn("d_ple_gate", h3, dgpre16, tn=1024))
    dh3 = _mm_nt("d_h3", dgpre16, exch.weight("w_ple_gate"), out_dtype=BF16, tm=1024, tk=1024)

    def norm_bwd(x_t, dres_t, dh_t, g_v):
        _, vjp = jax.vjp(_rms, x_t, g_v)
        dx, dg = vjp(dh_t.astype(F32))
        dx = dres_t + dx
        return dx, dx.astype(BF16), dg

    dx2, dx2_16, d_g_ple = _rowwise("d_norm_ple", norm_bwd, [x2, dx2_head, dh3], [g_ple],
                                    [tok(d_model), tok(d_model, BF16)], [acc(d_model)])
    exch.grad("w_ffn_out", _mm_tn("d_ffn_out", act, dx2_16))
    dgu16 = _hosted(exch, _d_act_swiglu, "d_act", dx2_16, exch.weight("w_ffn_out"), gu16)
    exch.grad("w_ffn_in", _hosted(exch, _mm_tn, "d_ffn_in", h2, dgu16, shards=N_CHIPS, g_halves=True, tn=1408, g_resident=True))
    dh2 = _hosted(exch, _mm_nt, "d_h2", dgu16, exch.weight("w_ffn_in"), sharded=True, g_halves=True, out_dtype=BF16, tm=256, w_resident=True)
    dx1, dx1_16, d_g_ffn = _rowwise("d_norm_ffn", norm_bwd, [x1, dx2, dh2], [g_ffn],
                                    [tok(d_model), tok(d_model, BF16)], [acc(d_model)])
    exch.grad("w_out", _mm_tn("d_proj_out", ycat, dx1_16, tn=1024))
    dycat = _mm_nt("d_ycat", dx1_16, exch.weight("w_out"), out_dtype=BF16, tm=1024, tk=1024)

    def glu_out_bwd(y_pre_t, q_t, dy_t, glu_b_v, g_v):
        _, vjp = jax.vjp(_glu_out, y_pre_t, q_t, glu_b_v, g_v)
        dy_pre, dq, db, dg = vjp(dy_t.astype(F32))
        return dy_pre, dq.astype(BF16), db, dg

    dy_pre_a, dq16, d_glu_b, d_g_ossm = _rowwise(
        "d_ssm_glu_out", glu_out_bwd, [y_pre, q, _Cols(dycat, d_ssm, 0)], [glu_b, g_ossm],
        [tok(d_ssm), tok(d_ssm, BF16)], [acc(d_ssm), acc(d_ssm)])
    exch.grad("ssm_glu_w", _mm_tn("d_ssm_glu", ya0_16, dq16))
    dya0 = _hosted(exch, _mm_nt, "d_ya0", dq16, exch.weight("ssm_glu_w"), out_dtype=BF16, tm=1024)

    dz_ssm16, dwb, dwc, da, dd = _hosted(exch, _ssm_bwd, "ssm_bwd", sd, y_pre, dy_pre_a, dya0, z, states, wct, wbt,
                                         cst_rev, d_row)

    def sgu_bwd(dz_ssm_t, zu_t, zv_t, dy_t, ln_g_v, ln_b_v, w_v, b_v, g_v):
        _, vjp = jax.vjp(_sgu_rows, zu_t, zv_t, ln_g_v, ln_b_v, w_v, b_v, g_v)
        dzu, dzv, dlg, dlb, dw, db, dg = vjp(dy_t.astype(F32))
        return jnp.concatenate([dz_ssm_t, dzu.astype(BF16), dzv.astype(BF16)], axis=1), dlg, dlb, dw, db, dg

    dz16, d_ln_g, d_ln_b, d_sgu_w, d_b_st, d_g_osgu = _hosted(
        exch, _rowwise, "d_sgu", sgu_bwd, [dz_ssm16, zu, zv, _Cols(dycat, d_sgu, 1)], [ln_g, ln_b, sp["sgu_w"], b_st, g_osgu],
        [tok(d_ssm + 2 * d_sgu, BF16)],
        [acc(d_sgu), acc(d_sgu), _sds(sp["sgu_w"].shape, F32), _sds(b_st.shape, F32), acc(d_sgu)], tr=sgu_tr)

    d_lam_re, d_lam_im, d_log_step, d_bt_re, d_bt_im, d_ct_re, d_ct_im = _ssm_param_grads(sd, ssm_rows, dwb, dwc, da)
    d_b_re, d_b_im = d_bt_re.T, d_bt_im.T
    d_c_re, d_c_im = (t.reshape(sd.h, sd.g, sd.p).transpose(1, 0, 2) for t in (d_ct_re, d_ct_im))
    d_ssm_d = dd[:, 0, :].reshape(sd.g, sd.h)

    exch.small_grads({
        "ssm_lambda_re": d_lam_re, "ssm_lambda_im": d_lam_im, "ssm_log_step": d_log_step,
        "ssm_b_re": d_b_re, "ssm_b_im": d_b_im, "ssm_c_re": d_c_re, "ssm_c_im": d_c_im, "ssm_d": d_ssm_d,
        "ssm_glu_b": d_glu_b, "sgu_ln_g": d_ln_g, "sgu_ln_b": d_ln_b, "sgu_w": d_sgu_w, "sgu_b": d_b_st.T,
        "out_norm_ssm_g": d_g_ossm, "out_norm_sgu_g": d_g_osgu, "norm_ffn_g": d_g_ffn, "norm_ple_g": d_g_ple,
        "b_ple_gate": d_b_g, "final_norm_g": d_g_fin,
    })

    exch.grad("w_in", _hosted(exch, _mm_tn, "d_proj_in", h1, dz16, shards=N_CHIPS, tn=768))
    dh1 = _hosted(exch, _mm_nt, "d_h1", dz16, exch.weight("w_in"), sharded=True, out_dtype=BF16, tm=1024, tk=1024)

    def norm_in_bwd(x_t, dres_t, dh_t, g_v):
        _, vjp = jax.vjp(_rms, x_t, g_v)
        dx, dg = vjp(dh_t.astype(F32))
        return dres_t + dx, dg

    grad_x, d_g_mix = _hosted(exch, _rowwise, "d_norm_mix", norm_in_bwd, [x, dx1, dh1], [g_mix], [tok(d_model)], [acc(d_model)])
    exch.small_grads({"norm_mix_g": d_g_mix})
    return loss, grad_x


def _place():
    x, y, c = lax.axis_index("x"), lax.axis_index("y"), lax.axis_index("c")
    chips = [(1 - x, y), (x, 1 - y), (1 - x, 1 - y)]
    return x, y, c, chips


def _cast_into_slot(name, w2d, shard, tr=256):
    rows, cols = w2d.shape
    rh = rows // 2
    tr = _pick(rh, tr, 16)
    per = rh // tr

    def body(s_ref, a_ref, o_ref):
        o_ref[...] = a_ref[...].astype(BF16)

    grid_spec = pltpu.PrefetchScalarGridSpec(
        num_scalar_prefetch=1, grid=(2, per),
        in_specs=[pl.BlockSpec((tr, cols), lambda h, i, s_ref: (h * per + i, 0))],
        out_specs=pl.BlockSpec((None, None, tr, cols), lambda h, i, s_ref: (s_ref[0], h, i, 0)))
    return pl.pallas_call(body, name=name, grid_spec=grid_spec, out_shape=_sds((N_CHIPS, 2, rh, cols), BF16),
                          compiler_params=_params(("arbitrary", "arbitrary")))(shard.reshape(1).astype(jnp.int32), w2d)


def _exchange_alone(name, side):
    n_in, n_out = len(side.ins), len(side.out_shapes)

    def body(*refs):
        ins, outs, sems = refs[:n_in], refs[n_in:n_in + n_out], refs[n_in + n_out:]
        side.first(ins, outs, *sems)
        if side.mid is not None:
            side.mid(ins, outs, *sems)
        side.last(ins, outs, *sems)

    return pl.pallas_call(
        body, name=name, in_specs=[ANY] * n_in, out_specs=[ANY] * n_out, out_shape=side.out_shapes,
        input_output_aliases=side.aliases,
        scratch_shapes=[pltpu.SemaphoreType.DMA((side.n_sems,)), pltpu.SemaphoreType.DMA((side.n_sems,))],
    )(*side.ins)


def _gather_side(slots, parts=None, mid_late=False):
    n = len(slots)
    parts = parts or [(0, GATHER_PARTS)] * n

    def copies(kind, outs, send_sems, recv_sems):
        x, y, c, chips = _place()

        def remote(k, w, shard, half, to):
            unit = outs[w].shape[2] // GATHER_PARTS
            lo, hi = parts[w]
            ref = outs[w].at[shard, half, pl.ds(lo * unit, (hi - lo) * unit), :]
            return pltpu.make_async_remote_copy(src_ref=ref, dst_ref=ref, send_sem=send_sems.at[k], recv_sem=recv_sems.at[k],
                                                device_id=to, device_id_type=MESH)

        pairs = [(w, j, 2 * cx + cy, (cx, cy)) for w in range(n) for j, (cx, cy) in enumerate(chips)]
        if kind == "sends":
            return [remote(3 * w + j, w, 2 * x + y, c, (*chip, c)) for w, j, _, chip in pairs]
        if kind == "arrivals":
            return [remote(3 * w + j, w, s, c, (x, y, c)) for w, j, s, _ in pairs]
        if kind == "passed":
            return [remote(3 * n + 3 * w + j, w, s, c, (x, y, 1 - c)) for w, j, s, _ in pairs]
        return [remote(3 * n + 3 * w + j, w, s, 1 - c, (x, y, c)) for w, j, s, _ in pairs]

    def first(ins, outs, *sems):
        for cp in copies("sends", outs, *sems):
            cp.start()

    def mid(ins, outs, *sems):
        for arrived, onward in zip(copies("arrivals", outs, *sems), copies("passed", outs, *sems)):
            arrived.wait_recv()
            onward.start()

    def last(ins, outs, *sems):
        for cp in copies("from_sibling", outs, *sems):
            cp.wait_recv()
        for cp in copies("sends", outs, *sems) + copies("passed", outs, *sems):
            cp.wait_send()

    return _Side(slots, [_sds(s.shape, s.dtype) for s in slots], 6 * n, first, last, mid=mid, aliases={w: w for w in range(n)},
                 mid_late=mid_late)


def _swap_side(grads):
    n = len(grads)

    def copies(ins, outs, send_sems, recv_sems):
        x, y, c, _ = _place()
        return [pltpu.make_async_remote_copy(src_ref=ins[w].at[:, 1 - c], dst_ref=outs[w], send_sem=send_sems.at[w],
                                             recv_sem=recv_sems.at[w], device_id=(x, y, 1 - c), device_id_type=MESH)
                for w in range(n)]

    def first(*refs):
        for cp in copies(*refs):
            cp.start()

    def last(*refs):
        for cp in copies(*refs):
            cp.wait()

    return _Side(grads, [_sds((g.shape[0], *g.shape[2:]), g.dtype) for g in grads], n, first, last)


def _scatter_side(halves):
    n = len(halves)

    def copies(ins, outs, send_sems, recv_sems):
        x, y, c, chips = _place()
        return [pltpu.make_async_remote_copy(
            src_ref=ins[w].at[2 * cx + cy], dst_ref=outs[w].at[j], send_sem=send_sems.at[3 * w + j],
            recv_sem=recv_sems.at[3 * w + j], device_id=(cx, cy, c), device_id_type=MESH)
            for w in range(n) for j, (cx, cy) in enumerate(chips)]

    def first(*refs):
        for cp in copies(*refs):
            cp.start()

    def last(*refs):
        for cp in copies(*refs):
            cp.wait()

    return _Side(halves, [_sds((3, *h.shape[1:]), h.dtype) for h in halves], 3 * n, first, last)


def _join_side(slots):
    n = len(slots)

    def copy(outs, send_sems, recv_sems, w, half, to):
        return pltpu.make_async_remote_copy(src_ref=outs[w].at[half], dst_ref=outs[w].at[half], send_sem=send_sems.at[w],
                                            recv_sem=recv_sems.at[w], device_id=to, device_id_type=MESH)

    def first(ins, outs, *sems):
        x, y, c, _ = _place()
        for w in range(n):
            copy(outs, *sems, w, c, (x, y, 1 - c)).start()

    def last(ins, outs, *sems):
        x, y, c, _ = _place()
        for w in range(n):
            copy(outs, *sems, w, 1 - c, (x, y, c)).wait_recv()
        for w in range(n):
            copy(outs, *sems, w, c, (x, y, 1 - c)).wait_send()

    return _Side(slots, [_sds(s.shape, s.dtype) for s in slots], n, first, last, aliases={w: w for w in range(n)})


def _allreduce_small(block, tr=256):
    rows, lanes = block.shape
    tr = _pick(rows, tr, SUBLANES)

    def body(x_ref, o_ref, buf, send_sems, recv_sems):
        x, y, c, chips = _place()
        me, sibling = (x, y, c), (x, y, 1 - c)

        def slot(px, py, pc):
            return buf.at[4 * px + 2 * py + pc]

        def copy(k, block_of, to):
            return pltpu.make_async_remote_copy(src_ref=slot(*block_of), dst_ref=slot(*block_of), send_sem=send_sems.at[k],
                                                recv_sem=recv_sems.at[k], device_id=to, device_id_type=MESH)

        slot(*me)[...] = x_ref[...]
        first = [copy(0, me, sibling)] + [copy(1 + j, me, (*chip, c)) for j, chip in enumerate(chips)]
        for cp in first:
            cp.start()
        passed = [copy(4 + j, (*chip, c), sibling) for j, chip in enumerate(chips)]
        for j, chip in enumerate(chips):
            copy(1 + j, (*chip, c), me).wait_recv()
            passed[j].start()
        copy(0, sibling, me).wait_recv()
        for j, chip in enumerate(chips):
            copy(4 + j, (*chip, 1 - c), me).wait_recv()
        for cp in first + passed:
            cp.wait_send()
        for r0 in range(0, rows, tr):
            acc = buf[0, r0:r0 + tr, :]
            for k in range(1, N_DEV):
                acc = acc + buf[k, r0:r0 + tr, :]
            o_ref[r0:r0 + tr, :] = acc

    vm = pl.BlockSpec(memory_space=pltpu.VMEM)
    return pl.pallas_call(
        body, name="allreduce_small", in_specs=[vm], out_specs=vm, out_shape=_sds((rows, lanes), block.dtype),
        scratch_shapes=[pltpu.VMEM((N_DEV, rows, lanes), block.dtype), pltpu.SemaphoreType.DMA((7,)), pltpu.SemaphoreType.DMA((7,))],
        compiler_params=pltpu.CompilerParams(vmem_limit_bytes=VMEM_LIMIT),
    )(block)


def _small_gather_side(block):
    def copy(kind, j, ins, outs, send_sems, recv_sems):
        x, y, c, chips = _place()
        chip = chips[j] if j is not None else None
        slot = lambda px, py, pc: outs[0].at[4 * px + 2 * py + pc]

        def remote(k, src, dst, to):
            return pltpu.make_async_remote_copy(src_ref=src, dst_ref=dst, send_sem=send_sems.at[k], recv_sem=recv_sems.at[k],
                                                device_id=to, device_id_type=MESH)

        if kind == "to_sibling":
            return remote(0, ins[0], slot(x, y, c), (x, y, 1 - c))
        if kind == "from_sibling":
            return remote(0, ins[0], slot(x, y, 1 - c), (x, y, c))
        if kind == "to_chip":
            return remote(1 + j, ins[0], slot(x, y, c), (*chip, c))
        if kind == "from_chip":
            return remote(1 + j, ins[0], slot(*chip, c), (x, y, c))
        if kind == "pass_on":
            return remote(4 + j, slot(*chip, c), slot(*chip, c), (x, y, 1 - c))
        return remote(4 + j, slot(*chip, 1 - c), slot(*chip, 1 - c), (x, y, c))

    def first(*refs):
        copy("to_sibling", None, *refs).start()
        for j in range(3):
            copy("to_chip", j, *refs).start()

    def mid(*refs):
        for j in range(3):
            copy("from_chip", j, *refs).wait_recv()
            copy("pass_on", j, *refs).start()

    def last(*refs):
        copy("from_sibling", None, *refs).wait_recv()
        for j in range(3):
            copy("passed_on", j, *refs).wait_recv()
        copy("to_sibling", None, *refs).wait_send()
        for j in range(3):
            copy("to_chip", j, *refs).wait_send()
            copy("pass_on", j, *refs).wait_send()

    return _Side([block], [_sds((N_DEV, *block.shape), block.dtype)], 7, first, last, mid=mid, mid_late=True)


def _sum_slots(name, own, gathered, me, tr=512):
    n, rows, cols = gathered.shape
    tr = _pick(rows, tr, SUBLANES)
    if tr < 64:
        tr = rows

    def body(me_ref, own_ref, g_ref, o_ref):
        mine = own_ref[...]
        acc = jnp.where(me_ref[0] == 0, mine, g_ref[0])
        for k in range(1, n):
            acc = acc + jnp.where(me_ref[0] == k, mine, g_ref[k])
        o_ref[...] = acc

    grid_spec = pltpu.PrefetchScalarGridSpec(
        num_scalar_prefetch=1, grid=(rows // tr,),
        in_specs=[pl.BlockSpec((tr, cols), lambda i, me_ref: (i, 0)), pl.BlockSpec((n, tr, cols), lambda i, me_ref: (0, i, 0))],
        out_specs=pl.BlockSpec((tr, cols), lambda i, me_ref: (i, 0)))
    return pl.pallas_call(body, name=name, grid_spec=grid_spec, out_shape=_sds((rows, cols), own.dtype),
                          compiler_params=_params(("arbitrary",)))(me.reshape(1).astype(jnp.int32), own, gathered)


def _sum_received(name, full, c, shard, swapped, received, tr=256):
    n, rows, cols = received.shape
    tr = _pick(rows, tr, 16)

    def body(i_ref, a_ref, b_ref, s_ref, o_ref):
        acc = a_ref[...] + b_ref[...]
        for k in range(n):
            acc = acc + s_ref[k].astype(F32)
        o_ref[...] = acc

    grid_spec = pltpu.PrefetchScalarGridSpec(
        num_scalar_prefetch=1, grid=(rows // tr,),
        in_specs=[pl.BlockSpec((None, None, tr, cols), lambda i, i_ref: (i_ref[1], i_ref[0], i, 0)),
                  pl.BlockSpec((None, tr, cols), lambda i, i_ref: (i_ref[1], i, 0)),
                  pl.BlockSpec((n, tr, cols), lambda i, i_ref: (0, i, 0))],
        out_specs=pl.BlockSpec((None, tr, cols), lambda i, i_ref: (i_ref[0], i, 0)))
    return pl.pallas_call(body, name=name, grid_spec=grid_spec, out_shape=_sds((2, rows, cols), F32),
                          compiler_params=_params(("arbitrary",)))(jnp.stack([c, shard]).astype(jnp.int32), full, swapped, received)


def _add_halves(name, full, c, shard, received, tr=256):
    s, _, rh, cols = full.shape
    tr = _pick(rh, tr, 16)

    def body(i_ref, a_ref, b_ref, o_ref):
        o_ref[...] = (a_ref[...] + b_ref[...]).astype(BF16)

    other = lambda q, i_ref: (i_ref[1] + 1 + q) % s
    grid_spec = pltpu.PrefetchScalarGridSpec(
        num_scalar_prefetch=1, grid=(s - 1, rh // tr),
        in_specs=[pl.BlockSpec((None, None, tr, cols), lambda q, i, i_ref: (other(q, i_ref), i_ref[0], i, 0)),
                  pl.BlockSpec((None, tr, cols), lambda q, i, i_ref: (other(q, i_ref), i, 0))],
        out_specs=pl.BlockSpec((None, tr, cols), lambda q, i, i_ref: (other(q, i_ref), i, 0)))
    return pl.pallas_call(body, name=name, grid_spec=grid_spec, out_shape=_sds((s, rh, cols), BF16),
                          compiler_params=_params(("arbitrary", "arbitrary")))(jnp.stack([c, shard]).astype(jnp.int32), full, received)


LARGE = ("w_in", "ssm_glu_w", "w_out", "w_ffn_in", "w_ffn_out", "w_ple_gate", "w_ple_proj")
COLUMN_SHARDED = ("w_in", "w_ffn_in", "w_ple_proj")
SMALL = ("norm_mix_g", "ssm_lambda_re", "ssm_lambda_im", "ssm_log_step", "ssm_b_re", "ssm_b_im", "ssm_c_re", "ssm_c_im",
         "ssm_d", "ssm_glu_b", "sgu_ln_g", "sgu_ln_b", "sgu_w", "sgu_b", "out_norm_ssm_g", "out_norm_sgu_g", "norm_ffn_g",
         "norm_ple_g", "b_ple_gate", "final_norm_g")
WEIGHTS = ("norm_mix_g", "w_in", "ssm_lambda_re", "ssm_lambda_im", "ssm_log_step", "ssm_b_re", "ssm_b_im", "ssm_c_re",
           "ssm_c_im", "ssm_d", "ssm_glu_w", "ssm_glu_b", "sgu_ln_g", "sgu_ln_b", "sgu_w", "sgu_b", "out_norm_ssm_g",
           "out_norm_sgu_g", "w_out", "norm_ffn_g", "w_ffn_in", "w_ffn_out", "norm_ple_g", "w_ple_gate", "b_ple_gate",
           "w_ple_proj", "final_norm_g")
PACK_ROWS = SUBLANES * LANES


def _pack(arrays):
    parts = []
    for a in arrays:
        flat = a.reshape(-1).astype(F32)
        pad = -flat.shape[0] % PACK_ROWS
        parts.append(jnp.pad(flat, (0, pad)) if pad else flat)
    return jnp.concatenate(parts).reshape(-1, LANES)


def _unpack(packed, like):
    flat = packed.reshape(-1)
    out, at = [], 0
    for a in like:
        size = a.size
        out.append(flat[at:at + size].reshape(a.shape))
        at += size + (-size % PACK_ROWS)
    return out


class _NoExchange:
    def __init__(self, weights):
        self.weights, self.grads, self.small = weights, {}, {}

    def weight(self, name):
        return self.weights[name]

    def grad(self, name, g):
        self.grads[name] = g

    def small_grads(self, grads):
        self.small.update(grads)

    def side(self, host):
        return None


class _MeshExchange:
    GATHER = {"norm_mix": (("w_in", 0, 16),),
              "proj_in": (("ssm_glu_w", 0, 16), ("w_out", 0, 16), ("w_ffn_in", 0, 1)),
              "ssm_fwd": (("w_ffn_in", 1, 13),),
              "proj_out": (("w_ffn_in", 13, 16),),
              "ffn_in": (("w_ffn_out", 0, 16), ("w_ple_gate", 0, 16), ("w_ple_proj", 0, 16))}
    GATHER_LONG = ("norm_mix", "proj_in", "ssm_fwd", "proj_out")
    SWAP = {"d_act": ("w_ple_proj", "w_ple_gate", "w_ffn_out"), "d_h2": ("w_ffn_in",), "d_ya0": ("w_out", "ssm_glu_w")}
    SWAP_ALONE = ("w_in",)
    SCATTER = {"d_ffn_in": ("w_ple_proj", "w_ple_gate", "w_ffn_out"), "ssm_bwd": ("w_ffn_in",),
               "d_sgu": ("w_out", "ssm_glu_w"), "d_h1": ("w_in",)}
    SMALL_GATHER = "d_proj_in"

    def __init__(self, shards, small_like, c, shard, me):
        self.c, self.shard, self.me, self.small_like = c, shard, me, small_like
        self.slots = {k: _cast_into_slot("cast_" + k, shards[k], shard) for k in LARGE}
        self.full, self.received, self.halves, self.quarters, self.small = {}, {}, {}, {}, {}

    def weight(self, name):
        g = self.slots[name]
        _, _, rh, cols = g.shape
        return g.reshape(N_CHIPS, 2 * rh, cols) if name in COLUMN_SHARDED else g.reshape(N_CHIPS * 2 * rh, cols)

    def grad(self, name, g):
        if name not in COLUMN_SHARDED:
            g = g.reshape(N_CHIPS, g.shape[0] // N_CHIPS, g.shape[1])
        self.full[name] = g.reshape(N_CHIPS, 2, g.shape[1] // 2, g.shape[2])
        if name in self.SWAP_ALONE:
            self._swapped((name,), _exchange_alone("grad_swap_" + name, _swap_side([self.full[name]])))

    def _swapped(self, names, received):
        for k, r in zip(names, received):
            self.received[k] = r
            self.halves[k] = _add_halves("grad_add_halves_" + k, self.full[k], self.c, self.shard, r)

    def small_grads(self, grads):
        self.small.update(grads)

    def _packed(self, names):
        return _pack([self.small[k].reshape(self.small_like[k].shape) for k in names])

    def side(self, host):
        if host in self.GATHER:
            return _gather_side([self.slots[k] for k, _, _ in self.GATHER[host]], [(lo, hi) for _, lo, hi in self.GATHER[host]],
                                mid_late=host in self.GATHER_LONG)
        if host in self.SWAP:
            return _swap_side([self.full[k] for k in self.SWAP[host]])
        if host in self.SCATTER:
            return _scatter_side([self.halves[k] for k in self.SCATTER[host]])
        if host == self.SMALL_GATHER:
            self.packed_early = self._packed(SMALL[1:] + ("loss",))
            return _small_gather_side(self.packed_early)
        return None

    def done(self, host, moved):
        if host in self.GATHER:
            self.slots.update(zip([k for k, _, _ in self.GATHER[host]], moved))
        elif host in self.SWAP:
            self._swapped(self.SWAP[host], moved)
        elif host in self.SCATTER:
            self.quarters.update(zip(self.SCATTER[host], moved))
        else:
            (self.gathered_early,) = moved

    def small_reduced(self):
        early = _sum_slots("small_sum", self.packed_early, self.gathered_early, self.me)
        late = _allreduce_small(self._packed(SMALL[:1]))
        loss_at = early.shape[0] - PACK_ROWS // LANES
        return jnp.concatenate([late, early[:loss_at]], axis=0), early[loss_at, 0]

    def summed(self):
        return [_sum_received("grad_sum_" + k, self.full[k], self.c, self.shard, self.received[k], self.quarters[k]) for k in LARGE]


def kernel(x, p, norm_mix_g, w_in, ssm_lambda_re, ssm_lambda_im, ssm_log_step, ssm_b_re, ssm_b_im, ssm_c_re, ssm_c_im, ssm_d, ssm_glu_w, ssm_glu_b, sgu_ln_g, sgu_ln_b, sgu_w, sgu_b, out_norm_ssm_g, out_norm_sgu_g, w_out, norm_ffn_g, w_ffn_in, w_ffn_out, norm_ple_g, w_ple_gate, b_ple_gate, w_ple_proj, final_norm_g, loss_target, m_norm_mix_g, m_w_in, m_ssm_lambda_re, m_ssm_lambda_im, m_ssm_log_step, m_ssm_b_re, m_ssm_b_im, m_ssm_c_re, m_ssm_c_im, m_ssm_d, m_ssm_glu_w, m_ssm_glu_b, m_sgu_ln_g, m_sgu_ln_b, m_sgu_w, m_sgu_b, m_out_norm_ssm_g, m_out_norm_sgu_g, m_w_out, m_norm_ffn_g, m_w_ffn_in, m_w_ffn_out, m_norm_ple_g, m_w_ple_gate, m_b_ple_gate, m_w_ple_proj, m_final_norm_g, v_norm_mix_g, v_w_in, v_ssm_lambda_re, v_ssm_lambda_im, v_ssm_log_step, v_ssm_b_re, v_ssm_b_im, v_ssm_c_re, v_ssm_c_im, v_ssm_d, v_ssm_glu_w, v_ssm_glu_b, v_sgu_ln_g, v_sgu_ln_b, v_sgu_w, v_sgu_b, v_out_norm_ssm_g, v_out_norm_sgu_g, v_w_out, v_norm_ffn_g, v_w_ffn_in, v_w_ffn_out, v_norm_ple_g, v_w_ple_gate, v_b_ple_gate, v_w_ple_proj, v_final_norm_g):
    given = dict(locals())
    w = {k: given[k] for k in WEIGHTS}
    m = {k: given["m_" + k] for k in WEIGHTS}
    v = {k: given["v_" + k] for k in WEIGHTS}
    c = lax.axis_index("c")
    shard = 2 * lax.axis_index("x") + lax.axis_index("y")

    small_like = {k: w[k] for k in SMALL}
    small_like["loss"] = _sds((1, LANES), F32)
    exch = _MeshExchange({k: w[k].reshape(w[k].shape[1:]) for k in LARGE}, small_like, c, shard, 2 * shard + c)
    unlayer = lambda a: a if a.ndim == 1 else a[0]
    sp = {k: unlayer(w[k]) for k in SMALL}
    n_tok, d_model = x.shape[1:]
    _, grad_x = _local_grads(x.reshape(n_tok, d_model), p.reshape(n_tok, p.shape[-1]),
                             loss_target.reshape(n_tok, d_model), sp, exch)

    grad_w, delta_w, new_m, new_v = {}, {}, {}, {}
    halves = exch.summed()
    packed_g, loss = exch.small_reduced()
    like = _sds(packed_g.shape, F32)
    (d_s, m_s, v_s), joined = _rowwise(
        "adamw_small", _adamw, [_pack([w[k] for k in SMALL]), packed_g, _pack([m[k] for k in SMALL]), _pack([v[k] for k in SMALL])],
        [], [like, like, like], side=_join_side(halves))
    shapes = [w[k] for k in SMALL]
    for k, g_k, d_k, m_k, v_k in zip(SMALL, _unpack(packed_g, shapes), _unpack(d_s, shapes), _unpack(m_s, shapes), _unpack(v_s, shapes)):
        grad_w[k], delta_w[k], new_m[k], new_v[k] = g_k, d_k, m_k, v_k

    reduced = {k: j.reshape(2 * j.shape[1], j.shape[2]) for k, j in zip(LARGE, joined)}
    for k in LARGE:
        shape = w[k].shape
        two_d = lambda a: a.reshape(shape[1:])
        like = _sds(shape[1:], F32)
        update = lambda w_t, g_t, m_t, v_t: (g_t, *_adamw(w_t, g_t, m_t, v_t))
        outs = _rowwise("adamw_" + k, update, [two_d(w[k]), reduced[k], two_d(m[k]), two_d(v[k])], [], [like, like, like, like])
        grad_w[k], delta_w[k], new_m[k], new_v[k] = (a.reshape(shape) for a in outs)

    return (loss, grad_x.reshape(x.shape), *[grad_w[k] for k in WEIGHTS], *[delta_w[k] for k in WEIGHTS],
            *[new_m[k] for k in WEIGHTS], *[new_v[k] for k in WEIGHTS])
```

```python
import functools

import jax
import jax.numpy as jnp
from jax import lax
from jax.experimental import pallas as pl
from jax.experimental.pallas import tpu as pltpu

F32 = jnp.float32
BF16 = jnp.bfloat16

EPS = 1e-6
LAMBDA_RE_MAX = -1e-4
ADAM_LR = 0.001
ADAM_B1 = 0.9
ADAM_B2 = 0.999
ADAM_EPS = 1e-08
ADAM_WD = 0.01
ADAM_STEP = 10

N_CHIPS = 4
N_DEV = 8
SUBLANES = 8
LANES = 128
SSM_CH_BLOCK = 256
SCAN_LANES = 256
SCAN_BLOCKS = 4
GATHER_PARTS = 16
VMEM_LIMIT = 56 * 1024 * 1024

MESH = pl.DeviceIdType.MESH


def _pick(n, pref, mult):
    if n <= pref:
        return n
    t = (pref // mult) * mult
    while t >= mult:
        if n % t == 0:
            return t
        t -= mult
    return n


def _params(semantics):
    return pltpu.CompilerParams(dimension_semantics=semantics, vmem_limit_bytes=VMEM_LIMIT)


class _Cols:
    def __init__(self, arr, width, blk):
        self.arr, self.width, self.blk = arr, width, blk


def _sds(shape, dtype):
    return jax.ShapeDtypeStruct(tuple(shape), dtype)


ANY = pl.BlockSpec(memory_space=pl.ANY)


class _Side:
    def __init__(self, ins, out_shapes, n_sems, first, last, mid=None, aliases=None, mid_late=False):
        self.ins, self.out_shapes, self.n_sems = list(ins), list(out_shapes), n_sems
        self.first, self.mid, self.last, self.mid_late = first, mid, last, mid_late
        self.aliases = dict(aliases or {})


def _call(body, side, operands, *, name, grid, in_specs, out_specs, out_shape, compiler_params, scratch_shapes=()):
    if side is None:
        return pl.pallas_call(body, name=name, grid=grid, in_specs=in_specs, out_specs=out_specs, out_shape=out_shape,
                              scratch_shapes=list(scratch_shapes), compiler_params=compiler_params)(*operands)
    single = not isinstance(out_specs, (list, tuple))
    out_specs = [out_specs] if single else list(out_specs)
    out_shape = [out_shape] if single else list(out_shape)
    n_in, n_out, n_scr = len(in_specs), len(out_specs), len(scratch_shapes)
    n_sin, n_sout = len(side.ins), len(side.out_shapes)
    steps = 1
    for g in grid:
        steps *= g

    def hosted(*refs):
        ins, s_ins = refs[:n_in], refs[n_in:n_in + n_sin]
        at = n_in + n_sin
        outs, s_outs = refs[at:at + n_out], refs[at + n_out:at + n_out + n_sout]
        scratch = refs[at + n_out + n_sout:at + n_out + n_sout + n_scr]
        sems = refs[-2:]
        step = pl.program_id(0)
        for d in range(1, len(grid)):
            step = step * grid[d] + pl.program_id(d)

        @pl.when(step == 0)
        def _():
            side.first(s_ins, s_outs, *sems)

        if side.mid is not None:
            @pl.when(step == (steps - 1 if side.mid_late else (3 * steps) // 4))
            def _():
                side.mid(s_ins, s_outs, *sems)

        body(*ins, *outs, *scratch)

        @pl.when(step == steps - 1)
        def _():
            side.last(s_ins, s_outs, *sems)

    res = pl.pallas_call(
        hosted, name=name, grid=grid, in_specs=[*in_specs, *[ANY] * n_sin], out_specs=[*out_specs, *[ANY] * n_sout],
        out_shape=[*out_shape, *side.out_shapes], input_output_aliases={n_in + i: n_out + o for i, o in side.aliases.items()},
        scratch_shapes=[*scratch_shapes, pltpu.SemaphoreType.DMA((side.n_sems,)), pltpu.SemaphoreType.DMA((side.n_sems,))],
        compiler_params=compiler_params)(*operands, *side.ins)
    return (res[0] if single else list(res[:n_out])), list(res[n_out:])


def _rowwise(name, fn, rows, params, row_outs, acc_outs=(), tr=256, side=None):
    rows = [r if isinstance(r, _Cols) else _Cols(r, r.shape[1], 0) for r in rows]
    m = rows[0].arr.shape[0]
    tr = _pick(m, tr, 16)
    n_in = len(rows) + len(params)
    n_ro = len(row_outs)

    def body(*refs):
        vals = fn(*[r[...] for r in refs[:n_in]])
        if not isinstance(vals, (tuple, list)):
            vals = (vals,)
        outs = refs[n_in:]
        for r, v in zip(outs[:n_ro], vals[:n_ro]):
            r[...] = v.astype(r.dtype)
        first = pl.program_id(0) == 0
        for r, v in zip(outs[n_ro:], vals[n_ro:]):
            @pl.when(first)
            def _():
                r[...] = jnp.zeros(r.shape, r.dtype)
            r[...] += v.astype(r.dtype).reshape(r.shape)

    in_specs = [pl.BlockSpec((tr, r.width), lambda i, b=r.blk: (i, b)) for r in rows]
    in_specs += [pl.BlockSpec(p.shape, lambda i, nd=p.ndim: (0,) * nd) for p in params]
    out_specs = [pl.BlockSpec((tr, o.shape[1]), lambda i: (i, 0)) for o in row_outs]
    out_specs += [pl.BlockSpec(o.shape, lambda i, nd=len(o.shape): (0,) * nd) for o in acc_outs]
    return _call(body, side, [*[r.arr for r in rows], *params], name=name, grid=(m // tr,), in_specs=in_specs,
                 out_specs=out_specs, out_shape=[*row_outs, *acc_outs], compiler_params=_params(("arbitrary",)))


def _grid_order(swap):
    if not swap:
        return (lambda grid: grid), (lambda f: f)
    return (lambda grid: grid[::-1]), (lambda f: (lambda j, i: f(i, j)))


def _mm_nn(name, a, w, *, sharded=False, res=None, out_dtype=F32, tm=512, tn=512, w_resident=False, side=None):
    m, k = a.shape
    tm = _pick(m, tm, 16)
    order, ix = _grid_order(w_resident)
    if sharded:
        s, _, ns = w.shape
        n = s * ns
        tn = _pick(ns, tn, LANES)
        per = ns // tn
        w_spec = pl.BlockSpec((None, k, tn), ix(lambda i, j: (j // per, 0, j % per)))
    else:
        n = w.shape[1]
        tn = _pick(n, tn, LANES)
        w_spec = pl.BlockSpec((k, tn), ix(lambda i, j: (0, j)))

    def body(a_ref, w_ref, *rest):
        acc = jnp.dot(a_ref[...], w_ref[...], preferred_element_type=F32)
        if res is not None:
            acc = acc + rest[0][...]
        rest[-1][...] = acc.astype(out_dtype)

    in_specs = [pl.BlockSpec((tm, k), ix(lambda i, j: (i, 0))), w_spec]
    ops = [a, w]
    if res is not None:
        in_specs.append(pl.BlockSpec((tm, tn), ix(lambda i, j: (i, j))))
        ops.append(res)
    return _call(body, side, ops, name=name, grid=order((m // tm, n // tn)), in_specs=in_specs,
                 out_specs=pl.BlockSpec((tm, tn), ix(lambda i, j: (i, j))), out_shape=_sds((m, n), out_dtype),
                 compiler_params=_params(("arbitrary", "arbitrary")))


def _mm_nt(name, g, w, *, sharded=False, g_halves=False, out_dtype=F32, tm=512, tk=512, w_resident=False, side=None):
    m, n = g.shape[-2:]
    tm = _pick(m, tm, 16)
    order, ix = _grid_order(w_resident)
    dims = (((1,), (1,)), ((), ()))
    g_spec = pl.BlockSpec((2, tm, n), ix(lambda i, j: (0, i, 0))) if g_halves else pl.BlockSpec((tm, n), ix(lambda i, j: (i, 0)))
    if sharded:
        s, k, ns = w.shape
        tk = _pick(k, tk, LANES)
        w_spec = pl.BlockSpec((s, tk, ns), ix(lambda i, j: (0, j, 0)))

        def columns(g_ref, q):
            if not g_halves:
                return g_ref[:, q * ns:(q + 1) * ns]
            half, at = divmod(q, s // 2)
            return g_ref[half, :, at * ns:(at + 1) * ns]

        def body(g_ref, w_ref, o_ref):
            acc = lax.dot_general(columns(g_ref, 0), w_ref[0], dims, preferred_element_type=F32)
            for q in range(1, s):
                acc = acc + lax.dot_general(columns(g_ref, q), w_ref[q], dims, preferred_element_type=F32)
            o_ref[...] = acc.astype(out_dtype)
    else:
        k = w.shape[0]
        tk = _pick(k, tk, LANES)
        w_spec = pl.BlockSpec((tk, n), ix(lambda i, j: (j, 0)))

        def body(g_ref, w_ref, o_ref):
            o_ref[...] = lax.dot_general(g_ref[...], w_ref[...], dims, preferred_element_type=F32).astype(out_dtype)

    return _call(body, side, [g, w], name=name, grid=order((m // tm, k // tk)), in_specs=[g_spec, w_spec],
                 out_specs=pl.BlockSpec((tm, tk), ix(lambda i, j: (i, j))), out_shape=_sds((m, k), out_dtype),
                 compiler_params=_params(("arbitrary", "arbitrary")))


def _mm_tn(name, a, g, *, shards=0, g_halves=False, tk=512, tn=512, g_resident=False, side=None):
    m, k = a.shape
    n = 2 * g.shape[2] if g_halves else g.shape[1]
    tk = _pick(k, tk, LANES)
    order, ix = _grid_order(g_resident)
    dims = (((0,), (0,)), ((), ()))
    if shards:
        ns = n // shards
        tn = _pick(ns, tn, LANES)
        per = ns // tn
        out_spec = pl.BlockSpec((None, tk, tn), ix(lambda i, j: (j // per, i, j % per)))
        out_shape = _sds((shards, k, ns), F32)
    else:
        tn = _pick(n, tn, LANES)
        out_spec = pl.BlockSpec((tk, tn), ix(lambda i, j: (i, j)))
        out_shape = _sds((k, n), F32)

    def body(a_ref, g_ref, o_ref):
        o_ref[...] = lax.dot_general(a_ref[...], g_ref[...], dims, preferred_element_type=F32)

    if g_halves:
        per_half = n // 2 // tn
        g_spec = pl.BlockSpec((None, m, tn), ix(lambda i, j: (j // per_half, 0, j % per_half)))
    else:
        g_spec = pl.BlockSpec((m, tn), ix(lambda i, j: (0, j)))
    return _call(body, side, [a, g], name=name, grid=order((k // tk, n // tn)),
                 in_specs=[pl.BlockSpec((m, tk), ix(lambda i, j: (0, i))), g_spec],
                 out_specs=out_spec, out_shape=out_shape, compiler_params=_params(("arbitrary", "arbitrary")))


def _ffn_in_swiglu(name, a, w, *, tm=512, tn=1408, side=None):
    m, k = a.shape
    s, _, ns = w.shape
    f = s * ns // 2
    tm = _pick(m, tm, 16)
    tn = _pick(ns, tn, LANES)
    per = ns // tn
    order, ix = _grid_order(True)

    def body(a_ref, wg_ref, wu_ref, act_ref, gu_ref):
        x = a_ref[...]
        gate = jnp.dot(x, wg_ref[...], preferred_element_type=F32)
        up = jnp.dot(x, wu_ref[...], preferred_element_type=F32)
        act_ref[...] = _swiglu(gate, up).astype(BF16)
        gu_ref[0] = gate.astype(BF16)
        gu_ref[1] = up.astype(BF16)

    return _call(body, side, [a, w, w], name=name, grid=order((m // tm, f // tn)),
                 in_specs=[pl.BlockSpec((tm, k), ix(lambda i, j: (i, 0))),
                           pl.BlockSpec((None, k, tn), ix(lambda i, j: (j // per, 0, j % per))),
                           pl.BlockSpec((None, k, tn), ix(lambda i, j: (s // 2 + j // per, 0, j % per)))],
                 out_specs=[pl.BlockSpec((tm, tn), ix(lambda i, j: (i, j))), pl.BlockSpec((2, tm, tn), ix(lambda i, j: (0, i, j)))],
                 out_shape=[_sds((m, f), BF16), _sds((2, m, f), BF16)], compiler_params=_params(("arbitrary", "arbitrary")))


def _d_act_swiglu(name, g, w, gu, *, tm=1024, tk=512, side=None):
    m, n = g.shape
    f = w.shape[0]
    tm = _pick(m, tm, 16)
    tk = _pick(f, tk, LANES)
    dims = (((1,), (1,)), ((), ()))

    def body(g_ref, w_ref, gu_ref, o_ref):
        dact = lax.dot_general(g_ref[...], w_ref[...], dims, preferred_element_type=F32)
        _, vjp = jax.vjp(_swiglu, gu_ref[0].astype(F32), gu_ref[1].astype(F32))
        dgate, dup = vjp(dact)
        o_ref[0] = dgate.astype(BF16)
        o_ref[1] = dup.astype(BF16)

    return _call(body, side, [g, w, gu], name=name, grid=(m // tm, f // tk),
                 in_specs=[pl.BlockSpec((tm, n), lambda i, j: (i, 0)), pl.BlockSpec((tk, n), lambda i, j: (j, 0)),
                           pl.BlockSpec((2, tm, tk), lambda i, j: (0, i, j))],
                 out_specs=pl.BlockSpec((2, tm, tk), lambda i, j: (0, i, j)), out_shape=_sds((2, m, f), BF16),
                 compiler_params=_params(("arbitrary", "arbitrary")))


def _rms(x, g):
    r = lax.rsqrt(jnp.mean(x * x, axis=-1, keepdims=True) + EPS)
    return (x * r) * g


def _glu_out(y_pre, q, glu_b, g_norm):
    ya0 = jax.nn.gelu(y_pre)
    return _rms(ya0 * jax.nn.sigmoid(q + glu_b), g_norm)


def _sgu_rows(zu, zv, ln_g, ln_b, w_s, b_st, g_norm):
    heads, t, _ = w_s.shape
    hd = zu.shape[1] // heads
    uu = jax.nn.gelu(zu)
    vv = jax.nn.gelu(zv)
    mu = jnp.mean(vv, axis=-1, keepdims=True)
    xc = vv - mu
    r = lax.rsqrt(jnp.mean(xc * xc, axis=-1, keepdims=True) + EPS)
    vn = (xc * r) * ln_g + ln_b
    row = lax.broadcasted_iota(jnp.int32, (t, t), 0)
    col = lax.broadcasted_iota(jnp.int32, (t, t), 1)
    causal = row >= col
    chunks = []
    for n in range(zu.shape[0] // t):
        blocks = []
        for h in range(heads):
            wm = jnp.where(causal, w_s[h], jnp.zeros_like(w_s[h])).astype(BF16)
            vb = vn[n * t:(n + 1) * t, h * hd:(h + 1) * hd].astype(BF16)
            blocks.append(jnp.dot(wm, vb, preferred_element_type=F32) + b_st[:, h:h + 1])
        chunks.append(jnp.concatenate(blocks, axis=1))
    s = jnp.concatenate(chunks, axis=0) if len(chunks) > 1 else chunks[0]
    return _rms(uu * s, g_norm)


def _swiglu(gate, up):
    return jax.nn.silu(gate) * up


def _head_loss(x2, gpre, pp, b_g, g_final, target):
    gate = jax.nn.sigmoid(gpre + b_g)
    out = _rms(x2 + gate * pp, g_final)
    err = jnp.square(out - target)
    return 0.5 * jnp.sum(jnp.mean(err, axis=-1))


def _ssm_disc(lam_re, lam_im, log_step):
    lr = jnp.minimum(lam_re, LAMBDA_RE_MAX)
    li = lam_im
    dt = jnp.exp(log_step)
    mag = jnp.exp(lr * dt)
    ang = li * dt
    abar_re = mag * jnp.cos(ang)
    abar_im = mag * jnp.sin(ang)
    nr = abar_re - 1.0
    ni = abar_im
    den = lr * lr + li * li
    q_re = (nr * lr + ni * li) / den
    q_im = (ni * lr - nr * li) / den
    return abar_re, abar_im, q_re, q_im


def _ssm_bbar(q_re, q_im, b_re, b_im):
    return q_re * b_re - q_im * b_im, q_re * b_im + q_im * b_re


def _ssm_discretised(lam_re, lam_im, log_step, bt_re, bt_im):
    ar, ai, qr, qi = _ssm_disc(lam_re, lam_im, log_step)
    return (ar, ai, *_ssm_bbar(qr, qi, bt_re, bt_im))


def _adamw(w, g, m, v):
    m = ADAM_B1 * m + (1.0 - ADAM_B1) * g
    v = ADAM_B2 * v + (1.0 - ADAM_B2) * jnp.square(g)
    m_hat = m / (1.0 - ADAM_B1 ** ADAM_STEP)
    v_hat = v / (1.0 - ADAM_B2 ** ADAM_STEP)
    delta = -ADAM_LR * (m_hat / (jnp.sqrt(v_hat) + ADAM_EPS) + ADAM_WD * w)
    return delta, m, v


class _SsmDims:
    def __init__(self, groups, state, gch):
        self.g, self.p, self.h = groups, state, gch
        self.d = groups * gch
        self.cb = min(SSM_CH_BLOCK, self.d)
        self.gb = self.cb // gch
        self.ns = self.gb * state
        self.nb = self.d // self.cb


def _ssm_rows(sd, sp):
    gp = sd.g * sd.p
    log_step = jnp.broadcast_to(sp["ssm_log_step"][:, None], (sd.g, sd.p)).reshape(1, gp)
    bt = [sp[k].reshape(gp, sd.h).T for k in ("ssm_b_re", "ssm_b_im")]
    ct = [sp[k].transpose(1, 0, 2).reshape(sd.h, gp) for k in ("ssm_c_re", "ssm_c_im")]
    return (sp["ssm_lambda_re"].reshape(1, gp), sp["ssm_lambda_im"].reshape(1, gp), log_step, *bt, *ct)


def _block_mask(sd):
    row = lax.broadcasted_iota(jnp.int32, (sd.cb, sd.ns), 0) // sd.h
    col = lax.broadcasted_iota(jnp.int32, (sd.cb, sd.ns), 1) // sd.p
    return row == col


def _scan_consts(pr, pi_, reverse):
    if reverse:
        pi_ = [-v for v in pi_]
    shape = (SUBLANES, pr[0].shape[1])
    rows = lax.broadcasted_iota(jnp.int32, shape, 0)
    parts = []
    for d in (1, 2, 4):
        keep = (rows < SUBLANES - d) if reverse else (rows >= d)
        parts += [jnp.where(keep, jnp.broadcast_to(v[d - 1], shape), 0.0) for v in (pr, pi_)]
    order = range(SUBLANES - 1, -1, -1) if reverse else range(SUBLANES)
    parts += [jnp.concatenate([v[t] for t in order], axis=0) for v in (pr, pi_)]
    return jnp.concatenate(parts, axis=0)


def _ssm_operands(sd, rows):
    cb, ns, nb = sd.cb, sd.ns, sd.nb

    def body(lam_re, lam_im, log_step, bt_re, bt_im, ct_re, ct_im, wb_ref, wbt_ref, wc_ref, wct_ref, cst_f_ref, cst_r_ref):
        ar, ai, bbar_re, bbar_im = _ssm_discretised(lam_re[...], lam_im[...], log_step[...], bt_re[...], bt_im[...])
        pr, pi_ = [ar], [ai]
        for _ in range(SUBLANES - 1):
            pr, pi_ = pr + [pr[-1] * ar - pi_[-1] * ai], pi_ + [pr[-1] * ai + pi_[-1] * ar]
        mask = _block_mask(sd)
        spread = lambda src: jnp.where(mask, jnp.concatenate([src] * sd.gb, axis=0), 0.0)
        for j in range(nb):
            at = slice(j * ns, (j + 1) * ns)
            w = jnp.concatenate([spread(bbar_re[:, at]), spread(bbar_im[:, at])], axis=1)
            v = jnp.concatenate([spread(ct_re[:, at]), -spread(ct_im[:, at])], axis=1)
            wb_ref[j] = w.astype(BF16)
            wbt_ref[j] = w.T.astype(BF16)
            wct_ref[j] = v.astype(BF16)
            wc_ref[j] = v.T.astype(BF16)
            pj, qj = [u[:, at] for u in pr], [u[:, at] for u in pi_]
            cst_f_ref[j] = _scan_consts(pj, qj, False)
            cst_r_ref[j] = _scan_consts(pj, qj, True)

    wide, tall = _sds((nb, cb, 2 * ns), BF16), _sds((nb, 2 * ns, cb), BF16)
    cst = _sds((nb, 8 * SUBLANES, ns), F32)
    vm = pl.BlockSpec(memory_space=pltpu.VMEM)
    return pl.pallas_call(body, name="ssm_operands", in_specs=[vm] * 7, out_specs=[vm] * 6,
                          out_shape=[wide, tall, tall, wide, cst, cst],
                          compiler_params=pltpu.CompilerParams(vmem_limit_bytes=VMEM_LIMIT))(*rows)


def _ssm_param_grads(sd, rows, dwb, dwc, da):
    ns, nb, gp = sd.ns, sd.nb, sd.g * sd.p

    def body(lam_re, lam_im, log_step, bt_re, bt_im, dwb_v, dwc_v, da_v, *outs):
        mask = _block_mask(sd)

        def fold(dense):
            kept = jnp.where(mask, dense, 0.0)
            acc = kept[0:sd.h]
            for gl in range(1, sd.gb):
                acc = acc + kept[gl * sd.h:(gl + 1) * sd.h]
            return acc

        lanes = lambda parts: jnp.concatenate(parts, axis=1) if len(parts) > 1 else parts[0]
        dbbar_re = lanes([fold(dwb_v[j][:, :ns]) for j in range(nb)])
        dbbar_im = lanes([fold(dwb_v[j][:, ns:]) for j in range(nb)])
        dwct = [dwc_v[j] for j in range(nb)]
        d_ct_re = lanes([fold(t[:, :ns]) for t in dwct])
        d_ct_im = -lanes([fold(t[:, ns:]) for t in dwct])
        dabar_re = lanes([da_v[j][0:1, :ns] for j in range(nb)])
        dabar_im = lanes([da_v[j][0:1, ns:] for j in range(nb)])
        _, vjp = jax.vjp(_ssm_discretised, lam_re[...], lam_im[...], log_step[...], bt_re[...], bt_im[...])
        d_lr, d_li, d_ls, d_bt_re, d_bt_im = vjp((dabar_re, dabar_im, dbbar_re, dbbar_im))
        group = (lax.broadcasted_iota(jnp.int32, (gp, sd.g), 0) // sd.p == lax.broadcasted_iota(jnp.int32, (gp, sd.g), 1))
        d_log_step = jnp.dot(d_ls, group.astype(F32), precision=lax.Precision.HIGHEST, preferred_element_type=F32)
        for ref, val in zip(outs, (d_lr, d_li, d_log_step, d_bt_re, d_bt_im, d_ct_re, d_ct_im)):
            ref[...] = val

    row, mat = _sds((1, gp), F32), _sds((sd.h, gp), F32)
    vm = pl.BlockSpec(memory_space=pltpu.VMEM)
    return pl.pallas_call(body, name="ssm_param_grads", in_specs=[vm] * 8, out_specs=[vm] * 7,
                          out_shape=[row, row, _sds((1, sd.g), F32), mat, mat, mat, mat],
                          compiler_params=pltpu.CompilerParams(vmem_limit_bytes=VMEM_LIMIT))(*rows[:5], dwb, dwc, da)


def _block_scan(s_ref, cst_ref, carry_ref, sd, rows, reverse):
    ns = sd.ns
    nblk = rows // SUBLANES
    w = min(SCAN_LANES, ns)
    for c0 in range(0, ns, w):
        re_l, im_l = slice(c0, c0 + w), slice(ns + c0, ns + c0 + w)
        cst = [cst_ref[k * SUBLANES:(k + 1) * SUBLANES, c0:c0 + w] for k in range(8)]

        def step(k, carry, re_l=re_l, im_l=im_l, cst=cst):
            local = []
            for b in range(SCAN_BLOCKS):
                blk = SCAN_BLOCKS * k + b
                blk = (nblk - 1 - blk) if reverse else blk
                r0 = pl.multiple_of(blk * SUBLANES, SUBLANES)
                xr = s_ref[pl.ds(r0, SUBLANES), re_l]
                xi = s_ref[pl.ds(r0, SUBLANES), im_l]
                for n, d in enumerate((1, 2, 4)):
                    ar, ai = cst[2 * n], cst[2 * n + 1]
                    shift = (SUBLANES - d) if reverse else d
                    sr = pltpu.roll(xr, shift, 0)
                    si = pltpu.roll(xi, shift, 0)
                    xr, xi = xr + ar * sr - ai * si, xi + ar * si + ai * sr
                local.append((r0, xr, xi))
            cr, ci = carry
            edge = slice(0, 1) if reverse else slice(SUBLANES - 1, SUBLANES)
            for r0, xr, xi in local:
                br = jnp.broadcast_to(cr, xr.shape)
                bi = jnp.broadcast_to(ci, xi.shape)
                xr, xi = xr + cst[6] * br - cst[7] * bi, xi + cst[6] * bi + cst[7] * br
                s_ref[pl.ds(r0, SUBLANES), re_l] = xr
                s_ref[pl.ds(r0, SUBLANES), im_l] = xi
                cr, ci = xr[edge, :], xi[edge, :]
            return cr, ci

        cr, ci = lax.fori_loop(0, nblk // SCAN_BLOCKS, step, (carry_ref[0:1, re_l], carry_ref[0:1, im_l]))
        carry_ref[0:1, re_l] = cr
        carry_ref[0:1, im_l] = ci


def _ssm_fwd(name, sd, z, wb, wc, cst, d_row, tt=512, side=None):
    n_tok = z.shape[0]
    tt = _pick(n_tok, tt, 16)
    cb, ns2 = sd.cb, 2 * sd.ns

    def body(z_ref, wb_ref, wc_ref, cst_ref, d_ref, y_ref, s_ref, a0_ref, carry_ref):
        @pl.when(pl.program_id(1) == 0)
        def _():
            carry_ref[...] = jnp.zeros(carry_ref.shape, F32)
        u = z_ref[...]
        s_ref[...] = jnp.dot(u.astype(BF16), wb_ref[...], preferred_element_type=F32)
        _block_scan(s_ref, cst_ref, carry_ref, sd, tt, reverse=False)
        y = jnp.dot(s_ref[...].astype(BF16), wc_ref[...], preferred_element_type=F32) + d_ref[...] * u
        y_ref[...] = y
        a0_ref[...] = jax.nn.gelu(y).astype(BF16)

    return _call(
        body, side, [z, wb, wc, cst, d_row], name=name, grid=(sd.nb, n_tok // tt),
        in_specs=[pl.BlockSpec((tt, cb), lambda j, i: (i, j)),
                  pl.BlockSpec((None, cb, ns2), lambda j, i: (j, 0, 0)),
                  pl.BlockSpec((None, ns2, cb), lambda j, i: (j, 0, 0)),
                  pl.BlockSpec((None, 8 * SUBLANES, sd.ns), lambda j, i: (j, 0, 0)),
                  pl.BlockSpec((1, cb), lambda j, i: (0, j))],
        out_specs=[pl.BlockSpec((tt, cb), lambda j, i: (i, j)), pl.BlockSpec((tt, ns2), lambda j, i: (i, j)),
                   pl.BlockSpec((tt, cb), lambda j, i: (i, j))],
        out_shape=[_sds((n_tok, sd.d), F32), _sds((n_tok, sd.nb * ns2), F32), _sds((n_tok, sd.d), BF16)],
        scratch_shapes=[pltpu.VMEM((SUBLANES, ns2), F32)],
        compiler_params=_params(("arbitrary", "arbitrary")))


def _ssm_bwd(name, sd, y_pre, dy_direct, dya0, z, states, wct, wbt, cst_rev, d_row, tt=512, side=None):
    n_tok = z.shape[0]
    tt = _pick(n_tok, tt, 16)
    nt = n_tok // tt
    cb, ns, ns2 = sd.cb, sd.ns, 2 * sd.ns
    blocks_per_tile = tt // SUBLANES
    tn_dims = (((0,), (0,)), ((), ()))

    def body(y_ref, dyd_ref, dya0_ref, z_ref, s_ref, sp_ref, wct_ref, wbt_ref, cst_ref, d_ref,
             du_ref, dwb_ref, dwc_ref, da_ref, dd_ref, lam_ref, carry_ref):
        i = pl.program_id(1)

        @pl.when(i == 0)
        def _():
            carry_ref[...] = jnp.zeros(carry_ref.shape, F32)
            dwb_ref[...] = jnp.zeros(dwb_ref.shape, F32)
            dwc_ref[...] = jnp.zeros(dwc_ref.shape, F32)
            da_ref[...] = jnp.zeros(da_ref.shape, F32)
            dd_ref[...] = jnp.zeros(dd_ref.shape, F32)

        _, gelu_vjp = jax.vjp(jax.nn.gelu, y_ref[...])
        dy_t = dyd_ref[...] + gelu_vjp(dya0_ref[...].astype(F32))[0]
        u = z_ref[...]
        dy16 = dy_t.astype(BF16)
        lam_ref[...] = jnp.dot(dy16, wct_ref[...], preferred_element_type=F32)
        _block_scan(lam_ref, cst_ref, carry_ref, sd, tt, reverse=True)
        lam = lam_ref[...]
        lam16 = lam.astype(BF16)
        du_ref[...] = (jnp.dot(lam16, wbt_ref[...], preferred_element_type=F32) + d_ref[...] * dy_t).astype(BF16)
        dd_ref[0:1, :] += jnp.sum(dy_t * u, axis=0, keepdims=True)
        dwb_ref[...] += lax.dot_general(u.astype(BF16), lam16, tn_dims, preferred_element_type=F32)
        s = s_ref[...]
        dwc_ref[...] += lax.dot_general(dy16, s.astype(BF16), tn_dims, preferred_element_type=F32)
        before = jnp.where(i == nt - 1, 0.0, 1.0) * sp_ref[SUBLANES - 1:SUBLANES, :]
        first_row = lax.broadcasted_iota(jnp.int32, s.shape, 0) == 0
        prev = jnp.where(first_row, jnp.broadcast_to(before, s.shape), pltpu.roll(s, 1, 0))
        lr, li = lam[:, :ns], lam[:, ns:]
        pr, pi_ = prev[:, :ns], prev[:, ns:]
        da_ref[0:1, 0:ns] += jnp.sum(lr * pr + li * pi_, axis=0, keepdims=True)
        da_ref[0:1, ns:ns2] += jnp.sum(li * pr - lr * pi_, axis=0, keepdims=True)

    rev = lambda i: nt - 1 - i
    return _call(
        body, side, [y_pre, dy_direct, dya0, z, states, states, wct, wbt, cst_rev, d_row], name=name, grid=(sd.nb, nt),
        in_specs=[pl.BlockSpec((tt, cb), lambda j, i: (rev(i), j)),
                  pl.BlockSpec((tt, cb), lambda j, i: (rev(i), j)),
                  pl.BlockSpec((tt, cb), lambda j, i: (rev(i), j)),
                  pl.BlockSpec((tt, cb), lambda j, i: (rev(i), j)),
                  pl.BlockSpec((tt, ns2), lambda j, i: (rev(i), j)),
                  pl.BlockSpec((SUBLANES, ns2), lambda j, i: (jnp.maximum(rev(i) * blocks_per_tile - 1, 0), j)),
                  pl.BlockSpec((None, cb, ns2), lambda j, i: (j, 0, 0)),
                  pl.BlockSpec((None, ns2, cb), lambda j, i: (j, 0, 0)),
                  pl.BlockSpec((None, 8 * SUBLANES, ns), lambda j, i: (j, 0, 0)),
                  pl.BlockSpec((1, cb), lambda j, i: (0, j))],
        out_specs=[pl.BlockSpec((tt, cb), lambda j, i: (rev(i), j)),
                   pl.BlockSpec((None, cb, ns2), lambda j, i: (j, 0, 0)),
                   pl.BlockSpec((None, cb, ns2), lambda j, i: (j, 0, 0)),
                   pl.BlockSpec((None, SUBLANES, ns2), lambda j, i: (j, 0, 0)),
                   pl.BlockSpec((None, SUBLANES, cb), lambda j, i: (j, 0, 0))],
        out_shape=[_sds((n_tok, sd.d), BF16), _sds((sd.nb, cb, ns2), F32), _sds((sd.nb, cb, ns2), F32),
                   _sds((sd.nb, SUBLANES, ns2), F32), _sds((sd.nb, SUBLANES, cb), F32)],
        scratch_shapes=[pltpu.VMEM((tt, ns2), F32), pltpu.VMEM((SUBLANES, ns2), F32)],
        compiler_params=_params(("arbitrary", "arbitrary")))


def _hosted(exch, fn, name, *args, **kw):
    side = exch.side(name)
    if side is None:
        return fn(name, *args, **kw)
    out, moved = fn(name, *args, side=side, **kw)
    exch.done(name, moved)
    return out


def _local_grads(x, p, target, sp, exch):
    n_tok, d_model = x.shape
    d_ssm = sp["ssm_d"].shape[0] * sp["ssm_d"].shape[1]
    d_sgu = sp["sgu_ln_g"].shape[-1]
    sd = _SsmDims(sp["ssm_b_re"].shape[0], sp["ssm_b_re"].shape[1], sp["ssm_b_re"].shape[2])
    heads, chunk, _ = sp["sgu_w"].shape
    row = lambda v: v.reshape(1, -1)
    tok = lambda w, dt=F32: _sds((n_tok, w), dt)
    acc = lambda w: _sds((1, w), F32)

    g_mix = row(sp["norm_mix_g"])
    (h1,) = _hosted(exch, _rowwise, "norm_mix", lambda a, g: _rms(a, g), [x], [g_mix], [tok(d_model, BF16)])
    z = _hosted(exch, _mm_nn, "proj_in", h1, exch.weight("w_in"), sharded=True, tm=1024, tn=768)

    ssm_rows = _ssm_rows(sd, sp)
    wb, wbt, wc, wct, cst_fwd, cst_rev = _ssm_operands(sd, ssm_rows)
    d_row = row(sp["ssm_d"])
    y_pre, states, ya0_16 = _hosted(exch, _ssm_fwd, "ssm_fwd", sd, z, wb, wc, cst_fwd, d_row)
    q = _mm_nn("ssm_glu", ya0_16, exch.weight("ssm_glu_w"), tm=1024)
    glu_b, g_ossm = row(sp["ssm_glu_b"]), row(sp["out_norm_ssm_g"])
    (ya_n,) = _rowwise("ssm_glu_out", _glu_out, [y_pre, q], [glu_b, g_ossm], [tok(d_ssm, BF16)])

    assert d_ssm == d_sgu
    zu, zv = _Cols(z, d_sgu, 1), _Cols(z, d_sgu, 2)
    ln_g, ln_b, g_osgu = row(sp["sgu_ln_g"]), row(sp["sgu_ln_b"]), row(sp["out_norm_sgu_g"])
    b_st = sp["sgu_b"].T
    sgu_tr = 2 * chunk

    def sgu_joined(ya_t, zu_t, zv_t, *params):
        return jnp.concatenate([ya_t, _sgu_rows(zu_t, zv_t, *params).astype(BF16)], axis=1)

    (ycat,) = _rowwise("sgu", sgu_joined, [ya_n, zu, zv], [ln_g, ln_b, sp["sgu_w"], b_st, g_osgu],
                       [tok(d_ssm + d_sgu, BF16)], tr=sgu_tr)
    x1 = _hosted(exch, _mm_nn, "proj_out", ycat, exch.weight("w_out"), res=x, tm=1024)

    g_ffn = row(sp["norm_ffn_g"])
    (h2,) = _rowwise("norm_ffn", lambda a, g: _rms(a, g), [x1], [g_ffn], [tok(d_model, BF16)], tr=512)
    act, gu16 = _hosted(exch, _ffn_in_swiglu, "ffn_in", h2, exch.weight("w_ffn_in"))
    x2 = _mm_nn("ffn_out", act, exch.weight("w_ffn_out"), res=x1, tn=1024)

    g_ple = row(sp["norm_ple_g"])
    (h3,) = _rowwise("norm_ple", lambda a, g: _rms(a, g), [x2], [g_ple], [tok(d_model, BF16)], tr=512)
    gpre = _mm_nn("ple_gate", h3, exch.weight("w_ple_gate"), tm=1024, tn=1024)
    (p16,) = _rowwise("ple_cast", lambda a: a, [p], [], [tok(p.shape[1], BF16)])
    pp = _mm_nn("ple_proj", p16, exch.weight("w_ple_proj"), sharded=True, tm=1024)

    b_g, g_fin = row(sp["b_ple_gate"]), row(sp["final_norm_g"])

    def head(x2_t, gpre_t, pp_t, tgt_t, b_g_v, g_fin_v):
        loss, grads = jax.value_and_grad(_head_loss, argnums=(0, 1, 2, 3, 4))(x2_t, gpre_t, pp_t, b_g_v, g_fin_v, tgt_t)
        dx2, dgpre, dpp, db, dg = grads
        return dx2, dgpre.astype(BF16), dpp.astype(BF16), jnp.full((1, LANES), loss, F32), db, dg

    dx2_head, dgpre16, dpp16, loss_row, d_b_g, d_g_fin = _rowwise(
        "head", head, [x2, gpre, pp, target], [b_g, g_fin],
        [tok(d_model), tok(d_model, BF16), tok(d_model, BF16)], [acc(LANES), acc(d_model), acc(d_model)])
    loss = loss_row[0, 0]
    exch.small_grads({"loss": loss_row})

    exch.grad("w_ple_proj", _mm_tn("d_ple_proj", p16, dpp16, shards=N_CHIPS, tk=256))
    exch.grad("w_ple_gate", _mm_tn("d_ple_gate", h3, dgpre16, tn=1024))
    dh3 = _mm_nt("d_h3", dgpre16, exch.weight("w_ple_gate"), out_dtype=BF16, tm=1024, tk=1024)

    def norm_bwd(x_t, dres_t, dh_t, g_v):
        _, vjp = jax.vjp(_rms, x_t, g_v)
        dx, dg = vjp(dh_t.astype(F32))
        dx = dres_t + dx
        return dx, dx.astype(BF16), dg

    dx2, dx2_16, d_g_ple = _rowwise("d_norm_ple", norm_bwd, [x2, dx2_head, dh3], [g_ple],
                                    [tok(d_model), tok(d_model, BF16)], [acc(d_model)])
    exch.grad("w_ffn_out", _mm_tn("d_ffn_out", act, dx2_16, tn=1024))
    dgu16 = _hosted(exch, _d_act_swiglu, "d_act", dx2_16, exch.weight("w_ffn_out"), gu16)
    exch.grad("w_ffn_in", _hosted(exch, _mm_tn, "d_ffn_in", h2, dgu16, shards=N_CHIPS, g_halves=True, tn=1408, g_resident=True))
    dh2 = _hosted(exch, _mm_nt, "d_h2", dgu16, exch.weight("w_ffn_in"), sharded=True, g_halves=True, out_dtype=BF16, tm=256, w_resident=True)
    dx1, dx1_16, d_g_ffn = _rowwise("d_norm_ffn", norm_bwd, [x1, dx2, dh2], [g_ffn],
                                    [tok(d_model), tok(d_model, BF16)], [acc(d_model)])
    exch.grad("w_out", _mm_tn("d_proj_out", ycat, dx1_16, tn=1024))
    dycat = _mm_nt("d_ycat", dx1_16, exch.weight("w_out"), out_dtype=BF16, tm=1024, tk=1024)

    def glu_out_bwd(y_pre_t, q_t, dy_t, glu_b_v, g_v):
        _, vjp = jax.vjp(_glu_out, y_pre_t, q_t, glu_b_v, g_v)
        dy_pre, dq, db, dg = vjp(dy_t.astype(F32))
        return dy_pre, dq.astype(BF16), db, dg

    dy_pre_a, dq16, d_glu_b, d_g_ossm = _rowwise(
        "d_ssm_glu_out", glu_out_bwd, [y_pre, q, _Cols(dycat, d_ssm, 0)], [glu_b, g_ossm],
        [tok(d_ssm), tok(d_ssm, BF16)], [acc(d_ssm), acc(d_ssm)])
    exch.grad("ssm_glu_w", _mm_tn("d_ssm_glu", ya0_16, dq16))
    dya0 = _hosted(exch, _mm_nt, "d_ya0", dq16, exch.weight("ssm_glu_w"), out_dtype=BF16, tm=1024)

    dz_ssm16, dwb, dwc, da, dd = _hosted(exch, _ssm_bwd, "ssm_bwd", sd, y_pre, dy_pre_a, dya0, z, states, wct, wbt,
                                         cst_rev, d_row)

    def sgu_bwd(dz_ssm_t, zu_t, zv_t, dy_t, ln_g_v, ln_b_v, w_v, b_v, g_v):
        _, vjp = jax.vjp(_sgu_rows, zu_t, zv_t, ln_g_v, ln_b_v, w_v, b_v, g_v)
        dzu, dzv, dlg, dlb, dw, db, dg = vjp(dy_t.astype(F32))
        return jnp.concatenate([dz_ssm_t, dzu.astype(BF16), dzv.astype(BF16)], axis=1), dlg, dlb, dw, db, dg

    dz16, d_ln_g, d_ln_b, d_sgu_w, d_b_st, d_g_osgu = _hosted(
        exch, _rowwise, "d_sgu", sgu_bwd, [dz_ssm16, zu, zv, _Cols(dycat, d_sgu, 1)], [ln_g, ln_b, sp["sgu_w"], b_st, g_osgu],
        [tok(d_ssm + 2 * d_sgu, BF16)],
        [acc(d_sgu), acc(d_sgu), _sds(sp["sgu_w"].shape, F32), _sds(b_st.shape, F32), acc(d_sgu)], tr=sgu_tr)

    d_lam_re, d_lam_im, d_log_step, d_bt_re, d_bt_im, d_ct_re, d_ct_im = _ssm_param_grads(sd, ssm_rows, dwb, dwc, da)
    d_b_re, d_b_im = d_bt_re.T, d_bt_im.T
    d_c_re, d_c_im = (t.reshape(sd.h, sd.g, sd.p).transpose(1, 0, 2) for t in (d_ct_re, d_ct_im))
    d_ssm_d = dd[:, 0, :].reshape(sd.g, sd.h)

    exch.small_grads({
        "ssm_lambda_re": d_lam_re, "ssm_lambda_im": d_lam_im, "ssm_log_step": d_log_step,
        "ssm_b_re": d_b_re, "ssm_b_im": d_b_im, "ssm_c_re": d_c_re, "ssm_c_im": d_c_im, "ssm_d": d_ssm_d,
        "ssm_glu_b": d_glu_b, "sgu_ln_g": d_ln_g, "sgu_ln_b": d_ln_b, "sgu_w": d_sgu_w, "sgu_b": d_b_st.T,
        "out_norm_ssm_g": d_g_ossm, "out_norm_sgu_g": d_g_osgu, "norm_ffn_g": d_g_ffn, "norm_ple_g": d_g_ple,
        "b_ple_gate": d_b_g, "final_norm_g": d_g_fin,
    })

    exch.grad("w_in", _hosted(exch, _mm_tn, "d_proj_in", h1, dz16, shards=N_CHIPS, tn=768))
    dh1 = _hosted(exch, _mm_nt, "d_h1", dz16, exch.weight("w_in"), sharded=True, out_dtype=BF16, tm=1024)

    def norm_in_bwd(x_t, dres_t, dh_t, g_v):
        _, vjp = jax.vjp(_rms, x_t, g_v)
        dx, dg = vjp(dh_t.astype(F32))
        return dres_t + dx, dg

    grad_x, d_g_mix = _hosted(exch, _rowwise, "d_norm_mix", norm_in_bwd, [x, dx1, dh1], [g_mix], [tok(d_model)], [acc(d_model)])
    exch.small_grads({"norm_mix_g": d_g_mix})
    return loss, grad_x


def _place():
    x, y, c = lax.axis_index("x"), lax.axis_index("y"), lax.axis_index("c")
    chips = [(1 - x, y), (x, 1 - y), (1 - x, 1 - y)]
    return x, y, c, chips


def _cast_into_slot(name, w2d, shard, tr=256):
    rows, cols = w2d.shape
    rh = rows // 2
    tr = _pick(rh, tr, 16)
    per = rh // tr

    def body(s_ref, a_ref, o_ref):
        o_ref[...] = a_ref[...].astype(BF16)

    grid_spec = pltpu.PrefetchScalarGridSpec(
        num_scalar_prefetch=1, grid=(2, per),
        in_specs=[pl.BlockSpec((tr, cols), lambda h, i, s_ref: (h * per + i, 0))],
        out_specs=pl.BlockSpec((None, None, tr, cols), lambda h, i, s_ref: (s_ref[0], h, i, 0)))
    return pl.pallas_call(body, name=name, grid_spec=grid_spec, out_shape=_sds((N_CHIPS, 2, rh, cols), BF16),
                          compiler_params=_params(("arbitrary", "arbitrary")))(shard.reshape(1).astype(jnp.int32), w2d)


def _exchange_alone(name, side):
    n_in, n_out = len(side.ins), len(side.out_shapes)

    def body(*refs):
        ins, outs, sems = refs[:n_in], refs[n_in:n_in + n_out], refs[n_in + n_out:]
        side.first(ins, outs, *sems)
        if side.mid is not None:
            side.mid(ins, outs, *sems)
        side.last(ins, outs, *sems)

    return pl.pallas_call(
        body, name=name, in_specs=[ANY] * n_in, out_specs=[ANY] * n_out, out_shape=side.out_shapes,
        input_output_aliases=side.aliases,
        scratch_shapes=[pltpu.SemaphoreType.DMA((side.n_sems,)), pltpu.SemaphoreType.DMA((side.n_sems,))],
    )(*side.ins)


def _gather_side(slots, parts=None, mid_late=False):
    n = len(slots)
    parts = parts or [(0, GATHER_PARTS)] * n

    def copies(kind, outs, send_sems, recv_sems):
        x, y, c, chips = _place()

        def remote(k, w, shard, half, to):
            unit = outs[w].shape[2] // GATHER_PARTS
            lo, hi = parts[w]
            ref = outs[w].at[shard, half, pl.ds(lo * unit, (hi - lo) * unit), :]
            return pltpu.make_async_remote_copy(src_ref=ref, dst_ref=ref, send_sem=send_sems.at[k], recv_sem=recv_sems.at[k],
                                                device_id=to, device_id_type=MESH)

        pairs = [(w, j, 2 * cx + cy, (cx, cy)) for w in range(n) for j, (cx, cy) in enumerate(chips)]
        if kind == "sends":
            return [remote(3 * w + j, w, 2 * x + y, c, (*chip, c)) for w, j, _, chip in pairs]
        if kind == "arrivals":
            return [remote(3 * w + j, w, s, c, (x, y, c)) for w, j, s, _ in pairs]
        if kind == "passed":
            return [remote(3 * n + 3 * w + j, w, s, c, (x, y, 1 - c)) for w, j, s, _ in pairs]
        return [remote(3 * n + 3 * w + j, w, s, 1 - c, (x, y, c)) for w, j, s, _ in pairs]

    def first(ins, outs, *sems):
        for cp in copies("sends", outs, *sems):
            cp.start()

    def mid(ins, outs, *sems):
        for arrived, onward in zip(copies("arrivals", outs, *sems), copies("passed", outs, *sems)):
            arrived.wait_recv()
            onward.start()

    def last(ins, outs, *sems):
        for cp in copies("from_sibling", outs, *sems):
            cp.wait_recv()
        for cp in copies("sends", outs, *sems) + copies("passed", outs, *sems):
            cp.wait_send()

    return _Side(slots, [_sds(s.shape, s.dtype) for s in slots], 6 * n, first, last, mid=mid, aliases={w: w for w in range(n)},
                 mid_late=mid_late)


def _swap_side(grads):
    n = len(grads)

    def copies(ins, outs, send_sems, recv_sems):
        x, y, c, _ = _place()
        return [pltpu.make_async_remote_copy(src_ref=ins[w].at[:, 1 - c], dst_ref=outs[w], send_sem=send_sems.at[w],
                                             recv_sem=recv_sems.at[w], device_id=(x, y, 1 - c), device_id_type=MESH)
                for w in range(n)]

    def first(*refs):
        for cp in copies(*refs):
            cp.start()

    def last(*refs):
        for cp in copies(*refs):
            cp.wait()

    return _Side(grads, [_sds((g.shape[0], *g.shape[2:]), g.dtype) for g in grads], n, first, last)


def _scatter_side(halves):
    n = len(halves)

    def copies(ins, outs, send_sems, recv_sems):
        x, y, c, chips = _place()
        return [pltpu.make_async_remote_copy(
            src_ref=ins[w].at[2 * cx + cy], dst_ref=outs[w].at[j], send_sem=send_sems.at[3 * w + j],
            recv_sem=recv_sems.at[3 * w + j], device_id=(cx, cy, c), device_id_type=MESH)
            for w in range(n) for j, (cx, cy) in enumerate(chips)]

    def first(*refs):
        for cp in copies(*refs):
            cp.start()

    def last(*refs):
        for cp in copies(*refs):
            cp.wait()

    return _Side(halves, [_sds((3, *h.shape[1:]), h.dtype) for h in halves], 3 * n, first, last)


def _join_side(slots):
    n = len(slots)

    def copy(outs, send_sems, recv_sems, w, half, to):
        return pltpu.make_async_remote_copy(src_ref=outs[w].at[half], dst_ref=outs[w].at[half], send_sem=send_sems.at[w],
                                            recv_sem=recv_sems.at[w], device_id=to, device_id_type=MESH)

    def first(ins, outs, *sems):
        x, y, c, _ = _place()
        for w in range(n):
            copy(outs, *sems, w, c, (x, y, 1 - c)).start()

    def last(ins, outs, *sems):
        x, y, c, _ = _place()
        for w in range(n):
            copy(outs, *sems, w, 1 - c, (x, y, c)).wait_recv()
        for w in range(n):
            copy(outs, *sems, w, c, (x, y, 1 - c)).wait_send()

    return _Side(slots, [_sds(s.shape, s.dtype) for s in slots], n, first, last, aliases={w: w for w in range(n)})


def _allreduce_small(block, tr=256):
    rows, lanes = block.shape
    tr = _pick(rows, tr, SUBLANES)

    def body(x_ref, o_ref, buf, send_sems, recv_sems):
        x, y, c, chips = _place()
        me, sibling = (x, y, c), (x, y, 1 - c)

        def slot(px, py, pc):
            return buf.at[4 * px + 2 * py + pc]

        def copy(k, block_of, to):
            return pltpu.make_async_remote_copy(src_ref=slot(*block_of), dst_ref=slot(*block_of), send_sem=send_sems.at[k],
                                                recv_sem=recv_sems.at[k], device_id=to, device_id_type=MESH)

        slot(*me)[...] = x_ref[...]
        first = [copy(0, me, sibling)] + [copy(1 + j, me, (*chip, c)) for j, chip in enumerate(chips)]
        for cp in first:
            cp.start()
        passed = [copy(4 + j, (*chip, c), sibling) for j, chip in enumerate(chips)]
        for j, chip in enumerate(chips):
            copy(1 + j, (*chip, c), me).wait_recv()
            passed[j].start()
        copy(0, sibling, me).wait_recv()
        for j, chip in enumerate(chips):
            copy(4 + j, (*chip, 1 - c), me).wait_recv()
        for cp in first + passed:
            cp.wait_send()
        for r0 in range(0, rows, tr):
            acc = buf[0, r0:r0 + tr, :]
            for k in range(1, N_DEV):
                acc = acc + buf[k, r0:r0 + tr, :]
            o_ref[r0:r0 + tr, :] = acc

    vm = pl.BlockSpec(memory_space=pltpu.VMEM)
    return pl.pallas_call(
        body, name="allreduce_small", in_specs=[vm], out_specs=vm, out_shape=_sds((rows, lanes), block.dtype),
        scratch_shapes=[pltpu.VMEM((N_DEV, rows, lanes), block.dtype), pltpu.SemaphoreType.DMA((7,)), pltpu.SemaphoreType.DMA((7,))],
        compiler_params=pltpu.CompilerParams(vmem_limit_bytes=VMEM_LIMIT),
    )(block)


def _small_gather_side(block):
    def copy(kind, j, ins, outs, send_sems, recv_sems):
        x, y, c, chips = _place()
        chip = chips[j] if j is not None else None
        slot = lambda px, py, pc: outs[0].at[4 * px + 2 * py + pc]

        def remote(k, src, dst, to):
            return pltpu.make_async_remote_copy(src_ref=src, dst_ref=dst, send_sem=send_sems.at[k], recv_sem=recv_sems.at[k],
                                                device_id=to, device_id_type=MESH)

        if kind == "to_sibling":
            return remote(0, ins[0], slot(x, y, c), (x, y, 1 - c))
        if kind == "from_sibling":
            return remote(0, ins[0], slot(x, y, 1 - c), (x, y, c))
        if kind == "to_chip":
            return remote(1 + j, ins[0], slot(x, y, c), (*chip, c))
        if kind == "from_chip":
            return remote(1 + j, ins[0], slot(*chip, c), (x, y, c))
        if kind == "pass_on":
            return remote(4 + j, slot(*chip, c), slot(*chip, c), (x, y, 1 - c))
        return remote(4 + j, slot(*chip, 1 - c), slot(*chip, 1 - c), (x, y, c))

    def first(*refs):
        copy("to_sibling", None, *refs).start()
        for j in range(3):
            copy("to_chip", j, *refs).start()

    def mid(*refs):
        for j in range(3):
            copy("from_chip", j, *refs).wait_recv()
            copy("pass_on", j, *refs).start()

    def last(*refs):
        copy("from_sibling", None, *refs).wait_recv()
        for j in range(3):
            copy("passed_on", j, *refs).wait_recv()
        copy("to_sibling", None, *refs).wait_send()
        for j in range(3):
            copy("to_chip", j, *refs).wait_send()
            copy("pass_on", j, *refs).wait_send()

    return _Side([block], [_sds((N_DEV, *block.shape), block.dtype)], 7, first, last, mid=mid, mid_late=True)


def _sum_slots(name, own, gathered, me, tr=512):
    n, rows, cols = gathered.shape
    tr = _pick(rows, tr, SUBLANES)
    if tr < 64:
        tr = rows

    def body(me_ref, own_ref, g_ref, o_ref):
        mine = own_ref[...]
        acc = jnp.where(me_ref[0] == 0, mine, g_ref[0])
        for k in range(1, n):
            acc = acc + jnp.where(me_ref[0] == k, mine, g_ref[k])
        o_ref[...] = acc

    grid_spec = pltpu.PrefetchScalarGridSpec(
        num_scalar_prefetch=1, grid=(rows // tr,),
        in_specs=[pl.BlockSpec((tr, cols), lambda i, me_ref: (i, 0)), pl.BlockSpec((n, tr, cols), lambda i, me_ref: (0, i, 0))],
        out_specs=pl.BlockSpec((tr, cols), lambda i, me_ref: (i, 0)))
    return pl.pallas_call(body, name=name, grid_spec=grid_spec, out_shape=_sds((rows, cols), own.dtype),
                          compiler_params=_params(("arbitrary",)))(me.reshape(1).astype(jnp.int32), own, gathered)


def _sum_received(name, full, c, shard, swapped, received, tr=256):
    n, rows, cols = received.shape
    tr = _pick(rows, tr, 16)

    def body(i_ref, a_ref, b_ref, s_ref, o_ref):
        acc = a_ref[...] + b_ref[...]
        for k in range(n):
            acc = acc + s_ref[k].astype(F32)
        o_ref[...] = acc

    grid_spec = pltpu.PrefetchScalarGridSpec(
        num_scalar_prefetch=1, grid=(rows // tr,),
        in_specs=[pl.BlockSpec((None, None, tr, cols), lambda i, i_ref: (i_ref[1], i_ref[0], i, 0)),
                  pl.BlockSpec((None, tr, cols), lambda i, i_ref: (i_ref[1], i, 0)),
                  pl.BlockSpec((n, tr, cols), lambda i, i_ref: (0, i, 0))],
        out_specs=pl.BlockSpec((None, tr, cols), lambda i, i_ref: (i_ref[0], i, 0)))
    return pl.pallas_call(body, name=name, grid_spec=grid_spec, out_shape=_sds((2, rows, cols), F32),
                          compiler_params=_params(("arbitrary",)))(jnp.stack([c, shard]).astype(jnp.int32), full, swapped, received)


def _add_halves(name, full, c, shard, received, tr=256):
    s, _, rh, cols = full.shape
    tr = _pick(rh, tr, 16)

    def body(i_ref, a_ref, b_ref, o_ref):
        o_ref[...] = (a_ref[...] + b_ref[...]).astype(BF16)

    other = lambda q, i_ref: (i_ref[1] + 1 + q) % s
    grid_spec = pltpu.PrefetchScalarGridSpec(
        num_scalar_prefetch=1, grid=(s - 1, rh // tr),
        in_specs=[pl.BlockSpec((None, None, tr, cols), lambda q, i, i_ref: (other(q, i_ref), i_ref[0], i, 0)),
                  pl.BlockSpec((None, tr, cols), lambda q, i, i_ref: (other(q, i_ref), i, 0))],
        out_specs=pl.BlockSpec((None, tr, cols), lambda q, i, i_ref: (other(q, i_ref), i, 0)))
    return pl.pallas_call(body, name=name, grid_spec=grid_spec, out_shape=_sds((s, rh, cols), BF16),
                          compiler_params=_params(("arbitrary", "arbitrary")))(jnp.stack([c, shard]).astype(jnp.int32), full, received)


LARGE = ("w_in", "ssm_glu_w", "w_out", "w_ffn_in", "w_ffn_out", "w_ple_gate", "w_ple_proj")
COLUMN_SHARDED = ("w_in", "w_ffn_in", "w_ple_proj")
SMALL = ("norm_mix_g", "ssm_lambda_re", "ssm_lambda_im", "ssm_log_step", "ssm_b_re", "ssm_b_im", "ssm_c_re", "ssm_c_im",
         "ssm_d", "ssm_glu_b", "sgu_ln_g", "sgu_ln_b", "sgu_w", "sgu_b", "out_norm_ssm_g", "out_norm_sgu_g", "norm_ffn_g",
         "norm_ple_g", "b_ple_gate", "final_norm_g")
WEIGHTS = ("norm_mix_g", "w_in", "ssm_lambda_re", "ssm_lambda_im", "ssm_log_step", "ssm_b_re", "ssm_b_im", "ssm_c_re",
           "ssm_c_im", "ssm_d", "ssm_glu_w", "ssm_glu_b", "sgu_ln_g", "sgu_ln_b", "sgu_w", "sgu_b", "out_norm_ssm_g",
           "out_norm_sgu_g", "w_out", "norm_ffn_g", "w_ffn_in", "w_ffn_out", "norm_ple_g", "w_ple_gate", "b_ple_gate",
           "w_ple_proj", "final_norm_g")
PACK_ROWS = SUBLANES * LANES


def _pack(arrays):
    parts = []
    for a in arrays:
        flat = a.reshape(-1).astype(F32)
        pad = -flat.shape[0] % PACK_ROWS
        parts.append(jnp.pad(flat, (0, pad)) if pad else flat)
    return jnp.concatenate(parts).reshape(-1, LANES)


def _unpack(packed, like):
    flat = packed.reshape(-1)
    out, at = [], 0
    for a in like:
        size = a.size
        out.append(flat[at:at + size].reshape(a.shape))
        at += size + (-size % PACK_ROWS)
    return out


class _NoExchange:
    def __init__(self, weights):
        self.weights, self.grads, self.small = weights, {}, {}

    def weight(self, name):
        return self.weights[name]

    def grad(self, name, g):
        self.grads[name] = g

    def small_grads(self, grads):
        self.small.update(grads)

    def side(self, host):
        return None


class _MeshExchange:
    GATHER = {"norm_mix": (("w_in", 0, 16),),
              "proj_in": (("ssm_glu_w", 0, 16), ("w_out", 0, 16), ("w_ffn_in", 0, 1)),
              "ssm_fwd": (("w_ffn_in", 1, 13),),
              "proj_out": (("w_ffn_in", 13, 16),),
              "ffn_in": (("w_ffn_out", 0, 16), ("w_ple_gate", 0, 16), ("w_ple_proj", 0, 16))}
    GATHER_LONG = ("norm_mix", "proj_in", "ssm_fwd", "proj_out")
    SWAP = {"d_act": ("w_ple_proj", "w_ple_gate", "w_ffn_out"), "d_h2": ("w_ffn_in",), "d_ya0": ("w_out", "ssm_glu_w")}
    SWAP_ALONE = ("w_in",)
    SCATTER = {"d_ffn_in": ("w_ple_proj", "w_ple_gate", "w_ffn_out"), "ssm_bwd": ("w_ffn_in",),
               "d_sgu": ("w_out", "ssm_glu_w"), "d_h1": ("w_in",)}
    SMALL_GATHER = "d_proj_in"

    def __init__(self, shards, small_like, c, shard, me):
        self.c, self.shard, self.me, self.small_like = c, shard, me, small_like
        self.slots = {k: _cast_into_slot("cast_" + k, shards[k], shard) for k in LARGE}
        self.full, self.received, self.halves, self.quarters, self.small = {}, {}, {}, {}, {}

    def weight(self, name):
        g = self.slots[name]
        _, _, rh, cols = g.shape
        return g.reshape(N_CHIPS, 2 * rh, cols) if name in COLUMN_SHARDED else g.reshape(N_CHIPS * 2 * rh, cols)

    def grad(self, name, g):
        if name not in COLUMN_SHARDED:
            g = g.reshape(N_CHIPS, g.shape[0] // N_CHIPS, g.shape[1])
        self.full[name] = g.reshape(N_CHIPS, 2, g.shape[1] // 2, g.shape[2])
        if name in self.SWAP_ALONE:
            self._swapped((name,), _exchange_alone("grad_swap_" + name, _swap_side([self.full[name]])))

    def _swapped(self, names, received):
        for k, r in zip(names, received):
            self.received[k] = r
            self.halves[k] = _add_halves("grad_add_halves_" + k, self.full[k], self.c, self.shard, r)

    def small_grads(self, grads):
        self.small.update(grads)

    def _packed(self, names):
        return _pack([self.small[k].reshape(self.small_like[k].shape) for k in names])

    def side(self, host):
        if host in self.GATHER:
            return _gather_side([self.slots[k] for k, _, _ in self.GATHER[host]], [(lo, hi) for _, lo, hi in self.GATHER[host]],
                                mid_late=host in self.GATHER_LONG)
        if host in self.SWAP:
            return _swap_side([self.full[k] for k in self.SWAP[host]])
        if host in self.SCATTER:
            return _scatter_side([self.halves[k] for k in self.SCATTER[host]])
        if host == self.SMALL_GATHER:
            self.packed_early = self._packed(SMALL[1:] + ("loss",))
            return _small_gather_side(self.packed_early)
        return None

    def done(self, host, moved):
        if host in self.GATHER:
            self.slots.update(zip([k for k, _, _ in self.GATHER[host]], moved))
        elif host in self.SWAP:
            self._swapped(self.SWAP[host], moved)
        elif host in self.SCATTER:
            self.quarters.update(zip(self.SCATTER[host], moved))
        else:
            (self.gathered_early,) = moved

    def small_reduced(self):
        early = _sum_slots("small_sum", self.packed_early, self.gathered_early, self.me)
        late = _allreduce_small(self._packed(SMALL[:1]))
        loss_at = early.shape[0] - PACK_ROWS // LANES
        return jnp.concatenate([late, early[:loss_at]], axis=0), early[loss_at, 0]

    def summed(self):
        return [_sum_received("grad_sum_" + k, self.full[k], self.c, self.shard, self.received[k], self.quarters[k]) for k in LARGE]


def kernel(x, p, norm_mix_g, w_in, ssm_lambda_re, ssm_lambda_im, ssm_log_step, ssm_b_re, ssm_b_im, ssm_c_re, ssm_c_im, ssm_d, ssm_glu_w, ssm_glu_b, sgu_ln_g, sgu_ln_b, sgu_w, sgu_b, out_norm_ssm_g, out_norm_sgu_g, w_out, norm_ffn_g, w_ffn_in, w_ffn_out, norm_ple_g, w_ple_gate, b_ple_gate, w_ple_proj, final_norm_g, loss_target, m_norm_mix_g, m_w_in, m_ssm_lambda_re, m_ssm_lambda_im, m_ssm_log_step, m_ssm_b_re, m_ssm_b_im, m_ssm_c_re, m_ssm_c_im, m_ssm_d, m_ssm_glu_w, m_ssm_glu_b, m_sgu_ln_g, m_sgu_ln_b, m_sgu_w, m_sgu_b, m_out_norm_ssm_g, m_out_norm_sgu_g, m_w_out, m_norm_ffn_g, m_w_ffn_in, m_w_ffn_out, m_norm_ple_g, m_w_ple_gate, m_b_ple_gate, m_w_ple_proj, m_final_norm_g, v_norm_mix_g, v_w_in, v_ssm_lambda_re, v_ssm_lambda_im, v_ssm_log_step, v_ssm_b_re, v_ssm_b_im, v_ssm_c_re, v_ssm_c_im, v_ssm_d, v_ssm_glu_w, v_ssm_glu_b, v_sgu_ln_g, v_sgu_ln_b, v_sgu_w, v_sgu_b, v_out_norm_ssm_g, v_out_norm_sgu_g, v_w_out, v_norm_ffn_g, v_w_ffn_in, v_w_ffn_out, v_norm_ple_g, v_w_ple_gate, v_b_ple_gate, v_w_ple_proj, v_final_norm_g):
    given = dict(locals())
    w = {k: given[k] for k in WEIGHTS}
    m = {k: given["m_" + k] for k in WEIGHTS}
    v = {k: given["v_" + k] for k in WEIGHTS}
    c = lax.axis_index("c")
    shard = 2 * lax.axis_index("x") + lax.axis_index("y")

    small_like = {k: w[k] for k in SMALL}
    small_like["loss"] = _sds((1, LANES), F32)
    exch = _MeshExchange({k: w[k].reshape(w[k].shape[1:]) for k in LARGE}, small_like, c, shard, 2 * shard + c)
    unlayer = lambda a: a if a.ndim == 1 else a[0]
    sp = {k: unlayer(w[k]) for k in SMALL}
    n_tok, d_model = x.shape[1:]
    _, grad_x = _local_grads(x.reshape(n_tok, d_model), p.reshape(n_tok, p.shape[-1]),
                             loss_target.reshape(n_tok, d_model), sp, exch)

    grad_w, delta_w, new_m, new_v = {}, {}, {}, {}
    halves = exch.summed()
    packed_g, loss = exch.small_reduced()
    like = _sds(packed_g.shape, F32)
    (d_s, m_s, v_s), joined = _rowwise(
        "adamw_small", _adamw, [_pack([w[k] for k in SMALL]), packed_g, _pack([m[k] for k in SMALL]), _pack([v[k] for k in SMALL])],
        [], [like, like, like], side=_join_side(halves))
    shapes = [w[k] for k in SMALL]
    for k, g_k, d_k, m_k, v_k in zip(SMALL, _unpack(packed_g, shapes), _unpack(d_s, shapes), _unpack(m_s, shapes), _unpack(v_s, shapes)):
        grad_w[k], delta_w[k], new_m[k], new_v[k] = g_k, d_k, m_k, v_k

    reduced = {k: j.reshape(2 * j.shape[1], j.shape[2]) for k, j in zip(LARGE, joined)}
    for k in LARGE:
        shape = w[k].shape
        two_d = lambda a: a.reshape(shape[1:])
        like = _sds(shape[1:], F32)
        update = lambda w_t, g_t, m_t, v_t: (g_t, *_adamw(w_t, g_t, m_t, v_t))
        outs = _rowwise("adamw_" + k, update, [two_d(w[k]), reduced[k], two_d(m[k]), two_d(v[k])], [], [like, like, like, like])
        grad_w[k], delta_w[k], new_m[k], new_v[k] = (a.reshape(shape) for a in outs)

    return (loss, grad_x.reshape(x.shape), *[grad_w[k] for k in WEIGHTS], *[delta_w[k] for k in WEIGHTS],
            *[new_m[k] for k in WEIGHTS], *[new_v[k] for k in WEIGHTS])
```

```python
import functools

import jax
import jax.numpy as jnp
from jax import lax
from jax.experimental import pallas as pl
from jax.experimental.pallas import tpu as pltpu

F32 = jnp.float32
BF16 = jnp.bfloat16

EPS = 1e-6
LAMBDA_RE_MAX = -1e-4
ADAM_LR = 0.001
ADAM_B1 = 0.9
ADAM_B2 = 0.999
ADAM_EPS = 1e-08
ADAM_WD = 0.01
ADAM_STEP = 10

N_CHIPS = 4
N_DEV = 8
SUBLANES = 8
LANES = 128
SSM_CH_BLOCK = 256
SCAN_LANES = 256
SCAN_BLOCKS = 4
GATHER_PARTS = 16
VMEM_LIMIT = 56 * 1024 * 1024

MESH = pl.DeviceIdType.MESH


def _pick(n, pref, mult):
    if n <= pref:
        return n
    t = (pref // mult) * mult
    while t >= mult:
        if n % t == 0:
            return t
        t -= mult
    return n


def _params(semantics):
    return pltpu.CompilerParams(dimension_semantics=semantics, vmem_limit_bytes=VMEM_LIMIT)


class _Cols:
    def __init__(self, arr, width, blk):
        self.arr, self.width, self.blk = arr, width, blk


def _sds(shape, dtype):
    return jax.ShapeDtypeStruct(tuple(shape), dtype)


ANY = pl.BlockSpec(memory_space=pl.ANY)


class _Side:
    def __init__(self, ins, out_shapes, n_sems, first, last, mid=None, aliases=None, mid_late=False):
        self.ins, self.out_shapes, self.n_sems = list(ins), list(out_shapes), n_sems
        self.first, self.mid, self.last, self.mid_late = first, mid, last, mid_late
        self.aliases = dict(aliases or {})


def _call(body, side, operands, *, name, grid, in_specs, out_specs, out_shape, compiler_params, scratch_shapes=()):
    if side is None:
        return pl.pallas_call(body, name=name, grid=grid, in_specs=in_specs, out_specs=out_specs, out_shape=out_shape,
                              scratch_shapes=list(scratch_shapes), compiler_params=compiler_params)(*operands)
    single = not isinstance(out_specs, (list, tuple))
    out_specs = [out_specs] if single else list(out_specs)
    out_shape = [out_shape] if single else list(out_shape)
    n_in, n_out, n_scr = len(in_specs), len(out_specs), len(scratch_shapes)
    n_sin, n_sout = len(side.ins), len(side.out_shapes)
    steps = 1
    for g in grid:
        steps *= g

    def hosted(*refs):
        ins, s_ins = refs[:n_in], refs[n_in:n_in + n_sin]
        at = n_in + n_sin
        outs, s_outs = refs[at:at + n_out], refs[at + n_out:at + n_out + n_sout]
        scratch = refs[at + n_out + n_sout:at + n_out + n_sout + n_scr]
        sems = refs[-2:]
        step = pl.program_id(0)
        for d in range(1, len(grid)):
            step = step * grid[d] + pl.program_id(d)

        @pl.when(step == 0)
        def _():
            side.first(s_ins, s_outs, *sems)

        if side.mid is not None:
            @pl.when(step == (steps - 1 if side.mid_late else (3 * steps) // 4))
            def _():
                side.mid(s_ins, s_outs, *sems)

        body(*ins, *outs, *scratch)

        @pl.when(step == steps - 1)
        def _():
            side.last(s_ins, s_outs, *sems)

    res = pl.pallas_call(
        hosted, name=name, grid=grid, in_specs=[*in_specs, *[ANY] * n_sin], out_specs=[*out_specs, *[ANY] * n_sout],
        out_shape=[*out_shape, *side.out_shapes], input_output_aliases={n_in + i: n_out + o for i, o in side.aliases.items()},
        scratch_shapes=[*scratch_shapes, pltpu.SemaphoreType.DMA((side.n_sems,)), pltpu.SemaphoreType.DMA((side.n_sems,))],
        compiler_params=compiler_params)(*operands, *side.ins)
    return (res[0] if single else list(res[:n_out])), list(res[n_out:])


def _rowwise(name, fn, rows, params, row_outs, acc_outs=(), tr=256, side=None):
    rows = [r if isinstance(r, _Cols) else _Cols(r, r.shape[1], 0) for r in rows]
    m = rows[0].arr.shape[0]
    tr = _pick(m, tr, 16)
    n_in = len(rows) + len(params)
    n_ro = len(row_outs)

    def body(*refs):
        vals = fn(*[r[...] for r in refs[:n_in]])
        if not isinstance(vals, (tuple, list)):
            vals = (vals,)
        outs = refs[n_in:]
        for r, v in zip(outs[:n_ro], vals[:n_ro]):
            r[...] = v.astype(r.dtype)
        first = pl.program_id(0) == 0
        for r, v in zip(outs[n_ro:], vals[n_ro:]):
            @pl.when(first)
            def _():
                r[...] = jnp.zeros(r.shape, r.dtype)
            r[...] += v.astype(r.dtype).reshape(r.shape)

    in_specs = [pl.BlockSpec((tr, r.width), lambda i, b=r.blk: (i, b)) for r in rows]
    in_specs += [pl.BlockSpec(p.shape, lambda i, nd=p.ndim: (0,) * nd) for p in params]
    out_specs = [pl.BlockSpec((tr, o.shape[1]), lambda i: (i, 0)) for o in row_outs]
    out_specs += [pl.BlockSpec(o.shape, lambda i, nd=len(o.shape): (0,) * nd) for o in acc_outs]
    return _call(body, side, [*[r.arr for r in rows], *params], name=name, grid=(m // tr,), in_specs=in_specs,
                 out_specs=out_specs, out_shape=[*row_outs, *acc_outs], compiler_params=_params(("arbitrary",)))


def _grid_order(swap):
    if not swap:
        return (lambda grid: grid), (lambda f: f)
    return (lambda grid: grid[::-1]), (lambda f: (lambda j, i: f(i, j)))


def _mm_nn(name, a, w, *, sharded=False, res=None, out_dtype=F32, tm=512, tn=512, w_resident=False, side=None):
    m, k = a.shape
    tm = _pick(m, tm, 16)
    order, ix = _grid_order(w_resident)
    if sharded:
        s, _, ns = w.shape
        n = s * ns
        tn = _pick(ns, tn, LANES)
        per = ns // tn
        w_spec = pl.BlockSpec((None, k, tn), ix(lambda i, j: (j // per, 0, j % per)))
    else:
        n = w.shape[1]
        tn = _pick(n, tn, LANES)
        w_spec = pl.BlockSpec((k, tn), ix(lambda i, j: (0, j)))

    def body(a_ref, w_ref, *rest):
        acc = jnp.dot(a_ref[...], w_ref[...], preferred_element_type=F32)
        if res is not None:
            acc = acc + rest[0][...]
        rest[-1][...] = acc.astype(out_dtype)

    in_specs = [pl.BlockSpec((tm, k), ix(lambda i, j: (i, 0))), w_spec]
    ops = [a, w]
    if res is not None:
        in_specs.append(pl.BlockSpec((tm, tn), ix(lambda i, j: (i, j))))
        ops.append(res)
    return _call(body, side, ops, name=name, grid=order((m // tm, n // tn)), in_specs=in_specs,
                 out_specs=pl.BlockSpec((tm, tn), ix(lambda i, j: (i, j))), out_shape=_sds((m, n), out_dtype),
                 compiler_params=_params(("arbitrary", "arbitrary")))


def _mm_nt(name, g, w, *, sharded=False, g_halves=False, out_dtype=F32, tm=512, tk=512, w_resident=False, side=None):
    m, n = g.shape[-2:]
    tm = _pick(m, tm, 16)
    order, ix = _grid_order(w_resident)
    dims = (((1,), (1,)), ((), ()))
    g_spec = pl.BlockSpec((2, tm, n), ix(lambda i, j: (0, i, 0))) if g_halves else pl.BlockSpec((tm, n), ix(lambda i, j: (i, 0)))
    if sharded:
        s, k, ns = w.shape
        tk = _pick(k, tk, LANES)
        w_spec = pl.BlockSpec((s, tk, ns), ix(lambda i, j: (0, j, 0)))

        def columns(g_ref, q):
            if not g_halves:
                return g_ref[:, q * ns:(q + 1) * ns]
            half, at = divmod(q, s // 2)
            return g_ref[half, :, at * ns:(at + 1) * ns]

        def body(g_ref, w_ref, o_ref):
            acc = lax.dot_general(columns(g_ref, 0), w_ref[0], dims, preferred_element_type=F32)
            for q in range(1, s):
                acc = acc + lax.dot_general(columns(g_ref, q), w_ref[q], dims, preferred_element_type=F32)
            o_ref[...] = acc.astype(out_dtype)
    else:
        k = w.shape[0]
        tk = _pick(k, tk, LANES)
        w_spec = pl.BlockSpec((tk, n), ix(lambda i, j: (j, 0)))

        def body(g_ref, w_ref, o_ref):
            o_ref[...] = lax.dot_general(g_ref[...], w_ref[...], dims, preferred_element_type=F32).astype(out_dtype)

    return _call(body, side, [g, w], name=name, grid=order((m // tm, k // tk)), in_specs=[g_spec, w_spec],
                 out_specs=pl.BlockSpec((tm, tk), ix(lambda i, j: (i, j))), out_shape=_sds((m, k), out_dtype),
                 compiler_params=_params(("arbitrary", "arbitrary")))


def _mm_tn(name, a, g, *, shards=0, g_halves=False, tk=512, tn=512, g_resident=False, side=None):
    m, k = a.shape
    n = 2 * g.shape[2] if g_halves else g.shape[1]
    tk = _pick(k, tk, LANES)
    order, ix = _grid_order(g_resident)
    dims = (((0,), (0,)), ((), ()))
    if shards:
        ns = n // shards
        tn = _pick(ns, tn, LANES)
        per = ns // tn
        out_spec = pl.BlockSpec((None, tk, tn), ix(lambda i, j: (j // per, i, j % per)))
        out_shape = _sds((shards, k, ns), F32)
    else:
        tn = _pick(n, tn, LANES)
        out_spec = pl.BlockSpec((tk, tn), ix(lambda i, j: (i, j)))
        out_shape = _sds((k, n), F32)

    def body(a_ref, g_ref, o_ref):
        o_ref[...] = lax.dot_general(a_ref[...], g_ref[...], dims, preferred_element_type=F32)

    if g_halves:
        per_half = n // 2 // tn
        g_spec = pl.BlockSpec((None, m, tn), ix(lambda i, j: (j // per_half, 0, j % per_half)))
    else:
        g_spec = pl.BlockSpec((m, tn), ix(lambda i, j: (0, j)))
    return _call(body, side, [a, g], name=name, grid=order((k // tk, n // tn)),
                 in_specs=[pl.BlockSpec((m, tk), ix(lambda i, j: (0, i))), g_spec],
                 out_specs=out_spec, out_shape=out_shape, compiler_params=_params(("arbitrary", "arbitrary")))


def _ffn_in_swiglu(name, a, w, *, tm=512, tn=1408, side=None):
    m, k = a.shape
    s, _, ns = w.shape
    f = s * ns // 2
    tm = _pick(m, tm, 16)
    tn = _pick(ns, tn, LANES)
    per = ns // tn
    order, ix = _grid_order(True)

    def body(a_ref, wg_ref, wu_ref, act_ref, gu_ref):
        x = a_ref[...]
        gate = jnp.dot(x, wg_ref[...], preferred_element_type=F32)
        up = jnp.dot(x, wu_ref[...], preferred_element_type=F32)
        act_ref[...] = _swiglu(gate, up).astype(BF16)
        gu_ref[0] = gate.astype(BF16)
        gu_ref[1] = up.astype(BF16)

    return _call(body, side, [a, w, w], name=name, grid=order((m // tm, f // tn)),
                 in_specs=[pl.BlockSpec((tm, k), ix(lambda i, j: (i, 0))),
                           pl.BlockSpec((None, k, tn), ix(lambda i, j: (j // per, 0, j % per))),
                           pl.BlockSpec((None, k, tn), ix(lambda i, j: (s // 2 + j // per, 0, j % per)))],
                 out_specs=[pl.BlockSpec((tm, tn), ix(lambda i, j: (i, j))), pl.BlockSpec((2, tm, tn), ix(lambda i, j: (0, i, j)))],
                 out_shape=[_sds((m, f), BF16), _sds((2, m, f), BF16)], compiler_params=_params(("arbitrary", "arbitrary")))


def _d_act_swiglu(name, g, w, gu, *, tm=1024, tk=512, side=None):
    m, n = g.shape
    f = w.shape[0]
    tm = _pick(m, tm, 16)
    tk = _pick(f, tk, LANES)
    dims = (((1,), (1,)), ((), ()))

    def body(g_ref, w_ref, gu_ref, o_ref):
        dact = lax.dot_general(g_ref[...], w_ref[...], dims, preferred_element_type=F32)
        _, vjp = jax.vjp(_swiglu, gu_ref[0].astype(F32), gu_ref[1].astype(F32))
        dgate, dup = vjp(dact)
        o_ref[0] = dgate.astype(BF16)
        o_ref[1] = dup.astype(BF16)

    return _call(body, side, [g, w, gu], name=name, grid=(m // tm, f // tk),
                 in_specs=[pl.BlockSpec((tm, n), lambda i, j: (i, 0)), pl.BlockSpec((tk, n), lambda i, j: (j, 0)),
                           pl.BlockSpec((2, tm, tk), lambda i, j: (0, i, j))],
                 out_specs=pl.BlockSpec((2, tm, tk), lambda i, j: (0, i, j)), out_shape=_sds((2, m, f), BF16),
                 compiler_params=_params(("arbitrary", "arbitrary")))


def _rms(x, g):
    r = lax.rsqrt(jnp.mean(x * x, axis=-1, keepdims=True) + EPS)
    return (x * r) * g


def _glu_out(y_pre, q, glu_b, g_norm):
    ya0 = jax.nn.gelu(y_pre)
    return _rms(ya0 * jax.nn.sigmoid(q + glu_b), g_norm)


def _sgu_rows(zu, zv, ln_g, ln_b, w_s, b_st, g_norm):
    heads, t, _ = w_s.shape
    hd = zu.shape[1] // heads
    uu = jax.nn.gelu(zu)
    vv = jax.nn.gelu(zv)
    mu = jnp.mean(vv, axis=-1, keepdims=True)
    xc = vv - mu
    r = lax.rsqrt(jnp.mean(xc * xc, axis=-1, keepdims=True) + EPS)
    vn = (xc * r) * ln_g + ln_b
    row = lax.broadcasted_iota(jnp.int32, (t, t), 0)
    col = lax.broadcasted_iota(jnp.int32, (t, t), 1)
    causal = row >= col
    chunks = []
    for n in range(zu.shape[0] // t):
        blocks = []
        for h in range(heads):
            wm = jnp.where(causal, w_s[h], jnp.zeros_like(w_s[h])).astype(BF16)
            vb = vn[n * t:(n + 1) * t, h * hd:(h + 1) * hd].astype(BF16)
            blocks.append(jnp.dot(wm, vb, preferred_element_type=F32) + b_st[:, h:h + 1])
        chunks.append(jnp.concatenate(blocks, axis=1))
    s = jnp.concatenate(chunks, axis=0) if len(chunks) > 1 else chunks[0]
    return _rms(uu * s, g_norm)


def _swiglu(gate, up):
    return jax.nn.silu(gate) * up


def _head_loss(x2, gpre, pp, b_g, g_final, target):
    gate = jax.nn.sigmoid(gpre + b_g)
    out = _rms(x2 + gate * pp, g_final)
    err = jnp.square(out - target)
    return 0.5 * jnp.sum(jnp.mean(err, axis=-1))


def _ssm_disc(lam_re, lam_im, log_step):
    lr = jnp.minimum(lam_re, LAMBDA_RE_MAX)
    li = lam_im
    dt = jnp.exp(log_step)
    mag = jnp.exp(lr * dt)
    ang = li * dt
    abar_re = mag * jnp.cos(ang)
    abar_im = mag * jnp.sin(ang)
    nr = abar_re - 1.0
    ni = abar_im
    den = lr * lr + li * li
    q_re = (nr * lr + ni * li) / den
    q_im = (ni * lr - nr * li) / den
    return abar_re, abar_im, q_re, q_im


def _ssm_bbar(q_re, q_im, b_re, b_im):
    return q_re * b_re - q_im * b_im, q_re * b_im + q_im * b_re


def _ssm_discretised(lam_re, lam_im, log_step, bt_re, bt_im):
    ar, ai, qr, qi = _ssm_disc(lam_re, lam_im, log_step)
    return (ar, ai, *_ssm_bbar(qr, qi, bt_re, bt_im))


def _adamw(w, g, m, v):
    m = ADAM_B1 * m + (1.0 - ADAM_B1) * g
    v = ADAM_B2 * v + (1.0 - ADAM_B2) * jnp.square(g)
    m_hat = m / (1.0 - ADAM_B1 ** ADAM_STEP)
    v_hat = v / (1.0 - ADAM_B2 ** ADAM_STEP)
    delta = -ADAM_LR * (m_hat / (jnp.sqrt(v_hat) + ADAM_EPS) + ADAM_WD * w)
    return delta, m, v


class _SsmDims:
    def __init__(self, groups, state, gch):
        self.g, self.p, self.h = groups, state, gch
        self.d = groups * gch
        self.cb = min(SSM_CH_BLOCK, self.d)
        self.gb = self.cb // gch
        self.ns = self.gb * state
        self.nb = self.d // self.cb


def _ssm_rows(sd, sp):
    gp = sd.g * sd.p
    log_step = jnp.broadcast_to(sp["ssm_log_step"][:, None], (sd.g, sd.p)).reshape(1, gp)
    bt = [sp[k].reshape(gp, sd.h).T for k in ("ssm_b_re", "ssm_b_im")]
    ct = [sp[k].transpose(1, 0, 2).reshape(sd.h, gp) for k in ("ssm_c_re", "ssm_c_im")]
    return (sp["ssm_lambda_re"].reshape(1, gp), sp["ssm_lambda_im"].reshape(1, gp), log_step, *bt, *ct)


def _block_mask(sd):
    row = lax.broadcasted_iota(jnp.int32, (sd.cb, sd.ns), 0) // sd.h
    col = lax.broadcasted_iota(jnp.int32, (sd.cb, sd.ns), 1) // sd.p
    return row == col


def _scan_consts(pr, pi_, reverse):
    if reverse:
        pi_ = [-v for v in pi_]
    shape = (SUBLANES, pr[0].shape[1])
    rows = lax.broadcasted_iota(jnp.int32, shape, 0)
    parts = []
    for d in (1, 2, 4):
        keep = (rows < SUBLANES - d) if reverse else (rows >= d)
        parts += [jnp.where(keep, jnp.broadcast_to(v[d - 1], shape), 0.0) for v in (pr, pi_)]
    order = range(SUBLANES - 1, -1, -1) if reverse else range(SUBLANES)
    parts += [jnp.concatenate([v[t] for t in order], axis=0) for v in (pr, pi_)]
    return jnp.concatenate(parts, axis=0)


def _ssm_operands(sd, rows):
    cb, ns, nb = sd.cb, sd.ns, sd.nb

    def body(lam_re, lam_im, log_step, bt_re, bt_im, ct_re, ct_im, wb_ref, wbt_ref, wc_ref, wct_ref, cst_f_ref, cst_r_ref):
        ar, ai, bbar_re, bbar_im = _ssm_discretised(lam_re[...], lam_im[...], log_step[...], bt_re[...], bt_im[...])
        pr, pi_ = [ar], [ai]
        for _ in range(SUBLANES - 1):
            pr, pi_ = pr + [pr[-1] * ar - pi_[-1] * ai], pi_ + [pr[-1] * ai + pi_[-1] * ar]
        mask = _block_mask(sd)
        spread = lambda src: jnp.where(mask, jnp.concatenate([src] * sd.gb, axis=0), 0.0)
        for j in range(nb):
            at = slice(j * ns, (j + 1) * ns)
            w = jnp.concatenate([spread(bbar_re[:, at]), spread(bbar_im[:, at])], axis=1)
            v = jnp.concatenate([spread(ct_re[:, at]), -spread(ct_im[:, at])], axis=1)
            wb_ref[j] = w.astype(BF16)
            wbt_ref[j] = w.T.astype(BF16)
            wct_ref[j] = v.astype(BF16)
            wc_ref[j] = v.T.astype(BF16)
            pj, qj = [u[:, at] for u in pr], [u[:, at] for u in pi_]
            cst_f_ref[j] = _scan_consts(pj, qj, False)
            cst_r_ref[j] = _scan_consts(pj, qj, True)

    wide, tall = _sds((nb, cb, 2 * ns), BF16), _sds((nb, 2 * ns, cb), BF16)
    cst = _sds((nb, 8 * SUBLANES, ns), F32)
    vm = pl.BlockSpec(memory_space=pltpu.VMEM)
    return pl.pallas_call(body, name="ssm_operands", in_specs=[vm] * 7, out_specs=[vm] * 6,
                          out_shape=[wide, tall, tall, wide, cst, cst],
                          compiler_params=pltpu.CompilerParams(vmem_limit_bytes=VMEM_LIMIT))(*rows)


def _ssm_param_grads(sd, rows, dwb, dwc, da):
    ns, nb, gp = sd.ns, sd.nb, sd.g * sd.p

    def body(lam_re, lam_im, log_step, bt_re, bt_im, dwb_v, dwc_v, da_v, *outs):
        mask = _block_mask(sd)

        def fold(dense):
            kept = jnp.where(mask, dense, 0.0)
            acc = kept[0:sd.h]
            for gl in range(1, sd.gb):
                acc = acc + kept[gl * sd.h:(gl + 1) * sd.h]
            return acc

        lanes = lambda parts: jnp.concatenate(parts, axis=1) if len(parts) > 1 else parts[0]
        dbbar_re = lanes([fold(dwb_v[j][:, :ns]) for j in range(nb)])
        dbbar_im = lanes([fold(dwb_v[j][:, ns:]) for j in range(nb)])
        dwct = [dwc_v[j] for j in range(nb)]
        d_ct_re = lanes([fold(t[:, :ns]) for t in dwct])
        d_ct_im = -lanes([fold(t[:, ns:]) for t in dwct])
        dabar_re = lanes([da_v[j][0:1, :ns] for j in range(nb)])
        dabar_im = lanes([da_v[j][0:1, ns:] for j in range(nb)])
        _, vjp = jax.vjp(_ssm_discretised, lam_re[...], lam_im[...], log_step[...], bt_re[...], bt_im[...])
        d_lr, d_li, d_ls, d_bt_re, d_bt_im = vjp((dabar_re, dabar_im, dbbar_re, dbbar_im))
        group = (lax.broadcasted_iota(jnp.int32, (gp, sd.g), 0) // sd.p == lax.broadcasted_iota(jnp.int32, (gp, sd.g), 1))
        d_log_step = jnp.dot(d_ls, group.astype(F32), precision=lax.Precision.HIGHEST, preferred_element_type=F32)
        for ref, val in zip(outs, (d_lr, d_li, d_log_step, d_bt_re, d_bt_im, d_ct_re, d_ct_im)):
            ref[...] = val

    row, mat = _sds((1, gp), F32), _sds((sd.h, gp), F32)
    vm = pl.BlockSpec(memory_space=pltpu.VMEM)
    return pl.pallas_call(body, name="ssm_param_grads", in_specs=[vm] * 8, out_specs=[vm] * 7,
                          out_shape=[row, row, _sds((1, sd.g), F32), mat, mat, mat, mat],
                          compiler_params=pltpu.CompilerParams(vmem_limit_bytes=VMEM_LIMIT))(*rows[:5], dwb, dwc, da)


def _block_scan(s_ref, cst_ref, carry_ref, sd, rows, reverse):
    ns = sd.ns
    nblk = rows // SUBLANES
    w = min(SCAN_LANES, ns)
    for c0 in range(0, ns, w):
        re_l, im_l = slice(c0, c0 + w), slice(ns + c0, ns + c0 + w)
        cst = [cst_ref[k * SUBLANES:(k + 1) * SUBLANES, c0:c0 + w] for k in range(8)]

        def step(k, carry, re_l=re_l, im_l=im_l, cst=cst):
            local = []
            for b in range(SCAN_BLOCKS):
                blk = SCAN_BLOCKS * k + b
                blk = (nblk - 1 - blk) if reverse else blk
                r0 = pl.multiple_of(blk * SUBLANES, SUBLANES)
                xr = s_ref[pl.ds(r0, SUBLANES), re_l]
                xi = s_ref[pl.ds(r0, SUBLANES), im_l]
                for n, d in enumerate((1, 2, 4)):
                    ar, ai = cst[2 * n], cst[2 * n + 1]
                    shift = (SUBLANES - d) if reverse else d
                    sr = pltpu.roll(xr, shift, 0)
                    si = pltpu.roll(xi, shift, 0)
                    xr, xi = xr + ar * sr - ai * si, xi + ar * si + ai * sr
                local.append((r0, xr, xi))
            cr, ci = carry
            edge = slice(0, 1) if reverse else slice(SUBLANES - 1, SUBLANES)
            for r0, xr, xi in local:
                br = jnp.broadcast_to(cr, xr.shape)
                bi = jnp.broadcast_to(ci, xi.shape)
                xr, xi = xr + cst[6] * br - cst[7] * bi, xi + cst[6] * bi + cst[7] * br
                s_ref[pl.ds(r0, SUBLANES), re_l] = xr
                s_ref[pl.ds(r0, SUBLANES), im_l] = xi
                cr, ci = xr[edge, :], xi[edge, :]
            return cr, ci

        cr, ci = lax.fori_loop(0, nblk // SCAN_BLOCKS, step, (carry_ref[0:1, re_l], carry_ref[0:1, im_l]))
        carry_ref[0:1, re_l] = cr
        carry_ref[0:1, im_l] = ci


def _ssm_fwd(name, sd, z, wb, wc, cst, d_row, tt=512, side=None):
    n_tok = z.shape[0]
    tt = _pick(n_tok, tt, 16)
    cb, ns2 = sd.cb, 2 * sd.ns

    def body(z_ref, wb_ref, wc_ref, cst_ref, d_ref, y_ref, s_ref, a0_ref, carry_ref):
        @pl.when(pl.program_id(1) == 0)
        def _():
            carry_ref[...] = jnp.zeros(carry_ref.shape, F32)
        u = z_ref[...]
        s_ref[...] = jnp.dot(u.astype(BF16), wb_ref[...], preferred_element_type=F32)
        _block_scan(s_ref, cst_ref, carry_ref, sd, tt, reverse=False)
        y = jnp.dot(s_ref[...].astype(BF16), wc_ref[...], preferred_element_type=F32) + d_ref[...] * u
        y_ref[...] = y
        a0_ref[...] = jax.nn.gelu(y).astype(BF16)

    return _call(
        body, side, [z, wb, wc, cst, d_row], name=name, grid=(sd.nb, n_tok // tt),
        in_specs=[pl.BlockSpec((tt, cb), lambda j, i: (i, j)),
                  pl.BlockSpec((None, cb, ns2), lambda j, i: (j, 0, 0)),
                  pl.BlockSpec((None, ns2, cb), lambda j, i: (j, 0, 0)),
                  pl.BlockSpec((None, 8 * SUBLANES, sd.ns), lambda j, i: (j, 0, 0)),
                  pl.BlockSpec((1, cb), lambda j, i: (0, j))],
        out_specs=[pl.BlockSpec((tt, cb), lambda j, i: (i, j)), pl.BlockSpec((tt, ns2), lambda j, i: (i, j)),
                   pl.BlockSpec((tt, cb), lambda j, i: (i, j))],
        out_shape=[_sds((n_tok, sd.d), F32), _sds((n_tok, sd.nb * ns2), F32), _sds((n_tok, sd.d), BF16)],
        scratch_shapes=[pltpu.VMEM((SUBLANES, ns2), F32)],
        compiler_params=_params(("arbitrary", "arbitrary")))


def _ssm_bwd(name, sd, y_pre, dy_direct, dya0, z, states, wct, wbt, cst_rev, d_row, tt=512, side=None):
    n_tok = z.shape[0]
    tt = _pick(n_tok, tt, 16)
    nt = n_tok // tt
    cb, ns, ns2 = sd.cb, sd.ns, 2 * sd.ns
    blocks_per_tile = tt // SUBLANES
    tn_dims = (((0,), (0,)), ((), ()))

    def body(y_ref, dyd_ref, dya0_ref, z_ref, s_ref, sp_ref, wct_ref, wbt_ref, cst_ref, d_ref,
             du_ref, dwb_ref, dwc_ref, da_ref, dd_ref, lam_ref, carry_ref):
        i = pl.program_id(1)

        @pl.when(i == 0)
        def _():
            carry_ref[...] = jnp.zeros(carry_ref.shape, F32)
            dwb_ref[...] = jnp.zeros(dwb_ref.shape, F32)
            dwc_ref[...] = jnp.zeros(dwc_ref.shape, F32)
            da_ref[...] = jnp.zeros(da_ref.shape, F32)
            dd_ref[...] = jnp.zeros(dd_ref.shape, F32)

        _, gelu_vjp = jax.vjp(jax.nn.gelu, y_ref[...])
        dy_t = dyd_ref[...] + gelu_vjp(dya0_ref[...].astype(F32))[0]
        u = z_ref[...]
        dy16 = dy_t.astype(BF16)
        lam_ref[...] = jnp.dot(dy16, wct_ref[...], preferred_element_type=F32)
        _block_scan(lam_ref, cst_ref, carry_ref, sd, tt, reverse=True)
        lam = lam_ref[...]
        lam16 = lam.astype(BF16)
        du_ref[...] = (jnp.dot(lam16, wbt_ref[...], preferred_element_type=F32) + d_ref[...] * dy_t).astype(BF16)
        dd_ref[0:1, :] += jnp.sum(dy_t * u, axis=0, keepdims=True)
        dwb_ref[...] += lax.dot_general(u.astype(BF16), lam16, tn_dims, preferred_element_type=F32)
        s = s_ref[...]
        dwc_ref[...] += lax.dot_general(dy16, s.astype(BF16), tn_dims, preferred_element_type=F32)
        before = jnp.where(i == nt - 1, 0.0, 1.0) * sp_ref[SUBLANES - 1:SUBLANES, :]
        first_row = lax.broadcasted_iota(jnp.int32, s.shape, 0) == 0
        prev = jnp.where(first_row, jnp.broadcast_to(before, s.shape), pltpu.roll(s, 1, 0))
        lr, li = lam[:, :ns], lam[:, ns:]
        pr, pi_ = prev[:, :ns], prev[:, ns:]
        da_ref[0:1, 0:ns] += jnp.sum(lr * pr + li * pi_, axis=0, keepdims=True)
        da_ref[0:1, ns:ns2] += jnp.sum(li * pr - lr * pi_, axis=0, keepdims=True)

    rev = lambda i: nt - 1 - i
    return _call(
        body, side, [y_pre, dy_direct, dya0, z, states, states, wct, wbt, cst_rev, d_row], name=name, grid=(sd.nb, nt),
        in_specs=[pl.BlockSpec((tt, cb), lambda j, i: (rev(i), j)),
                  pl.BlockSpec((tt, cb), lambda j, i: (rev(i), j)),
                  pl.BlockSpec((tt, cb), lambda j, i: (rev(i), j)),
                  pl.BlockSpec((tt, cb), lambda j, i: (rev(i), j)),
                  pl.BlockSpec((tt, ns2), lambda j, i: (rev(i), j)),
                  pl.BlockSpec((SUBLANES, ns2), lambda j, i: (jnp.maximum(rev(i) * blocks_per_tile - 1, 0), j)),
                  pl.BlockSpec((None, cb, ns2), lambda j, i: (j, 0, 0)),
                  pl.BlockSpec((None, ns2, cb), lambda j, i: (j, 0, 0)),
                  pl.BlockSpec((None, 8 * SUBLANES, ns), lambda j, i: (j, 0, 0)),
                  pl.BlockSpec((1, cb), lambda j, i: (0, j))],
        out_specs=[pl.BlockSpec((tt, cb), lambda j, i: (rev(i), j)),
                   pl.BlockSpec((None, cb, ns2), lambda j, i: (j, 0, 0)),
                   pl.BlockSpec((None, cb, ns2), lambda j, i: (j, 0, 0)),
                   pl.BlockSpec((None, SUBLANES, ns2), lambda j, i: (j, 0, 0)),
                   pl.BlockSpec((None, SUBLANES, cb), lambda j, i: (j, 0, 0))],
        out_shape=[_sds((n_tok, sd.d), BF16), _sds((sd.nb, cb, ns2), F32), _sds((sd.nb, cb, ns2), F32),
                   _sds((sd.nb, SUBLANES, ns2), F32), _sds((sd.nb, SUBLANES, cb), F32)],
        scratch_shapes=[pltpu.VMEM((tt, ns2), F32), pltpu.VMEM((SUBLANES, ns2), F32)],
        compiler_params=_params(("arbitrary", "arbitrary")))


def _hosted(exch, fn, name, *args, **kw):
    side = exch.side(name)
    if side is None:
        return fn(name, *args, **kw)
    out, moved = fn(name, *args, side=side, **kw)
    exch.done(name, moved)
    return out


def _local_grads(x, p, target, sp, exch):
    n_tok, d_model = x.shape
    d_ssm = sp["ssm_d"].shape[0] * sp["ssm_d"].shape[1]
    d_sgu = sp["sgu_ln_g"].shape[-1]
    sd = _SsmDims(sp["ssm_b_re"].shape[0], sp["ssm_b_re"].shape[1], sp["ssm_b_re"].shape[2])
    heads, chunk, _ = sp["sgu_w"].shape
    row = lambda v: v.reshape(1, -1)
    tok = lambda w, dt=F32: _sds((n_tok, w), dt)
    acc = lambda w: _sds((1, w), F32)

    g_mix = row(sp["norm_mix_g"])
    (h1,) = _hosted(exch, _rowwise, "norm_mix", lambda a, g: _rms(a, g), [x], [g_mix], [tok(d_model, BF16)])
    z = _hosted(exch, _mm_nn, "proj_in", h1, exch.weight("w_in"), sharded=True, tm=1024, tn=768)

    ssm_rows = _ssm_rows(sd, sp)
    wb, wbt, wc, wct, cst_fwd, cst_rev = _ssm_operands(sd, ssm_rows)
    d_row = row(sp["ssm_d"])
    y_pre, states, ya0_16 = _hosted(exch, _ssm_fwd, "ssm_fwd", sd, z, wb, wc, cst_fwd, d_row)
    q = _mm_nn("ssm_glu", ya0_16, exch.weight("ssm_glu_w"), tm=1024)
    glu_b, g_ossm = row(sp["ssm_glu_b"]), row(sp["out_norm_ssm_g"])
    (ya_n,) = _rowwise("ssm_glu_out", _glu_out, [y_pre, q], [glu_b, g_ossm], [tok(d_ssm, BF16)], tr=512)

    assert d_ssm == d_sgu
    zu, zv = _Cols(z, d_sgu, 1), _Cols(z, d_sgu, 2)
    ln_g, ln_b, g_osgu = row(sp["sgu_ln_g"]), row(sp["sgu_ln_b"]), row(sp["out_norm_sgu_g"])
    b_st = sp["sgu_b"].T
    sgu_tr = 2 * chunk

    def sgu_joined(ya_t, zu_t, zv_t, *params):
        return jnp.concatenate([ya_t, _sgu_rows(zu_t, zv_t, *params).astype(BF16)], axis=1)

    (ycat,) = _rowwise("sgu", sgu_joined, [ya_n, zu, zv], [ln_g, ln_b, sp["sgu_w"], b_st, g_osgu],
                       [tok(d_ssm + d_sgu, BF16)], tr=sgu_tr)
    x1 = _hosted(exch, _mm_nn, "proj_out", ycat, exch.weight("w_out"), res=x, tm=1024)

    g_ffn = row(sp["norm_ffn_g"])
    (h2,) = _rowwise("norm_ffn", lambda a, g: _rms(a, g), [x1], [g_ffn], [tok(d_model, BF16)], tr=512)
    act, gu16 = _hosted(exch, _ffn_in_swiglu, "ffn_in", h2, exch.weight("w_ffn_in"))
    x2 = _mm_nn("ffn_out", act, exch.weight("w_ffn_out"), res=x1, tn=1024)

    g_ple = row(sp["norm_ple_g"])
    (h3,) = _rowwise("norm_ple", lambda a, g: _rms(a, g), [x2], [g_ple], [tok(d_model, BF16)], tr=512)
    gpre = _mm_nn("ple_gate", h3, exch.weight("w_ple_gate"), tm=1024, tn=1024)
    (p16,) = _rowwise("ple_cast", lambda a: a, [p], [], [tok(p.shape[1], BF16)])
    pp = _mm_nn("ple_proj", p16, exch.weight("w_ple_proj"), sharded=True, tm=1024)

    b_g, g_fin = row(sp["b_ple_gate"]), row(sp["final_norm_g"])

    def head(x2_t, gpre_t, pp_t, tgt_t, b_g_v, g_fin_v):
        loss, grads = jax.value_and_grad(_head_loss, argnums=(0, 1, 2, 3, 4))(x2_t, gpre_t, pp_t, b_g_v, g_fin_v, tgt_t)
        dx2, dgpre, dpp, db, dg = grads
        return dx2, dgpre.astype(BF16), dpp.astype(BF16), jnp.full((1, LANES), loss, F32), db, dg

    dx2_head, dgpre16, dpp16, loss_row, d_b_g, d_g_fin = _rowwise(
        "head", head, [x2, gpre, pp, target], [b_g, g_fin],
        [tok(d_model), tok(d_model, BF16), tok(d_model, BF16)], [acc(LANES), acc(d_model), acc(d_model)])
    loss = loss_row[0, 0]
    exch.small_grads({"loss": loss_row})

    exch.grad("w_ple_proj", _mm_tn("d_ple_proj", p16, dpp16, shards=N_CHIPS, tk=256))
    exch.grad("w_ple_gate", _mm_tn("d_ple_gate", h3, dgpre16, tn=1024))
    dh3 = _mm_nt("d_h3", dgpre16, exch.weight("w_ple_gate"), out_dtype=BF16, tm=1024, tk=1024)

    def norm_bwd(x_t, dres_t, dh_t, g_v):
        _, vjp = jax.vjp(_rms, x_t, g_v)
        dx, dg = vjp(dh_t.astype(F32))
        dx = dres_t + dx
        return dx, dx.astype(BF16), dg

    dx2, dx2_16, d_g_ple = _rowwise("d_norm_ple", norm_bwd, [x2, dx2_head, dh3], [g_ple],
                                    [tok(d_model), tok(d_model, BF16)], [acc(d_model)], tr=512)
    exch.grad("w_ffn_out", _mm_tn("d_ffn_out", act, dx2_16, tn=1024))
    dgu16 = _hosted(exch, _d_act_swiglu, "d_act", dx2_16, exch.weight("w_ffn_out"), gu16, tm=2048)
    exch.grad("w_ffn_in", _hosted(exch, _mm_tn, "d_ffn_in", h2, dgu16, shards=N_CHIPS, g_halves=True, tn=1408, g_resident=True))
    dh2 = _hosted(exch, _mm_nt, "d_h2", dgu16, exch.weight("w_ffn_in"), sharded=True, g_halves=True, out_dtype=BF16, tm=256, w_resident=True)
    dx1, dx1_16, d_g_ffn = _rowwise("d_norm_ffn", norm_bwd, [x1, dx2, dh2], [g_ffn],
                                    [tok(d_model), tok(d_model, BF16)], [acc(d_model)], tr=512)
    exch.grad("w_out", _mm_tn("d_proj_out", ycat, dx1_16, tn=1024))
    dycat = _mm_nt("d_ycat", dx1_16, exch.weight("w_out"), out_dtype=BF16, tm=1024, tk=1024)

    def glu_out_bwd(y_pre_t, q_t, dy_t, glu_b_v, g_v):
        _, vjp = jax.vjp(_glu_out, y_pre_t, q_t, glu_b_v, g_v)
        dy_pre, dq, db, dg = vjp(dy_t.astype(F32))
        return dy_pre, dq.astype(BF16), db, dg

    dy_pre_a, dq16, d_glu_b, d_g_ossm = _rowwise(
        "d_ssm_glu_out", glu_out_bwd, [y_pre, q, _Cols(dycat, d_ssm, 0)], [glu_b, g_ossm],
        [tok(d_ssm), tok(d_ssm, BF16)], [acc(d_ssm), acc(d_ssm)], tr=512)
    exch.grad("ssm_glu_w", _mm_tn("d_ssm_glu", ya0_16, dq16))
    dya0 = _hosted(exch, _mm_nt, "d_ya0", dq16, exch.weight("ssm_glu_w"), out_dtype=BF16, tm=1024)

    dz_ssm16, dwb, dwc, da, dd = _hosted(exch, _ssm_bwd, "ssm_bwd", sd, y_pre, dy_pre_a, dya0, z, states, wct, wbt,
                                         cst_rev, d_row)

    def sgu_bwd(dz_ssm_t, zu_t, zv_t, dy_t, ln_g_v, ln_b_v, w_v, b_v, g_v):
        _, vjp = jax.vjp(_sgu_rows, zu_t, zv_t, ln_g_v, ln_b_v, w_v, b_v, g_v)
        dzu, dzv, dlg, dlb, dw, db, dg = vjp(dy_t.astype(F32))
        return jnp.concatenate([dz_ssm_t, dzu.astype(BF16), dzv.astype(BF16)], axis=1), dlg, dlb, dw, db, dg

    dz16, d_ln_g, d_ln_b, d_sgu_w, d_b_st, d_g_osgu = _hosted(
        exch, _rowwise, "d_sgu", sgu_bwd, [dz_ssm16, zu, zv, _Cols(dycat, d_sgu, 1)], [ln_g, ln_b, sp["sgu_w"], b_st, g_osgu],
        [tok(d_ssm + 2 * d_sgu, BF16)],
        [acc(d_sgu), acc(d_sgu), _sds(sp["sgu_w"].shape, F32), _sds(b_st.shape, F32), acc(d_sgu)], tr=sgu_tr)

    d_lam_re, d_lam_im, d_log_step, d_bt_re, d_bt_im, d_ct_re, d_ct_im = _ssm_param_grads(sd, ssm_rows, dwb, dwc, da)
    d_b_re, d_b_im = d_bt_re.T, d_bt_im.T
    d_c_re, d_c_im = (t.reshape(sd.h, sd.g, sd.p).transpose(1, 0, 2) for t in (d_ct_re, d_ct_im))
    d_ssm_d = dd[:, 0, :].reshape(sd.g, sd.h)

    exch.small_grads({
        "ssm_lambda_re": d_lam_re, "ssm_lambda_im": d_lam_im, "ssm_log_step": d_log_step,
        "ssm_b_re": d_b_re, "ssm_b_im": d_b_im, "ssm_c_re": d_c_re, "ssm_c_im": d_c_im, "ssm_d": d_ssm_d,
        "ssm_glu_b": d_glu_b, "sgu_ln_g": d_ln_g, "sgu_ln_b": d_ln_b, "sgu_w": d_sgu_w, "sgu_b": d_b_st.T,
        "out_norm_ssm_g": d_g_ossm, "out_norm_sgu_g": d_g_osgu, "norm_ffn_g": d_g_ffn, "norm_ple_g": d_g_ple,
        "b_ple_gate": d_b_g, "final_norm_g": d_g_fin,
    })

    exch.grad("w_in", _hosted(exch, _mm_tn, "d_proj_in", h1, dz16, shards=N_CHIPS, tn=768))
    dh1 = _hosted(exch, _mm_nt, "d_h1", dz16, exch.weight("w_in"), sharded=True, out_dtype=BF16, tm=1024)

    def norm_in_bwd(x_t, dres_t, dh_t, g_v):
        _, vjp = jax.vjp(_rms, x_t, g_v)
        dx, dg = vjp(dh_t.astype(F32))
        return dres_t + dx, dg

    grad_x, d_g_mix = _hosted(exch, _rowwise, "d_norm_mix", norm_in_bwd, [x, dx1, dh1], [g_mix], [tok(d_model)], [acc(d_model)],
                              tr=512)
    exch.small_grads({"norm_mix_g": d_g_mix})
    return loss, grad_x


def _place():
    x, y, c = lax.axis_index("x"), lax.axis_index("y"), lax.axis_index("c")
    chips = [(1 - x, y), (x, 1 - y), (1 - x, 1 - y)]
    return x, y, c, chips


def _cast_into_slot(name, w2d, shard, tr=256):
    rows, cols = w2d.shape
    rh = rows // 2
    tr = _pick(rh, tr, 16)
    per = rh // tr

    def body(s_ref, a_ref, o_ref):
        o_ref[...] = a_ref[...].astype(BF16)

    grid_spec = pltpu.PrefetchScalarGridSpec(
        num_scalar_prefetch=1, grid=(2, per),
        in_specs=[pl.BlockSpec((tr, cols), lambda h, i, s_ref: (h * per + i, 0))],
        out_specs=pl.BlockSpec((None, None, tr, cols), lambda h, i, s_ref: (s_ref[0], h, i, 0)))
    return pl.pallas_call(body, name=name, grid_spec=grid_spec, out_shape=_sds((N_CHIPS, 2, rh, cols), BF16),
                          compiler_params=_params(("arbitrary", "arbitrary")))(shard.reshape(1).astype(jnp.int32), w2d)


def _exchange_alone(name, side):
    n_in, n_out = len(side.ins), len(side.out_shapes)

    def body(*refs):
        ins, outs, sems = refs[:n_in], refs[n_in:n_in + n_out], refs[n_in + n_out:]
        side.first(ins, outs, *sems)
        if side.mid is not None:
            side.mid(ins, outs, *sems)
        side.last(ins, outs, *sems)

    return pl.pallas_call(
        body, name=name, in_specs=[ANY] * n_in, out_specs=[ANY] * n_out, out_shape=side.out_shapes,
        input_output_aliases=side.aliases,
        scratch_shapes=[pltpu.SemaphoreType.DMA((side.n_sems,)), pltpu.SemaphoreType.DMA((side.n_sems,))],
    )(*side.ins)


def _gather_side(slots, parts=None, mid_late=False):
    n = len(slots)
    parts = parts or [(0, GATHER_PARTS)] * n

    def copies(kind, outs, send_sems, recv_sems):
        x, y, c, chips = _place()

        def remote(k, w, shard, half, to):
            unit = outs[w].shape[2] // GATHER_PARTS
            lo, hi = parts[w]
            ref = outs[w].at[shard, half, pl.ds(lo * unit, (hi - lo) * unit), :]
            return pltpu.make_async_remote_copy(src_ref=ref, dst_ref=ref, send_sem=send_sems.at[k], recv_sem=recv_sems.at[k],
                                                device_id=to, device_id_type=MESH)

        pairs = [(w, j, 2 * cx + cy, (cx, cy)) for w in range(n) for j, (cx, cy) in enumerate(chips)]
        if kind == "sends":
            return [remote(3 * w + j, w, 2 * x + y, c, (*chip, c)) for w, j, _, chip in pairs]
        if kind == "arrivals":
            return [remote(3 * w + j, w, s, c, (x, y, c)) for w, j, s, _ in pairs]
        if kind == "passed":
            return [remote(3 * n + 3 * w + j, w, s, c, (x, y, 1 - c)) for w, j, s, _ in pairs]
        return [remote(3 * n + 3 * w + j, w, s, 1 - c, (x, y, c)) for w, j, s, _ in pairs]

    def first(ins, outs, *sems):
        for cp in copies("sends", outs, *sems):
            cp.start()

    def mid(ins, outs, *sems):
        for arrived, onward in zip(copies("arrivals", outs, *sems), copies("passed", outs, *sems)):
            arrived.wait_recv()
            onward.start()

    def last(ins, outs, *sems):
        for cp in copies("from_sibling", outs, *sems):
            cp.wait_recv()
        for cp in copies("sends", outs, *sems) + copies("passed", outs, *sems):
            cp.wait_send()

    return _Side(slots, [_sds(s.shape, s.dtype) for s in slots], 6 * n, first, last, mid=mid, aliases={w: w for w in range(n)},
                 mid_late=mid_late)


def _swap_side(grads):
    n = len(grads)

    def copies(ins, outs, send_sems, recv_sems):
        x, y, c, _ = _place()
        return [pltpu.make_async_remote_copy(src_ref=ins[w].at[:, 1 - c], dst_ref=outs[w], send_sem=send_sems.at[w],
                                             recv_sem=recv_sems.at[w], device_id=(x, y, 1 - c), device_id_type=MESH)
                for w in range(n)]

    def first(*refs):
        for cp in copies(*refs):
            cp.start()

    def last(*refs):
        for cp in copies(*refs):
            cp.wait()

    return _Side(grads, [_sds((g.shape[0], *g.shape[2:]), g.dtype) for g in grads], n, first, last)


def _scatter_side(halves):
    n = len(halves)

    def copies(ins, outs, send_sems, recv_sems):
        x, y, c, chips = _place()
        return [pltpu.make_async_remote_copy(
            src_ref=ins[w].at[2 * cx + cy], dst_ref=outs[w].at[j], send_sem=send_sems.at[3 * w + j],
            recv_sem=recv_sems.at[3 * w + j], device_id=(cx, cy, c), device_id_type=MESH)
            for w in range(n) for j, (cx, cy) in enumerate(chips)]

    def first(*refs):
        for cp in copies(*refs):
            cp.start()

    def last(*refs):
        for cp in copies(*refs):
            cp.wait()

    return _Side(halves, [_sds((3, *h.shape[1:]), h.dtype) for h in halves], 3 * n, first, last)


def _join_side(slots):
    n = len(slots)

    def copy(outs, send_sems, recv_sems, w, half, to):
        return pltpu.make_async_remote_copy(src_ref=outs[w].at[half], dst_ref=outs[w].at[half], send_sem=send_sems.at[w],
                                            recv_sem=recv_sems.at[w], device_id=to, device_id_type=MESH)

    def first(ins, outs, *sems):
        x, y, c, _ = _place()
        for w in range(n):
            copy(outs, *sems, w, c, (x, y, 1 - c)).start()

    def last(ins, outs, *sems):
        x, y, c, _ = _place()
        for w in range(n):
            copy(outs, *sems, w, 1 - c, (x, y, c)).wait_recv()
        for w in range(n):
            copy(outs, *sems, w, c, (x, y, 1 - c)).wait_send()

    return _Side(slots, [_sds(s.shape, s.dtype) for s in slots], n, first, last, aliases={w: w for w in range(n)})


def _allreduce_small(block, tr=256):
    rows, lanes = block.shape
    tr = _pick(rows, tr, SUBLANES)

    def body(x_ref, o_ref, buf, send_sems, recv_sems):
        x, y, c, chips = _place()
        me, sibling = (x, y, c), (x, y, 1 - c)

        def slot(px, py, pc):
            return buf.at[4 * px + 2 * py + pc]

        def copy(k, block_of, to):
            return pltpu.make_async_remote_copy(src_ref=slot(*block_of), dst_ref=slot(*block_of), send_sem=send_sems.at[k],
                                                recv_sem=recv_sems.at[k], device_id=to, device_id_type=MESH)

        slot(*me)[...] = x_ref[...]
        first = [copy(0, me, sibling)] + [copy(1 + j, me, (*chip, c)) for j, chip in enumerate(chips)]
        for cp in first:
            cp.start()
        passed = [copy(4 + j, (*chip, c), sibling) for j, chip in enumerate(chips)]
        for j, chip in enumerate(chips):
            copy(1 + j, (*chip, c), me).wait_recv()
            passed[j].start()
        copy(0, sibling, me).wait_recv()
        for j, chip in enumerate(chips):
            copy(4 + j, (*chip, 1 - c), me).wait_recv()
        for cp in first + passed:
            cp.wait_send()
        for r0 in range(0, rows, tr):
            acc = buf[0, r0:r0 + tr, :]
            for k in range(1, N_DEV):
                acc = acc + buf[k, r0:r0 + tr, :]
            o_ref[r0:r0 + tr, :] = acc

    vm = pl.BlockSpec(memory_space=pltpu.VMEM)
    return pl.pallas_call(
        body, name="allreduce_small", in_specs=[vm], out_specs=vm, out_shape=_sds((rows, lanes), block.dtype),
        scratch_shapes=[pltpu.VMEM((N_DEV, rows, lanes), block.dtype), pltpu.SemaphoreType.DMA((7,)), pltpu.SemaphoreType.DMA((7,))],
        compiler_params=pltpu.CompilerParams(vmem_limit_bytes=VMEM_LIMIT),
    )(block)


def _small_gather_side(block):
    def copy(kind, j, ins, outs, send_sems, recv_sems):
        x, y, c, chips = _place()
        chip = chips[j] if j is not None else None
        slot = lambda px, py, pc: outs[0].at[4 * px + 2 * py + pc]

        def remote(k, src, dst, to):
            return pltpu.make_async_remote_copy(src_ref=src, dst_ref=dst, send_sem=send_sems.at[k], recv_sem=recv_sems.at[k],
                                                device_id=to, device_id_type=MESH)

        if kind == "to_sibling":
            return remote(0, ins[0], slot(x, y, c), (x, y, 1 - c))
        if kind == "from_sibling":
            return remote(0, ins[0], slot(x, y, 1 - c), (x, y, c))
        if kind == "to_chip":
            return remote(1 + j, ins[0], slot(x, y, c), (*chip, c))
        if kind == "from_chip":
            return remote(1 + j, ins[0], slot(*chip, c), (x, y, c))
        if kind == "pass_on":
            return remote(4 + j, slot(*chip, c), slot(*chip, c), (x, y, 1 - c))
        return remote(4 + j, slot(*chip, 1 - c), slot(*chip, 1 - c), (x, y, c))

    def first(*refs):
        copy("to_sibling", None, *refs).start()
        for j in range(3):
            copy("to_chip", j, *refs).start()

    def mid(*refs):
        for j in range(3):
            copy("from_chip", j, *refs).wait_recv()
            copy("pass_on", j, *refs).start()

    def last(*refs):
        copy("from_sibling", None, *refs).wait_recv()
        for j in range(3):
            copy("passed_on", j, *refs).wait_recv()
        copy("to_sibling", None, *refs).wait_send()
        for j in range(3):
            copy("to_chip", j, *refs).wait_send()
            copy("pass_on", j, *refs).wait_send()

    return _Side([block], [_sds((N_DEV, *block.shape), block.dtype)], 7, first, last, mid=mid, mid_late=True)


def _sum_slots(name, own, gathered, me, tr=512):
    n, rows, cols = gathered.shape
    tr = _pick(rows, tr, SUBLANES)
    if tr < 64:
        tr = rows

    def body(me_ref, own_ref, g_ref, o_ref):
        mine = own_ref[...]
        acc = jnp.where(me_ref[0] == 0, mine, g_ref[0])
        for k in range(1, n):
            acc = acc + jnp.where(me_ref[0] == k, mine, g_ref[k])
        o_ref[...] = acc

    grid_spec = pltpu.PrefetchScalarGridSpec(
        num_scalar_prefetch=1, grid=(rows // tr,),
        in_specs=[pl.BlockSpec((tr, cols), lambda i, me_ref: (i, 0)), pl.BlockSpec((n, tr, cols), lambda i, me_ref: (0, i, 0))],
        out_specs=pl.BlockSpec((tr, cols), lambda i, me_ref: (i, 0)))
    return pl.pallas_call(body, name=name, grid_spec=grid_spec, out_shape=_sds((rows, cols), own.dtype),
                          compiler_params=_params(("arbitrary",)))(me.reshape(1).astype(jnp.int32), own, gathered)


def _sum_received(name, full, c, shard, swapped, received, tr=256):
    n, rows, cols = received.shape
    tr = _pick(rows, tr, 16)

    def body(i_ref, a_ref, b_ref, s_ref, o_ref):
        acc = a_ref[...] + b_ref[...]
        for k in range(n):
            acc = acc + s_ref[k].astype(F32)
        o_ref[...] = acc

    grid_spec = pltpu.PrefetchScalarGridSpec(
        num_scalar_prefetch=1, grid=(rows // tr,),
        in_specs=[pl.BlockSpec((None, None, tr, cols), lambda i, i_ref: (i_ref[1], i_ref[0], i, 0)),
                  pl.BlockSpec((None, tr, cols), lambda i, i_ref: (i_ref[1], i, 0)),
                  pl.BlockSpec((n, tr, cols), lambda i, i_ref: (0, i, 0))],
        out_specs=pl.BlockSpec((None, tr, cols), lambda i, i_ref: (i_ref[0], i, 0)))
    return pl.pallas_call(body, name=name, grid_spec=grid_spec, out_shape=_sds((2, rows, cols), F32),
                          compiler_params=_params(("arbitrary",)))(jnp.stack([c, shard]).astype(jnp.int32), full, swapped, received)


def _add_halves(name, full, c, shard, received, tr=256):
    s, _, rh, cols = full.shape
    tr = _pick(rh, tr, 16)

    def body(i_ref, a_ref, b_ref, o_ref):
        o_ref[...] = (a_ref[...] + b_ref[...]).astype(BF16)

    other = lambda q, i_ref: (i_ref[1] + 1 + q) % s
    grid_spec = pltpu.PrefetchScalarGridSpec(
        num_scalar_prefetch=1, grid=(s - 1, rh // tr),
        in_specs=[pl.BlockSpec((None, None, tr, cols), lambda q, i, i_ref: (other(q, i_ref), i_ref[0], i, 0)),
                  pl.BlockSpec((None, tr, cols), lambda q, i, i_ref: (other(q, i_ref), i, 0))],
        out_specs=pl.BlockSpec((None, tr, cols), lambda q, i, i_ref: (other(q, i_ref), i, 0)))
    return pl.pallas_call(body, name=name, grid_spec=grid_spec, out_shape=_sds((s, rh, cols), BF16),
                          compiler_params=_params(("arbitrary", "arbitrary")))(jnp.stack([c, shard]).astype(jnp.int32), full, received)


LARGE = ("w_in", "ssm_glu_w", "w_out", "w_ffn_in", "w_ffn_out", "w_ple_gate", "w_ple_proj")
COLUMN_SHARDED = ("w_in", "w_ffn_in", "w_ple_proj")
SMALL = ("norm_mix_g", "ssm_lambda_re", "ssm_lambda_im", "ssm_log_step", "ssm_b_re", "ssm_b_im", "ssm_c_re", "ssm_c_im",
         "ssm_d", "ssm_glu_b", "sgu_ln_g", "sgu_ln_b", "sgu_w", "sgu_b", "out_norm_ssm_g", "out_norm_sgu_g", "norm_ffn_g",
         "norm_ple_g", "b_ple_gate", "final_norm_g")
WEIGHTS = ("norm_mix_g", "w_in", "ssm_lambda_re", "ssm_lambda_im", "ssm_log_step", "ssm_b_re", "ssm_b_im", "ssm_c_re",
           "ssm_c_im", "ssm_d", "ssm_glu_w", "ssm_glu_b", "sgu_ln_g", "sgu_ln_b", "sgu_w", "sgu_b", "out_norm_ssm_g",
           "out_norm_sgu_g", "w_out", "norm_ffn_g", "w_ffn_in", "w_ffn_out", "norm_ple_g", "w_ple_gate", "b_ple_gate",
           "w_ple_proj", "final_norm_g")
PACK_ROWS = SUBLANES * LANES


def _pack(arrays):
    parts = []
    for a in arrays:
        flat = a.reshape(-1).astype(F32)
        pad = -flat.shape[0] % PACK_ROWS
        parts.append(jnp.pad(flat, (0, pad)) if pad else flat)
    return jnp.concatenate(parts).reshape(-1, LANES)


def _unpack(packed, like):
    flat = packed.reshape(-1)
    out, at = [], 0
    for a in like:
        size = a.size
        out.append(flat[at:at + size].reshape(a.shape))
        at += size + (-size % PACK_ROWS)
    return out


class _NoExchange:
    def __init__(self, weights):
        self.weights, self.grads, self.small = weights, {}, {}

    def weight(self, name):
        return self.weights[name]

    def grad(self, name, g):
        self.grads[name] = g

    def small_grads(self, grads):
        self.small.update(grads)

    def side(self, host):
        return None


class _MeshExchange:
    GATHER = {"norm_mix": (("w_in", 0, 16),),
              "proj_in": (("ssm_glu_w", 0, 16), ("w_out", 0, 16), ("w_ffn_in", 0, 1)),
              "ssm_fwd": (("w_ffn_in", 1, 13),),
              "proj_out": (("w_ffn_in", 13, 16),),
              "ffn_in": (("w_ffn_out", 0, 16), ("w_ple_gate", 0, 16), ("w_ple_proj", 0, 16))}
    GATHER_LONG = ("norm_mix", "proj_in", "ssm_fwd", "proj_out")
    SWAP = {"d_act": ("w_ple_proj", "w_ple_gate", "w_ffn_out"), "d_h2": ("w_ffn_in",), "d_ya0": ("w_out", "ssm_glu_w")}
    SWAP_ALONE = ("w_in",)
    SCATTER = {"d_ffn_in": ("w_ple_proj", "w_ple_gate", "w_ffn_out"), "ssm_bwd": ("w_ffn_in",),
               "d_sgu": ("w_out", "ssm_glu_w"), "d_h1": ("w_in",)}
    SMALL_GATHER = "d_proj_in"

    def __init__(self, shards, small_like, c, shard, me):
        self.c, self.shard, self.me, self.small_like = c, shard, me, small_like
        self.slots = {k: _cast_into_slot("cast_" + k, shards[k], shard) for k in LARGE}
        self.full, self.received, self.halves, self.quarters, self.small = {}, {}, {}, {}, {}

    def weight(self, name):
        g = self.slots[name]
        _, _, rh, cols = g.shape
        return g.reshape(N_CHIPS, 2 * rh, cols) if name in COLUMN_SHARDED else g.reshape(N_CHIPS * 2 * rh, cols)

    def grad(self, name, g):
        if name not in COLUMN_SHARDED:
            g = g.reshape(N_CHIPS, g.shape[0] // N_CHIPS, g.shape[1])
        self.full[name] = g.reshape(N_CHIPS, 2, g.shape[1] // 2, g.shape[2])
        if name in self.SWAP_ALONE:
            self._swapped((name,), _exchange_alone("grad_swap_" + name, _swap_side([self.full[name]])))

    def _swapped(self, names, received):
        for k, r in zip(names, received):
            self.received[k] = r
            self.halves[k] = _add_halves("grad_add_halves_" + k, self.full[k], self.c, self.shard, r)

    def small_grads(self, grads):
        self.small.update(grads)

    def _packed(self, names):
        return _pack([self.small[k].reshape(self.small_like[k].shape) for k in names])

    def side(self, host):
        if host in self.GATHER:
            return _gather_side([self.slots[k] for k, _, _ in self.GATHER[host]], [(lo, hi) for _, lo, hi in self.GATHER[host]],
                                mid_late=host in self.GATHER_LONG)
        if host in self.SWAP:
            return _swap_side([self.full[k] for k in self.SWAP[host]])
        if host in self.SCATTER:
            return _scatter_side([self.halves[k] for k in self.SCATTER[host]])
        if host == self.SMALL_GATHER:
            self.packed_early = self._packed(SMALL[1:] + ("loss",))
            return _small_gather_side(self.packed_early)
        return None

    def done(self, host, moved):
        if host in self.GATHER:
            self.slots.update(zip([k for k, _, _ in self.GATHER[host]], moved))
        elif host in self.SWAP:
            self._swapped(self.SWAP[host], moved)
        elif host in self.SCATTER:
            self.quarters.update(zip(self.SCATTER[host], moved))
        else:
            (self.gathered_early,) = moved

    def small_reduced(self):
        early = _sum_slots("small_sum", self.packed_early, self.gathered_early, self.me)
        late = _allreduce_small(self._packed(SMALL[:1]))
        loss_at = early.shape[0] - PACK_ROWS // LANES
        return jnp.concatenate([late, early[:loss_at]], axis=0), early[loss_at, 0]

    def summed(self):
        return [_sum_received("grad_sum_" + k, self.full[k], self.c, self.shard, self.received[k], self.quarters[k]) for k in LARGE]


def kernel(x, p, norm_mix_g, w_in, ssm_lambda_re, ssm_lambda_im, ssm_log_step, ssm_b_re, ssm_b_im, ssm_c_re, ssm_c_im, ssm_d, ssm_glu_w, ssm_glu_b, sgu_ln_g, sgu_ln_b, sgu_w, sgu_b, out_norm_ssm_g, out_norm_sgu_g, w_out, norm_ffn_g, w_ffn_in, w_ffn_out, norm_ple_g, w_ple_gate, b_ple_gate, w_ple_proj, final_norm_g, loss_target, m_norm_mix_g, m_w_in, m_ssm_lambda_re, m_ssm_lambda_im, m_ssm_log_step, m_ssm_b_re, m_ssm_b_im, m_ssm_c_re, m_ssm_c_im, m_ssm_d, m_ssm_glu_w, m_ssm_glu_b, m_sgu_ln_g, m_sgu_ln_b, m_sgu_w, m_sgu_b, m_out_norm_ssm_g, m_out_norm_sgu_g, m_w_out, m_norm_ffn_g, m_w_ffn_in, m_w_ffn_out, m_norm_ple_g, m_w_ple_gate, m_b_ple_gate, m_w_ple_proj, m_final_norm_g, v_norm_mix_g, v_w_in, v_ssm_lambda_re, v_ssm_lambda_im, v_ssm_log_step, v_ssm_b_re, v_ssm_b_im, v_ssm_c_re, v_ssm_c_im, v_ssm_d, v_ssm_glu_w, v_ssm_glu_b, v_sgu_ln_g, v_sgu_ln_b, v_sgu_w, v_sgu_b, v_out_norm_ssm_g, v_out_norm_sgu_g, v_w_out, v_norm_ffn_g, v_w_ffn_in, v_w_ffn_out, v_norm_ple_g, v_w_ple_gate, v_b_ple_gate, v_w_ple_proj, v_final_norm_g):
    given = dict(locals())
    w = {k: given[k] for k in WEIGHTS}
    m = {k: given["m_" + k] for k in WEIGHTS}
    v = {k: given["v_" + k] for k in WEIGHTS}
    c = lax.axis_index("c")
    shard = 2 * lax.axis_index("x") + lax.axis_index("y")

    small_like = {k: w[k] for k in SMALL}
    small_like["loss"] = _sds((1, LANES), F32)
    exch = _MeshExchange({k: w[k].reshape(w[k].shape[1:]) for k in LARGE}, small_like, c, shard, 2 * shard + c)
    unlayer = lambda a: a if a.ndim == 1 else a[0]
    sp = {k: unlayer(w[k]) for k in SMALL}
    n_tok, d_model = x.shape[1:]
    _, grad_x = _local_grads(x.reshape(n_tok, d_model), p.reshape(n_tok, p.shape[-1]),
                             loss_target.reshape(n_tok, d_model), sp, exch)

    grad_w, delta_w, new_m, new_v = {}, {}, {}, {}
    halves = exch.summed()
    packed_g, loss = exch.small_reduced()
    like = _sds(packed_g.shape, F32)
    (d_s, m_s, v_s), joined = _rowwise(
        "adamw_small", _adamw, [_pack([w[k] for k in SMALL]), packed_g, _pack([m[k] for k in SMALL]), _pack([v[k] for k in SMALL])],
        [], [like, like, like], side=_join_side(halves))
    shapes = [w[k] for k in SMALL]
    for k, g_k, d_k, m_k, v_k in zip(SMALL, _unpack(packed_g, shapes), _unpack(d_s, shapes), _unpack(m_s, shapes), _unpack(v_s, shapes)):
        grad_w[k], delta_w[k], new_m[k], new_v[k] = g_k, d_k, m_k, v_k

    reduced = {k: j.reshape(2 * j.shape[1], j.shape[2]) for k, j in zip(LARGE, joined)}
    for k in LARGE:
        shape = w[k].shape
        two_d = lambda a: a.reshape(shape[1:])
        like = _sds(shape[1:], F32)
        update = lambda w_t, g_t, m_t, v_t: (g_t, *_adamw(w_t, g_t, m_t, v_t))
        outs = _rowwise("adamw_" + k, update, [two_d(w[k]), reduced[k], two_d(m[k]), two_d(v[k])], [], [like, like, like, like])
        grad_w[k], delta_w[k], new_m[k], new_v[k] = (a.reshape(shape) for a in outs)

    return (loss, grad_x.reshape(x.shape), *[grad_w[k] for k in WEIGHTS], *[delta_w[k] for k in WEIGHTS],
            *[new_m[k] for k in WEIGHTS], *[new_v[k] for k in WEIGHTS])
```

```python
import functools

import jax
import jax.numpy as jnp
from jax import lax
from jax.experimental import pallas as pl
from jax.experimental.pallas import tpu as pltpu

F32 = jnp.float32
BF16 = jnp.bfloat16

EPS = 1e-6
LAMBDA_RE_MAX = -1e-4
ADAM_LR = 0.001
ADAM_B1 = 0.9
ADAM_B2 = 0.999
ADAM_EPS = 1e-08
ADAM_WD = 0.01
ADAM_STEP = 10

N_CHIPS = 4
N_DEV = 8
SUBLANES = 8
LANES = 128
SSM_CH_BLOCK = 256
SCAN_LANES = 256
SCAN_BLOCKS = 4
GATHER_PARTS = 16
VMEM_LIMIT = 56 * 1024 * 1024

MESH = pl.DeviceIdType.MESH


def _pick(n, pref, mult):
    if n <= pref:
        return n
    t = (pref // mult) * mult
    while t >= mult:
        if n % t == 0:
            return t
        t -= mult
    return n


def _params(semantics):
    return pltpu.CompilerParams(dimension_semantics=semantics, vmem_limit_bytes=VMEM_LIMIT)


class _Cols:
    def __init__(self, arr, width, blk):
        self.arr, self.width, self.blk = arr, width, blk


def _sds(shape, dtype):
    return jax.ShapeDtypeStruct(tuple(shape), dtype)


ANY = pl.BlockSpec(memory_space=pl.ANY)


class _Side:
    def __init__(self, ins, out_shapes, n_sems, first, last, mid=None, aliases=None, mid_late=False):
        self.ins, self.out_shapes, self.n_sems = list(ins), list(out_shapes), n_sems
        self.first, self.mid, self.last, self.mid_late = first, mid, last, mid_late
        self.aliases = dict(aliases or {})


def _call(body, side, operands, *, name, grid, in_specs, out_specs, out_shape, compiler_params, scratch_shapes=()):
    if side is None:
        return pl.pallas_call(body, name=name, grid=grid, in_specs=in_specs, out_specs=out_specs, out_shape=out_shape,
                              scratch_shapes=list(scratch_shapes), compiler_params=compiler_params)(*operands)
    single = not isinstance(out_specs, (list, tuple))
    out_specs = [out_specs] if single else list(out_specs)
    out_shape = [out_shape] if single else list(out_shape)
    n_in, n_out, n_scr = len(in_specs), len(out_specs), len(scratch_shapes)
    n_sin, n_sout = len(side.ins), len(side.out_shapes)
    steps = 1
    for g in grid:
        steps *= g

    def hosted(*refs):
        ins, s_ins = refs[:n_in], refs[n_in:n_in + n_sin]
        at = n_in + n_sin
        outs, s_outs = refs[at:at + n_out], refs[at + n_out:at + n_out + n_sout]
        scratch = refs[at + n_out + n_sout:at + n_out + n_sout + n_scr]
        sems = refs[-2:]
        step = pl.program_id(0)
        for d in range(1, len(grid)):
            step = step * grid[d] + pl.program_id(d)

        @pl.when(step == 0)
        def _():
            side.first(s_ins, s_outs, *sems)

        if side.mid is not None:
            @pl.when(step == (steps - 1 if side.mid_late else (3 * steps) // 4))
            def _():
                side.mid(s_ins, s_outs, *sems)

        body(*ins, *outs, *scratch)

        @pl.when(step == steps - 1)
        def _():
            side.last(s_ins, s_outs, *sems)

    res = pl.pallas_call(
        hosted, name=name, grid=grid, in_specs=[*in_specs, *[ANY] * n_sin], out_specs=[*out_specs, *[ANY] * n_sout],
        out_shape=[*out_shape, *side.out_shapes], input_output_aliases={n_in + i: n_out + o for i, o in side.aliases.items()},
        scratch_shapes=[*scratch_shapes, pltpu.SemaphoreType.DMA((side.n_sems,)), pltpu.SemaphoreType.DMA((side.n_sems,))],
        compiler_params=compiler_params)(*operands, *side.ins)
    return (res[0] if single else list(res[:n_out])), list(res[n_out:])


def _rowwise(name, fn, rows, params, row_outs, acc_outs=(), tr=256, side=None):
    rows = [r if isinstance(r, _Cols) else _Cols(r, r.shape[1], 0) for r in rows]
    m = rows[0].arr.shape[0]
    tr = _pick(m, tr, 16)
    n_in = len(rows) + len(params)
    n_ro = len(row_outs)

    def body(*refs):
        vals = fn(*[r[...] for r in refs[:n_in]])
        if not isinstance(vals, (tuple, list)):
            vals = (vals,)
        outs = refs[n_in:]
        for r, v in zip(outs[:n_ro], vals[:n_ro]):
            r[...] = v.astype(r.dtype)
        first = pl.program_id(0) == 0
        for r, v in zip(outs[n_ro:], vals[n_ro:]):
            @pl.when(first)
            def _():
                r[...] = jnp.zeros(r.shape, r.dtype)
            r[...] += v.astype(r.dtype).reshape(r.shape)

    in_specs = [pl.BlockSpec((tr, r.width), lambda i, b=r.blk: (i, b)) for r in rows]
    in_specs += [pl.BlockSpec(p.shape, lambda i, nd=p.ndim: (0,) * nd) for p in params]
    out_specs = [pl.BlockSpec((tr, o.shape[1]), lambda i: (i, 0)) for o in row_outs]
    out_specs += [pl.BlockSpec(o.shape, lambda i, nd=len(o.shape): (0,) * nd) for o in acc_outs]
    return _call(body, side, [*[r.arr for r in rows], *params], name=name, grid=(m // tr,), in_specs=in_specs,
                 out_specs=out_specs, out_shape=[*row_outs, *acc_outs], compiler_params=_params(("arbitrary",)))


def _grid_order(swap):
    if not swap:
        return (lambda grid: grid), (lambda f: f)
    return (lambda grid: grid[::-1]), (lambda f: (lambda j, i: f(i, j)))


def _mm_nn(name, a, w, *, sharded=False, res=None, out_dtype=F32, tm=512, tn=512, w_resident=False, side=None):
    m, k = a.shape
    tm = _pick(m, tm, 16)
    order, ix = _grid_order(w_resident)
    if sharded:
        s, _, ns = w.shape
        n = s * ns
        tn = _pick(ns, tn, LANES)
        per = ns // tn
        w_spec = pl.BlockSpec((None, k, tn), ix(lambda i, j: (j // per, 0, j % per)))
    else:
        n = w.shape[1]
        tn = _pick(n, tn, LANES)
        w_spec = pl.BlockSpec((k, tn), ix(lambda i, j: (0, j)))

    def body(a_ref, w_ref, *rest):
        acc = jnp.dot(a_ref[...], w_ref[...], preferred_element_type=F32)
        if res is not None:
            acc = acc + rest[0][...]
        rest[-1][...] = acc.astype(out_dtype)

    in_specs = [pl.BlockSpec((tm, k), ix(lambda i, j: (i, 0))), w_spec]
    ops = [a, w]
    if res is not None:
        in_specs.append(pl.BlockSpec((tm, tn), ix(lambda i, j: (i, j))))
        ops.append(res)
    return _call(body, side, ops, name=name, grid=order((m // tm, n // tn)), in_specs=in_specs,
                 out_specs=pl.BlockSpec((tm, tn), ix(lambda i, j: (i, j))), out_shape=_sds((m, n), out_dtype),
                 compiler_params=_params(("arbitrary", "arbitrary")))


def _mm_nt(name, g, w, *, sharded=False, g_halves=False, out_dtype=F32, tm=512, tk=512, w_resident=False, side=None):
    m, n = g.shape[-2:]
    tm = _pick(m, tm, 16)
    order, ix = _grid_order(w_resident)
    dims = (((1,), (1,)), ((), ()))
    g_spec = pl.BlockSpec((2, tm, n), ix(lambda i, j: (0, i, 0))) if g_halves else pl.BlockSpec((tm, n), ix(lambda i, j: (i, 0)))
    if sharded:
        s, k, ns = w.shape
        tk = _pick(k, tk, LANES)
        w_spec = pl.BlockSpec((s, tk, ns), ix(lambda i, j: (0, j, 0)))

        def columns(g_ref, q):
            if not g_halves:
                return g_ref[:, q * ns:(q + 1) * ns]
            half, at = divmod(q, s // 2)
            return g_ref[half, :, at * ns:(at + 1) * ns]

        def body(g_ref, w_ref, o_ref):
            acc = lax.dot_general(columns(g_ref, 0), w_ref[0], dims, preferred_element_type=F32)
            for q in range(1, s):
                acc = acc + lax.dot_general(columns(g_ref, q), w_ref[q], dims, preferred_element_type=F32)
            o_ref[...] = acc.astype(out_dtype)
    else:
        k = w.shape[0]
        tk = _pick(k, tk, LANES)
        w_spec = pl.BlockSpec((tk, n), ix(lambda i, j: (j, 0)))

        def body(g_ref, w_ref, o_ref):
            o_ref[...] = lax.dot_general(g_ref[...], w_ref[...], dims, preferred_element_type=F32).astype(out_dtype)

    return _call(body, side, [g, w], name=name, grid=order((m // tm, k // tk)), in_specs=[g_spec, w_spec],
                 out_specs=pl.BlockSpec((tm, tk), ix(lambda i, j: (i, j))), out_shape=_sds((m, k), out_dtype),
                 compiler_params=_params(("arbitrary", "arbitrary")))


def _mm_tn(name, a, g, *, shards=0, g_halves=False, tk=512, tn=512, g_resident=False, side=None):
    m, k = a.shape
    n = 2 * g.shape[2] if g_halves else g.shape[1]
    tk = _pick(k, tk, LANES)
    order, ix = _grid_order(g_resident)
    dims = (((0,), (0,)), ((), ()))
    if shards:
        ns = n // shards
        tn = _pick(ns, tn, LANES)
        per = ns // tn
        out_spec = pl.BlockSpec((None, tk, tn), ix(lambda i, j: (j // per, i, j % per)))
        out_shape = _sds((shards, k, ns), F32)
    else:
        tn = _pick(n, tn, LANES)
        out_spec = pl.BlockSpec((tk, tn), ix(lambda i, j: (i, j)))
        out_shape = _sds((k, n), F32)

    def body(a_ref, g_ref, o_ref):
        o_ref[...] = lax.dot_general(a_ref[...], g_ref[...], dims, preferred_element_type=F32)

    if g_halves:
        per_half = n // 2 // tn
        g_spec = pl.BlockSpec((None, m, tn), ix(lambda i, j: (j // per_half, 0, j % per_half)))
    else:
        g_spec = pl.BlockSpec((m, tn), ix(lambda i, j: (0, j)))
    return _call(body, side, [a, g], name=name, grid=order((k // tk, n // tn)),
                 in_specs=[pl.BlockSpec((m, tk), ix(lambda i, j: (0, i))), g_spec],
                 out_specs=out_spec, out_shape=out_shape, compiler_params=_params(("arbitrary", "arbitrary")))


def _ffn_in_swiglu(name, a, w, *, tm=512, tn=1408, side=None):
    m, k = a.shape
    s, _, ns = w.shape
    f = s * ns // 2
    tm = _pick(m, tm, 16)
    tn = _pick(ns, tn, LANES)
    per = ns // tn
    order, ix = _grid_order(True)

    def body(a_ref, wg_ref, wu_ref, act_ref, gu_ref):
        x = a_ref[...]
        gate = jnp.dot(x, wg_ref[...], preferred_element_type=F32)
        up = jnp.dot(x, wu_ref[...], preferred_element_type=F32)
        act_ref[...] = _swiglu(gate, up).astype(BF16)
        gu_ref[0] = gate.astype(BF16)
        gu_ref[1] = up.astype(BF16)

    return _call(body, side, [a, w, w], name=name, grid=order((m // tm, f // tn)),
                 in_specs=[pl.BlockSpec((tm, k), ix(lambda i, j: (i, 0))),
                           pl.BlockSpec((None, k, tn), ix(lambda i, j: (j // per, 0, j % per))),
                           pl.BlockSpec((None, k, tn), ix(lambda i, j: (s // 2 + j // per, 0, j % per)))],
                 out_specs=[pl.BlockSpec((tm, tn), ix(lambda i, j: (i, j))), pl.BlockSpec((2, tm, tn), ix(lambda i, j: (0, i, j)))],
                 out_shape=[_sds((m, f), BF16), _sds((2, m, f), BF16)], compiler_params=_params(("arbitrary", "arbitrary")))


def _d_act_swiglu(name, g, w, gu, *, tm=1024, tk=512, side=None):
    m, n = g.shape
    f = w.shape[0]
    tm = _pick(m, tm, 16)
    tk = _pick(f, tk, LANES)
    dims = (((1,), (1,)), ((), ()))

    def body(g_ref, w_ref, gu_ref, o_ref):
        dact = lax.dot_general(g_ref[...], w_ref[...], dims, preferred_element_type=F32)
        _, vjp = jax.vjp(_swiglu, gu_ref[0].astype(F32), gu_ref[1].astype(F32))
        dgate, dup = vjp(dact)
        o_ref[0] = dgate.astype(BF16)
        o_ref[1] = dup.astype(BF16)

    return _call(body, side, [g, w, gu], name=name, grid=(m // tm, f // tk),
                 in_specs=[pl.BlockSpec((tm, n), lambda i, j: (i, 0)), pl.BlockSpec((tk, n), lambda i, j: (j, 0)),
                           pl.BlockSpec((2, tm, tk), lambda i, j: (0, i, j))],
                 out_specs=pl.BlockSpec((2, tm, tk), lambda i, j: (0, i, j)), out_shape=_sds((2, m, f), BF16),
                 compiler_params=_params(("arbitrary", "arbitrary")))


def _rms(x, g):
    r = lax.rsqrt(jnp.mean(x * x, axis=-1, keepdims=True) + EPS)
    return (x * r) * g


def _glu_out(y_pre, q, glu_b, g_norm):
    ya0 = jax.nn.gelu(y_pre)
    return _rms(ya0 * jax.nn.sigmoid(q + glu_b), g_norm)


def _sgu_rows(zu, zv, ln_g, ln_b, w_s, b_st, g_norm):
    heads, t, _ = w_s.shape
    hd = zu.shape[1] // heads
    uu = jax.nn.gelu(zu)
    vv = jax.nn.gelu(zv)
    mu = jnp.mean(vv, axis=-1, keepdims=True)
    xc = vv - mu
    r = lax.rsqrt(jnp.mean(xc * xc, axis=-1, keepdims=True) + EPS)
    vn = (xc * r) * ln_g + ln_b
    row = lax.broadcasted_iota(jnp.int32, (t, t), 0)
    col = lax.broadcasted_iota(jnp.int32, (t, t), 1)
    causal = row >= col
    chunks = []
    for n in range(zu.shape[0] // t):
        blocks = []
        for h in range(heads):
            wm = jnp.where(causal, w_s[h], jnp.zeros_like(w_s[h])).astype(BF16)
            vb = vn[n * t:(n + 1) * t, h * hd:(h + 1) * hd].astype(BF16)
            blocks.append(jnp.dot(wm, vb, preferred_element_type=F32) + b_st[:, h:h + 1])
        chunks.append(jnp.concatenate(blocks, axis=1))
    s = jnp.concatenate(chunks, axis=0) if len(chunks) > 1 else chunks[0]
    return _rms(uu * s, g_norm)


def _swiglu(gate, up):
    return jax.nn.silu(gate) * up


def _head_loss(x2, gpre, pp, b_g, g_final, target):
    gate = jax.nn.sigmoid(gpre + b_g)
    out = _rms(x2 + gate * pp, g_final)
    err = jnp.square(out - target)
    return 0.5 * jnp.sum(jnp.mean(err, axis=-1))


def _ssm_disc(lam_re, lam_im, log_step):
    lr = jnp.minimum(lam_re, LAMBDA_RE_MAX)
    li = lam_im
    dt = jnp.exp(log_step)
    mag = jnp.exp(lr * dt)
    ang = li * dt
    abar_re = mag * jnp.cos(ang)
    abar_im = mag * jnp.sin(ang)
    nr = abar_re - 1.0
    ni = abar_im
    den = lr * lr + li * li
    q_re = (nr * lr + ni * li) / den
    q_im = (ni * lr - nr * li) / den
    return abar_re, abar_im, q_re, q_im


def _ssm_bbar(q_re, q_im, b_re, b_im):
    return q_re * b_re - q_im * b_im, q_re * b_im + q_im * b_re


def _ssm_discretised(lam_re, lam_im, log_step, bt_re, bt_im):
    ar, ai, qr, qi = _ssm_disc(lam_re, lam_im, log_step)
    return (ar, ai, *_ssm_bbar(qr, qi, bt_re, bt_im))


def _adamw(w, g, m, v):
    m = ADAM_B1 * m + (1.0 - ADAM_B1) * g
    v = ADAM_B2 * v + (1.0 - ADAM_B2) * jnp.square(g)
    m_hat = m / (1.0 - ADAM_B1 ** ADAM_STEP)
    v_hat = v / (1.0 - ADAM_B2 ** ADAM_STEP)
    delta = -ADAM_LR * (m_hat / (jnp.sqrt(v_hat) + ADAM_EPS) + ADAM_WD * w)
    return delta, m, v


class _SsmDims:
    def __init__(self, groups, state, gch):
        self.g, self.p, self.h = groups, state, gch
        self.d = groups * gch
        self.cb = min(SSM_CH_BLOCK, self.d)
        self.gb = self.cb // gch
        self.ns = self.gb * state
        self.nb = self.d // self.cb


def _ssm_rows(sd, sp):
    gp = sd.g * sd.p
    log_step = jnp.broadcast_to(sp["ssm_log_step"][:, None], (sd.g, sd.p)).reshape(1, gp)
    bt = [sp[k].reshape(gp, sd.h).T for k in ("ssm_b_re", "ssm_b_im")]
    ct = [sp[k].transpose(1, 0, 2).reshape(sd.h, gp) for k in ("ssm_c_re", "ssm_c_im")]
    return (sp["ssm_lambda_re"].reshape(1, gp), sp["ssm_lambda_im"].reshape(1, gp), log_step, *bt, *ct)


def _block_mask(sd):
    row = lax.broadcasted_iota(jnp.int32, (sd.cb, sd.ns), 0) // sd.h
    col = lax.broadcasted_iota(jnp.int32, (sd.cb, sd.ns), 1) // sd.p
    return row == col


def _scan_consts(pr, pi_, reverse):
    if reverse:
        pi_ = [-v for v in pi_]
    shape = (SUBLANES, pr[0].shape[1])
    rows = lax.broadcasted_iota(jnp.int32, shape, 0)
    parts = []
    for d in (1, 2, 4):
        keep = (rows < SUBLANES - d) if reverse else (rows >= d)
        parts += [jnp.where(keep, jnp.broadcast_to(v[d - 1], shape), 0.0) for v in (pr, pi_)]
    order = range(SUBLANES - 1, -1, -1) if reverse else range(SUBLANES)
    parts += [jnp.concatenate([v[t] for t in order], axis=0) for v in (pr, pi_)]
    return jnp.concatenate(parts, axis=0)


def _ssm_operands(sd, rows):
    cb, ns, nb = sd.cb, sd.ns, sd.nb

    def body(lam_re, lam_im, log_step, bt_re, bt_im, ct_re, ct_im, wb_ref, wbt_ref, wc_ref, wct_ref, cst_f_ref, cst_r_ref):
        ar, ai, bbar_re, bbar_im = _ssm_discretised(lam_re[...], lam_im[...], log_step[...], bt_re[...], bt_im[...])
        pr, pi_ = [ar], [ai]
        for _ in range(SUBLANES - 1):
            pr, pi_ = pr + [pr[-1] * ar - pi_[-1] * ai], pi_ + [pr[-1] * ai + pi_[-1] * ar]
        mask = _block_mask(sd)
        spread = lambda src: jnp.where(mask, jnp.concatenate([src] * sd.gb, axis=0), 0.0)
        for j in range(nb):
            at = slice(j * ns, (j + 1) * ns)
            w = jnp.concatenate([spread(bbar_re[:, at]), spread(bbar_im[:, at])], axis=1)
            v = jnp.concatenate([spread(ct_re[:, at]), -spread(ct_im[:, at])], axis=1)
            wb_ref[j] = w.astype(BF16)
            wbt_ref[j] = w.T.astype(BF16)
            wct_ref[j] = v.astype(BF16)
            wc_ref[j] = v.T.astype(BF16)
            pj, qj = [u[:, at] for u in pr], [u[:, at] for u in pi_]
            cst_f_ref[j] = _scan_consts(pj, qj, False)
            cst_r_ref[j] = _scan_consts(pj, qj, True)

    wide, tall = _sds((nb, cb, 2 * ns), BF16), _sds((nb, 2 * ns, cb), BF16)
    cst = _sds((nb, 8 * SUBLANES, ns), F32)
    vm = pl.BlockSpec(memory_space=pltpu.VMEM)
    return pl.pallas_call(body, name="ssm_operands", in_specs=[vm] * 7, out_specs=[vm] * 6,
                          out_shape=[wide, tall, tall, wide, cst, cst],
                          compiler_params=pltpu.CompilerParams(vmem_limit_bytes=VMEM_LIMIT))(*rows)


def _ssm_param_grads(sd, rows, dwb, dwc, da):
    ns, nb, gp = sd.ns, sd.nb, sd.g * sd.p

    def body(lam_re, lam_im, log_step, bt_re, bt_im, dwb_v, dwc_v, da_v, *outs):
        mask = _block_mask(sd)

        def fold(dense):
            kept = jnp.where(mask, dense, 0.0)
            acc = kept[0:sd.h]
            for gl in range(1, sd.gb):
                acc = acc + kept[gl * sd.h:(gl + 1) * sd.h]
            return acc

        lanes = lambda parts: jnp.concatenate(parts, axis=1) if len(parts) > 1 else parts[0]
        dbbar_re = lanes([fold(dwb_v[j][:, :ns]) for j in range(nb)])
        dbbar_im = lanes([fold(dwb_v[j][:, ns:]) for j in range(nb)])
        dwct = [dwc_v[j] for j in range(nb)]
        d_ct_re = lanes([fold(t[:, :ns]) for t in dwct])
        d_ct_im = -lanes([fold(t[:, ns:]) for t in dwct])
        dabar_re = lanes([da_v[j][0:1, :ns] for j in range(nb)])
        dabar_im = lanes([da_v[j][0:1, ns:] for j in range(nb)])
        _, vjp = jax.vjp(_ssm_discretised, lam_re[...], lam_im[...], log_step[...], bt_re[...], bt_im[...])
        d_lr, d_li, d_ls, d_bt_re, d_bt_im = vjp((dabar_re, dabar_im, dbbar_re, dbbar_im))
        group = (lax.broadcasted_iota(jnp.int32, (gp, sd.g), 0) // sd.p == lax.broadcasted_iota(jnp.int32, (gp, sd.g), 1))
        d_log_step = jnp.dot(d_ls, group.astype(F32), precision=lax.Precision.HIGHEST, preferred_element_type=F32)
        for ref, val in zip(outs, (d_lr, d_li, d_log_step, d_bt_re, d_bt_im, d_ct_re, d_ct_im)):
            ref[...] = val

    row, mat = _sds((1, gp), F32), _sds((sd.h, gp), F32)
    vm = pl.BlockSpec(memory_space=pltpu.VMEM)
    return pl.pallas_call(body, name="ssm_param_grads", in_specs=[vm] * 8, out_specs=[vm] * 7,
                          out_shape=[row, row, _sds((1, sd.g), F32), mat, mat, mat, mat],
                          compiler_params=pltpu.CompilerParams(vmem_limit_bytes=VMEM_LIMIT))(*rows[:5], dwb, dwc, da)


def _block_scan(s_ref, cst_ref, carry_ref, sd, rows, reverse):
    ns = sd.ns
    nblk = rows // SUBLANES
    w = min(SCAN_LANES, ns)
    for c0 in range(0, ns, w):
        re_l, im_l = slice(c0, c0 + w), slice(ns + c0, ns + c0 + w)
        cst = [cst_ref[k * SUBLANES:(k + 1) * SUBLANES, c0:c0 + w] for k in range(8)]

        def step(k, carry, re_l=re_l, im_l=im_l, cst=cst):
            local = []
            for b in range(SCAN_BLOCKS):
                blk = SCAN_BLOCKS * k + b
                blk = (nblk - 1 - blk) if reverse else blk
                r0 = pl.multiple_of(blk * SUBLANES, SUBLANES)
                xr = s_ref[pl.ds(r0, SUBLANES), re_l]
                xi = s_ref[pl.ds(r0, SUBLANES), im_l]
                for n, d in enumerate((1, 2, 4)):
                    ar, ai = cst[2 * n], cst[2 * n + 1]
                    shift = (SUBLANES - d) if reverse else d
                    sr = pltpu.roll(xr, shift, 0)
                    si = pltpu.roll(xi, shift, 0)
                    xr, xi = xr + ar * sr - ai * si, xi + ar * si + ai * sr
                local.append((r0, xr, xi))
            cr, ci = carry
            edge = slice(0, 1) if reverse else slice(SUBLANES - 1, SUBLANES)
            for r0, xr, xi in local:
                br = jnp.broadcast_to(cr, xr.shape)
                bi = jnp.broadcast_to(ci, xi.shape)
                xr, xi = xr + cst[6] * br - cst[7] * bi, xi + cst[6] * bi + cst[7] * br
                s_ref[pl.ds(r0, SUBLANES), re_l] = xr
                s_ref[pl.ds(r0, SUBLANES), im_l] = xi
                cr, ci = xr[edge, :], xi[edge, :]
            return cr, ci

        cr, ci = lax.fori_loop(0, nblk // SCAN_BLOCKS, step, (carry_ref[0:1, re_l], carry_ref[0:1, im_l]))
        carry_ref[0:1, re_l] = cr
        carry_ref[0:1, im_l] = ci


def _ssm_fwd(name, sd, z, wb, wc, cst, d_row, tt=512, side=None):
    n_tok = z.shape[0]
    tt = _pick(n_tok, tt, 16)
    cb, ns2 = sd.cb, 2 * sd.ns

    def body(z_ref, wb_ref, wc_ref, cst_ref, d_ref, y_ref, s_ref, a0_ref, carry_ref):
        @pl.when(pl.program_id(1) == 0)
        def _():
            carry_ref[...] = jnp.zeros(carry_ref.shape, F32)
        u = z_ref[...]
        s_ref[...] = jnp.dot(u.astype(BF16), wb_ref[...], preferred_element_type=F32)
        _block_scan(s_ref, cst_ref, carry_ref, sd, tt, reverse=False)
        y = jnp.dot(s_ref[...].astype(BF16), wc_ref[...], preferred_element_type=F32) + d_ref[...] * u
        y_ref[...] = y
        a0_ref[...] = jax.nn.gelu(y).astype(BF16)

    return _call(
        body, side, [z, wb, wc, cst, d_row], name=name, grid=(sd.nb, n_tok // tt),
        in_specs=[pl.BlockSpec((tt, cb), lambda j, i: (i, j)),
                  pl.BlockSpec((None, cb, ns2), lambda j, i: (j, 0, 0)),
                  pl.BlockSpec((None, ns2, cb), lambda j, i: (j, 0, 0)),
                  pl.BlockSpec((None, 8 * SUBLANES, sd.ns), lambda j, i: (j, 0, 0)),
                  pl.BlockSpec((1, cb), lambda j, i: (0, j))],
        out_specs=[pl.BlockSpec((tt, cb), lambda j, i: (i, j)), pl.BlockSpec((tt, ns2), lambda j, i: (i, j)),
                   pl.BlockSpec((tt, cb), lambda j, i: (i, j))],
        out_shape=[_sds((n_tok, sd.d), F32), _sds((n_tok, sd.nb * ns2), F32), _sds((n_tok, sd.d), BF16)],
        scratch_shapes=[pltpu.VMEM((SUBLANES, ns2), F32)],
        compiler_params=_params(("arbitrary", "arbitrary")))


def _ssm_bwd(name, sd, y_pre, dy_direct, dya0, z, states, wct, wbt, cst_rev, d_row, tt=512, side=None):
    n_tok = z.shape[0]
    tt = _pick(n_tok, tt, 16)
    nt = n_tok // tt
    cb, ns, ns2 = sd.cb, sd.ns, 2 * sd.ns
    blocks_per_tile = tt // SUBLANES
    tn_dims = (((0,), (0,)), ((), ()))

    def body(y_ref, dyd_ref, dya0_ref, z_ref, s_ref, sp_ref, wct_ref, wbt_ref, cst_ref, d_ref,
             du_ref, dwb_ref, dwc_ref, da_ref, dd_ref, lam_ref, carry_ref):
        i = pl.program_id(1)

        @pl.when(i == 0)
        def _():
            carry_ref[...] = jnp.zeros(carry_ref.shape, F32)
            dwb_ref[...] = jnp.zeros(dwb_ref.shape, F32)
            dwc_ref[...] = jnp.zeros(dwc_ref.shape, F32)
            da_ref[...] = jnp.zeros(da_ref.shape, F32)
            dd_ref[...] = jnp.zeros(dd_ref.shape, F32)

        _, gelu_vjp = jax.vjp(jax.nn.gelu, y_ref[...])
        dy_t = dyd_ref[...] + gelu_vjp(dya0_ref[...].astype(F32))[0]
        u = z_ref[...]
        dy16 = dy_t.astype(BF16)
        lam_ref[...] = jnp.dot(dy16, wct_ref[...], preferred_element_type=F32)
        _block_scan(lam_ref, cst_ref, carry_ref, sd, tt, reverse=True)
        lam = lam_ref[...]
        lam16 = lam.astype(BF16)
        du_ref[...] = (jnp.dot(lam16, wbt_ref[...], preferred_element_type=F32) + d_ref[...] * dy_t).astype(BF16)
        dd_ref[0:1, :] += jnp.sum(dy_t * u, axis=0, keepdims=True)
        dwb_ref[...] += lax.dot_general(u.astype(BF16), lam16, tn_dims, preferred_element_type=F32)
        s = s_ref[...]
        dwc_ref[...] += lax.dot_general(dy16, s.astype(BF16), tn_dims, preferred_element_type=F32)
        before = jnp.where(i == nt - 1, 0.0, 1.0) * sp_ref[SUBLANES - 1:SUBLANES, :]
        first_row = lax.broadcasted_iota(jnp.int32, s.shape, 0) == 0
        prev = jnp.where(first_row, jnp.broadcast_to(before, s.shape), pltpu.roll(s, 1, 0))
        lr, li = lam[:, :ns], lam[:, ns:]
        pr, pi_ = prev[:, :ns], prev[:, ns:]
        da_ref[0:1, 0:ns] += jnp.sum(lr * pr + li * pi_, axis=0, keepdims=True)
        da_ref[0:1, ns:ns2] += jnp.sum(li * pr - lr * pi_, axis=0, keepdims=True)

    rev = lambda i: nt - 1 - i
    return _call(
        body, side, [y_pre, dy_direct, dya0, z, states, states, wct, wbt, cst_rev, d_row], name=name, grid=(sd.nb, nt),
        in_specs=[pl.BlockSpec((tt, cb), lambda j, i: (rev(i), j)),
                  pl.BlockSpec((tt, cb), lambda j, i: (rev(i), j)),
                  pl.BlockSpec((tt, cb), lambda j, i: (rev(i), j)),
                  pl.BlockSpec((tt, cb), lambda j, i: (rev(i), j)),
                  pl.BlockSpec((tt, ns2), lambda j, i: (rev(i), j)),
                  pl.BlockSpec((SUBLANES, ns2), lambda j, i: (jnp.maximum(rev(i) * blocks_per_tile - 1, 0), j)),
                  pl.BlockSpec((None, cb, ns2), lambda j, i: (j, 0, 0)),
                  pl.BlockSpec((None, ns2, cb), lambda j, i: (j, 0, 0)),
                  pl.BlockSpec((None, 8 * SUBLANES, ns), lambda j, i: (j, 0, 0)),
                  pl.BlockSpec((1, cb), lambda j, i: (0, j))],
        out_specs=[pl.BlockSpec((tt, cb), lambda j, i: (rev(i), j)),
                   pl.BlockSpec((None, cb, ns2), lambda j, i: (j, 0, 0)),
                   pl.BlockSpec((None, cb, ns2), lambda j, i: (j, 0, 0)),
                   pl.BlockSpec((None, SUBLANES, ns2), lambda j, i: (j, 0, 0)),
                   pl.BlockSpec((None, SUBLANES, cb), lambda j, i: (j, 0, 0))],
        out_shape=[_sds((n_tok, sd.d), BF16), _sds((sd.nb, cb, ns2), F32), _sds((sd.nb, cb, ns2), F32),
                   _sds((sd.nb, SUBLANES, ns2), F32), _sds((sd.nb, SUBLANES, cb), F32)],
        scratch_shapes=[pltpu.VMEM((tt, ns2), F32), pltpu.VMEM((SUBLANES, ns2), F32)],
        compiler_params=_params(("arbitrary", "arbitrary")))


def _hosted(exch, fn, name, *args, **kw):
    side = exch.side(name)
    if side is None:
        return fn(name, *args, **kw)
    out, moved = fn(name, *args, side=side, **kw)
    exch.done(name, moved)
    return out


def _local_grads(x, p, target, sp, exch):
    n_tok, d_model = x.shape
    d_ssm = sp["ssm_d"].shape[0] * sp["ssm_d"].shape[1]
    d_sgu = sp["sgu_ln_g"].shape[-1]
    sd = _SsmDims(sp["ssm_b_re"].shape[0], sp["ssm_b_re"].shape[1], sp["ssm_b_re"].shape[2])
    heads, chunk, _ = sp["sgu_w"].shape
    row = lambda v: v.reshape(1, -1)
    tok = lambda w, dt=F32: _sds((n_tok, w), dt)
    acc = lambda w: _sds((1, w), F32)

    g_mix = row(sp["norm_mix_g"])
    (h1,) = _hosted(exch, _rowwise, "norm_mix", lambda a, g: _rms(a, g), [x], [g_mix], [tok(d_model, BF16)])
    z = _hosted(exch, _mm_nn, "proj_in", h1, exch.weight("w_in"), sharded=True, tm=1024, tn=768)

    ssm_rows = _ssm_rows(sd, sp)
    wb, wbt, wc, wct, cst_fwd, cst_rev = _ssm_operands(sd, ssm_rows)
    d_row = row(sp["ssm_d"])
    y_pre, states, ya0_16 = _hosted(exch, _ssm_fwd, "ssm_fwd", sd, z, wb, wc, cst_fwd, d_row)
    q = _mm_nn("ssm_glu", ya0_16, exch.weight("ssm_glu_w"), tm=1024)
    glu_b, g_ossm = row(sp["ssm_glu_b"]), row(sp["out_norm_ssm_g"])
    (ya_n,) = _rowwise("ssm_glu_out", _glu_out, [y_pre, q], [glu_b, g_ossm], [tok(d_ssm, BF16)], tr=512)

    assert d_ssm == d_sgu
    zu, zv = _Cols(z, d_sgu, 1), _Cols(z, d_sgu, 2)
    ln_g, ln_b, g_osgu = row(sp["sgu_ln_g"]), row(sp["sgu_ln_b"]), row(sp["out_norm_sgu_g"])
    b_st = sp["sgu_b"].T
    sgu_tr = 2 * chunk

    def sgu_joined(ya_t, zu_t, zv_t, *params):
        return jnp.concatenate([ya_t, _sgu_rows(zu_t, zv_t, *params).astype(BF16)], axis=1)

    (ycat,) = _rowwise("sgu", sgu_joined, [ya_n, zu, zv], [ln_g, ln_b, sp["sgu_w"], b_st, g_osgu],
                       [tok(d_ssm + d_sgu, BF16)], tr=sgu_tr)
    x1 = _hosted(exch, _mm_nn, "proj_out", ycat, exch.weight("w_out"), res=x, tm=1024)

    g_ffn = row(sp["norm_ffn_g"])
    (h2,) = _rowwise("norm_ffn", lambda a, g: _rms(a, g), [x1], [g_ffn], [tok(d_model, BF16)], tr=512)
    act, gu16 = _hosted(exch, _ffn_in_swiglu, "ffn_in", h2, exch.weight("w_ffn_in"))
    x2 = _mm_nn("ffn_out", act, exch.weight("w_ffn_out"), res=x1, tn=1024)

    g_ple = row(sp["norm_ple_g"])
    (h3,) = _rowwise("norm_ple", lambda a, g: _rms(a, g), [x2], [g_ple], [tok(d_model, BF16)], tr=512)
    gpre = _mm_nn("ple_gate", h3, exch.weight("w_ple_gate"), tm=1024, tn=1024)
    (p16,) = _rowwise("ple_cast", lambda a: a, [p], [], [tok(p.shape[1], BF16)])
    pp = _mm_nn("ple_proj", p16, exch.weight("w_ple_proj"), sharded=True, tm=1024)

    b_g, g_fin = row(sp["b_ple_gate"]), row(sp["final_norm_g"])

    def head(x2_t, gpre_t, pp_t, tgt_t, b_g_v, g_fin_v):
        loss, grads = jax.value_and_grad(_head_loss, argnums=(0, 1, 2, 3, 4))(x2_t, gpre_t, pp_t, b_g_v, g_fin_v, tgt_t)
        dx2, dgpre, dpp, db, dg = grads
        return dx2, dgpre.astype(BF16), dpp.astype(BF16), jnp.full((1, LANES), loss, F32), db, dg

    dx2_head, dgpre16, dpp16, loss_row, d_b_g, d_g_fin = _rowwise(
        "head", head, [x2, gpre, pp, target], [b_g, g_fin],
        [tok(d_model), tok(d_model, BF16), tok(d_model, BF16)], [acc(LANES), acc(d_model), acc(d_model)])
    loss = loss_row[0, 0]
    exch.small_grads({"loss": loss_row})

    exch.grad("w_ple_proj", _mm_tn("d_ple_proj", p16, dpp16, shards=N_CHIPS, tk=256))
    exch.grad("w_ple_gate", _mm_tn("d_ple_gate", h3, dgpre16, tn=1024))
    dh3 = _mm_nt("d_h3", dgpre16, exch.weight("w_ple_gate"), out_dtype=BF16, tm=1024, tk=1024)

    def norm_bwd(x_t, dres_t, dh_t, g_v):
        _, vjp = jax.vjp(_rms, x_t, g_v)
        dx, dg = vjp(dh_t.astype(F32))
        dx = dres_t + dx
        return dx, dx.astype(BF16), dg

    dx2, dx2_16, d_g_ple = _rowwise("d_norm_ple", norm_bwd, [x2, dx2_head, dh3], [g_ple],
                                    [tok(d_model), tok(d_model, BF16)], [acc(d_model)], tr=512)
    exch.grad("w_ffn_out", _mm_tn("d_ffn_out", act, dx2_16, tn=1024))
    dgu16 = _hosted(exch, _d_act_swiglu, "d_act", dx2_16, exch.weight("w_ffn_out"), gu16, tm=2048)
    exch.grad("w_ffn_in", _hosted(exch, _mm_tn, "d_ffn_in", h2, dgu16, shards=N_CHIPS, g_halves=True, tn=1408, g_resident=True))
    dh2 = _hosted(exch, _mm_nt, "d_h2", dgu16, exch.weight("w_ffn_in"), sharded=True, g_halves=True, out_dtype=BF16, tm=512, w_resident=True)
    dx1, dx1_16, d_g_ffn = _rowwise("d_norm_ffn", norm_bwd, [x1, dx2, dh2], [g_ffn],
                                    [tok(d_model), tok(d_model, BF16)], [acc(d_model)], tr=512)
    exch.grad("w_out", _mm_tn("d_proj_out", ycat, dx1_16, tn=1024))
    dycat = _mm_nt("d_ycat", dx1_16, exch.weight("w_out"), out_dtype=BF16, tm=1024, tk=1024)

    def glu_out_bwd(y_pre_t, q_t, dy_t, glu_b_v, g_v):
        _, vjp = jax.vjp(_glu_out, y_pre_t, q_t, glu_b_v, g_v)
        dy_pre, dq, db, dg = vjp(dy_t.astype(F32))
        return dy_pre, dq.astype(BF16), db, dg

    dy_pre_a, dq16, d_glu_b, d_g_ossm = _rowwise(
        "d_ssm_glu_out", glu_out_bwd, [y_pre, q, _Cols(dycat, d_ssm, 0)], [glu_b, g_ossm],
        [tok(d_ssm), tok(d_ssm, BF16)], [acc(d_ssm), acc(d_ssm)], tr=512)
    exch.grad("ssm_glu_w", _mm_tn("d_ssm_glu", ya0_16, dq16))
    dya0 = _hosted(exch, _mm_nt, "d_ya0", dq16, exch.weight("ssm_glu_w"), out_dtype=BF16, tm=1024)

    dz_ssm16, dwb, dwc, da, dd = _hosted(exch, _ssm_bwd, "ssm_bwd", sd, y_pre, dy_pre_a, dya0, z, states, wct, wbt,
                                         cst_rev, d_row)

    def sgu_bwd(dz_ssm_t, zu_t, zv_t, dy_t, ln_g_v, ln_b_v, w_v, b_v, g_v):
        _, vjp = jax.vjp(_sgu_rows, zu_t, zv_t, ln_g_v, ln_b_v, w_v, b_v, g_v)
        dzu, dzv, dlg, dlb, dw, db, dg = vjp(dy_t.astype(F32))
        return jnp.concatenate([dz_ssm_t, dzu.astype(BF16), dzv.astype(BF16)], axis=1), dlg, dlb, dw, db, dg

    dz16, d_ln_g, d_ln_b, d_sgu_w, d_b_st, d_g_osgu = _hosted(
        exch, _rowwise, "d_sgu", sgu_bwd, [dz_ssm16, zu, zv, _Cols(dycat, d_sgu, 1)], [ln_g, ln_b, sp["sgu_w"], b_st, g_osgu],
        [tok(d_ssm + 2 * d_sgu, BF16)],
        [acc(d_sgu), acc(d_sgu), _sds(sp["sgu_w"].shape, F32), _sds(b_st.shape, F32), acc(d_sgu)], tr=sgu_tr)

    d_lam_re, d_lam_im, d_log_step, d_bt_re, d_bt_im, d_ct_re, d_ct_im = _ssm_param_grads(sd, ssm_rows, dwb, dwc, da)
    d_b_re, d_b_im = d_bt_re.T, d_bt_im.T
    d_c_re, d_c_im = (t.reshape(sd.h, sd.g, sd.p).transpose(1, 0, 2) for t in (d_ct_re, d_ct_im))
    d_ssm_d = dd[:, 0, :].reshape(sd.g, sd.h)

    exch.small_grads({
        "ssm_lambda_re": d_lam_re, "ssm_lambda_im": d_lam_im, "ssm_log_step": d_log_step,
        "ssm_b_re": d_b_re, "ssm_b_im": d_b_im, "ssm_c_re": d_c_re, "ssm_c_im": d_c_im, "ssm_d": d_ssm_d,
        "ssm_glu_b": d_glu_b, "sgu_ln_g": d_ln_g, "sgu_ln_b": d_ln_b, "sgu_w": d_sgu_w, "sgu_b": d_b_st.T,
        "out_norm_ssm_g": d_g_ossm, "out_norm_sgu_g": d_g_osgu, "norm_ffn_g": d_g_ffn, "norm_ple_g": d_g_ple,
        "b_ple_gate": d_b_g, "final_norm_g": d_g_fin,
    })

    exch.grad("w_in", _hosted(exch, _mm_tn, "d_proj_in", h1, dz16, shards=N_CHIPS, tn=768))
    dh1 = _hosted(exch, _mm_nt, "d_h1", dz16, exch.weight("w_in"), sharded=True, out_dtype=BF16, tm=1024)

    def norm_in_bwd(x_t, dres_t, dh_t, g_v):
        _, vjp = jax.vjp(_rms, x_t, g_v)
        dx, dg = vjp(dh_t.astype(F32))
        return dres_t + dx, dg

    grad_x, d_g_mix = _hosted(exch, _rowwise, "d_norm_mix", norm_in_bwd, [x, dx1, dh1], [g_mix], [tok(d_model)], [acc(d_model)],
                              tr=512)
    exch.small_grads({"norm_mix_g": d_g_mix})
    return loss, grad_x


def _place():
    x, y, c = lax.axis_index("x"), lax.axis_index("y"), lax.axis_index("c")
    chips = [(1 - x, y), (x, 1 - y), (1 - x, 1 - y)]
    return x, y, c, chips


def _cast_into_slot(name, w2d, shard, tr=256):
    rows, cols = w2d.shape
    rh = rows // 2
    tr = _pick(rh, tr, 16)
    per = rh // tr

    def body(s_ref, a_ref, o_ref):
        o_ref[...] = a_ref[...].astype(BF16)

    grid_spec = pltpu.PrefetchScalarGridSpec(
        num_scalar_prefetch=1, grid=(2, per),
        in_specs=[pl.BlockSpec((tr, cols), lambda h, i, s_ref: (h * per + i, 0))],
        out_specs=pl.BlockSpec((None, None, tr, cols), lambda h, i, s_ref: (s_ref[0], h, i, 0)))
    return pl.pallas_call(body, name=name, grid_spec=grid_spec, out_shape=_sds((N_CHIPS, 2, rh, cols), BF16),
                          compiler_params=_params(("arbitrary", "arbitrary")))(shard.reshape(1).astype(jnp.int32), w2d)


def _exchange_alone(name, side):
    n_in, n_out = len(side.ins), len(side.out_shapes)

    def body(*refs):
        ins, outs, sems = refs[:n_in], refs[n_in:n_in + n_out], refs[n_in + n_out:]
        side.first(ins, outs, *sems)
        if side.mid is not None:
            side.mid(ins, outs, *sems)
        side.last(ins, outs, *sems)

    return pl.pallas_call(
        body, name=name, in_specs=[ANY] * n_in, out_specs=[ANY] * n_out, out_shape=side.out_shapes,
        input_output_aliases=side.aliases,
        scratch_shapes=[pltpu.SemaphoreType.DMA((side.n_sems,)), pltpu.SemaphoreType.DMA((side.n_sems,))],
    )(*side.ins)


def _gather_side(slots, parts=None, mid_late=False):
    n = len(slots)
    parts = parts or [(0, GATHER_PARTS)] * n

    def copies(kind, outs, send_sems, recv_sems):
        x, y, c, chips = _place()

        def remote(k, w, shard, half, to):
            unit = outs[w].shape[2] // GATHER_PARTS
            lo, hi = parts[w]
            ref = outs[w].at[shard, half, pl.ds(lo * unit, (hi - lo) * unit), :]
            return pltpu.make_async_remote_copy(src_ref=ref, dst_ref=ref, send_sem=send_sems.at[k], recv_sem=recv_sems.at[k],
                                                device_id=to, device_id_type=MESH)

        pairs = [(w, j, 2 * cx + cy, (cx, cy)) for w in range(n) for j, (cx, cy) in enumerate(chips)]
        if kind == "sends":
            return [remote(3 * w + j, w, 2 * x + y, c, (*chip, c)) for w, j, _, chip in pairs]
        if kind == "arrivals":
            return [remote(3 * w + j, w, s, c, (x, y, c)) for w, j, s, _ in pairs]
        if kind == "passed":
            return [remote(3 * n + 3 * w + j, w, s, c, (x, y, 1 - c)) for w, j, s, _ in pairs]
        return [remote(3 * n + 3 * w + j, w, s, 1 - c, (x, y, c)) for w, j, s, _ in pairs]

    def first(ins, outs, *sems):
        for cp in copies("sends", outs, *sems):
            cp.start()

    def mid(ins, outs, *sems):
        for arrived, onward in zip(copies("arrivals", outs, *sems), copies("passed", outs, *sems)):
            arrived.wait_recv()
            onward.start()

    def last(ins, outs, *sems):
        for cp in copies("from_sibling", outs, *sems):
            cp.wait_recv()
        for cp in copies("sends", outs, *sems) + copies("passed", outs, *sems):
            cp.wait_send()

    return _Side(slots, [_sds(s.shape, s.dtype) for s in slots], 6 * n, first, last, mid=mid, aliases={w: w for w in range(n)},
                 mid_late=mid_late)


def _swap_side(grads):
    n = len(grads)

    def copies(ins, outs, send_sems, recv_sems):
        x, y, c, _ = _place()
        return [pltpu.make_async_remote_copy(src_ref=ins[w].at[:, 1 - c], dst_ref=outs[w], send_sem=send_sems.at[w],
                                             recv_sem=recv_sems.at[w], device_id=(x, y, 1 - c), device_id_type=MESH)
                for w in range(n)]

    def first(*refs):
        for cp in copies(*refs):
            cp.start()

    def last(*refs):
        for cp in copies(*refs):
            cp.wait()

    return _Side(grads, [_sds((g.shape[0], *g.shape[2:]), g.dtype) for g in grads], n, first, last)


def _scatter_side(halves):
    n = len(halves)

    def copies(ins, outs, send_sems, recv_sems):
        x, y, c, chips = _place()
        return [pltpu.make_async_remote_copy(
            src_ref=ins[w].at[2 * cx + cy], dst_ref=outs[w].at[j], send_sem=send_sems.at[3 * w + j],
            recv_sem=recv_sems.at[3 * w + j], device_id=(cx, cy, c), device_id_type=MESH)
            for w in range(n) for j, (cx, cy) in enumerate(chips)]

    def first(*refs):
        for cp in copies(*refs):
            cp.start()

    def last(*refs):
        for cp in copies(*refs):
            cp.wait()

    return _Side(halves, [_sds((3, *h.shape[1:]), h.dtype) for h in halves], 3 * n, first, last)


def _join_side(slots):
    n = len(slots)

    def copy(outs, send_sems, recv_sems, w, half, to):
        return pltpu.make_async_remote_copy(src_ref=outs[w].at[half], dst_ref=outs[w].at[half], send_sem=send_sems.at[w],
                                            recv_sem=recv_sems.at[w], device_id=to, device_id_type=MESH)

    def first(ins, outs, *sems):
        x, y, c, _ = _place()
        for w in range(n):
            copy(outs, *sems, w, c, (x, y, 1 - c)).start()

    def last(ins, outs, *sems):
        x, y, c, _ = _place()
        for w in range(n):
            copy(outs, *sems, w, 1 - c, (x, y, c)).wait_recv()
        for w in range(n):
            copy(outs, *sems, w, c, (x, y, 1 - c)).wait_send()

    return _Side(slots, [_sds(s.shape, s.dtype) for s in slots], n, first, last, aliases={w: w for w in range(n)})


def _allreduce_small(block, tr=256):
    rows, lanes = block.shape
    tr = _pick(rows, tr, SUBLANES)

    def body(x_ref, o_ref, buf, send_sems, recv_sems):
        x, y, c, chips = _place()
        me, sibling = (x, y, c), (x, y, 1 - c)

        def slot(px, py, pc):
            return buf.at[4 * px + 2 * py + pc]

        def copy(k, block_of, to):
            return pltpu.make_async_remote_copy(src_ref=slot(*block_of), dst_ref=slot(*block_of), send_sem=send_sems.at[k],
                                                recv_sem=recv_sems.at[k], device_id=to, device_id_type=MESH)

        slot(*me)[...] = x_ref[...]
        first = [copy(0, me, sibling)] + [copy(1 + j, me, (*chip, c)) for j, chip in enumerate(chips)]
        for cp in first:
            cp.start()
        passed = [copy(4 + j, (*chip, c), sibling) for j, chip in enumerate(chips)]
        for j, chip in enumerate(chips):
            copy(1 + j, (*chip, c), me).wait_recv()
            passed[j].start()
        copy(0, sibling, me).wait_recv()
        for j, chip in enumerate(chips):
            copy(4 + j, (*chip, 1 - c), me).wait_recv()
        for cp in first + passed:
            cp.wait_send()
        for r0 in range(0, rows, tr):
            acc = buf[0, r0:r0 + tr, :]
            for k in range(1, N_DEV):
                acc = acc + buf[k, r0:r0 + tr, :]
            o_ref[r0:r0 + tr, :] = acc

    vm = pl.BlockSpec(memory_space=pltpu.VMEM)
    return pl.pallas_call(
        body, name="allreduce_small", in_specs=[vm], out_specs=vm, out_shape=_sds((rows, lanes), block.dtype),
        scratch_shapes=[pltpu.VMEM((N_DEV, rows, lanes), block.dtype), pltpu.SemaphoreType.DMA((7,)), pltpu.SemaphoreType.DMA((7,))],
        compiler_params=pltpu.CompilerParams(vmem_limit_bytes=VMEM_LIMIT),
    )(block)


def _small_gather_side(block):
    def copy(kind, j, ins, outs, send_sems, recv_sems):
        x, y, c, chips = _place()
        chip = chips[j] if j is not None else None
        slot = lambda px, py, pc: outs[0].at[4 * px + 2 * py + pc]

        def remote(k, src, dst, to):
            return pltpu.make_async_remote_copy(src_ref=src, dst_ref=dst, send_sem=send_sems.at[k], recv_sem=recv_sems.at[k],
                                                device_id=to, device_id_type=MESH)

        if kind == "to_sibling":
            return remote(0, ins[0], slot(x, y, c), (x, y, 1 - c))
        if kind == "from_sibling":
            return remote(0, ins[0], slot(x, y, 1 - c), (x, y, c))
        if kind == "to_chip":
            return remote(1 + j, ins[0], slot(x, y, c), (*chip, c))
        if kind == "from_chip":
            return remote(1 + j, ins[0], slot(*chip, c), (x, y, c))
        if kind == "pass_on":
            return remote(4 + j, slot(*chip, c), slot(*chip, c), (x, y, 1 - c))
        return remote(4 + j, slot(*chip, 1 - c), slot(*chip, 1 - c), (x, y, c))

    def first(*refs):
        copy("to_sibling", None, *refs).start()
        for j in range(3):
            copy("to_chip", j, *refs).start()

    def mid(*refs):
        for j in range(3):
            copy("from_chip", j, *refs).wait_recv()
            copy("pass_on", j, *refs).start()

    def last(*refs):
        copy("from_sibling", None, *refs).wait_recv()
        for j in range(3):
            copy("passed_on", j, *refs).wait_recv()
        copy("to_sibling", None, *refs).wait_send()
        for j in range(3):
            copy("to_chip", j, *refs).wait_send()
            copy("pass_on", j, *refs).wait_send()

    return _Side([block], [_sds((N_DEV, *block.shape), block.dtype)], 7, first, last, mid=mid, mid_late=True)


def _sum_slots(name, own, gathered, me, tr=512):
    n, rows, cols = gathered.shape
    tr = _pick(rows, tr, SUBLANES)
    if tr < 64:
        tr = rows

    def body(me_ref, own_ref, g_ref, o_ref):
        mine = own_ref[...]
        acc = jnp.where(me_ref[0] == 0, mine, g_ref[0])
        for k in range(1, n):
            acc = acc + jnp.where(me_ref[0] == k, mine, g_ref[k])
        o_ref[...] = acc

    grid_spec = pltpu.PrefetchScalarGridSpec(
        num_scalar_prefetch=1, grid=(rows // tr,),
        in_specs=[pl.BlockSpec((tr, cols), lambda i, me_ref: (i, 0)), pl.BlockSpec((n, tr, cols), lambda i, me_ref: (0, i, 0))],
        out_specs=pl.BlockSpec((tr, cols), lambda i, me_ref: (i, 0)))
    return pl.pallas_call(body, name=name, grid_spec=grid_spec, out_shape=_sds((rows, cols), own.dtype),
                          compiler_params=_params(("arbitrary",)))(me.reshape(1).astype(jnp.int32), own, gathered)


def _sum_received(name, full, c, shard, swapped, received, tr=256):
    n, rows, cols = received.shape
    tr = _pick(rows, tr, 16)

    def body(i_ref, a_ref, b_ref, s_ref, o_ref):
        acc = a_ref[...] + b_ref[...]
        for k in range(n):
            acc = acc + s_ref[k].astype(F32)
        o_ref[...] = acc

    grid_spec = pltpu.PrefetchScalarGridSpec(
        num_scalar_prefetch=1, grid=(rows // tr,),
        in_specs=[pl.BlockSpec((None, None, tr, cols), lambda i, i_ref: (i_ref[1], i_ref[0], i, 0)),
                  pl.BlockSpec((None, tr, cols), lambda i, i_ref: (i_ref[1], i, 0)),
                  pl.BlockSpec((n, tr, cols), lambda i, i_ref: (0, i, 0))],
        out_specs=pl.BlockSpec((None, tr, cols), lambda i, i_ref: (i_ref[0], i, 0)))
    return pl.pallas_call(body, name=name, grid_spec=grid_spec, out_shape=_sds((2, rows, cols), F32),
                          compiler_params=_params(("arbitrary",)))(jnp.stack([c, shard]).astype(jnp.int32), full, swapped, received)


def _add_halves(name, full, c, shard, received, tr=256):
    s, _, rh, cols = full.shape
    tr = _pick(rh, tr, 16)

    def body(i_ref, a_ref, b_ref, o_ref):
        o_ref[...] = (a_ref[...] + b_ref[...]).astype(BF16)

    other = lambda q, i_ref: (i_ref[1] + 1 + q) % s
    grid_spec = pltpu.PrefetchScalarGridSpec(
        num_scalar_prefetch=1, grid=(s - 1, rh // tr),
        in_specs=[pl.BlockSpec((None, None, tr, cols), lambda q, i, i_ref: (other(q, i_ref), i_ref[0], i, 0)),
                  pl.BlockSpec((None, tr, cols), lambda q, i, i_ref: (other(q, i_ref), i, 0))],
        out_specs=pl.BlockSpec((None, tr, cols), lambda q, i, i_ref: (other(q, i_ref), i, 0)))
    return pl.pallas_call(body, name=name, grid_spec=grid_spec, out_shape=_sds((s, rh, cols), BF16),
                          compiler_params=_params(("arbitrary", "arbitrary")))(jnp.stack([c, shard]).astype(jnp.int32), full, received)


LARGE = ("w_in", "ssm_glu_w", "w_out", "w_ffn_in", "w_ffn_out", "w_ple_gate", "w_ple_proj")
COLUMN_SHARDED = ("w_in", "w_ffn_in", "w_ple_proj")
SMALL = ("norm_mix_g", "ssm_lambda_re", "ssm_lambda_im", "ssm_log_step", "ssm_b_re", "ssm_b_im", "ssm_c_re", "ssm_c_im",
         "ssm_d", "ssm_glu_b", "sgu_ln_g", "sgu_ln_b", "sgu_w", "sgu_b", "out_norm_ssm_g", "out_norm_sgu_g", "norm_ffn_g",
         "norm_ple_g", "b_ple_gate", "final_norm_g")
WEIGHTS = ("norm_mix_g", "w_in", "ssm_lambda_re", "ssm_lambda_im", "ssm_log_step", "ssm_b_re", "ssm_b_im", "ssm_c_re",
           "ssm_c_im", "ssm_d", "ssm_glu_w", "ssm_glu_b", "sgu_ln_g", "sgu_ln_b", "sgu_w", "sgu_b", "out_norm_ssm_g",
           "out_norm_sgu_g", "w_out", "norm_ffn_g", "w_ffn_in", "w_ffn_out", "norm_ple_g", "w_ple_gate", "b_ple_gate",
           "w_ple_proj", "final_norm_g")
PACK_ROWS = SUBLANES * LANES


def _pack(arrays):
    parts = []
    for a in arrays:
        flat = a.reshape(-1).astype(F32)
        pad = -flat.shape[0] % PACK_ROWS
        parts.append(jnp.pad(flat, (0, pad)) if pad else flat)
    return jnp.concatenate(parts).reshape(-1, LANES)


def _unpack(packed, like):
    flat = packed.reshape(-1)
    out, at = [], 0
    for a in like:
        size = a.size
        out.append(flat[at:at + size].reshape(a.shape))
        at += size + (-size % PACK_ROWS)
    return out


class _NoExchange:
    def __init__(self, weights):
        self.weights, self.grads, self.small = weights, {}, {}

    def weight(self, name):
        return self.weights[name]

    def grad(self, name, g):
        self.grads[name] = g

    def small_grads(self, grads):
        self.small.update(grads)

    def side(self, host):
        return None


class _MeshExchange:
    GATHER = {"norm_mix": (("w_in", 0, 16),),
              "proj_in": (("ssm_glu_w", 0, 16), ("w_out", 0, 16), ("w_ffn_in", 0, 1)),
              "ssm_fwd": (("w_ffn_in", 1, 13),),
              "proj_out": (("w_ffn_in", 13, 16),),
              "ffn_in": (("w_ffn_out", 0, 16), ("w_ple_gate", 0, 16), ("w_ple_proj", 0, 16))}
    GATHER_LONG = ("norm_mix", "proj_in", "ssm_fwd", "proj_out")
    SWAP = {"d_act": ("w_ple_proj", "w_ple_gate", "w_ffn_out"), "d_h2": ("w_ffn_in",), "d_ya0": ("w_out", "ssm_glu_w")}
    SWAP_ALONE = ("w_in",)
    SCATTER = {"d_ffn_in": ("w_ple_proj", "w_ple_gate", "w_ffn_out"), "ssm_bwd": ("w_ffn_in",),
               "d_sgu": ("w_out", "ssm_glu_w"), "d_h1": ("w_in",)}
    SMALL_GATHER = "d_proj_in"

    def __init__(self, shards, small_like, c, shard, me):
        self.c, self.shard, self.me, self.small_like = c, shard, me, small_like
        self.slots = {k: _cast_into_slot("cast_" + k, shards[k], shard) for k in LARGE}
        self.full, self.received, self.halves, self.quarters, self.small = {}, {}, {}, {}, {}

    def weight(self, name):
        g = self.slots[name]
        _, _, rh, cols = g.shape
        return g.reshape(N_CHIPS, 2 * rh, cols) if name in COLUMN_SHARDED else g.reshape(N_CHIPS * 2 * rh, cols)

    def grad(self, name, g):
        if name not in COLUMN_SHARDED:
            g = g.reshape(N_CHIPS, g.shape[0] // N_CHIPS, g.shape[1])
        self.full[name] = g.reshape(N_CHIPS, 2, g.shape[1] // 2, g.shape[2])
        if name in self.SWAP_ALONE:
            self._swapped((name,), _exchange_alone("grad_swap_" + name, _swap_side([self.full[name]])))

    def _swapped(self, names, received):
        for k, r in zip(names, received):
            self.received[k] = r
            self.halves[k] = _add_halves("grad_add_halves_" + k, self.full[k], self.c, self.shard, r)

    def small_grads(self, grads):
        self.small.update(grads)

    def _packed(self, names):
        return _pack([self.small[k].reshape(self.small_like[k].shape) for k in names])

    def side(self, host):
        if host in self.GATHER:
            return _gather_side([self.slots[k] for k, _, _ in self.GATHER[host]], [(lo, hi) for _, lo, hi in self.GATHER[host]],
                                mid_late=host in self.GATHER_LONG)
        if host in self.SWAP:
            return _swap_side([self.full[k] for k in self.SWAP[host]])
        if host in self.SCATTER:
            return _scatter_side([self.halves[k] for k in self.SCATTER[host]])
        if host == self.SMALL_GATHER:
            self.packed_early = self._packed(SMALL[1:] + ("loss",))
            return _small_gather_side(self.packed_early)
        return None

    def done(self, host, moved):
        if host in self.GATHER:
            self.slots.update(zip([k for k, _, _ in self.GATHER[host]], moved))
        elif host in self.SWAP:
            self._swapped(self.SWAP[host], moved)
        elif host in self.SCATTER:
            self.quarters.update(zip(self.SCATTER[host], moved))
        else:
            (self.gathered_early,) = moved

    def small_reduced(self):
        early = _sum_slots("small_sum", self.packed_early, self.gathered_early, self.me)
        late = _allreduce_small(self._packed(SMALL[:1]))
        loss_at = early.shape[0] - PACK_ROWS // LANES
        return jnp.concatenate([late, early[:loss_at]], axis=0), early[loss_at, 0]

    def summed(self):
        return [_sum_received("grad_sum_" + k, self.full[k], self.c, self.shard, self.received[k], self.quarters[k]) for k in LARGE]


def kernel(x, p, norm_mix_g, w_in, ssm_lambda_re, ssm_lambda_im, ssm_log_step, ssm_b_re, ssm_b_im, ssm_c_re, ssm_c_im, ssm_d, ssm_glu_w, ssm_glu_b, sgu_ln_g, sgu_ln_b, sgu_w, sgu_b, out_norm_ssm_g, out_norm_sgu_g, w_out, norm_ffn_g, w_ffn_in, w_ffn_out, norm_ple_g, w_ple_gate, b_ple_gate, w_ple_proj, final_norm_g, loss_target, m_norm_mix_g, m_w_in, m_ssm_lambda_re, m_ssm_lambda_im, m_ssm_log_step, m_ssm_b_re, m_ssm_b_im, m_ssm_c_re, m_ssm_c_im, m_ssm_d, m_ssm_glu_w, m_ssm_glu_b, m_sgu_ln_g, m_sgu_ln_b, m_sgu_w, m_sgu_b, m_out_norm_ssm_g, m_out_norm_sgu_g, m_w_out, m_norm_ffn_g, m_w_ffn_in, m_w_ffn_out, m_norm_ple_g, m_w_ple_gate, m_b_ple_gate, m_w_ple_proj, m_final_norm_g, v_norm_mix_g, v_w_in, v_ssm_lambda_re, v_ssm_lambda_im, v_ssm_log_step, v_ssm_b_re, v_ssm_b_im, v_ssm_c_re, v_ssm_c_im, v_ssm_d, v_ssm_glu_w, v_ssm_glu_b, v_sgu_ln_g, v_sgu_ln_b, v_sgu_w, v_sgu_b, v_out_norm_ssm_g, v_out_norm_sgu_g, v_w_out, v_norm_ffn_g, v_w_ffn_in, v_w_ffn_out, v_norm_ple_g, v_w_ple_gate, v_b_ple_gate, v_w_ple_proj, v_final_norm_g):
    given = dict(locals())
    w = {k: given[k] for k in WEIGHTS}
    m = {k: given["m_" + k] for k in WEIGHTS}
    v = {k: given["v_" + k] for k in WEIGHTS}
    c = lax.axis_index("c")
    shard = 2 * lax.axis_index("x") + lax.axis_index("y")

    small_like = {k: w[k] for k in SMALL}
    small_like["loss"] = _sds((1, LANES), F32)
    exch = _MeshExchange({k: w[k].reshape(w[k].shape[1:]) for k in LARGE}, small_like, c, shard, 2 * shard + c)
    unlayer = lambda a: a if a.ndim == 1 else a[0]
    sp = {k: unlayer(w[k]) for k in SMALL}
    n_tok, d_model = x.shape[1:]
    _, grad_x = _local_grads(x.reshape(n_tok, d_model), p.reshape(n_tok, p.shape[-1]),
                             loss_target.reshape(n_tok, d_model), sp, exch)

    grad_w, delta_w, new_m, new_v = {}, {}, {}, {}
    halves = exch.summed()
    packed_g, loss = exch.small_reduced()
    like = _sds(packed_g.shape, F32)
    (d_s, m_s, v_s), joined = _rowwise(
        "adamw_small", _adamw, [_pack([w[k] for k in SMALL]), packed_g, _pack([m[k] for k in SMALL]), _pack([v[k] for k in SMALL])],
        [], [like, like, like], side=_join_side(halves))
    shapes = [w[k] for k in SMALL]
    for k, g_k, d_k, m_k, v_k in zip(SMALL, _unpack(packed_g, shapes), _unpack(d_s, shapes), _unpack(m_s, shapes), _unpack(v_s, shapes)):
        grad_w[k], delta_w[k], new_m[k], new_v[k] = g_k, d_k, m_k, v_k

    reduced = {k: j.reshape(2 * j.shape[1], j.shape[2]) for k, j in zip(LARGE, joined)}
    for k in LARGE:
        shape = w[k].shape
        two_d = lambda a: a.reshape(shape[1:])
        like = _sds(shape[1:], F32)
        update = lambda w_t, g_t, m_t, v_t: (g_t, *_adamw(w_t, g_t, m_t, v_t))
        outs = _rowwise("adamw_" + k, update, [two_d(w[k]), reduced[k], two_d(m[k]), two_d(v[k])], [], [like, like, like, like])
        grad_w[k], delta_w[k], new_m[k], new_v[k] = (a.reshape(shape) for a in outs)

    return (loss, grad_x.reshape(x.shape), *[grad_w[k] for k in WEIGHTS], *[delta_w[k] for k in WEIGHTS],
            *[new_m[k] for k in WEIGHTS], *[new_v[k] for k in WEIGHTS])
```
